```python
import math
import jax, jax.numpy as jnp
from jax import lax
import numpy as np

D_MODEL = 1024
BATCH = 32
SEQ = 2048
DEPTH = 1

CHUNK = 64
Q_BLOCK = 128
PLE_DIM = 256
GDN_HEADS = 4
GDN_DK = 128
GDN_DV = 128
GDN_CONV = 4
MLA_HEADS = 4
MLA_NOPE = 128
MLA_ROPE = 64
MLA_V = 128
MLA_Q_LORA = 384
MLA_KV_LORA = 256
ROPE_THETA = 10000.0
D_FF = 2816
FFN_CONV = 3
ALPHA = (2.0 * DEPTH) ** 0.25
BETA = (8.0 * DEPTH) ** -0.25
NORM_EPS = 1e-6
GDN_QK = GDN_HEADS * GDN_DK
GDN_VW = GDN_HEADS * GDN_DV
D_IN = 2 * GDN_QK + 2 * GDN_VW + 2 * GDN_HEADS + MLA_Q_LORA + MLA_KV_LORA + MLA_ROPE
D_MIX = GDN_VW + MLA_HEADS * MLA_V

kernel_name = "hybrid_gdn_mla_convffn_deepnorm"


def _rmsnorm(x, g):
    xf = x.astype(jnp.float32)
    y = xf * lax.rsqrt(jnp.mean(xf * xf, axis=-1, keepdims=True) + NORM_EPS)
    return (y * g.astype(jnp.float32)).astype(x.dtype)


def _layernorm(x, g, b):
    xf = x.astype(jnp.float32)
    mu = jnp.mean(xf, axis=-1, keepdims=True)
    xc = xf - mu
    var = jnp.mean(xc * xc, axis=-1, keepdims=True)
    y = xc * lax.rsqrt(var + NORM_EPS) * g.astype(jnp.float32) + b.astype(jnp.float32)
    return y.astype(x.dtype)


def _l2norm(x):
    return x * lax.rsqrt(jnp.sum(x * x, axis=-1, keepdims=True) + NORM_EPS)


def _causal_dwconv(x, w):
    k = w.shape[0]
    return lax.conv_general_dilated(
        x, w[:, None, :].astype(x.dtype), window_strides=(1,), padding=[(k - 1, 0)],
        dimension_numbers=("NWC", "WIO", "NWC"), feature_group_count=x.shape[-1])


def _rope_tables(seq):
    inv = ROPE_THETA ** (-jnp.arange(0, MLA_ROPE, 2, dtype=jnp.float32) / MLA_ROPE)
    ang = jnp.arange(seq, dtype=jnp.float32)[:, None] * inv[None, :]
    return jnp.cos(ang), jnp.sin(ang)


def _apply_rope(x, cos, sin):
    xf = x.astype(jnp.float32)
    x1, x2 = jnp.split(xf, 2, axis=-1)
    return jnp.concatenate([x1 * cos - x2 * sin, x1 * sin + x2 * cos], axis=-1).astype(x.dtype)


def _chunk_gated_delta_rule(q, k, v, g, beta):
    bsz, seq, nh, dk = q.shape
    dv = v.shape[-1]
    n = seq // CHUNK

    def blocks(t):
        t = t.reshape((bsz, n, CHUNK, nh) + t.shape[3:])
        return jnp.moveaxis(jnp.swapaxes(t, 2, 3), 1, 0)

    qc, kc, vc = blocks(q), blocks(k), blocks(v)
    bc = blocks(beta)
    gc = jnp.cumsum(blocks(g), axis=-1)
    idx = jnp.arange(CHUNK)
    incl = idx[:, None] >= idx[None, :]
    strict = idx[:, None] > idx[None, :]
    decay = jnp.exp(jnp.where(incl, gc[..., :, None] - gc[..., None, :], -jnp.inf))
    kb = kc * bc[..., None]
    lower = jnp.where(strict, jnp.einsum("nbhcd,nbhsd->nbhcs", kb, kc) * decay, 0.0)
    tri = lower + jnp.eye(CHUNK, dtype=jnp.float32)
    w = lax.linalg.triangular_solve(tri, kb * jnp.exp(gc)[..., None], left_side=True, lower=True,
                                    unit_diagonal=True)
    u = lax.linalg.triangular_solve(tri, vc * bc[..., None], left_side=True, lower=True,
                                    unit_diagonal=True)
    qk = jnp.einsum("nbhcd,nbhsd->nbhcs", qc, kc) * decay
    qg = qc * jnp.exp(gc)[..., None]
    kd = kc * jnp.exp(gc[..., -1:] - gc)[..., None]
    glast = jnp.exp(gc[..., -1])

    def step(state, xs):
        w_n, u_n, qk_n, qg_n, kd_n, gl_n = xs
        v_new = u_n - jnp.einsum("bhck,bhkv->bhcv", w_n, state)
        o_n = jnp.einsum("bhck,bhkv->bhcv", qg_n, state) + jnp.einsum("bhcs,bhsv->bhcv", qk_n, v_new)
        state = state * gl_n[..., None, None] + jnp.einsum("bhck,bhcv->bhkv", kd_n, v_new)
        return state, o_n

    s0 = jnp.zeros((bsz, nh, dk, dv), jnp.float32)
    _, o = lax.scan(step, s0, (w, u, qk, qg, kd, glast))
    return jnp.swapaxes(jnp.moveaxis(o, 0, 1), 2, 3).reshape(bsz, seq, nh, dv)


def _gated_deltanet(qkv, z, a, b, conv_w, a_log, dt_bias, norm_g):
    bsz, seq, _ = qkv.shape
    h = jax.nn.silu(_causal_dwconv(qkv, conv_w)).astype(jnp.float32)
    q, k, v = jnp.split(h, [GDN_QK, 2 * GDN_QK], axis=-1)
    q = _l2norm(q.reshape(bsz, seq, GDN_HEADS, GDN_DK)) * (GDN_DK ** -0.5)
    k = _l2norm(k.reshape(bsz, seq, GDN_HEADS, GDN_DK))
    v = v.reshape(bsz, seq, GDN_HEADS, GDN_DV)
    beta = jax.nn.sigmoid(b.astype(jnp.float32))
    g = -jnp.exp(a_log.astype(jnp.float32)) * jax.nn.softplus(
        a.astype(jnp.float32) + dt_bias.astype(jnp.float32))
    o = _chunk_gated_delta_rule(q, k, v, g, beta)
    o = _rmsnorm(o, norm_g) * jax.nn.silu(z.astype(jnp.float32).reshape(bsz, seq, GDN_HEADS, GDN_DV))
    return o.reshape(bsz, seq, GDN_VW).astype(z.dtype)


def _mla(cq, ckv, k_rope, q_norm_g, w_q_up, kv_norm_g, w_kv_up):
    bsz, seq, _ = cq.shape
    q = (_rmsnorm(cq, q_norm_g) @ w_q_up).reshape(bsz, seq, MLA_HEADS, MLA_NOPE + MLA_ROPE)
    q_nope, q_rope = q[..., :MLA_NOPE], q[..., MLA_NOPE:]
    kv = (_rmsnorm(ckv, kv_norm_g) @ w_kv_up).reshape(bsz, seq, MLA_HEADS, MLA_NOPE + MLA_V)
    k_nope, v = kv[..., :MLA_NOPE], kv[..., MLA_NOPE:]
    cos, sin = _rope_tables(seq)
    q_rope = _apply_rope(q_rope, cos[:, None, :], sin[:, None, :])
    k_rope = _apply_rope(k_rope, cos, sin)
    nqb = seq // Q_BLOCK

    def qblocks(t):
        return jnp.moveaxis(t.reshape(bsz, nqb, Q_BLOCK, MLA_HEADS, t.shape[-1]), 1, 0)

    key_chunk = jnp.arange(seq) // CHUNK
    scale = (MLA_NOPE + MLA_ROPE) ** -0.5

    def attend(xs):
        qn, qr, blk = xs
        s = jnp.einsum("bqhd,bkhd->bhqk", qn, k_nope) + jnp.einsum("bqhd,bkd->bhqk", qr, k_rope)
        q_chunk = (blk * Q_BLOCK + jnp.arange(Q_BLOCK)) // CHUNK
        allowed = key_chunk[None, :] <= q_chunk[:, None]
        s = jnp.where(allowed, s.astype(jnp.float32) * scale, -jnp.inf)
        pr = jax.nn.softmax(s, axis=-1).astype(v.dtype)
        return jnp.einsum("bhqk,bkhd->bqhd", pr, v)

    o = lax.map(attend, (qblocks(q_nope), qblocks(q_rope), jnp.arange(nqb)))
    return jnp.moveaxis(o, 0, 1).reshape(bsz, seq, MLA_HEADS * MLA_V)


def _conv_ffn(h, w_up, conv_w, conv_b, w_down):
    u = _causal_dwconv(h @ w_up, conv_w) + conv_b
    gate, up = jnp.split(u, 2, axis=-1)
    return (jax.nn.silu(gate) * up) @ w_down


def _fwd_setup_inputs(seed: int = 0) -> dict:
    key = jax.random.key(seed)
    ks = jax.random.split(key, 24)
    f32 = jnp.float32
    L = DEPTH

    def nrm(k, shape, scale):
        return jax.random.normal(k, shape, f32) * scale

    dt = jnp.exp(jax.random.uniform(ks[5], (L, GDN_HEADS), f32, math.log(1e-3), math.log(1e-1)))
    return {
        "x": nrm(ks[0], (BATCH, SEQ, D_MODEL), 1.0),
        "p": nrm(ks[1], (L, BATCH, SEQ, PLE_DIM), 1.0),
        "w_in": nrm(ks[2], (L, D_MODEL, D_IN), D_MODEL ** -0.5),
        "gdn_conv_w": nrm(ks[3], (L, GDN_CONV, 2 * GDN_QK + GDN_VW), GDN_CONV ** -0.5),
        "gdn_a_log": jnp.log(jax.random.uniform(ks[4], (L, GDN_HEADS), f32, 1.0, 16.0)),
        "gdn_dt_bias": dt + jnp.log(-jnp.expm1(-dt)),
        "gdn_norm_g": 1.0 + nrm(ks[6], (L, GDN_DV), 0.02),
        "mla_q_norm_g": 1.0 + nrm(ks[7], (L, MLA_Q_LORA), 0.02),
        "mla_w_q_up": nrm(ks[8], (L, MLA_Q_LORA, MLA_HEADS * (MLA_NOPE + MLA_ROPE)), MLA_Q_LORA ** -0.5),
        "mla_kv_norm_g": 1.0 + nrm(ks[9], (L, MLA_KV_LORA), 0.02),
        "mla_w_kv_up": nrm(ks[10], (L, MLA_KV_LORA, MLA_HEADS * (MLA_NOPE + MLA_V)), MLA_KV_LORA ** -0.5),
        "w_out": nrm(ks[11], (L, D_MIX, D_MODEL), BETA * D_MIX ** -0.5),
        "ln1_g": 1.0 + nrm(ks[12], (L, D_MODEL), 0.02),
        "ln1_b": nrm(ks[13], (L, D_MODEL), 0.02),
        "ffn_w_up": nrm(ks[14], (L, D_MODEL, 2 * D_FF), D_MODEL ** -0.5),
        "ffn_conv_w": nrm(ks[15], (L, FFN_CONV, 2 * D_FF), FFN_CONV ** -0.5),
        "ffn_conv_b": nrm(ks[16], (L, 2 * D_FF), 0.01),
        "ffn_w_down": nrm(ks[17], (L, D_FF, D_MODEL), BETA * D_FF ** -0.5),
        "ple_w_gate": nrm(ks[18], (L, D_MODEL, D_MODEL), D_MODEL ** -0.5),
        "ple_b_gate": nrm(ks[19], (L, D_MODEL), 0.01),
        "ple_w_proj": nrm(ks[20], (L, PLE_DIM, D_MODEL), BETA * PLE_DIM ** -0.5),
        "ln2_g": 1.0 + nrm(ks[21], (L, D_MODEL), 0.02),
        "ln2_b": nrm(ks[22], (L, D_MODEL), 0.02),
    }


def _fwd_reference(x, p, w_in, gdn_conv_w, gdn_a_log, gdn_dt_bias, gdn_norm_g, mla_q_norm_g, mla_w_q_up,
              mla_kv_norm_g, mla_w_kv_up, w_out, ln1_g, ln1_b, ffn_w_up, ffn_conv_w, ffn_conv_b,
              ffn_w_down, ple_w_gate, ple_b_gate, ple_w_proj, ln2_g, ln2_b):
    o_qkv = 2 * GDN_QK + GDN_VW
    o_z = o_qkv + GDN_VW
    o_a = o_z + GDN_HEADS
    o_b = o_a + GDN_HEADS
    o_cq = o_b + MLA_Q_LORA
    o_ckv = o_cq + MLA_KV_LORA
    h = x
    for i in range(DEPTH):
        proj = h @ w_in[i]
        qkv, z, a, b, cq, ckv, k_rope = jnp.split(proj, [o_qkv, o_z, o_a, o_b, o_cq, o_ckv], axis=-1)
        out_a = _gated_deltanet(qkv, z, a, b, gdn_conv_w[i], gdn_a_log[i], gdn_dt_bias[i], gdn_norm_g[i])
        out_b = _mla(cq, ckv, k_rope, mla_q_norm_g[i], mla_w_q_up[i], mla_kv_norm_g[i], mla_w_kv_up[i])
        mix = jnp.concatenate([out_a, out_b], axis=-1) @ w_out[i]
        h = _layernorm(ALPHA * h + mix, ln1_g[i], ln1_b[i])
        ffn = _conv_ffn(h, ffn_w_up[i], ffn_conv_w[i], ffn_conv_b[i], ffn_w_down[i])
        ple = jax.nn.sigmoid(h @ ple_w_gate[i] + ple_b_gate[i]) * (p[i] @ ple_w_proj[i])
        h = _layernorm(ALPHA * h + ffn + ple, ln2_g[i], ln2_b[i])
    return h


import jax as _jax
import jax.numpy as _jnp

TWIN_FORMAT = 'train_step'
FWD_PARAMS = ['x', 'p', 'w_in', 'gdn_conv_w', 'gdn_a_log', 'gdn_dt_bias', 'gdn_norm_g', 'mla_q_norm_g', 'mla_w_q_up', 'mla_kv_norm_g', 'mla_w_kv_up', 'w_out', 'ln1_g', 'ln1_b', 'ffn_w_up', 'ffn_conv_w', 'ffn_conv_b', 'ffn_w_down', 'ple_w_gate', 'ple_b_gate', 'ple_w_proj', 'ln2_g', 'ln2_b']
TWIN_WEIGHTS = ['w_in', 'gdn_conv_w', 'gdn_a_log', 'gdn_dt_bias', 'gdn_norm_g', 'mla_q_norm_g', 'mla_w_q_up', 'mla_kv_norm_g', 'mla_w_kv_up', 'w_out', 'ln1_g', 'ln1_b', 'ffn_w_up', 'ffn_conv_w', 'ffn_conv_b', 'ffn_w_down', 'ple_w_gate', 'ple_b_gate', 'ple_w_proj', 'ln2_g', 'ln2_b']
TWIN_DIFF_INPUT = 'x'
TWIN_INPUTS = ['x', 'p', 'w_in', 'gdn_conv_w', 'gdn_a_log', 'gdn_dt_bias', 'gdn_norm_g', 'mla_q_norm_g', 'mla_w_q_up', 'mla_kv_norm_g', 'mla_w_kv_up', 'w_out', 'ln1_g', 'ln1_b', 'ffn_w_up', 'ffn_conv_w', 'ffn_conv_b', 'ffn_w_down', 'ple_w_gate', 'ple_b_gate', 'ple_w_proj', 'ln2_g', 'ln2_b', 'loss_target', 'm_w_in', 'm_gdn_conv_w', 'm_gdn_a_log', 'm_gdn_dt_bias', 'm_gdn_norm_g', 'm_mla_q_norm_g', 'm_mla_w_q_up', 'm_mla_kv_norm_g', 'm_mla_w_kv_up', 'm_w_out', 'm_ln1_g', 'm_ln1_b', 'm_ffn_w_up', 'm_ffn_conv_w', 'm_ffn_conv_b', 'm_ffn_w_down', 'm_ple_w_gate', 'm_ple_b_gate', 'm_ple_w_proj', 'm_ln2_g', 'm_ln2_b', 'v_w_in', 'v_gdn_conv_w', 'v_gdn_a_log', 'v_gdn_dt_bias', 'v_gdn_norm_g', 'v_mla_q_norm_g', 'v_mla_w_q_up', 'v_mla_kv_norm_g', 'v_mla_w_kv_up', 'v_w_out', 'v_ln1_g', 'v_ln1_b', 'v_ffn_w_up', 'v_ffn_conv_w', 'v_ffn_conv_b', 'v_ffn_w_down', 'v_ple_w_gate', 'v_ple_b_gate', 'v_ple_w_proj', 'v_ln2_g', 'v_ln2_b']
TWIN_OUTPUTS = ['loss', 'grad_x', 'grad_w_in', 'grad_gdn_conv_w', 'grad_gdn_a_log', 'grad_gdn_dt_bias', 'grad_gdn_norm_g', 'grad_mla_q_norm_g', 'grad_mla_w_q_up', 'grad_mla_kv_norm_g', 'grad_mla_w_kv_up', 'grad_w_out', 'grad_ln1_g', 'grad_ln1_b', 'grad_ffn_w_up', 'grad_ffn_conv_w', 'grad_ffn_conv_b', 'grad_ffn_w_down', 'grad_ple_w_gate', 'grad_ple_b_gate', 'grad_ple_w_proj', 'grad_ln2_g', 'grad_ln2_b', 'delta_w_in', 'delta_gdn_conv_w', 'delta_gdn_a_log', 'delta_gdn_dt_bias', 'delta_gdn_norm_g', 'delta_mla_q_norm_g', 'delta_mla_w_q_up', 'delta_mla_kv_norm_g', 'delta_mla_w_kv_up', 'delta_w_out', 'delta_ln1_g', 'delta_ln1_b', 'delta_ffn_w_up', 'delta_ffn_conv_w', 'delta_ffn_conv_b', 'delta_ffn_w_down', 'delta_ple_w_gate', 'delta_ple_b_gate', 'delta_ple_w_proj', 'delta_ln2_g', 'delta_ln2_b', 'new_m_w_in', 'new_m_gdn_conv_w', 'new_m_gdn_a_log', 'new_m_gdn_dt_bias', 'new_m_gdn_norm_g', 'new_m_mla_q_norm_g', 'new_m_mla_w_q_up', 'new_m_mla_kv_norm_g', 'new_m_mla_w_kv_up', 'new_m_w_out', 'new_m_ln1_g', 'new_m_ln1_b', 'new_m_ffn_w_up', 'new_m_ffn_conv_w', 'new_m_ffn_conv_b', 'new_m_ffn_w_down', 'new_m_ple_w_gate', 'new_m_ple_b_gate', 'new_m_ple_w_proj', 'new_m_ln2_g', 'new_m_ln2_b', 'new_v_w_in', 'new_v_gdn_conv_w', 'new_v_gdn_a_log', 'new_v_gdn_dt_bias', 'new_v_gdn_norm_g', 'new_v_mla_q_norm_g', 'new_v_mla_w_q_up', 'new_v_mla_kv_norm_g', 'new_v_mla_w_kv_up', 'new_v_w_out', 'new_v_ln1_g', 'new_v_ln1_b', 'new_v_ffn_w_up', 'new_v_ffn_conv_w', 'new_v_ffn_conv_b', 'new_v_ffn_w_down', 'new_v_ple_w_gate', 'new_v_ple_b_gate', 'new_v_ple_w_proj', 'new_v_ln2_g', 'new_v_ln2_b']
TWIN_LEAF_KINDS = {'loss': 'loss', 'grad_x': 'grad_x', 'grad_w_in': 'grad_w', 'grad_gdn_conv_w': 'grad_w', 'grad_gdn_a_log': 'grad_w', 'grad_gdn_dt_bias': 'grad_w', 'grad_gdn_norm_g': 'grad_w', 'grad_mla_q_norm_g': 'grad_w', 'grad_mla_w_q_up': 'grad_w', 'grad_mla_kv_norm_g': 'grad_w', 'grad_mla_w_kv_up': 'grad_w', 'grad_w_out': 'grad_w', 'grad_ln1_g': 'grad_w', 'grad_ln1_b': 'grad_w', 'grad_ffn_w_up': 'grad_w', 'grad_ffn_conv_w': 'grad_w', 'grad_ffn_conv_b': 'grad_w', 'grad_ffn_w_down': 'grad_w', 'grad_ple_w_gate': 'grad_w', 'grad_ple_b_gate': 'grad_w', 'grad_ple_w_proj': 'grad_w', 'grad_ln2_g': 'grad_w', 'grad_ln2_b': 'grad_w', 'delta_w_in': 'delta_w', 'delta_gdn_conv_w': 'delta_w', 'delta_gdn_a_log': 'delta_w', 'delta_gdn_dt_bias': 'delta_w', 'delta_gdn_norm_g': 'delta_w', 'delta_mla_q_norm_g': 'delta_w', 'delta_mla_w_q_up': 'delta_w', 'delta_mla_kv_norm_g': 'delta_w', 'delta_mla_w_kv_up': 'delta_w', 'delta_w_out': 'delta_w', 'delta_ln1_g': 'delta_w', 'delta_ln1_b': 'delta_w', 'delta_ffn_w_up': 'delta_w', 'delta_ffn_conv_w': 'delta_w', 'delta_ffn_conv_b': 'delta_w', 'delta_ffn_w_down': 'delta_w', 'delta_ple_w_gate': 'delta_w', 'delta_ple_b_gate': 'delta_w', 'delta_ple_w_proj': 'delta_w', 'delta_ln2_g': 'delta_w', 'delta_ln2_b': 'delta_w', 'new_m_w_in': 'new_m', 'new_m_gdn_conv_w': 'new_m', 'new_m_gdn_a_log': 'new_m', 'new_m_gdn_dt_bias': 'new_m', 'new_m_gdn_norm_g': 'new_m', 'new_m_mla_q_norm_g': 'new_m', 'new_m_mla_w_q_up': 'new_m', 'new_m_mla_kv_norm_g': 'new_m', 'new_m_mla_w_kv_up': 'new_m', 'new_m_w_out': 'new_m', 'new_m_ln1_g': 'new_m', 'new_m_ln1_b': 'new_m', 'new_m_ffn_w_up': 'new_m', 'new_m_ffn_conv_w': 'new_m', 'new_m_ffn_conv_b': 'new_m', 'new_m_ffn_w_down': 'new_m', 'new_m_ple_w_gate': 'new_m', 'new_m_ple_b_gate': 'new_m', 'new_m_ple_w_proj': 'new_m', 'new_m_ln2_g': 'new_m', 'new_m_ln2_b': 'new_m', 'new_v_w_in': 'new_v', 'new_v_gdn_conv_w': 'new_v', 'new_v_gdn_a_log': 'new_v', 'new_v_gdn_dt_bias': 'new_v', 'new_v_gdn_norm_g': 'new_v', 'new_v_mla_q_norm_g': 'new_v', 'new_v_mla_w_q_up': 'new_v', 'new_v_mla_kv_norm_g': 'new_v', 'new_v_mla_w_kv_up': 'new_v', 'new_v_w_out': 'new_v', 'new_v_ln1_g': 'new_v', 'new_v_ln1_b': 'new_v', 'new_v_ffn_w_up': 'new_v', 'new_v_ffn_conv_w': 'new_v', 'new_v_ffn_conv_b': 'new_v', 'new_v_ffn_w_down': 'new_v', 'new_v_ple_w_gate': 'new_v', 'new_v_ple_b_gate': 'new_v', 'new_v_ple_w_proj': 'new_v', 'new_v_ln2_g': 'new_v', 'new_v_ln2_b': 'new_v'}


def _forward(args):
    return _fwd_reference(*[args[k] for k in FWD_PARAMS])


def _output_shape():
    out = _jax.eval_shape(lambda: _forward(_fwd_setup_inputs(0)))
    return out.shape, out.dtype

N_MICROBATCH = 1
ADAM_LR = 0.001
ADAM_B1 = 0.9
ADAM_B2 = 0.999
ADAM_EPS = 1e-08
ADAM_WD = 0.01
ADAM_STEP = 10
PER_EXAMPLE_BATCH_AXIS = {'x': 0, 'p': 1, 'loss_target': 0}
SHARED_INPUTS = []
_WEIGHT_DTYPES = {'w_in': _jnp.float32, 'gdn_conv_w': _jnp.float32, 'gdn_a_log': _jnp.float32, 'gdn_dt_bias': _jnp.float32, 'gdn_norm_g': _jnp.float32, 'mla_q_norm_g': _jnp.float32, 'mla_w_q_up': _jnp.float32, 'mla_kv_norm_g': _jnp.float32, 'mla_w_kv_up': _jnp.float32, 'w_out': _jnp.float32, 'ln1_g': _jnp.float32, 'ln1_b': _jnp.float32, 'ffn_w_up': _jnp.float32, 'ffn_conv_w': _jnp.float32, 'ffn_conv_b': _jnp.float32, 'ffn_w_down': _jnp.float32, 'ple_w_gate': _jnp.float32, 'ple_b_gate': _jnp.float32, 'ple_w_proj': _jnp.float32, 'ln2_g': _jnp.float32, 'ln2_b': _jnp.float32}
MOMENT_SCALE = {'w_in': 5.427040e-02, 'gdn_conv_w': 5.659225e-02, 'gdn_a_log': 7.399032e-01, 'gdn_dt_bias': 7.009002e-01, 'gdn_norm_g': 1.974827e-01, 'mla_q_norm_g': 2.161232e-02, 'mla_w_q_up': 1.505088e-02, 'mla_kv_norm_g': 3.593010e-02, 'mla_w_kv_up': 1.750497e-02, 'w_out': 8.934116e-02, 'ln1_g': 1.634380e+00, 'ln1_b': 6.716073e-01, 'ffn_w_up': 4.411378e-02, 'ffn_conv_w': 4.347174e-02, 'ffn_conv_b': 5.196954e-02, 'ffn_w_down': 1.210841e-01, 'ple_w_gate': 2.445376e-02, 'ple_b_gate': 3.760530e-02, 'ple_w_proj': 1.057349e-01, 'ln2_g': 6.401584e+01, 'ln2_b': 1.981352e+00}


def _to_microbatches(a, axis):
    t = _jnp.moveaxis(a, axis, 0)
    t = t.reshape((N_MICROBATCH, t.shape[0] // N_MICROBATCH) + t.shape[1:])
    return _jnp.moveaxis(t, 1, axis + 1)


def setup_inputs(seed: int = 0) -> dict:
    inp = _fwd_setup_inputs(seed)
    key = _jax.random.fold_in(_jax.random.key(seed), 7919)
    shape, _ = _output_shape()
    out = dict(inp)
    out["loss_target"] = _jax.random.normal(_jax.random.fold_in(key, 0), shape, _jnp.float32)
    for i, name in enumerate(TWIN_WEIGHTS):
        w = inp[name].astype(_jnp.float32)
        if MOMENT_SCALE is None:
            s = _jnp.sqrt(_jnp.mean(_jnp.square(w)) + 1e-30)
        else:
            s = MOMENT_SCALE[name]
        km, kv = _jax.random.split(_jax.random.fold_in(key, i + 1))
        out[name] = w
        out["m_" + name] = s * _jax.random.normal(km, w.shape, _jnp.float32)
        out["v_" + name] = (s * s) * _jax.random.uniform(kv, w.shape, _jnp.float32, 0.5, 1.5)
    if N_MICROBATCH > 1:
        for name, axis in PER_EXAMPLE_BATCH_AXIS.items():
            out[name] = _to_microbatches(out[name], axis)
    return {'x': out['x'], 'p': out['p'], 'w_in': out['w_in'], 'gdn_conv_w': out['gdn_conv_w'], 'gdn_a_log': out['gdn_a_log'], 'gdn_dt_bias': out['gdn_dt_bias'], 'gdn_norm_g': out['gdn_norm_g'], 'mla_q_norm_g': out['mla_q_norm_g'], 'mla_w_q_up': out['mla_w_q_up'], 'mla_kv_norm_g': out['mla_kv_norm_g'], 'mla_w_kv_up': out['mla_w_kv_up'], 'w_out': out['w_out'], 'ln1_g': out['ln1_g'], 'ln1_b': out['ln1_b'], 'ffn_w_up': out['ffn_w_up'], 'ffn_conv_w': out['ffn_conv_w'], 'ffn_conv_b': out['ffn_conv_b'], 'ffn_w_down': out['ffn_w_down'], 'ple_w_gate': out['ple_w_gate'], 'ple_b_gate': out['ple_b_gate'], 'ple_w_proj': out['ple_w_proj'], 'ln2_g': out['ln2_g'], 'ln2_b': out['ln2_b'], 'loss_target': out['loss_target'], 'm_w_in': out['m_w_in'], 'm_gdn_conv_w': out['m_gdn_conv_w'], 'm_gdn_a_log': out['m_gdn_a_log'], 'm_gdn_dt_bias': out['m_gdn_dt_bias'], 'm_gdn_norm_g': out['m_gdn_norm_g'], 'm_mla_q_norm_g': out['m_mla_q_norm_g'], 'm_mla_w_q_up': out['m_mla_w_q_up'], 'm_mla_kv_norm_g': out['m_mla_kv_norm_g'], 'm_mla_w_kv_up': out['m_mla_w_kv_up'], 'm_w_out': out['m_w_out'], 'm_ln1_g': out['m_ln1_g'], 'm_ln1_b': out['m_ln1_b'], 'm_ffn_w_up': out['m_ffn_w_up'], 'm_ffn_conv_w': out['m_ffn_conv_w'], 'm_ffn_conv_b': out['m_ffn_conv_b'], 'm_ffn_w_down': out['m_ffn_w_down'], 'm_ple_w_gate': out['m_ple_w_gate'], 'm_ple_b_gate': out['m_ple_b_gate'], 'm_ple_w_proj': out['m_ple_w_proj'], 'm_ln2_g': out['m_ln2_g'], 'm_ln2_b': out['m_ln2_b'], 'v_w_in': out['v_w_in'], 'v_gdn_conv_w': out['v_gdn_conv_w'], 'v_gdn_a_log': out['v_gdn_a_log'], 'v_gdn_dt_bias': out['v_gdn_dt_bias'], 'v_gdn_norm_g': out['v_gdn_norm_g'], 'v_mla_q_norm_g': out['v_mla_q_norm_g'], 'v_mla_w_q_up': out['v_mla_w_q_up'], 'v_mla_kv_norm_g': out['v_mla_kv_norm_g'], 'v_mla_w_kv_up': out['v_mla_w_kv_up'], 'v_w_out': out['v_w_out'], 'v_ln1_g': out['v_ln1_g'], 'v_ln1_b': out['v_ln1_b'], 'v_ffn_w_up': out['v_ffn_w_up'], 'v_ffn_conv_w': out['v_ffn_conv_w'], 'v_ffn_conv_b': out['v_ffn_conv_b'], 'v_ffn_w_down': out['v_ffn_w_down'], 'v_ple_w_gate': out['v_ple_w_gate'], 'v_ple_b_gate': out['v_ple_b_gate'], 'v_ple_w_proj': out['v_ple_w_proj'], 'v_ln2_g': out['v_ln2_g'], 'v_ln2_b': out['v_ln2_b']}


def _loss(weights, diff, rest, loss_target):
    with _jax.named_scope("forward"):
        args = {**rest, TWIN_DIFF_INPUT: diff, **{k: w.astype(_WEIGHT_DTYPES[k]) for k, w in weights.items()}}
        y = _forward(args)
    with _jax.named_scope("loss_head"):
        err = _jnp.square(y.astype(_jnp.float32) - loss_target)
        return 0.5 * _jnp.sum(_jnp.mean(err, axis=-1)) if err.ndim else 0.5 * err


def _adamw(w, g, m, v):
    m = ADAM_B1 * m + (1.0 - ADAM_B1) * g
    v = ADAM_B2 * v + (1.0 - ADAM_B2) * _jnp.square(g)
    m_hat = m / (1.0 - ADAM_B1 ** ADAM_STEP)
    v_hat = v / (1.0 - ADAM_B2 ** ADAM_STEP)
    delta = -ADAM_LR * (m_hat / (_jnp.sqrt(v_hat) + ADAM_EPS) + ADAM_WD * w)
    return delta, m, v


def reference(x, p, w_in, gdn_conv_w, gdn_a_log, gdn_dt_bias, gdn_norm_g, mla_q_norm_g, mla_w_q_up, mla_kv_norm_g, mla_w_kv_up, w_out, ln1_g, ln1_b, ffn_w_up, ffn_conv_w, ffn_conv_b, ffn_w_down, ple_w_gate, ple_b_gate, ple_w_proj, ln2_g, ln2_b, loss_target, m_w_in, m_gdn_conv_w, m_gdn_a_log, m_gdn_dt_bias, m_gdn_norm_g, m_mla_q_norm_g, m_mla_w_q_up, m_mla_kv_norm_g, m_mla_w_kv_up, m_w_out, m_ln1_g, m_ln1_b, m_ffn_w_up, m_ffn_conv_w, m_ffn_conv_b, m_ffn_w_down, m_ple_w_gate, m_ple_b_gate, m_ple_w_proj, m_ln2_g, m_ln2_b, v_w_in, v_gdn_conv_w, v_gdn_a_log, v_gdn_dt_bias, v_gdn_norm_g, v_mla_q_norm_g, v_mla_w_q_up, v_mla_kv_norm_g, v_mla_w_kv_up, v_w_out, v_ln1_g, v_ln1_b, v_ffn_w_up, v_ffn_conv_w, v_ffn_conv_b, v_ffn_w_down, v_ple_w_gate, v_ple_b_gate, v_ple_w_proj, v_ln2_g, v_ln2_b):
    given = dict(x=x, p=p, w_in=w_in, gdn_conv_w=gdn_conv_w, gdn_a_log=gdn_a_log, gdn_dt_bias=gdn_dt_bias, gdn_norm_g=gdn_norm_g, mla_q_norm_g=mla_q_norm_g, mla_w_q_up=mla_w_q_up, mla_kv_norm_g=mla_kv_norm_g, mla_w_kv_up=mla_w_kv_up, w_out=w_out, ln1_g=ln1_g, ln1_b=ln1_b, ffn_w_up=ffn_w_up, ffn_conv_w=ffn_conv_w, ffn_conv_b=ffn_conv_b, ffn_w_down=ffn_w_down, ple_w_gate=ple_w_gate, ple_b_gate=ple_b_gate, ple_w_proj=ple_w_proj, ln2_g=ln2_g, ln2_b=ln2_b, loss_target=loss_target, m_w_in=m_w_in, m_gdn_conv_w=m_gdn_conv_w, m_gdn_a_log=m_gdn_a_log, m_gdn_dt_bias=m_gdn_dt_bias, m_gdn_norm_g=m_gdn_norm_g, m_mla_q_norm_g=m_mla_q_norm_g, m_mla_w_q_up=m_mla_w_q_up, m_mla_kv_norm_g=m_mla_kv_norm_g, m_mla_w_kv_up=m_mla_w_kv_up, m_w_out=m_w_out, m_ln1_g=m_ln1_g, m_ln1_b=m_ln1_b, m_ffn_w_up=m_ffn_w_up, m_ffn_conv_w=m_ffn_conv_w, m_ffn_conv_b=m_ffn_conv_b, m_ffn_w_down=m_ffn_w_down, m_ple_w_gate=m_ple_w_gate, m_ple_b_gate=m_ple_b_gate, m_ple_w_proj=m_ple_w_proj, m_ln2_g=m_ln2_g, m_ln2_b=m_ln2_b, v_w_in=v_w_in, v_gdn_conv_w=v_gdn_conv_w, v_gdn_a_log=v_gdn_a_log, v_gdn_dt_bias=v_gdn_dt_bias, v_gdn_norm_g=v_gdn_norm_g, v_mla_q_norm_g=v_mla_q_norm_g, v_mla_w_q_up=v_mla_w_q_up, v_mla_kv_norm_g=v_mla_kv_norm_g, v_mla_w_kv_up=v_mla_w_kv_up, v_w_out=v_w_out, v_ln1_g=v_ln1_g, v_ln1_b=v_ln1_b, v_ffn_w_up=v_ffn_w_up, v_ffn_conv_w=v_ffn_conv_w, v_ffn_conv_b=v_ffn_conv_b, v_ffn_w_down=v_ffn_w_down, v_ple_w_gate=v_ple_w_gate, v_ple_b_gate=v_ple_b_gate, v_ple_w_proj=v_ple_w_proj, v_ln2_g=v_ln2_g, v_ln2_b=v_ln2_b)
    weights = {n: given[n] for n in TWIN_WEIGHTS}
    shared = {n: given[n] for n in SHARED_INPUTS}
    per_example = {n: given[n] for n in ['x', 'p']}
    grad_fn = _jax.value_and_grad(_loss, argnums=(0, 1))

    def one_microbatch(ex, loss_target):
        ex = dict(ex)
        diff = ex.pop(TWIN_DIFF_INPUT)
        return grad_fn(weights, diff, {**shared, **ex}, loss_target)

    if N_MICROBATCH == 1:
        loss, (grad_w, grad_x) = one_microbatch(per_example, given["loss_target"])
    else:
        def body(carry, xs):
            loss_sum, grad_sum = carry
            l_k, (gw_k, gx_k) = one_microbatch(xs[0], xs[1])
            with _jax.named_scope("update"):
                return (loss_sum + l_k, _jax.tree.map(_jnp.add, grad_sum, gw_k)), gx_k

        init = (_jnp.zeros((), _jnp.float32), _jax.tree.map(_jnp.zeros_like, weights))
        (loss, grad_w), grad_x = _jax.lax.scan(body, init, (per_example, given["loss_target"]))
    with _jax.named_scope("update"):
        delta_w, new_m, new_v = {}, {}, {}
        for n in TWIN_WEIGHTS:
            delta_w[n], new_m[n], new_v[n] = _adamw(weights[n], grad_w[n], given["m_" + n], given["v_" + n])
    return (loss, grad_x, *[grad_w[n] for n in TWIN_WEIGHTS], *[delta_w[n] for n in TWIN_WEIGHTS],
            *[new_m[n] for n in TWIN_WEIGHTS], *[new_v[n] for n in TWIN_WEIGHTS])
```

```python
import functools
import math

import numpy as np
import jax
import jax.numpy as jnp
from jax import lax
from jax.experimental import pallas as pl
from jax.experimental.pallas import tpu as pltpu

F32 = jnp.float32
BF16 = jnp.bfloat16

D_MODEL = 1024
CHUNK = 64
PLE_DIM = 256
GDN_HEADS = 4
GDN_DK = 128
GDN_DV = 128
GDN_CONV = 4
MLA_HEADS = 4
MLA_NOPE = 128
MLA_ROPE = 64
MLA_V = 128
MLA_Q_LORA = 384
MLA_KV_LORA = 256
ROPE_THETA = 10000.0
D_FF = 2816
FFN_CONV = 3
DEPTH = 1
ALPHA = (2.0 * DEPTH) ** 0.25
NORM_EPS = 1e-6
GDN_QK = GDN_HEADS * GDN_DK
GDN_VW = GDN_HEADS * GDN_DV
D_IN = 2 * GDN_QK + 2 * GDN_VW + 2 * GDN_HEADS + MLA_Q_LORA + MLA_KV_LORA + MLA_ROPE
ATT_SCALE = (MLA_NOPE + MLA_ROPE) ** -0.5

ADAM_LR = 0.001
ADAM_B1 = 0.9
ADAM_B2 = 0.999
ADAM_EPS = 1e-08
ADAM_WD = 0.01
ADAM_STEP = 10

LANES = 128
VMEM_LIMIT = 60 * 1024 * 1024
N_CHIPS = 4
N_DEV = 8

P_WIDTH = 3072
P_MLA = 2048
MESH = pl.DeviceIdType.MESH


def _rope_slot(j):
    return j if j < MLA_ROPE // 2 else 64 + (j - MLA_ROPE // 2)


def _w_in_cols():
    idx = -np.ones((P_WIDTH,), np.int64)
    for h in range(GDN_HEADS):
        base = h * 512
        idx[base:base + 128] = np.arange(128) + h * GDN_DK
        idx[base + 128:base + 256] = np.arange(128) + GDN_QK + h * GDN_DK
        idx[base + 256:base + 384] = np.arange(128) + 2 * GDN_QK + h * GDN_DV
        idx[base + 384:base + 512] = np.arange(128) + 2 * GDN_QK + GDN_VW + h * GDN_DV
    o_a = 2 * GDN_QK + 2 * GDN_VW
    idx[P_MLA:P_MLA + 2 * GDN_HEADS] = np.arange(2 * GDN_HEADS) + o_a
    o_cq = o_a + 2 * GDN_HEADS
    idx[P_MLA + 128:P_MLA + 512] = np.arange(MLA_Q_LORA) + o_cq
    o_ckv = o_cq + MLA_Q_LORA
    idx[P_MLA + 512:P_MLA + 768] = np.arange(MLA_KV_LORA) + o_ckv
    o_kr = o_ckv + MLA_KV_LORA
    for j in range(MLA_ROPE):
        idx[P_MLA + 768 + _rope_slot(j)] = o_kr + j
    return idx


def _w_q_cols():
    idx = -np.ones((MLA_HEADS * 256,), np.int64)
    for h in range(MLA_HEADS):
        o = h * (MLA_NOPE + MLA_ROPE)
        idx[h * 256:h * 256 + 128] = np.arange(128) + o
        for j in range(MLA_ROPE):
            idx[h * 256 + 128 + _rope_slot(j)] = o + MLA_NOPE + j
    return idx


def _pad_cols(w, idx):
    safe = np.where(idx >= 0, idx, 0)
    return jnp.where(jnp.asarray(idx >= 0)[None, :], w[:, safe], 0.0)


def _unpad_cols(wp, idx, n):
    inv = np.zeros((n,), np.int64)
    inv[idx[idx >= 0]] = np.nonzero(idx >= 0)[0]
    return wp[:, inv]


def _dot(a, b, ca, cb, precision=None):
    if precision is None:
        a = a.astype(BF16)
        b = b.astype(BF16)
    return lax.dot_general(a, b, (((ca,), (cb,)), ((), ())), preferred_element_type=F32, precision=precision)


@jax.custom_vjp
def mm(a, b):
    return _dot(a, b, 1, 0)


@jax.custom_vjp
def mm_nt(a, b):
    return _dot(a, b, 1, 1)


@jax.custom_vjp
def mm_tn(a, b):
    return _dot(a, b, 0, 0)


mm.defvjp(lambda a, b: (mm(a, b), (a, b)), lambda r, g: (mm_nt(g, r[1]), mm_tn(r[0], g)))
mm_nt.defvjp(lambda a, b: (mm_nt(a, b), (a, b)), lambda r, g: (mm(g, r[1]), mm_tn(g, r[0])))
mm_tn.defvjp(lambda a, b: (mm_tn(a, b), (a, b)), lambda r, g: (mm_nt(r[1], g), mm(r[0], g)))

_HI = lax.Precision.HIGHEST


def _unit_lower_inverse(low):
    n = low.shape[0]
    ii = lax.broadcasted_iota(jnp.int32, (n, n), 0)
    jj = lax.broadcasted_iota(jnp.int32, (n, n), 1)
    inv = jnp.where(ii == jj, 1.0, 0.0) - low
    power = _dot(low, low, 1, 0, _HI)
    k = 2
    while k < n:
        inv = inv + _dot(inv, power, 1, 0, _HI)
        k *= 2
        if k < n:
            power = _dot(power, power, 1, 0, _HI)
    return inv


@jax.custom_vjp
def unit_lower_solve(low, rhs):
    return _dot(_unit_lower_inverse(low), rhs, 1, 0, _HI)


def _uls_fwd(low, rhs):
    inv = _unit_lower_inverse(low)
    x = _dot(inv, rhs, 1, 0, _HI)
    return x, (inv, x)


def _uls_bwd(res, dx):
    inv, x = res
    drhs = _dot(inv, dx, 0, 0, _HI)
    n = inv.shape[0]
    ii = lax.broadcasted_iota(jnp.int32, (n, n), 0)
    jj = lax.broadcasted_iota(jnp.int32, (n, n), 1)
    dlow = jnp.where(ii > jj, -_dot(drhs, x, 1, 1, _HI), 0.0)
    return dlow, drhs


unit_lower_solve.defvjp(_uls_fwd, _uls_bwd)


def _shift_rows(x, s):
    if s == 0:
        return x
    n = x.shape[0]
    row = lax.broadcasted_iota(jnp.int32, x.shape, 0)
    rolled = pltpu.roll(x, s % n, 0)
    keep = (row >= s) if s > 0 else (row < n + s)
    return jnp.where(keep, rolled, 0.0)


def _row(w, j):
    tap = lax.broadcasted_iota(jnp.int32, w.shape, 0)
    return jnp.sum(jnp.where(tap == j, w, 0.0), axis=0, keepdims=True)


@jax.custom_vjp
def dwconv(x, w):
    k = w.shape[0]
    y = _row(w, k - 1) * x
    for j in range(k - 1):
        y = y + _row(w, j) * _shift_rows(x, k - 1 - j)
    return y


def _dwconv_fwd(x, w):
    return dwconv(x, w), (x, w)


def _dwconv_bwd(res, dy):
    x, w = res
    k = w.shape[0]
    dx = _row(w, k - 1) * dy
    tap = lax.broadcasted_iota(jnp.int32, w.shape, 0)
    dw = jnp.where(tap == k - 1, jnp.sum(dy * x, axis=0, keepdims=True), 0.0)
    for j in range(k - 1):
        dx = dx + _row(w, j) * _shift_rows(dy, -(k - 1 - j))
        dw = dw + jnp.where(tap == j, jnp.sum(dy * _shift_rows(x, k - 1 - j), axis=0, keepdims=True), 0.0)
    return dx, dw


dwconv.defvjp(_dwconv_fwd, _dwconv_bwd)


@jax.custom_vjp
def rope128(x, cos, sin):
    return x * cos + pltpu.roll(x, 64, 1) * sin


rope128.defvjp(lambda x, c, s: (rope128(x, c, s), (c, s)),
               lambda r, g: (g * r[0] + pltpu.roll(g * r[1], 64, 1), jnp.zeros_like(r[0]), jnp.zeros_like(r[1])))


def _silu(x):
    return x * jax.nn.sigmoid(x)


def _softplus(x):
    return jnp.maximum(x, 0.0) + jnp.log(1.0 + jnp.exp(-jnp.abs(x)))


def _rmsnorm(x, g):
    return x * lax.rsqrt(jnp.mean(x * x, axis=-1, keepdims=True) + NORM_EPS) * g


def _layernorm(x, g, b):
    mu = jnp.mean(x, axis=-1, keepdims=True)
    xc = x - mu
    var = jnp.mean(xc * xc, axis=-1, keepdims=True)
    return xc * lax.rsqrt(var + NORM_EPS) * g + b


def _pick_lane(row, lane):
    idx = lax.broadcasted_iota(jnp.int32, row.shape, 1)
    return jnp.sum(jnp.where(idx == lane, row, 0.0), axis=1, keepdims=True)


def _gdn_q(pq, cw):
    h = _silu(dwconv(pq, cw))
    return h * lax.rsqrt(jnp.sum(h * h, axis=-1, keepdims=True) + NORM_EPS) * (GDN_DK ** -0.5)


def _gdn_k(pk, cw):
    h = _silu(dwconv(pk, cw))
    return h * lax.rsqrt(jnp.sum(h * h, axis=-1, keepdims=True) + NORM_EPS)


def _gdn_v(pv, cw):
    return _silu(dwconv(pv, cw))


def _gdn_gate(ab, sc, head):
    a = _pick_lane(ab, head)
    b = _pick_lane(ab, GDN_HEADS + head)
    a_log = _pick_lane(_row(sc, 0), head)
    dt_bias = _pick_lane(_row(sc, 1), head)
    beta = jax.nn.sigmoid(b)
    g = -jnp.exp(a_log) * _softplus(a + dt_bias)
    return _two_lanes(g, beta)


def _two_lanes(c0, c1):
    lane = lax.broadcasted_iota(jnp.int32, (c0.shape[0], LANES), 1)
    return jnp.where(lane == 0, c0, jnp.where(lane == 1, c1, 0.0))


def _gdn_chunk_packed(q, k, v, gb, state):
    return _gdn_chunk(q, k, v, _pick_lane(gb, 0), _pick_lane(gb, 1), state)


def _gdn_post(o, z, norm_g):
    return _rmsnorm(o, norm_g) * _silu(z)


def _gdn_chunk(q, k, v, g, beta, state):
    c = q.shape[0]
    ii = lax.broadcasted_iota(jnp.int32, (c, c), 0)
    jj = lax.broadcasted_iota(jnp.int32, (c, c), 1)
    incl = ii >= jj
    g_row = jnp.sum(jnp.where(ii == jj, g, 0.0), axis=0, keepdims=True)
    gc_col = jnp.sum(jnp.where(incl, g_row, 0.0), axis=1, keepdims=True)
    gc_row = jnp.sum(jnp.where(jj >= ii, g, 0.0), axis=0, keepdims=True)
    decay = jnp.where(incl, jnp.exp(jnp.where(incl, gc_col - gc_row, 0.0)), 0.0)
    kb = k * beta
    low = jnp.where(ii > jj, mm_nt(kb, k) * decay, 0.0)
    eg = jnp.exp(gc_col)
    wu = unit_lower_solve(low, jnp.concatenate([kb * eg, v * beta], axis=1))
    w, u = wu[:, :GDN_DK], wu[:, GDN_DK:]
    qk = mm_nt(q, k) * decay
    g_last = jnp.sum(g_row, axis=1, keepdims=True)
    kd = k * jnp.exp(g_last - gc_col)
    v_new = u - mm(w, state)
    o = mm(q * eg, state) + mm(qk, v_new)
    new_state = state * jnp.exp(g_last) + mm_tn(kd, v_new)
    return o, new_state


def _attn_block(q, kn, kr, v, q0):
    s = (mm_nt(q[:, :128], kn) + mm_nt(q[:, 128:], kr)) * ATT_SCALE
    qpos = q0 + lax.broadcasted_iota(jnp.int32, s.shape, 0)
    kpos = lax.broadcasted_iota(jnp.int32, s.shape, 1)
    shift = int(math.log2(CHUNK))
    allowed = (kpos >> shift) <= (qpos >> shift)
    s = jnp.where(allowed, s, -1e30)
    p = jnp.exp(s - jnp.max(s, axis=-1, keepdims=True))
    p = p / jnp.sum(p, axis=-1, keepdims=True)
    return mm(p, v)


def _mla_prep(pm, qg, kvg, wq, wkv, cos, sin):
    cq = pm[:, 128:512]
    ckv = pm[:, 512:768]
    qf = mm(_rmsnorm(cq, qg), wq)
    parts = []
    for h in range(MLA_HEADS):
        parts.append(qf[:, h * 256:h * 256 + 128])
        parts.append(rope128(qf[:, h * 256 + 128:h * 256 + 256], cos, sin))
    kvf = mm(_rmsnorm(ckv, kvg), wkv)
    return jnp.concatenate(parts, axis=1), kvf, rope128(pm[:, 768:896], cos, sin)


def _ffn_act(ug, uu, wg, wu, bg, bu):
    return _silu(dwconv(ug, wg) + bg) * (dwconv(uu, wu) + bu)


def _head_loss(h1, ffn, gpre, pp, bgate, g2, b2, target):
    gate = jax.nn.sigmoid(gpre + bgate)
    h2 = _layernorm(ALPHA * h1 + ffn + gate * pp, g2, b2)
    err = h2 - target
    return 0.5 * jnp.sum(jnp.sum(err * err, axis=1, keepdims=True), axis=0, keepdims=True) / D_MODEL


def _params(sem):
    return pltpu.CompilerParams(dimension_semantics=sem, vmem_limit_bytes=VMEM_LIMIT)


def _matmul(a, b, *, name, ta=False, tb=False, tm=512, tn=512, tk=1024, add=None, add_scale=1.0):
    (k_dim, m) = a.shape if ta else a.shape[::-1]
    (n, k2) = b.shape if tb else b.shape[::-1]
    assert k_dim == k2, (a.shape, b.shape)
    tm, tn, tk = min(tm, m), min(tn, n), min(tk, k_dim)
    assert m % tm == 0 and n % tn == 0 and k_dim % tk == 0, (name, m, n, k_dim, tm, tn, tk)
    nk = k_dim // tk
    ca, cb = (0 if ta else 1), (1 if tb else 0)

    def body(*refs):
        if add is None:
            a_ref, b_ref, o_ref, acc = refs
        else:
            a_ref, b_ref, c_ref, o_ref, acc = refs
        kk = pl.program_id(2)

        @pl.when(kk == 0)
        def _():
            acc[...] = jnp.zeros_like(acc)

        acc[...] += _dot(a_ref[...], b_ref[...], ca, cb)

        @pl.when(kk == nk - 1)
        def _():
            r = acc[...]
            if add is not None:
                r = r + add_scale * c_ref[...]
            o_ref[...] = r

    a_spec = pl.BlockSpec((tk, tm), lambda i, j, k: (k, i)) if ta else pl.BlockSpec((tm, tk), lambda i, j, k: (i, k))
    b_spec = pl.BlockSpec((tn, tk), lambda i, j, k: (j, k)) if tb else pl.BlockSpec((tk, tn), lambda i, j, k: (k, j))
    in_specs = [a_spec, b_spec]
    args = [a, b]
    if add is not None:
        in_specs.append(pl.BlockSpec((tm, tn), lambda i, j, k: (i, j)))
        args.append(add)
    return pl.pallas_call(
        body, name=name, grid=(m // tm, n // tn, nk),
        in_specs=in_specs, out_specs=pl.BlockSpec((tm, tn), lambda i, j, k: (i, j)),
        out_shape=jax.ShapeDtypeStruct((m, n), F32),
        scratch_shapes=[pltpu.VMEM((tm, tn), F32)],
        compiler_params=_params(("parallel", "parallel", "arbitrary")),
    )(*args)


def _gdn_fwd(proj, conv_w, sc, norm_g, bl, s):
    nc = s // CHUNK

    def body(ph_ref, ab_ref, cwq_ref, cwk_ref, cwv_ref, sc_ref, ng_ref, cat_ref, o_ref, st_ref, q_s, k_s, v_s, gb_s):
        q_s[...] = _gdn_q(ph_ref[:, 0:128], cwq_ref[...])
        k_s[...] = _gdn_k(ph_ref[:, 128:256], cwk_ref[...])
        v_s[...] = _gdn_v(ph_ref[:, 256:384], cwv_ref[...])
        gb_s[...] = _gdn_gate(ab_ref[...], sc_ref[...], pl.program_id(1))

        def chunk(n, state):
            rows = pl.ds(pl.multiple_of(n * CHUNK, CHUNK), CHUNK)
            st_ref[n] = state
            o, new_state = _gdn_chunk_packed(q_s[rows, :], k_s[rows, :], v_s[rows, :], gb_s[rows, :], state)
            o_ref[rows, :] = o
            return new_state

        lax.fori_loop(0, nc, chunk, jnp.zeros((GDN_DK, GDN_DV), F32))
        cat_ref[...] = _gdn_post(o_ref[...], ph_ref[:, 384:512], ng_ref[...])

    t = bl * s
    return pl.pallas_call(
        body, name="gdn_fwd", grid=(bl, GDN_HEADS),
        in_specs=[
            pl.BlockSpec((s, 512), lambda b, h: (b, h)),
            pl.BlockSpec((s, 128), lambda b, h: (b, P_MLA // 128)),
            pl.BlockSpec((GDN_CONV, 128), lambda b, h: (0, h)),
            pl.BlockSpec((GDN_CONV, 128), lambda b, h: (0, GDN_HEADS + h)),
            pl.BlockSpec((GDN_CONV, 128), lambda b, h: (0, 2 * GDN_HEADS + h)),
            pl.BlockSpec((8, 128), lambda b, h: (0, 0)),
            pl.BlockSpec((1, 128), lambda b, h: (0, 0)),
        ],
        out_specs=[
            pl.BlockSpec((s, 128), lambda b, h: (b, h)),
            pl.BlockSpec((s, 128), lambda b, h: (b, h)),
            pl.BlockSpec((None, None, nc, GDN_DK, GDN_DV), lambda b, h: (b, h, 0, 0, 0)),
        ],
        out_shape=[
            jax.ShapeDtypeStruct((t, 2 * GDN_VW), F32),
            jax.ShapeDtypeStruct((t, GDN_VW), F32),
            jax.ShapeDtypeStruct((bl, GDN_HEADS, nc, GDN_DK, GDN_DV), F32),
        ],
        scratch_shapes=[pltpu.VMEM((s, 128), F32)] * 4,
        compiler_params=_params(("parallel", "arbitrary")),
    )(proj, proj, conv_w, conv_w, conv_w, sc, norm_g)


def _gdn_bwd(proj, conv_w, sc, norm_g, o_raw, states, dcat, bl, s):
    nc = s // CHUNK

    def body(ph_ref, ab_ref, cwq_ref, cwk_ref, cwv_ref, sc_ref, ng_ref, o_ref, st_ref, dc_ref,
             dph_ref, dab_ref, dcwq_ref, dcwk_ref, dcwv_ref, dsc_ref, dng_ref, q_s, k_s, v_s, gb_s, do_s):
        head = pl.program_id(1)
        gate = functools.partial(_gdn_gate, head=head)
        paths = [(_gdn_q, 0, cwq_ref, q_s, dcwq_ref), (_gdn_k, 128, cwk_ref, k_s, dcwk_ref), (_gdn_v, 256, cwv_ref, v_s, dcwv_ref)]
        for fn, col, cw_ref, val_s, _ in paths:
            val_s[...] = fn(ph_ref[:, col:col + 128], cw_ref[...])
        gb_s[...] = gate(ab_ref[...], sc_ref[...])
        _, post_vjp = jax.vjp(_gdn_post, o_ref[...], ph_ref[:, 384:512], ng_ref[...])
        d_o, dz, dng = post_vjp(dc_ref[...])
        do_s[...] = d_o
        dph_ref[:, 384:512] = dz
        dng_ref[...] = jnp.broadcast_to(dng, dng_ref.shape)

        def chunk(i, dstate):
            n = nc - 1 - i
            rows = pl.ds(pl.multiple_of(n * CHUNK, CHUNK), CHUNK)
            _, vjp = jax.vjp(_gdn_chunk_packed, q_s[rows, :], k_s[rows, :], v_s[rows, :], gb_s[rows, :], st_ref[n])
            dq, dk, dv, dgb, dstate_in = vjp((do_s[rows, :], dstate))
            q_s[rows, :], k_s[rows, :], v_s[rows, :], gb_s[rows, :] = dq, dk, dv, dgb
            return dstate_in

        lax.fori_loop(0, nc, chunk, jnp.zeros((GDN_DK, GDN_DV), F32))
        for fn, col, cw_ref, val_s, dcw_ref in paths:
            _, vjp = jax.vjp(fn, ph_ref[:, col:col + 128], cw_ref[...])
            dph_ref[:, col:col + 128], dcw_ref[...] = vjp(val_s[...])
        _, gate_vjp = jax.vjp(gate, ab_ref[...], sc_ref[...])
        dab, dsc_ref[...] = gate_vjp(gb_s[...])

        @pl.when(head == 0)
        def _():
            dab_ref[...] = jnp.zeros_like(dab_ref)

        dab_ref[...] += dab

    t = bl * s
    cw_out = pl.BlockSpec((None, GDN_CONV, 128), lambda b, h: (b, 0, h))
    part = pl.BlockSpec((None, None, 8, 128), lambda b, h: (b, h, 0, 0))
    return pl.pallas_call(
        body, name="gdn_bwd", grid=(bl, GDN_HEADS),
        in_specs=[
            pl.BlockSpec((s, 512), lambda b, h: (b, h)),
            pl.BlockSpec((s, 128), lambda b, h: (b, P_MLA // 128)),
            pl.BlockSpec((GDN_CONV, 128), lambda b, h: (0, h)),
            pl.BlockSpec((GDN_CONV, 128), lambda b, h: (0, GDN_HEADS + h)),
            pl.BlockSpec((GDN_CONV, 128), lambda b, h: (0, 2 * GDN_HEADS + h)),
            pl.BlockSpec((8, 128), lambda b, h: (0, 0)),
            pl.BlockSpec((1, 128), lambda b, h: (0, 0)),
            pl.BlockSpec((s, 128), lambda b, h: (b, h)),
            pl.BlockSpec((None, None, nc, GDN_DK, GDN_DV), lambda b, h: (b, h, 0, 0, 0)),
            pl.BlockSpec((s, 128), lambda b, h: (b, h)),
        ],
        out_specs=[
            pl.BlockSpec((s, 512), lambda b, h: (b, h)),
            pl.BlockSpec((s, 128), lambda b, h: (b, 0)),
            cw_out, cw_out, cw_out, part, part,
        ],
        out_shape=[
            jax.ShapeDtypeStruct((t, P_WIDTH), F32),
            jax.ShapeDtypeStruct((t, 128), F32),
            jax.ShapeDtypeStruct((bl, GDN_CONV, 512), F32),
            jax.ShapeDtypeStruct((bl, GDN_CONV, 512), F32),
            jax.ShapeDtypeStruct((bl, GDN_CONV, 512), F32),
            jax.ShapeDtypeStruct((bl, GDN_HEADS, 8, 128), F32),
            jax.ShapeDtypeStruct((bl, GDN_HEADS, 8, 128), F32),
        ],
        scratch_shapes=[pltpu.VMEM((s, 128), F32)] * 5,
        compiler_params=_params(("parallel", "arbitrary")),
    )(proj, proj, conv_w, conv_w, conv_w, sc, norm_g, o_raw, states, dcat)


def _mla_prep_fwd(proj, qg, kvg, wq, wkv, cos, sin, s, tm):
    t = proj.shape[0]
    tm = min(tm, s)
    nps = s // tm
    const = lambda shape: pl.BlockSpec(shape, lambda i: (0, 0))

    def body(pm_ref, qg_ref, kvg_ref, wq_ref, wkv_ref, cos_ref, sin_ref, qf_ref, kvf_ref, kr_ref):
        qf, kvf, kr = _mla_prep(pm_ref[...], qg_ref[...], kvg_ref[...], wq_ref[...], wkv_ref[...], cos_ref[...], sin_ref[...])
        qf_ref[...], kvf_ref[...], kr_ref[...] = qf, kvf, kr

    return pl.pallas_call(
        body, name="mla_prep_fwd", grid=(t // tm,),
        in_specs=[
            pl.BlockSpec((tm, 1024), lambda i: (i, P_MLA // 1024)),
            const((1, MLA_Q_LORA)), const((1, MLA_KV_LORA)), const(wq.shape), const(wkv.shape),
            pl.BlockSpec((tm, 128), lambda i: (i % nps, 0)), pl.BlockSpec((tm, 128), lambda i: (i % nps, 0)),
        ],
        out_specs=[pl.BlockSpec((tm, 1024), lambda i: (i, 0)), pl.BlockSpec((tm, 1024), lambda i: (i, 0)),
                   pl.BlockSpec((tm, 128), lambda i: (i, 0))],
        out_shape=[jax.ShapeDtypeStruct((t, 1024), F32), jax.ShapeDtypeStruct((t, 1024), F32),
                   jax.ShapeDtypeStruct((t, 128), F32)],
        compiler_params=_params(("parallel",)),
    )(proj, qg, kvg, wq, wkv, cos, sin)


def _mla_prep_bwd(proj, qg, kvg, wq, wkv, cos, sin, dqf, dkvf, dkr, dab, dproj, s, tm):
    t = proj.shape[0]
    tm = min(tm, s)
    nps = s // tm
    const = lambda shape: pl.BlockSpec(shape, lambda i: (0, 0))

    def body(pm_ref, qg_ref, kvg_ref, wq_ref, wkv_ref, cos_ref, sin_ref, dqf_ref, dkvf_ref, dkr_ref, dab_ref, dp_in,
             dp_ref, dqg_ref, dkvg_ref, dwq_ref, dwkv_ref):
        del dp_in
        fn = lambda pm, qg_, kvg_, wq_, wkv_: _mla_prep(pm, qg_, kvg_, wq_, wkv_, cos_ref[...], sin_ref[...])
        _, vjp = jax.vjp(fn, pm_ref[...], qg_ref[...], kvg_ref[...], wq_ref[...].astype(F32), wkv_ref[...].astype(F32))
        dpm, dqg, dkvg, dwq, dwkv = vjp((dqf_ref[...], dkvf_ref[...], dkr_ref[...]))
        dp_ref[...] = jnp.concatenate([dab_ref[...], dpm[:, 128:]], axis=1)

        @pl.when(pl.program_id(0) == 0)
        def _():
            dqg_ref[...] = jnp.zeros_like(dqg_ref)
            dkvg_ref[...] = jnp.zeros_like(dkvg_ref)
            dwq_ref[...] = jnp.zeros_like(dwq_ref)
            dwkv_ref[...] = jnp.zeros_like(dwkv_ref)

        dqg_ref[...] += dqg
        dkvg_ref[...] += dkvg
        dwq_ref[...] += dwq
        dwkv_ref[...] += dwkv

    rows = lambda w: pl.BlockSpec((tm, w), lambda i: (i, 0))
    return pl.pallas_call(
        body, name="mla_prep_bwd", grid=(t // tm,),
        in_specs=[
            pl.BlockSpec((tm, 1024), lambda i: (i, P_MLA // 1024)),
            const((1, MLA_Q_LORA)), const((1, MLA_KV_LORA)), const(wq.shape), const(wkv.shape),
            pl.BlockSpec((tm, 128), lambda i: (i % nps, 0)), pl.BlockSpec((tm, 128), lambda i: (i % nps, 0)),
            rows(1024), rows(1024), rows(128), rows(128),
            pl.BlockSpec(memory_space=pl.ANY),
        ],
        out_specs=[pl.BlockSpec((tm, 1024), lambda i: (i, P_MLA // 1024)),
                   const((1, MLA_Q_LORA)), const((1, MLA_KV_LORA)), const(wq.shape), const(wkv.shape)],
        out_shape=[jax.ShapeDtypeStruct(dproj.shape, F32),
                   jax.ShapeDtypeStruct((1, MLA_Q_LORA), F32), jax.ShapeDtypeStruct((1, MLA_KV_LORA), F32),
                   jax.ShapeDtypeStruct(wq.shape, F32), jax.ShapeDtypeStruct(wkv.shape, F32)],
        input_output_aliases={11: 0},
        compiler_params=_params(("arbitrary",)),
    )(proj, qg, kvg, wq, wkv, cos, sin, dqf, dkvf, dkr, dab, dproj)


def _attn_fwd(qf, kvf, kr, cat, bl, s, tq):
    tq = min(tq, s)
    nq = s // tq

    def body(q_ref, kv_ref, kr_ref, cat_in, o_ref):
        del cat_in
        q0 = pl.program_id(2) * tq
        o_ref[...] = _attn_block(q_ref[...], kv_ref[:, 0:128], kr_ref[...], kv_ref[:, 128:256], q0)

    return pl.pallas_call(
        body, name="attn_fwd", grid=(bl, MLA_HEADS, nq),
        in_specs=[
            pl.BlockSpec((tq, 256), lambda b, h, i: (b * nq + i, h)),
            pl.BlockSpec((s, 256), lambda b, h, i: (b, h)),
            pl.BlockSpec((s, 128), lambda b, h, i: (b, 0)),
            pl.BlockSpec(memory_space=pl.ANY),
        ],
        out_specs=pl.BlockSpec((tq, 128), lambda b, h, i: (b * nq + i, GDN_HEADS + h)),
        out_shape=jax.ShapeDtypeStruct(cat.shape, F32),
        input_output_aliases={3: 0},
        compiler_params=_params(("parallel", "parallel", "parallel")),
    )(qf, kvf, kr, cat)


def _attn_bwd(qf, kvf, kr, dcat, bl, s, tq):
    tq = min(tq, s)
    nq = s // tq

    def body(q_ref, kv_ref, kr_ref, do_ref, dq_ref, dkv_ref, dkr_ref):
        h, i = pl.program_id(1), pl.program_id(2)
        fn = functools.partial(_attn_block, q0=i * tq)
        _, vjp = jax.vjp(fn, q_ref[...], kv_ref[:, 0:128], kr_ref[...], kv_ref[:, 128:256])
        dq, dkn, dkr, dv = vjp(do_ref[...])
        dq_ref[...] = dq

        @pl.when(i == 0)
        def _():
            dkv_ref[...] = jnp.zeros_like(dkv_ref)

        @pl.when((i == 0) & (h == 0))
        def _():
            dkr_ref[...] = jnp.zeros_like(dkr_ref)

        dkv_ref[...] += jnp.concatenate([dkn, dv], axis=1)
        dkr_ref[...] += dkr

    t = bl * s
    return pl.pallas_call(
        body, name="attn_bwd", grid=(bl, MLA_HEADS, nq),
        in_specs=[
            pl.BlockSpec((tq, 256), lambda b, h, i: (b * nq + i, h)),
            pl.BlockSpec((s, 256), lambda b, h, i: (b, h)),
            pl.BlockSpec((s, 128), lambda b, h, i: (b, 0)),
            pl.BlockSpec((tq, 128), lambda b, h, i: (b * nq + i, GDN_HEADS + h)),
        ],
        out_specs=[
            pl.BlockSpec((tq, 256), lambda b, h, i: (b * nq + i, h)),
            pl.BlockSpec((s, 256), lambda b, h, i: (b, h)),
            pl.BlockSpec((s, 128), lambda b, h, i: (b, 0)),
        ],
        out_shape=[jax.ShapeDtypeStruct((t, 1024), F32), jax.ShapeDtypeStruct((t, 1024), F32),
                   jax.ShapeDtypeStruct((t, 128), F32)],
        compiler_params=_params(("parallel", "arbitrary", "arbitrary")),
    )(qf, kvf, kr, dcat)


def _ln1_fwd(x, mix, g, b, tm):
    t = x.shape[0]
    tm = min(tm, t)

    def body(x_ref, mix_ref, g_ref, b_ref, r_ref, h_ref):
        r = ALPHA * x_ref[...] + mix_ref[...]
        r_ref[...] = r
        h_ref[...] = _layernorm(r, g_ref[...], b_ref[...])

    rows = pl.BlockSpec((tm, D_MODEL), lambda i: (i, 0))
    vec = pl.BlockSpec((1, D_MODEL), lambda i: (0, 0))
    return pl.pallas_call(
        body, name="ln1_fwd", grid=(t // tm,), in_specs=[rows, rows, vec, vec], out_specs=[rows, rows],
        out_shape=[jax.ShapeDtypeStruct(x.shape, F32)] * 2, compiler_params=_params(("parallel",)),
    )(x, mix, g, b)


def _ln1_bwd(r1, dr2, da, db_, g, b, tm):
    t = r1.shape[0]
    tm = min(tm, t)

    def body(r_ref, d2_ref, da_ref, db_ref, g_ref, b_ref, dr_ref, dg_ref, dbias_ref):
        dh = ALPHA * d2_ref[...] + da_ref[...] + db_ref[...]
        _, vjp = jax.vjp(_layernorm, r_ref[...], g_ref[...], b_ref[...])
        dr, dg, dbias = vjp(dh)
        dr_ref[...] = dr

        @pl.when(pl.program_id(0) == 0)
        def _():
            dg_ref[...] = jnp.zeros_like(dg_ref)
            dbias_ref[...] = jnp.zeros_like(dbias_ref)

        dg_ref[...] += dg
        dbias_ref[...] += dbias

    rows = pl.BlockSpec((tm, D_MODEL), lambda i: (i, 0))
    vec = pl.BlockSpec((1, D_MODEL), lambda i: (0, 0))
    return pl.pallas_call(
        body, name="ln1_bwd", grid=(t // tm,), in_specs=[rows] * 4 + [vec, vec], out_specs=[rows, vec, vec],
        out_shape=[jax.ShapeDtypeStruct(r1.shape, F32)] + [jax.ShapeDtypeStruct((1, D_MODEL), F32)] * 2,
        compiler_params=_params(("arbitrary",)),
    )(r1, dr2, da, db_, g, b)


def _ffn_act_fwd(u, conv_w, conv_b, bl, s, cb):
    nj = D_FF // cb

    def body(ug_ref, uu_ref, wg_ref, wu_ref, bg_ref, bu_ref, act_ref):
        act_ref[...] = _ffn_act(ug_ref[...], uu_ref[...], wg_ref[...], wu_ref[...], bg_ref[...], bu_ref[...])

    return pl.pallas_call(
        body, name="ffn_act_fwd", grid=(bl, nj),
        in_specs=[
            pl.BlockSpec((s, cb), lambda b, j: (b, j)), pl.BlockSpec((s, cb), lambda b, j: (b, nj + j)),
            pl.BlockSpec((FFN_CONV, cb), lambda b, j: (0, j)), pl.BlockSpec((FFN_CONV, cb), lambda b, j: (0, nj + j)),
            pl.BlockSpec((1, cb), lambda b, j: (0, j)), pl.BlockSpec((1, cb), lambda b, j: (0, nj + j)),
        ],
        out_specs=pl.BlockSpec((s, cb), lambda b, j: (b, j)),
        out_shape=jax.ShapeDtypeStruct((bl * s, D_FF), F32),
        compiler_params=_params(("parallel", "parallel")),
    )(u, u, conv_w, conv_w, conv_b, conv_b)


def _ffn_act_bwd(u, conv_w, conv_b, dact, bl, s, cb):
    nj = D_FF // cb

    def body(ug_ref, uu_ref, wg_ref, wu_ref, bg_ref, bu_ref, da_ref, dug_ref, duu_ref, dwg_ref, dwu_ref, dbg_ref, dbu_ref):
        _, vjp = jax.vjp(_ffn_act, ug_ref[...], uu_ref[...], wg_ref[...], wu_ref[...], bg_ref[...], bu_ref[...])
        dug, duu, dwg, dwu, dbg, dbu = vjp(da_ref[...])
        dug_ref[...], duu_ref[...], dwg_ref[...], dwu_ref[...], dbg_ref[...], dbu_ref[...] = dug, duu, dwg, dwu, dbg, dbu

    t = bl * s
    blk = pl.BlockSpec((s, cb), lambda b, j: (b, j))
    wpart = pl.BlockSpec((None, FFN_CONV, cb), lambda b, j: (b, 0, j))
    bpart = pl.BlockSpec((None, 1, cb), lambda b, j: (b, 0, j))
    return pl.pallas_call(
        body, name="ffn_act_bwd", grid=(bl, nj),
        in_specs=[
            blk, pl.BlockSpec((s, cb), lambda b, j: (b, nj + j)),
            pl.BlockSpec((FFN_CONV, cb), lambda b, j: (0, j)), pl.BlockSpec((FFN_CONV, cb), lambda b, j: (0, nj + j)),
            pl.BlockSpec((1, cb), lambda b, j: (0, j)), pl.BlockSpec((1, cb), lambda b, j: (0, nj + j)),
            blk,
        ],
        out_specs=[blk, blk, wpart, wpart, bpart, bpart],
        out_shape=[jax.ShapeDtypeStruct((t, D_FF), F32)] * 2 + [jax.ShapeDtypeStruct((bl, FFN_CONV, D_FF), F32)] * 2
        + [jax.ShapeDtypeStruct((bl, 1, D_FF), F32)] * 2,
        compiler_params=_params(("parallel", "parallel")),
    )(u, u, conv_w, conv_w, conv_b, conv_b, dact)


def _head(h1, ffn, gpre, pp, bgate, g2, b2, target, tm):
    t = h1.shape[0]
    tm = min(tm, t)

    def body(h1_ref, ffn_ref, gp_ref, pp_ref, bg_ref, g2_ref, b2_ref, tg_ref,
             dr_ref, dgp_ref, dpp_ref, loss_ref, dbg_ref, dg2_ref, db2_ref):
        fn = functools.partial(_head_loss, target=tg_ref[...])
        loss, vjp = jax.vjp(fn, h1_ref[...], ffn_ref[...], gp_ref[...], pp_ref[...], bg_ref[...], g2_ref[...], b2_ref[...])
        _, dffn, dgp, dpp, dbg, dg2, db2 = vjp(jnp.ones((1, 1), F32))
        dr_ref[...], dgp_ref[...], dpp_ref[...] = dffn, dgp, dpp

        @pl.when(pl.program_id(0) == 0)
        def _():
            loss_ref[...] = jnp.zeros_like(loss_ref)
            dbg_ref[...] = jnp.zeros_like(dbg_ref)
            dg2_ref[...] = jnp.zeros_like(dg2_ref)
            db2_ref[...] = jnp.zeros_like(db2_ref)

        loss_ref[...] += jnp.broadcast_to(loss, loss_ref.shape)
        dbg_ref[...] += dbg
        dg2_ref[...] += dg2
        db2_ref[...] += db2

    rows = pl.BlockSpec((tm, D_MODEL), lambda i: (i, 0))
    vec = pl.BlockSpec((1, D_MODEL), lambda i: (0, 0))
    return pl.pallas_call(
        body, name="head", grid=(t // tm,), in_specs=[rows] * 4 + [vec] * 3 + [rows],
        out_specs=[rows] * 3 + [pl.BlockSpec((8, 128), lambda i: (0, 0))] + [vec] * 3,
        out_shape=[jax.ShapeDtypeStruct(h1.shape, F32)] * 3 + [jax.ShapeDtypeStruct((8, 128), F32)]
        + [jax.ShapeDtypeStruct((1, D_MODEL), F32)] * 3,
        compiler_params=_params(("arbitrary",)),
    )(h1, ffn, gpre, pp, bgate, g2, b2, target)


def _sum4(a, b, tr):
    r, c = a.shape
    tr = min(tr, r)

    def body(a_ref, b_ref, o_ref):
        o_ref[...] = ((a_ref[...] + b_ref[0]) + b_ref[1]) + b_ref[2]

    return pl.pallas_call(
        body, name="grad_sum4", grid=(r // tr,),
        in_specs=[pl.BlockSpec((tr, c), lambda i: (i, 0)), pl.BlockSpec((3, tr, c), lambda i: (0, i, 0))],
        out_specs=pl.BlockSpec((tr, c), lambda i: (i, 0)), out_shape=jax.ShapeDtypeStruct(a.shape, F32),
        compiler_params=_params(("parallel",)),
    )(a, b)


def _adamw(ga, gb, w, m, v, tr, name):
    r, c = w.shape
    tr = min(tr, r)

    def body(ga_ref, gb_ref, w_ref, m_ref, v_ref, g_ref, d_ref, nm_ref, nv_ref):
        g = ga_ref[...] + gb_ref[...]
        m2 = ADAM_B1 * m_ref[...] + (1.0 - ADAM_B1) * g
        v2 = ADAM_B2 * v_ref[...] + (1.0 - ADAM_B2) * jnp.square(g)
        m_hat = m2 / (1.0 - ADAM_B1 ** ADAM_STEP)
        v_hat = v2 / (1.0 - ADAM_B2 ** ADAM_STEP)
        g_ref[...] = g
        d_ref[...] = -ADAM_LR * (m_hat / (jnp.sqrt(v_hat) + ADAM_EPS) + ADAM_WD * w_ref[...])
        nm_ref[...] = m2
        nv_ref[...] = v2

    blk = pl.BlockSpec((tr, c), lambda i: (i, 0))
    return pl.pallas_call(
        body, name=name, grid=(r // tr,), in_specs=[blk] * 5, out_specs=[blk] * 4,
        out_shape=[jax.ShapeDtypeStruct(w.shape, F32)] * 4, compiler_params=_params(("parallel",)),
    )(ga, gb, w, m, v)


def _chip_peers():
    x, y = lax.axis_index("x"), lax.axis_index("y")
    return 2 * x + y, [(1 - x, y), (x, 1 - y), (1 - x, 1 - y)]


def _all_gather_chips(shard):
    def body(in_ref, out_ref, send_sems, recv_sems, local_sem):
        c = lax.axis_index("c")
        me, peers = _chip_peers()
        mine = pltpu.make_async_copy(in_ref, out_ref.at[me], local_sem)
        mine.start()
        sends = [pltpu.make_async_remote_copy(src_ref=in_ref, dst_ref=out_ref.at[me], send_sem=send_sems.at[r],
                                              recv_sem=recv_sems.at[r], device_id=(px, py, c), device_id_type=MESH)
                 for r, (px, py) in enumerate(peers)]
        for cp in sends:
            cp.start()
        for r, (px, py) in enumerate(peers):
            pltpu.make_async_remote_copy(src_ref=in_ref, dst_ref=out_ref.at[2 * px + py], send_sem=send_sems.at[r],
                                         recv_sem=recv_sems.at[r], device_id=(px, py, c), device_id_type=MESH).wait_recv()
        for cp in sends:
            cp.wait_send()
        mine.wait()

    return pl.pallas_call(
        body, name="weights_all_gather",
        in_specs=[pl.BlockSpec(memory_space=pl.ANY)], out_specs=pl.BlockSpec(memory_space=pl.ANY),
        out_shape=jax.ShapeDtypeStruct((N_CHIPS,) + shard.shape, shard.dtype),
        scratch_shapes=[pltpu.SemaphoreType.DMA((3,)), pltpu.SemaphoreType.DMA((3,)), pltpu.SemaphoreType.DMA],
        compiler_params=pltpu.CompilerParams(has_side_effects=True),
    )(shard)


def _scatter_to_chips(slabs):
    _, r_, c_ = slabs.shape

    def body(in_ref, out_ref, send_sems, recv_sems):
        c = lax.axis_index("c")
        _, peers = _chip_peers()
        sends = [pltpu.make_async_remote_copy(src_ref=in_ref.at[2 * px + py], dst_ref=out_ref.at[r], send_sem=send_sems.at[r],
                                              recv_sem=recv_sems.at[r], device_id=(px, py, c), device_id_type=MESH)
                 for r, (px, py) in enumerate(peers)]
        for cp in sends:
            cp.start()
        for cp in sends:
            cp.wait_recv()
        for cp in sends:
            cp.wait_send()

    return pl.pallas_call(
        body, name="grads_scatter",
        in_specs=[pl.BlockSpec(memory_space=pl.ANY)], out_specs=pl.BlockSpec(memory_space=pl.ANY),
        out_shape=jax.ShapeDtypeStruct((3, r_, c_), slabs.dtype),
        scratch_shapes=[pltpu.SemaphoreType.DMA((3,)), pltpu.SemaphoreType.DMA((3,))],
        compiler_params=pltpu.CompilerParams(has_side_effects=True),
    )(slabs)


def _swap_with_sibling(a):
    def body(in_ref, out_ref, send_sem, recv_sem):
        x, y, c = lax.axis_index("x"), lax.axis_index("y"), lax.axis_index("c")
        cp = pltpu.make_async_remote_copy(src_ref=in_ref, dst_ref=out_ref, send_sem=send_sem, recv_sem=recv_sem,
                                          device_id=(x, y, 1 - c), device_id_type=MESH)
        cp.start()
        cp.wait()

    return pl.pallas_call(
        body, name="grads_swap_sibling",
        in_specs=[pl.BlockSpec(memory_space=pl.ANY)], out_specs=pl.BlockSpec(memory_space=pl.ANY),
        out_shape=jax.ShapeDtypeStruct(a.shape, a.dtype),
        scratch_shapes=[pltpu.SemaphoreType.DMA, pltpu.SemaphoreType.DMA],
        compiler_params=pltpu.CompilerParams(has_side_effects=True),
    )(a)


def _all_reduce_small(a):
    def body(in_ref, out_ref, slots, send_sems, recv_sems):
        x, y, c = lax.axis_index("x"), lax.axis_index("y"), lax.axis_index("c")
        me = 4 * x + 2 * y + c
        slots[0] = in_ref[...]
        sends = []
        for r in range(1, N_DEV):
            peer = (x ^ (r >> 2), y ^ ((r >> 1) & 1), c ^ (r & 1))
            sends.append(pltpu.make_async_remote_copy(src_ref=in_ref, dst_ref=slots.at[r], send_sem=send_sems.at[r],
                                                      recv_sem=recv_sems.at[r], device_id=peer, device_id_type=MESH))
        for cp in sends:
            cp.start()
        for cp in sends:
            cp.wait_recv()
        acc = slots[me]
        for dev in range(1, N_DEV):
            acc = acc + slots[dev ^ me]
        out_ref[...] = acc
        for cp in sends:
            cp.wait_send()

    return pl.pallas_call(
        body, name="small_all_reduce",
        in_specs=[pl.BlockSpec(memory_space=pltpu.VMEM)], out_specs=pl.BlockSpec(memory_space=pltpu.VMEM),
        out_shape=jax.ShapeDtypeStruct(a.shape, a.dtype),
        scratch_shapes=[pltpu.VMEM((N_DEV,) + a.shape, a.dtype), pltpu.SemaphoreType.DMA((N_DEV,)),
                        pltpu.SemaphoreType.DMA((N_DEV,))],
        compiler_params=pltpu.CompilerParams(has_side_effects=True),
    )(a)


SHARDED = ["w_in", "mla_w_q_up", "mla_w_kv_up", "w_out", "ffn_w_up", "ffn_w_down", "ple_w_gate", "ple_w_proj",
           "gdn_conv_w", "ffn_conv_w"]
SHARD_AXIS = {"w_in": 1, "mla_w_q_up": 1, "mla_w_kv_up": 1, "w_out": 0, "ffn_w_up": 1, "ffn_w_down": 0,
              "ple_w_gate": 0, "ple_w_proj": 1, "gdn_conv_w": 1, "ffn_conv_w": 1}
SMALL = ["gdn_a_log", "gdn_dt_bias", "gdn_norm_g", "mla_q_norm_g", "mla_kv_norm_g", "ln1_g", "ln1_b", "ffn_conv_b",
         "ple_b_gate", "ln2_g", "ln2_b"]
WEIGHTS = ["w_in", "gdn_conv_w", "gdn_a_log", "gdn_dt_bias", "gdn_norm_g", "mla_q_norm_g", "mla_w_q_up", "mla_kv_norm_g",
           "mla_w_kv_up", "w_out", "ln1_g", "ln1_b", "ffn_w_up", "ffn_conv_w", "ffn_conv_b", "ffn_w_down", "ple_w_gate",
           "ple_b_gate", "ple_w_proj", "ln2_g", "ln2_b"]
PACK_COLS = 1024
PACK_ROW_TILE = 8


def _pack(arrays):
    flat = jnp.concatenate([a.reshape(-1) for a in arrays])
    quantum = PACK_COLS * PACK_ROW_TILE
    padded = -(-flat.shape[0] // quantum) * quantum
    return jnp.pad(flat, (0, padded - flat.shape[0])).reshape(-1, PACK_COLS)


def _unpack(packed, shapes):
    flat = packed.reshape(-1)
    out, off = [], 0
    for shp in shapes:
        n = int(np.prod(shp))
        out.append(flat[off:off + n].reshape(shp))
        off += n
    return out


def kernel(x, p, w_in, gdn_conv_w, gdn_a_log, gdn_dt_bias, gdn_norm_g, mla_q_norm_g, mla_w_q_up, mla_kv_norm_g, mla_w_kv_up, w_out, ln1_g, ln1_b, ffn_w_up, ffn_conv_w, ffn_conv_b, ffn_w_down, ple_w_gate, ple_b_gate, ple_w_proj, ln2_g, ln2_b, loss_target, m_w_in, m_gdn_conv_w, m_gdn_a_log, m_gdn_dt_bias, m_gdn_norm_g, m_mla_q_norm_g, m_mla_w_q_up, m_mla_kv_norm_g, m_mla_w_kv_up, m_w_out, m_ln1_g, m_ln1_b, m_ffn_w_up, m_ffn_conv_w, m_ffn_conv_b, m_ffn_w_down, m_ple_w_gate, m_ple_b_gate, m_ple_w_proj, m_ln2_g, m_ln2_b, v_w_in, v_gdn_conv_w, v_gdn_a_log, v_gdn_dt_bias, v_gdn_norm_g, v_mla_q_norm_g, v_mla_w_q_up, v_mla_kv_norm_g, v_mla_w_kv_up, v_w_out, v_ln1_g, v_ln1_b, v_ffn_w_up, v_ffn_conv_w, v_ffn_conv_b, v_ffn_w_down, v_ple_w_gate, v_ple_b_gate, v_ple_w_proj, v_ln2_g, v_ln2_b):
    given = dict(locals())
    wsh = {n: given[n][0] for n in WEIGHTS}
    msh = {n: given["m_" + n][0] for n in WEIGHTS}
    vsh = {n: given["v_" + n][0] for n in WEIGHTS}
    bl, s, _ = x.shape
    t = bl * s
    xt = x.reshape(t, D_MODEL)
    pt = p.reshape(t, PLE_DIM)
    target = loss_target.reshape(t, D_MODEL)

    shard_shapes = [wsh[n].shape for n in SHARDED]
    gathered = _all_gather_chips(_pack([wsh[n] for n in SHARDED]))
    pieces = [_unpack(gathered[j], shard_shapes) for j in range(N_CHIPS)]
    full = {n: jnp.concatenate([pieces[j][i] for j in range(N_CHIPS)], axis=SHARD_AXIS[n]) for i, n in enumerate(SHARDED)}

    in_cols, q_cols = _w_in_cols(), _w_q_cols()
    w_in_p = _pad_cols(full["w_in"], in_cols).astype(BF16)
    w_q_p = _pad_cols(full["mla_w_q_up"], q_cols).astype(BF16)
    w_kv = full["mla_w_kv_up"].astype(BF16)
    w_o = full["w_out"].astype(BF16)
    w_up = full["ffn_w_up"].astype(BF16)
    w_down = full["ffn_w_down"].astype(BF16)
    w_gate = full["ple_w_gate"].astype(BF16)
    w_proj = full["ple_w_proj"].astype(BF16)
    gconv, fconv = full["gdn_conv_w"], full["ffn_conv_w"]
    row = lambda a: a.reshape(1, -1)
    sc = jnp.zeros((8, 128), F32).at[0, :GDN_HEADS].set(wsh["gdn_a_log"]).at[1, :GDN_HEADS].set(wsh["gdn_dt_bias"])
    norm_g, qg, kvg = row(wsh["gdn_norm_g"]), row(wsh["mla_q_norm_g"]), row(wsh["mla_kv_norm_g"])
    g1, b1, g2, b2 = row(wsh["ln1_g"]), row(wsh["ln1_b"]), row(wsh["ln2_g"]), row(wsh["ln2_b"])
    fbias, bgate = row(wsh["ffn_conv_b"]), row(wsh["ple_b_gate"])

    inv = ROPE_THETA ** (-jnp.arange(0, MLA_ROPE, 2, dtype=F32) / MLA_ROPE)
    ang = jnp.arange(s, dtype=F32)[:, None] * inv[None, :]
    zero = jnp.zeros_like(ang)
    cos_t = jnp.concatenate([jnp.cos(ang), zero, jnp.cos(ang), zero], axis=1)
    sin_t = jnp.concatenate([-jnp.sin(ang), zero, jnp.sin(ang), zero], axis=1)

    proj = _matmul(xt, w_in_p, name="proj")
    cat, o_raw, states = _gdn_fwd(proj, gconv, sc, norm_g, bl, s)
    qf, kvf, kr = _mla_prep_fwd(proj, qg, kvg, w_q_p, w_kv, cos_t, sin_t, s, 256)
    cat = _attn_fwd(qf, kvf, kr, cat, bl, s, 256)
    mix = _matmul(cat, w_o, name="mix")
    r1, h1 = _ln1_fwd(xt, mix, g1, b1, 256)
    u = _matmul(h1, w_up, name="ffn_up")
    act = _ffn_act_fwd(u, fconv, fbias, bl, s, 256)
    ffn = _matmul(act, w_down, name="ffn_down", tk=1408)
    gpre = _matmul(h1, w_gate, name="ple_gate")
    pp = _matmul(pt, w_proj, name="ple_proj")
    dr2, dgpre, dpp, loss_acc, dbgate, dg2, db2 = _head(h1, ffn, gpre, pp, bgate, g2, b2, target, 256)

    dact = _matmul(dr2, w_down, name="d_act", tb=True, tn=256)
    d_w_down = _matmul(act, dr2, name="dw_down", ta=True, tm=256)
    dug, duu, dfcw_g, dfcw_u, dfcb_g, dfcb_u = _ffn_act_bwd(u, fconv, fbias, dact, bl, s, 256)
    du = jnp.concatenate([dug, duu], axis=1)
    dh1_a = _matmul(du, w_up, name="dh1_ffn", tb=True, tk=1408)
    dh1_b = _matmul(dgpre, w_gate, name="dh1_ple", tb=True)
    d_w_up = _matmul(h1, du, name="dw_up", ta=True)
    d_w_gate = _matmul(h1, dgpre, name="dw_gate", ta=True)
    d_w_proj = _matmul(pt, dpp, name="dw_proj", ta=True)
    dr1, dg1, db1 = _ln1_bwd(r1, dr2, dh1_a, dh1_b, g1, b1, 256)
    dcat = _matmul(dr1, w_o, name="d_cat", tb=True)
    d_w_o = _matmul(cat, dr1, name="dw_out", ta=True)
    dproj, dab, dcwq, dcwk, dcwv, dsc, dng = _gdn_bwd(proj, gconv, sc, norm_g, o_raw, states, dcat, bl, s)
    dqf, dkvf, dkr = _attn_bwd(qf, kvf, kr, dcat, bl, s, 256)
    dproj, dqg, dkvg, d_w_q_p, d_w_kv = _mla_prep_bwd(proj, qg, kvg, w_q_p, w_kv, cos_t, sin_t, dqf, dkvf, dkr, dab, dproj, s, 256)
    grad_x = _matmul(dproj, w_in_p, name="d_x", tb=True, add=dr1, add_scale=ALPHA)
    d_w_in_p = _matmul(xt, dproj, name="dw_in", ta=True)

    gfull = {
        "w_in": _unpad_cols(d_w_in_p, in_cols, D_IN),
        "mla_w_q_up": _unpad_cols(d_w_q_p, q_cols, MLA_HEADS * (MLA_NOPE + MLA_ROPE)),
        "mla_w_kv_up": d_w_kv, "w_out": d_w_o, "ffn_w_up": d_w_up, "ffn_w_down": d_w_down,
        "ple_w_gate": d_w_gate, "ple_w_proj": d_w_proj,
        "gdn_conv_w": jnp.concatenate([jnp.sum(dcwq, 0), jnp.sum(dcwk, 0), jnp.sum(dcwv, 0)], axis=1),
        "ffn_conv_w": jnp.concatenate([jnp.sum(dfcw_g, 0), jnp.sum(dfcw_u, 0)], axis=1),
    }
    dsc_sum = jnp.sum(dsc, axis=(0, 1))
    gsmall = {
        "gdn_a_log": dsc_sum[0, :GDN_HEADS], "gdn_dt_bias": dsc_sum[1, :GDN_HEADS],
        "gdn_norm_g": jnp.sum(dng[:, :, 0, :], axis=(0, 1)),
        "mla_q_norm_g": dqg[0], "mla_kv_norm_g": dkvg[0], "ln1_g": dg1[0], "ln1_b": db1[0],
        "ffn_conv_b": jnp.concatenate([jnp.sum(dfcb_g, 0), jnp.sum(dfcb_u, 0)], axis=1)[0],
        "ple_b_gate": dbgate[0], "ln2_g": dg2[0], "ln2_b": db2[0],
    }

    slabs = jnp.stack([_pack([jnp.split(gfull[n], N_CHIPS, axis=SHARD_AXIS[n])[j] for n in SHARDED]) for j in range(N_CHIPS)])
    me_chip = 2 * lax.axis_index("x") + lax.axis_index("y")
    received = _scatter_to_chips(slabs)
    partial = _sum4(lax.dynamic_index_in_dim(slabs, me_chip, 0, keepdims=False), received, 704)
    other = _swap_with_sibling(partial)
    packs = _adamw(partial, other, _pack([wsh[n] for n in SHARDED]), _pack([msh[n] for n in SHARDED]),
                   _pack([vsh[n] for n in SHARDED]), 704, "adamw_sharded")
    big = [dict(zip(SHARDED, _unpack(pk, shard_shapes))) for pk in packs]

    small_shapes = [wsh[n].shape for n in SMALL]
    gsum = _all_reduce_small(_pack([gsmall[n] for n in SMALL]))
    spacks = _adamw(gsum, jnp.zeros_like(gsum), _pack([wsh[n] for n in SMALL]), _pack([msh[n] for n in SMALL]),
                    _pack([vsh[n] for n in SMALL]), 16, "adamw_small")
    small = [dict(zip(SMALL, _unpack(pk, small_shapes))) for pk in spacks]

    loss = lax.psum(loss_acc[0, 0], ("x", "y", "c"))
    outs = [loss, grad_x.reshape(x.shape)]
    for kind in range(4):
        for n in WEIGHTS:
            val = big[kind][n] if n in big[kind] else small[kind][n]
            outs.append(val[None])
    return tuple(outs)
```

```python
import functools
import math

import numpy as np
import jax
import jax.numpy as jnp
from jax import lax
from jax.experimental import pallas as pl
from jax.experimental.pallas import tpu as pltpu

F32 = jnp.float32
BF16 = jnp.bfloat16

D_MODEL = 1024
CHUNK = 64
PLE_DIM = 256
GDN_HEADS = 4
GDN_DK = 128
GDN_DV = 128
GDN_CONV = 4
MLA_HEADS = 4
MLA_NOPE = 128
MLA_ROPE = 64
MLA_V = 128
MLA_Q_LORA = 384
MLA_KV_LORA = 256
ROPE_THETA = 10000.0
D_FF = 2816
FFN_CONV = 3
DEPTH = 1
ALPHA = (2.0 * DEPTH) ** 0.25
NORM_EPS = 1e-6
GDN_QK = GDN_HEADS * GDN_DK
GDN_VW = GDN_HEADS * GDN_DV
D_IN = 2 * GDN_QK + 2 * GDN_VW + 2 * GDN_HEADS + MLA_Q_LORA + MLA_KV_LORA + MLA_ROPE
ATT_SCALE = (MLA_NOPE + MLA_ROPE) ** -0.5

ADAM_LR = 0.001
ADAM_B1 = 0.9
ADAM_B2 = 0.999
ADAM_EPS = 1e-08
ADAM_WD = 0.01
ADAM_STEP = 10

LANES = 128
VMEM_LIMIT = 60 * 1024 * 1024
GDN_FWD_GROUP = 4
GDN_BWD_GROUP = 2
N_CHIPS = 4
N_DEV = 8

P_WIDTH = 3072
P_MLA = 2048
MESH = pl.DeviceIdType.MESH


def _rope_slot(j):
    return j if j < MLA_ROPE // 2 else 64 + (j - MLA_ROPE // 2)


def _w_in_cols():
    idx = -np.ones((P_WIDTH,), np.int64)
    for h in range(GDN_HEADS):
        base = h * 512
        idx[base:base + 128] = np.arange(128) + h * GDN_DK
        idx[base + 128:base + 256] = np.arange(128) + GDN_QK + h * GDN_DK
        idx[base + 256:base + 384] = np.arange(128) + 2 * GDN_QK + h * GDN_DV
        idx[base + 384:base + 512] = np.arange(128) + 2 * GDN_QK + GDN_VW + h * GDN_DV
    o_a = 2 * GDN_QK + 2 * GDN_VW
    idx[P_MLA:P_MLA + 2 * GDN_HEADS] = np.arange(2 * GDN_HEADS) + o_a
    o_cq = o_a + 2 * GDN_HEADS
    idx[P_MLA + 128:P_MLA + 512] = np.arange(MLA_Q_LORA) + o_cq
    o_ckv = o_cq + MLA_Q_LORA
    idx[P_MLA + 512:P_MLA + 768] = np.arange(MLA_KV_LORA) + o_ckv
    o_kr = o_ckv + MLA_KV_LORA
    for j in range(MLA_ROPE):
        idx[P_MLA + 768 + _rope_slot(j)] = o_kr + j
    return idx


def _w_q_cols():
    idx = -np.ones((MLA_HEADS * 256,), np.int64)
    for h in range(MLA_HEADS):
        o = h * (MLA_NOPE + MLA_ROPE)
        idx[h * 256:h * 256 + 128] = np.arange(128) + o
        for j in range(MLA_ROPE):
            idx[h * 256 + 128 + _rope_slot(j)] = o + MLA_NOPE + j
    return idx


def _pad_cols(w, idx):
    safe = np.where(idx >= 0, idx, 0)
    return jnp.where(jnp.asarray(idx >= 0)[None, :], w[:, safe], 0.0)


def _unpad_cols(wp, idx, n):
    inv = np.zeros((n,), np.int64)
    inv[idx[idx >= 0]] = np.nonzero(idx >= 0)[0]
    return wp[:, inv]


def _dot(a, b, ca, cb, precision=None):
    if precision is None:
        a = a.astype(BF16)
        b = b.astype(BF16)
    return lax.dot_general(a, b, (((ca,), (cb,)), ((), ())), preferred_element_type=F32, precision=precision)


@jax.custom_vjp
def mm(a, b):
    return _dot(a, b, 1, 0)


@jax.custom_vjp
def mm_nt(a, b):
    return _dot(a, b, 1, 1)


@jax.custom_vjp
def mm_tn(a, b):
    return _dot(a, b, 0, 0)


mm.defvjp(lambda a, b: (mm(a, b), (a, b)), lambda r, g: (mm_nt(g, r[1]), mm_tn(r[0], g)))
mm_nt.defvjp(lambda a, b: (mm_nt(a, b), (a, b)), lambda r, g: (mm(g, r[1]), mm_tn(g, r[0])))
mm_tn.defvjp(lambda a, b: (mm_tn(a, b), (a, b)), lambda r, g: (mm_nt(r[1], g), mm(r[0], g)))

def _split(a):
    hi = a.astype(BF16)
    return hi, (a - hi.astype(F32)).astype(BF16)


def _dot3(a, b, ca, cb):
    a_hi, a_lo = _split(a)
    b_hi, b_lo = _split(b)
    return (_dot(a_hi, b_hi, ca, cb) + _dot(a_hi, b_lo, ca, cb)) + _dot(a_lo, b_hi, ca, cb)


def _unit_lower_inverse(low):
    n = low.shape[0]
    ii = lax.broadcasted_iota(jnp.int32, (n, n), 0)
    jj = lax.broadcasted_iota(jnp.int32, (n, n), 1)
    inv = jnp.where(ii == jj, 1.0, 0.0) - low
    power = _dot3(low, low, 1, 0)
    k = 2
    while k < n:
        inv = inv + _dot3(inv, power, 1, 0)
        k *= 2
        if k < n:
            power = _dot3(power, power, 1, 0)
    return inv


@jax.custom_vjp
def unit_lower_solve(low, rhs):
    return _dot3(_unit_lower_inverse(low), rhs, 1, 0)


def _uls_fwd(low, rhs):
    inv = _unit_lower_inverse(low)
    x = _dot3(inv, rhs, 1, 0)
    return x, (inv, x)


def _uls_bwd(res, dx):
    inv, x = res
    drhs = _dot3(inv, dx, 0, 0)
    n = inv.shape[0]
    ii = lax.broadcasted_iota(jnp.int32, (n, n), 0)
    jj = lax.broadcasted_iota(jnp.int32, (n, n), 1)
    dlow = jnp.where(ii > jj, -_dot3(drhs, x, 1, 1), 0.0)
    return dlow, drhs


unit_lower_solve.defvjp(_uls_fwd, _uls_bwd)


def _shift_rows(x, s):
    if s == 0:
        return x
    n = x.shape[0]
    row = lax.broadcasted_iota(jnp.int32, x.shape, 0)
    rolled = pltpu.roll(x, s % n, 0)
    keep = (row >= s) if s > 0 else (row < n + s)
    return jnp.where(keep, rolled, 0.0)


def _row(w, j):
    tap = lax.broadcasted_iota(jnp.int32, w.shape, 0)
    return jnp.sum(jnp.where(tap == j, w, 0.0), axis=0, keepdims=True)


@jax.custom_vjp
def dwconv(x, w):
    k = w.shape[0]
    y = _row(w, k - 1) * x
    for j in range(k - 1):
        y = y + _row(w, j) * _shift_rows(x, k - 1 - j)
    return y


def _dwconv_fwd(x, w):
    return dwconv(x, w), (x, w)


def _dwconv_bwd(res, dy):
    x, w = res
    k = w.shape[0]
    dx = _row(w, k - 1) * dy
    tap = lax.broadcasted_iota(jnp.int32, w.shape, 0)
    dw = jnp.where(tap == k - 1, jnp.sum(dy * x, axis=0, keepdims=True), 0.0)
    for j in range(k - 1):
        dx = dx + _row(w, j) * _shift_rows(dy, -(k - 1 - j))
        dw = dw + jnp.where(tap == j, jnp.sum(dy * _shift_rows(x, k - 1 - j), axis=0, keepdims=True), 0.0)
    return dx, dw


dwconv.defvjp(_dwconv_fwd, _dwconv_bwd)


@jax.custom_vjp
def rope128(x, cos, sin):
    return x * cos + pltpu.roll(x, 64, 1) * sin


rope128.defvjp(lambda x, c, s: (rope128(x, c, s), (c, s)),
               lambda r, g: (g * r[0] + pltpu.roll(g * r[1], 64, 1), jnp.zeros_like(r[0]), jnp.zeros_like(r[1])))


def _silu(x):
    return x * jax.nn.sigmoid(x)


def _softplus(x):
    return jnp.maximum(x, 0.0) + jnp.log(1.0 + jnp.exp(-jnp.abs(x)))


def _rmsnorm(x, g):
    return x * lax.rsqrt(jnp.mean(x * x, axis=-1, keepdims=True) + NORM_EPS) * g


def _layernorm(x, g, b):
    mu = jnp.mean(x, axis=-1, keepdims=True)
    xc = x - mu
    var = jnp.mean(xc * xc, axis=-1, keepdims=True)
    return xc * lax.rsqrt(var + NORM_EPS) * g + b


def _pick_lane(row, lane):
    idx = lax.broadcasted_iota(jnp.int32, row.shape, 1)
    return jnp.sum(jnp.where(idx == lane, row, 0.0), axis=1, keepdims=True)


def _gdn_q(pq, cw):
    h = _silu(dwconv(pq, cw))
    return h * lax.rsqrt(jnp.sum(h * h, axis=-1, keepdims=True) + NORM_EPS) * (GDN_DK ** -0.5)


def _gdn_k(pk, cw):
    h = _silu(dwconv(pk, cw))
    return h * lax.rsqrt(jnp.sum(h * h, axis=-1, keepdims=True) + NORM_EPS)


def _gdn_v(pv, cw):
    return _silu(dwconv(pv, cw))


def _gdn_gate(ab, sc, head):
    a = _pick_lane(ab, head)
    b = _pick_lane(ab, GDN_HEADS + head)
    a_log = _pick_lane(_row(sc, 0), head)
    dt_bias = _pick_lane(_row(sc, 1), head)
    beta = jax.nn.sigmoid(b)
    g = -jnp.exp(a_log) * _softplus(a + dt_bias)
    return _two_lanes(g, beta)


def _two_lanes(c0, c1):
    lane = lax.broadcasted_iota(jnp.int32, (c0.shape[0], LANES), 1)
    return jnp.where(lane == 0, c0, jnp.where(lane == 1, c1, 0.0))


def _gdn_chunk_packed(q, k, v, gb, state):
    return _gdn_chunk(q, k, v, _pick_lane(gb, 0), _pick_lane(gb, 1), state)


def _gdn_post(o, z, norm_g):
    return _rmsnorm(o, norm_g) * _silu(z)


def _gdn_chunk(q, k, v, g, beta, state):
    c = q.shape[0]
    ii = lax.broadcasted_iota(jnp.int32, (c, c), 0)
    jj = lax.broadcasted_iota(jnp.int32, (c, c), 1)
    incl = ii >= jj
    g_row = jnp.sum(jnp.where(ii == jj, g, 0.0), axis=0, keepdims=True)
    gc_col = jnp.sum(jnp.where(incl, g_row, 0.0), axis=1, keepdims=True)
    gc_row = jnp.sum(jnp.where(jj >= ii, g, 0.0), axis=0, keepdims=True)
    decay = jnp.where(incl, jnp.exp(jnp.where(incl, gc_col - gc_row, 0.0)), 0.0)
    kb = k * beta
    low = jnp.where(ii > jj, mm_nt(kb, k) * decay, 0.0)
    eg = jnp.exp(gc_col)
    wu = unit_lower_solve(low, jnp.concatenate([kb * eg, v * beta], axis=1))
    w, u = wu[:, :GDN_DK], wu[:, GDN_DK:]
    qk = mm_nt(q, k) * decay
    g_last = jnp.sum(g_row, axis=1, keepdims=True)
    kd = k * jnp.exp(g_last - gc_col)
    v_new = u - mm(w, state)
    o = mm(q * eg, state) + mm(qk, v_new)
    new_state = state * jnp.exp(g_last) + mm_tn(kd, v_new)
    return o, new_state


def _attn_block(q, kn, kr, v, q0):
    s = (mm_nt(q[:, :128], kn) + mm_nt(q[:, 128:], kr)) * ATT_SCALE
    qpos = q0 + lax.broadcasted_iota(jnp.int32, s.shape, 0)
    kpos = lax.broadcasted_iota(jnp.int32, s.shape, 1)
    shift = int(math.log2(CHUNK))
    allowed = (kpos >> shift) <= (qpos >> shift)
    s = jnp.where(allowed, s, -1e30)
    p = jnp.exp(s - jnp.max(s, axis=-1, keepdims=True))
    p = p / jnp.sum(p, axis=-1, keepdims=True)
    return mm(p, v)


def _mla_prep(pm, qg, kvg, wq, wkv, cos, sin):
    cq = pm[:, 128:512]
    ckv = pm[:, 512:768]
    qf = mm(_rmsnorm(cq, qg), wq)
    parts = []
    for h in range(MLA_HEADS):
        parts.append(qf[:, h * 256:h * 256 + 128])
        parts.append(rope128(qf[:, h * 256 + 128:h * 256 + 256], cos, sin))
    kvf = mm(_rmsnorm(ckv, kvg), wkv)
    return jnp.concatenate(parts, axis=1), kvf, rope128(pm[:, 768:896], cos, sin)


def _ffn_act(ug, uu, wg, wu, bg, bu):
    return _silu(dwconv(ug, wg) + bg) * (dwconv(uu, wu) + bu)


def _head_loss(h1, ffn, gpre, pp, bgate, g2, b2, target):
    gate = jax.nn.sigmoid(gpre + bgate)
    h2 = _layernorm(ALPHA * h1 + ffn + gate * pp, g2, b2)
    err = h2 - target
    return 0.5 * jnp.sum(jnp.sum(err * err, axis=1, keepdims=True), axis=0, keepdims=True) / D_MODEL


def _params(sem):
    return pltpu.CompilerParams(dimension_semantics=sem, vmem_limit_bytes=VMEM_LIMIT)


def _matmul(a, b, *, name, ta=False, tb=False, tm=512, tn=512, tk=1024, add=None, add_scale=1.0):
    (k_dim, m) = a.shape if ta else a.shape[::-1]
    (n, k2) = b.shape if tb else b.shape[::-1]
    assert k_dim == k2, (a.shape, b.shape)
    tm, tn, tk = min(tm, m), min(tn, n), min(tk, k_dim)
    assert m % tm == 0 and n % tn == 0 and k_dim % tk == 0, (name, m, n, k_dim, tm, tn, tk)
    nk = k_dim // tk
    ca, cb = (0 if ta else 1), (1 if tb else 0)

    def body(*refs):
        if add is None:
            a_ref, b_ref, o_ref, acc = refs
        else:
            a_ref, b_ref, c_ref, o_ref, acc = refs
        kk = pl.program_id(2)

        @pl.when(kk == 0)
        def _():
            acc[...] = jnp.zeros_like(acc)

        acc[...] += _dot(a_ref[...], b_ref[...], ca, cb)

        @pl.when(kk == nk - 1)
        def _():
            r = acc[...]
            if add is not None:
                r = r + add_scale * c_ref[...]
            o_ref[...] = r

    a_spec = pl.BlockSpec((tk, tm), lambda i, j, k: (k, i)) if ta else pl.BlockSpec((tm, tk), lambda i, j, k: (i, k))
    b_spec = pl.BlockSpec((tn, tk), lambda i, j, k: (j, k)) if tb else pl.BlockSpec((tk, tn), lambda i, j, k: (k, j))
    in_specs = [a_spec, b_spec]
    args = [a, b]
    if add is not None:
        in_specs.append(pl.BlockSpec((tm, tn), lambda i, j, k: (i, j)))
        args.append(add)
    return pl.pallas_call(
        body, name=name, grid=(m // tm, n // tn, nk),
        in_specs=in_specs, out_specs=pl.BlockSpec((tm, tn), lambda i, j, k: (i, j)),
        out_shape=jax.ShapeDtypeStruct((m, n), F32),
        scratch_shapes=[pltpu.VMEM((tm, tn), F32)],
        compiler_params=_params(("parallel", "parallel", "arbitrary")),
    )(*args)


def _gdn_fwd(proj, conv_w, sc, norm_g, bl, s):
    nc = s // CHUNK

    def body(ph_ref, ab_ref, cwq_ref, cwk_ref, cwv_ref, sc_ref, ng_ref, cat_ref, o_ref, st_ref, q_s, k_s, v_s, gb_s):
        q_s[...] = _gdn_q(ph_ref[:, 0:128], cwq_ref[...])
        k_s[...] = _gdn_k(ph_ref[:, 128:256], cwk_ref[...])
        v_s[...] = _gdn_v(ph_ref[:, 256:384], cwv_ref[...])
        gb_s[...] = _gdn_gate(ab_ref[...], sc_ref[...], pl.program_id(1))

        group = math.gcd(nc, GDN_FWD_GROUP)

        def chunks(i, state):
            for j in range(group):
                n = i * group + j
                rows = pl.ds(pl.multiple_of(n * CHUNK, CHUNK), CHUNK)
                st_ref[n] = state
                o, state = _gdn_chunk_packed(q_s[rows, :], k_s[rows, :], v_s[rows, :], gb_s[rows, :], state)
                o_ref[rows, :] = o
            return state

        lax.fori_loop(0, nc // group, chunks, jnp.zeros((GDN_DK, GDN_DV), F32))
        cat_ref[...] = _gdn_post(o_ref[...], ph_ref[:, 384:512], ng_ref[...])

    t = bl * s
    return pl.pallas_call(
        body, name="gdn_fwd", grid=(bl, GDN_HEADS),
        in_specs=[
            pl.BlockSpec((s, 512), lambda b, h: (b, h)),
            pl.BlockSpec((s, 128), lambda b, h: (b, P_MLA // 128)),
            pl.BlockSpec((GDN_CONV, 128), lambda b, h: (0, h)),
            pl.BlockSpec((GDN_CONV, 128), lambda b, h: (0, GDN_HEADS + h)),
            pl.BlockSpec((GDN_CONV, 128), lambda b, h: (0, 2 * GDN_HEADS + h)),
            pl.BlockSpec((8, 128), lambda b, h: (0, 0)),
            pl.BlockSpec((1, 128), lambda b, h: (0, 0)),
        ],
        out_specs=[
            pl.BlockSpec((s, 128), lambda b, h: (b, h)),
            pl.BlockSpec((s, 128), lambda b, h: (b, h)),
            pl.BlockSpec((None, None, nc, GDN_DK, GDN_DV), lambda b, h: (b, h, 0, 0, 0)),
        ],
        out_shape=[
            jax.ShapeDtypeStruct((t, 2 * GDN_VW), F32),
            jax.ShapeDtypeStruct((t, GDN_VW), F32),
            jax.ShapeDtypeStruct((bl, GDN_HEADS, nc, GDN_DK, GDN_DV), F32),
        ],
        scratch_shapes=[pltpu.VMEM((s, 128), F32)] * 4,
        compiler_params=_params(("parallel", "arbitrary")),
    )(proj, proj, conv_w, conv_w, conv_w, sc, norm_g)


def _gdn_bwd(proj, conv_w, sc, norm_g, o_raw, states, dcat, bl, s):
    nc = s // CHUNK

    def body(ph_ref, ab_ref, cwq_ref, cwk_ref, cwv_ref, sc_ref, ng_ref, o_ref, st_ref, dc_ref,
             dph_ref, dab_ref, dcwq_ref, dcwk_ref, dcwv_ref, dsc_ref, dng_ref, q_s, k_s, v_s, gb_s, do_s):
        head = pl.program_id(1)
        gate = functools.partial(_gdn_gate, head=head)
        paths = [(_gdn_q, 0, cwq_ref, q_s, dcwq_ref), (_gdn_k, 128, cwk_ref, k_s, dcwk_ref), (_gdn_v, 256, cwv_ref, v_s, dcwv_ref)]
        for fn, col, cw_ref, val_s, _ in paths:
            val_s[...] = fn(ph_ref[:, col:col + 128], cw_ref[...])
        gb_s[...] = gate(ab_ref[...], sc_ref[...])
        _, post_vjp = jax.vjp(_gdn_post, o_ref[...], ph_ref[:, 384:512], ng_ref[...])
        d_o, dz, dng = post_vjp(dc_ref[...])
        do_s[...] = d_o
        dph_ref[:, 384:512] = dz
        dng_ref[...] = jnp.broadcast_to(dng, dng_ref.shape)

        group = math.gcd(nc, GDN_BWD_GROUP)

        def chunks(i, dstate):
            ns = [nc - 1 - (i * group + j) for j in range(group)]
            rows = [pl.ds(pl.multiple_of(n * CHUNK, CHUNK), CHUNK) for n in ns]
            vjps = [jax.vjp(_gdn_chunk_packed, q_s[r, :], k_s[r, :], v_s[r, :], gb_s[r, :], st_ref[n])[1] for n, r in zip(ns, rows)]
            d_os = [do_s[r, :] for r in rows]
            grads = []
            for vjp, d_o in zip(vjps, d_os):
                dq, dk, dv, dgb, dstate = vjp((d_o, dstate))
                grads.append((dq, dk, dv, dgb))
            for r, (dq, dk, dv, dgb) in zip(rows, grads):
                q_s[r, :], k_s[r, :], v_s[r, :], gb_s[r, :] = dq, dk, dv, dgb
            return dstate

        lax.fori_loop(0, nc // group, chunks, jnp.zeros((GDN_DK, GDN_DV), F32))
        for fn, col, cw_ref, val_s, dcw_ref in paths:
            _, vjp = jax.vjp(fn, ph_ref[:, col:col + 128], cw_ref[...])
            dph_ref[:, col:col + 128], dcw_ref[...] = vjp(val_s[...])
        _, gate_vjp = jax.vjp(gate, ab_ref[...], sc_ref[...])
        dab, dsc_ref[...] = gate_vjp(gb_s[...])

        @pl.when(head == 0)
        def _():
            dab_ref[...] = jnp.zeros_like(dab_ref)

        dab_ref[...] += dab

    t = bl * s
    cw_out = pl.BlockSpec((None, GDN_CONV, 128), lambda b, h: (b, 0, h))
    part = pl.BlockSpec((None, None, 8, 128), lambda b, h: (b, h, 0, 0))
    return pl.pallas_call(
        body, name="gdn_bwd", grid=(bl, GDN_HEADS),
        in_specs=[
            pl.BlockSpec((s, 512), lambda b, h: (b, h)),
            pl.BlockSpec((s, 128), lambda b, h: (b, P_MLA // 128)),
            pl.BlockSpec((GDN_CONV, 128), lambda b, h: (0, h)),
            pl.BlockSpec((GDN_CONV, 128), lambda b, h: (0, GDN_HEADS + h)),
            pl.BlockSpec((GDN_CONV, 128), lambda b, h: (0, 2 * GDN_HEADS + h)),
            pl.BlockSpec((8, 128), lambda b, h: (0, 0)),
            pl.BlockSpec((1, 128), lambda b, h: (0, 0)),
            pl.BlockSpec((s, 128), lambda b, h: (b, h)),
            pl.BlockSpec((None, None, nc, GDN_DK, GDN_DV), lambda b, h: (b, h, 0, 0, 0)),
            pl.BlockSpec((s, 128), lambda b, h: (b, h)),
        ],
        out_specs=[
            pl.BlockSpec((s, 512), lambda b, h: (b, h)),
            pl.BlockSpec((s, 128), lambda b, h: (b, 0)),
            cw_out, cw_out, cw_out, part, part,
        ],
        out_shape=[
            jax.ShapeDtypeStruct((t, P_WIDTH), F32),
            jax.ShapeDtypeStruct((t, 128), F32),
            jax.ShapeDtypeStruct((bl, GDN_CONV, 512), F32),
            jax.ShapeDtypeStruct((bl, GDN_CONV, 512), F32),
            jax.ShapeDtypeStruct((bl, GDN_CONV, 512), F32),
            jax.ShapeDtypeStruct((bl, GDN_HEADS, 8, 128), F32),
            jax.ShapeDtypeStruct((bl, GDN_HEADS, 8, 128), F32),
        ],
        scratch_shapes=[pltpu.VMEM((s, 128), F32)] * 5,
        compiler_params=_params(("parallel", "arbitrary")),
    )(proj, proj, conv_w, conv_w, conv_w, sc, norm_g, o_raw, states, dcat)


def _mla_prep_fwd(proj, qg, kvg, wq, wkv, cos, sin, s, tm):
    t = proj.shape[0]
    tm = min(tm, s)
    nps = s // tm
    const = lambda shape: pl.BlockSpec(shape, lambda i: (0, 0))

    def body(pm_ref, qg_ref, kvg_ref, wq_ref, wkv_ref, cos_ref, sin_ref, qf_ref, kvf_ref, kr_ref):
        qf, kvf, kr = _mla_prep(pm_ref[...], qg_ref[...], kvg_ref[...], wq_ref[...], wkv_ref[...], cos_ref[...], sin_ref[...])
        qf_ref[...], kvf_ref[...], kr_ref[...] = qf, kvf, kr

    return pl.pallas_call(
        body, name="mla_prep_fwd", grid=(t // tm,),
        in_specs=[
            pl.BlockSpec((tm, 1024), lambda i: (i, P_MLA // 1024)),
            const((1, MLA_Q_LORA)), const((1, MLA_KV_LORA)), const(wq.shape), const(wkv.shape),
            pl.BlockSpec((tm, 128), lambda i: (i % nps, 0)), pl.BlockSpec((tm, 128), lambda i: (i % nps, 0)),
        ],
        out_specs=[pl.BlockSpec((tm, 1024), lambda i: (i, 0)), pl.BlockSpec((tm, 1024), lambda i: (i, 0)),
                   pl.BlockSpec((tm, 128), lambda i: (i, 0))],
        out_shape=[jax.ShapeDtypeStruct((t, 1024), F32), jax.ShapeDtypeStruct((t, 1024), F32),
                   jax.ShapeDtypeStruct((t, 128), F32)],
        compiler_params=_params(("parallel",)),
    )(proj, qg, kvg, wq, wkv, cos, sin)


def _mla_prep_bwd(proj, qg, kvg, wq, wkv, cos, sin, dqf, dkvf, dkr, dab, dproj, s, tm):
    t = proj.shape[0]
    tm = min(tm, s)
    nps = s // tm
    const = lambda shape: pl.BlockSpec(shape, lambda i: (0, 0))

    def body(pm_ref, qg_ref, kvg_ref, wq_ref, wkv_ref, cos_ref, sin_ref, dqf_ref, dkvf_ref, dkr_ref, dab_ref, dp_in,
             dp_ref, dqg_ref, dkvg_ref, dwq_ref, dwkv_ref):
        del dp_in
        fn = lambda pm, qg_, kvg_, wq_, wkv_: _mla_prep(pm, qg_, kvg_, wq_, wkv_, cos_ref[...], sin_ref[...])
        _, vjp = jax.vjp(fn, pm_ref[...], qg_ref[...], kvg_ref[...], wq_ref[...].astype(F32), wkv_ref[...].astype(F32))
        dpm, dqg, dkvg, dwq, dwkv = vjp((dqf_ref[...], dkvf_ref[...], dkr_ref[...]))
        dp_ref[...] = jnp.concatenate([dab_ref[...], dpm[:, 128:]], axis=1)

        @pl.when(pl.program_id(0) == 0)
        def _():
            dqg_ref[...] = jnp.zeros_like(dqg_ref)
            dkvg_ref[...] = jnp.zeros_like(dkvg_ref)
            dwq_ref[...] = jnp.zeros_like(dwq_ref)
            dwkv_ref[...] = jnp.zeros_like(dwkv_ref)

        dqg_ref[...] += dqg
        dkvg_ref[...] += dkvg
        dwq_ref[...] += dwq
        dwkv_ref[...] += dwkv

    rows = lambda w: pl.BlockSpec((tm, w), lambda i: (i, 0))
    return pl.pallas_call(
        body, name="mla_prep_bwd", grid=(t // tm,),
        in_specs=[
            pl.BlockSpec((tm, 1024), lambda i: (i, P_MLA // 1024)),
            const((1, MLA_Q_LORA)), const((1, MLA_KV_LORA)), const(wq.shape), const(wkv.shape),
            pl.BlockSpec((tm, 128), lambda i: (i % nps, 0)), pl.BlockSpec((tm, 128), lambda i: (i % nps, 0)),
            rows(1024), rows(1024), rows(128), rows(128),
            pl.BlockSpec(memory_space=pl.ANY),
        ],
        out_specs=[pl.BlockSpec((tm, 1024), lambda i: (i, P_MLA // 1024)),
                   const((1, MLA_Q_LORA)), const((1, MLA_KV_LORA)), const(wq.shape), const(wkv.shape)],
        out_shape=[jax.ShapeDtypeStruct(dproj.shape, F32),
                   jax.ShapeDtypeStruct((1, MLA_Q_LORA), F32), jax.ShapeDtypeStruct((1, MLA_KV_LORA), F32),
                   jax.ShapeDtypeStruct(wq.shape, F32), jax.ShapeDtypeStruct(wkv.shape, F32)],
        input_output_aliases={11: 0},
        compiler_params=_params(("arbitrary",)),
    )(proj, qg, kvg, wq, wkv, cos, sin, dqf, dkvf, dkr, dab, dproj)


def _attn_fwd(qf, kvf, kr, cat, bl, s, tq):
    tq = min(tq, s)
    nq = s // tq

    def body(q_ref, kv_ref, kr_ref, cat_in, o_ref):
        del cat_in
        q0 = pl.program_id(2) * tq
        o_ref[...] = _attn_block(q_ref[...], kv_ref[:, 0:128], kr_ref[...], kv_ref[:, 128:256], q0)

    return pl.pallas_call(
        body, name="attn_fwd", grid=(bl, MLA_HEADS, nq),
        in_specs=[
            pl.BlockSpec((tq, 256), lambda b, h, i: (b * nq + i, h)),
            pl.BlockSpec((s, 256), lambda b, h, i: (b, h)),
            pl.BlockSpec((s, 128), lambda b, h, i: (b, 0)),
            pl.BlockSpec(memory_space=pl.ANY),
        ],
        out_specs=pl.BlockSpec((tq, 128), lambda b, h, i: (b * nq + i, GDN_HEADS + h)),
        out_shape=jax.ShapeDtypeStruct(cat.shape, F32),
        input_output_aliases={3: 0},
        compiler_params=_params(("parallel", "parallel", "parallel")),
    )(qf, kvf, kr, cat)


def _attn_bwd(qf, kvf, kr, dcat, bl, s, tq):
    tq = min(tq, s)
    nq = s // tq

    def body(q_ref, kv_ref, kr_ref, do_ref, dq_ref, dkv_ref, dkr_ref):
        h, i = pl.program_id(1), pl.program_id(2)
        fn = functools.partial(_attn_block, q0=i * tq)
        _, vjp = jax.vjp(fn, q_ref[...], kv_ref[:, 0:128], kr_ref[...], kv_ref[:, 128:256])
        dq, dkn, dkr, dv = vjp(do_ref[...])
        dq_ref[...] = dq

        @pl.when(i == 0)
        def _():
            dkv_ref[...] = jnp.zeros_like(dkv_ref)

        @pl.when((i == 0) & (h == 0))
        def _():
            dkr_ref[...] = jnp.zeros_like(dkr_ref)

        dkv_ref[...] += jnp.concatenate([dkn, dv], axis=1)
        dkr_ref[...] += dkr

    t = bl * s
    return pl.pallas_call(
        body, name="attn_bwd", grid=(bl, MLA_HEADS, nq),
        in_specs=[
            pl.BlockSpec((tq, 256), lambda b, h, i: (b * nq + i, h)),
            pl.BlockSpec((s, 256), lambda b, h, i: (b, h)),
            pl.BlockSpec((s, 128), lambda b, h, i: (b, 0)),
            pl.BlockSpec((tq, 128), lambda b, h, i: (b * nq + i, GDN_HEADS + h)),
        ],
        out_specs=[
            pl.BlockSpec((tq, 256), lambda b, h, i: (b * nq + i, h)),
            pl.BlockSpec((s, 256), lambda b, h, i: (b, h)),
            pl.BlockSpec((s, 128), lambda b, h, i: (b, 0)),
        ],
        out_shape=[jax.ShapeDtypeStruct((t, 1024), F32), jax.ShapeDtypeStruct((t, 1024), F32),
                   jax.ShapeDtypeStruct((t, 128), F32)],
        compiler_params=_params(("parallel", "arbitrary", "arbitrary")),
    )(qf, kvf, kr, dcat)


def _ln1_fwd(x, mix, g, b, tm):
    t = x.shape[0]
    tm = min(tm, t)

    def body(x_ref, mix_ref, g_ref, b_ref, r_ref, h_ref):
        r = ALPHA * x_ref[...] + mix_ref[...]
        r_ref[...] = r
        h_ref[...] = _layernorm(r, g_ref[...], b_ref[...])

    rows = pl.BlockSpec((tm, D_MODEL), lambda i: (i, 0))
    vec = pl.BlockSpec((1, D_MODEL), lambda i: (0, 0))
    return pl.pallas_call(
        body, name="ln1_fwd", grid=(t // tm,), in_specs=[rows, rows, vec, vec], out_specs=[rows, rows],
        out_shape=[jax.ShapeDtypeStruct(x.shape, F32)] * 2, compiler_params=_params(("parallel",)),
    )(x, mix, g, b)


def _ln1_bwd(r1, dr2, da, db_, g, b, tm):
    t = r1.shape[0]
    tm = min(tm, t)

    def body(r_ref, d2_ref, da_ref, db_ref, g_ref, b_ref, dr_ref, dg_ref, dbias_ref):
        dh = ALPHA * d2_ref[...] + da_ref[...] + db_ref[...]
        _, vjp = jax.vjp(_layernorm, r_ref[...], g_ref[...], b_ref[...])
        dr, dg, dbias = vjp(dh)
        dr_ref[...] = dr

        @pl.when(pl.program_id(0) == 0)
        def _():
            dg_ref[...] = jnp.zeros_like(dg_ref)
            dbias_ref[...] = jnp.zeros_like(dbias_ref)

        dg_ref[...] += dg
        dbias_ref[...] += dbias

    rows = pl.BlockSpec((tm, D_MODEL), lambda i: (i, 0))
    vec = pl.BlockSpec((1, D_MODEL), lambda i: (0, 0))
    return pl.pallas_call(
        body, name="ln1_bwd", grid=(t // tm,), in_specs=[rows] * 4 + [vec, vec], out_specs=[rows, vec, vec],
        out_shape=[jax.ShapeDtypeStruct(r1.shape, F32)] + [jax.ShapeDtypeStruct((1, D_MODEL), F32)] * 2,
        compiler_params=_params(("arbitrary",)),
    )(r1, dr2, da, db_, g, b)


def _ffn_act_fwd(u, conv_w, conv_b, bl, s, cb):
    nj = D_FF // cb

    def body(ug_ref, uu_ref, wg_ref, wu_ref, bg_ref, bu_ref, act_ref):
        act_ref[...] = _ffn_act(ug_ref[...], uu_ref[...], wg_ref[...], wu_ref[...], bg_ref[...], bu_ref[...])

    return pl.pallas_call(
        body, name="ffn_act_fwd", grid=(bl, nj),
        in_specs=[
            pl.BlockSpec((s, cb), lambda b, j: (b, j)), pl.BlockSpec((s, cb), lambda b, j: (b, nj + j)),
            pl.BlockSpec((FFN_CONV, cb), lambda b, j: (0, j)), pl.BlockSpec((FFN_CONV, cb), lambda b, j: (0, nj + j)),
            pl.BlockSpec((1, cb), lambda b, j: (0, j)), pl.BlockSpec((1, cb), lambda b, j: (0, nj + j)),
        ],
        out_specs=pl.BlockSpec((s, cb), lambda b, j: (b, j)),
        out_shape=jax.ShapeDtypeStruct((bl * s, D_FF), F32),
        compiler_params=_params(("parallel", "parallel")),
    )(u, u, conv_w, conv_w, conv_b, conv_b)


def _ffn_act_bwd(u, conv_w, conv_b, dact, bl, s, cb):
    nj = D_FF // cb

    def body(ug_ref, uu_ref, wg_ref, wu_ref, bg_ref, bu_ref, da_ref, dug_ref, duu_ref, dwg_ref, dwu_ref, dbg_ref, dbu_ref):
        _, vjp = jax.vjp(_ffn_act, ug_ref[...], uu_ref[...], wg_ref[...], wu_ref[...], bg_ref[...], bu_ref[...])
        dug, duu, dwg, dwu, dbg, dbu = vjp(da_ref[...])
        dug_ref[...], duu_ref[...], dwg_ref[...], dwu_ref[...], dbg_ref[...], dbu_ref[...] = dug, duu, dwg, dwu, dbg, dbu

    t = bl * s
    blk = pl.BlockSpec((s, cb), lambda b, j: (b, j))
    wpart = pl.BlockSpec((None, FFN_CONV, cb), lambda b, j: (b, 0, j))
    bpart = pl.BlockSpec((None, 1, cb), lambda b, j: (b, 0, j))
    return pl.pallas_call(
        body, name="ffn_act_bwd", grid=(bl, nj),
        in_specs=[
            blk, pl.BlockSpec((s, cb), lambda b, j: (b, nj + j)),
            pl.BlockSpec((FFN_CONV, cb), lambda b, j: (0, j)), pl.BlockSpec((FFN_CONV, cb), lambda b, j: (0, nj + j)),
            pl.BlockSpec((1, cb), lambda b, j: (0, j)), pl.BlockSpec((1, cb), lambda b, j: (0, nj + j)),
            blk,
        ],
        out_specs=[blk, blk, wpart, wpart, bpart, bpart],
        out_shape=[jax.ShapeDtypeStruct((t, D_FF), F32)] * 2 + [jax.ShapeDtypeStruct((bl, FFN_CONV, D_FF), F32)] * 2
        + [jax.ShapeDtypeStruct((bl, 1, D_FF), F32)] * 2,
        compiler_params=_params(("parallel", "parallel")),
    )(u, u, conv_w, conv_w, conv_b, conv_b, dact)


def _head(h1, ffn, gpre, pp, bgate, g2, b2, target, tm):
    t = h1.shape[0]
    tm = min(tm, t)

    def body(h1_ref, ffn_ref, gp_ref, pp_ref, bg_ref, g2_ref, b2_ref, tg_ref,
             dr_ref, dgp_ref, dpp_ref, loss_ref, dbg_ref, dg2_ref, db2_ref):
        fn = functools.partial(_head_loss, target=tg_ref[...])
        loss, vjp = jax.vjp(fn, h1_ref[...], ffn_ref[...], gp_ref[...], pp_ref[...], bg_ref[...], g2_ref[...], b2_ref[...])
        _, dffn, dgp, dpp, dbg, dg2, db2 = vjp(jnp.ones((1, 1), F32))
        dr_ref[...], dgp_ref[...], dpp_ref[...] = dffn, dgp, dpp

        @pl.when(pl.program_id(0) == 0)
        def _():
            loss_ref[...] = jnp.zeros_like(loss_ref)
            dbg_ref[...] = jnp.zeros_like(dbg_ref)
            dg2_ref[...] = jnp.zeros_like(dg2_ref)
            db2_ref[...] = jnp.zeros_like(db2_ref)

        loss_ref[...] += jnp.broadcast_to(loss, loss_ref.shape)
        dbg_ref[...] += dbg
        dg2_ref[...] += dg2
        db2_ref[...] += db2

    rows = pl.BlockSpec((tm, D_MODEL), lambda i: (i, 0))
    vec = pl.BlockSpec((1, D_MODEL), lambda i: (0, 0))
    return pl.pallas_call(
        body, name="head", grid=(t // tm,), in_specs=[rows] * 4 + [vec] * 3 + [rows],
        out_specs=[rows] * 3 + [pl.BlockSpec((8, 128), lambda i: (0, 0))] + [vec] * 3,
        out_shape=[jax.ShapeDtypeStruct(h1.shape, F32)] * 3 + [jax.ShapeDtypeStruct((8, 128), F32)]
        + [jax.ShapeDtypeStruct((1, D_MODEL), F32)] * 3,
        compiler_params=_params(("arbitrary",)),
    )(h1, ffn, gpre, pp, bgate, g2, b2, target)


def _adam_update(g, w_ref, m_ref, v_ref, g_ref, d_ref, nm_ref, nv_ref):
    m2 = ADAM_B1 * m_ref[...] + (1.0 - ADAM_B1) * g
    v2 = ADAM_B2 * v_ref[...] + (1.0 - ADAM_B2) * jnp.square(g)
    m_hat = m2 / (1.0 - ADAM_B1 ** ADAM_STEP)
    v_hat = v2 / (1.0 - ADAM_B2 ** ADAM_STEP)
    g_ref[...] = g
    d_ref[...] = -ADAM_LR * (m_hat / (jnp.sqrt(v_hat) + ADAM_EPS) + ADAM_WD * w_ref[...])
    nm_ref[...] = m2
    nv_ref[...] = v2


def _row_tile(rows, cols, limit_bytes=256 * 1024):
    best = None
    for t in range(8, rows + 1, 8):
        if rows % t == 0 and t * cols * 4 <= limit_bytes:
            best = t
    return best or rows


def _adamw_reduced(own, recv, w, m, v, name):
    a, b = w.shape
    ta = _row_tile(a, b)

    def body(own_ref, recv_ref, w_ref, m_ref, v_ref, g_ref, d_ref, nm_ref, nv_ref):
        c = lax.axis_index("c")
        for core in range(2):
            @pl.when(c == core)
            def _():
                same = [own_ref[...], recv_ref[0], recv_ref[1], recv_ref[2]]
                other = [recv_ref[3], recv_ref[4], recv_ref[5], recv_ref[6]]
                core0, core1 = (same, other) if core == 0 else (other, same)
                g = core0[0] + core1[0]
                for r in range(1, N_CHIPS):
                    g = (g + core0[r]) + core1[r]
                _adam_update(g, w_ref, m_ref, v_ref, g_ref, d_ref, nm_ref, nv_ref)

    blk = pl.BlockSpec((ta, b), lambda i: (i, 0))
    return pl.pallas_call(
        body, name=name, grid=(a // ta,),
        in_specs=[blk, pl.BlockSpec((7, ta, b), lambda i: (0, i, 0)), blk, blk, blk], out_specs=[blk] * 4,
        out_shape=[jax.ShapeDtypeStruct(w.shape, F32)] * 4, compiler_params=_params(("parallel",)),
    )(own, recv, w, m, v)


def _adamw_small(g, w, m, v):
    def body(g_in, w_ref, m_ref, v_ref, g_ref, d_ref, nm_ref, nv_ref):
        _adam_update(g_in[...], w_ref, m_ref, v_ref, g_ref, d_ref, nm_ref, nv_ref)

    blk = pl.BlockSpec(w.shape, lambda i: (0, 0))
    return pl.pallas_call(
        body, name="adamw_small", grid=(1,), in_specs=[blk] * 4, out_specs=[blk] * 4,
        out_shape=[jax.ShapeDtypeStruct(w.shape, F32)] * 4, compiler_params=_params(("arbitrary",)),
    )(g, w, m, v)


def _remote(src, dst, send_sem, recv_sem, device):
    return pltpu.make_async_remote_copy(src_ref=src, dst_ref=dst, send_sem=send_sem, recv_sem=recv_sem,
                                        device_id=device, device_id_type=MESH)


def _place():
    x, y, c = lax.axis_index("x"), lax.axis_index("y"), lax.axis_index("c")
    return x, y, c, 2 * x + y, [(1 - x, y), (x, 1 - y), (1 - x, 1 - y)]


HBM_REF = pl.BlockSpec(memory_space=pl.ANY)
HALF_ROWS_QUANTUM = 16


def _gather_sems(n):
    return [pltpu.SemaphoreType.DMA((3 * n,))] * 4 + [pltpu.SemaphoreType.DMA((n,))]


def _gather_copies(ins, outs, sems):
    send_s, recv_s, fsend_s, frecv_s, local_s = sems
    x, y, c, me, chips = _place()
    local, sends, steps = [], [], []
    for i, (src, dst) in enumerate(zip(ins, outs)):
        local.append(pltpu.make_async_copy(src, dst.at[me], local_s.at[i]))
        half = src.shape[0] // 2
        split = src.shape[0] % (2 * HALF_ROWS_QUANTUM) == 0
        if split:
            mine = pl.ds(pl.multiple_of(c * half, HALF_ROWS_QUANTUM), half)
            theirs = pl.ds(pl.multiple_of((1 - c) * half, HALF_ROWS_QUANTUM), half)
        for r, (px, py) in enumerate(chips):
            k, peer = 3 * i + r, 2 * px + py
            if split:
                sends.append(_remote(src.at[mine], dst.at[me, mine], send_s.at[k], recv_s.at[k], (px, py, c)))
                landed = dst.at[peer, mine]
                steps.append((_remote(src.at[mine], landed, send_s.at[k], recv_s.at[k], (px, py, c)),
                              _remote(landed, landed, fsend_s.at[k], frecv_s.at[k], (x, y, 1 - c)),
                              _remote(dst.at[peer, theirs], dst.at[peer, theirs], fsend_s.at[k], frecv_s.at[k], (x, y, 1 - c))))
            else:
                sends.append(_remote(src, dst.at[me], send_s.at[k], recv_s.at[k], (px, py, c)))
                steps.append((_remote(src, dst.at[peer], send_s.at[k], recv_s.at[k], (px, py, c)), None, None))
    return local, sends, steps


def _scatter_sems(n):
    return [pltpu.SemaphoreType.DMA((4 * n,))] * 2 + [pltpu.SemaphoreType.DMA((3 * n,))] * 2


def _scatter_copies(ins, outs, sems):
    send_s, recv_s, fsend_s, frecv_s = sems
    x, y, c, me, chips = _place()
    sends, steps = [], []
    for i, (src, dst) in enumerate(zip(ins, outs)):
        for r, (px, py) in enumerate(chips):
            k = 4 * i + r
            cp = _remote(src.at[2 * px + py], dst.at[r], send_s.at[k], recv_s.at[k], (px, py, c))
            fwd = _remote(dst.at[r], dst.at[4 + r], fsend_s.at[3 * i + r], frecv_s.at[3 * i + r], (x, y, 1 - c))
            sends.append(cp)
            steps.append((cp, fwd, fwd))
        k = 4 * i + 3
        cp = _remote(src.at[me], dst.at[3], send_s.at[k], recv_s.at[k], (x, y, 1 - c))
        sends.append(cp)
        steps.append((cp, None, None))
    return [], sends, steps


def _exchange_start(plan):
    local, sends, _ = plan
    for cp in local + sends:
        cp.start()


def _exchange_pass_on(plan):
    for arrival, pass_on, _ in plan[2]:
        arrival.wait_recv()
        if pass_on is not None:
            pass_on.start()


def _exchange_finish(plan):
    local, sends, steps = plan
    for _, pass_on, passed in steps:
        if pass_on is not None:
            passed.wait_recv()
    for cp in sends:
        cp.wait_send()
    for _, pass_on, _ in steps:
        if pass_on is not None:
            pass_on.wait_send()
    for cp in local:
        cp.wait()


def _exchange_call(arrays, copies, sems, out_shapes, name):
    n = len(arrays)

    def body(*refs):
        plan = copies(refs[:n], refs[n:2 * n], refs[2 * n:])
        _exchange_start(plan)
        _exchange_pass_on(plan)
        _exchange_finish(plan)

    return pl.pallas_call(
        body, name=name, in_specs=[HBM_REF] * n, out_specs=[HBM_REF] * n, out_shape=out_shapes,
        scratch_shapes=sems(n), compiler_params=pltpu.CompilerParams(has_side_effects=True),
    )(*arrays)


def _gather_call(shards, name):
    shapes = [jax.ShapeDtypeStruct((N_CHIPS,) + a.shape, a.dtype) for a in shards]
    return _exchange_call(shards, _gather_copies, _gather_sems, shapes, name)


def _scatter_call(slabs, name):
    shapes = [jax.ShapeDtypeStruct((N_DEV - 1,) + a.shape[1:], a.dtype) for a in slabs]
    return _exchange_call(slabs, _scatter_copies, _scatter_sems, shapes, name)


def _all_reduce_small(a):
    def body(in_ref, out_ref, slots, send_sems, recv_sems):
        x, y, c = lax.axis_index("x"), lax.axis_index("y"), lax.axis_index("c")
        me = 4 * x + 2 * y + c
        slots[0] = in_ref[...]
        sends = []
        for r in range(1, N_DEV):
            peer = (x ^ (r >> 2), y ^ ((r >> 1) & 1), c ^ (r & 1))
            sends.append(pltpu.make_async_remote_copy(src_ref=in_ref, dst_ref=slots.at[r], send_sem=send_sems.at[r],
                                                      recv_sem=recv_sems.at[r], device_id=peer, device_id_type=MESH))
        for cp in sends:
            cp.start()
        for cp in sends:
            cp.wait_recv()
        acc = slots[me]
        for dev in range(1, N_DEV):
            acc = acc + slots[dev ^ me]
        out_ref[...] = acc
        for cp in sends:
            cp.wait_send()

    return pl.pallas_call(
        body, name="small_all_reduce",
        in_specs=[pl.BlockSpec(memory_space=pltpu.VMEM)], out_specs=pl.BlockSpec(memory_space=pltpu.VMEM),
        out_shape=jax.ShapeDtypeStruct(a.shape, a.dtype),
        scratch_shapes=[pltpu.VMEM((N_DEV,) + a.shape, a.dtype), pltpu.SemaphoreType.DMA((N_DEV,)),
                        pltpu.SemaphoreType.DMA((N_DEV,))],
        compiler_params=pltpu.CompilerParams(has_side_effects=True),
    )(a)


SHARDED = ["w_in", "mla_w_q_up", "mla_w_kv_up", "w_out", "ffn_w_up", "ffn_w_down", "ple_w_gate", "ple_w_proj",
           "gdn_conv_w", "ffn_conv_w"]
SHARD_AXIS = {"w_in": 1, "mla_w_q_up": 1, "mla_w_kv_up": 1, "w_out": 0, "ffn_w_up": 1, "ffn_w_down": 0,
              "ple_w_gate": 0, "ple_w_proj": 1, "gdn_conv_w": 1, "ffn_conv_w": 1}
SMALL = ["gdn_a_log", "gdn_dt_bias", "gdn_norm_g", "mla_q_norm_g", "mla_kv_norm_g", "ln1_g", "ln1_b", "ffn_conv_b",
         "ple_b_gate", "ln2_g", "ln2_b"]
WEIGHTS = ["w_in", "gdn_conv_w", "gdn_a_log", "gdn_dt_bias", "gdn_norm_g", "mla_q_norm_g", "mla_w_q_up", "mla_kv_norm_g",
           "mla_w_kv_up", "w_out", "ln1_g", "ln1_b", "ffn_w_up", "ffn_conv_w", "ffn_conv_b", "ffn_w_down", "ple_w_gate",
           "ple_b_gate", "ple_w_proj", "ln2_g", "ln2_b"]
F32_ON_WIRE = ("gdn_conv_w", "ffn_conv_w")
GATHER_EARLY = ["w_in", "gdn_conv_w", "mla_w_q_up", "mla_w_kv_up"]
GATHER_LATE = ["w_out", "ffn_w_up", "ffn_conv_w", "ffn_w_down", "ple_w_gate", "ple_w_proj"]
SCATTER_EARLY = ["ffn_w_up", "ffn_conv_w", "ffn_w_down", "ple_w_gate", "ple_w_proj", "w_out"]
SCATTER_LATE = ["w_in", "gdn_conv_w", "mla_w_q_up", "mla_w_kv_up"]
PACK_COLS = 1024
PACK_ROW_TILE = 8


def _join_blocks(blocks, axis):
    n, a, b = blocks.shape
    if axis == 0:
        return blocks.reshape(n * a, b)
    return jnp.transpose(blocks, (1, 0, 2)).reshape(a, n * b)


def _split_blocks(full, axis):
    if axis == 0:
        return full.reshape(N_CHIPS, full.shape[0] // N_CHIPS, full.shape[1])
    a, nb = full.shape
    return jnp.transpose(full.reshape(a, N_CHIPS, nb // N_CHIPS), (1, 0, 2))


def _pack(arrays):
    flat = jnp.concatenate([a.reshape(-1) for a in arrays])
    quantum = PACK_COLS * PACK_ROW_TILE
    padded = -(-flat.shape[0] // quantum) * quantum
    return jnp.pad(flat, (0, padded - flat.shape[0])).reshape(-1, PACK_COLS)


def _unpack(packed, shapes):
    flat = packed.reshape(-1)
    out, off = [], 0
    for shp in shapes:
        n = int(np.prod(shp))
        out.append(flat[off:off + n].reshape(shp))
        off += n
    return out


def kernel(x, p, w_in, gdn_conv_w, gdn_a_log, gdn_dt_bias, gdn_norm_g, mla_q_norm_g, mla_w_q_up, mla_kv_norm_g, mla_w_kv_up, w_out, ln1_g, ln1_b, ffn_w_up, ffn_conv_w, ffn_conv_b, ffn_w_down, ple_w_gate, ple_b_gate, ple_w_proj, ln2_g, ln2_b, loss_target, m_w_in, m_gdn_conv_w, m_gdn_a_log, m_gdn_dt_bias, m_gdn_norm_g, m_mla_q_norm_g, m_mla_w_q_up, m_mla_kv_norm_g, m_mla_w_kv_up, m_w_out, m_ln1_g, m_ln1_b, m_ffn_w_up, m_ffn_conv_w, m_ffn_conv_b, m_ffn_w_down, m_ple_w_gate, m_ple_b_gate, m_ple_w_proj, m_ln2_g, m_ln2_b, v_w_in, v_gdn_conv_w, v_gdn_a_log, v_gdn_dt_bias, v_gdn_norm_g, v_mla_q_norm_g, v_mla_w_q_up, v_mla_kv_norm_g, v_mla_w_kv_up, v_w_out, v_ln1_g, v_ln1_b, v_ffn_w_up, v_ffn_conv_w, v_ffn_conv_b, v_ffn_w_down, v_ple_w_gate, v_ple_b_gate, v_ple_w_proj, v_ln2_g, v_ln2_b):
    given = dict(locals())
    wsh = {n: given[n][0] for n in WEIGHTS}
    msh = {n: given["m_" + n][0] for n in WEIGHTS}
    vsh = {n: given["v_" + n][0] for n in WEIGHTS}
    bl, s, _ = x.shape
    t = bl * s
    xt = x.reshape(t, D_MODEL)
    pt = p.reshape(t, PLE_DIM)
    target = loss_target.reshape(t, D_MODEL)

    wire = lambda n: wsh[n] if n in F32_ON_WIRE else wsh[n].astype(BF16)
    gathered = dict(zip(GATHER_EARLY, _gather_call([wire(n) for n in GATHER_EARLY], "weights_gather_early")))
    gathered.update(zip(GATHER_LATE, _gather_call([wire(n) for n in GATHER_LATE], "weights_gather_late")))
    full = {n: _join_blocks(gathered[n], SHARD_AXIS[n]) for n in SHARDED}

    in_cols, q_cols = _w_in_cols(), _w_q_cols()
    w_in_p = _pad_cols(full["w_in"], in_cols)
    w_q_p = _pad_cols(full["mla_w_q_up"], q_cols)
    w_kv, w_o, w_up, w_down = full["mla_w_kv_up"], full["w_out"], full["ffn_w_up"], full["ffn_w_down"]
    w_gate, w_proj = full["ple_w_gate"], full["ple_w_proj"]
    gconv, fconv = full["gdn_conv_w"], full["ffn_conv_w"]
    row = lambda a: a.reshape(1, -1)
    sc = jnp.zeros((8, 128), F32).at[0, :GDN_HEADS].set(wsh["gdn_a_log"]).at[1, :GDN_HEADS].set(wsh["gdn_dt_bias"])
    norm_g, qg, kvg = row(wsh["gdn_norm_g"]), row(wsh["mla_q_norm_g"]), row(wsh["mla_kv_norm_g"])
    g1, b1, g2, b2 = row(wsh["ln1_g"]), row(wsh["ln1_b"]), row(wsh["ln2_g"]), row(wsh["ln2_b"])
    fbias, bgate = row(wsh["ffn_conv_b"]), row(wsh["ple_b_gate"])

    inv = ROPE_THETA ** (-jnp.arange(0, MLA_ROPE, 2, dtype=F32) / MLA_ROPE)
    ang = jnp.arange(s, dtype=F32)[:, None] * inv[None, :]
    zero = jnp.zeros_like(ang)
    cos_t = jnp.concatenate([jnp.cos(ang), zero, jnp.cos(ang), zero], axis=1)
    sin_t = jnp.concatenate([-jnp.sin(ang), zero, jnp.sin(ang), zero], axis=1)

    proj = _matmul(xt, w_in_p, name="proj")
    cat, o_raw, states = _gdn_fwd(proj, gconv, sc, norm_g, bl, s)
    qf, kvf, kr = _mla_prep_fwd(proj, qg, kvg, w_q_p, w_kv, cos_t, sin_t, s, 256)
    cat = _attn_fwd(qf, kvf, kr, cat, bl, s, 256)
    mix = _matmul(cat, w_o, name="mix")
    r1, h1 = _ln1_fwd(xt, mix, g1, b1, 256)
    u = _matmul(h1, w_up, name="ffn_up")
    act = _ffn_act_fwd(u, fconv, fbias, bl, s, 256)
    ffn = _matmul(act, w_down, name="ffn_down", tk=1408)
    gpre = _matmul(h1, w_gate, name="ple_gate")
    pp = _matmul(pt, w_proj, name="ple_proj")
    dr2, dgpre, dpp, loss_acc, dbgate, dg2, db2 = _head(h1, ffn, gpre, pp, bgate, g2, b2, target, 256)

    dact = _matmul(dr2, w_down, name="d_act", tb=True, tn=256)
    d_w_down = _matmul(act, dr2, name="dw_down", ta=True, tm=256)
    dug, duu, dfcw_g, dfcw_u, dfcb_g, dfcb_u = _ffn_act_bwd(u, fconv, fbias, dact, bl, s, 256)
    du = jnp.concatenate([dug, duu], axis=1)
    dh1_a = _matmul(du, w_up, name="dh1_ffn", tb=True, tk=1408)
    dh1_b = _matmul(dgpre, w_gate, name="dh1_ple", tb=True)
    d_w_up = _matmul(h1, du, name="dw_up", ta=True)
    d_w_gate = _matmul(h1, dgpre, name="dw_gate", ta=True)
    d_w_proj = _matmul(pt, dpp, name="dw_proj", ta=True)
    dr1, dg1, db1 = _ln1_bwd(r1, dr2, dh1_a, dh1_b, g1, b1, 256)
    dcat = _matmul(dr1, w_o, name="d_cat", tb=True)
    d_w_o = _matmul(cat, dr1, name="dw_out", ta=True)
    dproj, dab, dcwq, dcwk, dcwv, dsc, dng = _gdn_bwd(proj, gconv, sc, norm_g, o_raw, states, dcat, bl, s)
    dqf, dkvf, dkr = _attn_bwd(qf, kvf, kr, dcat, bl, s, 256)
    dproj, dqg, dkvg, d_w_q_p, d_w_kv = _mla_prep_bwd(proj, qg, kvg, w_q_p, w_kv, cos_t, sin_t, dqf, dkvf, dkr, dab, dproj, s, 256)
    grad_x = _matmul(dproj, w_in_p, name="d_x", tb=True, add=dr1, add_scale=ALPHA)
    d_w_in_p = _matmul(xt, dproj, name="dw_in", ta=True)

    gfull = {
        "w_in": _unpad_cols(d_w_in_p, in_cols, D_IN),
        "mla_w_q_up": _unpad_cols(d_w_q_p, q_cols, MLA_HEADS * (MLA_NOPE + MLA_ROPE)),
        "mla_w_kv_up": d_w_kv, "w_out": d_w_o, "ffn_w_up": d_w_up, "ffn_w_down": d_w_down,
        "ple_w_gate": d_w_gate, "ple_w_proj": d_w_proj,
        "gdn_conv_w": jnp.concatenate([jnp.sum(dcwq, 0), jnp.sum(dcwk, 0), jnp.sum(dcwv, 0)], axis=1),
        "ffn_conv_w": jnp.concatenate([jnp.sum(dfcw_g, 0), jnp.sum(dfcw_u, 0)], axis=1),
    }
    dsc_sum = jnp.sum(dsc, axis=(0, 1))
    gsmall = {
        "gdn_a_log": dsc_sum[0, :GDN_HEADS], "gdn_dt_bias": dsc_sum[1, :GDN_HEADS],
        "gdn_norm_g": jnp.sum(dng[:, :, 0, :], axis=(0, 1)),
        "mla_q_norm_g": dqg[0], "mla_kv_norm_g": dkvg[0], "ln1_g": dg1[0], "ln1_b": db1[0],
        "ffn_conv_b": jnp.concatenate([jnp.sum(dfcb_g, 0), jnp.sum(dfcb_u, 0)], axis=1)[0],
        "ple_b_gate": dbgate[0], "ln2_g": dg2[0], "ln2_b": db2[0],
    }

    slabs = {n: _split_blocks(gfull[n], SHARD_AXIS[n]) for n in SHARDED}
    me_chip = 2 * lax.axis_index("x") + lax.axis_index("y")
    received = dict(zip(SCATTER_EARLY, _scatter_call([slabs[n] for n in SCATTER_EARLY], "grads_scatter_early")))
    received.update(zip(SCATTER_LATE, _scatter_call([slabs[n] for n in SCATTER_LATE], "grads_scatter_late")))
    big = [{}, {}, {}, {}]
    for n in SHARDED:
        own = lax.dynamic_index_in_dim(slabs[n], me_chip, 0, keepdims=False)
        for kind, val in enumerate(_adamw_reduced(own, received[n], wsh[n], msh[n], vsh[n], "adamw_" + n)):
            big[kind][n] = val

    small_shapes = [wsh[n].shape for n in SMALL]
    gsum = _all_reduce_small(_pack([gsmall[n] for n in SMALL]))
    spacks = _adamw_small(gsum, _pack([wsh[n] for n in SMALL]), _pack([msh[n] for n in SMALL]), _pack([vsh[n] for n in SMALL]))
    small = [dict(zip(SMALL, _unpack(pk, small_shapes))) for pk in spacks]

    loss = lax.psum(loss_acc[0, 0], ("x", "y", "c"))
    outs = [loss, grad_x.reshape(x.shape)]
    for kind in range(4):
        for n in WEIGHTS:
            val = big[kind][n] if n in big[kind] else small[kind][n]
            outs.append(val[None])
    return tuple(outs)
```

```python
import functools
import math

import numpy as np
import jax
import jax.numpy as jnp
from jax import lax
from jax.experimental import pallas as pl
from jax.experimental.pallas import tpu as pltpu

F32 = jnp.float32
BF16 = jnp.bfloat16

D_MODEL = 1024
CHUNK = 64
PLE_DIM = 256
GDN_HEADS = 4
GDN_DK = 128
GDN_DV = 128
GDN_CONV = 4
MLA_HEADS = 4
MLA_NOPE = 128
MLA_ROPE = 64
MLA_V = 128
MLA_Q_LORA = 384
MLA_KV_LORA = 256
ROPE_THETA = 10000.0
D_FF = 2816
FFN_CONV = 3
DEPTH = 1
ALPHA = (2.0 * DEPTH) ** 0.25
NORM_EPS = 1e-6
GDN_QK = GDN_HEADS * GDN_DK
GDN_VW = GDN_HEADS * GDN_DV
D_IN = 2 * GDN_QK + 2 * GDN_VW + 2 * GDN_HEADS + MLA_Q_LORA + MLA_KV_LORA + MLA_ROPE
ATT_SCALE = (MLA_NOPE + MLA_ROPE) ** -0.5

ADAM_LR = 0.001
ADAM_B1 = 0.9
ADAM_B2 = 0.999
ADAM_EPS = 1e-08
ADAM_WD = 0.01
ADAM_STEP = 10

LANES = 128
VMEM_LIMIT = 60 * 1024 * 1024
GDN_FWD_GROUP = 4
GDN_BWD_GROUP = 2
N_CHIPS = 4
N_DEV = 8

P_WIDTH = 3072
P_MLA = 2048
MESH = pl.DeviceIdType.MESH


def _rope_slot(j):
    return j if j < MLA_ROPE // 2 else 64 + (j - MLA_ROPE // 2)


def _w_in_cols():
    idx = -np.ones((P_WIDTH,), np.int64)
    for h in range(GDN_HEADS):
        base = h * 512
        idx[base:base + 128] = np.arange(128) + h * GDN_DK
        idx[base + 128:base + 256] = np.arange(128) + GDN_QK + h * GDN_DK
        idx[base + 256:base + 384] = np.arange(128) + 2 * GDN_QK + h * GDN_DV
        idx[base + 384:base + 512] = np.arange(128) + 2 * GDN_QK + GDN_VW + h * GDN_DV
    o_a = 2 * GDN_QK + 2 * GDN_VW
    idx[P_MLA:P_MLA + 2 * GDN_HEADS] = np.arange(2 * GDN_HEADS) + o_a
    o_cq = o_a + 2 * GDN_HEADS
    idx[P_MLA + 128:P_MLA + 512] = np.arange(MLA_Q_LORA) + o_cq
    o_ckv = o_cq + MLA_Q_LORA
    idx[P_MLA + 512:P_MLA + 768] = np.arange(MLA_KV_LORA) + o_ckv
    o_kr = o_ckv + MLA_KV_LORA
    for j in range(MLA_ROPE):
        idx[P_MLA + 768 + _rope_slot(j)] = o_kr + j
    return idx


def _w_q_cols():
    idx = -np.ones((MLA_HEADS * 256,), np.int64)
    for h in range(MLA_HEADS):
        o = h * (MLA_NOPE + MLA_ROPE)
        idx[h * 256:h * 256 + 128] = np.arange(128) + o
        for j in range(MLA_ROPE):
            idx[h * 256 + 128 + _rope_slot(j)] = o + MLA_NOPE + j
    return idx


def _pad_cols(w, idx):
    safe = np.where(idx >= 0, idx, 0)
    return jnp.where(jnp.asarray(idx >= 0)[None, :], w[:, safe], 0.0)


def _unpad_cols(wp, idx, n):
    inv = np.zeros((n,), np.int64)
    inv[idx[idx >= 0]] = np.nonzero(idx >= 0)[0]
    return wp[:, inv]


def _dot(a, b, ca, cb, precision=None):
    if precision is None:
        a = a.astype(BF16)
        b = b.astype(BF16)
    return lax.dot_general(a, b, (((ca,), (cb,)), ((), ())), preferred_element_type=F32, precision=precision)


@jax.custom_vjp
def mm(a, b):
    return _dot(a, b, 1, 0)


@jax.custom_vjp
def mm_nt(a, b):
    return _dot(a, b, 1, 1)


@jax.custom_vjp
def mm_tn(a, b):
    return _dot(a, b, 0, 0)


mm.defvjp(lambda a, b: (mm(a, b), (a, b)), lambda r, g: (mm_nt(g, r[1]), mm_tn(r[0], g)))
mm_nt.defvjp(lambda a, b: (mm_nt(a, b), (a, b)), lambda r, g: (mm(g, r[1]), mm_tn(g, r[0])))
mm_tn.defvjp(lambda a, b: (mm_tn(a, b), (a, b)), lambda r, g: (mm_nt(r[1], g), mm(r[0], g)))

def _split(a):
    hi = a.astype(BF16)
    return hi, (a - hi.astype(F32)).astype(BF16)


def _dot3(a, b, ca, cb):
    a_hi, a_lo = _split(a)
    b_hi, b_lo = _split(b)
    return (_dot(a_hi, b_hi, ca, cb) + _dot(a_hi, b_lo, ca, cb)) + _dot(a_lo, b_hi, ca, cb)


def _unit_lower_inverse(low):
    n = low.shape[0]
    ii = lax.broadcasted_iota(jnp.int32, (n, n), 0)
    jj = lax.broadcasted_iota(jnp.int32, (n, n), 1)
    inv = jnp.where(ii == jj, 1.0, 0.0) - low
    power = _dot3(low, low, 1, 0)
    k = 2
    while k < n:
        inv = inv + _dot3(inv, power, 1, 0)
        k *= 2
        if k < n:
            power = _dot3(power, power, 1, 0)
    return inv


@jax.custom_vjp
def unit_lower_solve(low, rhs):
    return _dot3(_unit_lower_inverse(low), rhs, 1, 0)


def _uls_fwd(low, rhs):
    inv = _unit_lower_inverse(low)
    x = _dot3(inv, rhs, 1, 0)
    return x, (inv, x)


def _uls_bwd(res, dx):
    inv, x = res
    drhs = _dot3(inv, dx, 0, 0)
    n = inv.shape[0]
    ii = lax.broadcasted_iota(jnp.int32, (n, n), 0)
    jj = lax.broadcasted_iota(jnp.int32, (n, n), 1)
    dlow = jnp.where(ii > jj, -_dot3(drhs, x, 1, 1), 0.0)
    return dlow, drhs


unit_lower_solve.defvjp(_uls_fwd, _uls_bwd)


def _shift_rows(x, s):
    if s == 0:
        return x
    n = x.shape[0]
    row = lax.broadcasted_iota(jnp.int32, x.shape, 0)
    rolled = pltpu.roll(x, s % n, 0)
    keep = (row >= s) if s > 0 else (row < n + s)
    return jnp.where(keep, rolled, 0.0)


def _row(w, j):
    tap = lax.broadcasted_iota(jnp.int32, w.shape, 0)
    return jnp.sum(jnp.where(tap == j, w, 0.0), axis=0, keepdims=True)


@jax.custom_vjp
def dwconv(x, w):
    k = w.shape[0]
    y = _row(w, k - 1) * x
    for j in range(k - 1):
        y = y + _row(w, j) * _shift_rows(x, k - 1 - j)
    return y


def _dwconv_fwd(x, w):
    return dwconv(x, w), (x, w)


def _dwconv_bwd(res, dy):
    x, w = res
    k = w.shape[0]
    dx = _row(w, k - 1) * dy
    tap = lax.broadcasted_iota(jnp.int32, w.shape, 0)
    dw = jnp.where(tap == k - 1, jnp.sum(dy * x, axis=0, keepdims=True), 0.0)
    for j in range(k - 1):
        dx = dx + _row(w, j) * _shift_rows(dy, -(k - 1 - j))
        dw = dw + jnp.where(tap == j, jnp.sum(dy * _shift_rows(x, k - 1 - j), axis=0, keepdims=True), 0.0)
    return dx, dw


dwconv.defvjp(_dwconv_fwd, _dwconv_bwd)


@jax.custom_vjp
def rope128(x, cos, sin):
    return x * cos + pltpu.roll(x, 64, 1) * sin


rope128.defvjp(lambda x, c, s: (rope128(x, c, s), (c, s)),
               lambda r, g: (g * r[0] + pltpu.roll(g * r[1], 64, 1), jnp.zeros_like(r[0]), jnp.zeros_like(r[1])))


def _silu(x):
    return x * jax.nn.sigmoid(x)


def _softplus(x):
    return jnp.maximum(x, 0.0) + jnp.log(1.0 + jnp.exp(-jnp.abs(x)))


def _rmsnorm(x, g):
    return x * lax.rsqrt(jnp.mean(x * x, axis=-1, keepdims=True) + NORM_EPS) * g


def _layernorm(x, g, b):
    mu = jnp.mean(x, axis=-1, keepdims=True)
    xc = x - mu
    var = jnp.mean(xc * xc, axis=-1, keepdims=True)
    return xc * lax.rsqrt(var + NORM_EPS) * g + b


def _pick_lane(row, lane):
    idx = lax.broadcasted_iota(jnp.int32, row.shape, 1)
    return jnp.sum(jnp.where(idx == lane, row, 0.0), axis=1, keepdims=True)


def _gdn_q(pq, cw):
    h = _silu(dwconv(pq, cw))
    return h * lax.rsqrt(jnp.sum(h * h, axis=-1, keepdims=True) + NORM_EPS) * (GDN_DK ** -0.5)


def _gdn_k(pk, cw):
    h = _silu(dwconv(pk, cw))
    return h * lax.rsqrt(jnp.sum(h * h, axis=-1, keepdims=True) + NORM_EPS)


def _gdn_v(pv, cw):
    return _silu(dwconv(pv, cw))


def _gdn_gate(ab, sc, head):
    a = _pick_lane(ab, head)
    b = _pick_lane(ab, GDN_HEADS + head)
    a_log = _pick_lane(_row(sc, 0), head)
    dt_bias = _pick_lane(_row(sc, 1), head)
    beta = jax.nn.sigmoid(b)
    g = -jnp.exp(a_log) * _softplus(a + dt_bias)
    return _two_lanes(g, beta)


def _two_lanes(c0, c1):
    lane = lax.broadcasted_iota(jnp.int32, (c0.shape[0], LANES), 1)
    return jnp.where(lane == 0, c0, jnp.where(lane == 1, c1, 0.0))


def _gdn_chunk_packed(q, k, v, gb, state):
    return _gdn_chunk(q, k, v, _pick_lane(gb, 0), _pick_lane(gb, 1), state)


def _gdn_post(o, z, norm_g):
    return _rmsnorm(o, norm_g) * _silu(z)


def _gdn_chunk(q, k, v, g, beta, state):
    c = q.shape[0]
    ii = lax.broadcasted_iota(jnp.int32, (c, c), 0)
    jj = lax.broadcasted_iota(jnp.int32, (c, c), 1)
    incl = ii >= jj
    g_row = jnp.sum(jnp.where(ii == jj, g, 0.0), axis=0, keepdims=True)
    gc_col = jnp.sum(jnp.where(incl, g_row, 0.0), axis=1, keepdims=True)
    gc_row = jnp.sum(jnp.where(jj >= ii, g, 0.0), axis=0, keepdims=True)
    decay = jnp.where(incl, jnp.exp(jnp.where(incl, gc_col - gc_row, 0.0)), 0.0)
    kb = k * beta
    low = jnp.where(ii > jj, mm_nt(kb, k) * decay, 0.0)
    eg = jnp.exp(gc_col)
    wu = unit_lower_solve(low, jnp.concatenate([kb * eg, v * beta], axis=1))
    w, u = wu[:, :GDN_DK], wu[:, GDN_DK:]
    qk = mm_nt(q, k) * decay
    g_last = jnp.sum(g_row, axis=1, keepdims=True)
    kd = k * jnp.exp(g_last - gc_col)
    v_new = u - mm(w, state)
    o = mm(q * eg, state) + mm(qk, v_new)
    new_state = state * jnp.exp(g_last) + mm_tn(kd, v_new)
    return o, new_state


def _attn_block(q, kn, kr, v, q0):
    s = (mm_nt(q[:, :128], kn) + mm_nt(q[:, 128:], kr)) * ATT_SCALE
    qpos = q0 + lax.broadcasted_iota(jnp.int32, s.shape, 0)
    kpos = lax.broadcasted_iota(jnp.int32, s.shape, 1)
    shift = int(math.log2(CHUNK))
    allowed = (kpos >> shift) <= (qpos >> shift)
    s = jnp.where(allowed, s, -1e30)
    p = jnp.exp(s - jnp.max(s, axis=-1, keepdims=True))
    p = p / jnp.sum(p, axis=-1, keepdims=True)
    return mm(p, v)


def _mla_prep(pm, qg, kvg, wq, wkv, cos, sin):
    cq = pm[:, 128:512]
    ckv = pm[:, 512:768]
    qf = mm(_rmsnorm(cq, qg), wq)
    parts = []
    for h in range(MLA_HEADS):
        parts.append(qf[:, h * 256:h * 256 + 128])
        parts.append(rope128(qf[:, h * 256 + 128:h * 256 + 256], cos, sin))
    kvf = mm(_rmsnorm(ckv, kvg), wkv)
    return jnp.concatenate(parts, axis=1), kvf, rope128(pm[:, 768:896], cos, sin)


def _ffn_act(ug, uu, wg, wu, bg, bu):
    return _silu(dwconv(ug, wg) + bg) * (dwconv(uu, wu) + bu)


def _head_loss(h1, ffn, gpre, pp, bgate, g2, b2, target):
    gate = jax.nn.sigmoid(gpre + bgate)
    h2 = _layernorm(ALPHA * h1 + ffn + gate * pp, g2, b2)
    err = h2 - target
    return 0.5 * jnp.sum(jnp.sum(err * err, axis=1, keepdims=True), axis=0, keepdims=True) / D_MODEL


def _params(sem):
    return pltpu.CompilerParams(dimension_semantics=sem, vmem_limit_bytes=VMEM_LIMIT)


def _matmul(a, b, *, name, ta=False, tb=False, tm=512, tn=512, tk=1024, add=None, add_scale=1.0,
            a_halves=False, b_halves=False):
    assert not (a_halves and ta) and not (b_halves and tb)
    a_shape = (a.shape[1], 2 * a.shape[2]) if a_halves else a.shape
    b_shape = (b.shape[1], 2 * b.shape[2]) if b_halves else b.shape
    (k_dim, m) = a_shape if ta else a_shape[::-1]
    (n, k2) = b_shape if tb else b_shape[::-1]
    assert k_dim == k2, (a.shape, b.shape)
    tm, tn, tk = min(tm, m), min(tn, n), min(tk, k_dim)
    assert m % tm == 0 and n % tn == 0 and k_dim % tk == 0, (name, m, n, k_dim, tm, tn, tk)
    nk = k_dim // tk
    ca, cb = (0 if ta else 1), (1 if tb else 0)

    def body(*refs):
        if add is None:
            a_ref, b_ref, o_ref, acc = refs
        else:
            a_ref, b_ref, c_ref, o_ref, acc = refs
        kk = pl.program_id(2)

        @pl.when(kk == 0)
        def _():
            acc[...] = jnp.zeros_like(acc)

        acc[...] += _dot(a_ref[...], b_ref[...], ca, cb)

        @pl.when(kk == nk - 1)
        def _():
            r = acc[...]
            if add is not None:
                r = r + add_scale * c_ref[...]
            o_ref[...] = r

    a_spec = pl.BlockSpec((tk, tm), lambda i, j, k: (k, i)) if ta else pl.BlockSpec((tm, tk), lambda i, j, k: (i, k))
    b_spec = pl.BlockSpec((tn, tk), lambda i, j, k: (j, k)) if tb else pl.BlockSpec((tk, tn), lambda i, j, k: (k, j))
    if a_halves:
        kh = k_dim // 2 // tk
        assert kh * tk * 2 == k_dim
        a_spec = pl.BlockSpec((None, tm, tk), lambda i, j, k: (k // kh, i, k % kh))
    if b_halves:
        nh = n // 2 // tn
        assert nh * tn * 2 == n
        b_spec = pl.BlockSpec((None, tk, tn), lambda i, j, k: (j // nh, k, j % nh))
    in_specs = [a_spec, b_spec]
    args = [a, b]
    if add is not None:
        in_specs.append(pl.BlockSpec((tm, tn), lambda i, j, k: (i, j)))
        args.append(add)
    return pl.pallas_call(
        body, name=name, grid=(m // tm, n // tn, nk),
        in_specs=in_specs, out_specs=pl.BlockSpec((tm, tn), lambda i, j, k: (i, j)),
        out_shape=jax.ShapeDtypeStruct((m, n), F32),
        scratch_shapes=[pltpu.VMEM((tm, tn), F32)],
        compiler_params=_params(("parallel", "parallel", "arbitrary")),
    )(*args)


def _riding(core, n_in, n_out, n_scratch, ride, inner, steps):
    if ride is None:
        return core
    copies, nr = ride[1], len(ride[0])

    def body(*refs):
        cuts = np.cumsum([0, n_in, nr, n_out, nr, n_scratch])
        ins, rin, outs, rout, scratch = (refs[a:b] for a, b in zip(cuts[:-1], cuts[1:]))
        sems = refs[cuts[-1]:]
        step = pl.program_id(0) * inner + pl.program_id(1)

        @pl.when(step == 0)
        def _():
            _exchange_start(copies(rin, rout, sems))

        @pl.when(step == steps // 2)
        def _():
            _exchange_pass_on(copies(rin, rout, sems))

        core(*ins, *outs, *scratch)

        @pl.when(step == steps - 1)
        def _():
            _exchange_finish(copies(rin, rout, sems))

    return body


def _ride_specs(ride):
    if ride is None:
        return [], [], [], [], []
    arrays, _, sems, shapes = ride
    return [HBM_REF] * len(arrays), [HBM_REF] * len(arrays), list(shapes), sems(len(arrays)), list(arrays)


def _gdn_fwd(proj, conv_w, sc, norm_g, bl, s, ride=None):
    nc = s // CHUNK

    def core(ph_ref, ab_ref, cwq_ref, cwk_ref, cwv_ref, sc_ref, ng_ref, cat_ref, o_ref, st_ref, q_s, k_s, v_s, gb_s):
        q_s[...] = _gdn_q(ph_ref[:, 0:128], cwq_ref[...])
        k_s[...] = _gdn_k(ph_ref[:, 128:256], cwk_ref[...])
        v_s[...] = _gdn_v(ph_ref[:, 256:384], cwv_ref[...])
        gb_s[...] = _gdn_gate(ab_ref[...], sc_ref[...], pl.program_id(1))

        group = math.gcd(nc, GDN_FWD_GROUP)

        def chunks(i, state):
            for j in range(group):
                n = i * group + j
                rows = pl.ds(pl.multiple_of(n * CHUNK, CHUNK), CHUNK)
                st_ref[n] = state
                o, state = _gdn_chunk_packed(q_s[rows, :], k_s[rows, :], v_s[rows, :], gb_s[rows, :], state)
                o_ref[rows, :] = o
            return state

        lax.fori_loop(0, nc // group, chunks, jnp.zeros((GDN_DK, GDN_DV), F32))
        cat_ref[...] = _gdn_post(o_ref[...], ph_ref[:, 384:512], ng_ref[...])

    t = bl * s
    r_in, r_out, r_shapes, r_sems, r_args = _ride_specs(ride)
    outs = pl.pallas_call(
        _riding(core, 7, 3, 4, ride, GDN_HEADS, bl * GDN_HEADS), name="gdn_fwd", grid=(bl, GDN_HEADS),
        in_specs=[
            pl.BlockSpec((s, 512), lambda b, h: (b, h)),
            pl.BlockSpec((s, 128), lambda b, h: (b, P_MLA // 128)),
            pl.BlockSpec((GDN_CONV, 128), lambda b, h: (0, h)),
            pl.BlockSpec((GDN_CONV, 128), lambda b, h: (0, GDN_HEADS + h)),
            pl.BlockSpec((GDN_CONV, 128), lambda b, h: (0, 2 * GDN_HEADS + h)),
            pl.BlockSpec((8, 128), lambda b, h: (0, 0)),
            pl.BlockSpec((1, 128), lambda b, h: (0, 0)),
        ] + r_in,
        out_specs=[
            pl.BlockSpec((s, 128), lambda b, h: (b, h)),
            pl.BlockSpec((s, 128), lambda b, h: (b, h)),
            pl.BlockSpec((None, None, nc, GDN_DK, GDN_DV), lambda b, h: (b, h, 0, 0, 0)),
        ] + r_out,
        out_shape=[
            jax.ShapeDtypeStruct((t, 2 * GDN_VW), F32),
            jax.ShapeDtypeStruct((t, GDN_VW), F32),
            jax.ShapeDtypeStruct((bl, GDN_HEADS, nc, GDN_DK, GDN_DV), F32),
        ] + r_shapes,
        scratch_shapes=[pltpu.VMEM((s, 128), F32)] * 4 + r_sems,
        compiler_params=_params(("arbitrary", "arbitrary")),
    )(proj, proj, conv_w, conv_w, conv_w, sc, norm_g, *r_args)
    return outs[0], outs[1], outs[2], list(outs[3:])


def _gdn_bwd(proj, conv_w, sc, norm_g, o_raw, states, dcat, bl, s, ride=None):
    nc = s // CHUNK

    def core(ph_ref, ab_ref, cwq_ref, cwk_ref, cwv_ref, sc_ref, ng_ref, o_ref, st_ref, dc_ref,
             dph_ref, dab_ref, dcwq_ref, dcwk_ref, dcwv_ref, dsc_ref, dng_ref, q_s, k_s, v_s, gb_s, do_s):
        head = pl.program_id(1)
        gate = functools.partial(_gdn_gate, head=head)
        paths = [(_gdn_q, 0, cwq_ref, q_s, dcwq_ref), (_gdn_k, 128, cwk_ref, k_s, dcwk_ref), (_gdn_v, 256, cwv_ref, v_s, dcwv_ref)]
        for fn, col, cw_ref, val_s, _ in paths:
            val_s[...] = fn(ph_ref[:, col:col + 128], cw_ref[...])
        gb_s[...] = gate(ab_ref[...], sc_ref[...])
        _, post_vjp = jax.vjp(_gdn_post, o_ref[...], ph_ref[:, 384:512], ng_ref[...])
        d_o, dz, dng = post_vjp(dc_ref[...])
        do_s[...] = d_o
        dph_ref[:, 384:512] = dz
        dng_ref[...] = jnp.broadcast_to(dng, dng_ref.shape)

        group = math.gcd(nc, GDN_BWD_GROUP)

        def chunks(i, dstate):
            ns = [nc - 1 - (i * group + j) for j in range(group)]
            rows = [pl.ds(pl.multiple_of(n * CHUNK, CHUNK), CHUNK) for n in ns]
            vjps = [jax.vjp(_gdn_chunk_packed, q_s[r, :], k_s[r, :], v_s[r, :], gb_s[r, :], st_ref[n])[1] for n, r in zip(ns, rows)]
            d_os = [do_s[r, :] for r in rows]
            grads = []
            for vjp, d_o in zip(vjps, d_os):
                dq, dk, dv, dgb, dstate = vjp((d_o, dstate))
                grads.append((dq, dk, dv, dgb))
            for r, (dq, dk, dv, dgb) in zip(rows, grads):
                q_s[r, :], k_s[r, :], v_s[r, :], gb_s[r, :] = dq, dk, dv, dgb
            return dstate

        lax.fori_loop(0, nc // group, chunks, jnp.zeros((GDN_DK, GDN_DV), F32))
        for fn, col, cw_ref, val_s, dcw_ref in paths:
            _, vjp = jax.vjp(fn, ph_ref[:, col:col + 128], cw_ref[...])
            dph_ref[:, col:col + 128], dcw_ref[...] = vjp(val_s[...])
        _, gate_vjp = jax.vjp(gate, ab_ref[...], sc_ref[...])
        dab, dsc_ref[...] = gate_vjp(gb_s[...])

        @pl.when(head == 0)
        def _():
            dab_ref[...] = jnp.zeros_like(dab_ref)

        dab_ref[...] += dab

    t = bl * s
    cw_out = pl.BlockSpec((None, GDN_CONV, 128), lambda b, h: (b, 0, h))
    part = pl.BlockSpec((None, None, 8, 128), lambda b, h: (b, h, 0, 0))
    r_in, r_out, r_shapes, r_sems, r_args = _ride_specs(ride)
    outs = pl.pallas_call(
        _riding(core, 10, 7, 5, ride, GDN_HEADS, bl * GDN_HEADS), name="gdn_bwd", grid=(bl, GDN_HEADS),
        in_specs=[
            pl.BlockSpec((s, 512), lambda b, h: (b, h)),
            pl.BlockSpec((s, 128), lambda b, h: (b, P_MLA // 128)),
            pl.BlockSpec((GDN_CONV, 128), lambda b, h: (0, h)),
            pl.BlockSpec((GDN_CONV, 128), lambda b, h: (0, GDN_HEADS + h)),
            pl.BlockSpec((GDN_CONV, 128), lambda b, h: (0, 2 * GDN_HEADS + h)),
            pl.BlockSpec((8, 128), lambda b, h: (0, 0)),
            pl.BlockSpec((1, 128), lambda b, h: (0, 0)),
            pl.BlockSpec((s, 128), lambda b, h: (b, h)),
            pl.BlockSpec((None, None, nc, GDN_DK, GDN_DV), lambda b, h: (b, h, 0, 0, 0)),
            pl.BlockSpec((s, 128), lambda b, h: (b, h)),
        ] + r_in,
        out_specs=[
            pl.BlockSpec((s, 512), lambda b, h: (b, h)),
            pl.BlockSpec((s, 128), lambda b, h: (b, 0)),
            cw_out, cw_out, cw_out, part, part,
        ] + r_out,
        out_shape=[
            jax.ShapeDtypeStruct((t, P_WIDTH), F32),
            jax.ShapeDtypeStruct((t, 128), F32),
            jax.ShapeDtypeStruct((bl, GDN_CONV, 512), F32),
            jax.ShapeDtypeStruct((bl, GDN_CONV, 512), F32),
            jax.ShapeDtypeStruct((bl, GDN_CONV, 512), F32),
            jax.ShapeDtypeStruct((bl, GDN_HEADS, 8, 128), F32),
            jax.ShapeDtypeStruct((bl, GDN_HEADS, 8, 128), F32),
        ] + r_shapes,
        scratch_shapes=[pltpu.VMEM((s, 128), F32)] * 5 + r_sems,
        compiler_params=_params(("arbitrary", "arbitrary")),
    )(proj, proj, conv_w, conv_w, conv_w, sc, norm_g, o_raw, states, dcat, *r_args)
    return tuple(outs[:7]) + (list(outs[7:]),)


def _mla_prep_fwd(proj, qg, kvg, wq, wkv, cos, sin, s, tm):
    t = proj.shape[0]
    tm = min(tm, s)
    nps = s // tm
    const = lambda shape: pl.BlockSpec(shape, lambda i: (0, 0))

    def body(pm_ref, qg_ref, kvg_ref, wq_ref, wkv_ref, cos_ref, sin_ref, qf_ref, kvf_ref, kr_ref):
        qf, kvf, kr = _mla_prep(pm_ref[...], qg_ref[...], kvg_ref[...], wq_ref[...], wkv_ref[...], cos_ref[...], sin_ref[...])
        qf_ref[...], kvf_ref[...], kr_ref[...] = qf, kvf, kr

    return pl.pallas_call(
        body, name="mla_prep_fwd", grid=(t // tm,),
        in_specs=[
            pl.BlockSpec((tm, 1024), lambda i: (i, P_MLA // 1024)),
            const((1, MLA_Q_LORA)), const((1, MLA_KV_LORA)), const(wq.shape), const(wkv.shape),
            pl.BlockSpec((tm, 128), lambda i: (i % nps, 0)), pl.BlockSpec((tm, 128), lambda i: (i % nps, 0)),
        ],
        out_specs=[pl.BlockSpec((tm, 1024), lambda i: (i, 0)), pl.BlockSpec((tm, 1024), lambda i: (i, 0)),
                   pl.BlockSpec((tm, 128), lambda i: (i, 0))],
        out_shape=[jax.ShapeDtypeStruct((t, 1024), F32), jax.ShapeDtypeStruct((t, 1024), F32),
                   jax.ShapeDtypeStruct((t, 128), F32)],
        compiler_params=_params(("parallel",)),
    )(proj, qg, kvg, wq, wkv, cos, sin)


def _mla_prep_bwd(proj, qg, kvg, wq, wkv, cos, sin, dqf, dkvf, dkr, dab, dproj, s, tm):
    t = proj.shape[0]
    tm = min(tm, s)
    nps = s // tm
    const = lambda shape: pl.BlockSpec(shape, lambda i: (0, 0))

    def body(pm_ref, qg_ref, kvg_ref, wq_ref, wkv_ref, cos_ref, sin_ref, dqf_ref, dkvf_ref, dkr_ref, dab_ref, dp_in,
             dp_ref, dqg_ref, dkvg_ref, dwq_ref, dwkv_ref):
        del dp_in
        fn = lambda pm, qg_, kvg_, wq_, wkv_: _mla_prep(pm, qg_, kvg_, wq_, wkv_, cos_ref[...], sin_ref[...])
        _, vjp = jax.vjp(fn, pm_ref[...], qg_ref[...], kvg_ref[...], wq_ref[...].astype(F32), wkv_ref[...].astype(F32))
        dpm, dqg, dkvg, dwq, dwkv = vjp((dqf_ref[...], dkvf_ref[...], dkr_ref[...]))
        dp_ref[...] = jnp.concatenate([dab_ref[...], dpm[:, 128:]], axis=1)

        @pl.when(pl.program_id(0) == 0)
        def _():
            dqg_ref[...] = jnp.zeros_like(dqg_ref)
            dkvg_ref[...] = jnp.zeros_like(dkvg_ref)
            dwq_ref[...] = jnp.zeros_like(dwq_ref)
            dwkv_ref[...] = jnp.zeros_like(dwkv_ref)

        dqg_ref[...] += dqg
        dkvg_ref[...] += dkvg
        dwq_ref[...] += dwq
        dwkv_ref[...] += dwkv

    rows = lambda w: pl.BlockSpec((tm, w), lambda i: (i, 0))
    return pl.pallas_call(
        body, name="mla_prep_bwd", grid=(t // tm,),
        in_specs=[
            pl.BlockSpec((tm, 1024), lambda i: (i, P_MLA // 1024)),
            const((1, MLA_Q_LORA)), const((1, MLA_KV_LORA)), const(wq.shape), const(wkv.shape),
            pl.BlockSpec((tm, 128), lambda i: (i % nps, 0)), pl.BlockSpec((tm, 128), lambda i: (i % nps, 0)),
            rows(1024), rows(1024), rows(128), rows(128),
            pl.BlockSpec(memory_space=pl.ANY),
        ],
        out_specs=[pl.BlockSpec((tm, 1024), lambda i: (i, P_MLA // 1024)),
                   const((1, MLA_Q_LORA)), const((1, MLA_KV_LORA)), const(wq.shape), const(wkv.shape)],
        out_shape=[jax.ShapeDtypeStruct(dproj.shape, F32),
                   jax.ShapeDtypeStruct((1, MLA_Q_LORA), F32), jax.ShapeDtypeStruct((1, MLA_KV_LORA), F32),
                   jax.ShapeDtypeStruct(wq.shape, F32), jax.ShapeDtypeStruct(wkv.shape, F32)],
        input_output_aliases={11: 0},
        compiler_params=_params(("arbitrary",)),
    )(proj, qg, kvg, wq, wkv, cos, sin, dqf, dkvf, dkr, dab, dproj)


def _attn_fwd(qf, kvf, kr, cat, bl, s, tq):
    tq = min(tq, s)
    nq = s // tq

    def body(q_ref, kv_ref, kr_ref, cat_in, o_ref):
        del cat_in
        q0 = pl.program_id(2) * tq
        o_ref[...] = _attn_block(q_ref[...], kv_ref[:, 0:128], kr_ref[...], kv_ref[:, 128:256], q0)

    return pl.pallas_call(
        body, name="attn_fwd", grid=(bl, MLA_HEADS, nq),
        in_specs=[
            pl.BlockSpec((tq, 256), lambda b, h, i: (b * nq + i, h)),
            pl.BlockSpec((s, 256), lambda b, h, i: (b, h)),
            pl.BlockSpec((s, 128), lambda b, h, i: (b, 0)),
            pl.BlockSpec(memory_space=pl.ANY),
        ],
        out_specs=pl.BlockSpec((tq, 128), lambda b, h, i: (b * nq + i, GDN_HEADS + h)),
        out_shape=jax.ShapeDtypeStruct(cat.shape, F32),
        input_output_aliases={3: 0},
        compiler_params=_params(("parallel", "parallel", "parallel")),
    )(qf, kvf, kr, cat)


def _attn_bwd(qf, kvf, kr, dcat, bl, s, tq):
    tq = min(tq, s)
    nq = s // tq

    def body(q_ref, kv_ref, kr_ref, do_ref, dq_ref, dkv_ref, dkr_ref):
        h, i = pl.program_id(1), pl.program_id(2)
        fn = functools.partial(_attn_block, q0=i * tq)
        _, vjp = jax.vjp(fn, q_ref[...], kv_ref[:, 0:128], kr_ref[...], kv_ref[:, 128:256])
        dq, dkn, dkr, dv = vjp(do_ref[...])
        dq_ref[...] = dq

        @pl.when(i == 0)
        def _():
            dkv_ref[...] = jnp.zeros_like(dkv_ref)

        @pl.when((i == 0) & (h == 0))
        def _():
            dkr_ref[...] = jnp.zeros_like(dkr_ref)

        dkv_ref[...] += jnp.concatenate([dkn, dv], axis=1)
        dkr_ref[...] += dkr

    t = bl * s
    return pl.pallas_call(
        body, name="attn_bwd", grid=(bl, MLA_HEADS, nq),
        in_specs=[
            pl.BlockSpec((tq, 256), lambda b, h, i: (b * nq + i, h)),
            pl.BlockSpec((s, 256), lambda b, h, i: (b, h)),
            pl.BlockSpec((s, 128), lambda b, h, i: (b, 0)),
            pl.BlockSpec((tq, 128), lambda b, h, i: (b * nq + i, GDN_HEADS + h)),
        ],
        out_specs=[
            pl.BlockSpec((tq, 256), lambda b, h, i: (b * nq + i, h)),
            pl.BlockSpec((s, 256), lambda b, h, i: (b, h)),
            pl.BlockSpec((s, 128), lambda b, h, i: (b, 0)),
        ],
        out_shape=[jax.ShapeDtypeStruct((t, 1024), F32), jax.ShapeDtypeStruct((t, 1024), F32),
                   jax.ShapeDtypeStruct((t, 128), F32)],
        compiler_params=_params(("parallel", "arbitrary", "arbitrary")),
    )(qf, kvf, kr, dcat)


def _ln1_fwd(x, mix, g, b, tm):
    t = x.shape[0]
    tm = min(tm, t)

    def body(x_ref, mix_ref, g_ref, b_ref, r_ref, h_ref):
        r = ALPHA * x_ref[...] + mix_ref[...]
        r_ref[...] = r
        h_ref[...] = _layernorm(r, g_ref[...], b_ref[...])

    rows = pl.BlockSpec((tm, D_MODEL), lambda i: (i, 0))
    vec = pl.BlockSpec((1, D_MODEL), lambda i: (0, 0))
    return pl.pallas_call(
        body, name="ln1_fwd", grid=(t // tm,), in_specs=[rows, rows, vec, vec], out_specs=[rows, rows],
        out_shape=[jax.ShapeDtypeStruct(x.shape, F32)] * 2, compiler_params=_params(("parallel",)),
    )(x, mix, g, b)


def _ln1_bwd(r1, dr2, da, db_, g, b, tm):
    t = r1.shape[0]
    tm = min(tm, t)

    def body(r_ref, d2_ref, da_ref, db_ref, g_ref, b_ref, dr_ref, dg_ref, dbias_ref):
        dh = ALPHA * d2_ref[...] + da_ref[...] + db_ref[...]
        _, vjp = jax.vjp(_layernorm, r_ref[...], g_ref[...], b_ref[...])
        dr, dg, dbias = vjp(dh)
        dr_ref[...] = dr

        @pl.when(pl.program_id(0) == 0)
        def _():
            dg_ref[...] = jnp.zeros_like(dg_ref)
            dbias_ref[...] = jnp.zeros_like(dbias_ref)

        dg_ref[...] += dg
        dbias_ref[...] += dbias

    rows = pl.BlockSpec((tm, D_MODEL), lambda i: (i, 0))
    vec = pl.BlockSpec((1, D_MODEL), lambda i: (0, 0))
    return pl.pallas_call(
        body, name="ln1_bwd", grid=(t // tm,), in_specs=[rows] * 4 + [vec, vec], out_specs=[rows, vec, vec],
        out_shape=[jax.ShapeDtypeStruct(r1.shape, F32)] + [jax.ShapeDtypeStruct((1, D_MODEL), F32)] * 2,
        compiler_params=_params(("arbitrary",)),
    )(r1, dr2, da, db_, g, b)


def _ffn_act_fwd(u, conv_w, conv_b, bl, s, cb):
    nj = D_FF // cb

    def body(ug_ref, uu_ref, wg_ref, wu_ref, bg_ref, bu_ref, act_ref):
        act_ref[...] = _ffn_act(ug_ref[...], uu_ref[...], wg_ref[...], wu_ref[...], bg_ref[...], bu_ref[...])

    return pl.pallas_call(
        body, name="ffn_act_fwd", grid=(bl, nj),
        in_specs=[
            pl.BlockSpec((s, cb), lambda b, j: (b, j)), pl.BlockSpec((s, cb), lambda b, j: (b, nj + j)),
            pl.BlockSpec((FFN_CONV, cb), lambda b, j: (0, j)), pl.BlockSpec((FFN_CONV, cb), lambda b, j: (0, nj + j)),
            pl.BlockSpec((1, cb), lambda b, j: (0, j)), pl.BlockSpec((1, cb), lambda b, j: (0, nj + j)),
        ],
        out_specs=pl.BlockSpec((s, cb), lambda b, j: (b, j)),
        out_shape=jax.ShapeDtypeStruct((bl * s, D_FF), F32),
        compiler_params=_params(("parallel", "parallel")),
    )(u, u, conv_w, conv_w, conv_b, conv_b)


def _ffn_act_bwd(u, conv_w, conv_b, dact, bl, s, cb):
    nj = D_FF // cb

    def body(ug_ref, uu_ref, wg_ref, wu_ref, bg_ref, bu_ref, da_ref, du_ref, dwg_ref, dwu_ref, dbg_ref, dbu_ref):
        _, vjp = jax.vjp(_ffn_act, ug_ref[...], uu_ref[...], wg_ref[...], wu_ref[...], bg_ref[...], bu_ref[...])
        du_ref[0], du_ref[1], dwg_ref[...], dwu_ref[...], dbg_ref[...], dbu_ref[...] = vjp(da_ref[...])

    t = bl * s
    blk = pl.BlockSpec((s, cb), lambda b, j: (b, j))
    wpart = pl.BlockSpec((None, FFN_CONV, cb), lambda b, j: (b, 0, j))
    bpart = pl.BlockSpec((None, 1, cb), lambda b, j: (b, 0, j))
    return pl.pallas_call(
        body, name="ffn_act_bwd", grid=(bl, nj),
        in_specs=[
            blk, pl.BlockSpec((s, cb), lambda b, j: (b, nj + j)),
            pl.BlockSpec((FFN_CONV, cb), lambda b, j: (0, j)), pl.BlockSpec((FFN_CONV, cb), lambda b, j: (0, nj + j)),
            pl.BlockSpec((1, cb), lambda b, j: (0, j)), pl.BlockSpec((1, cb), lambda b, j: (0, nj + j)),
            blk,
        ],
        out_specs=[pl.BlockSpec((2, s, cb), lambda b, j: (0, b, j)), wpart, wpart, bpart, bpart],
        out_shape=[jax.ShapeDtypeStruct((2, t, D_FF), F32)] + [jax.ShapeDtypeStruct((bl, FFN_CONV, D_FF), F32)] * 2
        + [jax.ShapeDtypeStruct((bl, 1, D_FF), F32)] * 2,
        compiler_params=_params(("parallel", "parallel")),
    )(u, u, conv_w, conv_w, conv_b, conv_b, dact)


def _head(h1, ffn, gpre, pp, bgate, g2, b2, target, tm):
    t = h1.shape[0]
    tm = min(tm, t)

    def body(h1_ref, ffn_ref, gp_ref, pp_ref, bg_ref, g2_ref, b2_ref, tg_ref,
             dr_ref, dgp_ref, dpp_ref, loss_ref, dbg_ref, dg2_ref, db2_ref):
        fn = functools.partial(_head_loss, target=tg_ref[...])
        loss, vjp = jax.vjp(fn, h1_ref[...], ffn_ref[...], gp_ref[...], pp_ref[...], bg_ref[...], g2_ref[...], b2_ref[...])
        _, dffn, dgp, dpp, dbg, dg2, db2 = vjp(jnp.ones((1, 1), F32))
        dr_ref[...], dgp_ref[...], dpp_ref[...] = dffn, dgp, dpp

        @pl.when(pl.program_id(0) == 0)
        def _():
            loss_ref[...] = jnp.zeros_like(loss_ref)
            dbg_ref[...] = jnp.zeros_like(dbg_ref)
            dg2_ref[...] = jnp.zeros_like(dg2_ref)
            db2_ref[...] = jnp.zeros_like(db2_ref)

        loss_ref[...] += jnp.broadcast_to(loss, loss_ref.shape)
        dbg_ref[...] += dbg
        dg2_ref[...] += dg2
        db2_ref[...] += db2

    rows = pl.BlockSpec((tm, D_MODEL), lambda i: (i, 0))
    vec = pl.BlockSpec((1, D_MODEL), lambda i: (0, 0))
    return pl.pallas_call(
        body, name="head", grid=(t // tm,), in_specs=[rows] * 4 + [vec] * 3 + [rows],
        out_specs=[rows] * 3 + [pl.BlockSpec((8, 128), lambda i: (0, 0))] + [vec] * 3,
        out_shape=[jax.ShapeDtypeStruct(h1.shape, F32)] * 3 + [jax.ShapeDtypeStruct((8, 128), F32)]
        + [jax.ShapeDtypeStruct((1, D_MODEL), F32)] * 3,
        compiler_params=_params(("arbitrary",)),
    )(h1, ffn, gpre, pp, bgate, g2, b2, target)


def _adam_update(g, w_ref, m_ref, v_ref, g_ref, d_ref, nm_ref, nv_ref):
    m2 = ADAM_B1 * m_ref[...] + (1.0 - ADAM_B1) * g
    v2 = ADAM_B2 * v_ref[...] + (1.0 - ADAM_B2) * jnp.square(g)
    m_hat = m2 / (1.0 - ADAM_B1 ** ADAM_STEP)
    v_hat = v2 / (1.0 - ADAM_B2 ** ADAM_STEP)
    g_ref[...] = g
    d_ref[...] = -ADAM_LR * (m_hat / (jnp.sqrt(v_hat) + ADAM_EPS) + ADAM_WD * w_ref[...])
    nm_ref[...] = m2
    nv_ref[...] = v2


def _row_tile(rows, cols, limit_bytes=256 * 1024):
    best = None
    for t in range(8, rows + 1, 8):
        if rows % t == 0 and t * cols * 4 <= limit_bytes:
            best = t
    return best or rows


def _adamw_reduced(own, recv, w, m, v, name):
    a, b = w.shape
    ta = _row_tile(a, b)

    def body(own_ref, recv_ref, w_ref, m_ref, v_ref, g_ref, d_ref, nm_ref, nv_ref):
        c = lax.axis_index("c")
        for core in range(2):
            @pl.when(c == core)
            def _():
                same = [own_ref[...], recv_ref[0], recv_ref[1], recv_ref[2]]
                other = [recv_ref[3], recv_ref[4], recv_ref[5], recv_ref[6]]
                core0, core1 = (same, other) if core == 0 else (other, same)
                g = core0[0] + core1[0]
                for r in range(1, N_CHIPS):
                    g = (g + core0[r]) + core1[r]
                _adam_update(g, w_ref, m_ref, v_ref, g_ref, d_ref, nm_ref, nv_ref)

    blk = pl.BlockSpec((ta, b), lambda i: (i, 0))
    return pl.pallas_call(
        body, name=name, grid=(a // ta,),
        in_specs=[blk, pl.BlockSpec((7, ta, b), lambda i: (0, i, 0)), blk, blk, blk], out_specs=[blk] * 4,
        out_shape=[jax.ShapeDtypeStruct(w.shape, F32)] * 4, compiler_params=_params(("parallel",)),
    )(own, recv, w, m, v)


def _adamw_small(g, w, m, v):
    def body(g_in, w_ref, m_ref, v_ref, g_ref, d_ref, nm_ref, nv_ref):
        _adam_update(g_in[...], w_ref, m_ref, v_ref, g_ref, d_ref, nm_ref, nv_ref)

    blk = pl.BlockSpec(w.shape, lambda i: (0, 0))
    return pl.pallas_call(
        body, name="adamw_small", grid=(1,), in_specs=[blk] * 4, out_specs=[blk] * 4,
        out_shape=[jax.ShapeDtypeStruct(w.shape, F32)] * 4, compiler_params=_params(("arbitrary",)),
    )(g, w, m, v)


def _remote(src, dst, send_sem, recv_sem, device):
    return pltpu.make_async_remote_copy(src_ref=src, dst_ref=dst, send_sem=send_sem, recv_sem=recv_sem,
                                        device_id=device, device_id_type=MESH)


def _place():
    x, y, c = lax.axis_index("x"), lax.axis_index("y"), lax.axis_index("c")
    return x, y, c, 2 * x + y, [(1 - x, y), (x, 1 - y), (1 - x, 1 - y)]


HBM_REF = pl.BlockSpec(memory_space=pl.ANY)
HALF_ROWS_QUANTUM = 16


def _gather_sems(n):
    return [pltpu.SemaphoreType.DMA((3 * n,))] * 4 + [pltpu.SemaphoreType.DMA((n,))]


def _gather_copies(ins, outs, sems):
    send_s, recv_s, fsend_s, frecv_s, local_s = sems
    x, y, c, me, chips = _place()
    local, sends, steps = [], [], []
    for i, (src, dst) in enumerate(zip(ins, outs)):
        local.append(pltpu.make_async_copy(src, dst.at[me], local_s.at[i]))
        half = src.shape[0] // 2
        split = src.shape[0] % (2 * HALF_ROWS_QUANTUM) == 0
        if split:
            mine = pl.ds(pl.multiple_of(c * half, HALF_ROWS_QUANTUM), half)
            theirs = pl.ds(pl.multiple_of((1 - c) * half, HALF_ROWS_QUANTUM), half)
        for r, (px, py) in enumerate(chips):
            k, peer = 3 * i + r, 2 * px + py
            if split:
                sends.append(_remote(src.at[mine], dst.at[me, mine], send_s.at[k], recv_s.at[k], (px, py, c)))
                landed = dst.at[peer, mine]
                steps.append((_remote(src.at[mine], landed, send_s.at[k], recv_s.at[k], (px, py, c)),
                              _remote(landed, landed, fsend_s.at[k], frecv_s.at[k], (x, y, 1 - c)),
                              _remote(dst.at[peer, theirs], dst.at[peer, theirs], fsend_s.at[k], frecv_s.at[k], (x, y, 1 - c))))
            else:
                sends.append(_remote(src, dst.at[me], send_s.at[k], recv_s.at[k], (px, py, c)))
                steps.append((_remote(src, dst.at[peer], send_s.at[k], recv_s.at[k], (px, py, c)), None, None))
    return local, sends, steps


def _scatter_sems(n):
    return [pltpu.SemaphoreType.DMA((4 * n,))] * 2 + [pltpu.SemaphoreType.DMA((3 * n,))] * 2


def _scatter_copies(ins, outs, sems):
    send_s, recv_s, fsend_s, frecv_s = sems
    x, y, c, me, chips = _place()
    sends, steps = [], []
    for i, (src, dst) in enumerate(zip(ins, outs)):
        for r, (px, py) in enumerate(chips):
            k = 4 * i + r
            cp = _remote(src.at[2 * px + py], dst.at[r], send_s.at[k], recv_s.at[k], (px, py, c))
            fwd = _remote(dst.at[r], dst.at[4 + r], fsend_s.at[3 * i + r], frecv_s.at[3 * i + r], (x, y, 1 - c))
            sends.append(cp)
            steps.append((cp, fwd, fwd))
        k = 4 * i + 3
        cp = _remote(src.at[me], dst.at[3], send_s.at[k], recv_s.at[k], (x, y, 1 - c))
        sends.append(cp)
        steps.append((cp, None, None))
    return [], sends, steps


def _exchange_start(plan):
    local, sends, _ = plan
    for cp in local + sends:
        cp.start()


def _exchange_pass_on(plan):
    for arrival, pass_on, _ in plan[2]:
        arrival.wait_recv()
        if pass_on is not None:
            pass_on.start()


def _exchange_finish(plan):
    local, sends, steps = plan
    for _, pass_on, passed in steps:
        if pass_on is not None:
            passed.wait_recv()
    for cp in sends:
        cp.wait_send()
    for _, pass_on, _ in steps:
        if pass_on is not None:
            pass_on.wait_send()
    for cp in local:
        cp.wait()


def _exchange_call(arrays, copies, sems, out_shapes, name):
    n = len(arrays)

    def body(*refs):
        plan = copies(refs[:n], refs[n:2 * n], refs[2 * n:])
        _exchange_start(plan)
        _exchange_pass_on(plan)
        _exchange_finish(plan)

    return pl.pallas_call(
        body, name=name, in_specs=[HBM_REF] * n, out_specs=[HBM_REF] * n, out_shape=out_shapes,
        scratch_shapes=sems(n), compiler_params=pltpu.CompilerParams(has_side_effects=True),
    )(*arrays)


def _gather_call(shards, name):
    shapes = [jax.ShapeDtypeStruct((N_CHIPS,) + a.shape, a.dtype) for a in shards]
    return _exchange_call(shards, _gather_copies, _gather_sems, shapes, name)


def _scatter_call(slabs, name):
    shapes = [jax.ShapeDtypeStruct((N_DEV - 1,) + a.shape[1:], a.dtype) for a in slabs]
    return _exchange_call(slabs, _scatter_copies, _scatter_sems, shapes, name)


def _all_reduce_small(a):
    def body(in_ref, out_ref, slots, send_sems, recv_sems):
        x, y, c = lax.axis_index("x"), lax.axis_index("y"), lax.axis_index("c")
        me = 4 * x + 2 * y + c
        slots[0] = in_ref[...]
        sends = []
        for r in range(1, N_DEV):
            peer = (x ^ (r >> 2), y ^ ((r >> 1) & 1), c ^ (r & 1))
            sends.append(pltpu.make_async_remote_copy(src_ref=in_ref, dst_ref=slots.at[r], send_sem=send_sems.at[r],
                                                      recv_sem=recv_sems.at[r], device_id=peer, device_id_type=MESH))
        for cp in sends:
            cp.start()
        for cp in sends:
            cp.wait_recv()
        acc = slots[me]
        for dev in range(1, N_DEV):
            acc = acc + slots[dev ^ me]
        out_ref[...] = acc
        for cp in sends:
            cp.wait_send()

    return pl.pallas_call(
        body, name="small_all_reduce",
        in_specs=[pl.BlockSpec(memory_space=pltpu.VMEM)], out_specs=pl.BlockSpec(memory_space=pltpu.VMEM),
        out_shape=jax.ShapeDtypeStruct(a.shape, a.dtype),
        scratch_shapes=[pltpu.VMEM((N_DEV,) + a.shape, a.dtype), pltpu.SemaphoreType.DMA((N_DEV,)),
                        pltpu.SemaphoreType.DMA((N_DEV,))],
        compiler_params=pltpu.CompilerParams(has_side_effects=True),
    )(a)


SHARDED = ["w_in", "mla_w_q_up", "mla_w_kv_up", "w_out", "ffn_w_up", "ffn_w_down", "ple_w_gate", "ple_w_proj",
           "gdn_conv_w", "ffn_conv_w"]
SHARD_AXIS = {"w_in": 1, "mla_w_q_up": 1, "mla_w_kv_up": 1, "w_out": 0, "ffn_w_up": 1, "ffn_w_down": 0,
              "ple_w_gate": 0, "ple_w_proj": 1, "gdn_conv_w": 1, "ffn_conv_w": 1}
SMALL = ["gdn_a_log", "gdn_dt_bias", "gdn_norm_g", "mla_q_norm_g", "mla_kv_norm_g", "ln1_g", "ln1_b", "ffn_conv_b",
         "ple_b_gate", "ln2_g", "ln2_b"]
WEIGHTS = ["w_in", "gdn_conv_w", "gdn_a_log", "gdn_dt_bias", "gdn_norm_g", "mla_q_norm_g", "mla_w_q_up", "mla_kv_norm_g",
           "mla_w_kv_up", "w_out", "ln1_g", "ln1_b", "ffn_w_up", "ffn_conv_w", "ffn_conv_b", "ffn_w_down", "ple_w_gate",
           "ple_b_gate", "ple_w_proj", "ln2_g", "ln2_b"]
F32_ON_WIRE = ("gdn_conv_w", "ffn_conv_w")
GATHER_EARLY = ["w_in", "gdn_conv_w", "mla_w_q_up", "mla_w_kv_up"]
GATHER_LATE = ["w_out", "ffn_w_up", "ffn_conv_w", "ffn_w_down", "ple_w_gate", "ple_w_proj"]
SCATTER_EARLY = ["ffn_w_up", "ffn_conv_w", "ffn_w_down", "ple_w_gate", "ple_w_proj", "w_out"]
SCATTER_LATE = ["w_in", "gdn_conv_w", "mla_w_q_up", "mla_w_kv_up"]
PACK_COLS = 1024
PACK_ROW_TILE = 8


def _join_blocks(blocks, axis):
    n, a, b = blocks.shape
    if axis == 0:
        return blocks.reshape(n * a, b)
    return jnp.transpose(blocks, (1, 0, 2)).reshape(a, n * b)


def _split_blocks(full, axis):
    if axis == 0:
        return full.reshape(N_CHIPS, full.shape[0] // N_CHIPS, full.shape[1])
    a, nb = full.shape
    return jnp.transpose(full.reshape(a, N_CHIPS, nb // N_CHIPS), (1, 0, 2))


def _pack(arrays):
    flat = jnp.concatenate([a.reshape(-1) for a in arrays])
    quantum = PACK_COLS * PACK_ROW_TILE
    padded = -(-flat.shape[0] // quantum) * quantum
    return jnp.pad(flat, (0, padded - flat.shape[0])).reshape(-1, PACK_COLS)


def _unpack(packed, shapes):
    flat = packed.reshape(-1)
    out, off = [], 0
    for shp in shapes:
        n = int(np.prod(shp))
        out.append(flat[off:off + n].reshape(shp))
        off += n
    return out


def kernel(x, p, w_in, gdn_conv_w, gdn_a_log, gdn_dt_bias, gdn_norm_g, mla_q_norm_g, mla_w_q_up, mla_kv_norm_g, mla_w_kv_up, w_out, ln1_g, ln1_b, ffn_w_up, ffn_conv_w, ffn_conv_b, ffn_w_down, ple_w_gate, ple_b_gate, ple_w_proj, ln2_g, ln2_b, loss_target, m_w_in, m_gdn_conv_w, m_gdn_a_log, m_gdn_dt_bias, m_gdn_norm_g, m_mla_q_norm_g, m_mla_w_q_up, m_mla_kv_norm_g, m_mla_w_kv_up, m_w_out, m_ln1_g, m_ln1_b, m_ffn_w_up, m_ffn_conv_w, m_ffn_conv_b, m_ffn_w_down, m_ple_w_gate, m_ple_b_gate, m_ple_w_proj, m_ln2_g, m_ln2_b, v_w_in, v_gdn_conv_w, v_gdn_a_log, v_gdn_dt_bias, v_gdn_norm_g, v_mla_q_norm_g, v_mla_w_q_up, v_mla_kv_norm_g, v_mla_w_kv_up, v_w_out, v_ln1_g, v_ln1_b, v_ffn_w_up, v_ffn_conv_w, v_ffn_conv_b, v_ffn_w_down, v_ple_w_gate, v_ple_b_gate, v_ple_w_proj, v_ln2_g, v_ln2_b):
    given = dict(locals())
    wsh = {n: given[n][0] for n in WEIGHTS}
    msh = {n: given["m_" + n][0] for n in WEIGHTS}
    vsh = {n: given["v_" + n][0] for n in WEIGHTS}
    bl, s, _ = x.shape
    t = bl * s
    xt = x.reshape(t, D_MODEL)
    pt = p.reshape(t, PLE_DIM)
    target = loss_target.reshape(t, D_MODEL)

    wire = lambda n: wsh[n] if n in F32_ON_WIRE else wsh[n].astype(BF16)
    early = _gather_call([wire(n) for n in GATHER_EARLY], "weights_gather_early")
    full = {n: _join_blocks(g, SHARD_AXIS[n]) for n, g in zip(GATHER_EARLY, early)}
    late_shards = [wire(n) for n in GATHER_LATE]
    late_ride = (late_shards, _gather_copies, _gather_sems,
                 [jax.ShapeDtypeStruct((N_CHIPS,) + a.shape, a.dtype) for a in late_shards])

    in_cols, q_cols = _w_in_cols(), _w_q_cols()
    w_in_p = _pad_cols(full["w_in"], in_cols)
    w_q_p = _pad_cols(full["mla_w_q_up"], q_cols)
    w_kv, gconv = full["mla_w_kv_up"], full["gdn_conv_w"]
    row = lambda a: a.reshape(1, -1)
    sc = jnp.zeros((8, 128), F32).at[0, :GDN_HEADS].set(wsh["gdn_a_log"]).at[1, :GDN_HEADS].set(wsh["gdn_dt_bias"])
    norm_g, qg, kvg = row(wsh["gdn_norm_g"]), row(wsh["mla_q_norm_g"]), row(wsh["mla_kv_norm_g"])
    g1, b1, g2, b2 = row(wsh["ln1_g"]), row(wsh["ln1_b"]), row(wsh["ln2_g"]), row(wsh["ln2_b"])
    fbias, bgate = row(wsh["ffn_conv_b"]), row(wsh["ple_b_gate"])

    inv = ROPE_THETA ** (-jnp.arange(0, MLA_ROPE, 2, dtype=F32) / MLA_ROPE)
    ang = jnp.arange(s, dtype=F32)[:, None] * inv[None, :]
    zero = jnp.zeros_like(ang)
    cos_t = jnp.concatenate([jnp.cos(ang), zero, jnp.cos(ang), zero], axis=1)
    sin_t = jnp.concatenate([-jnp.sin(ang), zero, jnp.sin(ang), zero], axis=1)

    proj = _matmul(xt, w_in_p, name="proj")
    cat, o_raw, states, late = _gdn_fwd(proj, gconv, sc, norm_g, bl, s, late_ride)
    full.update({n: _join_blocks(g, SHARD_AXIS[n]) for n, g in zip(GATHER_LATE, late)})
    w_o, w_up, w_down = full["w_out"], full["ffn_w_up"], full["ffn_w_down"]
    w_gate, w_proj, fconv = full["ple_w_gate"], full["ple_w_proj"], full["ffn_conv_w"]
    qf, kvf, kr = _mla_prep_fwd(proj, qg, kvg, w_q_p, w_kv, cos_t, sin_t, s, 256)
    cat = _attn_fwd(qf, kvf, kr, cat, bl, s, 256)
    mix = _matmul(cat, w_o, name="mix")
    r1, h1 = _ln1_fwd(xt, mix, g1, b1, 256)
    u = _matmul(h1, w_up, name="ffn_up")
    act = _ffn_act_fwd(u, fconv, fbias, bl, s, 256)
    ffn = _matmul(act, w_down, name="ffn_down", tk=1408)
    gpre = _matmul(h1, w_gate, name="ple_gate")
    pp = _matmul(pt, w_proj, name="ple_proj")
    dr2, dgpre, dpp, loss_acc, dbgate, dg2, db2 = _head(h1, ffn, gpre, pp, bgate, g2, b2, target, 256)

    dact = _matmul(dr2, w_down, name="d_act", tb=True, tn=256)
    d_w_down = _matmul(act, dr2, name="dw_down", ta=True, tm=256)
    du, dfcw_g, dfcw_u, dfcb_g, dfcb_u = _ffn_act_bwd(u, fconv, fbias, dact, bl, s, 256)
    dh1_a = _matmul(du, w_up, name="dh1_ffn", tb=True, tk=1408, a_halves=True)
    dh1_b = _matmul(dgpre, w_gate, name="dh1_ple", tb=True)
    d_w_up = _matmul(h1, du, name="dw_up", ta=True, tn=1408, b_halves=True)
    d_w_gate = _matmul(h1, dgpre, name="dw_gate", ta=True)
    d_w_proj = _matmul(pt, dpp, name="dw_proj", ta=True)
    dr1, dg1, db1 = _ln1_bwd(r1, dr2, dh1_a, dh1_b, g1, b1, 256)
    dcat = _matmul(dr1, w_o, name="d_cat", tb=True)
    d_w_o = _matmul(cat, dr1, name="dw_out", ta=True)

    gfull = {
        "ffn_w_up": d_w_up, "ffn_w_down": d_w_down, "ple_w_gate": d_w_gate, "ple_w_proj": d_w_proj, "w_out": d_w_o,
        "ffn_conv_w": jnp.concatenate([jnp.sum(dfcw_g, 0), jnp.sum(dfcw_u, 0)], axis=1),
    }
    slabs = {n: _split_blocks(gfull[n], SHARD_AXIS[n]) for n in SCATTER_EARLY}
    early_slabs = [slabs[n] for n in SCATTER_EARLY]
    early_ride = (early_slabs, _scatter_copies, _scatter_sems,
                  [jax.ShapeDtypeStruct((N_DEV - 1,) + a.shape[1:], a.dtype) for a in early_slabs])
    dproj, dab, dcwq, dcwk, dcwv, dsc, dng, early_recv = _gdn_bwd(proj, gconv, sc, norm_g, o_raw, states, dcat, bl, s, early_ride)
    received = dict(zip(SCATTER_EARLY, early_recv))
    dqf, dkvf, dkr = _attn_bwd(qf, kvf, kr, dcat, bl, s, 256)
    dproj, dqg, dkvg, d_w_q_p, d_w_kv = _mla_prep_bwd(proj, qg, kvg, w_q_p, w_kv, cos_t, sin_t, dqf, dkvf, dkr, dab, dproj, s, 256)
    grad_x = _matmul(dproj, w_in_p, name="d_x", tb=True, add=dr1, add_scale=ALPHA)
    d_w_in_p = _matmul(xt, dproj, name="dw_in", ta=True)

    gfull.update({
        "w_in": _unpad_cols(d_w_in_p, in_cols, D_IN),
        "mla_w_q_up": _unpad_cols(d_w_q_p, q_cols, MLA_HEADS * (MLA_NOPE + MLA_ROPE)),
        "mla_w_kv_up": d_w_kv,
        "gdn_conv_w": jnp.concatenate([jnp.sum(dcwq, 0), jnp.sum(dcwk, 0), jnp.sum(dcwv, 0)], axis=1),
    })
    dsc_sum = jnp.sum(dsc, axis=(0, 1))
    gsmall = {
        "gdn_a_log": dsc_sum[0, :GDN_HEADS], "gdn_dt_bias": dsc_sum[1, :GDN_HEADS],
        "gdn_norm_g": jnp.sum(dng[:, :, 0, :], axis=(0, 1)),
        "mla_q_norm_g": dqg[0], "mla_kv_norm_g": dkvg[0], "ln1_g": dg1[0], "ln1_b": db1[0],
        "ffn_conv_b": jnp.concatenate([jnp.sum(dfcb_g, 0), jnp.sum(dfcb_u, 0)], axis=1)[0],
        "ple_b_gate": dbgate[0], "ln2_g": dg2[0], "ln2_b": db2[0],
    }

    slabs.update({n: _split_blocks(gfull[n], SHARD_AXIS[n]) for n in SCATTER_LATE})
    me_chip = 2 * lax.axis_index("x") + lax.axis_index("y")
    received.update(zip(SCATTER_LATE, _scatter_call([slabs[n] for n in SCATTER_LATE], "grads_scatter_late")))
    big = [{}, {}, {}, {}]
    for n in SHARDED:
        own = lax.dynamic_index_in_dim(slabs[n], me_chip, 0, keepdims=False)
        for kind, val in enumerate(_adamw_reduced(own, received[n], wsh[n], msh[n], vsh[n], "adamw_" + n)):
            big[kind][n] = val

    small_shapes = [wsh[n].shape for n in SMALL]
    gsum = _all_reduce_small(_pack([gsmall[n] for n in SMALL]))
    spacks = _adamw_small(gsum, _pack([wsh[n] for n in SMALL]), _pack([msh[n] for n in SMALL]), _pack([vsh[n] for n in SMALL]))
    small = [dict(zip(SMALL, _unpack(pk, small_shapes))) for pk in spacks]

    loss = lax.psum(loss_acc[0, 0], ("x", "y", "c"))
    outs = [loss, grad_x.reshape(x.shape)]
    for kind in range(4):
        for n in WEIGHTS:
            val = big[kind][n] if n in big[kind] else small[kind][n]
            outs.append(val[None])
    return tuple(outs)
```

```python
import functools
import math

import numpy as np
import jax
import jax.numpy as jnp
from jax import lax
from jax.experimental import pallas as pl
from jax.experimental.pallas import tpu as pltpu

F32 = jnp.float32
BF16 = jnp.bfloat16

D_MODEL = 1024
CHUNK = 64
PLE_DIM = 256
GDN_HEADS = 4
GDN_DK = 128
GDN_DV = 128
GDN_CONV = 4
MLA_HEADS = 4
MLA_NOPE = 128
MLA_ROPE = 64
MLA_V = 128
MLA_Q_LORA = 384
MLA_KV_LORA = 256
ROPE_THETA = 10000.0
D_FF = 2816
FFN_CONV = 3
DEPTH = 1
ALPHA = (2.0 * DEPTH) ** 0.25
NORM_EPS = 1e-6
GDN_QK = GDN_HEADS * GDN_DK
GDN_VW = GDN_HEADS * GDN_DV
D_IN = 2 * GDN_QK + 2 * GDN_VW + 2 * GDN_HEADS + MLA_Q_LORA + MLA_KV_LORA + MLA_ROPE
ATT_SCALE = (MLA_NOPE + MLA_ROPE) ** -0.5

ADAM_LR = 0.001
ADAM_B1 = 0.9
ADAM_B2 = 0.999
ADAM_EPS = 1e-08
ADAM_WD = 0.01
ADAM_STEP = 10

LANES = 128
VMEM_LIMIT = 60 * 1024 * 1024
GDN_FWD_GROUP = 16
GDN_BWD_GROUP = 16
N_CHIPS = 4
N_DEV = 8

P_WIDTH = 3072
P_MLA = 2048
MESH = pl.DeviceIdType.MESH


def _rope_slot(j):
    return j if j < MLA_ROPE // 2 else 64 + (j - MLA_ROPE // 2)


def _w_in_cols():
    idx = -np.ones((P_WIDTH,), np.int64)
    for h in range(GDN_HEADS):
        base = h * 512
        idx[base:base + 128] = np.arange(128) + h * GDN_DK
        idx[base + 128:base + 256] = np.arange(128) + GDN_QK + h * GDN_DK
        idx[base + 256:base + 384] = np.arange(128) + 2 * GDN_QK + h * GDN_DV
        idx[base + 384:base + 512] = np.arange(128) + 2 * GDN_QK + GDN_VW + h * GDN_DV
    o_a = 2 * GDN_QK + 2 * GDN_VW
    idx[P_MLA:P_MLA + 2 * GDN_HEADS] = np.arange(2 * GDN_HEADS) + o_a
    o_cq = o_a + 2 * GDN_HEADS
    idx[P_MLA + 128:P_MLA + 512] = np.arange(MLA_Q_LORA) + o_cq
    o_ckv = o_cq + MLA_Q_LORA
    idx[P_MLA + 512:P_MLA + 768] = np.arange(MLA_KV_LORA) + o_ckv
    o_kr = o_ckv + MLA_KV_LORA
    for j in range(MLA_ROPE):
        idx[P_MLA + 768 + _rope_slot(j)] = o_kr + j
    return idx


def _w_q_cols():
    idx = -np.ones((MLA_HEADS * 256,), np.int64)
    for h in range(MLA_HEADS):
        o = h * (MLA_NOPE + MLA_ROPE)
        idx[h * 256:h * 256 + 128] = np.arange(128) + o
        for j in range(MLA_ROPE):
            idx[h * 256 + 128 + _rope_slot(j)] = o + MLA_NOPE + j
    return idx


def _pad_cols(w, idx):
    safe = np.where(idx >= 0, idx, 0)
    return jnp.where(jnp.asarray(idx >= 0)[None, :], w[:, safe], 0.0)


def _unpad_cols(wp, idx, n):
    inv = np.zeros((n,), np.int64)
    inv[idx[idx >= 0]] = np.nonzero(idx >= 0)[0]
    return wp[:, inv]


def _dot(a, b, ca, cb, precision=None):
    if precision is None:
        a = a.astype(BF16)
        b = b.astype(BF16)
    return lax.dot_general(a, b, (((ca,), (cb,)), ((), ())), preferred_element_type=F32, precision=precision)


@jax.custom_vjp
def mm(a, b):
    return _dot(a, b, 1, 0)


@jax.custom_vjp
def mm_nt(a, b):
    return _dot(a, b, 1, 1)


@jax.custom_vjp
def mm_tn(a, b):
    return _dot(a, b, 0, 0)


mm.defvjp(lambda a, b: (mm(a, b), (a, b)), lambda r, g: (mm_nt(g, r[1]), mm_tn(r[0], g)))
mm_nt.defvjp(lambda a, b: (mm_nt(a, b), (a, b)), lambda r, g: (mm(g, r[1]), mm_tn(g, r[0])))
mm_tn.defvjp(lambda a, b: (mm_tn(a, b), (a, b)), lambda r, g: (mm_nt(r[1], g), mm(r[0], g)))

def _split(a):
    hi = a.astype(BF16)
    return hi, (a - hi.astype(F32)).astype(BF16)


def _dot3(a, b, ca, cb):
    a_hi, a_lo = _split(a)
    b_hi, b_lo = _split(b)
    return (_dot(a_hi, b_hi, ca, cb) + _dot(a_hi, b_lo, ca, cb)) + _dot(a_lo, b_hi, ca, cb)


def _unit_lower_inverse(low):
    n = low.shape[0]
    ii = lax.broadcasted_iota(jnp.int32, (n, n), 0)
    jj = lax.broadcasted_iota(jnp.int32, (n, n), 1)
    inv = jnp.where(ii == jj, 1.0, 0.0) - low
    power = _dot3(low, low, 1, 0)
    k = 2
    while k < n:
        inv = inv + _dot3(inv, power, 1, 0)
        k *= 2
        if k < n:
            power = _dot3(power, power, 1, 0)
    return inv


@jax.custom_vjp
def unit_lower_solve(low, rhs):
    return _dot3(_unit_lower_inverse(low), rhs, 1, 0)


def _uls_fwd(low, rhs):
    inv = _unit_lower_inverse(low)
    x = _dot3(inv, rhs, 1, 0)
    return x, (inv, x)


def _uls_bwd(res, dx):
    inv, x = res
    drhs = _dot3(inv, dx, 0, 0)
    n = inv.shape[0]
    ii = lax.broadcasted_iota(jnp.int32, (n, n), 0)
    jj = lax.broadcasted_iota(jnp.int32, (n, n), 1)
    dlow = jnp.where(ii > jj, -_dot3(drhs, x, 1, 1), 0.0)
    return dlow, drhs


unit_lower_solve.defvjp(_uls_fwd, _uls_bwd)


def _shift_rows(x, s):
    if s == 0:
        return x
    n = x.shape[0]
    row = lax.broadcasted_iota(jnp.int32, x.shape, 0)
    rolled = pltpu.roll(x, s % n, 0)
    keep = (row >= s) if s > 0 else (row < n + s)
    return jnp.where(keep, rolled, 0.0)


def _row(w, j):
    tap = lax.broadcasted_iota(jnp.int32, w.shape, 0)
    return jnp.sum(jnp.where(tap == j, w, 0.0), axis=0, keepdims=True)


@jax.custom_vjp
def dwconv(x, w):
    k = w.shape[0]
    y = _row(w, k - 1) * x
    for j in range(k - 1):
        y = y + _row(w, j) * _shift_rows(x, k - 1 - j)
    return y


def _dwconv_fwd(x, w):
    return dwconv(x, w), (x, w)


def _dwconv_bwd(res, dy):
    x, w = res
    k = w.shape[0]
    dx = _row(w, k - 1) * dy
    tap = lax.broadcasted_iota(jnp.int32, w.shape, 0)
    dw = jnp.where(tap == k - 1, jnp.sum(dy * x, axis=0, keepdims=True), 0.0)
    for j in range(k - 1):
        dx = dx + _row(w, j) * _shift_rows(dy, -(k - 1 - j))
        dw = dw + jnp.where(tap == j, jnp.sum(dy * _shift_rows(x, k - 1 - j), axis=0, keepdims=True), 0.0)
    return dx, dw


dwconv.defvjp(_dwconv_fwd, _dwconv_bwd)


@jax.custom_vjp
def rope128(x, cos, sin):
    return x * cos + pltpu.roll(x, 64, 1) * sin


rope128.defvjp(lambda x, c, s: (rope128(x, c, s), (c, s)),
               lambda r, g: (g * r[0] + pltpu.roll(g * r[1], 64, 1), jnp.zeros_like(r[0]), jnp.zeros_like(r[1])))


def _silu(x):
    return x * jax.nn.sigmoid(x)


def _softplus(x):
    return jnp.maximum(x, 0.0) + jnp.log(1.0 + jnp.exp(-jnp.abs(x)))


def _rmsnorm(x, g):
    return x * lax.rsqrt(jnp.mean(x * x, axis=-1, keepdims=True) + NORM_EPS) * g


def _layernorm(x, g, b):
    mu = jnp.mean(x, axis=-1, keepdims=True)
    xc = x - mu
    var = jnp.mean(xc * xc, axis=-1, keepdims=True)
    return xc * lax.rsqrt(var + NORM_EPS) * g + b


def _pick_lane(row, lane):
    idx = lax.broadcasted_iota(jnp.int32, row.shape, 1)
    return jnp.sum(jnp.where(idx == lane, row, 0.0), axis=1, keepdims=True)


def _gdn_q(pq, cw):
    h = _silu(dwconv(pq, cw))
    return h * lax.rsqrt(jnp.sum(h * h, axis=-1, keepdims=True) + NORM_EPS) * (GDN_DK ** -0.5)


def _gdn_k(pk, cw):
    h = _silu(dwconv(pk, cw))
    return h * lax.rsqrt(jnp.sum(h * h, axis=-1, keepdims=True) + NORM_EPS)


def _gdn_v(pv, cw):
    return _silu(dwconv(pv, cw))


def _gdn_gate(ab, sc, head):
    a = _pick_lane(ab, head)
    b = _pick_lane(ab, GDN_HEADS + head)
    a_log = _pick_lane(_row(sc, 0), head)
    dt_bias = _pick_lane(_row(sc, 1), head)
    beta = jax.nn.sigmoid(b)
    g = -jnp.exp(a_log) * _softplus(a + dt_bias)
    return _two_lanes(g, beta)


def _two_lanes(c0, c1):
    lane = lax.broadcasted_iota(jnp.int32, (c0.shape[0], LANES), 1)
    return jnp.where(lane == 0, c0, jnp.where(lane == 1, c1, 0.0))


def _inverse_group(lows):
    n = lows[0].shape[0]
    ii = lax.broadcasted_iota(jnp.int32, (n, n), 0)
    jj = lax.broadcasted_iota(jnp.int32, (n, n), 1)
    eye = jnp.where(ii == jj, 1.0, 0.0)
    invs = [eye - low for low in lows]
    powers = [_dot3(low, low, 1, 0) for low in lows]
    k = 2
    while k < n:
        invs = [inv + _dot3(inv, p, 1, 0) for inv, p in zip(invs, powers)]
        k *= 2
        if k < n:
            powers = [_dot3(p, p, 1, 0) for p in powers]
    return invs


@jax.custom_vjp
def solve_group(lows, rhss):
    return [_dot3(inv, rhs, 1, 0) for inv, rhs in zip(_inverse_group(lows), rhss)]


def _solve_group_fwd(lows, rhss):
    invs = _inverse_group(lows)
    xs = [_dot3(inv, rhs, 1, 0) for inv, rhs in zip(invs, rhss)]
    return xs, (invs, xs)


def _solve_group_bwd(res, dxs):
    invs, xs = res
    n = invs[0].shape[0]
    strict = lax.broadcasted_iota(jnp.int32, (n, n), 0) > lax.broadcasted_iota(jnp.int32, (n, n), 1)
    drhss = [_dot3(inv, dx, 0, 0) for inv, dx in zip(invs, dxs)]
    dlows = [jnp.where(strict, -_dot3(drhs, x, 1, 1), 0.0) for drhs, x in zip(drhss, xs)]
    return dlows, drhss


solve_group.defvjp(_solve_group_fwd, _solve_group_bwd)


def _gdn_local_group(qs, ks, vs, gbs):
    c = qs[0].shape[0]
    ii = lax.broadcasted_iota(jnp.int32, (c, c), 0)
    jj = lax.broadcasted_iota(jnp.int32, (c, c), 1)
    incl = ii >= jj
    gs = [_pick_lane(gb, 0) for gb in gbs]
    betas = [_pick_lane(gb, 1) for gb in gbs]
    g_rows = [jnp.sum(jnp.where(ii == jj, g, 0.0), axis=0, keepdims=True) for g in gs]
    gc_cols = [jnp.sum(jnp.where(incl, g_row, 0.0), axis=1, keepdims=True) for g_row in g_rows]
    gc_rows = [jnp.sum(jnp.where(jj >= ii, g, 0.0), axis=0, keepdims=True) for g in gs]
    decays = [jnp.where(incl, jnp.exp(jnp.where(incl, gc - gr, 0.0)), 0.0) for gc, gr in zip(gc_cols, gc_rows)]
    kbs = [k * beta for k, beta in zip(ks, betas)]
    lows = [jnp.where(ii > jj, mm_nt(kb, k) * decay, 0.0) for kb, k, decay in zip(kbs, ks, decays)]
    egs = [jnp.exp(gc) for gc in gc_cols]
    wus = solve_group(lows, [jnp.concatenate([kb * eg, v * beta], axis=1) for kb, eg, v, beta in zip(kbs, egs, vs, betas)])
    qks = [mm_nt(q, k) * decay for q, k, decay in zip(qs, ks, decays)]
    g_lasts = [jnp.sum(g_row, axis=1, keepdims=True) for g_row in g_rows]
    kds = [k * jnp.exp(gl - gc) for k, gl, gc in zip(ks, g_lasts, gc_cols)]
    return [(wu[:, :GDN_DK], wu[:, GDN_DK:], qk, q * eg, kd, jnp.exp(gl))
            for wu, qk, q, eg, kd, gl in zip(wus, qks, qs, egs, kds, g_lasts)]


def _gdn_state_step(w, u, qk, qg, kd, eg_last, state):
    v_new = u - mm(w, state)
    o = mm(qg, state) + mm(qk, v_new)
    return o, state * eg_last + mm_tn(kd, v_new)


def _gdn_chunk_packed(q, k, v, gb, state):
    return _gdn_chunk(q, k, v, _pick_lane(gb, 0), _pick_lane(gb, 1), state)


def _gdn_post(o, z, norm_g):
    return _rmsnorm(o, norm_g) * _silu(z)


def _gdn_chunk(q, k, v, g, beta, state):
    c = q.shape[0]
    ii = lax.broadcasted_iota(jnp.int32, (c, c), 0)
    jj = lax.broadcasted_iota(jnp.int32, (c, c), 1)
    incl = ii >= jj
    g_row = jnp.sum(jnp.where(ii == jj, g, 0.0), axis=0, keepdims=True)
    gc_col = jnp.sum(jnp.where(incl, g_row, 0.0), axis=1, keepdims=True)
    gc_row = jnp.sum(jnp.where(jj >= ii, g, 0.0), axis=0, keepdims=True)
    decay = jnp.where(incl, jnp.exp(jnp.where(incl, gc_col - gc_row, 0.0)), 0.0)
    kb = k * beta
    low = jnp.where(ii > jj, mm_nt(kb, k) * decay, 0.0)
    eg = jnp.exp(gc_col)
    wu = unit_lower_solve(low, jnp.concatenate([kb * eg, v * beta], axis=1))
    w, u = wu[:, :GDN_DK], wu[:, GDN_DK:]
    qk = mm_nt(q, k) * decay
    g_last = jnp.sum(g_row, axis=1, keepdims=True)
    kd = k * jnp.exp(g_last - gc_col)
    v_new = u - mm(w, state)
    o = mm(q * eg, state) + mm(qk, v_new)
    new_state = state * jnp.exp(g_last) + mm_tn(kd, v_new)
    return o, new_state


def _attn_block(q, kn, kr, v, q0):
    s = (mm_nt(q[:, :128], kn) + mm_nt(q[:, 128:], kr)) * ATT_SCALE
    qpos = q0 + lax.broadcasted_iota(jnp.int32, s.shape, 0)
    kpos = lax.broadcasted_iota(jnp.int32, s.shape, 1)
    shift = int(math.log2(CHUNK))
    allowed = (kpos >> shift) <= (qpos >> shift)
    s = jnp.where(allowed, s, -1e30)
    p = jnp.exp(s - jnp.max(s, axis=-1, keepdims=True))
    p = p / jnp.sum(p, axis=-1, keepdims=True)
    return mm(p, v)


def _mla_prep(pm, qg, kvg, wq, wkv, cos, sin):
    cq = pm[:, 128:512]
    ckv = pm[:, 512:768]
    qf = mm(_rmsnorm(cq, qg), wq)
    parts = []
    for h in range(MLA_HEADS):
        parts.append(qf[:, h * 256:h * 256 + 128])
        parts.append(rope128(qf[:, h * 256 + 128:h * 256 + 256], cos, sin))
    kvf = mm(_rmsnorm(ckv, kvg), wkv)
    return jnp.concatenate(parts, axis=1), kvf, rope128(pm[:, 768:896], cos, sin)


def _ffn_act(ug, uu, wg, wu, bg, bu):
    return _silu(dwconv(ug, wg) + bg) * (dwconv(uu, wu) + bu)


def _head_loss(h1, ffn, gpre, pp, bgate, g2, b2, target):
    gate = jax.nn.sigmoid(gpre + bgate)
    h2 = _layernorm(ALPHA * h1 + ffn + gate * pp, g2, b2)
    err = h2 - target
    return 0.5 * jnp.sum(jnp.sum(err * err, axis=1, keepdims=True), axis=0, keepdims=True) / D_MODEL


def _params(sem):
    return pltpu.CompilerParams(dimension_semantics=sem, vmem_limit_bytes=VMEM_LIMIT)


def _matmul(a, b, *, name, ta=False, tb=False, tm=512, tn=512, tk=1024, add=None, add_scale=1.0,
            a_halves=False, b_halves=False):
    assert not (a_halves and ta) and not (b_halves and tb)
    a_shape = (a.shape[1], 2 * a.shape[2]) if a_halves else a.shape
    b_shape = (b.shape[1], 2 * b.shape[2]) if b_halves else b.shape
    (k_dim, m) = a_shape if ta else a_shape[::-1]
    (n, k2) = b_shape if tb else b_shape[::-1]
    assert k_dim == k2, (a.shape, b.shape)
    tm, tn, tk = min(tm, m), min(tn, n), min(tk, k_dim)
    assert m % tm == 0 and n % tn == 0 and k_dim % tk == 0, (name, m, n, k_dim, tm, tn, tk)
    nk = k_dim // tk
    ca, cb = (0 if ta else 1), (1 if tb else 0)

    def body(*refs):
        if add is None:
            a_ref, b_ref, o_ref, acc = refs
        else:
            a_ref, b_ref, c_ref, o_ref, acc = refs
        kk = pl.program_id(2)

        @pl.when(kk == 0)
        def _():
            acc[...] = jnp.zeros_like(acc)

        acc[...] += _dot(a_ref[...], b_ref[...], ca, cb)

        @pl.when(kk == nk - 1)
        def _():
            r = acc[...]
            if add is not None:
                r = r + add_scale * c_ref[...]
            o_ref[...] = r

    a_spec = pl.BlockSpec((tk, tm), lambda i, j, k: (k, i)) if ta else pl.BlockSpec((tm, tk), lambda i, j, k: (i, k))
    b_spec = pl.BlockSpec((tn, tk), lambda i, j, k: (j, k)) if tb else pl.BlockSpec((tk, tn), lambda i, j, k: (k, j))
    if a_halves:
        kh = k_dim // 2 // tk
        assert kh * tk * 2 == k_dim
        a_spec = pl.BlockSpec((None, tm, tk), lambda i, j, k: (k // kh, i, k % kh))
    if b_halves:
        nh = n // 2 // tn
        assert nh * tn * 2 == n
        b_spec = pl.BlockSpec((None, tk, tn), lambda i, j, k: (j // nh, k, j % nh))
    in_specs = [a_spec, b_spec]
    args = [a, b]
    if add is not None:
        in_specs.append(pl.BlockSpec((tm, tn), lambda i, j, k: (i, j)))
        args.append(add)
    return pl.pallas_call(
        body, name=name, grid=(m // tm, n // tn, nk),
        in_specs=in_specs, out_specs=pl.BlockSpec((tm, tn), lambda i, j, k: (i, j)),
        out_shape=jax.ShapeDtypeStruct((m, n), F32),
        scratch_shapes=[pltpu.VMEM((tm, tn), F32)],
        compiler_params=_params(("parallel", "parallel", "arbitrary")),
    )(*args)


def _riding(core, n_in, n_out, n_scratch, ride, inner, steps):
    if ride is None:
        return core
    copies, nr = ride[1], len(ride[0])

    def body(*refs):
        cuts = np.cumsum([0, n_in, nr, n_out, nr, n_scratch])
        ins, rin, outs, rout, scratch = (refs[a:b] for a, b in zip(cuts[:-1], cuts[1:]))
        sems = refs[cuts[-1]:]
        step = pl.program_id(0) * inner + pl.program_id(1)

        @pl.when(step == 0)
        def _():
            _exchange_start(copies(rin, rout, sems))

        @pl.when(step == steps // 2)
        def _():
            _exchange_pass_on(copies(rin, rout, sems))

        core(*ins, *outs, *scratch)

        @pl.when(step == steps - 1)
        def _():
            _exchange_finish(copies(rin, rout, sems))

    return body


def _ride_specs(ride):
    if ride is None:
        return [], [], [], [], []
    arrays, _, sems, shapes = ride
    return [HBM_REF] * len(arrays), [HBM_REF] * len(arrays), list(shapes), sems(len(arrays)), list(arrays)


def _gdn_fwd(proj, conv_w, sc, norm_g, bl, s, ride=None):
    nc = s // CHUNK

    def core(ph_ref, ab_ref, cwq_ref, cwk_ref, cwv_ref, sc_ref, ng_ref, cat_ref, o_ref, st_ref, q_s, k_s, v_s, gb_s):
        q_s[...] = _gdn_q(ph_ref[:, 0:128], cwq_ref[...])
        k_s[...] = _gdn_k(ph_ref[:, 128:256], cwk_ref[...])
        v_s[...] = _gdn_v(ph_ref[:, 256:384], cwv_ref[...])
        gb_s[...] = _gdn_gate(ab_ref[...], sc_ref[...], pl.program_id(1))

        group = math.gcd(nc, GDN_FWD_GROUP)

        def chunks(i, state):
            ns = [i * group + j for j in range(group)]
            rows = [pl.ds(pl.multiple_of(n * CHUNK, CHUNK), CHUNK) for n in ns]
            local = _gdn_local_group([q_s[r, :] for r in rows], [k_s[r, :] for r in rows], [v_s[r, :] for r in rows],
                                     [gb_s[r, :] for r in rows])
            for n, r, loc in zip(ns, rows, local):
                st_ref[n] = state
                o_ref[r, :], state = _gdn_state_step(*loc, state)
            return state

        lax.fori_loop(0, nc // group, chunks, jnp.zeros((GDN_DK, GDN_DV), F32))
        cat_ref[...] = _gdn_post(o_ref[...], ph_ref[:, 384:512], ng_ref[...])

    t = bl * s
    r_in, r_out, r_shapes, r_sems, r_args = _ride_specs(ride)
    outs = pl.pallas_call(
        _riding(core, 7, 3, 4, ride, GDN_HEADS, bl * GDN_HEADS), name="gdn_fwd", grid=(bl, GDN_HEADS),
        in_specs=[
            pl.BlockSpec((s, 512), lambda b, h: (b, h)),
            pl.BlockSpec((s, 128), lambda b, h: (b, P_MLA // 128)),
            pl.BlockSpec((GDN_CONV, 128), lambda b, h: (0, h)),
            pl.BlockSpec((GDN_CONV, 128), lambda b, h: (0, GDN_HEADS + h)),
            pl.BlockSpec((GDN_CONV, 128), lambda b, h: (0, 2 * GDN_HEADS + h)),
            pl.BlockSpec((8, 128), lambda b, h: (0, 0)),
            pl.BlockSpec((1, 128), lambda b, h: (0, 0)),
        ] + r_in,
        out_specs=[
            pl.BlockSpec((s, 128), lambda b, h: (b, h)),
            pl.BlockSpec((s, 128), lambda b, h: (b, h)),
            pl.BlockSpec((None, None, nc, GDN_DK, GDN_DV), lambda b, h: (b, h, 0, 0, 0)),
        ] + r_out,
        out_shape=[
            jax.ShapeDtypeStruct((t, 2 * GDN_VW), F32),
            jax.ShapeDtypeStruct((t, GDN_VW), F32),
            jax.ShapeDtypeStruct((bl, GDN_HEADS, nc, GDN_DK, GDN_DV), F32),
        ] + r_shapes,
        scratch_shapes=[pltpu.VMEM((s, 128), F32)] * 4 + r_sems,
        compiler_params=_params(("arbitrary", "arbitrary")),
    )(proj, proj, conv_w, conv_w, conv_w, sc, norm_g, *r_args)
    return outs[0], outs[1], outs[2], list(outs[3:])


def _gdn_bwd(proj, conv_w, sc, norm_g, o_raw, states, dcat, bl, s, ride=None):
    nc = s // CHUNK

    def core(ph_ref, ab_ref, cwq_ref, cwk_ref, cwv_ref, sc_ref, ng_ref, o_ref, st_ref, dc_ref,
             dph_ref, dab_ref, dcwq_ref, dcwk_ref, dcwv_ref, dsc_ref, dng_ref, q_s, k_s, v_s, gb_s, do_s):
        head = pl.program_id(1)
        gate = functools.partial(_gdn_gate, head=head)
        paths = [(_gdn_q, 0, cwq_ref, q_s, dcwq_ref), (_gdn_k, 128, cwk_ref, k_s, dcwk_ref), (_gdn_v, 256, cwv_ref, v_s, dcwv_ref)]
        for fn, col, cw_ref, val_s, _ in paths:
            val_s[...] = fn(ph_ref[:, col:col + 128], cw_ref[...])
        gb_s[...] = gate(ab_ref[...], sc_ref[...])
        _, post_vjp = jax.vjp(_gdn_post, o_ref[...], ph_ref[:, 384:512], ng_ref[...])
        d_o, dz, dng = post_vjp(dc_ref[...])
        do_s[...] = d_o
        dph_ref[:, 384:512] = dz
        dng_ref[...] = jnp.broadcast_to(dng, dng_ref.shape)

        group = math.gcd(nc, GDN_BWD_GROUP)

        def chunks(i, dstate):
            ns = [nc - 1 - (i * group + j) for j in range(group)]
            rows = [pl.ds(pl.multiple_of(n * CHUNK, CHUNK), CHUNK) for n in ns]
            local, local_vjp = jax.vjp(_gdn_local_group, [q_s[r, :] for r in rows], [k_s[r, :] for r in rows],
                                       [v_s[r, :] for r in rows], [gb_s[r, :] for r in rows])
            d_os = [do_s[r, :] for r in rows]
            dlocal = []
            for n, loc, d_o in zip(ns, local, d_os):
                _, step_vjp = jax.vjp(_gdn_state_step, *loc, st_ref[n])
                *dloc, dstate = step_vjp((d_o, dstate))
                dlocal.append(tuple(dloc))
            dqs, dks, dvs, dgbs = local_vjp(dlocal)
            for r, dq, dk, dv, dgb in zip(rows, dqs, dks, dvs, dgbs):
                q_s[r, :], k_s[r, :], v_s[r, :], gb_s[r, :] = dq, dk, dv, dgb
            return dstate

        lax.fori_loop(0, nc // group, chunks, jnp.zeros((GDN_DK, GDN_DV), F32))
        for fn, col, cw_ref, val_s, dcw_ref in paths:
            _, vjp = jax.vjp(fn, ph_ref[:, col:col + 128], cw_ref[...])
            dph_ref[:, col:col + 128], dcw_ref[...] = vjp(val_s[...])
        _, gate_vjp = jax.vjp(gate, ab_ref[...], sc_ref[...])
        dab, dsc_ref[...] = gate_vjp(gb_s[...])

        @pl.when(head == 0)
        def _():
            dab_ref[...] = jnp.zeros_like(dab_ref)

        dab_ref[...] += dab

    t = bl * s
    cw_out = pl.BlockSpec((None, GDN_CONV, 128), lambda b, h: (b, 0, h))
    part = pl.BlockSpec((None, None, 8, 128), lambda b, h: (b, h, 0, 0))
    r_in, r_out, r_shapes, r_sems, r_args = _ride_specs(ride)
    outs = pl.pallas_call(
        _riding(core, 10, 7, 5, ride, GDN_HEADS, bl * GDN_HEADS), name="gdn_bwd", grid=(bl, GDN_HEADS),
        in_specs=[
            pl.BlockSpec((s, 512), lambda b, h: (b, h)),
            pl.BlockSpec((s, 128), lambda b, h: (b, P_MLA // 128)),
            pl.BlockSpec((GDN_CONV, 128), lambda b, h: (0, h)),
            pl.BlockSpec((GDN_CONV, 128), lambda b, h: (0, GDN_HEADS + h)),
            pl.BlockSpec((GDN_CONV, 128), lambda b, h: (0, 2 * GDN_HEADS + h)),
            pl.BlockSpec((8, 128), lambda b, h: (0, 0)),
            pl.BlockSpec((1, 128), lambda b, h: (0, 0)),
            pl.BlockSpec((s, 128), lambda b, h: (b, h)),
            pl.BlockSpec((None, None, nc, GDN_DK, GDN_DV), lambda b, h: (b, h, 0, 0, 0)),
            pl.BlockSpec((s, 128), lambda b, h: (b, h)),
        ] + r_in,
        out_specs=[
            pl.BlockSpec((s, 512), lambda b, h: (b, h)),
            pl.BlockSpec((s, 128), lambda b, h: (b, 0)),
            cw_out, cw_out, cw_out, part, part,
        ] + r_out,
        out_shape=[
            jax.ShapeDtypeStruct((t, P_WIDTH), F32),
            jax.ShapeDtypeStruct((t, 128), F32),
            jax.ShapeDtypeStruct((bl, GDN_CONV, 512), F32),
            jax.ShapeDtypeStruct((bl, GDN_CONV, 512), F32),
            jax.ShapeDtypeStruct((bl, GDN_CONV, 512), F32),
            jax.ShapeDtypeStruct((bl, GDN_HEADS, 8, 128), F32),
            jax.ShapeDtypeStruct((bl, GDN_HEADS, 8, 128), F32),
        ] + r_shapes,
        scratch_shapes=[pltpu.VMEM((s, 128), F32)] * 5 + r_sems,
        compiler_params=_params(("arbitrary", "arbitrary")),
    )(proj, proj, conv_w, conv_w, conv_w, sc, norm_g, o_raw, states, dcat, *r_args)
    return tuple(outs[:7]) + (list(outs[7:]),)


def _mla_prep_fwd(proj, qg, kvg, wq, wkv, cos, sin, s, tm):
    t = proj.shape[0]
    tm = min(tm, s)
    nps = s // tm
    const = lambda shape: pl.BlockSpec(shape, lambda i: (0, 0))

    def body(pm_ref, qg_ref, kvg_ref, wq_ref, wkv_ref, cos_ref, sin_ref, qf_ref, kvf_ref, kr_ref):
        qf, kvf, kr = _mla_prep(pm_ref[...], qg_ref[...], kvg_ref[...], wq_ref[...], wkv_ref[...], cos_ref[...], sin_ref[...])
        qf_ref[...], kvf_ref[...], kr_ref[...] = qf, kvf, kr

    return pl.pallas_call(
        body, name="mla_prep_fwd", grid=(t // tm,),
        in_specs=[
            pl.BlockSpec((tm, 1024), lambda i: (i, P_MLA // 1024)),
            const((1, MLA_Q_LORA)), const((1, MLA_KV_LORA)), const(wq.shape), const(wkv.shape),
            pl.BlockSpec((tm, 128), lambda i: (i % nps, 0)), pl.BlockSpec((tm, 128), lambda i: (i % nps, 0)),
        ],
        out_specs=[pl.BlockSpec((tm, 1024), lambda i: (i, 0)), pl.BlockSpec((tm, 1024), lambda i: (i, 0)),
                   pl.BlockSpec((tm, 128), lambda i: (i, 0))],
        out_shape=[jax.ShapeDtypeStruct((t, 1024), F32), jax.ShapeDtypeStruct((t, 1024), F32),
                   jax.ShapeDtypeStruct((t, 128), F32)],
        compiler_params=_params(("parallel",)),
    )(proj, qg, kvg, wq, wkv, cos, sin)


def _mla_prep_bwd(proj, qg, kvg, wq, wkv, cos, sin, dqf, dkvf, dkr, dab, dproj, s, tm):
    t = proj.shape[0]
    tm = min(tm, s)
    nps = s // tm
    const = lambda shape: pl.BlockSpec(shape, lambda i: (0, 0))

    def body(pm_ref, qg_ref, kvg_ref, wq_ref, wkv_ref, cos_ref, sin_ref, dqf_ref, dkvf_ref, dkr_ref, dab_ref, dp_in,
             dp_ref, dqg_ref, dkvg_ref, dwq_ref, dwkv_ref):
        del dp_in
        fn = lambda pm, qg_, kvg_, wq_, wkv_: _mla_prep(pm, qg_, kvg_, wq_, wkv_, cos_ref[...], sin_ref[...])
        _, vjp = jax.vjp(fn, pm_ref[...], qg_ref[...], kvg_ref[...], wq_ref[...].astype(F32), wkv_ref[...].astype(F32))
        dpm, dqg, dkvg, dwq, dwkv = vjp((dqf_ref[...], dkvf_ref[...], dkr_ref[...]))
        dp_ref[...] = jnp.concatenate([dab_ref[...], dpm[:, 128:]], axis=1)

        @pl.when(pl.program_id(0) == 0)
        def _():
            dqg_ref[...] = jnp.zeros_like(dqg_ref)
            dkvg_ref[...] = jnp.zeros_like(dkvg_ref)
            dwq_ref[...] = jnp.zeros_like(dwq_ref)
            dwkv_ref[...] = jnp.zeros_like(dwkv_ref)

        dqg_ref[...] += dqg
        dkvg_ref[...] += dkvg
        dwq_ref[...] += dwq
        dwkv_ref[...] += dwkv

    rows = lambda w: pl.BlockSpec((tm, w), lambda i: (i, 0))
    return pl.pallas_call(
        body, name="mla_prep_bwd", grid=(t // tm,),
        in_specs=[
            pl.BlockSpec((tm, 1024), lambda i: (i, P_MLA // 1024)),
            const((1, MLA_Q_LORA)), const((1, MLA_KV_LORA)), const(wq.shape), const(wkv.shape),
            pl.BlockSpec((tm, 128), lambda i: (i % nps, 0)), pl.BlockSpec((tm, 128), lambda i: (i % nps, 0)),
            rows(1024), rows(1024), rows(128), rows(128),
            pl.BlockSpec(memory_space=pl.ANY),
        ],
        out_specs=[pl.BlockSpec((tm, 1024), lambda i: (i, P_MLA // 1024)),
                   const((1, MLA_Q_LORA)), const((1, MLA_KV_LORA)), const(wq.shape), const(wkv.shape)],
        out_shape=[jax.ShapeDtypeStruct(dproj.shape, F32),
                   jax.ShapeDtypeStruct((1, MLA_Q_LORA), F32), jax.ShapeDtypeStruct((1, MLA_KV_LORA), F32),
                   jax.ShapeDtypeStruct(wq.shape, F32), jax.ShapeDtypeStruct(wkv.shape, F32)],
        input_output_aliases={11: 0},
        compiler_params=_params(("arbitrary",)),
    )(proj, qg, kvg, wq, wkv, cos, sin, dqf, dkvf, dkr, dab, dproj)


def _attn_fwd(qf, kvf, kr, cat, bl, s, tq):
    tq = min(tq, s)
    nq = s // tq

    def body(q_ref, kv_ref, kr_ref, cat_in, o_ref):
        del cat_in
        q0 = pl.program_id(2) * tq
        o_ref[...] = _attn_block(q_ref[...], kv_ref[:, 0:128], kr_ref[...], kv_ref[:, 128:256], q0)

    return pl.pallas_call(
        body, name="attn_fwd", grid=(bl, MLA_HEADS, nq),
        in_specs=[
            pl.BlockSpec((tq, 256), lambda b, h, i: (b * nq + i, h)),
            pl.BlockSpec((s, 256), lambda b, h, i: (b, h)),
            pl.BlockSpec((s, 128), lambda b, h, i: (b, 0)),
            pl.BlockSpec(memory_space=pl.ANY),
        ],
        out_specs=pl.BlockSpec((tq, 128), lambda b, h, i: (b * nq + i, GDN_HEADS + h)),
        out_shape=jax.ShapeDtypeStruct(cat.shape, F32),
        input_output_aliases={3: 0},
        compiler_params=_params(("parallel", "parallel", "parallel")),
    )(qf, kvf, kr, cat)


def _attn_bwd(qf, kvf, kr, dcat, bl, s, tq):
    tq = min(tq, s)
    nq = s // tq

    def body(q_ref, kv_ref, kr_ref, do_ref, dq_ref, dkv_ref, dkr_ref):
        h, i = pl.program_id(1), pl.program_id(2)
        fn = functools.partial(_attn_block, q0=i * tq)
        _, vjp = jax.vjp(fn, q_ref[...], kv_ref[:, 0:128], kr_ref[...], kv_ref[:, 128:256])
        dq, dkn, dkr, dv = vjp(do_ref[...])
        dq_ref[...] = dq

        @pl.when(i == 0)
        def _():
            dkv_ref[...] = jnp.zeros_like(dkv_ref)

        @pl.when((i == 0) & (h == 0))
        def _():
            dkr_ref[...] = jnp.zeros_like(dkr_ref)

        dkv_ref[...] += jnp.concatenate([dkn, dv], axis=1)
        dkr_ref[...] += dkr

    t = bl * s
    return pl.pallas_call(
        body, name="attn_bwd", grid=(bl, MLA_HEADS, nq),
        in_specs=[
            pl.BlockSpec((tq, 256), lambda b, h, i: (b * nq + i, h)),
            pl.BlockSpec((s, 256), lambda b, h, i: (b, h)),
            pl.BlockSpec((s, 128), lambda b, h, i: (b, 0)),
            pl.BlockSpec((tq, 128), lambda b, h, i: (b * nq + i, GDN_HEADS + h)),
        ],
        out_specs=[
            pl.BlockSpec((tq, 256), lambda b, h, i: (b * nq + i, h)),
            pl.BlockSpec((s, 256), lambda b, h, i: (b, h)),
            pl.BlockSpec((s, 128), lambda b, h, i: (b, 0)),
        ],
        out_shape=[jax.ShapeDtypeStruct((t, 1024), F32), jax.ShapeDtypeStruct((t, 1024), F32),
                   jax.ShapeDtypeStruct((t, 128), F32)],
        compiler_params=_params(("parallel", "arbitrary", "arbitrary")),
    )(qf, kvf, kr, dcat)


def _ln1_fwd(x, mix, g, b, tm):
    t = x.shape[0]
    tm = min(tm, t)

    def body(x_ref, mix_ref, g_ref, b_ref, r_ref, h_ref):
        r = ALPHA * x_ref[...] + mix_ref[...]
        r_ref[...] = r
        h_ref[...] = _layernorm(r, g_ref[...], b_ref[...])

    rows = pl.BlockSpec((tm, D_MODEL), lambda i: (i, 0))
    vec = pl.BlockSpec((1, D_MODEL), lambda i: (0, 0))
    return pl.pallas_call(
        body, name="ln1_fwd", grid=(t // tm,), in_specs=[rows, rows, vec, vec], out_specs=[rows, rows],
        out_shape=[jax.ShapeDtypeStruct(x.shape, F32)] * 2, compiler_params=_params(("parallel",)),
    )(x, mix, g, b)


def _ln1_bwd(r1, dr2, da, db_, g, b, tm):
    t = r1.shape[0]
    tm = min(tm, t)

    def body(r_ref, d2_ref, da_ref, db_ref, g_ref, b_ref, dr_ref, dg_ref, dbias_ref):
        dh = ALPHA * d2_ref[...] + da_ref[...] + db_ref[...]
        _, vjp = jax.vjp(_layernorm, r_ref[...], g_ref[...], b_ref[...])
        dr, dg, dbias = vjp(dh)
        dr_ref[...] = dr

        @pl.when(pl.program_id(0) == 0)
        def _():
            dg_ref[...] = jnp.zeros_like(dg_ref)
            dbias_ref[...] = jnp.zeros_like(dbias_ref)

        dg_ref[...] += dg
        dbias_ref[...] += dbias

    rows = pl.BlockSpec((tm, D_MODEL), lambda i: (i, 0))
    vec = pl.BlockSpec((1, D_MODEL), lambda i: (0, 0))
    return pl.pallas_call(
        body, name="ln1_bwd", grid=(t // tm,), in_specs=[rows] * 4 + [vec, vec], out_specs=[rows, vec, vec],
        out_shape=[jax.ShapeDtypeStruct(r1.shape, F32)] + [jax.ShapeDtypeStruct((1, D_MODEL), F32)] * 2,
        compiler_params=_params(("arbitrary",)),
    )(r1, dr2, da, db_, g, b)


def _ffn_act_fwd(u, conv_w, conv_b, bl, s, cb):
    nj = D_FF // cb

    def body(ug_ref, uu_ref, wg_ref, wu_ref, bg_ref, bu_ref, act_ref):
        act_ref[...] = _ffn_act(ug_ref[...], uu_ref[...], wg_ref[...], wu_ref[...], bg_ref[...], bu_ref[...])

    return pl.pallas_call(
        body, name="ffn_act_fwd", grid=(bl, nj),
        in_specs=[
            pl.BlockSpec((s, cb), lambda b, j: (b, j)), pl.BlockSpec((s, cb), lambda b, j: (b, nj + j)),
            pl.BlockSpec((FFN_CONV, cb), lambda b, j: (0, j)), pl.BlockSpec((FFN_CONV, cb), lambda b, j: (0, nj + j)),
            pl.BlockSpec((1, cb), lambda b, j: (0, j)), pl.BlockSpec((1, cb), lambda b, j: (0, nj + j)),
        ],
        out_specs=pl.BlockSpec((s, cb), lambda b, j: (b, j)),
        out_shape=jax.ShapeDtypeStruct((bl * s, D_FF), F32),
        compiler_params=_params(("parallel", "parallel")),
    )(u, u, conv_w, conv_w, conv_b, conv_b)


def _ffn_act_bwd(u, conv_w, conv_b, dact, bl, s, cb):
    nj = D_FF // cb

    def body(ug_ref, uu_ref, wg_ref, wu_ref, bg_ref, bu_ref, da_ref, du_ref, dwg_ref, dwu_ref, dbg_ref, dbu_ref):
        _, vjp = jax.vjp(_ffn_act, ug_ref[...], uu_ref[...], wg_ref[...], wu_ref[...], bg_ref[...], bu_ref[...])
        du_ref[0], du_ref[1], dwg_ref[...], dwu_ref[...], dbg_ref[...], dbu_ref[...] = vjp(da_ref[...])

    t = bl * s
    blk = pl.BlockSpec((s, cb), lambda b, j: (b, j))
    wpart = pl.BlockSpec((None, FFN_CONV, cb), lambda b, j: (b, 0, j))
    bpart = pl.BlockSpec((None, 1, cb), lambda b, j: (b, 0, j))
    return pl.pallas_call(
        body, name="ffn_act_bwd", grid=(bl, nj),
        in_specs=[
            blk, pl.BlockSpec((s, cb), lambda b, j: (b, nj + j)),
            pl.BlockSpec((FFN_CONV, cb), lambda b, j: (0, j)), pl.BlockSpec((FFN_CONV, cb), lambda b, j: (0, nj + j)),
            pl.BlockSpec((1, cb), lambda b, j: (0, j)), pl.BlockSpec((1, cb), lambda b, j: (0, nj + j)),
            blk,
        ],
        out_specs=[pl.BlockSpec((2, s, cb), lambda b, j: (0, b, j)), wpart, wpart, bpart, bpart],
        out_shape=[jax.ShapeDtypeStruct((2, t, D_FF), F32)] + [jax.ShapeDtypeStruct((bl, FFN_CONV, D_FF), F32)] * 2
        + [jax.ShapeDtypeStruct((bl, 1, D_FF), F32)] * 2,
        compiler_params=_params(("parallel", "parallel")),
    )(u, u, conv_w, conv_w, conv_b, conv_b, dact)


def _head(h1, ffn, gpre, pp, bgate, g2, b2, target, tm):
    t = h1.shape[0]
    tm = min(tm, t)

    def body(h1_ref, ffn_ref, gp_ref, pp_ref, bg_ref, g2_ref, b2_ref, tg_ref,
             dr_ref, dgp_ref, dpp_ref, loss_ref, dbg_ref, dg2_ref, db2_ref):
        fn = functools.partial(_head_loss, target=tg_ref[...])
        loss, vjp = jax.vjp(fn, h1_ref[...], ffn_ref[...], gp_ref[...], pp_ref[...], bg_ref[...], g2_ref[...], b2_ref[...])
        _, dffn, dgp, dpp, dbg, dg2, db2 = vjp(jnp.ones((1, 1), F32))
        dr_ref[...], dgp_ref[...], dpp_ref[...] = dffn, dgp, dpp

        @pl.when(pl.program_id(0) == 0)
        def _():
            loss_ref[...] = jnp.zeros_like(loss_ref)
            dbg_ref[...] = jnp.zeros_like(dbg_ref)
            dg2_ref[...] = jnp.zeros_like(dg2_ref)
            db2_ref[...] = jnp.zeros_like(db2_ref)

        loss_ref[...] += jnp.broadcast_to(loss, loss_ref.shape)
        dbg_ref[...] += dbg
        dg2_ref[...] += dg2
        db2_ref[...] += db2

    rows = pl.BlockSpec((tm, D_MODEL), lambda i: (i, 0))
    vec = pl.BlockSpec((1, D_MODEL), lambda i: (0, 0))
    return pl.pallas_call(
        body, name="head", grid=(t // tm,), in_specs=[rows] * 4 + [vec] * 3 + [rows],
        out_specs=[rows] * 3 + [pl.BlockSpec((8, 128), lambda i: (0, 0))] + [vec] * 3,
        out_shape=[jax.ShapeDtypeStruct(h1.shape, F32)] * 3 + [jax.ShapeDtypeStruct((8, 128), F32)]
        + [jax.ShapeDtypeStruct((1, D_MODEL), F32)] * 3,
        compiler_params=_params(("arbitrary",)),
    )(h1, ffn, gpre, pp, bgate, g2, b2, target)


def _adam_update(g, w_ref, m_ref, v_ref, g_ref, d_ref, nm_ref, nv_ref):
    m2 = ADAM_B1 * m_ref[...] + (1.0 - ADAM_B1) * g
    v2 = ADAM_B2 * v_ref[...] + (1.0 - ADAM_B2) * jnp.square(g)
    m_hat = m2 / (1.0 - ADAM_B1 ** ADAM_STEP)
    v_hat = v2 / (1.0 - ADAM_B2 ** ADAM_STEP)
    g_ref[...] = g
    d_ref[...] = -ADAM_LR * (m_hat / (jnp.sqrt(v_hat) + ADAM_EPS) + ADAM_WD * w_ref[...])
    nm_ref[...] = m2
    nv_ref[...] = v2


def _row_tile(rows, cols, limit_bytes=256 * 1024):
    best = None
    for t in range(8, rows + 1, 8):
        if rows % t == 0 and t * cols * 4 <= limit_bytes:
            best = t
    return best or rows


def _adamw_reduced(own, recv, w, m, v, name):
    a, b = w.shape
    ta = _row_tile(a, b)

    def body(own_ref, recv_ref, w_ref, m_ref, v_ref, g_ref, d_ref, nm_ref, nv_ref):
        c = lax.axis_index("c")
        for core in range(2):
            @pl.when(c == core)
            def _():
                same = [own_ref[...], recv_ref[0], recv_ref[1], recv_ref[2]]
                other = [recv_ref[3], recv_ref[4], recv_ref[5], recv_ref[6]]
                core0, core1 = (same, other) if core == 0 else (other, same)
                g = core0[0] + core1[0]
                for r in range(1, N_CHIPS):
                    g = (g + core0[r]) + core1[r]
                _adam_update(g, w_ref, m_ref, v_ref, g_ref, d_ref, nm_ref, nv_ref)

    blk = pl.BlockSpec((ta, b), lambda i: (i, 0))
    return pl.pallas_call(
        body, name=name, grid=(a // ta,),
        in_specs=[blk, pl.BlockSpec((7, ta, b), lambda i: (0, i, 0)), blk, blk, blk], out_specs=[blk] * 4,
        out_shape=[jax.ShapeDtypeStruct(w.shape, F32)] * 4, compiler_params=_params(("parallel",)),
    )(own, recv, w, m, v)


def _adamw_small(g, w, m, v):
    def body(g_in, w_ref, m_ref, v_ref, g_ref, d_ref, nm_ref, nv_ref):
        _adam_update(g_in[...], w_ref, m_ref, v_ref, g_ref, d_ref, nm_ref, nv_ref)

    blk = pl.BlockSpec(w.shape, lambda i: (0, 0))
    return pl.pallas_call(
        body, name="adamw_small", grid=(1,), in_specs=[blk] * 4, out_specs=[blk] * 4,
        out_shape=[jax.ShapeDtypeStruct(w.shape, F32)] * 4, compiler_params=_params(("arbitrary",)),
    )(g, w, m, v)


def _remote(src, dst, send_sem, recv_sem, device):
    return pltpu.make_async_remote_copy(src_ref=src, dst_ref=dst, send_sem=send_sem, recv_sem=recv_sem,
                                        device_id=device, device_id_type=MESH)


def _place():
    x, y, c = lax.axis_index("x"), lax.axis_index("y"), lax.axis_index("c")
    return x, y, c, 2 * x + y, [(1 - x, y), (x, 1 - y), (1 - x, 1 - y)]


HBM_REF = pl.BlockSpec(memory_space=pl.ANY)
HALF_ROWS_QUANTUM = 16


def _gather_sems(n):
    return [pltpu.SemaphoreType.DMA((3 * n,))] * 4 + [pltpu.SemaphoreType.DMA((n,))]


def _gather_copies(ins, outs, sems):
    send_s, recv_s, fsend_s, frecv_s, local_s = sems
    x, y, c, me, chips = _place()
    local, sends, steps = [], [], []
    for i, (src, dst) in enumerate(zip(ins, outs)):
        local.append(pltpu.make_async_copy(src, dst.at[me], local_s.at[i]))
        half = src.shape[0] // 2
        split = src.shape[0] % (2 * HALF_ROWS_QUANTUM) == 0
        if split:
            mine = pl.ds(pl.multiple_of(c * half, HALF_ROWS_QUANTUM), half)
            theirs = pl.ds(pl.multiple_of((1 - c) * half, HALF_ROWS_QUANTUM), half)
        for r, (px, py) in enumerate(chips):
            k, peer = 3 * i + r, 2 * px + py
            if split:
                sends.append(_remote(src.at[mine], dst.at[me, mine], send_s.at[k], recv_s.at[k], (px, py, c)))
                landed = dst.at[peer, mine]
                steps.append((_remote(src.at[mine], landed, send_s.at[k], recv_s.at[k], (px, py, c)),
                              _remote(landed, landed, fsend_s.at[k], frecv_s.at[k], (x, y, 1 - c)),
                              _remote(dst.at[peer, theirs], dst.at[peer, theirs], fsend_s.at[k], frecv_s.at[k], (x, y, 1 - c))))
            else:
                sends.append(_remote(src, dst.at[me], send_s.at[k], recv_s.at[k], (px, py, c)))
                steps.append((_remote(src, dst.at[peer], send_s.at[k], recv_s.at[k], (px, py, c)), None, None))
    return local, sends, steps


def _scatter_sems(n):
    return [pltpu.SemaphoreType.DMA((4 * n,))] * 2 + [pltpu.SemaphoreType.DMA((3 * n,))] * 2


def _scatter_copies(ins, outs, sems):
    send_s, recv_s, fsend_s, frecv_s = sems
    x, y, c, me, chips = _place()
    sends, steps = [], []
    for i, (src, dst) in enumerate(zip(ins, outs)):
        for r, (px, py) in enumerate(chips):
            k = 4 * i + r
            cp = _remote(src.at[2 * px + py], dst.at[r], send_s.at[k], recv_s.at[k], (px, py, c))
            fwd = _remote(dst.at[r], dst.at[4 + r], fsend_s.at[3 * i + r], frecv_s.at[3 * i + r], (x, y, 1 - c))
            sends.append(cp)
            steps.append((cp, fwd, fwd))
        k = 4 * i + 3
        cp = _remote(src.at[me], dst.at[3], send_s.at[k], recv_s.at[k], (x, y, 1 - c))
        sends.append(cp)
        steps.append((cp, None, None))
    return [], sends, steps


def _exchange_start(plan):
    local, sends, _ = plan
    for cp in local + sends:
        cp.start()


def _exchange_pass_on(plan):
    for arrival, pass_on, _ in plan[2]:
        arrival.wait_recv()
        if pass_on is not None:
            pass_on.start()


def _exchange_finish(plan):
    local, sends, steps = plan
    for _, pass_on, passed in steps:
        if pass_on is not None:
            passed.wait_recv()
    for cp in sends:
        cp.wait_send()
    for _, pass_on, _ in steps:
        if pass_on is not None:
            pass_on.wait_send()
    for cp in local:
        cp.wait()


def _exchange_call(arrays, copies, sems, out_shapes, name):
    n = len(arrays)

    def body(*refs):
        plan = copies(refs[:n], refs[n:2 * n], refs[2 * n:])
        _exchange_start(plan)
        _exchange_pass_on(plan)
        _exchange_finish(plan)

    return pl.pallas_call(
        body, name=name, in_specs=[HBM_REF] * n, out_specs=[HBM_REF] * n, out_shape=out_shapes,
        scratch_shapes=sems(n), compiler_params=pltpu.CompilerParams(has_side_effects=True),
    )(*arrays)


def _gather_call(shards, name):
    shapes = [jax.ShapeDtypeStruct((N_CHIPS,) + a.shape, a.dtype) for a in shards]
    return _exchange_call(shards, _gather_copies, _gather_sems, shapes, name)


def _scatter_call(slabs, name):
    shapes = [jax.ShapeDtypeStruct((N_DEV - 1,) + a.shape[1:], a.dtype) for a in slabs]
    return _exchange_call(slabs, _scatter_copies, _scatter_sems, shapes, name)


def _all_reduce_small(a):
    def body(in_ref, out_ref, slots, send_sems, recv_sems):
        x, y, c = lax.axis_index("x"), lax.axis_index("y"), lax.axis_index("c")
        me = 4 * x + 2 * y + c
        slots[0] = in_ref[...]
        sends = []
        for r in range(1, N_DEV):
            peer = (x ^ (r >> 2), y ^ ((r >> 1) & 1), c ^ (r & 1))
            sends.append(pltpu.make_async_remote_copy(src_ref=in_ref, dst_ref=slots.at[r], send_sem=send_sems.at[r],
                                                      recv_sem=recv_sems.at[r], device_id=peer, device_id_type=MESH))
        for cp in sends:
            cp.start()
        for cp in sends:
            cp.wait_recv()
        acc = slots[me]
        for dev in range(1, N_DEV):
            acc = acc + slots[dev ^ me]
        out_ref[...] = acc
        for cp in sends:
            cp.wait_send()

    return pl.pallas_call(
        body, name="small_all_reduce",
        in_specs=[pl.BlockSpec(memory_space=pltpu.VMEM)], out_specs=pl.BlockSpec(memory_space=pltpu.VMEM),
        out_shape=jax.ShapeDtypeStruct(a.shape, a.dtype),
        scratch_shapes=[pltpu.VMEM((N_DEV,) + a.shape, a.dtype), pltpu.SemaphoreType.DMA((N_DEV,)),
                        pltpu.SemaphoreType.DMA((N_DEV,))],
        compiler_params=pltpu.CompilerParams(has_side_effects=True),
    )(a)


SHARDED = ["w_in", "mla_w_q_up", "mla_w_kv_up", "w_out", "ffn_w_up", "ffn_w_down", "ple_w_gate", "ple_w_proj",
           "gdn_conv_w", "ffn_conv_w"]
SHARD_AXIS = {"w_in": 1, "mla_w_q_up": 1, "mla_w_kv_up": 1, "w_out": 0, "ffn_w_up": 1, "ffn_w_down": 0,
              "ple_w_gate": 0, "ple_w_proj": 1, "gdn_conv_w": 1, "ffn_conv_w": 1}
SMALL = ["gdn_a_log", "gdn_dt_bias", "gdn_norm_g", "mla_q_norm_g", "mla_kv_norm_g", "ln1_g", "ln1_b", "ffn_conv_b",
         "ple_b_gate", "ln2_g", "ln2_b"]
WEIGHTS = ["w_in", "gdn_conv_w", "gdn_a_log", "gdn_dt_bias", "gdn_norm_g", "mla_q_norm_g", "mla_w_q_up", "mla_kv_norm_g",
           "mla_w_kv_up", "w_out", "ln1_g", "ln1_b", "ffn_w_up", "ffn_conv_w", "ffn_conv_b", "ffn_w_down", "ple_w_gate",
           "ple_b_gate", "ple_w_proj", "ln2_g", "ln2_b"]
F32_ON_WIRE = ("gdn_conv_w", "ffn_conv_w")
GATHER_EARLY = ["w_in", "gdn_conv_w", "mla_w_q_up", "mla_w_kv_up"]
GATHER_LATE = ["w_out", "ffn_w_up", "ffn_conv_w", "ffn_w_down", "ple_w_gate", "ple_w_proj"]
SCATTER_EARLY = ["ffn_w_up", "ffn_conv_w", "ffn_w_down", "ple_w_gate", "ple_w_proj", "w_out"]
SCATTER_LATE = ["w_in", "gdn_conv_w", "mla_w_q_up", "mla_w_kv_up"]
PACK_COLS = 1024
PACK_ROW_TILE = 8


def _join_blocks(blocks, axis):
    n, a, b = blocks.shape
    if axis == 0:
        return blocks.reshape(n * a, b)
    return jnp.transpose(blocks, (1, 0, 2)).reshape(a, n * b)


def _split_blocks(full, axis):
    if axis == 0:
        return full.reshape(N_CHIPS, full.shape[0] // N_CHIPS, full.shape[1])
    a, nb = full.shape
    return jnp.transpose(full.reshape(a, N_CHIPS, nb // N_CHIPS), (1, 0, 2))


def _pack(arrays):
    flat = jnp.concatenate([a.reshape(-1) for a in arrays])
    quantum = PACK_COLS * PACK_ROW_TILE
    padded = -(-flat.shape[0] // quantum) * quantum
    return jnp.pad(flat, (0, padded - flat.shape[0])).reshape(-1, PACK_COLS)


def _unpack(packed, shapes):
    flat = packed.reshape(-1)
    out, off = [], 0
    for shp in shapes:
        n = int(np.prod(shp))
        out.append(flat[off:off + n].reshape(shp))
        off += n
    return out


def kernel(x, p, w_in, gdn_conv_w, gdn_a_log, gdn_dt_bias, gdn_norm_g, mla_q_norm_g, mla_w_q_up, mla_kv_norm_g, mla_w_kv_up, w_out, ln1_g, ln1_b, ffn_w_up, ffn_conv_w, ffn_conv_b, ffn_w_down, ple_w_gate, ple_b_gate, ple_w_proj, ln2_g, ln2_b, loss_target, m_w_in, m_gdn_conv_w, m_gdn_a_log, m_gdn_dt_bias, m_gdn_norm_g, m_mla_q_norm_g, m_mla_w_q_up, m_mla_kv_norm_g, m_mla_w_kv_up, m_w_out, m_ln1_g, m_ln1_b, m_ffn_w_up, m_ffn_conv_w, m_ffn_conv_b, m_ffn_w_down, m_ple_w_gate, m_ple_b_gate, m_ple_w_proj, m_ln2_g, m_ln2_b, v_w_in, v_gdn_conv_w, v_gdn_a_log, v_gdn_dt_bias, v_gdn_norm_g, v_mla_q_norm_g, v_mla_w_q_up, v_mla_kv_norm_g, v_mla_w_kv_up, v_w_out, v_ln1_g, v_ln1_b, v_ffn_w_up, v_ffn_conv_w, v_ffn_conv_b, v_ffn_w_down, v_ple_w_gate, v_ple_b_gate, v_ple_w_proj, v_ln2_g, v_ln2_b):
    given = dict(locals())
    wsh = {n: given[n][0] for n in WEIGHTS}
    msh = {n: given["m_" + n][0] for n in WEIGHTS}
    vsh = {n: given["v_" + n][0] for n in WEIGHTS}
    bl, s, _ = x.shape
    t = bl * s
    xt = x.reshape(t, D_MODEL)
    pt = p.reshape(t, PLE_DIM)
    target = loss_target.reshape(t, D_MODEL)

    wire = lambda n: wsh[n] if n in F32_ON_WIRE else wsh[n].astype(BF16)
    early = _gather_call([wire(n) for n in GATHER_EARLY], "weights_gather_early")
    full = {n: _join_blocks(g, SHARD_AXIS[n]) for n, g in zip(GATHER_EARLY, early)}
    late_shards = [wire(n) for n in GATHER_LATE]
    late_ride = (late_shards, _gather_copies, _gather_sems,
                 [jax.ShapeDtypeStruct((N_CHIPS,) + a.shape, a.dtype) for a in late_shards])

    in_cols, q_cols = _w_in_cols(), _w_q_cols()
    w_in_p = _pad_cols(full["w_in"], in_cols)
    w_q_p = _pad_cols(full["mla_w_q_up"], q_cols)
    w_kv, gconv = full["mla_w_kv_up"], full["gdn_conv_w"]
    row = lambda a: a.reshape(1, -1)
    sc = jnp.zeros((8, 128), F32).at[0, :GDN_HEADS].set(wsh["gdn_a_log"]).at[1, :GDN_HEADS].set(wsh["gdn_dt_bias"])
    norm_g, qg, kvg = row(wsh["gdn_norm_g"]), row(wsh["mla_q_norm_g"]), row(wsh["mla_kv_norm_g"])
    g1, b1, g2, b2 = row(wsh["ln1_g"]), row(wsh["ln1_b"]), row(wsh["ln2_g"]), row(wsh["ln2_b"])
    fbias, bgate = row(wsh["ffn_conv_b"]), row(wsh["ple_b_gate"])

    inv = ROPE_THETA ** (-jnp.arange(0, MLA_ROPE, 2, dtype=F32) / MLA_ROPE)
    ang = jnp.arange(s, dtype=F32)[:, None] * inv[None, :]
    zero = jnp.zeros_like(ang)
    cos_t = jnp.concatenate([jnp.cos(ang), zero, jnp.cos(ang), zero], axis=1)
    sin_t = jnp.concatenate([-jnp.sin(ang), zero, jnp.sin(ang), zero], axis=1)

    proj = _matmul(xt, w_in_p, name="proj")
    cat, o_raw, states, late = _gdn_fwd(proj, gconv, sc, norm_g, bl, s, late_ride)
    full.update({n: _join_blocks(g, SHARD_AXIS[n]) for n, g in zip(GATHER_LATE, late)})
    w_o, w_up, w_down = full["w_out"], full["ffn_w_up"], full["ffn_w_down"]
    w_gate, w_proj, fconv = full["ple_w_gate"], full["ple_w_proj"], full["ffn_conv_w"]
    qf, kvf, kr = _mla_prep_fwd(proj, qg, kvg, w_q_p, w_kv, cos_t, sin_t, s, 256)
    cat = _attn_fwd(qf, kvf, kr, cat, bl, s, 256)
    mix = _matmul(cat, w_o, name="mix")
    r1, h1 = _ln1_fwd(xt, mix, g1, b1, 256)
    u = _matmul(h1, w_up, name="ffn_up")
    act = _ffn_act_fwd(u, fconv, fbias, bl, s, 256)
    ffn = _matmul(act, w_down, name="ffn_down", tk=1408)
    gpre = _matmul(h1, w_gate, name="ple_gate")
    pp = _matmul(pt, w_proj, name="ple_proj")
    dr2, dgpre, dpp, loss_acc, dbgate, dg2, db2 = _head(h1, ffn, gpre, pp, bgate, g2, b2, target, 256)

    dact = _matmul(dr2, w_down, name="d_act", tb=True, tn=256)
    d_w_down = _matmul(act, dr2, name="dw_down", ta=True, tm=256)
    du, dfcw_g, dfcw_u, dfcb_g, dfcb_u = _ffn_act_bwd(u, fconv, fbias, dact, bl, s, 256)
    dh1_a = _matmul(du, w_up, name="dh1_ffn", tb=True, tk=1408, a_halves=True)
    dh1_b = _matmul(dgpre, w_gate, name="dh1_ple", tb=True)
    d_w_up = _matmul(h1, du, name="dw_up", ta=True, tn=1408, b_halves=True)
    d_w_gate = _matmul(h1, dgpre, name="dw_gate", ta=True)
    d_w_proj = _matmul(pt, dpp, name="dw_proj", ta=True)
    dr1, dg1, db1 = _ln1_bwd(r1, dr2, dh1_a, dh1_b, g1, b1, 256)
    dcat = _matmul(dr1, w_o, name="d_cat", tb=True)
    d_w_o = _matmul(cat, dr1, name="dw_out", ta=True)

    gfull = {
        "ffn_w_up": d_w_up, "ffn_w_down": d_w_down, "ple_w_gate": d_w_gate, "ple_w_proj": d_w_proj, "w_out": d_w_o,
        "ffn_conv_w": jnp.concatenate([jnp.sum(dfcw_g, 0), jnp.sum(dfcw_u, 0)], axis=1),
    }
    slabs = {n: _split_blocks(gfull[n], SHARD_AXIS[n]) for n in SCATTER_EARLY}
    early_slabs = [slabs[n] for n in SCATTER_EARLY]
    early_ride = (early_slabs, _scatter_copies, _scatter_sems,
                  [jax.ShapeDtypeStruct((N_DEV - 1,) + a.shape[1:], a.dtype) for a in early_slabs])
    dproj, dab, dcwq, dcwk, dcwv, dsc, dng, early_recv = _gdn_bwd(proj, gconv, sc, norm_g, o_raw, states, dcat, bl, s, early_ride)
    received = dict(zip(SCATTER_EARLY, early_recv))
    dqf, dkvf, dkr = _attn_bwd(qf, kvf, kr, dcat, bl, s, 256)
    dproj, dqg, dkvg, d_w_q_p, d_w_kv = _mla_prep_bwd(proj, qg, kvg, w_q_p, w_kv, cos_t, sin_t, dqf, dkvf, dkr, dab, dproj, s, 256)
    grad_x = _matmul(dproj, w_in_p, name="d_x", tb=True, add=dr1, add_scale=ALPHA)
    d_w_in_p = _matmul(xt, dproj, name="dw_in", ta=True)

    gfull.update({
        "w_in": _unpad_cols(d_w_in_p, in_cols, D_IN),
        "mla_w_q_up": _unpad_cols(d_w_q_p, q_cols, MLA_HEADS * (MLA_NOPE + MLA_ROPE)),
        "mla_w_kv_up": d_w_kv,
        "gdn_conv_w": jnp.concatenate([jnp.sum(dcwq, 0), jnp.sum(dcwk, 0), jnp.sum(dcwv, 0)], axis=1),
    })
    dsc_sum = jnp.sum(dsc, axis=(0, 1))
    gsmall = {
        "gdn_a_log": dsc_sum[0, :GDN_HEADS], "gdn_dt_bias": dsc_sum[1, :GDN_HEADS],
        "gdn_norm_g": jnp.sum(dng[:, :, 0, :], axis=(0, 1)),
        "mla_q_norm_g": dqg[0], "mla_kv_norm_g": dkvg[0], "ln1_g": dg1[0], "ln1_b": db1[0],
        "ffn_conv_b": jnp.concatenate([jnp.sum(dfcb_g, 0), jnp.sum(dfcb_u, 0)], axis=1)[0],
        "ple_b_gate": dbgate[0], "ln2_g": dg2[0], "ln2_b": db2[0],
    }

    slabs.update({n: _split_blocks(gfull[n], SHARD_AXIS[n]) for n in SCATTER_LATE})
    me_chip = 2 * lax.axis_index("x") + lax.axis_index("y")
    received.update(zip(SCATTER_LATE, _scatter_call([slabs[n] for n in SCATTER_LATE], "grads_scatter_late")))
    big = [{}, {}, {}, {}]
    for n in SHARDED:
        own = lax.dynamic_index_in_dim(slabs[n], me_chip, 0, keepdims=False)
        for kind, val in enumerate(_adamw_reduced(own, received[n], wsh[n], msh[n], vsh[n], "adamw_" + n)):
            big[kind][n] = val

    small_shapes = [wsh[n].shape for n in SMALL]
    gsum = _all_reduce_small(_pack([gsmall[n] for n in SMALL]))
    spacks = _adamw_small(gsum, _pack([wsh[n] for n in SMALL]), _pack([msh[n] for n in SMALL]), _pack([vsh[n] for n in SMALL]))
    small = [dict(zip(SMALL, _unpack(pk, small_shapes))) for pk in spacks]

    loss = lax.psum(loss_acc[0, 0], ("x", "y", "c"))
    outs = [loss, grad_x.reshape(x.shape)]
    for kind in range(4):
        for n in WEIGHTS:
            val = big[kind][n] if n in big[kind] else small[kind][n]
            outs.append(val[None])
    return tuple(outs)
```

```python
import functools
import math

import numpy as np
import jax
import jax.numpy as jnp
from jax import lax
from jax.experimental import pallas as pl
from jax.experimental.pallas import tpu as pltpu

F32 = jnp.float32
BF16 = jnp.bfloat16

D_MODEL = 1024
CHUNK = 64
PLE_DIM = 256
GDN_HEADS = 4
GDN_DK = 128
GDN_DV = 128
GDN_CONV = 4
MLA_HEADS = 4
MLA_NOPE = 128
MLA_ROPE = 64
MLA_V = 128
MLA_Q_LORA = 384
MLA_KV_LORA = 256
ROPE_THETA = 10000.0
D_FF = 2816
FFN_CONV = 3
DEPTH = 1
ALPHA = (2.0 * DEPTH) ** 0.25
NORM_EPS = 1e-6
GDN_QK = GDN_HEADS * GDN_DK
GDN_VW = GDN_HEADS * GDN_DV
D_IN = 2 * GDN_QK + 2 * GDN_VW + 2 * GDN_HEADS + MLA_Q_LORA + MLA_KV_LORA + MLA_ROPE
ATT_SCALE = (MLA_NOPE + MLA_ROPE) ** -0.5

ADAM_LR = 0.001
ADAM_B1 = 0.9
ADAM_B2 = 0.999
ADAM_EPS = 1e-08
ADAM_WD = 0.01
ADAM_STEP = 10

LANES = 128
VMEM_LIMIT = 60 * 1024 * 1024
GDN_FWD_GROUP = 16
GDN_BWD_GROUP = 16
N_CHIPS = 4
N_DEV = 8

P_WIDTH = 3072
P_MLA = 2048
MESH = pl.DeviceIdType.MESH


def _rope_slot(j):
    return j if j < MLA_ROPE // 2 else 64 + (j - MLA_ROPE // 2)


def _w_in_cols():
    idx = -np.ones((P_WIDTH,), np.int64)
    for h in range(GDN_HEADS):
        base = h * 512
        idx[base:base + 128] = np.arange(128) + h * GDN_DK
        idx[base + 128:base + 256] = np.arange(128) + GDN_QK + h * GDN_DK
        idx[base + 256:base + 384] = np.arange(128) + 2 * GDN_QK + h * GDN_DV
        idx[base + 384:base + 512] = np.arange(128) + 2 * GDN_QK + GDN_VW + h * GDN_DV
    o_a = 2 * GDN_QK + 2 * GDN_VW
    idx[P_MLA:P_MLA + 2 * GDN_HEADS] = np.arange(2 * GDN_HEADS) + o_a
    o_cq = o_a + 2 * GDN_HEADS
    idx[P_MLA + 128:P_MLA + 512] = np.arange(MLA_Q_LORA) + o_cq
    o_ckv = o_cq + MLA_Q_LORA
    idx[P_MLA + 512:P_MLA + 768] = np.arange(MLA_KV_LORA) + o_ckv
    o_kr = o_ckv + MLA_KV_LORA
    for j in range(MLA_ROPE):
        idx[P_MLA + 768 + _rope_slot(j)] = o_kr + j
    return idx


def _w_q_cols():
    idx = -np.ones((MLA_HEADS * 256,), np.int64)
    for h in range(MLA_HEADS):
        o = h * (MLA_NOPE + MLA_ROPE)
        idx[h * 256:h * 256 + 128] = np.arange(128) + o
        for j in range(MLA_ROPE):
            idx[h * 256 + 128 + _rope_slot(j)] = o + MLA_NOPE + j
    return idx


def _pad_cols(w, idx):
    safe = np.where(idx >= 0, idx, 0)
    return jnp.where(jnp.asarray(idx >= 0)[None, :], w[:, safe], 0.0)


def _unpad_cols(wp, idx, n):
    inv = np.zeros((n,), np.int64)
    inv[idx[idx >= 0]] = np.nonzero(idx >= 0)[0]
    return wp[:, inv]


def _dot(a, b, ca, cb, precision=None):
    if precision is None:
        a = a.astype(BF16)
        b = b.astype(BF16)
    return lax.dot_general(a, b, (((ca,), (cb,)), ((), ())), preferred_element_type=F32, precision=precision)


@jax.custom_vjp
def mm(a, b):
    return _dot(a, b, 1, 0)


@jax.custom_vjp
def mm_nt(a, b):
    return _dot(a, b, 1, 1)


@jax.custom_vjp
def mm_tn(a, b):
    return _dot(a, b, 0, 0)


mm.defvjp(lambda a, b: (mm(a, b), (a, b)), lambda r, g: (mm_nt(g, r[1]), mm_tn(r[0], g)))
mm_nt.defvjp(lambda a, b: (mm_nt(a, b), (a, b)), lambda r, g: (mm(g, r[1]), mm_tn(g, r[0])))
mm_tn.defvjp(lambda a, b: (mm_tn(a, b), (a, b)), lambda r, g: (mm_nt(r[1], g), mm(r[0], g)))

def _split(a):
    hi = a.astype(BF16)
    return hi, (a - hi.astype(F32)).astype(BF16)


def _dot3(a, b, ca, cb):
    a_hi, a_lo = _split(a)
    b_hi, b_lo = _split(b)
    return (_dot(a_hi, b_hi, ca, cb) + _dot(a_hi, b_lo, ca, cb)) + _dot(a_lo, b_hi, ca, cb)


def _unit_lower_inverse(low):
    n = low.shape[0]
    ii = lax.broadcasted_iota(jnp.int32, (n, n), 0)
    jj = lax.broadcasted_iota(jnp.int32, (n, n), 1)
    inv = jnp.where(ii == jj, 1.0, 0.0) - low
    power = _dot3(low, low, 1, 0)
    k = 2
    while k < n:
        inv = inv + _dot3(inv, power, 1, 0)
        k *= 2
        if k < n:
            power = _dot3(power, power, 1, 0)
    return inv


@jax.custom_vjp
def unit_lower_solve(low, rhs):
    return _dot3(_unit_lower_inverse(low), rhs, 1, 0)


def _uls_fwd(low, rhs):
    inv = _unit_lower_inverse(low)
    x = _dot3(inv, rhs, 1, 0)
    return x, (inv, x)


def _uls_bwd(res, dx):
    inv, x = res
    drhs = _dot3(inv, dx, 0, 0)
    n = inv.shape[0]
    ii = lax.broadcasted_iota(jnp.int32, (n, n), 0)
    jj = lax.broadcasted_iota(jnp.int32, (n, n), 1)
    dlow = jnp.where(ii > jj, -_dot3(drhs, x, 1, 1), 0.0)
    return dlow, drhs


unit_lower_solve.defvjp(_uls_fwd, _uls_bwd)


def _shift_rows(x, s):
    if s == 0:
        return x
    n = x.shape[0]
    row = lax.broadcasted_iota(jnp.int32, x.shape, 0)
    rolled = pltpu.roll(x, s % n, 0)
    keep = (row >= s) if s > 0 else (row < n + s)
    return jnp.where(keep, rolled, 0.0)


def _row(w, j):
    tap = lax.broadcasted_iota(jnp.int32, w.shape, 0)
    return jnp.sum(jnp.where(tap == j, w, 0.0), axis=0, keepdims=True)


@jax.custom_vjp
def dwconv(x, w):
    k = w.shape[0]
    y = _row(w, k - 1) * x
    for j in range(k - 1):
        y = y + _row(w, j) * _shift_rows(x, k - 1 - j)
    return y


def _dwconv_fwd(x, w):
    return dwconv(x, w), (x, w)


def _dwconv_bwd(res, dy):
    x, w = res
    k = w.shape[0]
    dx = _row(w, k - 1) * dy
    tap = lax.broadcasted_iota(jnp.int32, w.shape, 0)
    dw = jnp.where(tap == k - 1, jnp.sum(dy * x, axis=0, keepdims=True), 0.0)
    for j in range(k - 1):
        dx = dx + _row(w, j) * _shift_rows(dy, -(k - 1 - j))
        dw = dw + jnp.where(tap == j, jnp.sum(dy * _shift_rows(x, k - 1 - j), axis=0, keepdims=True), 0.0)
    return dx, dw


dwconv.defvjp(_dwconv_fwd, _dwconv_bwd)


@jax.custom_vjp
def rope128(x, cos, sin):
    return x * cos + pltpu.roll(x, 64, 1) * sin


rope128.defvjp(lambda x, c, s: (rope128(x, c, s), (c, s)),
               lambda r, g: (g * r[0] + pltpu.roll(g * r[1], 64, 1), jnp.zeros_like(r[0]), jnp.zeros_like(r[1])))


def _silu(x):
    return x * jax.nn.sigmoid(x)


def _softplus(x):
    return jnp.maximum(x, 0.0) + jnp.log(1.0 + jnp.exp(-jnp.abs(x)))


def _rmsnorm(x, g):
    return x * lax.rsqrt(jnp.mean(x * x, axis=-1, keepdims=True) + NORM_EPS) * g


def _layernorm(x, g, b):
    mu = jnp.mean(x, axis=-1, keepdims=True)
    xc = x - mu
    var = jnp.mean(xc * xc, axis=-1, keepdims=True)
    return xc * lax.rsqrt(var + NORM_EPS) * g + b


def _pick_lane(row, lane):
    idx = lax.broadcasted_iota(jnp.int32, row.shape, 1)
    return jnp.sum(jnp.where(idx == lane, row, 0.0), axis=1, keepdims=True)


def _gdn_q(pq, cw):
    h = _silu(dwconv(pq, cw))
    return h * lax.rsqrt(jnp.sum(h * h, axis=-1, keepdims=True) + NORM_EPS) * (GDN_DK ** -0.5)


def _gdn_k(pk, cw):
    h = _silu(dwconv(pk, cw))
    return h * lax.rsqrt(jnp.sum(h * h, axis=-1, keepdims=True) + NORM_EPS)


def _gdn_v(pv, cw):
    return _silu(dwconv(pv, cw))


def _gdn_gate(ab, sc, head):
    a = _pick_lane(ab, head)
    b = _pick_lane(ab, GDN_HEADS + head)
    a_log = _pick_lane(_row(sc, 0), head)
    dt_bias = _pick_lane(_row(sc, 1), head)
    beta = jax.nn.sigmoid(b)
    g = -jnp.exp(a_log) * _softplus(a + dt_bias)
    return _two_lanes(g, beta)


def _two_lanes(c0, c1):
    lane = lax.broadcasted_iota(jnp.int32, (c0.shape[0], LANES), 1)
    return jnp.where(lane == 0, c0, jnp.where(lane == 1, c1, 0.0))


def _inverse_group(lows):
    n = lows[0].shape[0]
    ii = lax.broadcasted_iota(jnp.int32, (n, n), 0)
    jj = lax.broadcasted_iota(jnp.int32, (n, n), 1)
    eye = jnp.where(ii == jj, 1.0, 0.0)
    invs = [eye - low for low in lows]
    powers = [_dot3(low, low, 1, 0) for low in lows]
    k = 2
    while k < n:
        invs = [inv + _dot3(inv, p, 1, 0) for inv, p in zip(invs, powers)]
        k *= 2
        if k < n:
            powers = [_dot3(p, p, 1, 0) for p in powers]
    return invs


@jax.custom_vjp
def solve_group(lows, rhss):
    return [_dot3(inv, rhs, 1, 0) for inv, rhs in zip(_inverse_group(lows), rhss)]


def _solve_group_fwd(lows, rhss):
    invs = _inverse_group(lows)
    xs = [_dot3(inv, rhs, 1, 0) for inv, rhs in zip(invs, rhss)]
    return xs, (invs, xs)


def _solve_group_bwd(res, dxs):
    invs, xs = res
    n = invs[0].shape[0]
    strict = lax.broadcasted_iota(jnp.int32, (n, n), 0) > lax.broadcasted_iota(jnp.int32, (n, n), 1)
    drhss = [_dot3(inv, dx, 0, 0) for inv, dx in zip(invs, dxs)]
    dlows = [jnp.where(strict, -_dot3(drhs, x, 1, 1), 0.0) for drhs, x in zip(drhss, xs)]
    return dlows, drhss


solve_group.defvjp(_solve_group_fwd, _solve_group_bwd)


def _gdn_local_group(qs, ks, vs, gbs):
    c = qs[0].shape[0]
    ii = lax.broadcasted_iota(jnp.int32, (c, c), 0)
    jj = lax.broadcasted_iota(jnp.int32, (c, c), 1)
    incl = ii >= jj
    gs = [_pick_lane(gb, 0) for gb in gbs]
    betas = [_pick_lane(gb, 1) for gb in gbs]
    g_rows = [jnp.sum(jnp.where(ii == jj, g, 0.0), axis=0, keepdims=True) for g in gs]
    gc_cols = [jnp.sum(jnp.where(incl, g_row, 0.0), axis=1, keepdims=True) for g_row in g_rows]
    gc_rows = [jnp.sum(jnp.where(jj >= ii, g, 0.0), axis=0, keepdims=True) for g in gs]
    decays = [jnp.where(incl, jnp.exp(jnp.where(incl, gc - gr, 0.0)), 0.0) for gc, gr in zip(gc_cols, gc_rows)]
    kbs = [k * beta for k, beta in zip(ks, betas)]
    lows = [jnp.where(ii > jj, mm_nt(kb, k) * decay, 0.0) for kb, k, decay in zip(kbs, ks, decays)]
    egs = [jnp.exp(gc) for gc in gc_cols]
    wus = solve_group(lows, [jnp.concatenate([kb * eg, v * beta], axis=1) for kb, eg, v, beta in zip(kbs, egs, vs, betas)])
    qks = [mm_nt(q, k) * decay for q, k, decay in zip(qs, ks, decays)]
    g_lasts = [jnp.sum(g_row, axis=1, keepdims=True) for g_row in g_rows]
    kds = [k * jnp.exp(gl - gc) for k, gl, gc in zip(ks, g_lasts, gc_cols)]
    return [(wu[:, :GDN_DK], wu[:, GDN_DK:], qk, q * eg, kd, jnp.exp(gl))
            for wu, qk, q, eg, kd, gl in zip(wus, qks, qs, egs, kds, g_lasts)]


def _gdn_state_step(w, u, qk, qg, kd, eg_last, state):
    v_new = u - mm(w, state)
    o = mm(qg, state) + mm(qk, v_new)
    return o, state * eg_last + mm_tn(kd, v_new)


def _gdn_chunk_packed(q, k, v, gb, state):
    return _gdn_chunk(q, k, v, _pick_lane(gb, 0), _pick_lane(gb, 1), state)


def _gdn_post(o, z, norm_g):
    return _rmsnorm(o, norm_g) * _silu(z)


def _gdn_chunk(q, k, v, g, beta, state):
    c = q.shape[0]
    ii = lax.broadcasted_iota(jnp.int32, (c, c), 0)
    jj = lax.broadcasted_iota(jnp.int32, (c, c), 1)
    incl = ii >= jj
    g_row = jnp.sum(jnp.where(ii == jj, g, 0.0), axis=0, keepdims=True)
    gc_col = jnp.sum(jnp.where(incl, g_row, 0.0), axis=1, keepdims=True)
    gc_row = jnp.sum(jnp.where(jj >= ii, g, 0.0), axis=0, keepdims=True)
    decay = jnp.where(incl, jnp.exp(jnp.where(incl, gc_col - gc_row, 0.0)), 0.0)
    kb = k * beta
    low = jnp.where(ii > jj, mm_nt(kb, k) * decay, 0.0)
    eg = jnp.exp(gc_col)
    wu = unit_lower_solve(low, jnp.concatenate([kb * eg, v * beta], axis=1))
    w, u = wu[:, :GDN_DK], wu[:, GDN_DK:]
    qk = mm_nt(q, k) * decay
    g_last = jnp.sum(g_row, axis=1, keepdims=True)
    kd = k * jnp.exp(g_last - gc_col)
    v_new = u - mm(w, state)
    o = mm(q * eg, state) + mm(qk, v_new)
    new_state = state * jnp.exp(g_last) + mm_tn(kd, v_new)
    return o, new_state


def _attn_block(q, kn, kr, v, q0):
    s = (mm_nt(q[:, :128], kn) + mm_nt(q[:, 128:], kr)) * ATT_SCALE
    qpos = q0 + lax.broadcasted_iota(jnp.int32, s.shape, 0)
    kpos = lax.broadcasted_iota(jnp.int32, s.shape, 1)
    shift = int(math.log2(CHUNK))
    allowed = (kpos >> shift) <= (qpos >> shift)
    s = jnp.where(allowed, s, -1e30)
    p = jnp.exp(s - jnp.max(s, axis=-1, keepdims=True))
    p = p / jnp.sum(p, axis=-1, keepdims=True)
    return mm(p, v)


def _mla_prep(pm, qg, kvg, wq, wkv, cos, sin):
    cq = pm[:, 128:512]
    ckv = pm[:, 512:768]
    qf = mm(_rmsnorm(cq, qg), wq)
    parts = []
    for h in range(MLA_HEADS):
        parts.append(qf[:, h * 256:h * 256 + 128])
        parts.append(rope128(qf[:, h * 256 + 128:h * 256 + 256], cos, sin))
    kvf = mm(_rmsnorm(ckv, kvg), wkv)
    return jnp.concatenate(parts, axis=1), kvf, rope128(pm[:, 768:896], cos, sin)


def _ffn_act(ug, uu, wg, wu, bg, bu):
    return _silu(dwconv(ug, wg) + bg) * (dwconv(uu, wu) + bu)


def _head_loss(h1, ffn, gpre, pp, bgate, g2, b2, target):
    gate = jax.nn.sigmoid(gpre + bgate)
    h2 = _layernorm(ALPHA * h1 + ffn + gate * pp, g2, b2)
    err = h2 - target
    return 0.5 * jnp.sum(jnp.sum(err * err, axis=1, keepdims=True), axis=0, keepdims=True) / D_MODEL


def _params(sem):
    return pltpu.CompilerParams(dimension_semantics=sem, vmem_limit_bytes=VMEM_LIMIT)


def _matmul(a, b, *, name, ta=False, tb=False, tm=512, tn=512, tk=1024, add=None, add_scale=1.0,
            a_halves=False, b_halves=False):
    assert not (a_halves and ta) and not (b_halves and tb)
    a_shape = (a.shape[1], 2 * a.shape[2]) if a_halves else a.shape
    b_shape = (b.shape[1], 2 * b.shape[2]) if b_halves else b.shape
    (k_dim, m) = a_shape if ta else a_shape[::-1]
    (n, k2) = b_shape if tb else b_shape[::-1]
    assert k_dim == k2, (a.shape, b.shape)
    tm, tn, tk = min(tm, m), min(tn, n), min(tk, k_dim)
    assert m % tm == 0 and n % tn == 0 and k_dim % tk == 0, (name, m, n, k_dim, tm, tn, tk)
    nk = k_dim // tk
    ca, cb = (0 if ta else 1), (1 if tb else 0)

    def body(*refs):
        if add is None:
            a_ref, b_ref, o_ref, acc = refs
        else:
            a_ref, b_ref, c_ref, o_ref, acc = refs
        kk = pl.program_id(2)

        @pl.when(kk == 0)
        def _():
            acc[...] = jnp.zeros_like(acc)

        acc[...] += _dot(a_ref[...], b_ref[...], ca, cb)

        @pl.when(kk == nk - 1)
        def _():
            r = acc[...]
            if add is not None:
                r = r + add_scale * c_ref[...]
            o_ref[...] = r

    a_spec = pl.BlockSpec((tk, tm), lambda i, j, k: (k, i)) if ta else pl.BlockSpec((tm, tk), lambda i, j, k: (i, k))
    b_spec = pl.BlockSpec((tn, tk), lambda i, j, k: (j, k)) if tb else pl.BlockSpec((tk, tn), lambda i, j, k: (k, j))
    if a_halves:
        kh = k_dim // 2 // tk
        assert kh * tk * 2 == k_dim
        a_spec = pl.BlockSpec((None, tm, tk), lambda i, j, k: (k // kh, i, k % kh))
    if b_halves:
        nh = n // 2 // tn
        assert nh * tn * 2 == n
        b_spec = pl.BlockSpec((None, tk, tn), lambda i, j, k: (j // nh, k, j % nh))
    in_specs = [a_spec, b_spec]
    args = [a, b]
    if add is not None:
        in_specs.append(pl.BlockSpec((tm, tn), lambda i, j, k: (i, j)))
        args.append(add)
    return pl.pallas_call(
        body, name=name, grid=(m // tm, n // tn, nk),
        in_specs=in_specs, out_specs=pl.BlockSpec((tm, tn), lambda i, j, k: (i, j)),
        out_shape=jax.ShapeDtypeStruct((m, n), F32),
        scratch_shapes=[pltpu.VMEM((tm, tn), F32)],
        compiler_params=_params(("parallel", "parallel", "arbitrary")),
    )(*args)


def _riding(core, n_in, n_out, n_scratch, ride, inner, steps):
    if ride is None:
        return core
    copies, nr = ride[1], len(ride[0])

    def body(*refs):
        cuts = np.cumsum([0, n_in, nr, n_out, nr, n_scratch])
        ins, rin, outs, rout, scratch = (refs[a:b] for a, b in zip(cuts[:-1], cuts[1:]))
        sems = refs[cuts[-1]:]
        step = pl.program_id(0) * inner + pl.program_id(1)

        @pl.when(step == 0)
        def _():
            _exchange_start(copies(rin, rout, sems))

        @pl.when(step == steps - 2)
        def _():
            _exchange_pass_on(copies(rin, rout, sems))

        core(*ins, *outs, *scratch)

        @pl.when(step == steps - 1)
        def _():
            _exchange_finish(copies(rin, rout, sems))

    return body


def _ride_specs(ride):
    if ride is None:
        return [], [], [], [], []
    arrays, _, sems, shapes = ride
    return [HBM_REF] * len(arrays), [HBM_REF] * len(arrays), list(shapes), sems(len(arrays)), list(arrays)


def _gdn_fwd(proj, conv_w, sc, norm_g, bl, s, ride=None):
    nc = s // CHUNK

    def core(ph_ref, ab_ref, cwq_ref, cwk_ref, cwv_ref, sc_ref, ng_ref, cat_ref, o_ref, st_ref, q_s, k_s, v_s, gb_s):
        q_s[...] = _gdn_q(ph_ref[:, 0:128], cwq_ref[...])
        k_s[...] = _gdn_k(ph_ref[:, 128:256], cwk_ref[...])
        v_s[...] = _gdn_v(ph_ref[:, 256:384], cwv_ref[...])
        gb_s[...] = _gdn_gate(ab_ref[...], sc_ref[...], pl.program_id(1))

        group = math.gcd(nc, GDN_FWD_GROUP)

        def chunks(i, state):
            ns = [i * group + j for j in range(group)]
            rows = [pl.ds(pl.multiple_of(n * CHUNK, CHUNK), CHUNK) for n in ns]
            local = _gdn_local_group([q_s[r, :] for r in rows], [k_s[r, :] for r in rows], [v_s[r, :] for r in rows],
                                     [gb_s[r, :] for r in rows])
            for n, r, loc in zip(ns, rows, local):
                st_ref[n] = state
                o_ref[r, :], state = _gdn_state_step(*loc, state)
            return state

        lax.fori_loop(0, nc // group, chunks, jnp.zeros((GDN_DK, GDN_DV), F32))
        cat_ref[...] = _gdn_post(o_ref[...], ph_ref[:, 384:512], ng_ref[...])

    t = bl * s
    r_in, r_out, r_shapes, r_sems, r_args = _ride_specs(ride)
    outs = pl.pallas_call(
        _riding(core, 7, 3, 4, ride, GDN_HEADS, bl * GDN_HEADS), name="gdn_fwd", grid=(bl, GDN_HEADS),
        in_specs=[
            pl.BlockSpec((s, 512), lambda b, h: (b, h)),
            pl.BlockSpec((s, 128), lambda b, h: (b, P_MLA // 128)),
            pl.BlockSpec((GDN_CONV, 128), lambda b, h: (0, h)),
            pl.BlockSpec((GDN_CONV, 128), lambda b, h: (0, GDN_HEADS + h)),
            pl.BlockSpec((GDN_CONV, 128), lambda b, h: (0, 2 * GDN_HEADS + h)),
            pl.BlockSpec((8, 128), lambda b, h: (0, 0)),
            pl.BlockSpec((1, 128), lambda b, h: (0, 0)),
        ] + r_in,
        out_specs=[
            pl.BlockSpec((s, 128), lambda b, h: (b, h)),
            pl.BlockSpec((s, 128), lambda b, h: (b, h)),
            pl.BlockSpec((None, None, nc, GDN_DK, GDN_DV), lambda b, h: (b, h, 0, 0, 0)),
        ] + r_out,
        out_shape=[
            jax.ShapeDtypeStruct((t, 2 * GDN_VW), F32),
            jax.ShapeDtypeStruct((t, GDN_VW), F32),
            jax.ShapeDtypeStruct((bl, GDN_HEADS, nc, GDN_DK, GDN_DV), F32),
        ] + r_shapes,
        scratch_shapes=[pltpu.VMEM((s, 128), F32)] * 4 + r_sems,
        compiler_params=_params(("arbitrary", "arbitrary")),
    )(proj, proj, conv_w, conv_w, conv_w, sc, norm_g, *r_args)
    return outs[0], outs[1], outs[2], list(outs[3:])


def _gdn_bwd(proj, conv_w, sc, norm_g, o_raw, states, dcat, bl, s, ride=None):
    nc = s // CHUNK

    def core(ph_ref, ab_ref, cwq_ref, cwk_ref, cwv_ref, sc_ref, ng_ref, o_ref, st_ref, dc_ref,
             dph_ref, dab_ref, dcwq_ref, dcwk_ref, dcwv_ref, dsc_ref, dng_ref, q_s, k_s, v_s, gb_s, do_s):
        head = pl.program_id(1)
        gate = functools.partial(_gdn_gate, head=head)
        paths = [(_gdn_q, 0, cwq_ref, q_s, dcwq_ref), (_gdn_k, 128, cwk_ref, k_s, dcwk_ref), (_gdn_v, 256, cwv_ref, v_s, dcwv_ref)]
        for fn, col, cw_ref, val_s, _ in paths:
            val_s[...] = fn(ph_ref[:, col:col + 128], cw_ref[...])
        gb_s[...] = gate(ab_ref[...], sc_ref[...])
        _, post_vjp = jax.vjp(_gdn_post, o_ref[...], ph_ref[:, 384:512], ng_ref[...])
        d_o, dz, dng = post_vjp(dc_ref[...])
        do_s[...] = d_o
        dph_ref[:, 384:512] = dz
        dng_ref[...] = jnp.broadcast_to(dng, dng_ref.shape)

        group = math.gcd(nc, GDN_BWD_GROUP)

        def chunks(i, dstate):
            ns = [nc - 1 - (i * group + j) for j in range(group)]
            rows = [pl.ds(pl.multiple_of(n * CHUNK, CHUNK), CHUNK) for n in ns]
            local, local_vjp = jax.vjp(_gdn_local_group, [q_s[r, :] for r in rows], [k_s[r, :] for r in rows],
                                       [v_s[r, :] for r in rows], [gb_s[r, :] for r in rows])
            d_os = [do_s[r, :] for r in rows]
            dlocal = []
            for n, loc, d_o in zip(ns, local, d_os):
                _, step_vjp = jax.vjp(_gdn_state_step, *loc, st_ref[n])
                *dloc, dstate = step_vjp((d_o, dstate))
                dlocal.append(tuple(dloc))
            dqs, dks, dvs, dgbs = local_vjp(dlocal)
            for r, dq, dk, dv, dgb in zip(rows, dqs, dks, dvs, dgbs):
                q_s[r, :], k_s[r, :], v_s[r, :], gb_s[r, :] = dq, dk, dv, dgb
            return dstate

        lax.fori_loop(0, nc // group, chunks, jnp.zeros((GDN_DK, GDN_DV), F32))
        for fn, col, cw_ref, val_s, dcw_ref in paths:
            _, vjp = jax.vjp(fn, ph_ref[:, col:col + 128], cw_ref[...])
            dph_ref[:, col:col + 128], dcw_ref[...] = vjp(val_s[...])
        _, gate_vjp = jax.vjp(gate, ab_ref[...], sc_ref[...])
        dab, dsc_ref[...] = gate_vjp(gb_s[...])

        @pl.when(head == 0)
        def _():
            dab_ref[...] = jnp.zeros_like(dab_ref)

        dab_ref[...] += dab

    t = bl * s
    cw_out = pl.BlockSpec((None, GDN_CONV, 128), lambda b, h: (b, 0, h))
    part = pl.BlockSpec((None, None, 8, 128), lambda b, h: (b, h, 0, 0))
    r_in, r_out, r_shapes, r_sems, r_args = _ride_specs(ride)
    outs = pl.pallas_call(
        _riding(core, 10, 7, 5, ride, GDN_HEADS, bl * GDN_HEADS), name="gdn_bwd", grid=(bl, GDN_HEADS),
        in_specs=[
            pl.BlockSpec((s, 512), lambda b, h: (b, h)),
            pl.BlockSpec((s, 128), lambda b, h: (b, P_MLA // 128)),
            pl.BlockSpec((GDN_CONV, 128), lambda b, h: (0, h)),
            pl.BlockSpec((GDN_CONV, 128), lambda b, h: (0, GDN_HEADS + h)),
            pl.BlockSpec((GDN_CONV, 128), lambda b, h: (0, 2 * GDN_HEADS + h)),
            pl.BlockSpec((8, 128), lambda b, h: (0, 0)),
            pl.BlockSpec((1, 128), lambda b, h: (0, 0)),
            pl.BlockSpec((s, 128), lambda b, h: (b, h)),
            pl.BlockSpec((None, None, nc, GDN_DK, GDN_DV), lambda b, h: (b, h, 0, 0, 0)),
            pl.BlockSpec((s, 128), lambda b, h: (b, h)),
        ] + r_in,
        out_specs=[
            pl.BlockSpec((s, 512), lambda b, h: (b, h)),
            pl.BlockSpec((s, 128), lambda b, h: (b, 0)),
            cw_out, cw_out, cw_out, part, part,
        ] + r_out,
        out_shape=[
            jax.ShapeDtypeStruct((t, P_WIDTH), F32),
            jax.ShapeDtypeStruct((t, 128), F32),
            jax.ShapeDtypeStruct((bl, GDN_CONV, 512), F32),
            jax.ShapeDtypeStruct((bl, GDN_CONV, 512), F32),
            jax.ShapeDtypeStruct((bl, GDN_CONV, 512), F32),
            jax.ShapeDtypeStruct((bl, GDN_HEADS, 8, 128), F32),
            jax.ShapeDtypeStruct((bl, GDN_HEADS, 8, 128), F32),
        ] + r_shapes,
        scratch_shapes=[pltpu.VMEM((s, 128), F32)] * 5 + r_sems,
        compiler_params=_params(("arbitrary", "arbitrary")),
    )(proj, proj, conv_w, conv_w, conv_w, sc, norm_g, o_raw, states, dcat, *r_args)
    return tuple(outs[:7]) + (list(outs[7:]),)


def _mla_prep_fwd(proj, qg, kvg, wq, wkv, cos, sin, s, tm):
    t = proj.shape[0]
    tm = min(tm, s)
    nps = s // tm
    const = lambda shape: pl.BlockSpec(shape, lambda i: (0, 0))

    def body(pm_ref, qg_ref, kvg_ref, wq_ref, wkv_ref, cos_ref, sin_ref, qf_ref, kvf_ref, kr_ref):
        qf, kvf, kr = _mla_prep(pm_ref[...], qg_ref[...], kvg_ref[...], wq_ref[...], wkv_ref[...], cos_ref[...], sin_ref[...])
        qf_ref[...], kvf_ref[...], kr_ref[...] = qf, kvf, kr

    return pl.pallas_call(
        body, name="mla_prep_fwd", grid=(t // tm,),
        in_specs=[
            pl.BlockSpec((tm, 1024), lambda i: (i, P_MLA // 1024)),
            const((1, MLA_Q_LORA)), const((1, MLA_KV_LORA)), const(wq.shape), const(wkv.shape),
            pl.BlockSpec((tm, 128), lambda i: (i % nps, 0)), pl.BlockSpec((tm, 128), lambda i: (i % nps, 0)),
        ],
        out_specs=[pl.BlockSpec((tm, 1024), lambda i: (i, 0)), pl.BlockSpec((tm, 1024), lambda i: (i, 0)),
                   pl.BlockSpec((tm, 128), lambda i: (i, 0))],
        out_shape=[jax.ShapeDtypeStruct((t, 1024), F32), jax.ShapeDtypeStruct((t, 1024), F32),
                   jax.ShapeDtypeStruct((t, 128), F32)],
        compiler_params=_params(("parallel",)),
    )(proj, qg, kvg, wq, wkv, cos, sin)


def _mla_prep_bwd(proj, qg, kvg, wq, wkv, cos, sin, dqf, dkvf, dkr, dab, dproj, s, tm):
    t = proj.shape[0]
    tm = min(tm, s)
    nps = s // tm
    const = lambda shape: pl.BlockSpec(shape, lambda i: (0, 0))

    def body(pm_ref, qg_ref, kvg_ref, wq_ref, wkv_ref, cos_ref, sin_ref, dqf_ref, dkvf_ref, dkr_ref, dab_ref, dp_in,
             dp_ref, dqg_ref, dkvg_ref, dwq_ref, dwkv_ref):
        del dp_in
        fn = lambda pm, qg_, kvg_, wq_, wkv_: _mla_prep(pm, qg_, kvg_, wq_, wkv_, cos_ref[...], sin_ref[...])
        _, vjp = jax.vjp(fn, pm_ref[...], qg_ref[...], kvg_ref[...], wq_ref[...].astype(F32), wkv_ref[...].astype(F32))
        dpm, dqg, dkvg, dwq, dwkv = vjp((dqf_ref[...], dkvf_ref[...], dkr_ref[...]))
        dp_ref[...] = jnp.concatenate([dab_ref[...], dpm[:, 128:]], axis=1)

        @pl.when(pl.program_id(0) == 0)
        def _():
            dqg_ref[...] = jnp.zeros_like(dqg_ref)
            dkvg_ref[...] = jnp.zeros_like(dkvg_ref)
            dwq_ref[...] = jnp.zeros_like(dwq_ref)
            dwkv_ref[...] = jnp.zeros_like(dwkv_ref)

        dqg_ref[...] += dqg
        dkvg_ref[...] += dkvg
        dwq_ref[...] += dwq
        dwkv_ref[...] += dwkv

    rows = lambda w: pl.BlockSpec((tm, w), lambda i: (i, 0))
    return pl.pallas_call(
        body, name="mla_prep_bwd", grid=(t // tm,),
        in_specs=[
            pl.BlockSpec((tm, 1024), lambda i: (i, P_MLA // 1024)),
            const((1, MLA_Q_LORA)), const((1, MLA_KV_LORA)), const(wq.shape), const(wkv.shape),
            pl.BlockSpec((tm, 128), lambda i: (i % nps, 0)), pl.BlockSpec((tm, 128), lambda i: (i % nps, 0)),
            rows(1024), rows(1024), rows(128), rows(128),
            pl.BlockSpec(memory_space=pl.ANY),
        ],
        out_specs=[pl.BlockSpec((tm, 1024), lambda i: (i, P_MLA // 1024)),
                   const((1, MLA_Q_LORA)), const((1, MLA_KV_LORA)), const(wq.shape), const(wkv.shape)],
        out_shape=[jax.ShapeDtypeStruct(dproj.shape, F32),
                   jax.ShapeDtypeStruct((1, MLA_Q_LORA), F32), jax.ShapeDtypeStruct((1, MLA_KV_LORA), F32),
                   jax.ShapeDtypeStruct(wq.shape, F32), jax.ShapeDtypeStruct(wkv.shape, F32)],
        input_output_aliases={11: 0},
        compiler_params=_params(("arbitrary",)),
    )(proj, qg, kvg, wq, wkv, cos, sin, dqf, dkvf, dkr, dab, dproj)


def _attn_fwd(qf, kvf, kr, cat, bl, s, tq):
    tq = min(tq, s)
    nq = s // tq

    def body(q_ref, kv_ref, kr_ref, cat_in, o_ref):
        del cat_in
        for i in range(nq):
            rows, keys = slice(i * tq, (i + 1) * tq), slice(0, (i + 1) * tq)
            o_ref[rows, :] = _attn_block(q_ref[rows, :], kv_ref[keys, 0:128], kr_ref[keys, :], kv_ref[keys, 128:256], i * tq)

    return pl.pallas_call(
        body, name="attn_fwd", grid=(bl, MLA_HEADS),
        in_specs=[
            pl.BlockSpec((s, 256), lambda b, h: (b, h)),
            pl.BlockSpec((s, 256), lambda b, h: (b, h)),
            pl.BlockSpec((s, 128), lambda b, h: (b, 0)),
            pl.BlockSpec(memory_space=pl.ANY),
        ],
        out_specs=pl.BlockSpec((s, 128), lambda b, h: (b, GDN_HEADS + h)),
        out_shape=jax.ShapeDtypeStruct(cat.shape, F32),
        input_output_aliases={3: 0},
        compiler_params=_params(("parallel", "parallel")),
    )(qf, kvf, kr, cat)


def _attn_bwd(qf, kvf, kr, dcat, bl, s, tq):
    tq = min(tq, s)
    nq = s // tq

    def body(q_ref, kv_ref, kr_ref, do_ref, dq_ref, dkv_ref, dkr_ref):
        dkv_ref[...] = jnp.zeros_like(dkv_ref)

        @pl.when(pl.program_id(1) == 0)
        def _():
            dkr_ref[...] = jnp.zeros_like(dkr_ref)

        for i in range(nq):
            rows, keys = slice(i * tq, (i + 1) * tq), slice(0, (i + 1) * tq)
            fn = functools.partial(_attn_block, q0=i * tq)
            _, vjp = jax.vjp(fn, q_ref[rows, :], kv_ref[keys, 0:128], kr_ref[keys, :], kv_ref[keys, 128:256])
            dq_ref[rows, :], dkn, dkr, dv = vjp(do_ref[rows, :])
            dkv_ref[keys, 0:128] += dkn
            dkv_ref[keys, 128:256] += dv
            dkr_ref[keys, :] += dkr

    t = bl * s
    return pl.pallas_call(
        body, name="attn_bwd", grid=(bl, MLA_HEADS),
        in_specs=[
            pl.BlockSpec((s, 256), lambda b, h: (b, h)),
            pl.BlockSpec((s, 256), lambda b, h: (b, h)),
            pl.BlockSpec((s, 128), lambda b, h: (b, 0)),
            pl.BlockSpec((s, 128), lambda b, h: (b, GDN_HEADS + h)),
        ],
        out_specs=[
            pl.BlockSpec((s, 256), lambda b, h: (b, h)),
            pl.BlockSpec((s, 256), lambda b, h: (b, h)),
            pl.BlockSpec((s, 128), lambda b, h: (b, 0)),
        ],
        out_shape=[jax.ShapeDtypeStruct((t, 1024), F32), jax.ShapeDtypeStruct((t, 1024), F32),
                   jax.ShapeDtypeStruct((t, 128), F32)],
        compiler_params=_params(("parallel", "arbitrary")),
    )(qf, kvf, kr, dcat)


def _ln1_fwd(x, mix, g, b, tm):
    t = x.shape[0]
    tm = min(tm, t)

    def body(x_ref, mix_ref, g_ref, b_ref, r_ref, h_ref):
        r = ALPHA * x_ref[...] + mix_ref[...]
        r_ref[...] = r
        h_ref[...] = _layernorm(r, g_ref[...], b_ref[...])

    rows = pl.BlockSpec((tm, D_MODEL), lambda i: (i, 0))
    vec = pl.BlockSpec((1, D_MODEL), lambda i: (0, 0))
    return pl.pallas_call(
        body, name="ln1_fwd", grid=(t // tm,), in_specs=[rows, rows, vec, vec], out_specs=[rows, rows],
        out_shape=[jax.ShapeDtypeStruct(x.shape, F32)] * 2, compiler_params=_params(("parallel",)),
    )(x, mix, g, b)


def _ln1_bwd(r1, dr2, da, db_, g, b, tm):
    t = r1.shape[0]
    tm = min(tm, t)

    def body(r_ref, d2_ref, da_ref, db_ref, g_ref, b_ref, dr_ref, dg_ref, dbias_ref):
        dh = ALPHA * d2_ref[...] + da_ref[...] + db_ref[...]
        _, vjp = jax.vjp(_layernorm, r_ref[...], g_ref[...], b_ref[...])
        dr, dg, dbias = vjp(dh)
        dr_ref[...] = dr

        @pl.when(pl.program_id(0) == 0)
        def _():
            dg_ref[...] = jnp.zeros_like(dg_ref)
            dbias_ref[...] = jnp.zeros_like(dbias_ref)

        dg_ref[...] += dg
        dbias_ref[...] += dbias

    rows = pl.BlockSpec((tm, D_MODEL), lambda i: (i, 0))
    vec = pl.BlockSpec((1, D_MODEL), lambda i: (0, 0))
    return pl.pallas_call(
        body, name="ln1_bwd", grid=(t // tm,), in_specs=[rows] * 4 + [vec, vec], out_specs=[rows, vec, vec],
        out_shape=[jax.ShapeDtypeStruct(r1.shape, F32)] + [jax.ShapeDtypeStruct((1, D_MODEL), F32)] * 2,
        compiler_params=_params(("arbitrary",)),
    )(r1, dr2, da, db_, g, b)


def _ffn_act_fwd(u, conv_w, conv_b, bl, s, cb):
    nj = D_FF // cb

    def body(ug_ref, uu_ref, wg_ref, wu_ref, bg_ref, bu_ref, act_ref):
        act_ref[...] = _ffn_act(ug_ref[...], uu_ref[...], wg_ref[...], wu_ref[...], bg_ref[...], bu_ref[...])

    return pl.pallas_call(
        body, name="ffn_act_fwd", grid=(bl, nj),
        in_specs=[
            pl.BlockSpec((s, cb), lambda b, j: (b, j)), pl.BlockSpec((s, cb), lambda b, j: (b, nj + j)),
            pl.BlockSpec((FFN_CONV, cb), lambda b, j: (0, j)), pl.BlockSpec((FFN_CONV, cb), lambda b, j: (0, nj + j)),
            pl.BlockSpec((1, cb), lambda b, j: (0, j)), pl.BlockSpec((1, cb), lambda b, j: (0, nj + j)),
        ],
        out_specs=pl.BlockSpec((s, cb), lambda b, j: (b, j)),
        out_shape=jax.ShapeDtypeStruct((bl * s, D_FF), F32),
        compiler_params=_params(("parallel", "parallel")),
    )(u, u, conv_w, conv_w, conv_b, conv_b)


def _ffn_act_bwd(u, conv_w, conv_b, dact, bl, s, cb):
    nj = D_FF // cb

    def body(ug_ref, uu_ref, wg_ref, wu_ref, bg_ref, bu_ref, da_ref, du_ref, dwg_ref, dwu_ref, dbg_ref, dbu_ref):
        _, vjp = jax.vjp(_ffn_act, ug_ref[...], uu_ref[...], wg_ref[...], wu_ref[...], bg_ref[...], bu_ref[...])
        du_ref[0], du_ref[1], dwg_ref[...], dwu_ref[...], dbg_ref[...], dbu_ref[...] = vjp(da_ref[...])

    t = bl * s
    blk = pl.BlockSpec((s, cb), lambda b, j: (b, j))
    wpart = pl.BlockSpec((None, FFN_CONV, cb), lambda b, j: (b, 0, j))
    bpart = pl.BlockSpec((None, 1, cb), lambda b, j: (b, 0, j))
    return pl.pallas_call(
        body, name="ffn_act_bwd", grid=(bl, nj),
        in_specs=[
            blk, pl.BlockSpec((s, cb), lambda b, j: (b, nj + j)),
            pl.BlockSpec((FFN_CONV, cb), lambda b, j: (0, j)), pl.BlockSpec((FFN_CONV, cb), lambda b, j: (0, nj + j)),
            pl.BlockSpec((1, cb), lambda b, j: (0, j)), pl.BlockSpec((1, cb), lambda b, j: (0, nj + j)),
            blk,
        ],
        out_specs=[pl.BlockSpec((2, s, cb), lambda b, j: (0, b, j)), wpart, wpart, bpart, bpart],
        out_shape=[jax.ShapeDtypeStruct((2, t, D_FF), F32)] + [jax.ShapeDtypeStruct((bl, FFN_CONV, D_FF), F32)] * 2
        + [jax.ShapeDtypeStruct((bl, 1, D_FF), F32)] * 2,
        compiler_params=_params(("parallel", "parallel")),
    )(u, u, conv_w, conv_w, conv_b, conv_b, dact)


def _head(h1, ffn, gpre, pp, bgate, g2, b2, target, tm):
    t = h1.shape[0]
    tm = min(tm, t)

    def body(h1_ref, ffn_ref, gp_ref, pp_ref, bg_ref, g2_ref, b2_ref, tg_ref,
             dr_ref, dgp_ref, dpp_ref, loss_ref, dbg_ref, dg2_ref, db2_ref):
        fn = functools.partial(_head_loss, target=tg_ref[...])
        loss, vjp = jax.vjp(fn, h1_ref[...], ffn_ref[...], gp_ref[...], pp_ref[...], bg_ref[...], g2_ref[...], b2_ref[...])
        _, dffn, dgp, dpp, dbg, dg2, db2 = vjp(jnp.ones((1, 1), F32))
        dr_ref[...], dgp_ref[...], dpp_ref[...] = dffn, dgp, dpp

        @pl.when(pl.program_id(0) == 0)
        def _():
            loss_ref[...] = jnp.zeros_like(loss_ref)
            dbg_ref[...] = jnp.zeros_like(dbg_ref)
            dg2_ref[...] = jnp.zeros_like(dg2_ref)
            db2_ref[...] = jnp.zeros_like(db2_ref)

        loss_ref[...] += jnp.broadcast_to(loss, loss_ref.shape)
        dbg_ref[...] += dbg
        dg2_ref[...] += dg2
        db2_ref[...] += db2

    rows = pl.BlockSpec((tm, D_MODEL), lambda i: (i, 0))
    vec = pl.BlockSpec((1, D_MODEL), lambda i: (0, 0))
    return pl.pallas_call(
        body, name="head", grid=(t // tm,), in_specs=[rows] * 4 + [vec] * 3 + [rows],
        out_specs=[rows] * 3 + [pl.BlockSpec((8, 128), lambda i: (0, 0))] + [vec] * 3,
        out_shape=[jax.ShapeDtypeStruct(h1.shape, F32)] * 3 + [jax.ShapeDtypeStruct((8, 128), F32)]
        + [jax.ShapeDtypeStruct((1, D_MODEL), F32)] * 3,
        compiler_params=_params(("arbitrary",)),
    )(h1, ffn, gpre, pp, bgate, g2, b2, target)


def _adam_update(g, w_ref, m_ref, v_ref, g_ref, d_ref, nm_ref, nv_ref):
    m2 = ADAM_B1 * m_ref[...] + (1.0 - ADAM_B1) * g
    v2 = ADAM_B2 * v_ref[...] + (1.0 - ADAM_B2) * jnp.square(g)
    m_hat = m2 / (1.0 - ADAM_B1 ** ADAM_STEP)
    v_hat = v2 / (1.0 - ADAM_B2 ** ADAM_STEP)
    g_ref[...] = g
    d_ref[...] = -ADAM_LR * (m_hat / (jnp.sqrt(v_hat) + ADAM_EPS) + ADAM_WD * w_ref[...])
    nm_ref[...] = m2
    nv_ref[...] = v2


def _row_tile(rows, cols, limit_bytes=256 * 1024):
    best = None
    for t in range(8, rows + 1, 8):
        if rows % t == 0 and t * cols * 4 <= limit_bytes:
            best = t
    return best or rows


def _adamw_reduced(own, recv, w, m, v, name):
    a, b = w.shape
    ta = _row_tile(a, b)

    def body(own_ref, recv_ref, w_ref, m_ref, v_ref, g_ref, d_ref, nm_ref, nv_ref):
        c = lax.axis_index("c")
        for core in range(2):
            @pl.when(c == core)
            def _():
                same = [own_ref[...], recv_ref[0], recv_ref[1], recv_ref[2]]
                other = [recv_ref[3], recv_ref[4], recv_ref[5], recv_ref[6]]
                core0, core1 = (same, other) if core == 0 else (other, same)
                g = core0[0] + core1[0]
                for r in range(1, N_CHIPS):
                    g = (g + core0[r]) + core1[r]
                _adam_update(g, w_ref, m_ref, v_ref, g_ref, d_ref, nm_ref, nv_ref)

    blk = pl.BlockSpec((ta, b), lambda i: (i, 0))
    return pl.pallas_call(
        body, name=name, grid=(a // ta,),
        in_specs=[blk, pl.BlockSpec((7, ta, b), lambda i: (0, i, 0)), blk, blk, blk], out_specs=[blk] * 4,
        out_shape=[jax.ShapeDtypeStruct(w.shape, F32)] * 4, compiler_params=_params(("parallel",)),
    )(own, recv, w, m, v)


def _adamw_small(g, w, m, v):
    def body(g_in, w_ref, m_ref, v_ref, g_ref, d_ref, nm_ref, nv_ref):
        _adam_update(g_in[...], w_ref, m_ref, v_ref, g_ref, d_ref, nm_ref, nv_ref)

    blk = pl.BlockSpec(w.shape, lambda i: (0, 0))
    return pl.pallas_call(
        body, name="adamw_small", grid=(1,), in_specs=[blk] * 4, out_specs=[blk] * 4,
        out_shape=[jax.ShapeDtypeStruct(w.shape, F32)] * 4, compiler_params=_params(("arbitrary",)),
    )(g, w, m, v)


def _remote(src, dst, send_sem, recv_sem, device):
    return pltpu.make_async_remote_copy(src_ref=src, dst_ref=dst, send_sem=send_sem, recv_sem=recv_sem,
                                        device_id=device, device_id_type=MESH)


def _place():
    x, y, c = lax.axis_index("x"), lax.axis_index("y"), lax.axis_index("c")
    return x, y, c, 2 * x + y, [(1 - x, y), (x, 1 - y), (1 - x, 1 - y)]


HBM_REF = pl.BlockSpec(memory_space=pl.ANY)
HALF_ROWS_QUANTUM = 16


def _gather_sems(n):
    return [pltpu.SemaphoreType.DMA((3 * n,))] * 4 + [pltpu.SemaphoreType.DMA((n,))]


def _gather_copies(ins, outs, sems):
    send_s, recv_s, fsend_s, frecv_s, local_s = sems
    x, y, c, me, chips = _place()
    local, sends, steps = [], [], []
    for i, (src, dst) in enumerate(zip(ins, outs)):
        local.append(pltpu.make_async_copy(src, dst.at[me], local_s.at[i]))
        half = src.shape[0] // 2
        split = src.shape[0] % (2 * HALF_ROWS_QUANTUM) == 0
        if split:
            mine = pl.ds(pl.multiple_of(c * half, HALF_ROWS_QUANTUM), half)
            theirs = pl.ds(pl.multiple_of((1 - c) * half, HALF_ROWS_QUANTUM), half)
        for r, (px, py) in enumerate(chips):
            k, peer = 3 * i + r, 2 * px + py
            if split:
                sends.append(_remote(src.at[mine], dst.at[me, mine], send_s.at[k], recv_s.at[k], (px, py, c)))
                landed = dst.at[peer, mine]
                steps.append((_remote(src.at[mine], landed, send_s.at[k], recv_s.at[k], (px, py, c)),
                              _remote(landed, landed, fsend_s.at[k], frecv_s.at[k], (x, y, 1 - c)),
                              _remote(dst.at[peer, theirs], dst.at[peer, theirs], fsend_s.at[k], frecv_s.at[k], (x, y, 1 - c))))
            else:
                sends.append(_remote(src, dst.at[me], send_s.at[k], recv_s.at[k], (px, py, c)))
                steps.append((_remote(src, dst.at[peer], send_s.at[k], recv_s.at[k], (px, py, c)), None, None))
    return local, sends, steps


def _scatter_sems(n):
    return [pltpu.SemaphoreType.DMA((4 * n,))] * 2 + [pltpu.SemaphoreType.DMA((3 * n,))] * 2


def _scatter_copies(ins, outs, sems):
    send_s, recv_s, fsend_s, frecv_s = sems
    x, y, c, me, chips = _place()
    sends, steps = [], []
    for i, (src, dst) in enumerate(zip(ins, outs)):
        for r, (px, py) in enumerate(chips):
            k = 4 * i + r
            cp = _remote(src.at[2 * px + py], dst.at[r], send_s.at[k], recv_s.at[k], (px, py, c))
            fwd = _remote(dst.at[r], dst.at[4 + r], fsend_s.at[3 * i + r], frecv_s.at[3 * i + r], (x, y, 1 - c))
            sends.append(cp)
            steps.append((cp, fwd, fwd))
        k = 4 * i + 3
        cp = _remote(src.at[me], dst.at[3], send_s.at[k], recv_s.at[k], (x, y, 1 - c))
        sends.append(cp)
        steps.append((cp, None, None))
    return [], sends, steps


def _exchange_start(plan):
    local, sends, _ = plan
    for cp in local + sends:
        cp.start()


def _exchange_pass_on(plan):
    for arrival, pass_on, _ in plan[2]:
        arrival.wait_recv()
        if pass_on is not None:
            pass_on.start()


def _exchange_finish(plan):
    local, sends, steps = plan
    for _, pass_on, passed in steps:
        if pass_on is not None:
            passed.wait_recv()
    for cp in sends:
        cp.wait_send()
    for _, pass_on, _ in steps:
        if pass_on is not None:
            pass_on.wait_send()
    for cp in local:
        cp.wait()


def _exchange_call(arrays, copies, sems, out_shapes, name):
    n = len(arrays)

    def body(*refs):
        plan = copies(refs[:n], refs[n:2 * n], refs[2 * n:])
        _exchange_start(plan)
        _exchange_pass_on(plan)
        _exchange_finish(plan)

    return pl.pallas_call(
        body, name=name, in_specs=[HBM_REF] * n, out_specs=[HBM_REF] * n, out_shape=out_shapes,
        scratch_shapes=sems(n), compiler_params=pltpu.CompilerParams(has_side_effects=True),
    )(*arrays)


def _gather_call(shards, name):
    shapes = [jax.ShapeDtypeStruct((N_CHIPS,) + a.shape, a.dtype) for a in shards]
    return _exchange_call(shards, _gather_copies, _gather_sems, shapes, name)


def _scatter_call(slabs, name):
    shapes = [jax.ShapeDtypeStruct((N_DEV - 1,) + a.shape[1:], a.dtype) for a in slabs]
    return _exchange_call(slabs, _scatter_copies, _scatter_sems, shapes, name)


def _all_reduce_small(a):
    def body(in_ref, out_ref, slots, send_sems, recv_sems):
        x, y, c = lax.axis_index("x"), lax.axis_index("y"), lax.axis_index("c")
        me = 4 * x + 2 * y + c
        slots[0] = in_ref[...]
        sends = []
        for r in range(1, N_DEV):
            peer = (x ^ (r >> 2), y ^ ((r >> 1) & 1), c ^ (r & 1))
            sends.append(pltpu.make_async_remote_copy(src_ref=in_ref, dst_ref=slots.at[r], send_sem=send_sems.at[r],
                                                      recv_sem=recv_sems.at[r], device_id=peer, device_id_type=MESH))
        for cp in sends:
            cp.start()
        for cp in sends:
            cp.wait_recv()
        acc = slots[me]
        for dev in range(1, N_DEV):
            acc = acc + slots[dev ^ me]
        out_ref[...] = acc
        for cp in sends:
            cp.wait_send()

    return pl.pallas_call(
        body, name="small_all_reduce",
        in_specs=[pl.BlockSpec(memory_space=pltpu.VMEM)], out_specs=pl.BlockSpec(memory_space=pltpu.VMEM),
        out_shape=jax.ShapeDtypeStruct(a.shape, a.dtype),
        scratch_shapes=[pltpu.VMEM((N_DEV,) + a.shape, a.dtype), pltpu.SemaphoreType.DMA((N_DEV,)),
                        pltpu.SemaphoreType.DMA((N_DEV,))],
        compiler_params=pltpu.CompilerParams(has_side_effects=True),
    )(a)


SHARDED = ["w_in", "mla_w_q_up", "mla_w_kv_up", "w_out", "ffn_w_up", "ffn_w_down", "ple_w_gate", "ple_w_proj",
           "gdn_conv_w", "ffn_conv_w"]
SHARD_AXIS = {"w_in": 1, "mla_w_q_up": 1, "mla_w_kv_up": 1, "w_out": 0, "ffn_w_up": 1, "ffn_w_down": 0,
              "ple_w_gate": 0, "ple_w_proj": 1, "gdn_conv_w": 1, "ffn_conv_w": 1}
SMALL = ["gdn_a_log", "gdn_dt_bias", "gdn_norm_g", "mla_q_norm_g", "mla_kv_norm_g", "ln1_g", "ln1_b", "ffn_conv_b",
         "ple_b_gate", "ln2_g", "ln2_b"]
WEIGHTS = ["w_in", "gdn_conv_w", "gdn_a_log", "gdn_dt_bias", "gdn_norm_g", "mla_q_norm_g", "mla_w_q_up", "mla_kv_norm_g",
           "mla_w_kv_up", "w_out", "ln1_g", "ln1_b", "ffn_w_up", "ffn_conv_w", "ffn_conv_b", "ffn_w_down", "ple_w_gate",
           "ple_b_gate", "ple_w_proj", "ln2_g", "ln2_b"]
F32_ON_WIRE = ("gdn_conv_w", "ffn_conv_w")
GATHER_EARLY = ["w_in", "gdn_conv_w", "mla_w_q_up", "mla_w_kv_up"]
GATHER_LATE = ["w_out", "ffn_w_up", "ffn_conv_w", "ffn_w_down", "ple_w_gate", "ple_w_proj"]
SCATTER_EARLY = ["ffn_w_up", "ffn_conv_w", "ffn_w_down", "ple_w_gate", "ple_w_proj", "w_out"]
SCATTER_LATE = ["w_in", "gdn_conv_w", "mla_w_q_up", "mla_w_kv_up"]
PACK_COLS = 1024
PACK_ROW_TILE = 8


def _join_blocks(blocks, axis):
    n, a, b = blocks.shape
    if axis == 0:
        return blocks.reshape(n * a, b)
    return jnp.transpose(blocks, (1, 0, 2)).reshape(a, n * b)


def _split_blocks(full, axis):
    if axis == 0:
        return full.reshape(N_CHIPS, full.shape[0] // N_CHIPS, full.shape[1])
    a, nb = full.shape
    return jnp.transpose(full.reshape(a, N_CHIPS, nb // N_CHIPS), (1, 0, 2))


def _pack(arrays):
    flat = jnp.concatenate([a.reshape(-1) for a in arrays])
    quantum = PACK_COLS * PACK_ROW_TILE
    padded = -(-flat.shape[0] // quantum) * quantum
    return jnp.pad(flat, (0, padded - flat.shape[0])).reshape(-1, PACK_COLS)


def _unpack(packed, shapes):
    flat = packed.reshape(-1)
    out, off = [], 0
    for shp in shapes:
        n = int(np.prod(shp))
        out.append(flat[off:off + n].reshape(shp))
        off += n
    return out


def kernel(x, p, w_in, gdn_conv_w, gdn_a_log, gdn_dt_bias, gdn_norm_g, mla_q_norm_g, mla_w_q_up, mla_kv_norm_g, mla_w_kv_up, w_out, ln1_g, ln1_b, ffn_w_up, ffn_conv_w, ffn_conv_b, ffn_w_down, ple_w_gate, ple_b_gate, ple_w_proj, ln2_g, ln2_b, loss_target, m_w_in, m_gdn_conv_w, m_gdn_a_log, m_gdn_dt_bias, m_gdn_norm_g, m_mla_q_norm_g, m_mla_w_q_up, m_mla_kv_norm_g, m_mla_w_kv_up, m_w_out, m_ln1_g, m_ln1_b, m_ffn_w_up, m_ffn_conv_w, m_ffn_conv_b, m_ffn_w_down, m_ple_w_gate, m_ple_b_gate, m_ple_w_proj, m_ln2_g, m_ln2_b, v_w_in, v_gdn_conv_w, v_gdn_a_log, v_gdn_dt_bias, v_gdn_norm_g, v_mla_q_norm_g, v_mla_w_q_up, v_mla_kv_norm_g, v_mla_w_kv_up, v_w_out, v_ln1_g, v_ln1_b, v_ffn_w_up, v_ffn_conv_w, v_ffn_conv_b, v_ffn_w_down, v_ple_w_gate, v_ple_b_gate, v_ple_w_proj, v_ln2_g, v_ln2_b):
    given = dict(locals())
    wsh = {n: given[n][0] for n in WEIGHTS}
    msh = {n: given["m_" + n][0] for n in WEIGHTS}
    vsh = {n: given["v_" + n][0] for n in WEIGHTS}
    bl, s, _ = x.shape
    t = bl * s
    xt = x.reshape(t, D_MODEL)
    pt = p.reshape(t, PLE_DIM)
    target = loss_target.reshape(t, D_MODEL)

    wire = lambda n: wsh[n] if n in F32_ON_WIRE else wsh[n].astype(BF16)
    early = _gather_call([wire(n) for n in GATHER_EARLY], "weights_gather_early")
    full = {n: _join_blocks(g, SHARD_AXIS[n]) for n, g in zip(GATHER_EARLY, early)}
    late_shards = [wire(n) for n in GATHER_LATE]
    late_ride = (late_shards, _gather_copies, _gather_sems,
                 [jax.ShapeDtypeStruct((N_CHIPS,) + a.shape, a.dtype) for a in late_shards])

    in_cols, q_cols = _w_in_cols(), _w_q_cols()
    w_in_p = _pad_cols(full["w_in"], in_cols)
    w_q_p = _pad_cols(full["mla_w_q_up"], q_cols)
    w_kv, gconv = full["mla_w_kv_up"], full["gdn_conv_w"]
    row = lambda a: a.reshape(1, -1)
    sc = jnp.zeros((8, 128), F32).at[0, :GDN_HEADS].set(wsh["gdn_a_log"]).at[1, :GDN_HEADS].set(wsh["gdn_dt_bias"])
    norm_g, qg, kvg = row(wsh["gdn_norm_g"]), row(wsh["mla_q_norm_g"]), row(wsh["mla_kv_norm_g"])
    g1, b1, g2, b2 = row(wsh["ln1_g"]), row(wsh["ln1_b"]), row(wsh["ln2_g"]), row(wsh["ln2_b"])
    fbias, bgate = row(wsh["ffn_conv_b"]), row(wsh["ple_b_gate"])

    inv = ROPE_THETA ** (-jnp.arange(0, MLA_ROPE, 2, dtype=F32) / MLA_ROPE)
    ang = jnp.arange(s, dtype=F32)[:, None] * inv[None, :]
    zero = jnp.zeros_like(ang)
    cos_t = jnp.concatenate([jnp.cos(ang), zero, jnp.cos(ang), zero], axis=1)
    sin_t = jnp.concatenate([-jnp.sin(ang), zero, jnp.sin(ang), zero], axis=1)

    proj = _matmul(xt, w_in_p, name="proj")
    cat, o_raw, states, late = _gdn_fwd(proj, gconv, sc, norm_g, bl, s, late_ride)
    full.update({n: _join_blocks(g, SHARD_AXIS[n]) for n, g in zip(GATHER_LATE, late)})
    w_o, w_up, w_down = full["w_out"], full["ffn_w_up"], full["ffn_w_down"]
    w_gate, w_proj, fconv = full["ple_w_gate"], full["ple_w_proj"], full["ffn_conv_w"]
    qf, kvf, kr = _mla_prep_fwd(proj, qg, kvg, w_q_p, w_kv, cos_t, sin_t, s, 256)
    cat = _attn_fwd(qf, kvf, kr, cat, bl, s, 256)
    mix = _matmul(cat, w_o, name="mix")
    r1, h1 = _ln1_fwd(xt, mix, g1, b1, 256)
    u = _matmul(h1, w_up, name="ffn_up")
    act = _ffn_act_fwd(u, fconv, fbias, bl, s, 256)
    ffn = _matmul(act, w_down, name="ffn_down", tk=1408)
    gpre = _matmul(h1, w_gate, name="ple_gate")
    pp = _matmul(pt, w_proj, name="ple_proj")
    dr2, dgpre, dpp, loss_acc, dbgate, dg2, db2 = _head(h1, ffn, gpre, pp, bgate, g2, b2, target, 256)

    dact = _matmul(dr2, w_down, name="d_act", tb=True, tn=256)
    d_w_down = _matmul(act, dr2, name="dw_down", ta=True, tm=256)
    du, dfcw_g, dfcw_u, dfcb_g, dfcb_u = _ffn_act_bwd(u, fconv, fbias, dact, bl, s, 256)
    dh1_a = _matmul(du, w_up, name="dh1_ffn", tb=True, tk=1408, a_halves=True)
    dh1_b = _matmul(dgpre, w_gate, name="dh1_ple", tb=True)
    d_w_up = _matmul(h1, du, name="dw_up", ta=True, tn=1408, b_halves=True)
    d_w_gate = _matmul(h1, dgpre, name="dw_gate", ta=True)
    d_w_proj = _matmul(pt, dpp, name="dw_proj", ta=True)
    dr1, dg1, db1 = _ln1_bwd(r1, dr2, dh1_a, dh1_b, g1, b1, 256)
    dcat = _matmul(dr1, w_o, name="d_cat", tb=True)
    d_w_o = _matmul(cat, dr1, name="dw_out", ta=True)

    gfull = {
        "ffn_w_up": d_w_up, "ffn_w_down": d_w_down, "ple_w_gate": d_w_gate, "ple_w_proj": d_w_proj, "w_out": d_w_o,
        "ffn_conv_w": jnp.concatenate([jnp.sum(dfcw_g, 0), jnp.sum(dfcw_u, 0)], axis=1),
    }
    slabs = {n: _split_blocks(gfull[n], SHARD_AXIS[n]) for n in SCATTER_EARLY}
    early_slabs = [slabs[n] for n in SCATTER_EARLY]
    early_ride = (early_slabs, _scatter_copies, _scatter_sems,
                  [jax.ShapeDtypeStruct((N_DEV - 1,) + a.shape[1:], a.dtype) for a in early_slabs])
    dproj, dab, dcwq, dcwk, dcwv, dsc, dng, early_recv = _gdn_bwd(proj, gconv, sc, norm_g, o_raw, states, dcat, bl, s, early_ride)
    received = dict(zip(SCATTER_EARLY, early_recv))
    dqf, dkvf, dkr = _attn_bwd(qf, kvf, kr, dcat, bl, s, 256)
    dproj, dqg, dkvg, d_w_q_p, d_w_kv = _mla_prep_bwd(proj, qg, kvg, w_q_p, w_kv, cos_t, sin_t, dqf, dkvf, dkr, dab, dproj, s, 256)
    grad_x = _matmul(dproj, w_in_p, name="d_x", tb=True, add=dr1, add_scale=ALPHA)
    d_w_in_p = _matmul(xt, dproj, name="dw_in", ta=True)

    gfull.update({
        "w_in": _unpad_cols(d_w_in_p, in_cols, D_IN),
        "mla_w_q_up": _unpad_cols(d_w_q_p, q_cols, MLA_HEADS * (MLA_NOPE + MLA_ROPE)),
        "mla_w_kv_up": d_w_kv,
        "gdn_conv_w": jnp.concatenate([jnp.sum(dcwq, 0), jnp.sum(dcwk, 0), jnp.sum(dcwv, 0)], axis=1),
    })
    dsc_sum = jnp.sum(dsc, axis=(0, 1))
    gsmall = {
        "gdn_a_log": dsc_sum[0, :GDN_HEADS], "gdn_dt_bias": dsc_sum[1, :GDN_HEADS],
        "gdn_norm_g": jnp.sum(dng[:, :, 0, :], axis=(0, 1)),
        "mla_q_norm_g": dqg[0], "mla_kv_norm_g": dkvg[0], "ln1_g": dg1[0], "ln1_b": db1[0],
        "ffn_conv_b": jnp.concatenate([jnp.sum(dfcb_g, 0), jnp.sum(dfcb_u, 0)], axis=1)[0],
        "ple_b_gate": dbgate[0], "ln2_g": dg2[0], "ln2_b": db2[0],
    }

    slabs.update({n: _split_blocks(gfull[n], SHARD_AXIS[n]) for n in SCATTER_LATE})
    me_chip = 2 * lax.axis_index("x") + lax.axis_index("y")
    received.update(zip(SCATTER_LATE, _scatter_call([slabs[n] for n in SCATTER_LATE], "grads_scatter_late")))
    big = [{}, {}, {}, {}]
    for n in SHARDED:
        own = lax.dynamic_index_in_dim(slabs[n], me_chip, 0, keepdims=False)
        for kind, val in enumerate(_adamw_reduced(own, received[n], wsh[n], msh[n], vsh[n], "adamw_" + n)):
            big[kind][n] = val

    small_shapes = [wsh[n].shape for n in SMALL]
    gsum = _all_reduce_small(_pack([gsmall[n] for n in SMALL]))
    spacks = _adamw_small(gsum, _pack([wsh[n] for n in SMALL]), _pack([msh[n] for n in SMALL]), _pack([vsh[n] for n in SMALL]))
    small = [dict(zip(SMALL, _unpack(pk, small_shapes))) for pk in spacks]

    loss = lax.psum(loss_acc[0, 0], ("x", "y", "c"))
    outs = [loss, grad_x.reshape(x.shape)]
    for kind in range(4):
        for n in WEIGHTS:
            val = big[kind][n] if n in big[kind] else small[kind][n]
            outs.append(val[None])
    return tuple(outs)
```

```python
import functools
import math

import numpy as np
import jax
import jax.numpy as jnp
from jax import lax
from jax.experimental import pallas as pl
from jax.experimental.pallas import tpu as pltpu

F32 = jnp.float32
BF16 = jnp.bfloat16

D_MODEL = 1024
CHUNK = 64
PLE_DIM = 256
GDN_HEADS = 4
GDN_DK = 128
GDN_DV = 128
GDN_CONV = 4
MLA_HEADS = 4
MLA_NOPE = 128
MLA_ROPE = 64
MLA_V = 128
MLA_Q_LORA = 384
MLA_KV_LORA = 256
ROPE_THETA = 10000.0
D_FF = 2816
FFN_CONV = 3
DEPTH = 1
ALPHA = (2.0 * DEPTH) ** 0.25
NORM_EPS = 1e-6
GDN_QK = GDN_HEADS * GDN_DK
GDN_VW = GDN_HEADS * GDN_DV
D_IN = 2 * GDN_QK + 2 * GDN_VW + 2 * GDN_HEADS + MLA_Q_LORA + MLA_KV_LORA + MLA_ROPE
ATT_SCALE = (MLA_NOPE + MLA_ROPE) ** -0.5

ADAM_LR = 0.001
ADAM_B1 = 0.9
ADAM_B2 = 0.999
ADAM_EPS = 1e-08
ADAM_WD = 0.01
ADAM_STEP = 10

LANES = 128
VMEM_LIMIT = 60 * 1024 * 1024
GDN_FWD_GROUP = 16
GDN_BWD_GROUP = 16
N_CHIPS = 4
N_DEV = 8

P_WIDTH = 3072
P_MLA = 2048
MESH = pl.DeviceIdType.MESH


def _rope_slot(j):
    return j if j < MLA_ROPE // 2 else 64 + (j - MLA_ROPE // 2)


def _w_in_cols():
    idx = -np.ones((P_WIDTH,), np.int64)
    for h in range(GDN_HEADS):
        base = h * 512
        idx[base:base + 128] = np.arange(128) + h * GDN_DK
        idx[base + 128:base + 256] = np.arange(128) + GDN_QK + h * GDN_DK
        idx[base + 256:base + 384] = np.arange(128) + 2 * GDN_QK + h * GDN_DV
        idx[base + 384:base + 512] = np.arange(128) + 2 * GDN_QK + GDN_VW + h * GDN_DV
    o_a = 2 * GDN_QK + 2 * GDN_VW
    idx[P_MLA:P_MLA + 2 * GDN_HEADS] = np.arange(2 * GDN_HEADS) + o_a
    o_cq = o_a + 2 * GDN_HEADS
    idx[P_MLA + 128:P_MLA + 512] = np.arange(MLA_Q_LORA) + o_cq
    o_ckv = o_cq + MLA_Q_LORA
    idx[P_MLA + 512:P_MLA + 768] = np.arange(MLA_KV_LORA) + o_ckv
    o_kr = o_ckv + MLA_KV_LORA
    for j in range(MLA_ROPE):
        idx[P_MLA + 768 + _rope_slot(j)] = o_kr + j
    return idx


def _w_q_cols():
    idx = -np.ones((MLA_HEADS * 256,), np.int64)
    for h in range(MLA_HEADS):
        o = h * (MLA_NOPE + MLA_ROPE)
        idx[h * 256:h * 256 + 128] = np.arange(128) + o
        for j in range(MLA_ROPE):
            idx[h * 256 + 128 + _rope_slot(j)] = o + MLA_NOPE + j
    return idx


def _pad_cols(w, idx):
    safe = np.where(idx >= 0, idx, 0)
    return jnp.where(jnp.asarray(idx >= 0)[None, :], w[:, safe], 0.0)


def _unpad_cols(wp, idx, n):
    inv = np.zeros((n,), np.int64)
    inv[idx[idx >= 0]] = np.nonzero(idx >= 0)[0]
    return wp[:, inv]


def _dot(a, b, ca, cb, precision=None):
    if precision is None:
        a = a.astype(BF16)
        b = b.astype(BF16)
    return lax.dot_general(a, b, (((ca,), (cb,)), ((), ())), preferred_element_type=F32, precision=precision)


@jax.custom_vjp
def mm(a, b):
    return _dot(a, b, 1, 0)


@jax.custom_vjp
def mm_nt(a, b):
    return _dot(a, b, 1, 1)


@jax.custom_vjp
def mm_tn(a, b):
    return _dot(a, b, 0, 0)


mm.defvjp(lambda a, b: (mm(a, b), (a, b)), lambda r, g: (mm_nt(g, r[1]), mm_tn(r[0], g)))
mm_nt.defvjp(lambda a, b: (mm_nt(a, b), (a, b)), lambda r, g: (mm(g, r[1]), mm_tn(g, r[0])))
mm_tn.defvjp(lambda a, b: (mm_tn(a, b), (a, b)), lambda r, g: (mm_nt(r[1], g), mm(r[0], g)))

def _split(a):
    hi = a.astype(BF16)
    return hi, (a - hi.astype(F32)).astype(BF16)


def _dot3(a, b, ca, cb):
    a_hi, a_lo = _split(a)
    b_hi, b_lo = _split(b)
    return (_dot(a_hi, b_hi, ca, cb) + _dot(a_hi, b_lo, ca, cb)) + _dot(a_lo, b_hi, ca, cb)


def _unit_lower_inverse(low):
    n = low.shape[0]
    ii = lax.broadcasted_iota(jnp.int32, (n, n), 0)
    jj = lax.broadcasted_iota(jnp.int32, (n, n), 1)
    inv = jnp.where(ii == jj, 1.0, 0.0) - low
    power = _dot3(low, low, 1, 0)
    k = 2
    while k < n:
        inv = inv + _dot3(inv, power, 1, 0)
        k *= 2
        if k < n:
            power = _dot3(power, power, 1, 0)
    return inv


@jax.custom_vjp
def unit_lower_solve(low, rhs):
    return _dot3(_unit_lower_inverse(low), rhs, 1, 0)


def _uls_fwd(low, rhs):
    inv = _unit_lower_inverse(low)
    x = _dot3(inv, rhs, 1, 0)
    return x, (inv, x)


def _uls_bwd(res, dx):
    inv, x = res
    drhs = _dot3(inv, dx, 0, 0)
    n = inv.shape[0]
    ii = lax.broadcasted_iota(jnp.int32, (n, n), 0)
    jj = lax.broadcasted_iota(jnp.int32, (n, n), 1)
    dlow = jnp.where(ii > jj, -_dot3(drhs, x, 1, 1), 0.0)
    return dlow, drhs


unit_lower_solve.defvjp(_uls_fwd, _uls_bwd)


def _shift_rows(x, s):
    if s == 0:
        return x
    n = x.shape[0]
    row = lax.broadcasted_iota(jnp.int32, x.shape, 0)
    rolled = pltpu.roll(x, s % n, 0)
    keep = (row >= s) if s > 0 else (row < n + s)
    return jnp.where(keep, rolled, 0.0)


def _row(w, j):
    tap = lax.broadcasted_iota(jnp.int32, w.shape, 0)
    return jnp.sum(jnp.where(tap == j, w, 0.0), axis=0, keepdims=True)


@jax.custom_vjp
def dwconv(x, w):
    k = w.shape[0]
    y = _row(w, k - 1) * x
    for j in range(k - 1):
        y = y + _row(w, j) * _shift_rows(x, k - 1 - j)
    return y


def _dwconv_fwd(x, w):
    return dwconv(x, w), (x, w)


def _dwconv_bwd(res, dy):
    x, w = res
    k = w.shape[0]
    dx = _row(w, k - 1) * dy
    tap = lax.broadcasted_iota(jnp.int32, w.shape, 0)
    dw = jnp.where(tap == k - 1, jnp.sum(dy * x, axis=0, keepdims=True), 0.0)
    for j in range(k - 1):
        dx = dx + _row(w, j) * _shift_rows(dy, -(k - 1 - j))
        dw = dw + jnp.where(tap == j, jnp.sum(dy * _shift_rows(x, k - 1 - j), axis=0, keepdims=True), 0.0)
    return dx, dw


dwconv.defvjp(_dwconv_fwd, _dwconv_bwd)


@jax.custom_vjp
def rope128(x, cos, sin):
    return x * cos + pltpu.roll(x, 64, 1) * sin


rope128.defvjp(lambda x, c, s: (rope128(x, c, s), (c, s)),
               lambda r, g: (g * r[0] + pltpu.roll(g * r[1], 64, 1), jnp.zeros_like(r[0]), jnp.zeros_like(r[1])))


def _silu(x):
    return x * jax.nn.sigmoid(x)


def _softplus(x):
    return jnp.maximum(x, 0.0) + jnp.log(1.0 + jnp.exp(-jnp.abs(x)))


def _rmsnorm(x, g):
    return x * lax.rsqrt(jnp.mean(x * x, axis=-1, keepdims=True) + NORM_EPS) * g


def _layernorm(x, g, b):
    mu = jnp.mean(x, axis=-1, keepdims=True)
    xc = x - mu
    var = jnp.mean(xc * xc, axis=-1, keepdims=True)
    return xc * lax.rsqrt(var + NORM_EPS) * g + b


def _pick_lane(row, lane):
    idx = lax.broadcasted_iota(jnp.int32, row.shape, 1)
    return jnp.sum(jnp.where(idx == lane, row, 0.0), axis=1, keepdims=True)


def _gdn_q(pq, cw):
    h = _silu(dwconv(pq, cw))
    return h * lax.rsqrt(jnp.sum(h * h, axis=-1, keepdims=True) + NORM_EPS) * (GDN_DK ** -0.5)


def _gdn_k(pk, cw):
    h = _silu(dwconv(pk, cw))
    return h * lax.rsqrt(jnp.sum(h * h, axis=-1, keepdims=True) + NORM_EPS)


def _gdn_v(pv, cw):
    return _silu(dwconv(pv, cw))


def _gdn_gate(ab, sc, head):
    a = _pick_lane(ab, head)
    b = _pick_lane(ab, GDN_HEADS + head)
    a_log = _pick_lane(_row(sc, 0), head)
    dt_bias = _pick_lane(_row(sc, 1), head)
    beta = jax.nn.sigmoid(b)
    g = -jnp.exp(a_log) * _softplus(a + dt_bias)
    return _two_lanes(g, beta)


def _two_lanes(c0, c1):
    lane = lax.broadcasted_iota(jnp.int32, (c0.shape[0], LANES), 1)
    return jnp.where(lane == 0, c0, jnp.where(lane == 1, c1, 0.0))


def _inverse_group(lows):
    n = lows[0].shape[0]
    ii = lax.broadcasted_iota(jnp.int32, (n, n), 0)
    jj = lax.broadcasted_iota(jnp.int32, (n, n), 1)
    eye = jnp.where(ii == jj, 1.0, 0.0)
    invs = [eye - low for low in lows]
    powers = [_dot3(low, low, 1, 0) for low in lows]
    k = 2
    while k < n:
        invs = [inv + _dot3(inv, p, 1, 0) for inv, p in zip(invs, powers)]
        k *= 2
        if k < n:
            powers = [_dot3(p, p, 1, 0) for p in powers]
    return invs


@jax.custom_vjp
def solve_group(lows, rhss):
    return [_dot3(inv, rhs, 1, 0) for inv, rhs in zip(_inverse_group(lows), rhss)]


def _solve_group_fwd(lows, rhss):
    invs = _inverse_group(lows)
    xs = [_dot3(inv, rhs, 1, 0) for inv, rhs in zip(invs, rhss)]
    return xs, (invs, xs)


def _solve_group_bwd(res, dxs):
    invs, xs = res
    n = invs[0].shape[0]
    strict = lax.broadcasted_iota(jnp.int32, (n, n), 0) > lax.broadcasted_iota(jnp.int32, (n, n), 1)
    drhss = [_dot3(inv, dx, 0, 0) for inv, dx in zip(invs, dxs)]
    dlows = [jnp.where(strict, -_dot3(drhs, x, 1, 1), 0.0) for drhs, x in zip(drhss, xs)]
    return dlows, drhss


solve_group.defvjp(_solve_group_fwd, _solve_group_bwd)


def _gdn_local_group(qs, ks, vs, gbs):
    c = qs[0].shape[0]
    ii = lax.broadcasted_iota(jnp.int32, (c, c), 0)
    jj = lax.broadcasted_iota(jnp.int32, (c, c), 1)
    incl = ii >= jj
    gs = [_pick_lane(gb, 0) for gb in gbs]
    betas = [_pick_lane(gb, 1) for gb in gbs]
    g_rows = [jnp.sum(jnp.where(ii == jj, g, 0.0), axis=0, keepdims=True) for g in gs]
    gc_cols = [jnp.sum(jnp.where(incl, g_row, 0.0), axis=1, keepdims=True) for g_row in g_rows]
    gc_rows = [jnp.sum(jnp.where(jj >= ii, g, 0.0), axis=0, keepdims=True) for g in gs]
    decays = [jnp.where(incl, jnp.exp(jnp.where(incl, gc - gr, 0.0)), 0.0) for gc, gr in zip(gc_cols, gc_rows)]
    kbs = [k * beta for k, beta in zip(ks, betas)]
    lows = [jnp.where(ii > jj, mm_nt(kb, k) * decay, 0.0) for kb, k, decay in zip(kbs, ks, decays)]
    egs = [jnp.exp(gc) for gc in gc_cols]
    wus = solve_group(lows, [jnp.concatenate([kb * eg, v * beta], axis=1) for kb, eg, v, beta in zip(kbs, egs, vs, betas)])
    qks = [mm_nt(q, k) * decay for q, k, decay in zip(qs, ks, decays)]
    g_lasts = [jnp.sum(g_row, axis=1, keepdims=True) for g_row in g_rows]
    kds = [k * jnp.exp(gl - gc) for k, gl, gc in zip(ks, g_lasts, gc_cols)]
    return [(wu[:, :GDN_DK], wu[:, GDN_DK:], qk, q * eg, kd, jnp.exp(gl))
            for wu, qk, q, eg, kd, gl in zip(wus, qks, qs, egs, kds, g_lasts)]


def _gdn_state_step(w, u, qk, qg, kd, eg_last, state):
    v_new = u - mm(w, state)
    o = mm(qg, state) + mm(qk, v_new)
    return o, state * eg_last + mm_tn(kd, v_new)


def _gdn_chunk_packed(q, k, v, gb, state):
    return _gdn_chunk(q, k, v, _pick_lane(gb, 0), _pick_lane(gb, 1), state)


def _gdn_post(o, z, norm_g):
    return _rmsnorm(o, norm_g) * _silu(z)


def _gdn_chunk(q, k, v, g, beta, state):
    c = q.shape[0]
    ii = lax.broadcasted_iota(jnp.int32, (c, c), 0)
    jj = lax.broadcasted_iota(jnp.int32, (c, c), 1)
    incl = ii >= jj
    g_row = jnp.sum(jnp.where(ii == jj, g, 0.0), axis=0, keepdims=True)
    gc_col = jnp.sum(jnp.where(incl, g_row, 0.0), axis=1, keepdims=True)
    gc_row = jnp.sum(jnp.where(jj >= ii, g, 0.0), axis=0, keepdims=True)
    decay = jnp.where(incl, jnp.exp(jnp.where(incl, gc_col - gc_row, 0.0)), 0.0)
    kb = k * beta
    low = jnp.where(ii > jj, mm_nt(kb, k) * decay, 0.0)
    eg = jnp.exp(gc_col)
    wu = unit_lower_solve(low, jnp.concatenate([kb * eg, v * beta], axis=1))
    w, u = wu[:, :GDN_DK], wu[:, GDN_DK:]
    qk = mm_nt(q, k) * decay
    g_last = jnp.sum(g_row, axis=1, keepdims=True)
    kd = k * jnp.exp(g_last - gc_col)
    v_new = u - mm(w, state)
    o = mm(q * eg, state) + mm(qk, v_new)
    new_state = state * jnp.exp(g_last) + mm_tn(kd, v_new)
    return o, new_state


def _attn_block(q, kn, kr, v, q0):
    s = (mm_nt(q[:, :128], kn) + mm_nt(q[:, 128:], kr)) * ATT_SCALE
    qpos = q0 + lax.broadcasted_iota(jnp.int32, s.shape, 0)
    kpos = lax.broadcasted_iota(jnp.int32, s.shape, 1)
    shift = int(math.log2(CHUNK))
    allowed = (kpos >> shift) <= (qpos >> shift)
    s = jnp.where(allowed, s, -1e30)
    p = jnp.exp(s - jnp.max(s, axis=-1, keepdims=True))
    p = p / jnp.sum(p, axis=-1, keepdims=True)
    return mm(p, v)


def _mla_prep(pm, qg, kvg, wq, wkv, cos, sin):
    cq = pm[:, 128:512]
    ckv = pm[:, 512:768]
    qf = mm(_rmsnorm(cq, qg), wq)
    parts = []
    for h in range(MLA_HEADS):
        parts.append(qf[:, h * 256:h * 256 + 128])
        parts.append(rope128(qf[:, h * 256 + 128:h * 256 + 256], cos, sin))
    kvf = mm(_rmsnorm(ckv, kvg), wkv)
    return jnp.concatenate(parts, axis=1), kvf, rope128(pm[:, 768:896], cos, sin)


def _ffn_act(ug, uu, wg, wu, bg, bu):
    return _silu(dwconv(ug, wg) + bg) * (dwconv(uu, wu) + bu)


def _head_loss(h1, ffn, gpre, pp, bgate, g2, b2, target):
    gate = jax.nn.sigmoid(gpre + bgate)
    h2 = _layernorm(ALPHA * h1 + ffn + gate * pp, g2, b2)
    err = h2 - target
    return 0.5 * jnp.sum(jnp.sum(err * err, axis=1, keepdims=True), axis=0, keepdims=True) / D_MODEL


def _params(sem):
    return pltpu.CompilerParams(dimension_semantics=sem, vmem_limit_bytes=VMEM_LIMIT)


def _matmul(a, b, *, name, ta=False, tb=False, tm=512, tn=512, tk=1024, add=None, add_scale=1.0,
            a_halves=False, b_halves=False):
    assert not (a_halves and ta) and not (b_halves and tb)
    a_shape = (a.shape[1], 2 * a.shape[2]) if a_halves else a.shape
    b_shape = (b.shape[1], 2 * b.shape[2]) if b_halves else b.shape
    (k_dim, m) = a_shape if ta else a_shape[::-1]
    (n, k2) = b_shape if tb else b_shape[::-1]
    assert k_dim == k2, (a.shape, b.shape)
    tm, tn, tk = min(tm, m), min(tn, n), min(tk, k_dim)
    assert m % tm == 0 and n % tn == 0 and k_dim % tk == 0, (name, m, n, k_dim, tm, tn, tk)
    nk = k_dim // tk
    ca, cb = (0 if ta else 1), (1 if tb else 0)

    def body(*refs):
        if add is None:
            a_ref, b_ref, o_ref, acc = refs
        else:
            a_ref, b_ref, c_ref, o_ref, acc = refs
        kk = pl.program_id(2)

        @pl.when(kk == 0)
        def _():
            acc[...] = jnp.zeros_like(acc)

        acc[...] += _dot(a_ref[...], b_ref[...], ca, cb)

        @pl.when(kk == nk - 1)
        def _():
            r = acc[...]
            if add is not None:
                r = r + add_scale * c_ref[...]
            o_ref[...] = r

    a_spec = pl.BlockSpec((tk, tm), lambda i, j, k: (k, i)) if ta else pl.BlockSpec((tm, tk), lambda i, j, k: (i, k))
    b_spec = pl.BlockSpec((tn, tk), lambda i, j, k: (j, k)) if tb else pl.BlockSpec((tk, tn), lambda i, j, k: (k, j))
    if a_halves:
        kh = k_dim // 2 // tk
        assert kh * tk * 2 == k_dim
        a_spec = pl.BlockSpec((None, tm, tk), lambda i, j, k: (k // kh, i, k % kh))
    if b_halves:
        nh = n // 2 // tn
        assert nh * tn * 2 == n
        b_spec = pl.BlockSpec((None, tk, tn), lambda i, j, k: (j // nh, k, j % nh))
    in_specs = [a_spec, b_spec]
    args = [a, b]
    if add is not None:
        in_specs.append(pl.BlockSpec((tm, tn), lambda i, j, k: (i, j)))
        args.append(add)
    return pl.pallas_call(
        body, name=name, grid=(m // tm, n // tn, nk),
        in_specs=in_specs, out_specs=pl.BlockSpec((tm, tn), lambda i, j, k: (i, j)),
        out_shape=jax.ShapeDtypeStruct((m, n), F32),
        scratch_shapes=[pltpu.VMEM((tm, tn), F32)],
        compiler_params=_params(("parallel", "parallel", "arbitrary")),
    )(*args)


def _riding(core, n_in, n_out, n_scratch, ride, inner, steps):
    if ride is None:
        return core
    copies, nr = ride[1], len(ride[0])

    def body(*refs):
        cuts = np.cumsum([0, n_in, nr, n_out, nr, n_scratch])
        ins, rin, outs, rout, scratch = (refs[a:b] for a, b in zip(cuts[:-1], cuts[1:]))
        sems = refs[cuts[-1]:]
        step = pl.program_id(0) * inner + pl.program_id(1)

        @pl.when(step == 0)
        def _():
            _exchange_start(copies(rin, rout, sems))

        @pl.when(step == steps - 2)
        def _():
            _exchange_pass_on(copies(rin, rout, sems))

        core(*ins, *outs, *scratch)

        @pl.when(step == steps - 1)
        def _():
            _exchange_finish(copies(rin, rout, sems))

    return body


def _ride_specs(ride):
    if ride is None:
        return [], [], [], [], []
    arrays, _, sems, shapes = ride
    return [HBM_REF] * len(arrays), [HBM_REF] * len(arrays), list(shapes), sems(len(arrays)), list(arrays)


def _gdn_fwd(proj, conv_w, sc, norm_g, bl, s, ride=None):
    nc = s // CHUNK

    def core(ph_ref, ab_ref, cwq_ref, cwk_ref, cwv_ref, sc_ref, ng_ref, cat_ref, o_ref, st_ref, q_s, k_s, v_s, gb_s):
        q_s[...] = _gdn_q(ph_ref[:, 0:128], cwq_ref[...])
        k_s[...] = _gdn_k(ph_ref[:, 128:256], cwk_ref[...])
        v_s[...] = _gdn_v(ph_ref[:, 256:384], cwv_ref[...])
        gb_s[...] = _gdn_gate(ab_ref[...], sc_ref[...], pl.program_id(1))

        group = math.gcd(nc, GDN_FWD_GROUP)

        def chunks(i, state):
            ns = [i * group + j for j in range(group)]
            rows = [pl.ds(pl.multiple_of(n * CHUNK, CHUNK), CHUNK) for n in ns]
            local = _gdn_local_group([q_s[r, :] for r in rows], [k_s[r, :] for r in rows], [v_s[r, :] for r in rows],
                                     [gb_s[r, :] for r in rows])
            for n, r, loc in zip(ns, rows, local):
                st_ref[n] = state
                o_ref[r, :], state = _gdn_state_step(*loc, state)
            return state

        lax.fori_loop(0, nc // group, chunks, jnp.zeros((GDN_DK, GDN_DV), F32))
        cat_ref[...] = _gdn_post(o_ref[...], ph_ref[:, 384:512], ng_ref[...]).astype(BF16)

    t = bl * s
    r_in, r_out, r_shapes, r_sems, r_args = _ride_specs(ride)
    outs = pl.pallas_call(
        _riding(core, 7, 3, 4, ride, GDN_HEADS, bl * GDN_HEADS), name="gdn_fwd", grid=(bl, GDN_HEADS),
        in_specs=[
            pl.BlockSpec((s, 512), lambda b, h: (b, h)),
            pl.BlockSpec((s, 128), lambda b, h: (b, P_MLA // 128)),
            pl.BlockSpec((GDN_CONV, 128), lambda b, h: (0, h)),
            pl.BlockSpec((GDN_CONV, 128), lambda b, h: (0, GDN_HEADS + h)),
            pl.BlockSpec((GDN_CONV, 128), lambda b, h: (0, 2 * GDN_HEADS + h)),
            pl.BlockSpec((8, 128), lambda b, h: (0, 0)),
            pl.BlockSpec((1, 128), lambda b, h: (0, 0)),
        ] + r_in,
        out_specs=[
            pl.BlockSpec((s, 128), lambda b, h: (b, h)),
            pl.BlockSpec((s, 128), lambda b, h: (b, h)),
            pl.BlockSpec((None, None, nc, GDN_DK, GDN_DV), lambda b, h: (b, h, 0, 0, 0)),
        ] + r_out,
        out_shape=[
            jax.ShapeDtypeStruct((t, 2 * GDN_VW), BF16),
            jax.ShapeDtypeStruct((t, GDN_VW), F32),
            jax.ShapeDtypeStruct((bl, GDN_HEADS, nc, GDN_DK, GDN_DV), F32),
        ] + r_shapes,
        scratch_shapes=[pltpu.VMEM((s, 128), F32)] * 4 + r_sems,
        compiler_params=_params(("arbitrary", "arbitrary")),
    )(proj, proj, conv_w, conv_w, conv_w, sc, norm_g, *r_args)
    return outs[0], outs[1], outs[2], list(outs[3:])


def _gdn_bwd(proj, conv_w, sc, norm_g, o_raw, states, dcat, bl, s, ride=None):
    nc = s // CHUNK

    def core(ph_ref, ab_ref, cwq_ref, cwk_ref, cwv_ref, sc_ref, ng_ref, o_ref, st_ref, dc_ref,
             dph_ref, dab_ref, dcwq_ref, dcwk_ref, dcwv_ref, dsc_ref, dng_ref, q_s, k_s, v_s, gb_s, do_s):
        head = pl.program_id(1)
        gate = functools.partial(_gdn_gate, head=head)
        paths = [(_gdn_q, 0, cwq_ref, q_s, dcwq_ref), (_gdn_k, 128, cwk_ref, k_s, dcwk_ref), (_gdn_v, 256, cwv_ref, v_s, dcwv_ref)]
        for fn, col, cw_ref, val_s, _ in paths:
            val_s[...] = fn(ph_ref[:, col:col + 128], cw_ref[...])
        gb_s[...] = gate(ab_ref[...], sc_ref[...])
        _, post_vjp = jax.vjp(_gdn_post, o_ref[...], ph_ref[:, 384:512], ng_ref[...])
        d_o, dz, dng = post_vjp(dc_ref[...])
        do_s[...] = d_o
        dph_ref[:, 384:512] = dz.astype(BF16)
        dng_ref[...] = jnp.broadcast_to(dng, dng_ref.shape)

        group = math.gcd(nc, GDN_BWD_GROUP)

        def chunks(i, dstate):
            ns = [nc - 1 - (i * group + j) for j in range(group)]
            rows = [pl.ds(pl.multiple_of(n * CHUNK, CHUNK), CHUNK) for n in ns]
            local, local_vjp = jax.vjp(_gdn_local_group, [q_s[r, :] for r in rows], [k_s[r, :] for r in rows],
                                       [v_s[r, :] for r in rows], [gb_s[r, :] for r in rows])
            d_os = [do_s[r, :] for r in rows]
            dlocal = []
            for n, loc, d_o in zip(ns, local, d_os):
                _, step_vjp = jax.vjp(_gdn_state_step, *loc, st_ref[n])
                *dloc, dstate = step_vjp((d_o, dstate))
                dlocal.append(tuple(dloc))
            dqs, dks, dvs, dgbs = local_vjp(dlocal)
            for r, dq, dk, dv, dgb in zip(rows, dqs, dks, dvs, dgbs):
                q_s[r, :], k_s[r, :], v_s[r, :], gb_s[r, :] = dq, dk, dv, dgb
            return dstate

        lax.fori_loop(0, nc // group, chunks, jnp.zeros((GDN_DK, GDN_DV), F32))
        for fn, col, cw_ref, val_s, dcw_ref in paths:
            _, vjp = jax.vjp(fn, ph_ref[:, col:col + 128], cw_ref[...])
            dph, dcw_ref[...] = vjp(val_s[...])
            dph_ref[:, col:col + 128] = dph.astype(BF16)
        _, gate_vjp = jax.vjp(gate, ab_ref[...], sc_ref[...])
        dab, dsc_ref[...] = gate_vjp(gb_s[...])

        @pl.when(head == 0)
        def _():
            dab_ref[...] = jnp.zeros_like(dab_ref)

        dab_ref[...] += dab

    t = bl * s
    cw_out = pl.BlockSpec((None, GDN_CONV, 128), lambda b, h: (b, 0, h))
    part = pl.BlockSpec((None, None, 8, 128), lambda b, h: (b, h, 0, 0))
    r_in, r_out, r_shapes, r_sems, r_args = _ride_specs(ride)
    outs = pl.pallas_call(
        _riding(core, 10, 7, 5, ride, GDN_HEADS, bl * GDN_HEADS), name="gdn_bwd", grid=(bl, GDN_HEADS),
        in_specs=[
            pl.BlockSpec((s, 512), lambda b, h: (b, h)),
            pl.BlockSpec((s, 128), lambda b, h: (b, P_MLA // 128)),
            pl.BlockSpec((GDN_CONV, 128), lambda b, h: (0, h)),
            pl.BlockSpec((GDN_CONV, 128), lambda b, h: (0, GDN_HEADS + h)),
            pl.BlockSpec((GDN_CONV, 128), lambda b, h: (0, 2 * GDN_HEADS + h)),
            pl.BlockSpec((8, 128), lambda b, h: (0, 0)),
            pl.BlockSpec((1, 128), lambda b, h: (0, 0)),
            pl.BlockSpec((s, 128), lambda b, h: (b, h)),
            pl.BlockSpec((None, None, nc, GDN_DK, GDN_DV), lambda b, h: (b, h, 0, 0, 0)),
            pl.BlockSpec((s, 128), lambda b, h: (b, h)),
        ] + r_in,
        out_specs=[
            pl.BlockSpec((s, 512), lambda b, h: (b, h)),
            pl.BlockSpec((s, 128), lambda b, h: (b, 0)),
            cw_out, cw_out, cw_out, part, part,
        ] + r_out,
        out_shape=[
            jax.ShapeDtypeStruct((t, P_WIDTH), BF16),
            jax.ShapeDtypeStruct((t, 128), F32),
            jax.ShapeDtypeStruct((bl, GDN_CONV, 512), F32),
            jax.ShapeDtypeStruct((bl, GDN_CONV, 512), F32),
            jax.ShapeDtypeStruct((bl, GDN_CONV, 512), F32),
            jax.ShapeDtypeStruct((bl, GDN_HEADS, 8, 128), F32),
            jax.ShapeDtypeStruct((bl, GDN_HEADS, 8, 128), F32),
        ] + r_shapes,
        scratch_shapes=[pltpu.VMEM((s, 128), F32)] * 5 + r_sems,
        compiler_params=_params(("arbitrary", "arbitrary")),
    )(proj, proj, conv_w, conv_w, conv_w, sc, norm_g, o_raw, states, dcat, *r_args)
    return tuple(outs[:7]) + (list(outs[7:]),)


def _mla_prep_fwd(proj, qg, kvg, wq, wkv, cos, sin, s, tm):
    t = proj.shape[0]
    tm = min(tm, s)
    nps = s // tm
    const = lambda shape: pl.BlockSpec(shape, lambda i: (0, 0))

    def body(pm_ref, qg_ref, kvg_ref, wq_ref, wkv_ref, cos_ref, sin_ref, qf_ref, kvf_ref, kr_ref):
        qf, kvf, kr = _mla_prep(pm_ref[...], qg_ref[...], kvg_ref[...], wq_ref[...], wkv_ref[...], cos_ref[...], sin_ref[...])
        qf_ref[...], kvf_ref[...], kr_ref[...] = qf.astype(BF16), kvf.astype(BF16), kr.astype(BF16)

    return pl.pallas_call(
        body, name="mla_prep_fwd", grid=(t // tm,),
        in_specs=[
            pl.BlockSpec((tm, 1024), lambda i: (i, P_MLA // 1024)),
            const((1, MLA_Q_LORA)), const((1, MLA_KV_LORA)), const(wq.shape), const(wkv.shape),
            pl.BlockSpec((tm, 128), lambda i: (i % nps, 0)), pl.BlockSpec((tm, 128), lambda i: (i % nps, 0)),
        ],
        out_specs=[pl.BlockSpec((tm, 1024), lambda i: (i, 0)), pl.BlockSpec((tm, 1024), lambda i: (i, 0)),
                   pl.BlockSpec((tm, 128), lambda i: (i, 0))],
        out_shape=[jax.ShapeDtypeStruct((t, 1024), BF16), jax.ShapeDtypeStruct((t, 1024), BF16),
                   jax.ShapeDtypeStruct((t, 128), BF16)],
        compiler_params=_params(("parallel",)),
    )(proj, qg, kvg, wq, wkv, cos, sin)


def _mla_prep_bwd(proj, qg, kvg, wq, wkv, cos, sin, dqf, dkvf, dkr, dab, dproj, s, tm):
    t = proj.shape[0]
    tm = min(tm, s)
    nps = s // tm
    const = lambda shape: pl.BlockSpec(shape, lambda i: (0, 0))

    def body(pm_ref, qg_ref, kvg_ref, wq_ref, wkv_ref, cos_ref, sin_ref, dqf_ref, dkvf_ref, dkr_ref, dab_ref, dp_in,
             dp_ref, dqg_ref, dkvg_ref, dwq_ref, dwkv_ref):
        del dp_in
        fn = lambda pm, qg_, kvg_, wq_, wkv_: _mla_prep(pm, qg_, kvg_, wq_, wkv_, cos_ref[...], sin_ref[...])
        _, vjp = jax.vjp(fn, pm_ref[...], qg_ref[...], kvg_ref[...], wq_ref[...].astype(F32), wkv_ref[...].astype(F32))
        dpm, dqg, dkvg, dwq, dwkv = vjp((dqf_ref[...], dkvf_ref[...], dkr_ref[...]))
        dp_ref[...] = jnp.concatenate([dab_ref[...], dpm[:, 128:]], axis=1).astype(BF16)

        @pl.when(pl.program_id(0) == 0)
        def _():
            dqg_ref[...] = jnp.zeros_like(dqg_ref)
            dkvg_ref[...] = jnp.zeros_like(dkvg_ref)
            dwq_ref[...] = jnp.zeros_like(dwq_ref)
            dwkv_ref[...] = jnp.zeros_like(dwkv_ref)

        dqg_ref[...] += dqg
        dkvg_ref[...] += dkvg
        dwq_ref[...] += dwq
        dwkv_ref[...] += dwkv

    rows = lambda w: pl.BlockSpec((tm, w), lambda i: (i, 0))
    return pl.pallas_call(
        body, name="mla_prep_bwd", grid=(t // tm,),
        in_specs=[
            pl.BlockSpec((tm, 1024), lambda i: (i, P_MLA // 1024)),
            const((1, MLA_Q_LORA)), const((1, MLA_KV_LORA)), const(wq.shape), const(wkv.shape),
            pl.BlockSpec((tm, 128), lambda i: (i % nps, 0)), pl.BlockSpec((tm, 128), lambda i: (i % nps, 0)),
            rows(1024), rows(1024), rows(128), rows(128),
            pl.BlockSpec(memory_space=pl.ANY),
        ],
        out_specs=[pl.BlockSpec((tm, 1024), lambda i: (i, P_MLA // 1024)),
                   const((1, MLA_Q_LORA)), const((1, MLA_KV_LORA)), const(wq.shape), const(wkv.shape)],
        out_shape=[jax.ShapeDtypeStruct(dproj.shape, dproj.dtype),
                   jax.ShapeDtypeStruct((1, MLA_Q_LORA), F32), jax.ShapeDtypeStruct((1, MLA_KV_LORA), F32),
                   jax.ShapeDtypeStruct(wq.shape, F32), jax.ShapeDtypeStruct(wkv.shape, F32)],
        input_output_aliases={11: 0},
        compiler_params=_params(("arbitrary",)),
    )(proj, qg, kvg, wq, wkv, cos, sin, dqf, dkvf, dkr, dab, dproj)


def _attn_fwd(qf, kvf, kr, cat, bl, s, tq):
    tq = min(tq, s)
    nq = s // tq

    def body(q_ref, kv_ref, kr_ref, cat_in, o_ref):
        del cat_in
        for i in range(nq):
            rows, keys = slice(i * tq, (i + 1) * tq), slice(0, (i + 1) * tq)
            o = _attn_block(q_ref[rows, :], kv_ref[keys, 0:128], kr_ref[keys, :], kv_ref[keys, 128:256], i * tq)
            o_ref[rows, :] = o.astype(o_ref.dtype)

    return pl.pallas_call(
        body, name="attn_fwd", grid=(bl, MLA_HEADS),
        in_specs=[
            pl.BlockSpec((s, 256), lambda b, h: (b, h)),
            pl.BlockSpec((s, 256), lambda b, h: (b, h)),
            pl.BlockSpec((s, 128), lambda b, h: (b, 0)),
            pl.BlockSpec(memory_space=pl.ANY),
        ],
        out_specs=pl.BlockSpec((s, 128), lambda b, h: (b, GDN_HEADS + h)),
        out_shape=jax.ShapeDtypeStruct(cat.shape, cat.dtype),
        input_output_aliases={3: 0},
        compiler_params=_params(("parallel", "parallel")),
    )(qf, kvf, kr, cat)


def _attn_bwd(qf, kvf, kr, dcat, bl, s, tq):
    tq = min(tq, s)
    nq = s // tq

    def body(q_ref, kv_ref, kr_ref, do_ref, dq_ref, dkv_ref, dkr_ref):
        dkv_ref[...] = jnp.zeros_like(dkv_ref)

        @pl.when(pl.program_id(1) == 0)
        def _():
            dkr_ref[...] = jnp.zeros_like(dkr_ref)

        for i in range(nq):
            rows, keys = slice(i * tq, (i + 1) * tq), slice(0, (i + 1) * tq)
            fn = functools.partial(_attn_block, q0=i * tq)
            f32 = lambda a: a.astype(F32)
            _, vjp = jax.vjp(fn, f32(q_ref[rows, :]), f32(kv_ref[keys, 0:128]), f32(kr_ref[keys, :]), f32(kv_ref[keys, 128:256]))
            dq_ref[rows, :], dkn, dkr, dv = vjp(do_ref[rows, :])
            dkv_ref[keys, 0:128] += dkn
            dkv_ref[keys, 128:256] += dv
            dkr_ref[keys, :] += dkr

    t = bl * s
    return pl.pallas_call(
        body, name="attn_bwd", grid=(bl, MLA_HEADS),
        in_specs=[
            pl.BlockSpec((s, 256), lambda b, h: (b, h)),
            pl.BlockSpec((s, 256), lambda b, h: (b, h)),
            pl.BlockSpec((s, 128), lambda b, h: (b, 0)),
            pl.BlockSpec((s, 128), lambda b, h: (b, GDN_HEADS + h)),
        ],
        out_specs=[
            pl.BlockSpec((s, 256), lambda b, h: (b, h)),
            pl.BlockSpec((s, 256), lambda b, h: (b, h)),
            pl.BlockSpec((s, 128), lambda b, h: (b, 0)),
        ],
        out_shape=[jax.ShapeDtypeStruct((t, 1024), F32), jax.ShapeDtypeStruct((t, 1024), F32),
                   jax.ShapeDtypeStruct((t, 128), F32)],
        compiler_params=_params(("parallel", "arbitrary")),
    )(qf, kvf, kr, dcat)


def _ln1_fwd(x, mix, g, b, tm):
    t = x.shape[0]
    tm = min(tm, t)

    def body(x_ref, mix_ref, g_ref, b_ref, r_ref, h_ref, hb_ref, xb_ref):
        r = ALPHA * x_ref[...] + mix_ref[...]
        r_ref[...] = r
        h = _layernorm(r, g_ref[...], b_ref[...])
        h_ref[...] = h
        hb_ref[...] = h.astype(BF16)
        xb_ref[...] = x_ref[...].astype(BF16)

    rows = pl.BlockSpec((tm, D_MODEL), lambda i: (i, 0))
    vec = pl.BlockSpec((1, D_MODEL), lambda i: (0, 0))
    return pl.pallas_call(
        body, name="ln1_fwd", grid=(t // tm,), in_specs=[rows, rows, vec, vec], out_specs=[rows] * 4,
        out_shape=[jax.ShapeDtypeStruct(x.shape, F32)] * 2 + [jax.ShapeDtypeStruct(x.shape, BF16)] * 2,
        compiler_params=_params(("parallel",)),
    )(x, mix, g, b)


def _ln1_bwd(r1, dr2, da, db_, g, b, tm):
    t = r1.shape[0]
    tm = min(tm, t)

    def body(r_ref, d2_ref, da_ref, db_ref, g_ref, b_ref, dr_ref, drb_ref, dg_ref, dbias_ref):
        dh = ALPHA * d2_ref[...] + da_ref[...] + db_ref[...]
        _, vjp = jax.vjp(_layernorm, r_ref[...], g_ref[...], b_ref[...])
        dr, dg, dbias = vjp(dh)
        dr_ref[...] = dr
        drb_ref[...] = dr.astype(BF16)

        @pl.when(pl.program_id(0) == 0)
        def _():
            dg_ref[...] = jnp.zeros_like(dg_ref)
            dbias_ref[...] = jnp.zeros_like(dbias_ref)

        dg_ref[...] += dg
        dbias_ref[...] += dbias

    rows = pl.BlockSpec((tm, D_MODEL), lambda i: (i, 0))
    vec = pl.BlockSpec((1, D_MODEL), lambda i: (0, 0))
    return pl.pallas_call(
        body, name="ln1_bwd", grid=(t // tm,), in_specs=[rows] * 4 + [vec, vec], out_specs=[rows, rows, vec, vec],
        out_shape=[jax.ShapeDtypeStruct(r1.shape, F32), jax.ShapeDtypeStruct(r1.shape, BF16)]
        + [jax.ShapeDtypeStruct((1, D_MODEL), F32)] * 2,
        compiler_params=_params(("arbitrary",)),
    )(r1, dr2, da, db_, g, b)


def _ffn_act_fwd(u, conv_w, conv_b, bl, s, cb):
    nj = D_FF // cb

    def body(ug_ref, uu_ref, wg_ref, wu_ref, bg_ref, bu_ref, act_ref):
        act_ref[...] = _ffn_act(ug_ref[...], uu_ref[...], wg_ref[...], wu_ref[...], bg_ref[...], bu_ref[...]).astype(BF16)

    return pl.pallas_call(
        body, name="ffn_act_fwd", grid=(bl, nj),
        in_specs=[
            pl.BlockSpec((s, cb), lambda b, j: (b, j)), pl.BlockSpec((s, cb), lambda b, j: (b, nj + j)),
            pl.BlockSpec((FFN_CONV, cb), lambda b, j: (0, j)), pl.BlockSpec((FFN_CONV, cb), lambda b, j: (0, nj + j)),
            pl.BlockSpec((1, cb), lambda b, j: (0, j)), pl.BlockSpec((1, cb), lambda b, j: (0, nj + j)),
        ],
        out_specs=pl.BlockSpec((s, cb), lambda b, j: (b, j)),
        out_shape=jax.ShapeDtypeStruct((bl * s, D_FF), BF16),
        compiler_params=_params(("parallel", "parallel")),
    )(u, u, conv_w, conv_w, conv_b, conv_b)


def _ffn_act_bwd(u, conv_w, conv_b, dact, bl, s, cb):
    nj = D_FF // cb

    def body(ug_ref, uu_ref, wg_ref, wu_ref, bg_ref, bu_ref, da_ref, du_ref, dwg_ref, dwu_ref, dbg_ref, dbu_ref):
        _, vjp = jax.vjp(_ffn_act, ug_ref[...], uu_ref[...], wg_ref[...], wu_ref[...], bg_ref[...], bu_ref[...])
        dug, duu, dwg_ref[...], dwu_ref[...], dbg_ref[...], dbu_ref[...] = vjp(da_ref[...])
        du_ref[0], du_ref[1] = dug.astype(BF16), duu.astype(BF16)

    t = bl * s
    blk = pl.BlockSpec((s, cb), lambda b, j: (b, j))
    wpart = pl.BlockSpec((None, FFN_CONV, cb), lambda b, j: (b, 0, j))
    bpart = pl.BlockSpec((None, 1, cb), lambda b, j: (b, 0, j))
    return pl.pallas_call(
        body, name="ffn_act_bwd", grid=(bl, nj),
        in_specs=[
            blk, pl.BlockSpec((s, cb), lambda b, j: (b, nj + j)),
            pl.BlockSpec((FFN_CONV, cb), lambda b, j: (0, j)), pl.BlockSpec((FFN_CONV, cb), lambda b, j: (0, nj + j)),
            pl.BlockSpec((1, cb), lambda b, j: (0, j)), pl.BlockSpec((1, cb), lambda b, j: (0, nj + j)),
            blk,
        ],
        out_specs=[pl.BlockSpec((2, s, cb), lambda b, j: (0, b, j)), wpart, wpart, bpart, bpart],
        out_shape=[jax.ShapeDtypeStruct((2, t, D_FF), BF16)] + [jax.ShapeDtypeStruct((bl, FFN_CONV, D_FF), F32)] * 2
        + [jax.ShapeDtypeStruct((bl, 1, D_FF), F32)] * 2,
        compiler_params=_params(("parallel", "parallel")),
    )(u, u, conv_w, conv_w, conv_b, conv_b, dact)


def _head(h1, ffn, gpre, pp, bgate, g2, b2, target, tm):
    t = h1.shape[0]
    tm = min(tm, t)

    def body(h1_ref, ffn_ref, gp_ref, pp_ref, bg_ref, g2_ref, b2_ref, tg_ref,
             dr_ref, drb_ref, dgp_ref, dpp_ref, loss_ref, dbg_ref, dg2_ref, db2_ref):
        fn = functools.partial(_head_loss, target=tg_ref[...])
        loss, vjp = jax.vjp(fn, h1_ref[...], ffn_ref[...], gp_ref[...], pp_ref[...], bg_ref[...], g2_ref[...], b2_ref[...])
        _, dffn, dgp, dpp, dbg, dg2, db2 = vjp(jnp.ones((1, 1), F32))
        dr_ref[...] = dffn
        drb_ref[...], dgp_ref[...], dpp_ref[...] = dffn.astype(BF16), dgp.astype(BF16), dpp.astype(BF16)

        @pl.when(pl.program_id(0) == 0)
        def _():
            loss_ref[...] = jnp.zeros_like(loss_ref)
            dbg_ref[...] = jnp.zeros_like(dbg_ref)
            dg2_ref[...] = jnp.zeros_like(dg2_ref)
            db2_ref[...] = jnp.zeros_like(db2_ref)

        loss_ref[...] += jnp.broadcast_to(loss, loss_ref.shape)
        dbg_ref[...] += dbg
        dg2_ref[...] += dg2
        db2_ref[...] += db2

    rows = pl.BlockSpec((tm, D_MODEL), lambda i: (i, 0))
    vec = pl.BlockSpec((1, D_MODEL), lambda i: (0, 0))
    return pl.pallas_call(
        body, name="head", grid=(t // tm,), in_specs=[rows] * 4 + [vec] * 3 + [rows],
        out_specs=[rows] * 4 + [pl.BlockSpec((8, 128), lambda i: (0, 0))] + [vec] * 3,
        out_shape=[jax.ShapeDtypeStruct(h1.shape, F32)] + [jax.ShapeDtypeStruct(h1.shape, BF16)] * 3
        + [jax.ShapeDtypeStruct((8, 128), F32)]
        + [jax.ShapeDtypeStruct((1, D_MODEL), F32)] * 3,
        compiler_params=_params(("arbitrary",)),
    )(h1, ffn, gpre, pp, bgate, g2, b2, target)


def _adam_update(g, w_ref, m_ref, v_ref, g_ref, d_ref, nm_ref, nv_ref):
    m2 = ADAM_B1 * m_ref[...] + (1.0 - ADAM_B1) * g
    v2 = ADAM_B2 * v_ref[...] + (1.0 - ADAM_B2) * jnp.square(g)
    m_hat = m2 / (1.0 - ADAM_B1 ** ADAM_STEP)
    v_hat = v2 / (1.0 - ADAM_B2 ** ADAM_STEP)
    g_ref[...] = g
    d_ref[...] = -ADAM_LR * (m_hat / (jnp.sqrt(v_hat) + ADAM_EPS) + ADAM_WD * w_ref[...])
    nm_ref[...] = m2
    nv_ref[...] = v2


def _row_tile(rows, cols, limit_bytes=256 * 1024):
    best = None
    for t in range(8, rows + 1, 8):
        if rows % t == 0 and t * cols * 4 <= limit_bytes:
            best = t
    return best or rows


def _adamw_reduced(own, recv, w, m, v, name):
    a, b = w.shape
    ta = _row_tile(a, b)

    def body(own_ref, recv_ref, w_ref, m_ref, v_ref, g_ref, d_ref, nm_ref, nv_ref):
        c = lax.axis_index("c")
        for core in range(2):
            @pl.when(c == core)
            def _():
                same = [own_ref[...], recv_ref[0], recv_ref[1], recv_ref[2]]
                other = [recv_ref[3], recv_ref[4], recv_ref[5], recv_ref[6]]
                core0, core1 = (same, other) if core == 0 else (other, same)
                g = core0[0] + core1[0]
                for r in range(1, N_CHIPS):
                    g = (g + core0[r]) + core1[r]
                _adam_update(g, w_ref, m_ref, v_ref, g_ref, d_ref, nm_ref, nv_ref)

    blk = pl.BlockSpec((ta, b), lambda i: (i, 0))
    return pl.pallas_call(
        body, name=name, grid=(a // ta,),
        in_specs=[blk, pl.BlockSpec((7, ta, b), lambda i: (0, i, 0)), blk, blk, blk], out_specs=[blk] * 4,
        out_shape=[jax.ShapeDtypeStruct(w.shape, F32)] * 4, compiler_params=_params(("parallel",)),
    )(own, recv, w, m, v)


def _adamw_small(g, w, m, v):
    def body(g_in, w_ref, m_ref, v_ref, g_ref, d_ref, nm_ref, nv_ref):
        _adam_update(g_in[...], w_ref, m_ref, v_ref, g_ref, d_ref, nm_ref, nv_ref)

    blk = pl.BlockSpec(w.shape, lambda i: (0, 0))
    return pl.pallas_call(
        body, name="adamw_small", grid=(1,), in_specs=[blk] * 4, out_specs=[blk] * 4,
        out_shape=[jax.ShapeDtypeStruct(w.shape, F32)] * 4, compiler_params=_params(("arbitrary",)),
    )(g, w, m, v)


def _remote(src, dst, send_sem, recv_sem, device):
    return pltpu.make_async_remote_copy(src_ref=src, dst_ref=dst, send_sem=send_sem, recv_sem=recv_sem,
                                        device_id=device, device_id_type=MESH)


def _place():
    x, y, c = lax.axis_index("x"), lax.axis_index("y"), lax.axis_index("c")
    return x, y, c, 2 * x + y, [(1 - x, y), (x, 1 - y), (1 - x, 1 - y)]


HBM_REF = pl.BlockSpec(memory_space=pl.ANY)
HALF_ROWS_QUANTUM = 16


def _gather_sems(n):
    return [pltpu.SemaphoreType.DMA((3 * n,))] * 4 + [pltpu.SemaphoreType.DMA((n,))]


def _gather_copies(ins, outs, sems):
    send_s, recv_s, fsend_s, frecv_s, local_s = sems
    x, y, c, me, chips = _place()
    local, sends, steps = [], [], []
    for i, (src, dst) in enumerate(zip(ins, outs)):
        local.append(pltpu.make_async_copy(src, dst.at[me], local_s.at[i]))
        half = src.shape[0] // 2
        split = src.shape[0] % (2 * HALF_ROWS_QUANTUM) == 0
        if split:
            mine = pl.ds(pl.multiple_of(c * half, HALF_ROWS_QUANTUM), half)
            theirs = pl.ds(pl.multiple_of((1 - c) * half, HALF_ROWS_QUANTUM), half)
        for r, (px, py) in enumerate(chips):
            k, peer = 3 * i + r, 2 * px + py
            if split:
                sends.append(_remote(src.at[mine], dst.at[me, mine], send_s.at[k], recv_s.at[k], (px, py, c)))
                landed = dst.at[peer, mine]
                steps.append((_remote(src.at[mine], landed, send_s.at[k], recv_s.at[k], (px, py, c)),
                              _remote(landed, landed, fsend_s.at[k], frecv_s.at[k], (x, y, 1 - c)),
                              _remote(dst.at[peer, theirs], dst.at[peer, theirs], fsend_s.at[k], frecv_s.at[k], (x, y, 1 - c))))
            else:
                sends.append(_remote(src, dst.at[me], send_s.at[k], recv_s.at[k], (px, py, c)))
                steps.append((_remote(src, dst.at[peer], send_s.at[k], recv_s.at[k], (px, py, c)), None, None))
    return local, sends, steps


def _scatter_sems(n):
    return [pltpu.SemaphoreType.DMA((4 * n,))] * 2 + [pltpu.SemaphoreType.DMA((3 * n,))] * 2


def _scatter_copies(ins, outs, sems):
    send_s, recv_s, fsend_s, frecv_s = sems
    x, y, c, me, chips = _place()
    sends, steps = [], []
    for i, (src, dst) in enumerate(zip(ins, outs)):
        for r, (px, py) in enumerate(chips):
            k = 4 * i + r
            cp = _remote(src.at[2 * px + py], dst.at[r], send_s.at[k], recv_s.at[k], (px, py, c))
            fwd = _remote(dst.at[r], dst.at[4 + r], fsend_s.at[3 * i + r], frecv_s.at[3 * i + r], (x, y, 1 - c))
            sends.append(cp)
            steps.append((cp, fwd, fwd))
        k = 4 * i + 3
        cp = _remote(src.at[me], dst.at[3], send_s.at[k], recv_s.at[k], (x, y, 1 - c))
        sends.append(cp)
        steps.append((cp, None, None))
    return [], sends, steps


def _exchange_start(plan):
    local, sends, _ = plan
    for cp in local + sends:
        cp.start()


def _exchange_pass_on(plan):
    for arrival, pass_on, _ in plan[2]:
        arrival.wait_recv()
        if pass_on is not None:
            pass_on.start()


def _exchange_finish(plan):
    local, sends, steps = plan
    for _, pass_on, passed in steps:
        if pass_on is not None:
            passed.wait_recv()
    for cp in sends:
        cp.wait_send()
    for _, pass_on, _ in steps:
        if pass_on is not None:
            pass_on.wait_send()
    for cp in local:
        cp.wait()


def _exchange_call(arrays, copies, sems, out_shapes, name):
    n = len(arrays)

    def body(*refs):
        plan = copies(refs[:n], refs[n:2 * n], refs[2 * n:])
        _exchange_start(plan)
        _exchange_pass_on(plan)
        _exchange_finish(plan)

    return pl.pallas_call(
        body, name=name, in_specs=[HBM_REF] * n, out_specs=[HBM_REF] * n, out_shape=out_shapes,
        scratch_shapes=sems(n), compiler_params=pltpu.CompilerParams(has_side_effects=True),
    )(*arrays)


def _gather_call(shards, name):
    shapes = [jax.ShapeDtypeStruct((N_CHIPS,) + a.shape, a.dtype) for a in shards]
    return _exchange_call(shards, _gather_copies, _gather_sems, shapes, name)


def _scatter_call(slabs, name):
    shapes = [jax.ShapeDtypeStruct((N_DEV - 1,) + a.shape[1:], a.dtype) for a in slabs]
    return _exchange_call(slabs, _scatter_copies, _scatter_sems, shapes, name)


def _all_reduce_small(a):
    def body(in_ref, out_ref, slots, send_sems, recv_sems):
        x, y, c = lax.axis_index("x"), lax.axis_index("y"), lax.axis_index("c")
        me = 4 * x + 2 * y + c
        slots[0] = in_ref[...]
        sends = []
        for r in range(1, N_DEV):
            peer = (x ^ (r >> 2), y ^ ((r >> 1) & 1), c ^ (r & 1))
            sends.append(pltpu.make_async_remote_copy(src_ref=in_ref, dst_ref=slots.at[r], send_sem=send_sems.at[r],
                                                      recv_sem=recv_sems.at[r], device_id=peer, device_id_type=MESH))
        for cp in sends:
            cp.start()
        for cp in sends:
            cp.wait_recv()
        acc = slots[me]
        for dev in range(1, N_DEV):
            acc = acc + slots[dev ^ me]
        out_ref[...] = acc
        for cp in sends:
            cp.wait_send()

    return pl.pallas_call(
        body, name="small_all_reduce",
        in_specs=[pl.BlockSpec(memory_space=pltpu.VMEM)], out_specs=pl.BlockSpec(memory_space=pltpu.VMEM),
        out_shape=jax.ShapeDtypeStruct(a.shape, a.dtype),
        scratch_shapes=[pltpu.VMEM((N_DEV,) + a.shape, a.dtype), pltpu.SemaphoreType.DMA((N_DEV,)),
                        pltpu.SemaphoreType.DMA((N_DEV,))],
        compiler_params=pltpu.CompilerParams(has_side_effects=True),
    )(a)


SHARDED = ["w_in", "mla_w_q_up", "mla_w_kv_up", "w_out", "ffn_w_up", "ffn_w_down", "ple_w_gate", "ple_w_proj",
           "gdn_conv_w", "ffn_conv_w"]
SHARD_AXIS = {"w_in": 1, "mla_w_q_up": 1, "mla_w_kv_up": 1, "w_out": 0, "ffn_w_up": 1, "ffn_w_down": 0,
              "ple_w_gate": 0, "ple_w_proj": 1, "gdn_conv_w": 1, "ffn_conv_w": 1}
SMALL = ["gdn_a_log", "gdn_dt_bias", "gdn_norm_g", "mla_q_norm_g", "mla_kv_norm_g", "ln1_g", "ln1_b", "ffn_conv_b",
         "ple_b_gate", "ln2_g", "ln2_b"]
WEIGHTS = ["w_in", "gdn_conv_w", "gdn_a_log", "gdn_dt_bias", "gdn_norm_g", "mla_q_norm_g", "mla_w_q_up", "mla_kv_norm_g",
           "mla_w_kv_up", "w_out", "ln1_g", "ln1_b", "ffn_w_up", "ffn_conv_w", "ffn_conv_b", "ffn_w_down", "ple_w_gate",
           "ple_b_gate", "ple_w_proj", "ln2_g", "ln2_b"]
F32_ON_WIRE = ("gdn_conv_w", "ffn_conv_w")
GATHER_EARLY = ["w_in", "gdn_conv_w", "mla_w_q_up", "mla_w_kv_up"]
GATHER_LATE = ["w_out", "ffn_w_up", "ffn_conv_w", "ffn_w_down", "ple_w_gate", "ple_w_proj"]
SCATTER_EARLY = ["ffn_w_up", "ffn_conv_w", "ffn_w_down", "ple_w_gate", "ple_w_proj", "w_out"]
SCATTER_LATE = ["w_in", "gdn_conv_w", "mla_w_q_up", "mla_w_kv_up"]
PACK_COLS = 1024
PACK_ROW_TILE = 8


def _join_blocks(blocks, axis):
    n, a, b = blocks.shape
    if axis == 0:
        return blocks.reshape(n * a, b)
    return jnp.transpose(blocks, (1, 0, 2)).reshape(a, n * b)


def _split_blocks(full, axis):
    if axis == 0:
        return full.reshape(N_CHIPS, full.shape[0] // N_CHIPS, full.shape[1])
    a, nb = full.shape
    return jnp.transpose(full.reshape(a, N_CHIPS, nb // N_CHIPS), (1, 0, 2))


def _pack(arrays):
    flat = jnp.concatenate([a.reshape(-1) for a in arrays])
    quantum = PACK_COLS * PACK_ROW_TILE
    padded = -(-flat.shape[0] // quantum) * quantum
    return jnp.pad(flat, (0, padded - flat.shape[0])).reshape(-1, PACK_COLS)


def _unpack(packed, shapes):
    flat = packed.reshape(-1)
    out, off = [], 0
    for shp in shapes:
        n = int(np.prod(shp))
        out.append(flat[off:off + n].reshape(shp))
        off += n
    return out


def kernel(x, p, w_in, gdn_conv_w, gdn_a_log, gdn_dt_bias, gdn_norm_g, mla_q_norm_g, mla_w_q_up, mla_kv_norm_g, mla_w_kv_up, w_out, ln1_g, ln1_b, ffn_w_up, ffn_conv_w, ffn_conv_b, ffn_w_down, ple_w_gate, ple_b_gate, ple_w_proj, ln2_g, ln2_b, loss_target, m_w_in, m_gdn_conv_w, m_gdn_a_log, m_gdn_dt_bias, m_gdn_norm_g, m_mla_q_norm_g, m_mla_w_q_up, m_mla_kv_norm_g, m_mla_w_kv_up, m_w_out, m_ln1_g, m_ln1_b, m_ffn_w_up, m_ffn_conv_w, m_ffn_conv_b, m_ffn_w_down, m_ple_w_gate, m_ple_b_gate, m_ple_w_proj, m_ln2_g, m_ln2_b, v_w_in, v_gdn_conv_w, v_gdn_a_log, v_gdn_dt_bias, v_gdn_norm_g, v_mla_q_norm_g, v_mla_w_q_up, v_mla_kv_norm_g, v_mla_w_kv_up, v_w_out, v_ln1_g, v_ln1_b, v_ffn_w_up, v_ffn_conv_w, v_ffn_conv_b, v_ffn_w_down, v_ple_w_gate, v_ple_b_gate, v_ple_w_proj, v_ln2_g, v_ln2_b):
    given = dict(locals())
    wsh = {n: given[n][0] for n in WEIGHTS}
    msh = {n: given["m_" + n][0] for n in WEIGHTS}
    vsh = {n: given["v_" + n][0] for n in WEIGHTS}
    bl, s, _ = x.shape
    t = bl * s
    xt = x.reshape(t, D_MODEL)
    pt = p.reshape(t, PLE_DIM)
    target = loss_target.reshape(t, D_MODEL)

    wire = lambda n: wsh[n] if n in F32_ON_WIRE else wsh[n].astype(BF16)
    early = _gather_call([wire(n) for n in GATHER_EARLY], "weights_gather_early")
    full = {n: _join_blocks(g, SHARD_AXIS[n]) for n, g in zip(GATHER_EARLY, early)}
    late_shards = [wire(n) for n in GATHER_LATE]
    late_ride = (late_shards, _gather_copies, _gather_sems,
                 [jax.ShapeDtypeStruct((N_CHIPS,) + a.shape, a.dtype) for a in late_shards])

    in_cols, q_cols = _w_in_cols(), _w_q_cols()
    w_in_p = _pad_cols(full["w_in"], in_cols)
    w_q_p = _pad_cols(full["mla_w_q_up"], q_cols)
    w_kv, gconv = full["mla_w_kv_up"], full["gdn_conv_w"]
    row = lambda a: a.reshape(1, -1)
    sc = jnp.zeros((8, 128), F32).at[0, :GDN_HEADS].set(wsh["gdn_a_log"]).at[1, :GDN_HEADS].set(wsh["gdn_dt_bias"])
    norm_g, qg, kvg = row(wsh["gdn_norm_g"]), row(wsh["mla_q_norm_g"]), row(wsh["mla_kv_norm_g"])
    g1, b1, g2, b2 = row(wsh["ln1_g"]), row(wsh["ln1_b"]), row(wsh["ln2_g"]), row(wsh["ln2_b"])
    fbias, bgate = row(wsh["ffn_conv_b"]), row(wsh["ple_b_gate"])

    inv = ROPE_THETA ** (-jnp.arange(0, MLA_ROPE, 2, dtype=F32) / MLA_ROPE)
    ang = jnp.arange(s, dtype=F32)[:, None] * inv[None, :]
    zero = jnp.zeros_like(ang)
    cos_t = jnp.concatenate([jnp.cos(ang), zero, jnp.cos(ang), zero], axis=1)
    sin_t = jnp.concatenate([-jnp.sin(ang), zero, jnp.sin(ang), zero], axis=1)

    proj = _matmul(xt, w_in_p, name="proj", tm=1024)
    cat, o_raw, states, late = _gdn_fwd(proj, gconv, sc, norm_g, bl, s, late_ride)
    full.update({n: _join_blocks(g, SHARD_AXIS[n]) for n, g in zip(GATHER_LATE, late)})
    w_o, w_up, w_down = full["w_out"], full["ffn_w_up"], full["ffn_w_down"]
    w_gate, w_proj, fconv = full["ple_w_gate"], full["ple_w_proj"], full["ffn_conv_w"]
    qf, kvf, kr = _mla_prep_fwd(proj, qg, kvg, w_q_p, w_kv, cos_t, sin_t, s, 256)
    cat = _attn_fwd(qf, kvf, kr, cat, bl, s, 256)
    wide = dict(tm=1024, tn=1024)
    mix = _matmul(cat, w_o, name="mix", **wide)
    r1, h1, h1b, xb = _ln1_fwd(xt, mix, g1, b1, 256)
    u = _matmul(h1b, w_up, name="ffn_up", tm=1024)
    act = _ffn_act_fwd(u, fconv, fbias, bl, s, 256)
    ffn = _matmul(act, w_down, name="ffn_down", tk=1408, **wide)
    gpre = _matmul(h1b, w_gate, name="ple_gate", **wide)
    pp = _matmul(pt, w_proj, name="ple_proj", **wide)
    dr2, dr2b, dgpre, dpp, loss_acc, dbgate, dg2, db2 = _head(h1, ffn, gpre, pp, bgate, g2, b2, target, 256)

    dact = _matmul(dr2b, w_down, name="d_act", tb=True, tm=1024, tn=1408)
    d_w_down = _matmul(act, dr2b, name="dw_down", ta=True, tm=1408, tn=1024, tk=512)
    du, dfcw_g, dfcw_u, dfcb_g, dfcb_u = _ffn_act_bwd(u, fconv, fbias, dact, bl, s, 256)
    dh1_a = _matmul(du, w_up, name="dh1_ffn", tb=True, tk=1408, a_halves=True, **wide)
    dh1_b = _matmul(dgpre, w_gate, name="dh1_ple", tb=True, **wide)
    d_w_up = _matmul(h1b, du, name="dw_up", ta=True, tn=1408, b_halves=True)
    d_w_gate = _matmul(h1b, dgpre, name="dw_gate", ta=True, tk=512, **wide)
    d_w_proj = _matmul(pt, dpp, name="dw_proj", ta=True, tn=1024)
    dr1, dr1b, dg1, db1 = _ln1_bwd(r1, dr2, dh1_a, dh1_b, g1, b1, 256)
    dcat = _matmul(dr1b, w_o, name="d_cat", tb=True, **wide)
    d_w_o = _matmul(cat, dr1b, name="dw_out", ta=True, tk=512, **wide)

    gfull = {
        "ffn_w_up": d_w_up, "ffn_w_down": d_w_down, "ple_w_gate": d_w_gate, "ple_w_proj": d_w_proj, "w_out": d_w_o,
        "ffn_conv_w": jnp.concatenate([jnp.sum(dfcw_g, 0), jnp.sum(dfcw_u, 0)], axis=1),
    }
    slabs = {n: _split_blocks(gfull[n], SHARD_AXIS[n]) for n in SCATTER_EARLY}
    early_slabs = [slabs[n] for n in SCATTER_EARLY]
    early_ride = (early_slabs, _scatter_copies, _scatter_sems,
                  [jax.ShapeDtypeStruct((N_DEV - 1,) + a.shape[1:], a.dtype) for a in early_slabs])
    dproj, dab, dcwq, dcwk, dcwv, dsc, dng, early_recv = _gdn_bwd(proj, gconv, sc, norm_g, o_raw, states, dcat, bl, s, early_ride)
    received = dict(zip(SCATTER_EARLY, early_recv))
    dqf, dkvf, dkr = _attn_bwd(qf, kvf, kr, dcat, bl, s, 256)
    dproj, dqg, dkvg, d_w_q_p, d_w_kv = _mla_prep_bwd(proj, qg, kvg, w_q_p, w_kv, cos_t, sin_t, dqf, dkvf, dkr, dab, dproj, s, 256)
    grad_x = _matmul(dproj, w_in_p, name="d_x", tb=True, add=dr1, add_scale=ALPHA, **wide)
    d_w_in_p = _matmul(xb, dproj, name="dw_in", ta=True, tk=512, **wide)

    gfull.update({
        "w_in": _unpad_cols(d_w_in_p, in_cols, D_IN),
        "mla_w_q_up": _unpad_cols(d_w_q_p, q_cols, MLA_HEADS * (MLA_NOPE + MLA_ROPE)),
        "mla_w_kv_up": d_w_kv,
        "gdn_conv_w": jnp.concatenate([jnp.sum(dcwq, 0), jnp.sum(dcwk, 0), jnp.sum(dcwv, 0)], axis=1),
    })
    dsc_sum = jnp.sum(dsc, axis=(0, 1))
    gsmall = {
        "gdn_a_log": dsc_sum[0, :GDN_HEADS], "gdn_dt_bias": dsc_sum[1, :GDN_HEADS],
        "gdn_norm_g": jnp.sum(dng[:, :, 0, :], axis=(0, 1)),
        "mla_q_norm_g": dqg[0], "mla_kv_norm_g": dkvg[0], "ln1_g": dg1[0], "ln1_b": db1[0],
        "ffn_conv_b": jnp.concatenate([jnp.sum(dfcb_g, 0), jnp.sum(dfcb_u, 0)], axis=1)[0],
        "ple_b_gate": dbgate[0], "ln2_g": dg2[0], "ln2_b": db2[0],
    }

    slabs.update({n: _split_blocks(gfull[n], SHARD_AXIS[n]) for n in SCATTER_LATE})
    me_chip = 2 * lax.axis_index("x") + lax.axis_index("y")
    received.update(zip(SCATTER_LATE, _scatter_call([slabs[n] for n in SCATTER_LATE], "grads_scatter_late")))
    big = [{}, {}, {}, {}]
    for n in SHARDED:
        own = lax.dynamic_index_in_dim(slabs[n], me_chip, 0, keepdims=False)
        for kind, val in enumerate(_adamw_reduced(own, received[n], wsh[n], msh[n], vsh[n], "adamw_" + n)):
            big[kind][n] = val

    small_shapes = [wsh[n].shape for n in SMALL]
    gsum = _all_reduce_small(_pack([gsmall[n] for n in SMALL]))
    spacks = _adamw_small(gsum, _pack([wsh[n] for n in SMALL]), _pack([msh[n] for n in SMALL]), _pack([vsh[n] for n in SMALL]))
    small = [dict(zip(SMALL, _unpack(pk, small_shapes))) for pk in spacks]

    loss = lax.psum(loss_acc[0, 0], ("x", "y", "c"))
    outs = [loss, grad_x.reshape(x.shape)]
    for kind in range(4):
        for n in WEIGHTS:
            val = big[kind][n] if n in big[kind] else small[kind][n]
            outs.append(val[None])
    return tuple(outs)
```

```python
import functools
import math

import numpy as np
import jax
import jax.numpy as jnp
from jax import lax
from jax.experimental import pallas as pl
from jax.experimental.pallas import tpu as pltpu

F32 = jnp.float32
BF16 = jnp.bfloat16

D_MODEL = 1024
CHUNK = 64
PLE_DIM = 256
GDN_HEADS = 4
GDN_DK = 128
GDN_DV = 128
GDN_CONV = 4
MLA_HEADS = 4
MLA_NOPE = 128
MLA_ROPE = 64
MLA_V = 128
MLA_Q_LORA = 384
MLA_KV_LORA = 256
ROPE_THETA = 10000.0
D_FF = 2816
FFN_CONV = 3
DEPTH = 1
ALPHA = (2.0 * DEPTH) ** 0.25
NORM_EPS = 1e-6
GDN_QK = GDN_HEADS * GDN_DK
GDN_VW = GDN_HEADS * GDN_DV
D_IN = 2 * GDN_QK + 2 * GDN_VW + 2 * GDN_HEADS + MLA_Q_LORA + MLA_KV_LORA + MLA_ROPE
ATT_SCALE = (MLA_NOPE + MLA_ROPE) ** -0.5

ADAM_LR = 0.001
ADAM_B1 = 0.9
ADAM_B2 = 0.999
ADAM_EPS = 1e-08
ADAM_WD = 0.01
ADAM_STEP = 10

LANES = 128
VMEM_LIMIT = 60 * 1024 * 1024
GDN_FWD_GROUP = 16
GDN_BWD_GROUP = 16
N_CHIPS = 4
N_DEV = 8

P_WIDTH = 3072
P_MLA = 2048
MESH = pl.DeviceIdType.MESH


def _rope_slot(j):
    return j if j < MLA_ROPE // 2 else 64 + (j - MLA_ROPE // 2)


def _w_in_cols():
    idx = -np.ones((P_WIDTH,), np.int64)
    for h in range(GDN_HEADS):
        base = h * 512
        idx[base:base + 128] = np.arange(128) + h * GDN_DK
        idx[base + 128:base + 256] = np.arange(128) + GDN_QK + h * GDN_DK
        idx[base + 256:base + 384] = np.arange(128) + 2 * GDN_QK + h * GDN_DV
        idx[base + 384:base + 512] = np.arange(128) + 2 * GDN_QK + GDN_VW + h * GDN_DV
    o_a = 2 * GDN_QK + 2 * GDN_VW
    idx[P_MLA:P_MLA + 2 * GDN_HEADS] = np.arange(2 * GDN_HEADS) + o_a
    o_cq = o_a + 2 * GDN_HEADS
    idx[P_MLA + 128:P_MLA + 512] = np.arange(MLA_Q_LORA) + o_cq
    o_ckv = o_cq + MLA_Q_LORA
    idx[P_MLA + 512:P_MLA + 768] = np.arange(MLA_KV_LORA) + o_ckv
    o_kr = o_ckv + MLA_KV_LORA
    for j in range(MLA_ROPE):
        idx[P_MLA + 768 + _rope_slot(j)] = o_kr + j
    return idx


def _w_q_cols():
    idx = -np.ones((MLA_HEADS * 256,), np.int64)
    for h in range(MLA_HEADS):
        o = h * (MLA_NOPE + MLA_ROPE)
        idx[h * 256:h * 256 + 128] = np.arange(128) + o
        for j in range(MLA_ROPE):
            idx[h * 256 + 128 + _rope_slot(j)] = o + MLA_NOPE + j
    return idx


def _pad_cols(w, idx):
    safe = np.where(idx >= 0, idx, 0)
    return jnp.where(jnp.asarray(idx >= 0)[None, :], w[:, safe], 0.0)


def _unpad_cols(wp, idx, n):
    inv = np.zeros((n,), np.int64)
    inv[idx[idx >= 0]] = np.nonzero(idx >= 0)[0]
    return wp[:, inv]


def _dot(a, b, ca, cb, precision=None):
    if precision is None:
        a = a.astype(BF16)
        b = b.astype(BF16)
    return lax.dot_general(a, b, (((ca,), (cb,)), ((), ())), preferred_element_type=F32, precision=precision)


@jax.custom_vjp
def mm(a, b):
    return _dot(a, b, 1, 0)


@jax.custom_vjp
def mm_nt(a, b):
    return _dot(a, b, 1, 1)


@jax.custom_vjp
def mm_tn(a, b):
    return _dot(a, b, 0, 0)


mm.defvjp(lambda a, b: (mm(a, b), (a, b)), lambda r, g: (mm_nt(g, r[1]), mm_tn(r[0], g)))
mm_nt.defvjp(lambda a, b: (mm_nt(a, b), (a, b)), lambda r, g: (mm(g, r[1]), mm_tn(g, r[0])))
mm_tn.defvjp(lambda a, b: (mm_tn(a, b), (a, b)), lambda r, g: (mm_nt(r[1], g), mm(r[0], g)))

def _split(a):
    hi = a.astype(BF16)
    return hi, (a - hi.astype(F32)).astype(BF16)


def _dot3(a, b, ca, cb):
    a_hi, a_lo = _split(a)
    b_hi, b_lo = _split(b)
    return (_dot(a_hi, b_hi, ca, cb) + _dot(a_hi, b_lo, ca, cb)) + _dot(a_lo, b_hi, ca, cb)


def _unit_lower_inverse(low):
    n = low.shape[0]
    ii = lax.broadcasted_iota(jnp.int32, (n, n), 0)
    jj = lax.broadcasted_iota(jnp.int32, (n, n), 1)
    inv = jnp.where(ii == jj, 1.0, 0.0) - low
    power = _dot3(low, low, 1, 0)
    k = 2
    while k < n:
        inv = inv + _dot3(inv, power, 1, 0)
        k *= 2
        if k < n:
            power = _dot3(power, power, 1, 0)
    return inv


@jax.custom_vjp
def unit_lower_solve(low, rhs):
    return _dot3(_unit_lower_inverse(low), rhs, 1, 0)


def _uls_fwd(low, rhs):
    inv = _unit_lower_inverse(low)
    x = _dot3(inv, rhs, 1, 0)
    return x, (inv, x)


def _uls_bwd(res, dx):
    inv, x = res
    drhs = _dot3(inv, dx, 0, 0)
    n = inv.shape[0]
    ii = lax.broadcasted_iota(jnp.int32, (n, n), 0)
    jj = lax.broadcasted_iota(jnp.int32, (n, n), 1)
    dlow = jnp.where(ii > jj, -_dot3(drhs, x, 1, 1), 0.0)
    return dlow, drhs


unit_lower_solve.defvjp(_uls_fwd, _uls_bwd)


def _shift_rows(x, s):
    return x if s == 0 else pltpu.roll(x, s % x.shape[0], 0)


def _row(w, j):
    tap = lax.broadcasted_iota(jnp.int32, w.shape, 0)
    return jnp.sum(jnp.where(tap == j, w, 0.0), axis=0, keepdims=True)


@jax.custom_vjp
def dwconv(x, w):
    k = w.shape[0]
    y = _row(w, k - 1) * x
    for j in range(k - 1):
        y = y + _row(w, j) * _shift_rows(x, k - 1 - j)
    return y


def _dwconv_fwd(x, w):
    return dwconv(x, w), (x, w)


def _dwconv_bwd(res, dy):
    x, w = res
    k = w.shape[0]
    dx = _row(w, k - 1) * dy
    tap = lax.broadcasted_iota(jnp.int32, w.shape, 0)
    dw = jnp.where(tap == k - 1, jnp.sum(dy * x, axis=0, keepdims=True), 0.0)
    for j in range(k - 1):
        dx = dx + _row(w, j) * _shift_rows(dy, -(k - 1 - j))
        dw = dw + jnp.where(tap == j, jnp.sum(dy * _shift_rows(x, k - 1 - j), axis=0, keepdims=True), 0.0)
    return dx, dw


dwconv.defvjp(_dwconv_fwd, _dwconv_bwd)


@jax.custom_vjp
def rope128(x, cos, sin):
    return x * cos + pltpu.roll(x, 64, 1) * sin


rope128.defvjp(lambda x, c, s: (rope128(x, c, s), (c, s)),
               lambda r, g: (g * r[0] + pltpu.roll(g * r[1], 64, 1), jnp.zeros_like(r[0]), jnp.zeros_like(r[1])))


def _silu(x):
    return x * jax.nn.sigmoid(x)


def _softplus(x):
    return jnp.maximum(x, 0.0) + jnp.log(1.0 + jnp.exp(-jnp.abs(x)))


def _rmsnorm(x, g):
    return x * lax.rsqrt(jnp.mean(x * x, axis=-1, keepdims=True) + NORM_EPS) * g


def _layernorm(x, g, b):
    mu = jnp.mean(x, axis=-1, keepdims=True)
    xc = x - mu
    var = jnp.mean(xc * xc, axis=-1, keepdims=True)
    return xc * lax.rsqrt(var + NORM_EPS) * g + b


def _pick_lane(row, lane):
    idx = lax.broadcasted_iota(jnp.int32, row.shape, 1)
    return jnp.sum(jnp.where(idx == lane, row, 0.0), axis=1, keepdims=True)


def _gdn_q(pq, cw):
    h = _silu(dwconv(pq, cw))
    return h * lax.rsqrt(jnp.sum(h * h, axis=-1, keepdims=True) + NORM_EPS) * (GDN_DK ** -0.5)


def _gdn_k(pk, cw):
    h = _silu(dwconv(pk, cw))
    return h * lax.rsqrt(jnp.sum(h * h, axis=-1, keepdims=True) + NORM_EPS)


def _gdn_v(pv, cw):
    return _silu(dwconv(pv, cw))


def _gdn_gate(ab, sc, head):
    a = _pick_lane(ab, head)
    b = _pick_lane(ab, GDN_HEADS + head)
    a_log = _pick_lane(_row(sc, 0), head)
    dt_bias = _pick_lane(_row(sc, 1), head)
    beta = jax.nn.sigmoid(b)
    g = -jnp.exp(a_log) * _softplus(a + dt_bias)
    return _two_lanes(g, beta)


def _two_lanes(c0, c1):
    lane = lax.broadcasted_iota(jnp.int32, (c0.shape[0], LANES), 1)
    return jnp.where(lane == 0, c0, jnp.where(lane == 1, c1, 0.0))


def _inverse_group(lows):
    n = lows[0].shape[0]
    ii = lax.broadcasted_iota(jnp.int32, (n, n), 0)
    jj = lax.broadcasted_iota(jnp.int32, (n, n), 1)
    eye = jnp.where(ii == jj, 1.0, 0.0)
    invs = [eye - low for low in lows]
    powers = [_dot3(low, low, 1, 0) for low in lows]
    k = 2
    while k < n:
        invs = [inv + _dot3(inv, p, 1, 0) for inv, p in zip(invs, powers)]
        k *= 2
        if k < n:
            powers = [_dot3(p, p, 1, 0) for p in powers]
    return invs


@jax.custom_vjp
def solve_group(lows, rhss):
    return [_dot3(inv, rhs, 1, 0) for inv, rhs in zip(_inverse_group(lows), rhss)]


def _solve_group_fwd(lows, rhss):
    invs = _inverse_group(lows)
    xs = [_dot3(inv, rhs, 1, 0) for inv, rhs in zip(invs, rhss)]
    return xs, (invs, xs)


def _solve_group_bwd(res, dxs):
    invs, xs = res
    n = invs[0].shape[0]
    strict = lax.broadcasted_iota(jnp.int32, (n, n), 0) > lax.broadcasted_iota(jnp.int32, (n, n), 1)
    drhss = [_dot3(inv, dx, 0, 0) for inv, dx in zip(invs, dxs)]
    dlows = [jnp.where(strict, -_dot3(drhs, x, 1, 1), 0.0) for drhs, x in zip(drhss, xs)]
    return dlows, drhss


solve_group.defvjp(_solve_group_fwd, _solve_group_bwd)


def _gdn_local_group(qs, ks, vs, gbs):
    c = qs[0].shape[0]
    ii = lax.broadcasted_iota(jnp.int32, (c, c), 0)
    jj = lax.broadcasted_iota(jnp.int32, (c, c), 1)
    incl = ii >= jj
    gs = [_pick_lane(gb, 0) for gb in gbs]
    betas = [_pick_lane(gb, 1) for gb in gbs]
    g_rows = [jnp.sum(jnp.where(ii == jj, g, 0.0), axis=0, keepdims=True) for g in gs]
    gc_cols = [jnp.sum(jnp.where(incl, g_row, 0.0), axis=1, keepdims=True) for g_row in g_rows]
    gc_rows = [jnp.sum(jnp.where(jj >= ii, g, 0.0), axis=0, keepdims=True) for g in gs]
    decays = [jnp.where(incl, jnp.exp(jnp.where(incl, gc - gr, 0.0)), 0.0) for gc, gr in zip(gc_cols, gc_rows)]
    kbs = [k * beta for k, beta in zip(ks, betas)]
    lows = [jnp.where(ii > jj, mm_nt(kb, k) * decay, 0.0) for kb, k, decay in zip(kbs, ks, decays)]
    egs = [jnp.exp(gc) for gc in gc_cols]
    wus = solve_group(lows, [jnp.concatenate([kb * eg, v * beta], axis=1) for kb, eg, v, beta in zip(kbs, egs, vs, betas)])
    qks = [mm_nt(q, k) * decay for q, k, decay in zip(qs, ks, decays)]
    g_lasts = [jnp.sum(g_row, axis=1, keepdims=True) for g_row in g_rows]
    kds = [k * jnp.exp(gl - gc) for k, gl, gc in zip(ks, g_lasts, gc_cols)]
    return [(wu[:, :GDN_DK], wu[:, GDN_DK:], qk, q * eg, kd, jnp.exp(gl))
            for wu, qk, q, eg, kd, gl in zip(wus, qks, qs, egs, kds, g_lasts)]


def _gdn_state_step(w, u, qk, qg, kd, eg_last, state):
    v_new = u - mm(w, state)
    o = mm(qg, state) + mm(qk, v_new)
    return o, state * eg_last + mm_tn(kd, v_new)


def _gdn_chunk_packed(q, k, v, gb, state):
    return _gdn_chunk(q, k, v, _pick_lane(gb, 0), _pick_lane(gb, 1), state)


def _gdn_post(o, z, norm_g):
    return _rmsnorm(o, norm_g) * _silu(z)


def _gdn_chunk(q, k, v, g, beta, state):
    c = q.shape[0]
    ii = lax.broadcasted_iota(jnp.int32, (c, c), 0)
    jj = lax.broadcasted_iota(jnp.int32, (c, c), 1)
    incl = ii >= jj
    g_row = jnp.sum(jnp.where(ii == jj, g, 0.0), axis=0, keepdims=True)
    gc_col = jnp.sum(jnp.where(incl, g_row, 0.0), axis=1, keepdims=True)
    gc_row = jnp.sum(jnp.where(jj >= ii, g, 0.0), axis=0, keepdims=True)
    decay = jnp.where(incl, jnp.exp(jnp.where(incl, gc_col - gc_row, 0.0)), 0.0)
    kb = k * beta
    low = jnp.where(ii > jj, mm_nt(kb, k) * decay, 0.0)
    eg = jnp.exp(gc_col)
    wu = unit_lower_solve(low, jnp.concatenate([kb * eg, v * beta], axis=1))
    w, u = wu[:, :GDN_DK], wu[:, GDN_DK:]
    qk = mm_nt(q, k) * decay
    g_last = jnp.sum(g_row, axis=1, keepdims=True)
    kd = k * jnp.exp(g_last - gc_col)
    v_new = u - mm(w, state)
    o = mm(q * eg, state) + mm(qk, v_new)
    new_state = state * jnp.exp(g_last) + mm_tn(kd, v_new)
    return o, new_state


def _attn_block(q, kn, kr, v, q0):
    s = (mm_nt(q[:, :128], kn) + mm_nt(q[:, 128:], kr)) * ATT_SCALE
    qpos = q0 + lax.broadcasted_iota(jnp.int32, s.shape, 0)
    kpos = lax.broadcasted_iota(jnp.int32, s.shape, 1)
    shift = int(math.log2(CHUNK))
    allowed = (kpos >> shift) <= (qpos >> shift)
    s = jnp.where(allowed, s, -1e30)
    p = jnp.exp(s - jnp.max(s, axis=-1, keepdims=True))
    p = p / jnp.sum(p, axis=-1, keepdims=True)
    return mm(p, v)


def _mla_prep(pm, qg, kvg, wq, wkv, cos, sin):
    cq = pm[:, 128:512]
    ckv = pm[:, 512:768]
    qf = mm(_rmsnorm(cq, qg), wq)
    parts = []
    for h in range(MLA_HEADS):
        parts.append(qf[:, h * 256:h * 256 + 128])
        parts.append(rope128(qf[:, h * 256 + 128:h * 256 + 256], cos, sin))
    kvf = mm(_rmsnorm(ckv, kvg), wkv)
    return jnp.concatenate(parts, axis=1), kvf, rope128(pm[:, 768:896], cos, sin)


def _ffn_act(ug, uu, wg, wu, bg, bu):
    return _silu(dwconv(ug, wg) + bg) * (dwconv(uu, wu) + bu)


def _head_loss(h1, ffn, gpre, pp, bgate, g2, b2, target):
    gate = jax.nn.sigmoid(gpre + bgate)
    h2 = _layernorm(ALPHA * h1 + ffn + gate * pp, g2, b2)
    err = h2 - target
    return 0.5 * jnp.sum(jnp.sum(err * err, axis=1, keepdims=True), axis=0, keepdims=True) / D_MODEL


ROW_TILE_VREGS = 32
CONV_HALO = 8


def _rows_per_tile(n_rows, cols):
    tile = min(n_rows, ROW_TILE_VREGS * 8 * LANES // cols)
    assert n_rows % tile == 0 and tile % CONV_HALO == 0, (n_rows, cols)
    return tile


def _tile_inputs(loads, t0, first, halo, tile):
    if halo == 0:
        return [ld(pl.ds(t0, tile)) for ld in loads]
    if first:
        xs = [ld(pl.ds(0, tile)) for ld in loads]
        return [jnp.concatenate([jnp.zeros((halo, x.shape[1]), x.dtype), x], axis=0) for x in xs]
    return [ld(pl.ds(pl.multiple_of(t0 - halo, CONV_HALO), tile + halo)) for ld in loads]


def _rows_apply(fn, loads, consts, store, n_rows, cols, halo):
    tile = _rows_per_tile(n_rows, cols)

    def one(t0, first):
        y = fn(*_tile_inputs(loads, t0, first, halo, tile), *consts)
        store(pl.ds(t0, tile), y[halo:] if halo else y)

    one(0, True)

    def step(i, carry):
        one(pl.multiple_of(i * tile, tile), False)
        return carry

    lax.fori_loop(1, n_rows // tile, step, 0)


def _rows_vjp(fn, loads, consts, load_dy, stores, n_rows, cols, halo):
    tile = _rows_per_tile(n_rows, cols)

    def one(t0, first, dconsts):
        xs = _tile_inputs(loads, t0, first, halo, tile)
        _, vjp = jax.vjp(lambda *a: fn(*a)[halo:] if halo else fn(*a), *xs, *consts)
        grads = vjp(load_dy(pl.ds(t0, tile)))
        for st, dx in zip(stores, grads[:len(xs)]):
            st(pl.ds(t0, tile), dx[halo:] if halo else dx, False)
            if halo and not first:
                st(pl.ds(pl.multiple_of(t0 - halo, CONV_HALO), halo), dx[:halo], True)
        return tuple(a + b for a, b in zip(dconsts, grads[len(xs):]))

    dconsts = one(0, True, tuple(jnp.zeros_like(c) for c in consts))
    return lax.fori_loop(1, n_rows // tile, lambda i, dc: one(pl.multiple_of(i * tile, tile), False, dc), dconsts)


def _params(sem):
    return pltpu.CompilerParams(dimension_semantics=sem, vmem_limit_bytes=VMEM_LIMIT)


def _matmul(a, b, *, name, ta=False, tb=False, tm=512, tn=512, tk=1024, add=None, add_scale=1.0,
            a_halves=False, b_halves=False, ride=None):
    assert not (a_halves and ta) and not (b_halves and tb)
    a_shape = (a.shape[1], 2 * a.shape[2]) if a_halves else a.shape
    b_shape = (b.shape[1], 2 * b.shape[2]) if b_halves else b.shape
    (k_dim, m) = a_shape if ta else a_shape[::-1]
    (n, k2) = b_shape if tb else b_shape[::-1]
    assert k_dim == k2, (a.shape, b.shape)
    tm, tn, tk = min(tm, m), min(tn, n), min(tk, k_dim)
    assert m % tm == 0 and n % tn == 0 and k_dim % tk == 0, (name, m, n, k_dim, tm, tn, tk)
    nk = k_dim // tk
    ca, cb = (0 if ta else 1), (1 if tb else 0)

    def body(*refs):
        if add is None:
            a_ref, b_ref, o_ref, acc = refs
        else:
            a_ref, b_ref, c_ref, o_ref, acc = refs
        kk = pl.program_id(2)

        @pl.when(kk == 0)
        def _():
            acc[...] = jnp.zeros_like(acc)

        acc[...] += _dot(a_ref[...], b_ref[...], ca, cb)

        @pl.when(kk == nk - 1)
        def _():
            r = acc[...]
            if add is not None:
                r = r + add_scale * c_ref[...]
            o_ref[...] = r

    a_spec = pl.BlockSpec((tk, tm), lambda i, j, k: (k, i)) if ta else pl.BlockSpec((tm, tk), lambda i, j, k: (i, k))
    b_spec = pl.BlockSpec((tn, tk), lambda i, j, k: (j, k)) if tb else pl.BlockSpec((tk, tn), lambda i, j, k: (k, j))
    if a_halves:
        kh = k_dim // 2 // tk
        assert kh * tk * 2 == k_dim
        a_spec = pl.BlockSpec((None, tm, tk), lambda i, j, k: (k // kh, i, k % kh))
    if b_halves:
        nh = n // 2 // tn
        assert nh * tn * 2 == n
        b_spec = pl.BlockSpec((None, tk, tn), lambda i, j, k: (j // nh, k, j % nh))
    in_specs = [a_spec, b_spec]
    args = [a, b]
    if add is not None:
        in_specs.append(pl.BlockSpec((tm, tn), lambda i, j, k: (i, j)))
        args.append(add)
    grid = (m // tm, n // tn, nk)
    r_in, r_out, r_shapes, r_sems, r_args = _ride_specs(ride)
    outs = pl.pallas_call(
        _riding(body, len(args), 1, 1, ride, grid), name=name, grid=grid,
        in_specs=in_specs + r_in, out_specs=[pl.BlockSpec((tm, tn), lambda i, j, k: (i, j))] + r_out,
        out_shape=[jax.ShapeDtypeStruct((m, n), F32)] + r_shapes,
        scratch_shapes=[pltpu.VMEM((tm, tn), F32)] + r_sems,
        compiler_params=_params(("parallel", "parallel", "arbitrary") if ride is None else ("arbitrary",) * 3),
    )(*args, *r_args)
    return outs[0] if ride is None else (outs[0], list(outs[1:]))


def _riding(core, n_in, n_out, n_scratch, ride, grid):
    if ride is None:
        return core
    copies, nr = ride[1], len(ride[0])
    steps = int(np.prod(grid))
    assert steps >= 3, grid

    def body(*refs):
        cuts = np.cumsum([0, n_in, nr, n_out, nr, n_scratch])
        ins, rin, outs, rout, scratch = (refs[a:b] for a, b in zip(cuts[:-1], cuts[1:]))
        sems = refs[cuts[-1]:]
        step = 0
        for axis, size in enumerate(grid):
            step = step * size + pl.program_id(axis)

        @pl.when(step == 0)
        def _():
            _exchange_start(copies(rin, rout, sems))

        @pl.when(step == steps - 2)
        def _():
            _exchange_pass_on(copies(rin, rout, sems))

        core(*ins, *outs, *scratch)

        @pl.when(step == steps - 1)
        def _():
            _exchange_finish(copies(rin, rout, sems))

    return body


def _ride_specs(ride):
    if ride is None:
        return [], [], [], [], []
    arrays, _, sems, shapes = ride
    return [HBM_REF] * len(arrays), [HBM_REF] * len(arrays), list(shapes), sems(len(arrays)), list(arrays)


def _gdn_fwd(proj, conv_w, sc, norm_g, bl, s, ride=None):
    nc = s // CHUNK

    def core(ph_ref, ab_ref, cwq_ref, cwk_ref, cwv_ref, sc_ref, ng_ref, cat_ref, o_ref, st_ref, q_s, k_s, v_s, gb_s):
        def into(ref):
            def store(rows, value):
                ref[rows, :] = value.astype(ref.dtype)
            return store

        for fn, col, cw_ref, val_s in [(_gdn_q, 0, cwq_ref, q_s), (_gdn_k, 128, cwk_ref, k_s), (_gdn_v, 256, cwv_ref, v_s)]:
            _rows_apply(fn, [lambda r, col=col: ph_ref[r, col:col + 128]], [cw_ref[...]], into(val_s), s, LANES, CONV_HALO)
        _rows_apply(functools.partial(_gdn_gate, head=pl.program_id(1)), [lambda r: ab_ref[r, :]], [sc_ref[...]], into(gb_s), s, LANES, 0)

        group = math.gcd(nc, GDN_FWD_GROUP)

        def chunks(i, state):
            ns = [i * group + j for j in range(group)]
            rows = [pl.ds(pl.multiple_of(n * CHUNK, CHUNK), CHUNK) for n in ns]
            local = _gdn_local_group([q_s[r, :] for r in rows], [k_s[r, :] for r in rows], [v_s[r, :] for r in rows],
                                     [gb_s[r, :] for r in rows])
            for n, r, loc in zip(ns, rows, local):
                st_ref[n] = state
                o_ref[r, :], state = _gdn_state_step(*loc, state)
            return state

        lax.fori_loop(0, nc // group, chunks, jnp.zeros((GDN_DK, GDN_DV), F32))
        _rows_apply(_gdn_post, [lambda r: o_ref[r, :], lambda r: ph_ref[r, 384:512]], [ng_ref[...]], into(cat_ref), s, LANES, 0)

    t = bl * s
    r_in, r_out, r_shapes, r_sems, r_args = _ride_specs(ride)
    outs = pl.pallas_call(
        _riding(core, 7, 3, 4, ride, (bl, GDN_HEADS)), name="gdn_fwd", grid=(bl, GDN_HEADS),
        in_specs=[
            pl.BlockSpec((s, 512), lambda b, h: (b, h)),
            pl.BlockSpec((s, 128), lambda b, h: (b, P_MLA // 128)),
            pl.BlockSpec((GDN_CONV, 128), lambda b, h: (0, h)),
            pl.BlockSpec((GDN_CONV, 128), lambda b, h: (0, GDN_HEADS + h)),
            pl.BlockSpec((GDN_CONV, 128), lambda b, h: (0, 2 * GDN_HEADS + h)),
            pl.BlockSpec((8, 128), lambda b, h: (0, 0)),
            pl.BlockSpec((1, 128), lambda b, h: (0, 0)),
        ] + r_in,
        out_specs=[
            pl.BlockSpec((s, 128), lambda b, h: (b, h)),
            pl.BlockSpec((s, 128), lambda b, h: (b, h)),
            pl.BlockSpec((None, None, nc, GDN_DK, GDN_DV), lambda b, h: (b, h, 0, 0, 0)),
        ] + r_out,
        out_shape=[
            jax.ShapeDtypeStruct((t, 2 * GDN_VW), BF16),
            jax.ShapeDtypeStruct((t, GDN_VW), F32),
            jax.ShapeDtypeStruct((bl, GDN_HEADS, nc, GDN_DK, GDN_DV), F32),
        ] + r_shapes,
        scratch_shapes=[pltpu.VMEM((s, 128), F32)] * 4 + r_sems,
        compiler_params=_params(("arbitrary", "arbitrary")),
    )(proj, proj, conv_w, conv_w, conv_w, sc, norm_g, *r_args)
    return outs[0], outs[1], outs[2], list(outs[3:])


def _gdn_bwd(proj, conv_w, sc, norm_g, o_raw, states, dcat, bl, s, ride=None):
    nc = s // CHUNK

    def core(ph_ref, ab_ref, cwq_ref, cwk_ref, cwv_ref, sc_ref, ng_ref, o_ref, st_ref, dc_ref,
             dph_ref, dab_ref, dcwq_ref, dcwk_ref, dcwv_ref, dsc_ref, dng_ref, q_s, k_s, v_s, gb_s, do_s):
        head = pl.program_id(1)
        gate = functools.partial(_gdn_gate, head=head)
        paths = [(_gdn_q, 0, cwq_ref, q_s, dcwq_ref), (_gdn_k, 128, cwk_ref, k_s, dcwk_ref), (_gdn_v, 256, cwv_ref, v_s, dcwv_ref)]
        def into(ref, cols=slice(None)):
            def store(rows, value, add=False):
                if add:
                    ref[rows, cols] += value.astype(ref.dtype)
                else:
                    ref[rows, cols] = value.astype(ref.dtype)
            return store

        for fn, col, cw_ref, val_s, _ in paths:
            _rows_apply(fn, [lambda r, col=col: ph_ref[r, col:col + 128]], [cw_ref[...]], into(val_s), s, LANES, CONV_HALO)
        _rows_apply(gate, [lambda r: ab_ref[r, :]], [sc_ref[...]], into(gb_s), s, LANES, 0)
        (dng,) = _rows_vjp(_gdn_post, [lambda r: o_ref[r, :], lambda r: ph_ref[r, 384:512]], [ng_ref[...]],
                           lambda r: dc_ref[r, :], [into(do_s), into(dph_ref, slice(384, 512))], s, LANES, 0)
        dng_ref[...] = jnp.broadcast_to(dng, dng_ref.shape)

        group = math.gcd(nc, GDN_BWD_GROUP)

        def chunks(i, dstate):
            ns = [nc - 1 - (i * group + j) for j in range(group)]
            rows = [pl.ds(pl.multiple_of(n * CHUNK, CHUNK), CHUNK) for n in ns]
            local, local_vjp = jax.vjp(_gdn_local_group, [q_s[r, :] for r in rows], [k_s[r, :] for r in rows],
                                       [v_s[r, :] for r in rows], [gb_s[r, :] for r in rows])
            d_os = [do_s[r, :] for r in rows]
            dlocal = []
            for n, loc, d_o in zip(ns, local, d_os):
                _, step_vjp = jax.vjp(_gdn_state_step, *loc, st_ref[n])
                *dloc, dstate = step_vjp((d_o, dstate))
                dlocal.append(tuple(dloc))
            dqs, dks, dvs, dgbs = local_vjp(dlocal)
            for r, dq, dk, dv, dgb in zip(rows, dqs, dks, dvs, dgbs):
                q_s[r, :], k_s[r, :], v_s[r, :], gb_s[r, :] = dq, dk, dv, dgb
            return dstate

        lax.fori_loop(0, nc // group, chunks, jnp.zeros((GDN_DK, GDN_DV), F32))
        for fn, col, cw_ref, val_s, dcw_ref in paths:
            (dcw_ref[...],) = _rows_vjp(fn, [lambda r, col=col: ph_ref[r, col:col + 128]], [cw_ref[...]],
                                        lambda r, val_s=val_s: val_s[r, :], [into(val_s)], s, LANES, CONV_HALO)
            dph_ref[:, col:col + 128] = val_s[...].astype(BF16)

        @pl.when(head == 0)
        def _():
            dab_ref[...] = jnp.zeros_like(dab_ref)

        def add_dab(rows, value, add=False):
            dab_ref[rows, :] += value

        (dsc_ref[...],) = _rows_vjp(gate, [lambda r: ab_ref[r, :]], [sc_ref[...]], lambda r: gb_s[r, :], [add_dab], s, LANES, 0)

    t = bl * s
    cw_out = pl.BlockSpec((None, GDN_CONV, 128), lambda b, h: (b, 0, h))
    part = pl.BlockSpec((None, None, 8, 128), lambda b, h: (b, h, 0, 0))
    r_in, r_out, r_shapes, r_sems, r_args = _ride_specs(ride)
    outs = pl.pallas_call(
        _riding(core, 10, 7, 5, ride, (bl, GDN_HEADS)), name="gdn_bwd", grid=(bl, GDN_HEADS),
        in_specs=[
            pl.BlockSpec((s, 512), lambda b, h: (b, h)),
            pl.BlockSpec((s, 128), lambda b, h: (b, P_MLA // 128)),
            pl.BlockSpec((GDN_CONV, 128), lambda b, h: (0, h)),
            pl.BlockSpec((GDN_CONV, 128), lambda b, h: (0, GDN_HEADS + h)),
            pl.BlockSpec((GDN_CONV, 128), lambda b, h: (0, 2 * GDN_HEADS + h)),
            pl.BlockSpec((8, 128), lambda b, h: (0, 0)),
            pl.BlockSpec((1, 128), lambda b, h: (0, 0)),
            pl.BlockSpec((s, 128), lambda b, h: (b, h)),
            pl.BlockSpec((None, None, nc, GDN_DK, GDN_DV), lambda b, h: (b, h, 0, 0, 0)),
            pl.BlockSpec((s, 128), lambda b, h: (b, h)),
        ] + r_in,
        out_specs=[
            pl.BlockSpec((s, 512), lambda b, h: (b, h)),
            pl.BlockSpec((s, 128), lambda b, h: (b, 0)),
            cw_out, cw_out, cw_out, part, part,
        ] + r_out,
        out_shape=[
            jax.ShapeDtypeStruct((t, P_WIDTH), BF16),
            jax.ShapeDtypeStruct((t, 128), F32),
            jax.ShapeDtypeStruct((bl, GDN_CONV, 512), F32),
            jax.ShapeDtypeStruct((bl, GDN_CONV, 512), F32),
            jax.ShapeDtypeStruct((bl, GDN_CONV, 512), F32),
            jax.ShapeDtypeStruct((bl, GDN_HEADS, 8, 128), F32),
            jax.ShapeDtypeStruct((bl, GDN_HEADS, 8, 128), F32),
        ] + r_shapes,
        scratch_shapes=[pltpu.VMEM((s, 128), F32)] * 5 + r_sems,
        compiler_params=_params(("arbitrary", "arbitrary")),
    )(proj, proj, conv_w, conv_w, conv_w, sc, norm_g, o_raw, states, dcat, *r_args)
    return tuple(outs[:7]) + (list(outs[7:]),)


def _mla_prep_fwd(proj, qg, kvg, wq, wkv, cos, sin, s, tm):
    t = proj.shape[0]
    tm = min(tm, s)
    nps = s // tm
    const = lambda shape: pl.BlockSpec(shape, lambda i: (0, 0))

    def body(pm_ref, qg_ref, kvg_ref, wq_ref, wkv_ref, cos_ref, sin_ref, qf_ref, kvf_ref, kr_ref):
        qf, kvf, kr = _mla_prep(pm_ref[...], qg_ref[...], kvg_ref[...], wq_ref[...], wkv_ref[...], cos_ref[...], sin_ref[...])
        qf_ref[...], kvf_ref[...], kr_ref[...] = qf.astype(BF16), kvf.astype(BF16), kr.astype(BF16)

    return pl.pallas_call(
        body, name="mla_prep_fwd", grid=(t // tm,),
        in_specs=[
            pl.BlockSpec((tm, 1024), lambda i: (i, P_MLA // 1024)),
            const((1, MLA_Q_LORA)), const((1, MLA_KV_LORA)), const(wq.shape), const(wkv.shape),
            pl.BlockSpec((tm, 128), lambda i: (i % nps, 0)), pl.BlockSpec((tm, 128), lambda i: (i % nps, 0)),
        ],
        out_specs=[pl.BlockSpec((tm, 1024), lambda i: (i, 0)), pl.BlockSpec((tm, 1024), lambda i: (i, 0)),
                   pl.BlockSpec((tm, 128), lambda i: (i, 0))],
        out_shape=[jax.ShapeDtypeStruct((t, 1024), BF16), jax.ShapeDtypeStruct((t, 1024), BF16),
                   jax.ShapeDtypeStruct((t, 128), BF16)],
        compiler_params=_params(("parallel",)),
    )(proj, qg, kvg, wq, wkv, cos, sin)


def _mla_prep_bwd(proj, qg, kvg, wq, wkv, cos, sin, dqf, dkvf, dkr, dab, dproj, s, tm):
    t = proj.shape[0]
    tm = min(tm, s)
    nps = s // tm
    const = lambda shape: pl.BlockSpec(shape, lambda i: (0, 0))

    def body(pm_ref, qg_ref, kvg_ref, wq_ref, wkv_ref, cos_ref, sin_ref, dqf_ref, dkvf_ref, dkr_ref, dab_ref, dp_in,
             dp_ref, dqg_ref, dkvg_ref, dwq_ref, dwkv_ref):
        del dp_in
        fn = lambda pm, qg_, kvg_, wq_, wkv_: _mla_prep(pm, qg_, kvg_, wq_, wkv_, cos_ref[...], sin_ref[...])
        _, vjp = jax.vjp(fn, pm_ref[...], qg_ref[...], kvg_ref[...], wq_ref[...].astype(F32), wkv_ref[...].astype(F32))
        dpm, dqg, dkvg, dwq, dwkv = vjp((dqf_ref[...], dkvf_ref[...], dkr_ref[...]))
        dp_ref[...] = jnp.concatenate([dab_ref[...], dpm[:, 128:]], axis=1).astype(BF16)

        @pl.when(pl.program_id(0) == 0)
        def _():
            dqg_ref[...] = jnp.zeros_like(dqg_ref)
            dkvg_ref[...] = jnp.zeros_like(dkvg_ref)
            dwq_ref[...] = jnp.zeros_like(dwq_ref)
            dwkv_ref[...] = jnp.zeros_like(dwkv_ref)

        dqg_ref[...] += dqg
        dkvg_ref[...] += dkvg
        dwq_ref[...] += dwq
        dwkv_ref[...] += dwkv

    rows = lambda w: pl.BlockSpec((tm, w), lambda i: (i, 0))
    return pl.pallas_call(
        body, name="mla_prep_bwd", grid=(t // tm,),
        in_specs=[
            pl.BlockSpec((tm, 1024), lambda i: (i, P_MLA // 1024)),
            const((1, MLA_Q_LORA)), const((1, MLA_KV_LORA)), const(wq.shape), const(wkv.shape),
            pl.BlockSpec((tm, 128), lambda i: (i % nps, 0)), pl.BlockSpec((tm, 128), lambda i: (i % nps, 0)),
            rows(1024), rows(1024), rows(128), rows(128),
            pl.BlockSpec(memory_space=pl.ANY),
        ],
        out_specs=[pl.BlockSpec((tm, 1024), lambda i: (i, P_MLA // 1024)),
                   const((1, MLA_Q_LORA)), const((1, MLA_KV_LORA)), const(wq.shape), const(wkv.shape)],
        out_shape=[jax.ShapeDtypeStruct(dproj.shape, dproj.dtype),
                   jax.ShapeDtypeStruct((1, MLA_Q_LORA), F32), jax.ShapeDtypeStruct((1, MLA_KV_LORA), F32),
                   jax.ShapeDtypeStruct(wq.shape, F32), jax.ShapeDtypeStruct(wkv.shape, F32)],
        input_output_aliases={11: 0},
        compiler_params=_params(("arbitrary",)),
    )(proj, qg, kvg, wq, wkv, cos, sin, dqf, dkvf, dkr, dab, dproj)


def _attn_fwd(qf, kvf, kr, cat, bl, s, tq):
    tq = min(tq, s)
    nq = s // tq

    def body(q_ref, kv_ref, kr_ref, cat_in, o_ref):
        del cat_in
        for i in range(nq):
            rows, keys = slice(i * tq, (i + 1) * tq), slice(0, (i + 1) * tq)
            o = _attn_block(q_ref[rows, :], kv_ref[keys, 0:128], kr_ref[keys, :], kv_ref[keys, 128:256], i * tq)
            o_ref[rows, :] = o.astype(o_ref.dtype)

    return pl.pallas_call(
        body, name="attn_fwd", grid=(bl, MLA_HEADS),
        in_specs=[
            pl.BlockSpec((s, 256), lambda b, h: (b, h)),
            pl.BlockSpec((s, 256), lambda b, h: (b, h)),
            pl.BlockSpec((s, 128), lambda b, h: (b, 0)),
            pl.BlockSpec(memory_space=pl.ANY),
        ],
        out_specs=pl.BlockSpec((s, 128), lambda b, h: (b, GDN_HEADS + h)),
        out_shape=jax.ShapeDtypeStruct(cat.shape, cat.dtype),
        input_output_aliases={3: 0},
        compiler_params=_params(("parallel", "parallel")),
    )(qf, kvf, kr, cat)


def _attn_bwd(qf, kvf, kr, dcat, bl, s, tq):
    tq = min(tq, s)
    nq = s // tq

    def body(q_ref, kv_ref, kr_ref, do_ref, dq_ref, dkv_ref, dkr_ref):
        dkv_ref[...] = jnp.zeros_like(dkv_ref)

        @pl.when(pl.program_id(1) == 0)
        def _():
            dkr_ref[...] = jnp.zeros_like(dkr_ref)

        for i in range(nq):
            rows, keys = slice(i * tq, (i + 1) * tq), slice(0, (i + 1) * tq)
            fn = functools.partial(_attn_block, q0=i * tq)
            f32 = lambda a: a.astype(F32)
            _, vjp = jax.vjp(fn, f32(q_ref[rows, :]), f32(kv_ref[keys, 0:128]), f32(kr_ref[keys, :]), f32(kv_ref[keys, 128:256]))
            dq_ref[rows, :], dkn, dkr, dv = vjp(do_ref[rows, :])
            dkv_ref[keys, 0:128] += dkn
            dkv_ref[keys, 128:256] += dv
            dkr_ref[keys, :] += dkr

    t = bl * s
    return pl.pallas_call(
        body, name="attn_bwd", grid=(bl, MLA_HEADS),
        in_specs=[
            pl.BlockSpec((s, 256), lambda b, h: (b, h)),
            pl.BlockSpec((s, 256), lambda b, h: (b, h)),
            pl.BlockSpec((s, 128), lambda b, h: (b, 0)),
            pl.BlockSpec((s, 128), lambda b, h: (b, GDN_HEADS + h)),
        ],
        out_specs=[
            pl.BlockSpec((s, 256), lambda b, h: (b, h)),
            pl.BlockSpec((s, 256), lambda b, h: (b, h)),
            pl.BlockSpec((s, 128), lambda b, h: (b, 0)),
        ],
        out_shape=[jax.ShapeDtypeStruct((t, 1024), F32), jax.ShapeDtypeStruct((t, 1024), F32),
                   jax.ShapeDtypeStruct((t, 128), F32)],
        compiler_params=_params(("parallel", "arbitrary")),
    )(qf, kvf, kr, dcat)


def _ln1_fwd(x, mix, g, b, tm):
    t = x.shape[0]
    tm = min(tm, t)

    def body(x_ref, mix_ref, g_ref, b_ref, r_ref, h_ref, hb_ref, xb_ref):
        r = ALPHA * x_ref[...] + mix_ref[...]
        r_ref[...] = r
        h = _layernorm(r, g_ref[...], b_ref[...])
        h_ref[...] = h
        hb_ref[...] = h.astype(BF16)
        xb_ref[...] = x_ref[...].astype(BF16)

    rows = pl.BlockSpec((tm, D_MODEL), lambda i: (i, 0))
    vec = pl.BlockSpec((1, D_MODEL), lambda i: (0, 0))
    return pl.pallas_call(
        body, name="ln1_fwd", grid=(t // tm,), in_specs=[rows, rows, vec, vec], out_specs=[rows] * 4,
        out_shape=[jax.ShapeDtypeStruct(x.shape, F32)] * 2 + [jax.ShapeDtypeStruct(x.shape, BF16)] * 2,
        compiler_params=_params(("parallel",)),
    )(x, mix, g, b)


def _ln1_bwd(r1, dr2, da, db_, g, b, tm):
    t = r1.shape[0]
    tm = min(tm, t)

    def body(r_ref, d2_ref, da_ref, db_ref, g_ref, b_ref, dr_ref, drb_ref, dg_ref, dbias_ref):
        dh = ALPHA * d2_ref[...] + da_ref[...] + db_ref[...]
        _, vjp = jax.vjp(_layernorm, r_ref[...], g_ref[...], b_ref[...])
        dr, dg, dbias = vjp(dh)
        dr_ref[...] = dr
        drb_ref[...] = dr.astype(BF16)

        @pl.when(pl.program_id(0) == 0)
        def _():
            dg_ref[...] = jnp.zeros_like(dg_ref)
            dbias_ref[...] = jnp.zeros_like(dbias_ref)

        dg_ref[...] += dg
        dbias_ref[...] += dbias

    rows = pl.BlockSpec((tm, D_MODEL), lambda i: (i, 0))
    vec = pl.BlockSpec((1, D_MODEL), lambda i: (0, 0))
    return pl.pallas_call(
        body, name="ln1_bwd", grid=(t // tm,), in_specs=[rows] * 4 + [vec, vec], out_specs=[rows, rows, vec, vec],
        out_shape=[jax.ShapeDtypeStruct(r1.shape, F32), jax.ShapeDtypeStruct(r1.shape, BF16)]
        + [jax.ShapeDtypeStruct((1, D_MODEL), F32)] * 2,
        compiler_params=_params(("arbitrary",)),
    )(r1, dr2, da, db_, g, b)


def _ffn_act_fwd(u, conv_w, conv_b, bl, s, cb):
    nj = D_FF // cb

    def body(ug_ref, uu_ref, wg_ref, wu_ref, bg_ref, bu_ref, act_ref):
        def store(rows, act):
            act_ref[rows, :] = act.astype(BF16)

        _rows_apply(_ffn_act, [lambda r: ug_ref[r, :], lambda r: uu_ref[r, :]],
                    [wg_ref[...], wu_ref[...], bg_ref[...], bu_ref[...]], store, s, cb, CONV_HALO)

    return pl.pallas_call(
        body, name="ffn_act_fwd", grid=(bl, nj),
        in_specs=[
            pl.BlockSpec((s, cb), lambda b, j: (b, j)), pl.BlockSpec((s, cb), lambda b, j: (b, nj + j)),
            pl.BlockSpec((FFN_CONV, cb), lambda b, j: (0, j)), pl.BlockSpec((FFN_CONV, cb), lambda b, j: (0, nj + j)),
            pl.BlockSpec((1, cb), lambda b, j: (0, j)), pl.BlockSpec((1, cb), lambda b, j: (0, nj + j)),
        ],
        out_specs=pl.BlockSpec((s, cb), lambda b, j: (b, j)),
        out_shape=jax.ShapeDtypeStruct((bl * s, D_FF), BF16),
        compiler_params=_params(("parallel", "parallel")),
    )(u, u, conv_w, conv_w, conv_b, conv_b)


def _ffn_act_bwd(u, conv_w, conv_b, dact, bl, s, cb):
    nj = D_FF // cb

    def body(ug_ref, uu_ref, wg_ref, wu_ref, bg_ref, bu_ref, da_ref, du_ref, dwg_ref, dwu_ref, dbg_ref, dbu_ref, acc):
        def store_into(half):
            def store(rows, value, add):
                if add:
                    acc[half, rows, :] += value
                else:
                    acc[half, rows, :] = value
            return store

        dwg_ref[...], dwu_ref[...], dbg_ref[...], dbu_ref[...] = _rows_vjp(
            _ffn_act, [lambda r: ug_ref[r, :], lambda r: uu_ref[r, :]], [wg_ref[...], wu_ref[...], bg_ref[...], bu_ref[...]],
            lambda r: da_ref[r, :], [store_into(0), store_into(1)], s, cb, CONV_HALO)
        du_ref[...] = acc[...].astype(BF16)

    t = bl * s
    blk = pl.BlockSpec((s, cb), lambda b, j: (b, j))
    wpart = pl.BlockSpec((None, FFN_CONV, cb), lambda b, j: (b, 0, j))
    bpart = pl.BlockSpec((None, 1, cb), lambda b, j: (b, 0, j))
    return pl.pallas_call(
        body, name="ffn_act_bwd", grid=(bl, nj),
        in_specs=[
            blk, pl.BlockSpec((s, cb), lambda b, j: (b, nj + j)),
            pl.BlockSpec((FFN_CONV, cb), lambda b, j: (0, j)), pl.BlockSpec((FFN_CONV, cb), lambda b, j: (0, nj + j)),
            pl.BlockSpec((1, cb), lambda b, j: (0, j)), pl.BlockSpec((1, cb), lambda b, j: (0, nj + j)),
            blk,
        ],
        out_specs=[pl.BlockSpec((2, s, cb), lambda b, j: (0, b, j)), wpart, wpart, bpart, bpart],
        out_shape=[jax.ShapeDtypeStruct((2, t, D_FF), BF16)] + [jax.ShapeDtypeStruct((bl, FFN_CONV, D_FF), F32)] * 2
        + [jax.ShapeDtypeStruct((bl, 1, D_FF), F32)] * 2,
        scratch_shapes=[pltpu.VMEM((2, s, cb), F32)],
        compiler_params=_params(("parallel", "parallel")),
    )(u, u, conv_w, conv_w, conv_b, conv_b, dact)


def _head(h1, ffn, gpre, pp, bgate, g2, b2, target, tm):
    t = h1.shape[0]
    tm = min(tm, t)

    def body(h1_ref, ffn_ref, gp_ref, pp_ref, bg_ref, g2_ref, b2_ref, tg_ref,
             dr_ref, drb_ref, dgp_ref, dpp_ref, loss_ref, dbg_ref, dg2_ref, db2_ref):
        fn = functools.partial(_head_loss, target=tg_ref[...])
        loss, vjp = jax.vjp(fn, h1_ref[...], ffn_ref[...], gp_ref[...], pp_ref[...], bg_ref[...], g2_ref[...], b2_ref[...])
        _, dffn, dgp, dpp, dbg, dg2, db2 = vjp(jnp.ones((1, 1), F32))
        dr_ref[...] = dffn
        drb_ref[...], dgp_ref[...], dpp_ref[...] = dffn.astype(BF16), dgp.astype(BF16), dpp.astype(BF16)

        @pl.when(pl.program_id(0) == 0)
        def _():
            loss_ref[...] = jnp.zeros_like(loss_ref)
            dbg_ref[...] = jnp.zeros_like(dbg_ref)
            dg2_ref[...] = jnp.zeros_like(dg2_ref)
            db2_ref[...] = jnp.zeros_like(db2_ref)

        loss_ref[...] += jnp.broadcast_to(loss, loss_ref.shape)
        dbg_ref[...] += dbg
        dg2_ref[...] += dg2
        db2_ref[...] += db2

    rows = pl.BlockSpec((tm, D_MODEL), lambda i: (i, 0))
    vec = pl.BlockSpec((1, D_MODEL), lambda i: (0, 0))
    return pl.pallas_call(
        body, name="head", grid=(t // tm,), in_specs=[rows] * 4 + [vec] * 3 + [rows],
        out_specs=[rows] * 4 + [pl.BlockSpec((8, 128), lambda i: (0, 0))] + [vec] * 3,
        out_shape=[jax.ShapeDtypeStruct(h1.shape, F32)] + [jax.ShapeDtypeStruct(h1.shape, BF16)] * 3
        + [jax.ShapeDtypeStruct((8, 128), F32)]
        + [jax.ShapeDtypeStruct((1, D_MODEL), F32)] * 3,
        compiler_params=_params(("arbitrary",)),
    )(h1, ffn, gpre, pp, bgate, g2, b2, target)


def _adam_update(g, w_ref, m_ref, v_ref, g_ref, d_ref, nm_ref, nv_ref):
    m2 = ADAM_B1 * m_ref[...] + (1.0 - ADAM_B1) * g
    v2 = ADAM_B2 * v_ref[...] + (1.0 - ADAM_B2) * jnp.square(g)
    m_hat = m2 / (1.0 - ADAM_B1 ** ADAM_STEP)
    v_hat = v2 / (1.0 - ADAM_B2 ** ADAM_STEP)
    g_ref[...] = g
    d_ref[...] = -ADAM_LR * (m_hat / (jnp.sqrt(v_hat) + ADAM_EPS) + ADAM_WD * w_ref[...])
    nm_ref[...] = m2
    nv_ref[...] = v2


def _row_tile(rows, cols, limit_bytes=256 * 1024):
    best = None
    for t in range(8, rows + 1, 8):
        if rows % t == 0 and t * cols * 4 <= limit_bytes:
            best = t
    return best or rows


def _adamw_reduced(own, recv, w, m, v, name):
    a, b = w.shape
    ta = _row_tile(a, b)

    def body(own_ref, recv_ref, w_ref, m_ref, v_ref, g_ref, d_ref, nm_ref, nv_ref):
        c = lax.axis_index("c")
        for core in range(2):
            @pl.when(c == core)
            def _():
                got = [recv_ref[k].astype(F32) for k in range(N_DEV - 1)]
                same = [own_ref[...].astype(F32), got[0], got[1], got[2]]
                other = got[3:]
                core0, core1 = (same, other) if core == 0 else (other, same)
                g = core0[0] + core1[0]
                for r in range(1, N_CHIPS):
                    g = (g + core0[r]) + core1[r]
                _adam_update(g, w_ref, m_ref, v_ref, g_ref, d_ref, nm_ref, nv_ref)

    blk = pl.BlockSpec((ta, b), lambda i: (i, 0))
    return pl.pallas_call(
        body, name=name, grid=(a // ta,),
        in_specs=[blk, pl.BlockSpec((7, ta, b), lambda i: (0, i, 0)), blk, blk, blk], out_specs=[blk] * 4,
        out_shape=[jax.ShapeDtypeStruct(w.shape, F32)] * 4, compiler_params=_params(("parallel",)),
    )(own, recv, w, m, v)


def _adamw_small(g, w, m, v):
    def body(g_in, w_ref, m_ref, v_ref, g_ref, d_ref, nm_ref, nv_ref):
        _adam_update(g_in[...], w_ref, m_ref, v_ref, g_ref, d_ref, nm_ref, nv_ref)

    blk = pl.BlockSpec(w.shape, lambda i: (0, 0))
    return pl.pallas_call(
        body, name="adamw_small", grid=(1,), in_specs=[blk] * 4, out_specs=[blk] * 4,
        out_shape=[jax.ShapeDtypeStruct(w.shape, F32)] * 4, compiler_params=_params(("arbitrary",)),
    )(g, w, m, v)


def _remote(src, dst, send_sem, recv_sem, device):
    return pltpu.make_async_remote_copy(src_ref=src, dst_ref=dst, send_sem=send_sem, recv_sem=recv_sem,
                                        device_id=device, device_id_type=MESH)


def _place():
    x, y, c = lax.axis_index("x"), lax.axis_index("y"), lax.axis_index("c")
    return x, y, c, 2 * x + y, [(1 - x, y), (x, 1 - y), (1 - x, 1 - y)]


HBM_REF = pl.BlockSpec(memory_space=pl.ANY)
HALF_ROWS_QUANTUM = 16


def _gather_sems(n):
    return [pltpu.SemaphoreType.DMA((3 * n,))] * 4 + [pltpu.SemaphoreType.DMA((n,))]


def _gather_copies(ins, outs, sems):
    send_s, recv_s, fsend_s, frecv_s, local_s = sems
    x, y, c, me, chips = _place()
    local, sends, steps = [], [], []
    for i, (src, dst) in enumerate(zip(ins, outs)):
        local.append(pltpu.make_async_copy(src, dst.at[me], local_s.at[i]))
        half = src.shape[0] // 2
        split = src.shape[0] % (2 * HALF_ROWS_QUANTUM) == 0
        if split:
            mine = pl.ds(pl.multiple_of(c * half, HALF_ROWS_QUANTUM), half)
            theirs = pl.ds(pl.multiple_of((1 - c) * half, HALF_ROWS_QUANTUM), half)
        for r, (px, py) in enumerate(chips):
            k, peer = 3 * i + r, 2 * px + py
            if split:
                sends.append(_remote(src.at[mine], dst.at[me, mine], send_s.at[k], recv_s.at[k], (px, py, c)))
                landed = dst.at[peer, mine]
                steps.append((_remote(src.at[mine], landed, send_s.at[k], recv_s.at[k], (px, py, c)),
                              _remote(landed, landed, fsend_s.at[k], frecv_s.at[k], (x, y, 1 - c)),
                              _remote(dst.at[peer, theirs], dst.at[peer, theirs], fsend_s.at[k], frecv_s.at[k], (x, y, 1 - c))))
            else:
                sends.append(_remote(src, dst.at[me], send_s.at[k], recv_s.at[k], (px, py, c)))
                steps.append((_remote(src, dst.at[peer], send_s.at[k], recv_s.at[k], (px, py, c)), None, None))
    return local, sends, steps


def _scatter_sems(n):
    return [pltpu.SemaphoreType.DMA((4 * n,))] * 2 + [pltpu.SemaphoreType.DMA((3 * n,))] * 2


def _scatter_copies(ins, outs, sems):
    send_s, recv_s, fsend_s, frecv_s = sems
    x, y, c, me, chips = _place()
    sends, steps = [], []
    for i, (src, dst) in enumerate(zip(ins, outs)):
        for r, (px, py) in enumerate(chips):
            k = 4 * i + r
            cp = _remote(src.at[2 * px + py], dst.at[r], send_s.at[k], recv_s.at[k], (px, py, c))
            fwd = _remote(dst.at[r], dst.at[4 + r], fsend_s.at[3 * i + r], frecv_s.at[3 * i + r], (x, y, 1 - c))
            sends.append(cp)
            steps.append((cp, fwd, fwd))
        k = 4 * i + 3
        cp = _remote(src.at[me], dst.at[3], send_s.at[k], recv_s.at[k], (x, y, 1 - c))
        sends.append(cp)
        steps.append((cp, None, None))
    return [], sends, steps


def _exchange_start(plan):
    local, sends, _ = plan
    for cp in local + sends:
        cp.start()


def _exchange_pass_on(plan):
    for arrival, pass_on, _ in plan[2]:
        arrival.wait_recv()
        if pass_on is not None:
            pass_on.start()


def _exchange_finish(plan):
    local, sends, steps = plan
    for _, pass_on, passed in steps:
        if pass_on is not None:
            passed.wait_recv()
    for cp in sends:
        cp.wait_send()
    for _, pass_on, _ in steps:
        if pass_on is not None:
            pass_on.wait_send()
    for cp in local:
        cp.wait()


def _exchange_call(arrays, copies, sems, out_shapes, name):
    n = len(arrays)

    def body(*refs):
        plan = copies(refs[:n], refs[n:2 * n], refs[2 * n:])
        _exchange_start(plan)
        _exchange_pass_on(plan)
        _exchange_finish(plan)

    return pl.pallas_call(
        body, name=name, in_specs=[HBM_REF] * n, out_specs=[HBM_REF] * n, out_shape=out_shapes,
        scratch_shapes=sems(n), compiler_params=pltpu.CompilerParams(has_side_effects=True),
    )(*arrays)


def _gather_call(shards, name):
    shapes = [jax.ShapeDtypeStruct((N_CHIPS,) + a.shape, a.dtype) for a in shards]
    return _exchange_call(shards, _gather_copies, _gather_sems, shapes, name)


def _all_reduce_small(a):
    def body(in_ref, out_ref, slots, send_sems, recv_sems):
        x, y, c = lax.axis_index("x"), lax.axis_index("y"), lax.axis_index("c")
        me = 4 * x + 2 * y + c
        slots[0] = in_ref[...]
        sends = []
        for r in range(1, N_DEV):
            peer = (x ^ (r >> 2), y ^ ((r >> 1) & 1), c ^ (r & 1))
            sends.append(pltpu.make_async_remote_copy(src_ref=in_ref, dst_ref=slots.at[r], send_sem=send_sems.at[r],
                                                      recv_sem=recv_sems.at[r], device_id=peer, device_id_type=MESH))
        for cp in sends:
            cp.start()
        for cp in sends:
            cp.wait_recv()
        acc = slots[me]
        for dev in range(1, N_DEV):
            acc = acc + slots[dev ^ me]
        out_ref[...] = acc
        for cp in sends:
            cp.wait_send()

    return pl.pallas_call(
        body, name="small_all_reduce",
        in_specs=[pl.BlockSpec(memory_space=pltpu.VMEM)], out_specs=pl.BlockSpec(memory_space=pltpu.VMEM),
        out_shape=jax.ShapeDtypeStruct(a.shape, a.dtype),
        scratch_shapes=[pltpu.VMEM((N_DEV,) + a.shape, a.dtype), pltpu.SemaphoreType.DMA((N_DEV,)),
                        pltpu.SemaphoreType.DMA((N_DEV,))],
        compiler_params=pltpu.CompilerParams(has_side_effects=True),
    )(a)


SHARDED = ["w_in", "mla_w_q_up", "mla_w_kv_up", "w_out", "ffn_w_up", "ffn_w_down", "ple_w_gate", "ple_w_proj",
           "gdn_conv_w", "ffn_conv_w"]
SHARD_AXIS = {"w_in": 1, "mla_w_q_up": 1, "mla_w_kv_up": 1, "w_out": 0, "ffn_w_up": 1, "ffn_w_down": 0,
              "ple_w_gate": 0, "ple_w_proj": 1, "gdn_conv_w": 1, "ffn_conv_w": 1}
SMALL = ["gdn_a_log", "gdn_dt_bias", "gdn_norm_g", "mla_q_norm_g", "mla_kv_norm_g", "ln1_g", "ln1_b", "ffn_conv_b",
         "ple_b_gate", "ln2_g", "ln2_b"]
WEIGHTS = ["w_in", "gdn_conv_w", "gdn_a_log", "gdn_dt_bias", "gdn_norm_g", "mla_q_norm_g", "mla_w_q_up", "mla_kv_norm_g",
           "mla_w_kv_up", "w_out", "ln1_g", "ln1_b", "ffn_w_up", "ffn_conv_w", "ffn_conv_b", "ffn_w_down", "ple_w_gate",
           "ple_b_gate", "ple_w_proj", "ln2_g", "ln2_b"]
F32_ON_WIRE = ("gdn_conv_w", "ffn_conv_w")
GATHER_EARLY = ["w_in", "gdn_conv_w", "mla_w_q_up", "mla_w_kv_up"]
GATHER_LATE = ["w_out", "ffn_w_up", "ffn_conv_w", "ffn_w_down", "ple_w_gate", "ple_w_proj"]
SCATTER_EARLY = ["ffn_w_up", "ffn_conv_w", "ffn_w_down", "ple_w_gate", "ple_w_proj", "w_out"]
SCATTER_LATE = ["w_in", "gdn_conv_w", "mla_w_q_up", "mla_w_kv_up"]
PACK_COLS = 1024
PACK_ROW_TILE = 8


def _join_blocks(blocks, axis):
    n, a, b = blocks.shape
    if axis == 0:
        return blocks.reshape(n * a, b)
    return jnp.transpose(blocks, (1, 0, 2)).reshape(a, n * b)


def _split_blocks(full, axis):
    if axis == 0:
        return full.reshape(N_CHIPS, full.shape[0] // N_CHIPS, full.shape[1])
    a, nb = full.shape
    return jnp.transpose(full.reshape(a, N_CHIPS, nb // N_CHIPS), (1, 0, 2))


def _pack(arrays):
    flat = jnp.concatenate([a.reshape(-1) for a in arrays])
    quantum = PACK_COLS * PACK_ROW_TILE
    padded = -(-flat.shape[0] // quantum) * quantum
    return jnp.pad(flat, (0, padded - flat.shape[0])).reshape(-1, PACK_COLS)


def _unpack(packed, shapes):
    flat = packed.reshape(-1)
    out, off = [], 0
    for shp in shapes:
        n = int(np.prod(shp))
        out.append(flat[off:off + n].reshape(shp))
        off += n
    return out


def kernel(x, p, w_in, gdn_conv_w, gdn_a_log, gdn_dt_bias, gdn_norm_g, mla_q_norm_g, mla_w_q_up, mla_kv_norm_g, mla_w_kv_up, w_out, ln1_g, ln1_b, ffn_w_up, ffn_conv_w, ffn_conv_b, ffn_w_down, ple_w_gate, ple_b_gate, ple_w_proj, ln2_g, ln2_b, loss_target, m_w_in, m_gdn_conv_w, m_gdn_a_log, m_gdn_dt_bias, m_gdn_norm_g, m_mla_q_norm_g, m_mla_w_q_up, m_mla_kv_norm_g, m_mla_w_kv_up, m_w_out, m_ln1_g, m_ln1_b, m_ffn_w_up, m_ffn_conv_w, m_ffn_conv_b, m_ffn_w_down, m_ple_w_gate, m_ple_b_gate, m_ple_w_proj, m_ln2_g, m_ln2_b, v_w_in, v_gdn_conv_w, v_gdn_a_log, v_gdn_dt_bias, v_gdn_norm_g, v_mla_q_norm_g, v_mla_w_q_up, v_mla_kv_norm_g, v_mla_w_kv_up, v_w_out, v_ln1_g, v_ln1_b, v_ffn_w_up, v_ffn_conv_w, v_ffn_conv_b, v_ffn_w_down, v_ple_w_gate, v_ple_b_gate, v_ple_w_proj, v_ln2_g, v_ln2_b):
    given = dict(locals())
    wsh = {n: given[n][0] for n in WEIGHTS}
    msh = {n: given["m_" + n][0] for n in WEIGHTS}
    vsh = {n: given["v_" + n][0] for n in WEIGHTS}
    bl, s, _ = x.shape
    t = bl * s
    xt = x.reshape(t, D_MODEL)
    pt = p.reshape(t, PLE_DIM)
    target = loss_target.reshape(t, D_MODEL)

    wire = lambda n: wsh[n] if n in F32_ON_WIRE else wsh[n].astype(BF16)
    early = _gather_call([wire(n) for n in GATHER_EARLY], "weights_gather_early")
    full = {n: _join_blocks(g, SHARD_AXIS[n]) for n, g in zip(GATHER_EARLY, early)}
    late_shards = [wire(n) for n in GATHER_LATE]
    late_ride = (late_shards, _gather_copies, _gather_sems,
                 [jax.ShapeDtypeStruct((N_CHIPS,) + a.shape, a.dtype) for a in late_shards])

    in_cols, q_cols = _w_in_cols(), _w_q_cols()
    w_in_p = _pad_cols(full["w_in"], in_cols)
    w_q_p = _pad_cols(full["mla_w_q_up"], q_cols)
    w_kv, gconv = full["mla_w_kv_up"], full["gdn_conv_w"]
    row = lambda a: a.reshape(1, -1)
    sc = jnp.zeros((8, 128), F32).at[0, :GDN_HEADS].set(wsh["gdn_a_log"]).at[1, :GDN_HEADS].set(wsh["gdn_dt_bias"])
    norm_g, qg, kvg = row(wsh["gdn_norm_g"]), row(wsh["mla_q_norm_g"]), row(wsh["mla_kv_norm_g"])
    g1, b1, g2, b2 = row(wsh["ln1_g"]), row(wsh["ln1_b"]), row(wsh["ln2_g"]), row(wsh["ln2_b"])
    fbias, bgate = row(wsh["ffn_conv_b"]), row(wsh["ple_b_gate"])

    inv = ROPE_THETA ** (-jnp.arange(0, MLA_ROPE, 2, dtype=F32) / MLA_ROPE)
    ang = jnp.arange(s, dtype=F32)[:, None] * inv[None, :]
    zero = jnp.zeros_like(ang)
    cos_t = jnp.concatenate([jnp.cos(ang), zero, jnp.cos(ang), zero], axis=1)
    sin_t = jnp.concatenate([-jnp.sin(ang), zero, jnp.sin(ang), zero], axis=1)

    proj = _matmul(xt, w_in_p, name="proj", tm=1024)
    cat, o_raw, states, late = _gdn_fwd(proj, gconv, sc, norm_g, bl, s, late_ride)
    full.update({n: _join_blocks(g, SHARD_AXIS[n]) for n, g in zip(GATHER_LATE, late)})
    w_o, w_up, w_down = full["w_out"], full["ffn_w_up"], full["ffn_w_down"]
    w_gate, w_proj, fconv = full["ple_w_gate"], full["ple_w_proj"], full["ffn_conv_w"]
    qf, kvf, kr = _mla_prep_fwd(proj, qg, kvg, w_q_p, w_kv, cos_t, sin_t, s, 256)
    cat = _attn_fwd(qf, kvf, kr, cat, bl, s, 256)
    wide = dict(tm=1024, tn=1024)
    mix = _matmul(cat, w_o, name="mix", **wide)
    r1, h1, h1b, xb = _ln1_fwd(xt, mix, g1, b1, 256)
    u = _matmul(h1b, w_up, name="ffn_up", tm=1024)
    act = _ffn_act_fwd(u, fconv, fbias, bl, s, 256)
    ffn = _matmul(act, w_down, name="ffn_down", tk=1408, **wide)
    gpre = _matmul(h1b, w_gate, name="ple_gate", **wide)
    pp = _matmul(pt, w_proj, name="ple_proj", **wide)
    dr2, dr2b, dgpre, dpp, loss_acc, dbgate, dg2, db2 = _head(h1, ffn, gpre, pp, bgate, g2, b2, target, 256)

    dact = _matmul(dr2b, w_down, name="d_act", tb=True, tm=1024, tn=1408)
    d_w_down = _matmul(act, dr2b, name="dw_down", ta=True, tm=1408, tn=1024, tk=512)
    du, dfcw_g, dfcw_u, dfcb_g, dfcb_u = _ffn_act_bwd(u, fconv, fbias, dact, bl, s, 256)
    dh1_a = _matmul(du, w_up, name="dh1_ffn", tb=True, tk=1408, a_halves=True, **wide)
    dh1_b = _matmul(dgpre, w_gate, name="dh1_ple", tb=True, **wide)
    d_w_up = _matmul(h1b, du, name="dw_up", ta=True, tn=1408, b_halves=True)
    d_w_gate = _matmul(h1b, dgpre, name="dw_gate", ta=True, tk=512, **wide)
    d_w_proj = _matmul(pt, dpp, name="dw_proj", ta=True, tn=1024)
    dr1, dr1b, dg1, db1 = _ln1_bwd(r1, dr2, dh1_a, dh1_b, g1, b1, 256)
    dcat = _matmul(dr1b, w_o, name="d_cat", tb=True, **wide)
    d_w_o = _matmul(cat, dr1b, name="dw_out", ta=True, tk=512, **wide)

    gfull = {
        "ffn_w_up": d_w_up, "ffn_w_down": d_w_down, "ple_w_gate": d_w_gate, "ple_w_proj": d_w_proj, "w_out": d_w_o,
        "ffn_conv_w": jnp.concatenate([jnp.sum(dfcw_g, 0), jnp.sum(dfcw_u, 0)], axis=1),
    }
    slabs = {n: _split_blocks(gfull[n], SHARD_AXIS[n]) for n in SCATTER_EARLY}
    early_slabs = [slabs[n] for n in SCATTER_EARLY]
    early_ride = (early_slabs, _scatter_copies, _scatter_sems,
                  [jax.ShapeDtypeStruct((N_DEV - 1,) + a.shape[1:], a.dtype) for a in early_slabs])
    dproj, dab, dcwq, dcwk, dcwv, dsc, dng, early_recv = _gdn_bwd(proj, gconv, sc, norm_g, o_raw, states, dcat, bl, s, early_ride)
    received = dict(zip(SCATTER_EARLY, early_recv))
    dqf, dkvf, dkr = _attn_bwd(qf, kvf, kr, dcat, bl, s, 256)
    dproj, dqg, dkvg, d_w_q_p, d_w_kv = _mla_prep_bwd(proj, qg, kvg, w_q_p, w_kv, cos_t, sin_t, dqf, dkvf, dkr, dab, dproj, s, 256)
    d_w_in_p = _matmul(xb, dproj, name="dw_in", ta=True, tk=512, **wide)

    gfull.update({
        "w_in": _unpad_cols(d_w_in_p, in_cols, D_IN),
        "mla_w_q_up": _unpad_cols(d_w_q_p, q_cols, MLA_HEADS * (MLA_NOPE + MLA_ROPE)),
        "mla_w_kv_up": d_w_kv,
        "gdn_conv_w": jnp.concatenate([jnp.sum(dcwq, 0), jnp.sum(dcwk, 0), jnp.sum(dcwv, 0)], axis=1),
    })
    slabs.update({n: _split_blocks(gfull[n], SHARD_AXIS[n]).astype(BF16) for n in SCATTER_LATE})
    late_slabs = [slabs[n] for n in SCATTER_LATE]
    late_scatter = (late_slabs, _scatter_copies, _scatter_sems,
                    [jax.ShapeDtypeStruct((N_DEV - 1,) + a.shape[1:], a.dtype) for a in late_slabs])
    grad_x, late_recv = _matmul(dproj, w_in_p, name="d_x", tb=True, add=dr1, add_scale=ALPHA, ride=late_scatter, **wide)
    received.update(zip(SCATTER_LATE, late_recv))
    dsc_sum = jnp.sum(dsc, axis=(0, 1))
    gsmall = {
        "gdn_a_log": dsc_sum[0, :GDN_HEADS], "gdn_dt_bias": dsc_sum[1, :GDN_HEADS],
        "gdn_norm_g": jnp.sum(dng[:, :, 0, :], axis=(0, 1)),
        "mla_q_norm_g": dqg[0], "mla_kv_norm_g": dkvg[0], "ln1_g": dg1[0], "ln1_b": db1[0],
        "ffn_conv_b": jnp.concatenate([jnp.sum(dfcb_g, 0), jnp.sum(dfcb_u, 0)], axis=1)[0],
        "ple_b_gate": dbgate[0], "ln2_g": dg2[0], "ln2_b": db2[0],
    }

    me_chip = 2 * lax.axis_index("x") + lax.axis_index("y")
    big = [{}, {}, {}, {}]
    for n in SHARDED:
        own = lax.dynamic_index_in_dim(slabs[n], me_chip, 0, keepdims=False)
        for kind, val in enumerate(_adamw_reduced(own, received[n], wsh[n], msh[n], vsh[n], "adamw_" + n)):
            big[kind][n] = val

    small_shapes = [wsh[n].shape for n in SMALL]
    gsum = _all_reduce_small(_pack([gsmall[n] for n in SMALL]))
    spacks = _adamw_small(gsum, _pack([wsh[n] for n in SMALL]), _pack([msh[n] for n in SMALL]), _pack([vsh[n] for n in SMALL]))
    small = [dict(zip(SMALL, _unpack(pk, small_shapes))) for pk in spacks]

    loss = lax.psum(loss_acc[0, 0], ("x", "y", "c"))
    outs = [loss, grad_x.reshape(x.shape)]
    for kind in range(4):
        for n in WEIGHTS:
            val = big[kind][n] if n in big[kind] else small[kind][n]
            outs.append(val[None])
    return tuple(outs)
```

```python
import functools
import math

import numpy as np
import jax
import jax.numpy as jnp
from jax import lax
from jax.experimental import pallas as pl
from jax.experimental.pallas import tpu as pltpu

F32 = jnp.float32
BF16 = jnp.bfloat16

D_MODEL = 1024
CHUNK = 64
PLE_DIM = 256
GDN_HEADS = 4
GDN_DK = 128
GDN_DV = 128
GDN_CONV = 4
MLA_HEADS = 4
MLA_NOPE = 128
MLA_ROPE = 64
MLA_V = 128
MLA_Q_LORA = 384
MLA_KV_LORA = 256
ROPE_THETA = 10000.0
D_FF = 2816
FFN_CONV = 3
DEPTH = 1
ALPHA = (2.0 * DEPTH) ** 0.25
NORM_EPS = 1e-6
GDN_QK = GDN_HEADS * GDN_DK
GDN_VW = GDN_HEADS * GDN_DV
D_IN = 2 * GDN_QK + 2 * GDN_VW + 2 * GDN_HEADS + MLA_Q_LORA + MLA_KV_LORA + MLA_ROPE
ATT_SCALE = (MLA_NOPE + MLA_ROPE) ** -0.5

ADAM_LR = 0.001
ADAM_B1 = 0.9
ADAM_B2 = 0.999
ADAM_EPS = 1e-08
ADAM_WD = 0.01
ADAM_STEP = 10

LANES = 128
VMEM_LIMIT = 60 * 1024 * 1024
GDN_FWD_GROUP = 16
GDN_BWD_GROUP = 16
N_CHIPS = 4
N_DEV = 8

P_WIDTH = 3072
P_MLA = 2048
MESH = pl.DeviceIdType.MESH


def _rope_slot(j):
    return j if j < MLA_ROPE // 2 else 64 + (j - MLA_ROPE // 2)


def _w_in_cols():
    idx = -np.ones((P_WIDTH,), np.int64)
    for h in range(GDN_HEADS):
        base = h * 512
        idx[base:base + 128] = np.arange(128) + h * GDN_DK
        idx[base + 128:base + 256] = np.arange(128) + GDN_QK + h * GDN_DK
        idx[base + 256:base + 384] = np.arange(128) + 2 * GDN_QK + h * GDN_DV
        idx[base + 384:base + 512] = np.arange(128) + 2 * GDN_QK + GDN_VW + h * GDN_DV
    o_a = 2 * GDN_QK + 2 * GDN_VW
    idx[P_MLA:P_MLA + 2 * GDN_HEADS] = np.arange(2 * GDN_HEADS) + o_a
    o_cq = o_a + 2 * GDN_HEADS
    idx[P_MLA + 128:P_MLA + 512] = np.arange(MLA_Q_LORA) + o_cq
    o_ckv = o_cq + MLA_Q_LORA
    idx[P_MLA + 512:P_MLA + 768] = np.arange(MLA_KV_LORA) + o_ckv
    o_kr = o_ckv + MLA_KV_LORA
    for j in range(MLA_ROPE):
        idx[P_MLA + 768 + _rope_slot(j)] = o_kr + j
    return idx


def _w_q_cols():
    idx = -np.ones((MLA_HEADS * 256,), np.int64)
    for h in range(MLA_HEADS):
        o = h * (MLA_NOPE + MLA_ROPE)
        idx[h * 256:h * 256 + 128] = np.arange(128) + o
        for j in range(MLA_ROPE):
            idx[h * 256 + 128 + _rope_slot(j)] = o + MLA_NOPE + j
    return idx


def _pad_cols(w, idx):
    safe = np.where(idx >= 0, idx, 0)
    return jnp.where(jnp.asarray(idx >= 0)[None, :], w[:, safe], 0.0)


def _unpad_cols(wp, idx, n):
    inv = np.zeros((n,), np.int64)
    inv[idx[idx >= 0]] = np.nonzero(idx >= 0)[0]
    return wp[:, inv]


def _dot(a, b, ca, cb, precision=None):
    if precision is None:
        a = a.astype(BF16)
        b = b.astype(BF16)
    return lax.dot_general(a, b, (((ca,), (cb,)), ((), ())), preferred_element_type=F32, precision=precision)


@jax.custom_vjp
def mm(a, b):
    return _dot(a, b, 1, 0)


@jax.custom_vjp
def mm_nt(a, b):
    return _dot(a, b, 1, 1)


@jax.custom_vjp
def mm_tn(a, b):
    return _dot(a, b, 0, 0)


mm.defvjp(lambda a, b: (mm(a, b), (a, b)), lambda r, g: (mm_nt(g, r[1]), mm_tn(r[0], g)))
mm_nt.defvjp(lambda a, b: (mm_nt(a, b), (a, b)), lambda r, g: (mm(g, r[1]), mm_tn(g, r[0])))
mm_tn.defvjp(lambda a, b: (mm_tn(a, b), (a, b)), lambda r, g: (mm_nt(r[1], g), mm(r[0], g)))

def _split(a):
    hi = a.astype(BF16)
    return hi, (a - hi.astype(F32)).astype(BF16)


def _dot3(a, b, ca, cb):
    a_hi, a_lo = _split(a)
    b_hi, b_lo = _split(b)
    return (_dot(a_hi, b_hi, ca, cb) + _dot(a_hi, b_lo, ca, cb)) + _dot(a_lo, b_hi, ca, cb)


def _shift_rows(x, s):
    return x if s == 0 else pltpu.roll(x, s % x.shape[0], 0)


def _row(w, j):
    tap = lax.broadcasted_iota(jnp.int32, w.shape, 0)
    return jnp.sum(jnp.where(tap == j, w, 0.0), axis=0, keepdims=True)


@jax.custom_vjp
def dwconv(x, w):
    k = w.shape[0]
    y = _row(w, k - 1) * x
    for j in range(k - 1):
        y = y + _row(w, j) * _shift_rows(x, k - 1 - j)
    return y


def _dwconv_fwd(x, w):
    return dwconv(x, w), (x, w)


def _dwconv_bwd(res, dy):
    x, w = res
    k = w.shape[0]
    dx = _row(w, k - 1) * dy
    tap = lax.broadcasted_iota(jnp.int32, w.shape, 0)
    dw = jnp.where(tap == k - 1, jnp.sum(dy * x, axis=0, keepdims=True), 0.0)
    for j in range(k - 1):
        dx = dx + _row(w, j) * _shift_rows(dy, -(k - 1 - j))
        dw = dw + jnp.where(tap == j, jnp.sum(dy * _shift_rows(x, k - 1 - j), axis=0, keepdims=True), 0.0)
    return dx, dw


dwconv.defvjp(_dwconv_fwd, _dwconv_bwd)


@jax.custom_vjp
def rope128(x, cos, sin):
    return x * cos + pltpu.roll(x, 64, 1) * sin


rope128.defvjp(lambda x, c, s: (rope128(x, c, s), (c, s)),
               lambda r, g: (g * r[0] + pltpu.roll(g * r[1], 64, 1), jnp.zeros_like(r[0]), jnp.zeros_like(r[1])))


def _silu(x):
    return x * jax.nn.sigmoid(x)


def _softplus(x):
    return jnp.maximum(x, 0.0) + jnp.log(1.0 + jnp.exp(-jnp.abs(x)))


def _rmsnorm(x, g):
    return x * lax.rsqrt(jnp.mean(x * x, axis=-1, keepdims=True) + NORM_EPS) * g


def _layernorm(x, g, b):
    mu = jnp.mean(x, axis=-1, keepdims=True)
    xc = x - mu
    var = jnp.mean(xc * xc, axis=-1, keepdims=True)
    return xc * lax.rsqrt(var + NORM_EPS) * g + b


def _pick_lane(row, lane):
    idx = lax.broadcasted_iota(jnp.int32, row.shape, 1)
    return jnp.sum(jnp.where(idx == lane, row, 0.0), axis=1, keepdims=True)


def _gdn_q(pq, cw):
    h = _silu(dwconv(pq, cw))
    return h * lax.rsqrt(jnp.sum(h * h, axis=-1, keepdims=True) + NORM_EPS) * (GDN_DK ** -0.5)


def _gdn_k(pk, cw):
    h = _silu(dwconv(pk, cw))
    return h * lax.rsqrt(jnp.sum(h * h, axis=-1, keepdims=True) + NORM_EPS)


def _gdn_v(pv, cw):
    return _silu(dwconv(pv, cw))


def _gdn_gate(ab, sc, head):
    a = _pick_lane(ab, head)
    b = _pick_lane(ab, GDN_HEADS + head)
    a_log = _pick_lane(_row(sc, 0), head)
    dt_bias = _pick_lane(_row(sc, 1), head)
    beta = jax.nn.sigmoid(b)
    g = -jnp.exp(a_log) * _softplus(a + dt_bias)
    return _two_lanes(g, beta)


def _two_lanes(c0, c1):
    lane = lax.broadcasted_iota(jnp.int32, (c0.shape[0], LANES), 1)
    return jnp.where(lane == 0, c0, jnp.where(lane == 1, c1, 0.0))


def _inverse_group(lows):
    n = lows[0].shape[0]
    ii = lax.broadcasted_iota(jnp.int32, (n, n), 0)
    jj = lax.broadcasted_iota(jnp.int32, (n, n), 1)
    eye = jnp.where(ii == jj, 1.0, 0.0)
    invs = [eye - low for low in lows]
    powers = [_dot3(low, low, 1, 0) for low in lows]
    k = 2
    while k < n:
        invs = [inv + _dot3(inv, p, 1, 0) for inv, p in zip(invs, powers)]
        k *= 2
        if k < n:
            powers = [_dot3(p, p, 1, 0) for p in powers]
    return invs


@jax.custom_vjp
def solve_group(lows, rhss):
    return [_dot3(inv, rhs, 1, 0) for inv, rhs in zip(_inverse_group(lows), rhss)]


def _solve_group_fwd(lows, rhss):
    invs = _inverse_group(lows)
    xs = [_dot3(inv, rhs, 1, 0) for inv, rhs in zip(invs, rhss)]
    return xs, (invs, xs)


def _solve_group_bwd(res, dxs):
    invs, xs = res
    n = invs[0].shape[0]
    strict = lax.broadcasted_iota(jnp.int32, (n, n), 0) > lax.broadcasted_iota(jnp.int32, (n, n), 1)
    drhss = [_dot3(inv, dx, 0, 0) for inv, dx in zip(invs, dxs)]
    dlows = [jnp.where(strict, -_dot3(drhs, x, 1, 1), 0.0) for drhs, x in zip(drhss, xs)]
    return dlows, drhss


solve_group.defvjp(_solve_group_fwd, _solve_group_bwd)


def _gdn_local_group(qs, ks, vs, gbs):
    c = qs[0].shape[0]
    ii = lax.broadcasted_iota(jnp.int32, (c, c), 0)
    jj = lax.broadcasted_iota(jnp.int32, (c, c), 1)
    incl = ii >= jj
    gs = [_pick_lane(gb, 0) for gb in gbs]
    betas = [_pick_lane(gb, 1) for gb in gbs]
    g_rows = [jnp.sum(jnp.where(ii == jj, g, 0.0), axis=0, keepdims=True) for g in gs]
    gc_cols = [jnp.sum(jnp.where(incl, g_row, 0.0), axis=1, keepdims=True) for g_row in g_rows]
    gc_rows = [jnp.sum(jnp.where(jj >= ii, g, 0.0), axis=0, keepdims=True) for g in gs]
    decays = [jnp.where(incl, jnp.exp(jnp.where(incl, gc - gr, 0.0)), 0.0) for gc, gr in zip(gc_cols, gc_rows)]
    kbs = [k * beta for k, beta in zip(ks, betas)]
    lows = [jnp.where(ii > jj, mm_nt(kb, k) * decay, 0.0) for kb, k, decay in zip(kbs, ks, decays)]
    egs = [jnp.exp(gc) for gc in gc_cols]
    wus = solve_group(lows, [jnp.concatenate([kb * eg, v * beta], axis=1) for kb, eg, v, beta in zip(kbs, egs, vs, betas)])
    qks = [mm_nt(q, k) * decay for q, k, decay in zip(qs, ks, decays)]
    g_lasts = [jnp.sum(g_row, axis=1, keepdims=True) for g_row in g_rows]
    kds = [k * jnp.exp(gl - gc) for k, gl, gc in zip(ks, g_lasts, gc_cols)]
    ws, us = [wu[:, :GDN_DK] for wu in wus], [wu[:, GDN_DK:] for wu in wus]
    q_effs = [q * eg - mm(qk, w) for q, eg, qk, w in zip(qs, egs, qks, ws)]
    o_locals = [mm(qk, u) for qk, u in zip(qks, us)]
    mixes = [mm_tn(kd, w) for kd, w in zip(kds, ws)]
    adds = [mm_tn(kd, u) for kd, u in zip(kds, us)]
    return [(q_eff, o_loc, mix, add, jnp.exp(gl))
            for q_eff, o_loc, mix, add, gl in zip(q_effs, o_locals, mixes, adds, g_lasts)]


def _gdn_state_step(q_eff, o_local, mix, add, eg_last, state):
    return mm(q_eff, state) + o_local, state * eg_last - mm(mix, state) + add


def _gdn_post(o, z, norm_g):
    return _rmsnorm(o, norm_g) * _silu(z)


def _attn_block(q, kn, kr, v, q0):
    s = (mm_nt(q[:, :128], kn) + mm_nt(q[:, 128:], kr)) * ATT_SCALE
    qpos = q0 + lax.broadcasted_iota(jnp.int32, s.shape, 0)
    kpos = lax.broadcasted_iota(jnp.int32, s.shape, 1)
    shift = int(math.log2(CHUNK))
    allowed = (kpos >> shift) <= (qpos >> shift)
    s = jnp.where(allowed, s, -1e30)
    p = jnp.exp(s - jnp.max(s, axis=-1, keepdims=True))
    p = p / jnp.sum(p, axis=-1, keepdims=True)
    return mm(p, v)


def _mla_prep(pm, qg, kvg, wq, wkv, cos, sin):
    cq = pm[:, 128:512]
    ckv = pm[:, 512:768]
    qf = mm(_rmsnorm(cq, qg), wq)
    parts = []
    for h in range(MLA_HEADS):
        parts.append(qf[:, h * 256:h * 256 + 128])
        parts.append(rope128(qf[:, h * 256 + 128:h * 256 + 256], cos, sin))
    kvf = mm(_rmsnorm(ckv, kvg), wkv)
    return jnp.concatenate(parts, axis=1), kvf, rope128(pm[:, 768:896], cos, sin)


def _ffn_act(ug, uu, wg, wu, bg, bu):
    return _silu(dwconv(ug, wg) + bg) * (dwconv(uu, wu) + bu)


def _head_loss(h1, ffn, gpre, pp, bgate, g2, b2, target):
    gate = jax.nn.sigmoid(gpre + bgate)
    h2 = _layernorm(ALPHA * h1 + ffn + gate * pp, g2, b2)
    err = h2 - target
    return 0.5 * jnp.sum(jnp.sum(err * err, axis=1, keepdims=True), axis=0, keepdims=True) / D_MODEL


ROW_TILE_VREGS = 32
CONV_HALO = 8


def _rows_per_tile(n_rows, cols):
    tile = min(n_rows, ROW_TILE_VREGS * 8 * LANES // cols)
    assert n_rows % tile == 0 and tile % CONV_HALO == 0, (n_rows, cols)
    return tile


def _tile_inputs(loads, t0, first, halo, tile):
    if halo == 0:
        return [ld(pl.ds(t0, tile)) for ld in loads]
    if first:
        xs = [ld(pl.ds(0, tile)) for ld in loads]
        return [jnp.concatenate([jnp.zeros((halo, x.shape[1]), x.dtype), x], axis=0) for x in xs]
    return [ld(pl.ds(pl.multiple_of(t0 - halo, CONV_HALO), tile + halo)) for ld in loads]


def _rows_apply(fn, loads, consts, store, n_rows, cols, halo):
    tile = _rows_per_tile(n_rows, cols)

    def one(t0, first):
        y = fn(*_tile_inputs(loads, t0, first, halo, tile), *consts)
        store(pl.ds(t0, tile), y[halo:] if halo else y)

    one(0, True)

    def step(i, carry):
        one(pl.multiple_of(i * tile, tile), False)
        return carry

    lax.fori_loop(1, n_rows // tile, step, 0)


def _rows_vjp(fn, loads, consts, load_dy, stores, n_rows, cols, halo):
    tile = _rows_per_tile(n_rows, cols)

    def one(t0, first, dconsts):
        xs = _tile_inputs(loads, t0, first, halo, tile)
        _, vjp = jax.vjp(lambda *a: fn(*a)[halo:] if halo else fn(*a), *xs, *consts)
        grads = vjp(load_dy(pl.ds(t0, tile)))
        for st, dx in zip(stores, grads[:len(xs)]):
            st(pl.ds(t0, tile), dx[halo:] if halo else dx, False)
            if halo and not first:
                st(pl.ds(pl.multiple_of(t0 - halo, CONV_HALO), halo), dx[:halo], True)
        return tuple(a + b for a, b in zip(dconsts, grads[len(xs):]))

    dconsts = one(0, True, tuple(jnp.zeros_like(c) for c in consts))
    return lax.fori_loop(1, n_rows // tile, lambda i, dc: one(pl.multiple_of(i * tile, tile), False, dc), dconsts)


def _params(sem):
    return pltpu.CompilerParams(dimension_semantics=sem, vmem_limit_bytes=VMEM_LIMIT)


def _matmul(a, b, *, name, ta=False, tb=False, tm=512, tn=512, tk=1024, add=None, add_scale=1.0,
            a_halves=False, b_halves=False, ride=None):
    assert not (a_halves and ta) and not (b_halves and tb)
    a_shape = (a.shape[1], 2 * a.shape[2]) if a_halves else a.shape
    b_shape = (b.shape[1], 2 * b.shape[2]) if b_halves else b.shape
    (k_dim, m) = a_shape if ta else a_shape[::-1]
    (n, k2) = b_shape if tb else b_shape[::-1]
    assert k_dim == k2, (a.shape, b.shape)
    tm, tn, tk = min(tm, m), min(tn, n), min(tk, k_dim)
    assert m % tm == 0 and n % tn == 0 and k_dim % tk == 0, (name, m, n, k_dim, tm, tn, tk)
    nk = k_dim // tk
    ca, cb = (0 if ta else 1), (1 if tb else 0)

    def body(*refs):
        if add is None:
            a_ref, b_ref, o_ref, acc = refs
        else:
            a_ref, b_ref, c_ref, o_ref, acc = refs
        kk = pl.program_id(2)

        @pl.when(kk == 0)
        def _():
            acc[...] = jnp.zeros_like(acc)

        acc[...] += _dot(a_ref[...], b_ref[...], ca, cb)

        @pl.when(kk == nk - 1)
        def _():
            r = acc[...]
            if add is not None:
                r = r + add_scale * c_ref[...]
            o_ref[...] = r

    a_spec = pl.BlockSpec((tk, tm), lambda i, j, k: (k, i)) if ta else pl.BlockSpec((tm, tk), lambda i, j, k: (i, k))
    b_spec = pl.BlockSpec((tn, tk), lambda i, j, k: (j, k)) if tb else pl.BlockSpec((tk, tn), lambda i, j, k: (k, j))
    if a_halves:
        kh = k_dim // 2 // tk
        assert kh * tk * 2 == k_dim
        a_spec = pl.BlockSpec((None, tm, tk), lambda i, j, k: (k // kh, i, k % kh))
    if b_halves:
        nh = n // 2 // tn
        assert nh * tn * 2 == n
        b_spec = pl.BlockSpec((None, tk, tn), lambda i, j, k: (j // nh, k, j % nh))
    in_specs = [a_spec, b_spec]
    args = [a, b]
    if add is not None:
        in_specs.append(pl.BlockSpec((tm, tn), lambda i, j, k: (i, j)))
        args.append(add)
    grid = (m // tm, n // tn, nk)
    r_in, r_out, r_shapes, r_sems, r_args = _ride_specs(ride)
    outs = pl.pallas_call(
        _riding(body, len(args), 1, 1, ride, grid), name=name, grid=grid,
        in_specs=in_specs + r_in, out_specs=[pl.BlockSpec((tm, tn), lambda i, j, k: (i, j))] + r_out,
        out_shape=[jax.ShapeDtypeStruct((m, n), F32)] + r_shapes,
        scratch_shapes=[pltpu.VMEM((tm, tn), F32)] + r_sems,
        compiler_params=_params(("parallel", "parallel", "arbitrary") if ride is None else ("arbitrary",) * 3),
    )(*args, *r_args)
    return outs[0] if ride is None else (outs[0], list(outs[1:]))


def _riding(core, n_in, n_out, n_scratch, ride, grid):
    if ride is None:
        return core
    copies, nr = ride[1], len(ride[0])
    steps = int(np.prod(grid))
    assert steps >= 3, grid

    def body(*refs):
        cuts = np.cumsum([0, n_in, nr, n_out, nr, n_scratch])
        ins, rin, outs, rout, scratch = (refs[a:b] for a, b in zip(cuts[:-1], cuts[1:]))
        sems = refs[cuts[-1]:]
        step = 0
        for axis, size in enumerate(grid):
            step = step * size + pl.program_id(axis)

        @pl.when(step == 0)
        def _():
            _exchange_start(copies(rin, rout, sems))

        @pl.when(step == steps - 2)
        def _():
            _exchange_pass_on(copies(rin, rout, sems))

        core(*ins, *outs, *scratch)

        @pl.when(step == steps - 1)
        def _():
            _exchange_finish(copies(rin, rout, sems))

    return body


def _ride_specs(ride):
    if ride is None:
        return [], [], [], [], []
    arrays, _, sems, shapes = ride
    return [HBM_REF] * len(arrays), [HBM_REF] * len(arrays), list(shapes), sems(len(arrays)), list(arrays)


def _gdn_fwd(proj, conv_w, sc, norm_g, bl, s, ride=None):
    nc = s // CHUNK

    def core(ph_ref, ab_ref, cwq_ref, cwk_ref, cwv_ref, sc_ref, ng_ref, cat_ref, o_ref, st_ref, q_s, k_s, v_s, gb_s):
        def into(ref):
            def store(rows, value):
                ref[rows, :] = value.astype(ref.dtype)
            return store

        for fn, col, cw_ref, val_s in [(_gdn_q, 0, cwq_ref, q_s), (_gdn_k, 128, cwk_ref, k_s), (_gdn_v, 256, cwv_ref, v_s)]:
            _rows_apply(fn, [lambda r, col=col: ph_ref[r, col:col + 128]], [cw_ref[...]], into(val_s), s, LANES, CONV_HALO)
        _rows_apply(functools.partial(_gdn_gate, head=pl.program_id(1)), [lambda r: ab_ref[r, :]], [sc_ref[...]], into(gb_s), s, LANES, 0)

        group = math.gcd(nc, GDN_FWD_GROUP)

        def chunks(i, state):
            ns = [i * group + j for j in range(group)]
            rows = [pl.ds(pl.multiple_of(n * CHUNK, CHUNK), CHUNK) for n in ns]
            local = _gdn_local_group([q_s[r, :] for r in rows], [k_s[r, :] for r in rows], [v_s[r, :] for r in rows],
                                     [gb_s[r, :] for r in rows])
            for n, r, loc in zip(ns, rows, local):
                st_ref[n] = state
                o_ref[r, :], state = _gdn_state_step(*loc, state)
            return state

        lax.fori_loop(0, nc // group, chunks, jnp.zeros((GDN_DK, GDN_DV), F32))
        _rows_apply(_gdn_post, [lambda r: o_ref[r, :], lambda r: ph_ref[r, 384:512]], [ng_ref[...]], into(cat_ref), s, LANES, 0)

    t = bl * s
    r_in, r_out, r_shapes, r_sems, r_args = _ride_specs(ride)
    outs = pl.pallas_call(
        _riding(core, 7, 7, 0, ride, (bl, GDN_HEADS)), name="gdn_fwd", grid=(bl, GDN_HEADS),
        in_specs=[
            pl.BlockSpec((s, 512), lambda b, h: (b, h)),
            pl.BlockSpec((s, 128), lambda b, h: (b, P_MLA // 128)),
            pl.BlockSpec((GDN_CONV, 128), lambda b, h: (0, h)),
            pl.BlockSpec((GDN_CONV, 128), lambda b, h: (0, GDN_HEADS + h)),
            pl.BlockSpec((GDN_CONV, 128), lambda b, h: (0, 2 * GDN_HEADS + h)),
            pl.BlockSpec((8, 128), lambda b, h: (0, 0)),
            pl.BlockSpec((1, 128), lambda b, h: (0, 0)),
        ] + r_in,
        out_specs=[
            pl.BlockSpec((s, 128), lambda b, h: (b, h)),
            pl.BlockSpec((s, 128), lambda b, h: (b, h)),
            pl.BlockSpec((None, None, nc, GDN_DK, GDN_DV), lambda b, h: (b, h, 0, 0, 0)),
        ] + [pl.BlockSpec((s, 128), lambda b, h: (b, h))] * 4 + r_out,
        out_shape=[
            jax.ShapeDtypeStruct((t, 2 * GDN_VW), BF16),
            jax.ShapeDtypeStruct((t, GDN_VW), F32),
            jax.ShapeDtypeStruct((bl, GDN_HEADS, nc, GDN_DK, GDN_DV), F32),
        ] + [jax.ShapeDtypeStruct((t, GDN_VW), F32)] * 4 + r_shapes,
        scratch_shapes=r_sems,
        compiler_params=_params(("arbitrary", "arbitrary")),
    )(proj, proj, conv_w, conv_w, conv_w, sc, norm_g, *r_args)
    return outs[0], outs[1], outs[2], tuple(outs[3:7]), list(outs[7:])


def _gdn_bwd(proj, conv_w, sc, norm_g, o_raw, states, qkvg, dcat, bl, s, ride=None):
    nc = s // CHUNK

    def core(ph_ref, ab_ref, cwq_ref, cwk_ref, cwv_ref, sc_ref, ng_ref, o_ref, st_ref, dc_ref, q_in, k_in, v_in, gb_in,
             dph_ref, dab_ref, dcwq_ref, dcwk_ref, dcwv_ref, dsc_ref, dng_ref, q_s, k_s, v_s, gb_s, do_s):
        head = pl.program_id(1)
        gate = functools.partial(_gdn_gate, head=head)
        paths = [(_gdn_q, 0, cwq_ref, q_s, dcwq_ref), (_gdn_k, 128, cwk_ref, k_s, dcwk_ref), (_gdn_v, 256, cwv_ref, v_s, dcwv_ref)]
        def into(ref, cols=slice(None)):
            def store(rows, value, add=False):
                if add:
                    ref[rows, cols] += value.astype(ref.dtype)
                else:
                    ref[rows, cols] = value.astype(ref.dtype)
            return store

        (dng,) = _rows_vjp(_gdn_post, [lambda r: o_ref[r, :], lambda r: ph_ref[r, 384:512]], [ng_ref[...]],
                           lambda r: dc_ref[r, :], [into(do_s), into(dph_ref, slice(384, 512))], s, LANES, 0)
        dng_ref[...] = jnp.broadcast_to(dng, dng_ref.shape)

        group = math.gcd(nc, GDN_BWD_GROUP)

        def chunks(i, dstate):
            ns = [nc - 1 - (i * group + j) for j in range(group)]
            rows = [pl.ds(pl.multiple_of(n * CHUNK, CHUNK), CHUNK) for n in ns]
            local, local_vjp = jax.vjp(_gdn_local_group, [q_in[r, :] for r in rows], [k_in[r, :] for r in rows],
                                       [v_in[r, :] for r in rows], [gb_in[r, :] for r in rows])
            d_os = [do_s[r, :] for r in rows]
            dlocal = []
            for n, loc, d_o in zip(ns, local, d_os):
                _, step_vjp = jax.vjp(_gdn_state_step, *loc, st_ref[n])
                *dloc, dstate = step_vjp((d_o, dstate))
                dlocal.append(tuple(dloc))
            dqs, dks, dvs, dgbs = local_vjp(dlocal)
            for r, dq, dk, dv, dgb in zip(rows, dqs, dks, dvs, dgbs):
                q_s[r, :], k_s[r, :], v_s[r, :], gb_s[r, :] = dq, dk, dv, dgb
            return dstate

        lax.fori_loop(0, nc // group, chunks, jnp.zeros((GDN_DK, GDN_DV), F32))
        for fn, col, cw_ref, val_s, dcw_ref in paths:
            (dcw_ref[...],) = _rows_vjp(fn, [lambda r, col=col: ph_ref[r, col:col + 128]], [cw_ref[...]],
                                        lambda r, val_s=val_s: val_s[r, :], [into(val_s)], s, LANES, CONV_HALO)
            dph_ref[:, col:col + 128] = val_s[...].astype(BF16)

        @pl.when(head == 0)
        def _():
            dab_ref[...] = jnp.zeros_like(dab_ref)

        def add_dab(rows, value, add=False):
            dab_ref[rows, :] += value

        (dsc_ref[...],) = _rows_vjp(gate, [lambda r: ab_ref[r, :]], [sc_ref[...]], lambda r: gb_s[r, :], [add_dab], s, LANES, 0)

    t = bl * s
    cw_out = pl.BlockSpec((None, GDN_CONV, 128), lambda b, h: (b, 0, h))
    part = pl.BlockSpec((None, None, 8, 128), lambda b, h: (b, h, 0, 0))
    r_in, r_out, r_shapes, r_sems, r_args = _ride_specs(ride)
    outs = pl.pallas_call(
        _riding(core, 14, 7, 5, ride, (bl, GDN_HEADS)), name="gdn_bwd", grid=(bl, GDN_HEADS),
        in_specs=[
            pl.BlockSpec((s, 512), lambda b, h: (b, h)),
            pl.BlockSpec((s, 128), lambda b, h: (b, P_MLA // 128)),
            pl.BlockSpec((GDN_CONV, 128), lambda b, h: (0, h)),
            pl.BlockSpec((GDN_CONV, 128), lambda b, h: (0, GDN_HEADS + h)),
            pl.BlockSpec((GDN_CONV, 128), lambda b, h: (0, 2 * GDN_HEADS + h)),
            pl.BlockSpec((8, 128), lambda b, h: (0, 0)),
            pl.BlockSpec((1, 128), lambda b, h: (0, 0)),
            pl.BlockSpec((s, 128), lambda b, h: (b, h)),
            pl.BlockSpec((None, None, nc, GDN_DK, GDN_DV), lambda b, h: (b, h, 0, 0, 0)),
        ] + [pl.BlockSpec((s, 128), lambda b, h: (b, h))] * 5 + r_in,
        out_specs=[
            pl.BlockSpec((s, 512), lambda b, h: (b, h)),
            pl.BlockSpec((s, 128), lambda b, h: (b, 0)),
            cw_out, cw_out, cw_out, part, part,
        ] + r_out,
        out_shape=[
            jax.ShapeDtypeStruct((t, P_WIDTH), BF16),
            jax.ShapeDtypeStruct((t, 128), F32),
            jax.ShapeDtypeStruct((bl, GDN_CONV, 512), F32),
            jax.ShapeDtypeStruct((bl, GDN_CONV, 512), F32),
            jax.ShapeDtypeStruct((bl, GDN_CONV, 512), F32),
            jax.ShapeDtypeStruct((bl, GDN_HEADS, 8, 128), F32),
            jax.ShapeDtypeStruct((bl, GDN_HEADS, 8, 128), F32),
        ] + r_shapes,
        scratch_shapes=[pltpu.VMEM((s, 128), F32)] * 5 + r_sems,
        compiler_params=_params(("arbitrary", "arbitrary")),
    )(proj, proj, conv_w, conv_w, conv_w, sc, norm_g, o_raw, states, dcat, *qkvg, *r_args)
    return tuple(outs[:7]) + (list(outs[7:]),)


def _mla_prep_fwd(proj, qg, kvg, wq, wkv, cos, sin, s, tm):
    t = proj.shape[0]
    tm = min(tm, s)
    nps = s // tm
    const = lambda shape: pl.BlockSpec(shape, lambda i: (0, 0))

    def body(pm_ref, qg_ref, kvg_ref, wq_ref, wkv_ref, cos_ref, sin_ref, qf_ref, kvf_ref, kr_ref):
        qf, kvf, kr = _mla_prep(pm_ref[...], qg_ref[...], kvg_ref[...], wq_ref[...], wkv_ref[...], cos_ref[...], sin_ref[...])
        qf_ref[...], kvf_ref[...], kr_ref[...] = qf.astype(BF16), kvf.astype(BF16), kr.astype(BF16)

    return pl.pallas_call(
        body, name="mla_prep_fwd", grid=(t // tm,),
        in_specs=[
            pl.BlockSpec((tm, 1024), lambda i: (i, P_MLA // 1024)),
            const((1, MLA_Q_LORA)), const((1, MLA_KV_LORA)), const(wq.shape), const(wkv.shape),
            pl.BlockSpec((tm, 128), lambda i: (i % nps, 0)), pl.BlockSpec((tm, 128), lambda i: (i % nps, 0)),
        ],
        out_specs=[pl.BlockSpec((tm, 1024), lambda i: (i, 0)), pl.BlockSpec((tm, 1024), lambda i: (i, 0)),
                   pl.BlockSpec((tm, 128), lambda i: (i, 0))],
        out_shape=[jax.ShapeDtypeStruct((t, 1024), BF16), jax.ShapeDtypeStruct((t, 1024), BF16),
                   jax.ShapeDtypeStruct((t, 128), BF16)],
        compiler_params=_params(("parallel",)),
    )(proj, qg, kvg, wq, wkv, cos, sin)


def _mla_prep_bwd(proj, qg, kvg, wq, wkv, cos, sin, dqf, dkvf, dkr, dab, dproj, s, tm):
    t = proj.shape[0]
    tm = min(tm, s)
    nps = s // tm
    const = lambda shape: pl.BlockSpec(shape, lambda i: (0, 0))

    def body(pm_ref, qg_ref, kvg_ref, wq_ref, wkv_ref, cos_ref, sin_ref, dqf_ref, dkvf_ref, dkr_ref, dab_ref, dp_in,
             dp_ref, dqg_ref, dkvg_ref, dwq_ref, dwkv_ref):
        del dp_in
        fn = lambda pm, qg_, kvg_, wq_, wkv_: _mla_prep(pm, qg_, kvg_, wq_, wkv_, cos_ref[...], sin_ref[...])
        _, vjp = jax.vjp(fn, pm_ref[...], qg_ref[...], kvg_ref[...], wq_ref[...].astype(F32), wkv_ref[...].astype(F32))
        dpm, dqg, dkvg, dwq, dwkv = vjp((dqf_ref[...], dkvf_ref[...], dkr_ref[...]))
        dp_ref[...] = jnp.concatenate([dab_ref[...], dpm[:, 128:]], axis=1).astype(BF16)

        @pl.when(pl.program_id(0) == 0)
        def _():
            dqg_ref[...] = jnp.zeros_like(dqg_ref)
            dkvg_ref[...] = jnp.zeros_like(dkvg_ref)
            dwq_ref[...] = jnp.zeros_like(dwq_ref)
            dwkv_ref[...] = jnp.zeros_like(dwkv_ref)

        dqg_ref[...] += dqg
        dkvg_ref[...] += dkvg
        dwq_ref[...] += dwq
        dwkv_ref[...] += dwkv

    rows = lambda w: pl.BlockSpec((tm, w), lambda i: (i, 0))
    return pl.pallas_call(
        body, name="mla_prep_bwd", grid=(t // tm,),
        in_specs=[
            pl.BlockSpec((tm, 1024), lambda i: (i, P_MLA // 1024)),
            const((1, MLA_Q_LORA)), const((1, MLA_KV_LORA)), const(wq.shape), const(wkv.shape),
            pl.BlockSpec((tm, 128), lambda i: (i % nps, 0)), pl.BlockSpec((tm, 128), lambda i: (i % nps, 0)),
            rows(1024), rows(1024), rows(128), rows(128),
            pl.BlockSpec(memory_space=pl.ANY),
        ],
        out_specs=[pl.BlockSpec((tm, 1024), lambda i: (i, P_MLA // 1024)),
                   const((1, MLA_Q_LORA)), const((1, MLA_KV_LORA)), const(wq.shape), const(wkv.shape)],
        out_shape=[jax.ShapeDtypeStruct(dproj.shape, dproj.dtype),
                   jax.ShapeDtypeStruct((1, MLA_Q_LORA), F32), jax.ShapeDtypeStruct((1, MLA_KV_LORA), F32),
                   jax.ShapeDtypeStruct(wq.shape, F32), jax.ShapeDtypeStruct(wkv.shape, F32)],
        input_output_aliases={11: 0},
        compiler_params=_params(("arbitrary",)),
    )(proj, qg, kvg, wq, wkv, cos, sin, dqf, dkvf, dkr, dab, dproj)


def _attn_fwd(qf, kvf, kr, cat, bl, s, tq):
    tq = min(tq, s)
    nq = s // tq

    def body(q_ref, kv_ref, kr_ref, cat_in, o_ref):
        del cat_in
        for i in range(nq):
            rows, keys = slice(i * tq, (i + 1) * tq), slice(0, (i + 1) * tq)
            o = _attn_block(q_ref[rows, :], kv_ref[keys, 0:128], kr_ref[keys, :], kv_ref[keys, 128:256], i * tq)
            o_ref[rows, :] = o.astype(o_ref.dtype)

    return pl.pallas_call(
        body, name="attn_fwd", grid=(bl, MLA_HEADS),
        in_specs=[
            pl.BlockSpec((s, 256), lambda b, h: (b, h)),
            pl.BlockSpec((s, 256), lambda b, h: (b, h)),
            pl.BlockSpec((s, 128), lambda b, h: (b, 0)),
            pl.BlockSpec(memory_space=pl.ANY),
        ],
        out_specs=pl.BlockSpec((s, 128), lambda b, h: (b, GDN_HEADS + h)),
        out_shape=jax.ShapeDtypeStruct(cat.shape, cat.dtype),
        input_output_aliases={3: 0},
        compiler_params=_params(("parallel", "parallel")),
    )(qf, kvf, kr, cat)


def _attn_bwd(qf, kvf, kr, dcat, bl, s, tq):
    tq = min(tq, s)
    nq = s // tq

    def body(q_ref, kv_ref, kr_ref, do_ref, dq_ref, dkv_ref, dkr_ref):
        dkv_ref[...] = jnp.zeros_like(dkv_ref)

        @pl.when(pl.program_id(1) == 0)
        def _():
            dkr_ref[...] = jnp.zeros_like(dkr_ref)

        for i in range(nq):
            rows, keys = slice(i * tq, (i + 1) * tq), slice(0, (i + 1) * tq)
            fn = functools.partial(_attn_block, q0=i * tq)
            f32 = lambda a: a.astype(F32)
            _, vjp = jax.vjp(fn, f32(q_ref[rows, :]), f32(kv_ref[keys, 0:128]), f32(kr_ref[keys, :]), f32(kv_ref[keys, 128:256]))
            dq_ref[rows, :], dkn, dkr, dv = vjp(do_ref[rows, :])
            dkv_ref[keys, 0:128] += dkn
            dkv_ref[keys, 128:256] += dv
            dkr_ref[keys, :] += dkr

    t = bl * s
    return pl.pallas_call(
        body, name="attn_bwd", grid=(bl, MLA_HEADS),
        in_specs=[
            pl.BlockSpec((s, 256), lambda b, h: (b, h)),
            pl.BlockSpec((s, 256), lambda b, h: (b, h)),
            pl.BlockSpec((s, 128), lambda b, h: (b, 0)),
            pl.BlockSpec((s, 128), lambda b, h: (b, GDN_HEADS + h)),
        ],
        out_specs=[
            pl.BlockSpec((s, 256), lambda b, h: (b, h)),
            pl.BlockSpec((s, 256), lambda b, h: (b, h)),
            pl.BlockSpec((s, 128), lambda b, h: (b, 0)),
        ],
        out_shape=[jax.ShapeDtypeStruct((t, 1024), F32), jax.ShapeDtypeStruct((t, 1024), F32),
                   jax.ShapeDtypeStruct((t, 128), F32)],
        compiler_params=_params(("parallel", "arbitrary")),
    )(qf, kvf, kr, dcat)


def _ln1_fwd(x, mix, g, b, tm):
    t = x.shape[0]
    tm = min(tm, t)

    def body(x_ref, mix_ref, g_ref, b_ref, r_ref, h_ref, hb_ref, xb_ref):
        r = ALPHA * x_ref[...] + mix_ref[...]
        r_ref[...] = r
        h = _layernorm(r, g_ref[...], b_ref[...])
        h_ref[...] = h
        hb_ref[...] = h.astype(BF16)
        xb_ref[...] = x_ref[...].astype(BF16)

    rows = pl.BlockSpec((tm, D_MODEL), lambda i: (i, 0))
    vec = pl.BlockSpec((1, D_MODEL), lambda i: (0, 0))
    return pl.pallas_call(
        body, name="ln1_fwd", grid=(t // tm,), in_specs=[rows, rows, vec, vec], out_specs=[rows] * 4,
        out_shape=[jax.ShapeDtypeStruct(x.shape, F32)] * 2 + [jax.ShapeDtypeStruct(x.shape, BF16)] * 2,
        compiler_params=_params(("parallel",)),
    )(x, mix, g, b)


def _ln1_bwd(r1, dr2, da, db_, g, b, tm):
    t = r1.shape[0]
    tm = min(tm, t)

    def body(r_ref, d2_ref, da_ref, db_ref, g_ref, b_ref, dr_ref, drb_ref, dg_ref, dbias_ref):
        dh = ALPHA * d2_ref[...] + da_ref[...] + db_ref[...]
        _, vjp = jax.vjp(_layernorm, r_ref[...], g_ref[...], b_ref[...])
        dr, dg, dbias = vjp(dh)
        dr_ref[...] = dr
        drb_ref[...] = dr.astype(BF16)

        @pl.when(pl.program_id(0) == 0)
        def _():
            dg_ref[...] = jnp.zeros_like(dg_ref)
            dbias_ref[...] = jnp.zeros_like(dbias_ref)

        dg_ref[...] += dg
        dbias_ref[...] += dbias

    rows = pl.BlockSpec((tm, D_MODEL), lambda i: (i, 0))
    vec = pl.BlockSpec((1, D_MODEL), lambda i: (0, 0))
    return pl.pallas_call(
        body, name="ln1_bwd", grid=(t // tm,), in_specs=[rows] * 4 + [vec, vec], out_specs=[rows, rows, vec, vec],
        out_shape=[jax.ShapeDtypeStruct(r1.shape, F32), jax.ShapeDtypeStruct(r1.shape, BF16)]
        + [jax.ShapeDtypeStruct((1, D_MODEL), F32)] * 2,
        compiler_params=_params(("arbitrary",)),
    )(r1, dr2, da, db_, g, b)


def _ffn_act_fwd(u, conv_w, conv_b, bl, s, cb):
    nj = D_FF // cb

    def body(ug_ref, uu_ref, wg_ref, wu_ref, bg_ref, bu_ref, act_ref):
        def store(rows, act):
            act_ref[rows, :] = act.astype(BF16)

        _rows_apply(_ffn_act, [lambda r: ug_ref[r, :], lambda r: uu_ref[r, :]],
                    [wg_ref[...], wu_ref[...], bg_ref[...], bu_ref[...]], store, s, cb, CONV_HALO)

    return pl.pallas_call(
        body, name="ffn_act_fwd", grid=(bl, nj),
        in_specs=[
            pl.BlockSpec((s, cb), lambda b, j: (b, j)), pl.BlockSpec((s, cb), lambda b, j: (b, nj + j)),
            pl.BlockSpec((FFN_CONV, cb), lambda b, j: (0, j)), pl.BlockSpec((FFN_CONV, cb), lambda b, j: (0, nj + j)),
            pl.BlockSpec((1, cb), lambda b, j: (0, j)), pl.BlockSpec((1, cb), lambda b, j: (0, nj + j)),
        ],
        out_specs=pl.BlockSpec((s, cb), lambda b, j: (b, j)),
        out_shape=jax.ShapeDtypeStruct((bl * s, D_FF), BF16),
        compiler_params=_params(("parallel", "parallel")),
    )(u, u, conv_w, conv_w, conv_b, conv_b)


def _ffn_act_bwd(u, conv_w, conv_b, dact, bl, s, cb):
    nj = D_FF // cb

    def body(ug_ref, uu_ref, wg_ref, wu_ref, bg_ref, bu_ref, da_ref, du_ref, dwg_ref, dwu_ref, dbg_ref, dbu_ref, acc):
        def store_into(half):
            def store(rows, value, add):
                if add:
                    acc[half, rows, :] += value
                else:
                    acc[half, rows, :] = value
            return store

        dwg_ref[...], dwu_ref[...], dbg_ref[...], dbu_ref[...] = _rows_vjp(
            _ffn_act, [lambda r: ug_ref[r, :], lambda r: uu_ref[r, :]], [wg_ref[...], wu_ref[...], bg_ref[...], bu_ref[...]],
            lambda r: da_ref[r, :], [store_into(0), store_into(1)], s, cb, CONV_HALO)
        du_ref[...] = acc[...].astype(BF16)

    t = bl * s
    blk = pl.BlockSpec((s, cb), lambda b, j: (b, j))
    wpart = pl.BlockSpec((None, FFN_CONV, cb), lambda b, j: (b, 0, j))
    bpart = pl.BlockSpec((None, 1, cb), lambda b, j: (b, 0, j))
    return pl.pallas_call(
        body, name="ffn_act_bwd", grid=(bl, nj),
        in_specs=[
            blk, pl.BlockSpec((s, cb), lambda b, j: (b, nj + j)),
            pl.BlockSpec((FFN_CONV, cb), lambda b, j: (0, j)), pl.BlockSpec((FFN_CONV, cb), lambda b, j: (0, nj + j)),
            pl.BlockSpec((1, cb), lambda b, j: (0, j)), pl.BlockSpec((1, cb), lambda b, j: (0, nj + j)),
            blk,
        ],
        out_specs=[pl.BlockSpec((2, s, cb), lambda b, j: (0, b, j)), wpart, wpart, bpart, bpart],
        out_shape=[jax.ShapeDtypeStruct((2, t, D_FF), BF16)] + [jax.ShapeDtypeStruct((bl, FFN_CONV, D_FF), F32)] * 2
        + [jax.ShapeDtypeStruct((bl, 1, D_FF), F32)] * 2,
        scratch_shapes=[pltpu.VMEM((2, s, cb), F32)],
        compiler_params=_params(("parallel", "parallel")),
    )(u, u, conv_w, conv_w, conv_b, conv_b, dact)


def _head(h1, ffn, gpre, pp, bgate, g2, b2, target, tm):
    t = h1.shape[0]
    tm = min(tm, t)

    def body(h1_ref, ffn_ref, gp_ref, pp_ref, bg_ref, g2_ref, b2_ref, tg_ref,
             dr_ref, drb_ref, dgp_ref, dpp_ref, loss_ref, dbg_ref, dg2_ref, db2_ref):
        fn = functools.partial(_head_loss, target=tg_ref[...])
        loss, vjp = jax.vjp(fn, h1_ref[...], ffn_ref[...], gp_ref[...], pp_ref[...], bg_ref[...], g2_ref[...], b2_ref[...])
        _, dffn, dgp, dpp, dbg, dg2, db2 = vjp(jnp.ones((1, 1), F32))
        dr_ref[...] = dffn
        drb_ref[...], dgp_ref[...], dpp_ref[...] = dffn.astype(BF16), dgp.astype(BF16), dpp.astype(BF16)

        @pl.when(pl.program_id(0) == 0)
        def _():
            loss_ref[...] = jnp.zeros_like(loss_ref)
            dbg_ref[...] = jnp.zeros_like(dbg_ref)
            dg2_ref[...] = jnp.zeros_like(dg2_ref)
            db2_ref[...] = jnp.zeros_like(db2_ref)

        loss_ref[...] += jnp.broadcast_to(loss, loss_ref.shape)
        dbg_ref[...] += dbg
        dg2_ref[...] += dg2
        db2_ref[...] += db2

    rows = pl.BlockSpec((tm, D_MODEL), lambda i: (i, 0))
    vec = pl.BlockSpec((1, D_MODEL), lambda i: (0, 0))
    return pl.pallas_call(
        body, name="head", grid=(t // tm,), in_specs=[rows] * 4 + [vec] * 3 + [rows],
        out_specs=[rows] * 4 + [pl.BlockSpec((8, 128), lambda i: (0, 0))] + [vec] * 3,
        out_shape=[jax.ShapeDtypeStruct(h1.shape, F32)] + [jax.ShapeDtypeStruct(h1.shape, BF16)] * 3
        + [jax.ShapeDtypeStruct((8, 128), F32)]
        + [jax.ShapeDtypeStruct((1, D_MODEL), F32)] * 3,
        compiler_params=_params(("arbitrary",)),
    )(h1, ffn, gpre, pp, bgate, g2, b2, target)


def _adam_update(g, w_ref, m_ref, v_ref, g_ref, d_ref, nm_ref, nv_ref):
    m2 = ADAM_B1 * m_ref[...] + (1.0 - ADAM_B1) * g
    v2 = ADAM_B2 * v_ref[...] + (1.0 - ADAM_B2) * jnp.square(g)
    m_hat = m2 / (1.0 - ADAM_B1 ** ADAM_STEP)
    v_hat = v2 / (1.0 - ADAM_B2 ** ADAM_STEP)
    g_ref[...] = g
    d_ref[...] = -ADAM_LR * (m_hat / (jnp.sqrt(v_hat) + ADAM_EPS) + ADAM_WD * w_ref[...])
    nm_ref[...] = m2
    nv_ref[...] = v2


def _row_tile(rows, cols, limit_bytes=256 * 1024):
    best = None
    for t in range(8, rows + 1, 8):
        if rows % t == 0 and t * cols * 4 <= limit_bytes:
            best = t
    return best or rows


def _adamw_reduced(own, recv, w, m, v, name):
    a, b = w.shape
    ta = _row_tile(a, b)

    def body(own_ref, recv_ref, w_ref, m_ref, v_ref, g_ref, d_ref, nm_ref, nv_ref):
        c = lax.axis_index("c")
        for core in range(2):
            @pl.when(c == core)
            def _():
                got = [recv_ref[k].astype(F32) for k in range(N_DEV - 1)]
                same = [own_ref[...].astype(F32), got[0], got[1], got[2]]
                other = got[3:]
                core0, core1 = (same, other) if core == 0 else (other, same)
                g = core0[0] + core1[0]
                for r in range(1, N_CHIPS):
                    g = (g + core0[r]) + core1[r]
                _adam_update(g, w_ref, m_ref, v_ref, g_ref, d_ref, nm_ref, nv_ref)

    blk = pl.BlockSpec((ta, b), lambda i: (i, 0))
    return pl.pallas_call(
        body, name=name, grid=(a // ta,),
        in_specs=[blk, pl.BlockSpec((7, ta, b), lambda i: (0, i, 0)), blk, blk, blk], out_specs=[blk] * 4,
        out_shape=[jax.ShapeDtypeStruct(w.shape, F32)] * 4, compiler_params=_params(("parallel",)),
    )(own, recv, w, m, v)


def _adamw_small(g, w, m, v):
    def body(g_in, w_ref, m_ref, v_ref, g_ref, d_ref, nm_ref, nv_ref):
        _adam_update(g_in[...], w_ref, m_ref, v_ref, g_ref, d_ref, nm_ref, nv_ref)

    blk = pl.BlockSpec(w.shape, lambda i: (0, 0))
    return pl.pallas_call(
        body, name="adamw_small", grid=(1,), in_specs=[blk] * 4, out_specs=[blk] * 4,
        out_shape=[jax.ShapeDtypeStruct(w.shape, F32)] * 4, compiler_params=_params(("arbitrary",)),
    )(g, w, m, v)


def _remote(src, dst, send_sem, recv_sem, device):
    return pltpu.make_async_remote_copy(src_ref=src, dst_ref=dst, send_sem=send_sem, recv_sem=recv_sem,
                                        device_id=device, device_id_type=MESH)


def _place():
    x, y, c = lax.axis_index("x"), lax.axis_index("y"), lax.axis_index("c")
    return x, y, c, 2 * x + y, [(1 - x, y), (x, 1 - y), (1 - x, 1 - y)]


HBM_REF = pl.BlockSpec(memory_space=pl.ANY)
HALF_ROWS_QUANTUM = 16


def _gather_sems(n):
    return [pltpu.SemaphoreType.DMA((3 * n,))] * 4 + [pltpu.SemaphoreType.DMA((n,))]


def _gather_copies(ins, outs, sems):
    send_s, recv_s, fsend_s, frecv_s, local_s = sems
    x, y, c, me, chips = _place()
    local, sends, steps = [], [], []
    for i, (src, dst) in enumerate(zip(ins, outs)):
        local.append(pltpu.make_async_copy(src, dst.at[me], local_s.at[i]))
        half = src.shape[0] // 2
        split = src.shape[0] % (2 * HALF_ROWS_QUANTUM) == 0
        if split:
            mine = pl.ds(pl.multiple_of(c * half, HALF_ROWS_QUANTUM), half)
            theirs = pl.ds(pl.multiple_of((1 - c) * half, HALF_ROWS_QUANTUM), half)
        for r, (px, py) in enumerate(chips):
            k, peer = 3 * i + r, 2 * px + py
            if split:
                sends.append(_remote(src.at[mine], dst.at[me, mine], send_s.at[k], recv_s.at[k], (px, py, c)))
                landed = dst.at[peer, mine]
                steps.append((_remote(src.at[mine], landed, send_s.at[k], recv_s.at[k], (px, py, c)),
                              _remote(landed, landed, fsend_s.at[k], frecv_s.at[k], (x, y, 1 - c)),
                              _remote(dst.at[peer, theirs], dst.at[peer, theirs], fsend_s.at[k], frecv_s.at[k], (x, y, 1 - c))))
            else:
                sends.append(_remote(src, dst.at[me], send_s.at[k], recv_s.at[k], (px, py, c)))
                steps.append((_remote(src, dst.at[peer], send_s.at[k], recv_s.at[k], (px, py, c)), None, None))
    return local, sends, steps


def _scatter_sems(n):
    return [pltpu.SemaphoreType.DMA((4 * n,))] * 2 + [pltpu.SemaphoreType.DMA((3 * n,))] * 2


def _scatter_copies(ins, outs, sems):
    send_s, recv_s, fsend_s, frecv_s = sems
    x, y, c, me, chips = _place()
    sends, steps = [], []
    for i, (src, dst) in enumerate(zip(ins, outs)):
        for r, (px, py) in enumerate(chips):
            k = 4 * i + r
            cp = _remote(src.at[2 * px + py], dst.at[r], send_s.at[k], recv_s.at[k], (px, py, c))
            fwd = _remote(dst.at[r], dst.at[4 + r], fsend_s.at[3 * i + r], frecv_s.at[3 * i + r], (x, y, 1 - c))
            sends.append(cp)
            steps.append((cp, fwd, fwd))
        k = 4 * i + 3
        cp = _remote(src.at[me], dst.at[3], send_s.at[k], recv_s.at[k], (x, y, 1 - c))
        sends.append(cp)
        steps.append((cp, None, None))
    return [], sends, steps


def _exchange_start(plan):
    local, sends, _ = plan
    for cp in local + sends:
        cp.start()


def _exchange_pass_on(plan):
    for arrival, pass_on, _ in plan[2]:
        arrival.wait_recv()
        if pass_on is not None:
            pass_on.start()


def _exchange_finish(plan):
    local, sends, steps = plan
    for _, pass_on, passed in steps:
        if pass_on is not None:
            passed.wait_recv()
    for cp in sends:
        cp.wait_send()
    for _, pass_on, _ in steps:
        if pass_on is not None:
            pass_on.wait_send()
    for cp in local:
        cp.wait()


def _exchange_call(arrays, copies, sems, out_shapes, name):
    n = len(arrays)

    def body(*refs):
        plan = copies(refs[:n], refs[n:2 * n], refs[2 * n:])
        _exchange_start(plan)
        _exchange_pass_on(plan)
        _exchange_finish(plan)

    return pl.pallas_call(
        body, name=name, in_specs=[HBM_REF] * n, out_specs=[HBM_REF] * n, out_shape=out_shapes,
        scratch_shapes=sems(n), compiler_params=pltpu.CompilerParams(has_side_effects=True),
    )(*arrays)


def _gather_call(shards, name):
    shapes = [jax.ShapeDtypeStruct((N_CHIPS,) + a.shape, a.dtype) for a in shards]
    return _exchange_call(shards, _gather_copies, _gather_sems, shapes, name)


def _all_reduce_small(a):
    def body(in_ref, out_ref, slots, send_sems, recv_sems):
        x, y, c = lax.axis_index("x"), lax.axis_index("y"), lax.axis_index("c")
        me = 4 * x + 2 * y + c
        slots[0] = in_ref[...]
        sends = []
        for r in range(1, N_DEV):
            peer = (x ^ (r >> 2), y ^ ((r >> 1) & 1), c ^ (r & 1))
            sends.append(pltpu.make_async_remote_copy(src_ref=in_ref, dst_ref=slots.at[r], send_sem=send_sems.at[r],
                                                      recv_sem=recv_sems.at[r], device_id=peer, device_id_type=MESH))
        for cp in sends:
            cp.start()
        for cp in sends:
            cp.wait_recv()
        acc = slots[me]
        for dev in range(1, N_DEV):
            acc = acc + slots[dev ^ me]
        out_ref[...] = acc
        for cp in sends:
            cp.wait_send()

    return pl.pallas_call(
        body, name="small_all_reduce",
        in_specs=[pl.BlockSpec(memory_space=pltpu.VMEM)], out_specs=pl.BlockSpec(memory_space=pltpu.VMEM),
        out_shape=jax.ShapeDtypeStruct(a.shape, a.dtype),
        scratch_shapes=[pltpu.VMEM((N_DEV,) + a.shape, a.dtype), pltpu.SemaphoreType.DMA((N_DEV,)),
                        pltpu.SemaphoreType.DMA((N_DEV,))],
        compiler_params=pltpu.CompilerParams(has_side_effects=True),
    )(a)


SHARDED = ["w_in", "mla_w_q_up", "mla_w_kv_up", "w_out", "ffn_w_up", "ffn_w_down", "ple_w_gate", "ple_w_proj",
           "gdn_conv_w", "ffn_conv_w"]
SHARD_AXIS = {"w_in": 1, "mla_w_q_up": 1, "mla_w_kv_up": 1, "w_out": 0, "ffn_w_up": 1, "ffn_w_down": 0,
              "ple_w_gate": 0, "ple_w_proj": 1, "gdn_conv_w": 1, "ffn_conv_w": 1}
SMALL = ["gdn_a_log", "gdn_dt_bias", "gdn_norm_g", "mla_q_norm_g", "mla_kv_norm_g", "ln1_g", "ln1_b", "ffn_conv_b",
         "ple_b_gate", "ln2_g", "ln2_b"]
WEIGHTS = ["w_in", "gdn_conv_w", "gdn_a_log", "gdn_dt_bias", "gdn_norm_g", "mla_q_norm_g", "mla_w_q_up", "mla_kv_norm_g",
           "mla_w_kv_up", "w_out", "ln1_g", "ln1_b", "ffn_w_up", "ffn_conv_w", "ffn_conv_b", "ffn_w_down", "ple_w_gate",
           "ple_b_gate", "ple_w_proj", "ln2_g", "ln2_b"]
F32_ON_WIRE = ("gdn_conv_w", "ffn_conv_w")
GATHER_EARLY = ["w_in", "gdn_conv_w", "mla_w_q_up", "mla_w_kv_up"]
GATHER_LATE = ["w_out", "ffn_w_up", "ffn_conv_w", "ffn_w_down", "ple_w_gate", "ple_w_proj"]
SCATTER_EARLY = ["ffn_w_up", "ffn_conv_w", "ffn_w_down", "ple_w_gate", "ple_w_proj", "w_out"]
SCATTER_LATE = ["w_in", "gdn_conv_w", "mla_w_q_up", "mla_w_kv_up"]
PACK_COLS = 1024
PACK_ROW_TILE = 8


def _join_blocks(blocks, axis):
    n, a, b = blocks.shape
    if axis == 0:
        return blocks.reshape(n * a, b)
    return jnp.transpose(blocks, (1, 0, 2)).reshape(a, n * b)


def _split_blocks(full, axis):
    if axis == 0:
        return full.reshape(N_CHIPS, full.shape[0] // N_CHIPS, full.shape[1])
    a, nb = full.shape
    return jnp.transpose(full.reshape(a, N_CHIPS, nb // N_CHIPS), (1, 0, 2))


def _pack(arrays):
    flat = jnp.concatenate([a.reshape(-1) for a in arrays])
    quantum = PACK_COLS * PACK_ROW_TILE
    padded = -(-flat.shape[0] // quantum) * quantum
    return jnp.pad(flat, (0, padded - flat.shape[0])).reshape(-1, PACK_COLS)


def _unpack(packed, shapes):
    flat = packed.reshape(-1)
    out, off = [], 0
    for shp in shapes:
        n = int(np.prod(shp))
        out.append(flat[off:off + n].reshape(shp))
        off += n
    return out


def kernel(x, p, w_in, gdn_conv_w, gdn_a_log, gdn_dt_bias, gdn_norm_g, mla_q_norm_g, mla_w_q_up, mla_kv_norm_g, mla_w_kv_up, w_out, ln1_g, ln1_b, ffn_w_up, ffn_conv_w, ffn_conv_b, ffn_w_down, ple_w_gate, ple_b_gate, ple_w_proj, ln2_g, ln2_b, loss_target, m_w_in, m_gdn_conv_w, m_gdn_a_log, m_gdn_dt_bias, m_gdn_norm_g, m_mla_q_norm_g, m_mla_w_q_up, m_mla_kv_norm_g, m_mla_w_kv_up, m_w_out, m_ln1_g, m_ln1_b, m_ffn_w_up, m_ffn_conv_w, m_ffn_conv_b, m_ffn_w_down, m_ple_w_gate, m_ple_b_gate, m_ple_w_proj, m_ln2_g, m_ln2_b, v_w_in, v_gdn_conv_w, v_gdn_a_log, v_gdn_dt_bias, v_gdn_norm_g, v_mla_q_norm_g, v_mla_w_q_up, v_mla_kv_norm_g, v_mla_w_kv_up, v_w_out, v_ln1_g, v_ln1_b, v_ffn_w_up, v_ffn_conv_w, v_ffn_conv_b, v_ffn_w_down, v_ple_w_gate, v_ple_b_gate, v_ple_w_proj, v_ln2_g, v_ln2_b):
    given = dict(locals())
    wsh = {n: given[n][0] for n in WEIGHTS}
    msh = {n: given["m_" + n][0] for n in WEIGHTS}
    vsh = {n: given["v_" + n][0] for n in WEIGHTS}
    bl, s, _ = x.shape
    t = bl * s
    xt = x.reshape(t, D_MODEL)
    pt = p.reshape(t, PLE_DIM)
    target = loss_target.reshape(t, D_MODEL)

    wire = lambda n: wsh[n] if n in F32_ON_WIRE else wsh[n].astype(BF16)
    early = _gather_call([wire(n) for n in GATHER_EARLY], "weights_gather_early")
    full = {n: _join_blocks(g, SHARD_AXIS[n]) for n, g in zip(GATHER_EARLY, early)}
    late_shards = [wire(n) for n in GATHER_LATE]
    late_ride = (late_shards, _gather_copies, _gather_sems,
                 [jax.ShapeDtypeStruct((N_CHIPS,) + a.shape, a.dtype) for a in late_shards])

    in_cols, q_cols = _w_in_cols(), _w_q_cols()
    w_in_p = _pad_cols(full["w_in"], in_cols)
    w_q_p = _pad_cols(full["mla_w_q_up"], q_cols)
    w_kv, gconv = full["mla_w_kv_up"], full["gdn_conv_w"]
    row = lambda a: a.reshape(1, -1)
    sc = jnp.zeros((8, 128), F32).at[0, :GDN_HEADS].set(wsh["gdn_a_log"]).at[1, :GDN_HEADS].set(wsh["gdn_dt_bias"])
    norm_g, qg, kvg = row(wsh["gdn_norm_g"]), row(wsh["mla_q_norm_g"]), row(wsh["mla_kv_norm_g"])
    g1, b1, g2, b2 = row(wsh["ln1_g"]), row(wsh["ln1_b"]), row(wsh["ln2_g"]), row(wsh["ln2_b"])
    fbias, bgate = row(wsh["ffn_conv_b"]), row(wsh["ple_b_gate"])

    inv = ROPE_THETA ** (-jnp.arange(0, MLA_ROPE, 2, dtype=F32) / MLA_ROPE)
    ang = jnp.arange(s, dtype=F32)[:, None] * inv[None, :]
    zero = jnp.zeros_like(ang)
    cos_t = jnp.concatenate([jnp.cos(ang), zero, jnp.cos(ang), zero], axis=1)
    sin_t = jnp.concatenate([-jnp.sin(ang), zero, jnp.sin(ang), zero], axis=1)

    proj = _matmul(xt, w_in_p, name="proj", tm=1024)
    cat, o_raw, states, qkvg, late = _gdn_fwd(proj, gconv, sc, norm_g, bl, s, late_ride)
    full.update({n: _join_blocks(g, SHARD_AXIS[n]) for n, g in zip(GATHER_LATE, late)})
    w_o, w_up, w_down = full["w_out"], full["ffn_w_up"], full["ffn_w_down"]
    w_gate, w_proj, fconv = full["ple_w_gate"], full["ple_w_proj"], full["ffn_conv_w"]
    qf, kvf, kr = _mla_prep_fwd(proj, qg, kvg, w_q_p, w_kv, cos_t, sin_t, s, 256)
    cat = _attn_fwd(qf, kvf, kr, cat, bl, s, 256)
    wide = dict(tm=1024, tn=1024)
    mix = _matmul(cat, w_o, name="mix", **wide)
    r1, h1, h1b, xb = _ln1_fwd(xt, mix, g1, b1, 256)
    u = _matmul(h1b, w_up, name="ffn_up", tm=1024)
    act = _ffn_act_fwd(u, fconv, fbias, bl, s, 256)
    ffn = _matmul(act, w_down, name="ffn_down", tk=1408, **wide)
    gpre = _matmul(h1b, w_gate, name="ple_gate", **wide)
    pp = _matmul(pt, w_proj, name="ple_proj", **wide)
    dr2, dr2b, dgpre, dpp, loss_acc, dbgate, dg2, db2 = _head(h1, ffn, gpre, pp, bgate, g2, b2, target, 256)

    dact = _matmul(dr2b, w_down, name="d_act", tb=True, tm=1024, tn=1408)
    d_w_down = _matmul(act, dr2b, name="dw_down", ta=True, tm=1408, tn=1024, tk=512)
    du, dfcw_g, dfcw_u, dfcb_g, dfcb_u = _ffn_act_bwd(u, fconv, fbias, dact, bl, s, 256)
    dh1_a = _matmul(du, w_up, name="dh1_ffn", tb=True, tk=1408, a_halves=True, **wide)
    dh1_b = _matmul(dgpre, w_gate, name="dh1_ple", tb=True, **wide)
    d_w_up = _matmul(h1b, du, name="dw_up", ta=True, tn=1408, b_halves=True)
    d_w_gate = _matmul(h1b, dgpre, name="dw_gate", ta=True, tk=512, **wide)
    d_w_proj = _matmul(pt, dpp, name="dw_proj", ta=True, tn=1024)
    dr1, dr1b, dg1, db1 = _ln1_bwd(r1, dr2, dh1_a, dh1_b, g1, b1, 256)
    dcat = _matmul(dr1b, w_o, name="d_cat", tb=True, **wide)
    d_w_o = _matmul(cat, dr1b, name="dw_out", ta=True, tk=512, **wide)

    gfull = {
        "ffn_w_up": d_w_up, "ffn_w_down": d_w_down, "ple_w_gate": d_w_gate, "ple_w_proj": d_w_proj, "w_out": d_w_o,
        "ffn_conv_w": jnp.concatenate([jnp.sum(dfcw_g, 0), jnp.sum(dfcw_u, 0)], axis=1),
    }
    slabs = {n: _split_blocks(gfull[n], SHARD_AXIS[n]) for n in SCATTER_EARLY}
    early_slabs = [slabs[n] for n in SCATTER_EARLY]
    early_ride = (early_slabs, _scatter_copies, _scatter_sems,
                  [jax.ShapeDtypeStruct((N_DEV - 1,) + a.shape[1:], a.dtype) for a in early_slabs])
    dproj, dab, dcwq, dcwk, dcwv, dsc, dng, early_recv = _gdn_bwd(proj, gconv, sc, norm_g, o_raw, states, qkvg, dcat, bl, s,
                                                                  early_ride)
    received = dict(zip(SCATTER_EARLY, early_recv))
    dqf, dkvf, dkr = _attn_bwd(qf, kvf, kr, dcat, bl, s, 256)
    dproj, dqg, dkvg, d_w_q_p, d_w_kv = _mla_prep_bwd(proj, qg, kvg, w_q_p, w_kv, cos_t, sin_t, dqf, dkvf, dkr, dab, dproj, s, 256)
    d_w_in_p = _matmul(xb, dproj, name="dw_in", ta=True, tk=512, **wide)

    gfull.update({
        "w_in": _unpad_cols(d_w_in_p, in_cols, D_IN),
        "mla_w_q_up": _unpad_cols(d_w_q_p, q_cols, MLA_HEADS * (MLA_NOPE + MLA_ROPE)),
        "mla_w_kv_up": d_w_kv,
        "gdn_conv_w": jnp.concatenate([jnp.sum(dcwq, 0), jnp.sum(dcwk, 0), jnp.sum(dcwv, 0)], axis=1),
    })
    slabs.update({n: _split_blocks(gfull[n], SHARD_AXIS[n]).astype(BF16) for n in SCATTER_LATE})
    late_slabs = [slabs[n] for n in SCATTER_LATE]
    late_scatter = (late_slabs, _scatter_copies, _scatter_sems,
                    [jax.ShapeDtypeStruct((N_DEV - 1,) + a.shape[1:], a.dtype) for a in late_slabs])
    grad_x, late_recv = _matmul(dproj, w_in_p, name="d_x", tb=True, add=dr1, add_scale=ALPHA, ride=late_scatter, **wide)
    received.update(zip(SCATTER_LATE, late_recv))
    dsc_sum = jnp.sum(dsc, axis=(0, 1))
    gsmall = {
        "gdn_a_log": dsc_sum[0, :GDN_HEADS], "gdn_dt_bias": dsc_sum[1, :GDN_HEADS],
        "gdn_norm_g": jnp.sum(dng[:, :, 0, :], axis=(0, 1)),
        "mla_q_norm_g": dqg[0], "mla_kv_norm_g": dkvg[0], "ln1_g": dg1[0], "ln1_b": db1[0],
        "ffn_conv_b": jnp.concatenate([jnp.sum(dfcb_g, 0), jnp.sum(dfcb_u, 0)], axis=1)[0],
        "ple_b_gate": dbgate[0], "ln2_g": dg2[0], "ln2_b": db2[0],
    }

    me_chip = 2 * lax.axis_index("x") + lax.axis_index("y")
    big = [{}, {}, {}, {}]
    for n in SHARDED:
        own = lax.dynamic_index_in_dim(slabs[n], me_chip, 0, keepdims=False)
        for kind, val in enumerate(_adamw_reduced(own, received[n], wsh[n], msh[n], vsh[n], "adamw_" + n)):
            big[kind][n] = val

    small_shapes = [wsh[n].shape for n in SMALL]
    gsum = _all_reduce_small(_pack([gsmall[n] for n in SMALL]))
    spacks = _adamw_small(gsum, _pack([wsh[n] for n in SMALL]), _pack([msh[n] for n in SMALL]), _pack([vsh[n] for n in SMALL]))
    small = [dict(zip(SMALL, _unpack(pk, small_shapes))) for pk in spacks]

    loss = lax.psum(loss_acc[0, 0], ("x", "y", "c"))
    outs = [loss, grad_x.reshape(x.shape)]
    for kind in range(4):
        for n in WEIGHTS:
            val = big[kind][n] if n in big[kind] else small[kind][n]
            outs.append(val[None])
    return tuple(outs)
```

```python
import functools
import math

import numpy as np
import jax
import jax.numpy as jnp
from jax import lax
from jax.experimental import pallas as pl
from jax.experimental.pallas import tpu as pltpu

F32 = jnp.float32
BF16 = jnp.bfloat16

D_MODEL = 1024
CHUNK = 64
PLE_DIM = 256
GDN_HEADS = 4
GDN_DK = 128
GDN_DV = 128
GDN_CONV = 4
MLA_HEADS = 4
MLA_NOPE = 128
MLA_ROPE = 64
MLA_V = 128
MLA_Q_LORA = 384
MLA_KV_LORA = 256
ROPE_THETA = 10000.0
D_FF = 2816
FFN_CONV = 3
DEPTH = 1
ALPHA = (2.0 * DEPTH) ** 0.25
NORM_EPS = 1e-6
GDN_QK = GDN_HEADS * GDN_DK
GDN_VW = GDN_HEADS * GDN_DV
D_IN = 2 * GDN_QK + 2 * GDN_VW + 2 * GDN_HEADS + MLA_Q_LORA + MLA_KV_LORA + MLA_ROPE
ATT_SCALE = (MLA_NOPE + MLA_ROPE) ** -0.5

ADAM_LR = 0.001
ADAM_B1 = 0.9
ADAM_B2 = 0.999
ADAM_EPS = 1e-08
ADAM_WD = 0.01
ADAM_STEP = 10

LANES = 128
VMEM_LIMIT = 60 * 1024 * 1024
GDN_FWD_GROUP = 16
GDN_BWD_GROUP = 16
N_CHIPS = 4
N_DEV = 8

P_WIDTH = 3072
P_MLA = 2048
MESH = pl.DeviceIdType.MESH


def _rope_slot(j):
    return j if j < MLA_ROPE // 2 else 64 + (j - MLA_ROPE // 2)


def _w_in_cols():
    idx = -np.ones((P_WIDTH,), np.int64)
    for h in range(GDN_HEADS):
        base = h * 512
        idx[base:base + 128] = np.arange(128) + h * GDN_DK
        idx[base + 128:base + 256] = np.arange(128) + GDN_QK + h * GDN_DK
        idx[base + 256:base + 384] = np.arange(128) + 2 * GDN_QK + h * GDN_DV
        idx[base + 384:base + 512] = np.arange(128) + 2 * GDN_QK + GDN_VW + h * GDN_DV
    o_a = 2 * GDN_QK + 2 * GDN_VW
    idx[P_MLA:P_MLA + 2 * GDN_HEADS] = np.arange(2 * GDN_HEADS) + o_a
    o_cq = o_a + 2 * GDN_HEADS
    idx[P_MLA + 128:P_MLA + 512] = np.arange(MLA_Q_LORA) + o_cq
    o_ckv = o_cq + MLA_Q_LORA
    idx[P_MLA + 512:P_MLA + 768] = np.arange(MLA_KV_LORA) + o_ckv
    o_kr = o_ckv + MLA_KV_LORA
    for j in range(MLA_ROPE):
        idx[P_MLA + 768 + _rope_slot(j)] = o_kr + j
    return idx


def _w_q_cols():
    idx = -np.ones((MLA_HEADS * 256,), np.int64)
    for h in range(MLA_HEADS):
        o = h * (MLA_NOPE + MLA_ROPE)
        idx[h * 256:h * 256 + 128] = np.arange(128) + o
        for j in range(MLA_ROPE):
            idx[h * 256 + 128 + _rope_slot(j)] = o + MLA_NOPE + j
    return idx


def _pad_cols(w, idx):
    safe = np.where(idx >= 0, idx, 0)
    return jnp.where(jnp.asarray(idx >= 0)[None, :], w[:, safe], 0.0)


def _unpad_cols(wp, idx, n):
    inv = np.zeros((n,), np.int64)
    inv[idx[idx >= 0]] = np.nonzero(idx >= 0)[0]
    return wp[:, inv]


def _dot(a, b, ca, cb, precision=None):
    if precision is None:
        a = a.astype(BF16)
        b = b.astype(BF16)
    return lax.dot_general(a, b, (((ca,), (cb,)), ((), ())), preferred_element_type=F32, precision=precision)


@jax.custom_vjp
def mm(a, b):
    return _dot(a, b, 1, 0)


@jax.custom_vjp
def mm_nt(a, b):
    return _dot(a, b, 1, 1)


@jax.custom_vjp
def mm_tn(a, b):
    return _dot(a, b, 0, 0)


mm.defvjp(lambda a, b: (mm(a, b), (a, b)), lambda r, g: (mm_nt(g, r[1]), mm_tn(r[0], g)))
mm_nt.defvjp(lambda a, b: (mm_nt(a, b), (a, b)), lambda r, g: (mm(g, r[1]), mm_tn(g, r[0])))
mm_tn.defvjp(lambda a, b: (mm_tn(a, b), (a, b)), lambda r, g: (mm_nt(r[1], g), mm(r[0], g)))

def _split(a):
    hi = a.astype(BF16)
    return hi, (a - hi.astype(F32)).astype(BF16)


def _dot3(a, b, ca, cb):
    a_hi, a_lo = _split(a)
    b_hi, b_lo = _split(b)
    return (_dot(a_hi, b_hi, ca, cb) + _dot(a_hi, b_lo, ca, cb)) + _dot(a_lo, b_hi, ca, cb)


def _shift_rows(x, s):
    return x if s == 0 else pltpu.roll(x, s % x.shape[0], 0)


def _row(w, j):
    tap = lax.broadcasted_iota(jnp.int32, w.shape, 0)
    return jnp.sum(jnp.where(tap == j, w, 0.0), axis=0, keepdims=True)


@jax.custom_vjp
def dwconv(x, w):
    k = w.shape[0]
    y = _row(w, k - 1) * x
    for j in range(k - 1):
        y = y + _row(w, j) * _shift_rows(x, k - 1 - j)
    return y


def _dwconv_fwd(x, w):
    return dwconv(x, w), (x, w)


def _dwconv_bwd(res, dy):
    x, w = res
    k = w.shape[0]
    dx = _row(w, k - 1) * dy
    tap = lax.broadcasted_iota(jnp.int32, w.shape, 0)
    dw = jnp.where(tap == k - 1, jnp.sum(dy * x, axis=0, keepdims=True), 0.0)
    for j in range(k - 1):
        dx = dx + _row(w, j) * _shift_rows(dy, -(k - 1 - j))
        dw = dw + jnp.where(tap == j, jnp.sum(dy * _shift_rows(x, k - 1 - j), axis=0, keepdims=True), 0.0)
    return dx, dw


dwconv.defvjp(_dwconv_fwd, _dwconv_bwd)


@jax.custom_vjp
def rope128(x, cos, sin):
    return x * cos + pltpu.roll(x, 64, 1) * sin


rope128.defvjp(lambda x, c, s: (rope128(x, c, s), (c, s)),
               lambda r, g: (g * r[0] + pltpu.roll(g * r[1], 64, 1), jnp.zeros_like(r[0]), jnp.zeros_like(r[1])))


def _silu(x):
    return x * jax.nn.sigmoid(x)


def _softplus(x):
    return jnp.maximum(x, 0.0) + jnp.log(1.0 + jnp.exp(-jnp.abs(x)))


def _rmsnorm(x, g):
    return x * lax.rsqrt(jnp.mean(x * x, axis=-1, keepdims=True) + NORM_EPS) * g


def _layernorm(x, g, b):
    mu = jnp.mean(x, axis=-1, keepdims=True)
    xc = x - mu
    var = jnp.mean(xc * xc, axis=-1, keepdims=True)
    return xc * lax.rsqrt(var + NORM_EPS) * g + b


def _pick_lane(row, lane):
    idx = lax.broadcasted_iota(jnp.int32, row.shape, 1)
    return jnp.sum(jnp.where(idx == lane, row, 0.0), axis=1, keepdims=True)


def _gdn_q(pq, cw):
    h = _silu(dwconv(pq, cw))
    return h * lax.rsqrt(jnp.sum(h * h, axis=-1, keepdims=True) + NORM_EPS) * (GDN_DK ** -0.5)


def _gdn_k(pk, cw):
    h = _silu(dwconv(pk, cw))
    return h * lax.rsqrt(jnp.sum(h * h, axis=-1, keepdims=True) + NORM_EPS)


def _gdn_v(pv, cw):
    return _silu(dwconv(pv, cw))


def _gdn_gate(ab, sc, head):
    a = _pick_lane(ab, head)
    b = _pick_lane(ab, GDN_HEADS + head)
    a_log = _pick_lane(_row(sc, 0), head)
    dt_bias = _pick_lane(_row(sc, 1), head)
    beta = jax.nn.sigmoid(b)
    g = -jnp.exp(a_log) * _softplus(a + dt_bias)
    return _two_lanes(g, beta)


def _two_lanes(c0, c1):
    lane = lax.broadcasted_iota(jnp.int32, (c0.shape[0], LANES), 1)
    return jnp.where(lane == 0, c0, jnp.where(lane == 1, c1, 0.0))


def _inverse_group(lows):
    n = lows[0].shape[0]
    ii = lax.broadcasted_iota(jnp.int32, (n, n), 0)
    jj = lax.broadcasted_iota(jnp.int32, (n, n), 1)
    eye = jnp.where(ii == jj, 1.0, 0.0)
    invs = [eye - low for low in lows]
    powers = [_dot3(low, low, 1, 0) for low in lows]
    k = 2
    while k < n:
        invs = [inv + _dot3(inv, p, 1, 0) for inv, p in zip(invs, powers)]
        k *= 2
        if k < n:
            powers = [_dot3(p, p, 1, 0) for p in powers]
    return invs


@jax.custom_vjp
def solve_group(lows, rhss):
    return [_dot3(inv, rhs, 1, 0) for inv, rhs in zip(_inverse_group(lows), rhss)]


def _solve_group_fwd(lows, rhss):
    invs = _inverse_group(lows)
    xs = [_dot3(inv, rhs, 1, 0) for inv, rhs in zip(invs, rhss)]
    return xs, (invs, xs)


def _solve_group_bwd(res, dxs):
    invs, xs = res
    n = invs[0].shape[0]
    strict = lax.broadcasted_iota(jnp.int32, (n, n), 0) > lax.broadcasted_iota(jnp.int32, (n, n), 1)
    drhss = [_dot3(inv, dx, 0, 0) for inv, dx in zip(invs, dxs)]
    dlows = [jnp.where(strict, -_dot3(drhs, x, 1, 1), 0.0) for drhs, x in zip(drhss, xs)]
    return dlows, drhss


solve_group.defvjp(_solve_group_fwd, _solve_group_bwd)


def _gdn_local_group(qs, ks, vs, gbs):
    c = qs[0].shape[0]
    ii = lax.broadcasted_iota(jnp.int32, (c, c), 0)
    jj = lax.broadcasted_iota(jnp.int32, (c, c), 1)
    incl = ii >= jj
    gs = [_pick_lane(gb, 0) for gb in gbs]
    betas = [_pick_lane(gb, 1) for gb in gbs]
    g_rows = [jnp.sum(jnp.where(ii == jj, g, 0.0), axis=0, keepdims=True) for g in gs]
    gc_cols = [jnp.sum(jnp.where(incl, g_row, 0.0), axis=1, keepdims=True) for g_row in g_rows]
    gc_rows = [jnp.sum(jnp.where(jj >= ii, g, 0.0), axis=0, keepdims=True) for g in gs]
    decays = [jnp.where(incl, jnp.exp(jnp.where(incl, gc - gr, 0.0)), 0.0) for gc, gr in zip(gc_cols, gc_rows)]
    kbs = [k * beta for k, beta in zip(ks, betas)]
    lows = [jnp.where(ii > jj, mm_nt(kb, k) * decay, 0.0) for kb, k, decay in zip(kbs, ks, decays)]
    egs = [jnp.exp(gc) for gc in gc_cols]
    wus = solve_group(lows, [jnp.concatenate([kb * eg, v * beta], axis=1) for kb, eg, v, beta in zip(kbs, egs, vs, betas)])
    qks = [mm_nt(q, k) * decay for q, k, decay in zip(qs, ks, decays)]
    g_lasts = [jnp.sum(g_row, axis=1, keepdims=True) for g_row in g_rows]
    kds = [k * jnp.exp(gl - gc) for k, gl, gc in zip(ks, g_lasts, gc_cols)]
    ws, us = [wu[:, :GDN_DK] for wu in wus], [wu[:, GDN_DK:] for wu in wus]
    q_effs = [q * eg - mm(qk, w) for q, eg, qk, w in zip(qs, egs, qks, ws)]
    o_locals = [mm(qk, u) for qk, u in zip(qks, us)]
    mixes = [mm_tn(kd, w) for kd, w in zip(kds, ws)]
    adds = [mm_tn(kd, u) for kd, u in zip(kds, us)]
    return [(q_eff, o_loc, mix, add, jnp.exp(gl))
            for q_eff, o_loc, mix, add, gl in zip(q_effs, o_locals, mixes, adds, g_lasts)]


def _gdn_state_step(q_eff, o_local, mix, add, eg_last, state):
    return mm(q_eff, state) + o_local, state * eg_last - mm(mix, state) + add


def _gdn_post(o, z, norm_g):
    return _rmsnorm(o, norm_g) * _silu(z)


def _attn_block(q, kn, kr, v, q0):
    s = (mm_nt(q[:, :128], kn) + mm_nt(q[:, 128:], kr)) * ATT_SCALE
    qpos = q0 + lax.broadcasted_iota(jnp.int32, s.shape, 0)
    kpos = lax.broadcasted_iota(jnp.int32, s.shape, 1)
    shift = int(math.log2(CHUNK))
    allowed = (kpos >> shift) <= (qpos >> shift)
    s = jnp.where(allowed, s, -1e30)
    p = jnp.exp(s - jnp.max(s, axis=-1, keepdims=True))
    p = p / jnp.sum(p, axis=-1, keepdims=True)
    return mm(p, v)


def _mla_prep(pm, qg, kvg, wq, wkv, cos, sin):
    cq = pm[:, 128:512]
    ckv = pm[:, 512:768]
    qf = mm(_rmsnorm(cq, qg), wq)
    parts = []
    for h in range(MLA_HEADS):
        parts.append(qf[:, h * 256:h * 256 + 128])
        parts.append(rope128(qf[:, h * 256 + 128:h * 256 + 256], cos, sin))
    kvf = mm(_rmsnorm(ckv, kvg), wkv)
    return jnp.concatenate(parts, axis=1), kvf, rope128(pm[:, 768:896], cos, sin)


def _ffn_act(ug, uu, wg, wu, bg, bu):
    return _silu(dwconv(ug, wg) + bg) * (dwconv(uu, wu) + bu)


def _head_loss(h1, ffn, gpre, pp, bgate, g2, b2, target):
    gate = jax.nn.sigmoid(gpre + bgate)
    h2 = _layernorm(ALPHA * h1 + ffn + gate * pp, g2, b2)
    err = h2 - target
    return 0.5 * jnp.sum(jnp.sum(err * err, axis=1, keepdims=True), axis=0, keepdims=True) / D_MODEL


ROW_TILE_VREGS = 32
CONV_HALO = 8


def _rows_per_tile(n_rows, cols):
    tile = min(n_rows, ROW_TILE_VREGS * 8 * LANES // cols)
    assert n_rows % tile == 0 and tile % CONV_HALO == 0, (n_rows, cols)
    return tile


def _tile_inputs(loads, t0, first, halo, tile):
    if halo == 0:
        return [ld(pl.ds(t0, tile)) for ld in loads]
    if first:
        xs = [ld(pl.ds(0, tile)) for ld in loads]
        return [jnp.concatenate([jnp.zeros((halo, x.shape[1]), x.dtype), x], axis=0) for x in xs]
    return [ld(pl.ds(pl.multiple_of(t0 - halo, CONV_HALO), tile + halo)) for ld in loads]


def _rows_apply(fn, loads, consts, store, n_rows, cols, halo):
    tile = _rows_per_tile(n_rows, cols)

    def one(t0, first):
        y = fn(*_tile_inputs(loads, t0, first, halo, tile), *consts)
        store(pl.ds(t0, tile), y[halo:] if halo else y)

    one(0, True)

    def step(i, carry):
        one(pl.multiple_of(i * tile, tile), False)
        return carry

    lax.fori_loop(1, n_rows // tile, step, 0)


def _rows_vjp(fn, loads, consts, load_dy, stores, n_rows, cols, halo):
    tile = _rows_per_tile(n_rows, cols)

    def one(t0, first, dconsts):
        xs = _tile_inputs(loads, t0, first, halo, tile)
        _, vjp = jax.vjp(lambda *a: fn(*a)[halo:] if halo else fn(*a), *xs, *consts)
        grads = vjp(load_dy(pl.ds(t0, tile)))
        for st, dx in zip(stores, grads[:len(xs)]):
            st(pl.ds(t0, tile), dx[halo:] if halo else dx, False)
            if halo and not first:
                st(pl.ds(pl.multiple_of(t0 - halo, CONV_HALO), halo), dx[:halo], True)
        return tuple(a + b for a, b in zip(dconsts, grads[len(xs):]))

    dconsts = one(0, True, tuple(jnp.zeros_like(c) for c in consts))
    return lax.fori_loop(1, n_rows // tile, lambda i, dc: one(pl.multiple_of(i * tile, tile), False, dc), dconsts)


def _params(sem):
    return pltpu.CompilerParams(dimension_semantics=sem, vmem_limit_bytes=VMEM_LIMIT)


def _matmul(a, b, *, name, ta=False, tb=False, tm=512, tn=512, tk=1024, add=None, add_scale=1.0,
            a_halves=False, b_halves=False, ride=None):
    assert not (a_halves and ta) and not (b_halves and tb)
    a_shape = (a.shape[1], 2 * a.shape[2]) if a_halves else a.shape
    b_shape = (b.shape[1], 2 * b.shape[2]) if b_halves else b.shape
    (k_dim, m) = a_shape if ta else a_shape[::-1]
    (n, k2) = b_shape if tb else b_shape[::-1]
    assert k_dim == k2, (a.shape, b.shape)
    tm, tn, tk = min(tm, m), min(tn, n), min(tk, k_dim)
    assert m % tm == 0 and n % tn == 0 and k_dim % tk == 0, (name, m, n, k_dim, tm, tn, tk)
    nk = k_dim // tk
    ca, cb = (0 if ta else 1), (1 if tb else 0)

    def body(*refs):
        if add is None:
            a_ref, b_ref, o_ref, acc = refs
        else:
            a_ref, b_ref, c_ref, o_ref, acc = refs
        kk = pl.program_id(2)

        @pl.when(kk == 0)
        def _():
            acc[...] = jnp.zeros_like(acc)

        acc[...] += _dot(a_ref[...], b_ref[...], ca, cb)

        @pl.when(kk == nk - 1)
        def _():
            r = acc[...]
            if add is not None:
                r = r + add_scale * c_ref[...]
            o_ref[...] = r

    spec = pl.BlockSpec
    a_spec = spec((tk, tm), lambda i, j, k: (k, i)) if ta else spec((tm, tk), lambda i, j, k: (i, k))
    b_spec = spec((tn, tk), lambda i, j, k: (j, k)) if tb else spec((tk, tn), lambda i, j, k: (k, j))
    if a_halves:
        kh = k_dim // 2 // tk
        assert kh * tk * 2 == k_dim
        a_spec = spec((None, tm, tk), lambda i, j, k: (k // kh, i, k % kh))
    if b_halves:
        nh = n // 2 // tn
        assert nh * tn * 2 == n
        b_spec = spec((None, tk, tn), lambda i, j, k: (j // nh, k, j % nh))
    in_specs = [a_spec, b_spec]
    args = [a, b]
    if add is not None:
        in_specs.append(pl.BlockSpec((tm, tn), lambda i, j, k: (i, j)))
        args.append(add)
    grid = (m // tm, n // tn, nk)
    r_in, r_out, r_shapes, r_sems, r_args = _ride_specs(ride)
    outs = pl.pallas_call(
        _riding(body, len(args), 1, 1, ride, grid), name=name, grid=grid,
        in_specs=in_specs + r_in, out_specs=[pl.BlockSpec((tm, tn), lambda i, j, k: (i, j))] + r_out,
        out_shape=[jax.ShapeDtypeStruct((m, n), F32)] + r_shapes,
        scratch_shapes=[pltpu.VMEM((tm, tn), F32)] + r_sems,
        compiler_params=_params(("parallel", "parallel", "arbitrary") if ride is None else ("arbitrary",) * 3),
    )(*args, *r_args)
    return outs[0] if ride is None else (outs[0], list(outs[1:]))


def _riding(core, n_in, n_out, n_scratch, ride, grid):
    if ride is None:
        return core
    copies, nr = ride[1], len(ride[0])
    steps = int(np.prod(grid))
    assert steps >= 3, grid

    def body(*refs):
        cuts = np.cumsum([0, n_in, nr, n_out, nr, n_scratch])
        ins, rin, outs, rout, scratch = (refs[a:b] for a, b in zip(cuts[:-1], cuts[1:]))
        sems = refs[cuts[-1]:]
        step = 0
        for axis, size in enumerate(grid):
            step = step * size + pl.program_id(axis)

        @pl.when(step == 0)
        def _():
            _exchange_start(copies(rin, rout, sems))

        @pl.when(step == steps - 2)
        def _():
            _exchange_pass_on(copies(rin, rout, sems))

        core(*ins, *outs, *scratch)

        @pl.when(step == steps - 1)
        def _():
            _exchange_finish(copies(rin, rout, sems))

    return body


def _ride_specs(ride):
    if ride is None:
        return [], [], [], [], []
    arrays, _, sems, shapes = ride
    return [HBM_REF] * len(arrays), [HBM_REF] * len(arrays), list(shapes), sems(len(arrays)), list(arrays)


def _gdn_fwd(proj, conv_w, sc, norm_g, bl, s, ride=None):
    nc = s // CHUNK

    def core(ph_ref, ab_ref, cwq_ref, cwk_ref, cwv_ref, sc_ref, ng_ref, cat_ref, o_ref, st_ref, q_s, k_s, v_s, gb_s):
        def into(ref):
            def store(rows, value):
                ref[rows, :] = value.astype(ref.dtype)
            return store

        for fn, col, cw_ref, val_s in [(_gdn_q, 0, cwq_ref, q_s), (_gdn_k, 128, cwk_ref, k_s), (_gdn_v, 256, cwv_ref, v_s)]:
            _rows_apply(fn, [lambda r, col=col: ph_ref[r, col:col + 128]], [cw_ref[...]], into(val_s), s, LANES, CONV_HALO)
        _rows_apply(functools.partial(_gdn_gate, head=pl.program_id(1)), [lambda r: ab_ref[r, :]], [sc_ref[...]], into(gb_s), s, LANES, 0)

        group = math.gcd(nc, GDN_FWD_GROUP)

        def chunks(i, state):
            ns = [i * group + j for j in range(group)]
            rows = [pl.ds(pl.multiple_of(n * CHUNK, CHUNK), CHUNK) for n in ns]
            local = _gdn_local_group([q_s[r, :] for r in rows], [k_s[r, :] for r in rows], [v_s[r, :] for r in rows],
                                     [gb_s[r, :] for r in rows])
            for n, r, loc in zip(ns, rows, local):
                st_ref[n] = state
                o_ref[r, :], state = _gdn_state_step(*loc, state)
            return state

        lax.fori_loop(0, nc // group, chunks, jnp.zeros((GDN_DK, GDN_DV), F32))
        _rows_apply(_gdn_post, [lambda r: o_ref[r, :], lambda r: ph_ref[r, 384:512]], [ng_ref[...]], into(cat_ref), s, LANES, 0)

    t = bl * s
    r_in, r_out, r_shapes, r_sems, r_args = _ride_specs(ride)
    outs = pl.pallas_call(
        _riding(core, 7, 7, 0, ride, (bl, GDN_HEADS)), name="gdn_fwd", grid=(bl, GDN_HEADS),
        in_specs=[
            pl.BlockSpec((s, 512), lambda b, h: (b, h)),
            pl.BlockSpec((s, 128), lambda b, h: (b, P_MLA // 128)),
            pl.BlockSpec((GDN_CONV, 128), lambda b, h: (0, h)),
            pl.BlockSpec((GDN_CONV, 128), lambda b, h: (0, GDN_HEADS + h)),
            pl.BlockSpec((GDN_CONV, 128), lambda b, h: (0, 2 * GDN_HEADS + h)),
            pl.BlockSpec((8, 128), lambda b, h: (0, 0)),
            pl.BlockSpec((1, 128), lambda b, h: (0, 0)),
        ] + r_in,
        out_specs=[
            pl.BlockSpec((s, 128), lambda b, h: (b, h)),
            pl.BlockSpec((s, 128), lambda b, h: (b, h)),
            pl.BlockSpec((None, None, nc, GDN_DK, GDN_DV), lambda b, h: (b, h, 0, 0, 0)),
        ] + [pl.BlockSpec((s, 128), lambda b, h: (b, h))] * 4 + r_out,
        out_shape=[
            jax.ShapeDtypeStruct((t, 2 * GDN_VW), BF16),
            jax.ShapeDtypeStruct((t, GDN_VW), F32),
            jax.ShapeDtypeStruct((bl, GDN_HEADS, nc, GDN_DK, GDN_DV), F32),
        ] + [jax.ShapeDtypeStruct((t, GDN_VW), F32)] * 4 + r_shapes,
        scratch_shapes=r_sems,
        compiler_params=_params(("arbitrary", "arbitrary")),
    )(proj, proj, conv_w, conv_w, conv_w, sc, norm_g, *r_args)
    return outs[0], outs[1], outs[2], tuple(outs[3:7]), list(outs[7:])


def _gdn_bwd(proj, conv_w, sc, norm_g, o_raw, states, qkvg, dcat, bl, s, ride=None):
    nc = s // CHUNK

    def core(ph_ref, ab_ref, cwq_ref, cwk_ref, cwv_ref, sc_ref, ng_ref, o_ref, st_ref, dc_ref, q_in, k_in, v_in, gb_in,
             dph_ref, dab_ref, dcwq_ref, dcwk_ref, dcwv_ref, dsc_ref, dng_ref, q_s, k_s, v_s, gb_s, do_s):
        head = pl.program_id(1)
        gate = functools.partial(_gdn_gate, head=head)
        paths = [(_gdn_q, 0, cwq_ref, q_s, dcwq_ref), (_gdn_k, 128, cwk_ref, k_s, dcwk_ref), (_gdn_v, 256, cwv_ref, v_s, dcwv_ref)]
        def into(ref, cols=slice(None)):
            def store(rows, value, add=False):
                if add:
                    ref[rows, cols] += value.astype(ref.dtype)
                else:
                    ref[rows, cols] = value.astype(ref.dtype)
            return store

        (dng,) = _rows_vjp(_gdn_post, [lambda r: o_ref[r, :], lambda r: ph_ref[r, 384:512]], [ng_ref[...]],
                           lambda r: dc_ref[r, :], [into(do_s), into(dph_ref, slice(384, 512))], s, LANES, 0)
        dng_ref[...] = jnp.broadcast_to(dng, dng_ref.shape)

        group = math.gcd(nc, GDN_BWD_GROUP)

        def chunks(i, dstate):
            ns = [nc - 1 - (i * group + j) for j in range(group)]
            rows = [pl.ds(pl.multiple_of(n * CHUNK, CHUNK), CHUNK) for n in ns]
            local, local_vjp = jax.vjp(_gdn_local_group, [q_in[r, :] for r in rows], [k_in[r, :] for r in rows],
                                       [v_in[r, :] for r in rows], [gb_in[r, :] for r in rows])
            d_os = [do_s[r, :] for r in rows]
            dlocal = []
            for n, loc, d_o in zip(ns, local, d_os):
                _, step_vjp = jax.vjp(_gdn_state_step, *loc, st_ref[n])
                *dloc, dstate = step_vjp((d_o, dstate))
                dlocal.append(tuple(dloc))
            dqs, dks, dvs, dgbs = local_vjp(dlocal)
            for r, dq, dk, dv, dgb in zip(rows, dqs, dks, dvs, dgbs):
                q_s[r, :], k_s[r, :], v_s[r, :], gb_s[r, :] = dq, dk, dv, dgb
            return dstate

        lax.fori_loop(0, nc // group, chunks, jnp.zeros((GDN_DK, GDN_DV), F32))
        for fn, col, cw_ref, val_s, dcw_ref in paths:
            (dcw_ref[...],) = _rows_vjp(fn, [lambda r, col=col: ph_ref[r, col:col + 128]], [cw_ref[...]],
                                        lambda r, val_s=val_s: val_s[r, :], [into(val_s)], s, LANES, CONV_HALO)
            dph_ref[:, col:col + 128] = val_s[...].astype(BF16)

        @pl.when(head == 0)
        def _():
            dab_ref[...] = jnp.zeros_like(dab_ref)

        def add_dab(rows, value, add=False):
            dab_ref[rows, :] += value

        (dsc_ref[...],) = _rows_vjp(gate, [lambda r: ab_ref[r, :]], [sc_ref[...]], lambda r: gb_s[r, :], [add_dab], s, LANES, 0)

    t = bl * s
    cw_out = pl.BlockSpec((None, GDN_CONV, 128), lambda b, h: (b, 0, h))
    part = pl.BlockSpec((None, None, 8, 128), lambda b, h: (b, h, 0, 0))
    r_in, r_out, r_shapes, r_sems, r_args = _ride_specs(ride)
    outs = pl.pallas_call(
        _riding(core, 14, 7, 5, ride, (bl, GDN_HEADS)), name="gdn_bwd", grid=(bl, GDN_HEADS),
        in_specs=[
            pl.BlockSpec((s, 512), lambda b, h: (b, h)),
            pl.BlockSpec((s, 128), lambda b, h: (b, P_MLA // 128)),
            pl.BlockSpec((GDN_CONV, 128), lambda b, h: (0, h)),
            pl.BlockSpec((GDN_CONV, 128), lambda b, h: (0, GDN_HEADS + h)),
            pl.BlockSpec((GDN_CONV, 128), lambda b, h: (0, 2 * GDN_HEADS + h)),
            pl.BlockSpec((8, 128), lambda b, h: (0, 0)),
            pl.BlockSpec((1, 128), lambda b, h: (0, 0)),
            pl.BlockSpec((s, 128), lambda b, h: (b, h)),
            pl.BlockSpec((None, None, nc, GDN_DK, GDN_DV), lambda b, h: (b, h, 0, 0, 0)),
        ] + [pl.BlockSpec((s, 128), lambda b, h: (b, h))] * 5 + r_in,
        out_specs=[
            pl.BlockSpec((s, 512), lambda b, h: (b, h)),
            pl.BlockSpec((s, 128), lambda b, h: (b, 0)),
            cw_out, cw_out, cw_out, part, part,
        ] + r_out,
        out_shape=[
            jax.ShapeDtypeStruct((t, P_WIDTH), BF16),
            jax.ShapeDtypeStruct((t, 128), F32),
            jax.ShapeDtypeStruct((bl, GDN_CONV, 512), F32),
            jax.ShapeDtypeStruct((bl, GDN_CONV, 512), F32),
            jax.ShapeDtypeStruct((bl, GDN_CONV, 512), F32),
            jax.ShapeDtypeStruct((bl, GDN_HEADS, 8, 128), F32),
            jax.ShapeDtypeStruct((bl, GDN_HEADS, 8, 128), F32),
        ] + r_shapes,
        scratch_shapes=[pltpu.VMEM((s, 128), F32)] * 5 + r_sems,
        compiler_params=_params(("arbitrary", "arbitrary")),
    )(proj, proj, conv_w, conv_w, conv_w, sc, norm_g, o_raw, states, dcat, *qkvg, *r_args)
    return tuple(outs[:7]) + (list(outs[7:]),)


def _mla_prep_fwd(proj, qg, kvg, wq, wkv, cos, sin, s, tm):
    t = proj.shape[0]
    tm = min(tm, s)
    nps = s // tm
    const = lambda shape: pl.BlockSpec(shape, lambda i: (0, 0))

    def body(pm_ref, qg_ref, kvg_ref, wq_ref, wkv_ref, cos_ref, sin_ref, qf_ref, kvf_ref, kr_ref):
        qf, kvf, kr = _mla_prep(pm_ref[...], qg_ref[...], kvg_ref[...], wq_ref[...], wkv_ref[...], cos_ref[...], sin_ref[...])
        qf_ref[...], kvf_ref[...], kr_ref[...] = qf.astype(BF16), kvf.astype(BF16), kr.astype(BF16)

    return pl.pallas_call(
        body, name="mla_prep_fwd", grid=(t // tm,),
        in_specs=[
            pl.BlockSpec((tm, 1024), lambda i: (i, P_MLA // 1024)),
            const((1, MLA_Q_LORA)), const((1, MLA_KV_LORA)), const(wq.shape), const(wkv.shape),
            pl.BlockSpec((tm, 128), lambda i: (i % nps, 0)), pl.BlockSpec((tm, 128), lambda i: (i % nps, 0)),
        ],
        out_specs=[pl.BlockSpec((tm, 1024), lambda i: (i, 0)), pl.BlockSpec((tm, 1024), lambda i: (i, 0)),
                   pl.BlockSpec((tm, 128), lambda i: (i, 0))],
        out_shape=[jax.ShapeDtypeStruct((t, 1024), BF16), jax.ShapeDtypeStruct((t, 1024), BF16),
                   jax.ShapeDtypeStruct((t, 128), BF16)],
        compiler_params=_params(("parallel",)),
    )(proj, qg, kvg, wq, wkv, cos, sin)


def _mla_prep_bwd(proj, qg, kvg, wq, wkv, cos, sin, dqf, dkvf, dkr, dab, dproj, s, tm):
    t = proj.shape[0]
    tm = min(tm, s)
    nps = s // tm
    const = lambda shape: pl.BlockSpec(shape, lambda i: (0, 0))

    def body(pm_ref, qg_ref, kvg_ref, wq_ref, wkv_ref, cos_ref, sin_ref, dqf_ref, dkvf_ref, dkr_ref, dab_ref, dp_in,
             dp_ref, dqg_ref, dkvg_ref, dwq_ref, dwkv_ref):
        del dp_in
        fn = lambda pm, qg_, kvg_, wq_, wkv_: _mla_prep(pm, qg_, kvg_, wq_, wkv_, cos_ref[...], sin_ref[...])
        _, vjp = jax.vjp(fn, pm_ref[...], qg_ref[...], kvg_ref[...], wq_ref[...].astype(F32), wkv_ref[...].astype(F32))
        dpm, dqg, dkvg, dwq, dwkv = vjp((dqf_ref[...], dkvf_ref[...], dkr_ref[...]))
        dp_ref[...] = jnp.concatenate([dab_ref[...], dpm[:, 128:]], axis=1).astype(BF16)

        @pl.when(pl.program_id(0) == 0)
        def _():
            dqg_ref[...] = jnp.zeros_like(dqg_ref)
            dkvg_ref[...] = jnp.zeros_like(dkvg_ref)
            dwq_ref[...] = jnp.zeros_like(dwq_ref)
            dwkv_ref[...] = jnp.zeros_like(dwkv_ref)

        dqg_ref[...] += dqg
        dkvg_ref[...] += dkvg
        dwq_ref[...] += dwq
        dwkv_ref[...] += dwkv

    rows = lambda w: pl.BlockSpec((tm, w), lambda i: (i, 0))
    return pl.pallas_call(
        body, name="mla_prep_bwd", grid=(t // tm,),
        in_specs=[
            pl.BlockSpec((tm, 1024), lambda i: (i, P_MLA // 1024)),
            const((1, MLA_Q_LORA)), const((1, MLA_KV_LORA)), const(wq.shape), const(wkv.shape),
            pl.BlockSpec((tm, 128), lambda i: (i % nps, 0)), pl.BlockSpec((tm, 128), lambda i: (i % nps, 0)),
            rows(1024), rows(1024), rows(128), rows(128),
            pl.BlockSpec(memory_space=pl.ANY),
        ],
        out_specs=[pl.BlockSpec((tm, 1024), lambda i: (i, P_MLA // 1024)),
                   const((1, MLA_Q_LORA)), const((1, MLA_KV_LORA)), const(wq.shape), const(wkv.shape)],
        out_shape=[jax.ShapeDtypeStruct(dproj.shape, dproj.dtype),
                   jax.ShapeDtypeStruct((1, MLA_Q_LORA), F32), jax.ShapeDtypeStruct((1, MLA_KV_LORA), F32),
                   jax.ShapeDtypeStruct(wq.shape, F32), jax.ShapeDtypeStruct(wkv.shape, F32)],
        input_output_aliases={11: 0},
        compiler_params=_params(("arbitrary",)),
    )(proj, qg, kvg, wq, wkv, cos, sin, dqf, dkvf, dkr, dab, dproj)


def _attn_fwd(qf, kvf, kr, cat, bl, s, tq):
    tq = min(tq, s)
    nq = s // tq

    def body(q_ref, kv_ref, kr_ref, cat_in, o_ref):
        del cat_in
        for i in range(nq):
            rows, keys = slice(i * tq, (i + 1) * tq), slice(0, (i + 1) * tq)
            o = _attn_block(q_ref[rows, :], kv_ref[keys, 0:128], kr_ref[keys, :], kv_ref[keys, 128:256], i * tq)
            o_ref[rows, :] = o.astype(o_ref.dtype)

    return pl.pallas_call(
        body, name="attn_fwd", grid=(bl, MLA_HEADS),
        in_specs=[
            pl.BlockSpec((s, 256), lambda b, h: (b, h)),
            pl.BlockSpec((s, 256), lambda b, h: (b, h)),
            pl.BlockSpec((s, 128), lambda b, h: (b, 0)),
            pl.BlockSpec(memory_space=pl.ANY),
        ],
        out_specs=pl.BlockSpec((s, 128), lambda b, h: (b, GDN_HEADS + h)),
        out_shape=jax.ShapeDtypeStruct(cat.shape, cat.dtype),
        input_output_aliases={3: 0},
        compiler_params=_params(("parallel", "parallel")),
    )(qf, kvf, kr, cat)


def _attn_bwd(qf, kvf, kr, dcat, bl, s, tq):
    tq = min(tq, s)
    nq = s // tq

    def body(q_ref, kv_ref, kr_ref, do_ref, dq_ref, dkv_ref, dkr_ref):
        dkv_ref[...] = jnp.zeros_like(dkv_ref)

        @pl.when(pl.program_id(1) == 0)
        def _():
            dkr_ref[...] = jnp.zeros_like(dkr_ref)

        for i in range(nq):
            rows, keys = slice(i * tq, (i + 1) * tq), slice(0, (i + 1) * tq)
            fn = functools.partial(_attn_block, q0=i * tq)
            f32 = lambda a: a.astype(F32)
            _, vjp = jax.vjp(fn, f32(q_ref[rows, :]), f32(kv_ref[keys, 0:128]), f32(kr_ref[keys, :]), f32(kv_ref[keys, 128:256]))
            dq_ref[rows, :], dkn, dkr, dv = vjp(do_ref[rows, :])
            dkv_ref[keys, 0:128] += dkn
            dkv_ref[keys, 128:256] += dv
            dkr_ref[keys, :] += dkr

    t = bl * s
    return pl.pallas_call(
        body, name="attn_bwd", grid=(bl, MLA_HEADS),
        in_specs=[
            pl.BlockSpec((s, 256), lambda b, h: (b, h)),
            pl.BlockSpec((s, 256), lambda b, h: (b, h)),
            pl.BlockSpec((s, 128), lambda b, h: (b, 0)),
            pl.BlockSpec((s, 128), lambda b, h: (b, GDN_HEADS + h)),
        ],
        out_specs=[
            pl.BlockSpec((s, 256), lambda b, h: (b, h)),
            pl.BlockSpec((s, 256), lambda b, h: (b, h)),
            pl.BlockSpec((s, 128), lambda b, h: (b, 0)),
        ],
        out_shape=[jax.ShapeDtypeStruct((t, 1024), F32), jax.ShapeDtypeStruct((t, 1024), F32),
                   jax.ShapeDtypeStruct((t, 128), F32)],
        compiler_params=_params(("parallel", "arbitrary")),
    )(qf, kvf, kr, dcat)


def _ln1_fwd(x, mix, g, b, tm):
    t = x.shape[0]
    tm = min(tm, t)

    def body(x_ref, mix_ref, g_ref, b_ref, r_ref, h_ref, hb_ref, xb_ref):
        r = ALPHA * x_ref[...] + mix_ref[...]
        r_ref[...] = r
        h = _layernorm(r, g_ref[...], b_ref[...])
        h_ref[...] = h
        hb_ref[...] = h.astype(BF16)
        xb_ref[...] = x_ref[...].astype(BF16)

    rows = pl.BlockSpec((tm, D_MODEL), lambda i: (i, 0))
    vec = pl.BlockSpec((1, D_MODEL), lambda i: (0, 0))
    return pl.pallas_call(
        body, name="ln1_fwd", grid=(t // tm,), in_specs=[rows, rows, vec, vec], out_specs=[rows] * 4,
        out_shape=[jax.ShapeDtypeStruct(x.shape, F32)] * 2 + [jax.ShapeDtypeStruct(x.shape, BF16)] * 2,
        compiler_params=_params(("parallel",)),
    )(x, mix, g, b)


def _ln1_bwd(r1, dr2, da, db_, g, b, tm):
    t = r1.shape[0]
    tm = min(tm, t)

    def body(r_ref, d2_ref, da_ref, db_ref, g_ref, b_ref, dr_ref, drb_ref, dg_ref, dbias_ref):
        dh = ALPHA * d2_ref[...] + da_ref[...] + db_ref[...]
        _, vjp = jax.vjp(_layernorm, r_ref[...], g_ref[...], b_ref[...])
        dr, dg, dbias = vjp(dh)
        dr_ref[...] = dr
        drb_ref[...] = dr.astype(BF16)

        @pl.when(pl.program_id(0) == 0)
        def _():
            dg_ref[...] = jnp.zeros_like(dg_ref)
            dbias_ref[...] = jnp.zeros_like(dbias_ref)

        dg_ref[...] += dg
        dbias_ref[...] += dbias

    rows = pl.BlockSpec((tm, D_MODEL), lambda i: (i, 0))
    vec = pl.BlockSpec((1, D_MODEL), lambda i: (0, 0))
    return pl.pallas_call(
        body, name="ln1_bwd", grid=(t // tm,), in_specs=[rows] * 4 + [vec, vec], out_specs=[rows, rows, vec, vec],
        out_shape=[jax.ShapeDtypeStruct(r1.shape, F32), jax.ShapeDtypeStruct(r1.shape, BF16)]
        + [jax.ShapeDtypeStruct((1, D_MODEL), F32)] * 2,
        compiler_params=_params(("arbitrary",)),
    )(r1, dr2, da, db_, g, b)


def _ffn_act_fwd(u, conv_w, conv_b, bl, s, cb):
    nj = D_FF // cb

    def body(ug_ref, uu_ref, wg_ref, wu_ref, bg_ref, bu_ref, act_ref):
        def store(rows, act):
            act_ref[rows, :] = act.astype(BF16)

        _rows_apply(_ffn_act, [lambda r: ug_ref[r, :], lambda r: uu_ref[r, :]],
                    [wg_ref[...], wu_ref[...], bg_ref[...], bu_ref[...]], store, s, cb, CONV_HALO)

    return pl.pallas_call(
        body, name="ffn_act_fwd", grid=(bl, nj),
        in_specs=[
            pl.BlockSpec((s, cb), lambda b, j: (b, j)), pl.BlockSpec((s, cb), lambda b, j: (b, nj + j)),
            pl.BlockSpec((FFN_CONV, cb), lambda b, j: (0, j)), pl.BlockSpec((FFN_CONV, cb), lambda b, j: (0, nj + j)),
            pl.BlockSpec((1, cb), lambda b, j: (0, j)), pl.BlockSpec((1, cb), lambda b, j: (0, nj + j)),
        ],
        out_specs=pl.BlockSpec((s, cb), lambda b, j: (b, j)),
        out_shape=jax.ShapeDtypeStruct((bl * s, D_FF), BF16),
        compiler_params=_params(("parallel", "parallel")),
    )(u, u, conv_w, conv_w, conv_b, conv_b)


def _ffn_act_bwd(u, conv_w, conv_b, dact, bl, s, cb):
    nj = D_FF // cb

    def body(ug_ref, uu_ref, wg_ref, wu_ref, bg_ref, bu_ref, da_ref, du_ref, dwg_ref, dwu_ref, dbg_ref, dbu_ref, acc):
        def store_into(half):
            def store(rows, value, add):
                if add:
                    acc[half, rows, :] += value
                else:
                    acc[half, rows, :] = value
            return store

        dwg_ref[...], dwu_ref[...], dbg_ref[...], dbu_ref[...] = _rows_vjp(
            _ffn_act, [lambda r: ug_ref[r, :], lambda r: uu_ref[r, :]], [wg_ref[...], wu_ref[...], bg_ref[...], bu_ref[...]],
            lambda r: da_ref[r, :], [store_into(0), store_into(1)], s, cb, CONV_HALO)
        du_ref[...] = acc[...].astype(BF16)

    t = bl * s
    blk = pl.BlockSpec((s, cb), lambda b, j: (b, j))
    wpart = pl.BlockSpec((None, FFN_CONV, cb), lambda b, j: (b, 0, j))
    bpart = pl.BlockSpec((None, 1, cb), lambda b, j: (b, 0, j))
    return pl.pallas_call(
        body, name="ffn_act_bwd", grid=(bl, nj),
        in_specs=[
            blk, pl.BlockSpec((s, cb), lambda b, j: (b, nj + j)),
            pl.BlockSpec((FFN_CONV, cb), lambda b, j: (0, j)), pl.BlockSpec((FFN_CONV, cb), lambda b, j: (0, nj + j)),
            pl.BlockSpec((1, cb), lambda b, j: (0, j)), pl.BlockSpec((1, cb), lambda b, j: (0, nj + j)),
            blk,
        ],
        out_specs=[pl.BlockSpec((2, s, cb), lambda b, j: (0, b, j)), wpart, wpart, bpart, bpart],
        out_shape=[jax.ShapeDtypeStruct((2, t, D_FF), BF16)] + [jax.ShapeDtypeStruct((bl, FFN_CONV, D_FF), F32)] * 2
        + [jax.ShapeDtypeStruct((bl, 1, D_FF), F32)] * 2,
        scratch_shapes=[pltpu.VMEM((2, s, cb), F32)],
        compiler_params=_params(("parallel", "parallel")),
    )(u, u, conv_w, conv_w, conv_b, conv_b, dact)


def _head(h1, ffn, gpre, pp, bgate, g2, b2, target, tm):
    t = h1.shape[0]
    tm = min(tm, t)

    def body(h1_ref, ffn_ref, gp_ref, pp_ref, bg_ref, g2_ref, b2_ref, tg_ref,
             dr_ref, drb_ref, dgp_ref, dpp_ref, loss_ref, dbg_ref, dg2_ref, db2_ref):
        fn = functools.partial(_head_loss, target=tg_ref[...])
        loss, vjp = jax.vjp(fn, h1_ref[...], ffn_ref[...], gp_ref[...], pp_ref[...], bg_ref[...], g2_ref[...], b2_ref[...])
        _, dffn, dgp, dpp, dbg, dg2, db2 = vjp(jnp.ones((1, 1), F32))
        dr_ref[...] = dffn
        drb_ref[...], dgp_ref[...], dpp_ref[...] = dffn.astype(BF16), dgp.astype(BF16), dpp.astype(BF16)

        @pl.when(pl.program_id(0) == 0)
        def _():
            loss_ref[...] = jnp.zeros_like(loss_ref)
            dbg_ref[...] = jnp.zeros_like(dbg_ref)
            dg2_ref[...] = jnp.zeros_like(dg2_ref)
            db2_ref[...] = jnp.zeros_like(db2_ref)

        loss_ref[...] += jnp.broadcast_to(loss, loss_ref.shape)
        dbg_ref[...] += dbg
        dg2_ref[...] += dg2
        db2_ref[...] += db2

    rows = pl.BlockSpec((tm, D_MODEL), lambda i: (i, 0))
    vec = pl.BlockSpec((1, D_MODEL), lambda i: (0, 0))
    return pl.pallas_call(
        body, name="head", grid=(t // tm,), in_specs=[rows] * 4 + [vec] * 3 + [rows],
        out_specs=[rows] * 4 + [pl.BlockSpec((8, 128), lambda i: (0, 0))] + [vec] * 3,
        out_shape=[jax.ShapeDtypeStruct(h1.shape, F32)] + [jax.ShapeDtypeStruct(h1.shape, BF16)] * 3
        + [jax.ShapeDtypeStruct((8, 128), F32)]
        + [jax.ShapeDtypeStruct((1, D_MODEL), F32)] * 3,
        compiler_params=_params(("arbitrary",)),
    )(h1, ffn, gpre, pp, bgate, g2, b2, target)


def _adam_update(g, w_ref, m_ref, v_ref, g_ref, d_ref, nm_ref, nv_ref):
    m2 = ADAM_B1 * m_ref[...] + (1.0 - ADAM_B1) * g
    v2 = ADAM_B2 * v_ref[...] + (1.0 - ADAM_B2) * jnp.square(g)
    m_hat = m2 / (1.0 - ADAM_B1 ** ADAM_STEP)
    v_hat = v2 / (1.0 - ADAM_B2 ** ADAM_STEP)
    g_ref[...] = g
    d_ref[...] = -ADAM_LR * (m_hat / (jnp.sqrt(v_hat) + ADAM_EPS) + ADAM_WD * w_ref[...])
    nm_ref[...] = m2
    nv_ref[...] = v2


def _row_tile(rows, cols, limit_bytes=256 * 1024):
    best = None
    for t in range(8, rows + 1, 8):
        if rows % t == 0 and t * cols * 4 <= limit_bytes:
            best = t
    return best or rows


def _adamw_reduced(own, recv, w, m, v, name):
    a, b = w.shape
    ta = _row_tile(a, b)

    def body(own_ref, recv_ref, w_ref, m_ref, v_ref, g_ref, d_ref, nm_ref, nv_ref):
        c = lax.axis_index("c")
        for core in range(2):
            @pl.when(c == core)
            def _():
                got = [recv_ref[k].astype(F32) for k in range(N_DEV - 1)]
                same = [own_ref[...].astype(F32), got[0], got[1], got[2]]
                other = got[3:]
                core0, core1 = (same, other) if core == 0 else (other, same)
                g = core0[0] + core1[0]
                for r in range(1, N_CHIPS):
                    g = (g + core0[r]) + core1[r]
                _adam_update(g, w_ref, m_ref, v_ref, g_ref, d_ref, nm_ref, nv_ref)

    blk = pl.BlockSpec((ta, b), lambda i: (i, 0))
    return pl.pallas_call(
        body, name=name, grid=(a // ta,),
        in_specs=[blk, pl.BlockSpec((7, ta, b), lambda i: (0, i, 0)), blk, blk, blk], out_specs=[blk] * 4,
        out_shape=[jax.ShapeDtypeStruct(w.shape, F32)] * 4, compiler_params=_params(("parallel",)),
    )(own, recv, w, m, v)


def _adamw_small(g, w, m, v):
    def body(g_in, w_ref, m_ref, v_ref, g_ref, d_ref, nm_ref, nv_ref):
        _adam_update(g_in[...], w_ref, m_ref, v_ref, g_ref, d_ref, nm_ref, nv_ref)

    blk = pl.BlockSpec(w.shape, lambda i: (0, 0))
    return pl.pallas_call(
        body, name="adamw_small", grid=(1,), in_specs=[blk] * 4, out_specs=[blk] * 4,
        out_shape=[jax.ShapeDtypeStruct(w.shape, F32)] * 4, compiler_params=_params(("arbitrary",)),
    )(g, w, m, v)


def _remote(src, dst, send_sem, recv_sem, device):
    return pltpu.make_async_remote_copy(src_ref=src, dst_ref=dst, send_sem=send_sem, recv_sem=recv_sem,
                                        device_id=device, device_id_type=MESH)


def _place():
    x, y, c = lax.axis_index("x"), lax.axis_index("y"), lax.axis_index("c")
    return x, y, c, 2 * x + y, [(1 - x, y), (x, 1 - y), (1 - x, 1 - y)]


HBM_REF = pl.BlockSpec(memory_space=pl.ANY)
HALF_ROWS_QUANTUM = 16


def _gather_sems(n):
    return [pltpu.SemaphoreType.DMA((3 * n,))] * 4 + [pltpu.SemaphoreType.DMA((n,))]


def _gather_copies(ins, outs, sems):
    send_s, recv_s, fsend_s, frecv_s, local_s = sems
    x, y, c, me, chips = _place()
    local, sends, steps = [], [], []
    for i, (src, dst) in enumerate(zip(ins, outs)):
        local.append(pltpu.make_async_copy(src, dst.at[me], local_s.at[i]))
        half = src.shape[0] // 2
        split = src.shape[0] % (2 * HALF_ROWS_QUANTUM) == 0
        if split:
            mine = pl.ds(pl.multiple_of(c * half, HALF_ROWS_QUANTUM), half)
            theirs = pl.ds(pl.multiple_of((1 - c) * half, HALF_ROWS_QUANTUM), half)
        for r, (px, py) in enumerate(chips):
            k, peer = 3 * i + r, 2 * px + py
            if split:
                sends.append(_remote(src.at[mine], dst.at[me, mine], send_s.at[k], recv_s.at[k], (px, py, c)))
                landed = dst.at[peer, mine]
                steps.append((_remote(src.at[mine], landed, send_s.at[k], recv_s.at[k], (px, py, c)),
                              _remote(landed, landed, fsend_s.at[k], frecv_s.at[k], (x, y, 1 - c)),
                              _remote(dst.at[peer, theirs], dst.at[peer, theirs], fsend_s.at[k], frecv_s.at[k], (x, y, 1 - c))))
            else:
                sends.append(_remote(src, dst.at[me], send_s.at[k], recv_s.at[k], (px, py, c)))
                steps.append((_remote(src, dst.at[peer], send_s.at[k], recv_s.at[k], (px, py, c)), None, None))
    return local, sends, steps


def _scatter_sems(n):
    return [pltpu.SemaphoreType.DMA((4 * n,))] * 2 + [pltpu.SemaphoreType.DMA((3 * n,))] * 2


def _scatter_copies(ins, outs, sems):
    send_s, recv_s, fsend_s, frecv_s = sems
    x, y, c, me, chips = _place()
    sends, steps = [], []
    for i, (src, dst) in enumerate(zip(ins, outs)):
        for r, (px, py) in enumerate(chips):
            k = 4 * i + r
            cp = _remote(src.at[2 * px + py], dst.at[r], send_s.at[k], recv_s.at[k], (px, py, c))
            fwd = _remote(dst.at[r], dst.at[4 + r], fsend_s.at[3 * i + r], frecv_s.at[3 * i + r], (x, y, 1 - c))
            sends.append(cp)
            steps.append((cp, fwd, fwd))
        k = 4 * i + 3
        cp = _remote(src.at[me], dst.at[3], send_s.at[k], recv_s.at[k], (x, y, 1 - c))
        sends.append(cp)
        steps.append((cp, None, None))
    return [], sends, steps


def _exchange_start(plan):
    local, sends, _ = plan
    for cp in local + sends:
        cp.start()


def _exchange_pass_on(plan):
    for arrival, pass_on, _ in plan[2]:
        arrival.wait_recv()
        if pass_on is not None:
            pass_on.start()


def _exchange_finish(plan):
    local, sends, steps = plan
    for _, pass_on, passed in steps:
        if pass_on is not None:
            passed.wait_recv()
    for cp in sends:
        cp.wait_send()
    for _, pass_on, _ in steps:
        if pass_on is not None:
            pass_on.wait_send()
    for cp in local:
        cp.wait()


def _exchange_call(arrays, copies, sems, out_shapes, name):
    n = len(arrays)

    def body(*refs):
        plan = copies(refs[:n], refs[n:2 * n], refs[2 * n:])
        _exchange_start(plan)
        _exchange_pass_on(plan)
        _exchange_finish(plan)

    return pl.pallas_call(
        body, name=name, in_specs=[HBM_REF] * n, out_specs=[HBM_REF] * n, out_shape=out_shapes,
        scratch_shapes=sems(n), compiler_params=pltpu.CompilerParams(has_side_effects=True),
    )(*arrays)


def _gather_call(shards, name):
    shapes = [jax.ShapeDtypeStruct((N_CHIPS,) + a.shape, a.dtype) for a in shards]
    return _exchange_call(shards, _gather_copies, _gather_sems, shapes, name)


def _all_reduce_small(a):
    def body(in_ref, out_ref, slots, send_sems, recv_sems):
        x, y, c = lax.axis_index("x"), lax.axis_index("y"), lax.axis_index("c")
        me = 4 * x + 2 * y + c
        slots[0] = in_ref[...]
        sends = []
        for r in range(1, N_DEV):
            peer = (x ^ (r >> 2), y ^ ((r >> 1) & 1), c ^ (r & 1))
            sends.append(pltpu.make_async_remote_copy(src_ref=in_ref, dst_ref=slots.at[r], send_sem=send_sems.at[r],
                                                      recv_sem=recv_sems.at[r], device_id=peer, device_id_type=MESH))
        for cp in sends:
            cp.start()
        for cp in sends:
            cp.wait_recv()
        acc = slots[me]
        for dev in range(1, N_DEV):
            acc = acc + slots[dev ^ me]
        out_ref[...] = acc
        for cp in sends:
            cp.wait_send()

    return pl.pallas_call(
        body, name="small_all_reduce",
        in_specs=[pl.BlockSpec(memory_space=pltpu.VMEM)], out_specs=pl.BlockSpec(memory_space=pltpu.VMEM),
        out_shape=jax.ShapeDtypeStruct(a.shape, a.dtype),
        scratch_shapes=[pltpu.VMEM((N_DEV,) + a.shape, a.dtype), pltpu.SemaphoreType.DMA((N_DEV,)),
                        pltpu.SemaphoreType.DMA((N_DEV,))],
        compiler_params=pltpu.CompilerParams(has_side_effects=True),
    )(a)


SHARDED = ["w_in", "mla_w_q_up", "mla_w_kv_up", "w_out", "ffn_w_up", "ffn_w_down", "ple_w_gate", "ple_w_proj",
           "gdn_conv_w", "ffn_conv_w"]
SHARD_AXIS = {"w_in": 1, "mla_w_q_up": 1, "mla_w_kv_up": 1, "w_out": 0, "ffn_w_up": 1, "ffn_w_down": 0,
              "ple_w_gate": 0, "ple_w_proj": 1, "gdn_conv_w": 1, "ffn_conv_w": 1}
SMALL = ["gdn_a_log", "gdn_dt_bias", "gdn_norm_g", "mla_q_norm_g", "mla_kv_norm_g", "ln1_g", "ln1_b", "ffn_conv_b",
         "ple_b_gate", "ln2_g", "ln2_b"]
WEIGHTS = ["w_in", "gdn_conv_w", "gdn_a_log", "gdn_dt_bias", "gdn_norm_g", "mla_q_norm_g", "mla_w_q_up", "mla_kv_norm_g",
           "mla_w_kv_up", "w_out", "ln1_g", "ln1_b", "ffn_w_up", "ffn_conv_w", "ffn_conv_b", "ffn_w_down", "ple_w_gate",
           "ple_b_gate", "ple_w_proj", "ln2_g", "ln2_b"]
F32_ON_WIRE = ("gdn_conv_w", "ffn_conv_w")
GATHER_EARLY = ["w_in", "gdn_conv_w", "mla_w_q_up", "mla_w_kv_up"]
GATHER_LATE = ["w_out", "ffn_w_up", "ffn_conv_w", "ffn_w_down", "ple_w_gate", "ple_w_proj"]
SCATTER_EARLY = ["ffn_w_up", "ffn_conv_w", "ffn_w_down", "ple_w_gate", "ple_w_proj", "w_out"]
SCATTER_LATE = ["w_in", "gdn_conv_w", "mla_w_q_up", "mla_w_kv_up"]
PACK_COLS = 1024
PACK_ROW_TILE = 8


def _join_blocks(blocks, axis):
    n, a, b = blocks.shape
    if axis == 0:
        return blocks.reshape(n * a, b)
    return jnp.transpose(blocks, (1, 0, 2)).reshape(a, n * b)


def _split_blocks(full, axis):
    if axis == 0:
        return full.reshape(N_CHIPS, full.shape[0] // N_CHIPS, full.shape[1])
    a, nb = full.shape
    return jnp.transpose(full.reshape(a, N_CHIPS, nb // N_CHIPS), (1, 0, 2))


def _pack(arrays):
    flat = jnp.concatenate([a.reshape(-1) for a in arrays])
    quantum = PACK_COLS * PACK_ROW_TILE
    padded = -(-flat.shape[0] // quantum) * quantum
    return jnp.pad(flat, (0, padded - flat.shape[0])).reshape(-1, PACK_COLS)


def _unpack(packed, shapes):
    flat = packed.reshape(-1)
    out, off = [], 0
    for shp in shapes:
        n = int(np.prod(shp))
        out.append(flat[off:off + n].reshape(shp))
        off += n
    return out


def kernel(x, p, w_in, gdn_conv_w, gdn_a_log, gdn_dt_bias, gdn_norm_g, mla_q_norm_g, mla_w_q_up, mla_kv_norm_g, mla_w_kv_up, w_out, ln1_g, ln1_b, ffn_w_up, ffn_conv_w, ffn_conv_b, ffn_w_down, ple_w_gate, ple_b_gate, ple_w_proj, ln2_g, ln2_b, loss_target, m_w_in, m_gdn_conv_w, m_gdn_a_log, m_gdn_dt_bias, m_gdn_norm_g, m_mla_q_norm_g, m_mla_w_q_up, m_mla_kv_norm_g, m_mla_w_kv_up, m_w_out, m_ln1_g, m_ln1_b, m_ffn_w_up, m_ffn_conv_w, m_ffn_conv_b, m_ffn_w_down, m_ple_w_gate, m_ple_b_gate, m_ple_w_proj, m_ln2_g, m_ln2_b, v_w_in, v_gdn_conv_w, v_gdn_a_log, v_gdn_dt_bias, v_gdn_norm_g, v_mla_q_norm_g, v_mla_w_q_up, v_mla_kv_norm_g, v_mla_w_kv_up, v_w_out, v_ln1_g, v_ln1_b, v_ffn_w_up, v_ffn_conv_w, v_ffn_conv_b, v_ffn_w_down, v_ple_w_gate, v_ple_b_gate, v_ple_w_proj, v_ln2_g, v_ln2_b):
    given = dict(locals())
    wsh = {n: given[n][0] for n in WEIGHTS}
    msh = {n: given["m_" + n][0] for n in WEIGHTS}
    vsh = {n: given["v_" + n][0] for n in WEIGHTS}
    bl, s, _ = x.shape
    t = bl * s
    xt = x.reshape(t, D_MODEL)
    pt = p.reshape(t, PLE_DIM)
    target = loss_target.reshape(t, D_MODEL)

    wire = lambda n: wsh[n] if n in F32_ON_WIRE else wsh[n].astype(BF16)
    early = _gather_call([wire(n) for n in GATHER_EARLY], "weights_gather_early")
    full = {n: _join_blocks(g, SHARD_AXIS[n]) for n, g in zip(GATHER_EARLY, early)}
    late_shards = [wire(n) for n in GATHER_LATE]
    late_ride = (late_shards, _gather_copies, _gather_sems,
                 [jax.ShapeDtypeStruct((N_CHIPS,) + a.shape, a.dtype) for a in late_shards])

    in_cols, q_cols = _w_in_cols(), _w_q_cols()
    w_in_p = _pad_cols(full["w_in"], in_cols)
    w_q_p = _pad_cols(full["mla_w_q_up"], q_cols)
    w_kv, gconv = full["mla_w_kv_up"], full["gdn_conv_w"]
    row = lambda a: a.reshape(1, -1)
    sc = jnp.zeros((8, 128), F32).at[0, :GDN_HEADS].set(wsh["gdn_a_log"]).at[1, :GDN_HEADS].set(wsh["gdn_dt_bias"])
    norm_g, qg, kvg = row(wsh["gdn_norm_g"]), row(wsh["mla_q_norm_g"]), row(wsh["mla_kv_norm_g"])
    g1, b1, g2, b2 = row(wsh["ln1_g"]), row(wsh["ln1_b"]), row(wsh["ln2_g"]), row(wsh["ln2_b"])
    fbias, bgate = row(wsh["ffn_conv_b"]), row(wsh["ple_b_gate"])

    inv = ROPE_THETA ** (-jnp.arange(0, MLA_ROPE, 2, dtype=F32) / MLA_ROPE)
    ang = jnp.arange(s, dtype=F32)[:, None] * inv[None, :]
    zero = jnp.zeros_like(ang)
    cos_t = jnp.concatenate([jnp.cos(ang), zero, jnp.cos(ang), zero], axis=1)
    sin_t = jnp.concatenate([-jnp.sin(ang), zero, jnp.sin(ang), zero], axis=1)

    proj = _matmul(xt, w_in_p, name="proj", tm=1024)
    cat, o_raw, states, qkvg, late = _gdn_fwd(proj, gconv, sc, norm_g, bl, s, late_ride)
    full.update({n: _join_blocks(g, SHARD_AXIS[n]) for n, g in zip(GATHER_LATE, late)})
    w_o, w_up, w_down = full["w_out"], full["ffn_w_up"], full["ffn_w_down"]
    w_gate, w_proj, fconv = full["ple_w_gate"], full["ple_w_proj"], full["ffn_conv_w"]
    qf, kvf, kr = _mla_prep_fwd(proj, qg, kvg, w_q_p, w_kv, cos_t, sin_t, s, 256)
    cat = _attn_fwd(qf, kvf, kr, cat, bl, s, 256)
    wide = dict(tm=1024, tn=1024)
    mix = _matmul(cat, w_o, name="mix", **wide)
    r1, h1, h1b, xb = _ln1_fwd(xt, mix, g1, b1, 256)
    u = _matmul(h1b, w_up, name="ffn_up", tm=1024, tn=1408)
    act = _ffn_act_fwd(u, fconv, fbias, bl, s, 256)
    ffn = _matmul(act, w_down, name="ffn_down", tk=1408, **wide)
    gpre = _matmul(h1b, w_gate, name="ple_gate", **wide)
    pp = _matmul(pt, w_proj, name="ple_proj", **wide)
    dr2, dr2b, dgpre, dpp, loss_acc, dbgate, dg2, db2 = _head(h1, ffn, gpre, pp, bgate, g2, b2, target, 256)

    dact = _matmul(dr2b, w_down, name="d_act", tb=True, tm=1024, tn=1408)
    long_k = dict(ta=True, tk=2048)
    d_w_down = _matmul(act, dr2b, name="dw_down", tm=1408, tn=1024, **long_k)
    du, dfcw_g, dfcw_u, dfcb_g, dfcb_u = _ffn_act_bwd(u, fconv, fbias, dact, bl, s, 256)
    dh1_a = _matmul(du, w_up, name="dh1_ffn", tb=True, tk=1408, a_halves=True, **wide)
    dh1_b = _matmul(dgpre, w_gate, name="dh1_ple", tb=True, **wide)
    d_w_up = _matmul(h1b, du, name="dw_up", tn=1408, b_halves=True, **long_k)
    d_w_gate = _matmul(h1b, dgpre, name="dw_gate", **long_k, **wide)
    d_w_proj = _matmul(pt, dpp, name="dw_proj", ta=True, tn=1024)
    dr1, dr1b, dg1, db1 = _ln1_bwd(r1, dr2, dh1_a, dh1_b, g1, b1, 256)
    dcat = _matmul(dr1b, w_o, name="d_cat", tb=True, **wide)
    d_w_o = _matmul(cat, dr1b, name="dw_out", **long_k, **wide)

    gfull = {
        "ffn_w_up": d_w_up, "ffn_w_down": d_w_down, "ple_w_gate": d_w_gate, "ple_w_proj": d_w_proj, "w_out": d_w_o,
        "ffn_conv_w": jnp.concatenate([jnp.sum(dfcw_g, 0), jnp.sum(dfcw_u, 0)], axis=1),
    }
    slabs = {n: _split_blocks(gfull[n], SHARD_AXIS[n]).astype(BF16) for n in SCATTER_EARLY}
    early_slabs = [slabs[n] for n in SCATTER_EARLY]
    early_ride = (early_slabs, _scatter_copies, _scatter_sems,
                  [jax.ShapeDtypeStruct((N_DEV - 1,) + a.shape[1:], a.dtype) for a in early_slabs])
    dproj, dab, dcwq, dcwk, dcwv, dsc, dng, early_recv = _gdn_bwd(proj, gconv, sc, norm_g, o_raw, states, qkvg, dcat, bl, s,
                                                                  early_ride)
    received = dict(zip(SCATTER_EARLY, early_recv))
    dqf, dkvf, dkr = _attn_bwd(qf, kvf, kr, dcat, bl, s, 256)
    dproj, dqg, dkvg, d_w_q_p, d_w_kv = _mla_prep_bwd(proj, qg, kvg, w_q_p, w_kv, cos_t, sin_t, dqf, dkvf, dkr, dab, dproj, s, 256)
    d_w_in_p = _matmul(xb, dproj, name="dw_in", **long_k, **wide)

    gfull.update({
        "w_in": _unpad_cols(d_w_in_p, in_cols, D_IN),
        "mla_w_q_up": _unpad_cols(d_w_q_p, q_cols, MLA_HEADS * (MLA_NOPE + MLA_ROPE)),
        "mla_w_kv_up": d_w_kv,
        "gdn_conv_w": jnp.concatenate([jnp.sum(dcwq, 0), jnp.sum(dcwk, 0), jnp.sum(dcwv, 0)], axis=1),
    })
    slabs.update({n: _split_blocks(gfull[n], SHARD_AXIS[n]).astype(BF16) for n in SCATTER_LATE})
    late_slabs = [slabs[n] for n in SCATTER_LATE]
    late_scatter = (late_slabs, _scatter_copies, _scatter_sems,
                    [jax.ShapeDtypeStruct((N_DEV - 1,) + a.shape[1:], a.dtype) for a in late_slabs])
    grad_x, late_recv = _matmul(dproj, w_in_p, name="d_x", tb=True, add=dr1, add_scale=ALPHA, ride=late_scatter, **wide)
    received.update(zip(SCATTER_LATE, late_recv))
    dsc_sum = jnp.sum(dsc, axis=(0, 1))
    gsmall = {
        "gdn_a_log": dsc_sum[0, :GDN_HEADS], "gdn_dt_bias": dsc_sum[1, :GDN_HEADS],
        "gdn_norm_g": jnp.sum(dng[:, :, 0, :], axis=(0, 1)),
        "mla_q_norm_g": dqg[0], "mla_kv_norm_g": dkvg[0], "ln1_g": dg1[0], "ln1_b": db1[0],
        "ffn_conv_b": jnp.concatenate([jnp.sum(dfcb_g, 0), jnp.sum(dfcb_u, 0)], axis=1)[0],
        "ple_b_gate": dbgate[0], "ln2_g": dg2[0], "ln2_b": db2[0],
    }

    me_chip = 2 * lax.axis_index("x") + lax.axis_index("y")
    big = [{}, {}, {}, {}]
    for n in SHARDED:
        own = lax.dynamic_index_in_dim(slabs[n], me_chip, 0, keepdims=False)
        for kind, val in enumerate(_adamw_reduced(own, received[n], wsh[n], msh[n], vsh[n], "adamw_" + n)):
            big[kind][n] = val

    small_shapes = [wsh[n].shape for n in SMALL]
    gsum = _all_reduce_small(_pack([gsmall[n] for n in SMALL]))
    spacks = _adamw_small(gsum, _pack([wsh[n] for n in SMALL]), _pack([msh[n] for n in SMALL]), _pack([vsh[n] for n in SMALL]))
    small = [dict(zip(SMALL, _unpack(pk, small_shapes))) for pk in spacks]

    loss = lax.psum(loss_acc[0, 0], ("x", "y", "c"))
    outs = [loss, grad_x.reshape(x.shape)]
    for kind in range(4):
        for n in WEIGHTS:
            val = big[kind][n] if n in big[kind] else small[kind][n]
            outs.append(val[None])
    return tuple(outs)
```

```python
import functools
import math

import numpy as np
import jax
import jax.numpy as jnp
from jax import lax
from jax.experimental import pallas as pl
from jax.experimental.pallas import tpu as pltpu

F32 = jnp.float32
BF16 = jnp.bfloat16

D_MODEL = 1024
CHUNK = 64
PLE_DIM = 256
GDN_HEADS = 4
GDN_DK = 128
GDN_DV = 128
GDN_CONV = 4
MLA_HEADS = 4
MLA_NOPE = 128
MLA_ROPE = 64
MLA_V = 128
MLA_Q_LORA = 384
MLA_KV_LORA = 256
ROPE_THETA = 10000.0
D_FF = 2816
FFN_CONV = 3
DEPTH = 1
ALPHA = (2.0 * DEPTH) ** 0.25
NORM_EPS = 1e-6
GDN_QK = GDN_HEADS * GDN_DK
GDN_VW = GDN_HEADS * GDN_DV
D_IN = 2 * GDN_QK + 2 * GDN_VW + 2 * GDN_HEADS + MLA_Q_LORA + MLA_KV_LORA + MLA_ROPE
ATT_SCALE = (MLA_NOPE + MLA_ROPE) ** -0.5

ADAM_LR = 0.001
ADAM_B1 = 0.9
ADAM_B2 = 0.999
ADAM_EPS = 1e-08
ADAM_WD = 0.01
ADAM_STEP = 10

LANES = 128
VMEM_LIMIT = 60 * 1024 * 1024
GDN_FWD_GROUP = 16
GDN_BWD_GROUP = 16
N_CHIPS = 4
N_DEV = 8

P_WIDTH = 3072
P_MLA = 2048
MESH = pl.DeviceIdType.MESH


def _rope_slot(j):
    return j if j < MLA_ROPE // 2 else 64 + (j - MLA_ROPE // 2)


def _w_in_cols():
    idx = -np.ones((P_WIDTH,), np.int64)
    for h in range(GDN_HEADS):
        base = h * 512
        idx[base:base + 128] = np.arange(128) + h * GDN_DK
        idx[base + 128:base + 256] = np.arange(128) + GDN_QK + h * GDN_DK
        idx[base + 256:base + 384] = np.arange(128) + 2 * GDN_QK + h * GDN_DV
        idx[base + 384:base + 512] = np.arange(128) + 2 * GDN_QK + GDN_VW + h * GDN_DV
    o_a = 2 * GDN_QK + 2 * GDN_VW
    idx[P_MLA:P_MLA + 2 * GDN_HEADS] = np.arange(2 * GDN_HEADS) + o_a
    o_cq = o_a + 2 * GDN_HEADS
    idx[P_MLA + 128:P_MLA + 512] = np.arange(MLA_Q_LORA) + o_cq
    o_ckv = o_cq + MLA_Q_LORA
    idx[P_MLA + 512:P_MLA + 768] = np.arange(MLA_KV_LORA) + o_ckv
    o_kr = o_ckv + MLA_KV_LORA
    for j in range(MLA_ROPE):
        idx[P_MLA + 768 + _rope_slot(j)] = o_kr + j
    return idx


def _w_q_cols():
    idx = -np.ones((MLA_HEADS * 256,), np.int64)
    for h in range(MLA_HEADS):
        o = h * (MLA_NOPE + MLA_ROPE)
        idx[h * 256:h * 256 + 128] = np.arange(128) + o
        for j in range(MLA_ROPE):
            idx[h * 256 + 128 + _rope_slot(j)] = o + MLA_NOPE + j
    return idx


def _pad_cols(w, idx):
    safe = np.where(idx >= 0, idx, 0)
    return jnp.where(jnp.asarray(idx >= 0)[None, :], w[:, safe], 0.0)


def _unpad_cols(wp, idx, n):
    inv = np.zeros((n,), np.int64)
    inv[idx[idx >= 0]] = np.nonzero(idx >= 0)[0]
    return wp[:, inv]


def _dot(a, b, ca, cb, precision=None):
    if precision is None:
        a = a.astype(BF16)
        b = b.astype(BF16)
    return lax.dot_general(a, b, (((ca,), (cb,)), ((), ())), preferred_element_type=F32, precision=precision)


@jax.custom_vjp
def mm(a, b):
    return _dot(a, b, 1, 0)


@jax.custom_vjp
def mm_nt(a, b):
    return _dot(a, b, 1, 1)


@jax.custom_vjp
def mm_tn(a, b):
    return _dot(a, b, 0, 0)


mm.defvjp(lambda a, b: (mm(a, b), (a, b)), lambda r, g: (mm_nt(g, r[1]), mm_tn(r[0], g)))
mm_nt.defvjp(lambda a, b: (mm_nt(a, b), (a, b)), lambda r, g: (mm(g, r[1]), mm_tn(g, r[0])))
mm_tn.defvjp(lambda a, b: (mm_tn(a, b), (a, b)), lambda r, g: (mm_nt(r[1], g), mm(r[0], g)))

def _split(a):
    hi = a.astype(BF16)
    return hi, (a - hi.astype(F32)).astype(BF16)


def _dot3(a, b, ca, cb):
    a_hi, a_lo = _split(a)
    b_hi, b_lo = _split(b)
    return (_dot(a_hi, b_hi, ca, cb) + _dot(a_hi, b_lo, ca, cb)) + _dot(a_lo, b_hi, ca, cb)


def _shift_rows(x, s):
    return x if s == 0 else pltpu.roll(x, s % x.shape[0], 0)


def _row(w, j):
    tap = lax.broadcasted_iota(jnp.int32, w.shape, 0)
    return jnp.sum(jnp.where(tap == j, w, 0.0), axis=0, keepdims=True)


@jax.custom_vjp
def dwconv(x, w):
    k = w.shape[0]
    y = _row(w, k - 1) * x
    for j in range(k - 1):
        y = y + _row(w, j) * _shift_rows(x, k - 1 - j)
    return y


def _dwconv_fwd(x, w):
    return dwconv(x, w), (x, w)


def _dwconv_bwd(res, dy):
    x, w = res
    k = w.shape[0]
    dx = _row(w, k - 1) * dy
    tap = lax.broadcasted_iota(jnp.int32, w.shape, 0)
    dw = jnp.where(tap == k - 1, jnp.sum(dy * x, axis=0, keepdims=True), 0.0)
    for j in range(k - 1):
        dx = dx + _row(w, j) * _shift_rows(dy, -(k - 1 - j))
        dw = dw + jnp.where(tap == j, jnp.sum(dy * _shift_rows(x, k - 1 - j), axis=0, keepdims=True), 0.0)
    return dx, dw


dwconv.defvjp(_dwconv_fwd, _dwconv_bwd)


@jax.custom_vjp
def rope128(x, cos, sin):
    return x * cos + pltpu.roll(x, 64, 1) * sin


rope128.defvjp(lambda x, c, s: (rope128(x, c, s), (c, s)),
               lambda r, g: (g * r[0] + pltpu.roll(g * r[1], 64, 1), jnp.zeros_like(r[0]), jnp.zeros_like(r[1])))


def _silu(x):
    return x * jax.nn.sigmoid(x)


def _softplus(x):
    return jnp.maximum(x, 0.0) + jnp.log(1.0 + jnp.exp(-jnp.abs(x)))


def _rmsnorm(x, g):
    return x * lax.rsqrt(jnp.mean(x * x, axis=-1, keepdims=True) + NORM_EPS) * g


def _layernorm(x, g, b):
    mu = jnp.mean(x, axis=-1, keepdims=True)
    xc = x - mu
    var = jnp.mean(xc * xc, axis=-1, keepdims=True)
    return xc * lax.rsqrt(var + NORM_EPS) * g + b


def _pick_lane(row, lane):
    idx = lax.broadcasted_iota(jnp.int32, row.shape, 1)
    return jnp.sum(jnp.where(idx == lane, row, 0.0), axis=1, keepdims=True)


def _gdn_q(pq, cw):
    h = _silu(dwconv(pq, cw))
    return h * lax.rsqrt(jnp.sum(h * h, axis=-1, keepdims=True) + NORM_EPS) * (GDN_DK ** -0.5)


def _gdn_k(pk, cw):
    h = _silu(dwconv(pk, cw))
    return h * lax.rsqrt(jnp.sum(h * h, axis=-1, keepdims=True) + NORM_EPS)


def _gdn_v(pv, cw):
    return _silu(dwconv(pv, cw))


def _gdn_gate(ab, sc, head):
    a = _pick_lane(ab, head)
    b = _pick_lane(ab, GDN_HEADS + head)
    a_log = _pick_lane(_row(sc, 0), head)
    dt_bias = _pick_lane(_row(sc, 1), head)
    beta = jax.nn.sigmoid(b)
    g = -jnp.exp(a_log) * _softplus(a + dt_bias)
    return _two_lanes(g, beta)


def _two_lanes(c0, c1):
    lane = lax.broadcasted_iota(jnp.int32, (c0.shape[0], LANES), 1)
    return jnp.where(lane == 0, c0, jnp.where(lane == 1, c1, 0.0))


def _inverse_group(lows):
    n = lows[0].shape[0]
    ii = lax.broadcasted_iota(jnp.int32, (n, n), 0)
    jj = lax.broadcasted_iota(jnp.int32, (n, n), 1)
    eye = jnp.where(ii == jj, 1.0, 0.0)
    invs = [eye - low for low in lows]
    powers = [_dot3(low, low, 1, 0) for low in lows]
    k = 2
    while k < n:
        invs = [inv + _dot3(inv, p, 1, 0) for inv, p in zip(invs, powers)]
        k *= 2
        if k < n:
            powers = [_dot3(p, p, 1, 0) for p in powers]
    return invs


@jax.custom_vjp
def solve_group(lows, rhss):
    return [_dot3(inv, rhs, 1, 0) for inv, rhs in zip(_inverse_group(lows), rhss)]


def _solve_group_fwd(lows, rhss):
    invs = _inverse_group(lows)
    xs = [_dot3(inv, rhs, 1, 0) for inv, rhs in zip(invs, rhss)]
    return xs, (invs, xs)


def _solve_group_bwd(res, dxs):
    invs, xs = res
    n = invs[0].shape[0]
    strict = lax.broadcasted_iota(jnp.int32, (n, n), 0) > lax.broadcasted_iota(jnp.int32, (n, n), 1)
    drhss = [_dot3(inv, dx, 0, 0) for inv, dx in zip(invs, dxs)]
    dlows = [jnp.where(strict, -_dot3(drhs, x, 1, 1), 0.0) for drhs, x in zip(drhss, xs)]
    return dlows, drhss


solve_group.defvjp(_solve_group_fwd, _solve_group_bwd)


def _gdn_local_group(qs, ks, vs, gbs):
    c = qs[0].shape[0]
    ii = lax.broadcasted_iota(jnp.int32, (c, c), 0)
    jj = lax.broadcasted_iota(jnp.int32, (c, c), 1)
    incl = ii >= jj
    gs = [_pick_lane(gb, 0) for gb in gbs]
    betas = [_pick_lane(gb, 1) for gb in gbs]
    g_rows = [jnp.sum(jnp.where(ii == jj, g, 0.0), axis=0, keepdims=True) for g in gs]
    gc_cols = [jnp.sum(jnp.where(incl, g_row, 0.0), axis=1, keepdims=True) for g_row in g_rows]
    gc_rows = [jnp.sum(jnp.where(jj >= ii, g, 0.0), axis=0, keepdims=True) for g in gs]
    decays = [jnp.where(incl, jnp.exp(jnp.where(incl, gc - gr, 0.0)), 0.0) for gc, gr in zip(gc_cols, gc_rows)]
    kbs = [k * beta for k, beta in zip(ks, betas)]
    lows = [jnp.where(ii > jj, mm_nt(kb, k) * decay, 0.0) for kb, k, decay in zip(kbs, ks, decays)]
    egs = [jnp.exp(gc) for gc in gc_cols]
    wus = solve_group(lows, [jnp.concatenate([kb * eg, v * beta], axis=1) for kb, eg, v, beta in zip(kbs, egs, vs, betas)])
    qks = [mm_nt(q, k) * decay for q, k, decay in zip(qs, ks, decays)]
    g_lasts = [jnp.sum(g_row, axis=1, keepdims=True) for g_row in g_rows]
    kds = [k * jnp.exp(gl - gc) for k, gl, gc in zip(ks, g_lasts, gc_cols)]
    ws, us = [wu[:, :GDN_DK] for wu in wus], [wu[:, GDN_DK:] for wu in wus]
    q_effs = [q * eg - mm(qk, w) for q, eg, qk, w in zip(qs, egs, qks, ws)]
    o_locals = [mm(qk, u) for qk, u in zip(qks, us)]
    mixes = [mm_tn(kd, w) for kd, w in zip(kds, ws)]
    adds = [mm_tn(kd, u) for kd, u in zip(kds, us)]
    return [(q_eff, o_loc, mix, add, jnp.exp(gl))
            for q_eff, o_loc, mix, add, gl in zip(q_effs, o_locals, mixes, adds, g_lasts)]


def _gdn_state_step(q_eff, o_local, mix, add, eg_last, state):
    return mm(q_eff, state) + o_local, state * eg_last - mm(mix, state) + add


def _gdn_post(o, z, norm_g):
    return _rmsnorm(o, norm_g) * _silu(z)


def _attn_block(q, kn, kr, v, q0):
    s = (mm_nt(q[:, :128], kn) + mm_nt(q[:, 128:], kr)) * ATT_SCALE
    qpos = q0 + lax.broadcasted_iota(jnp.int32, s.shape, 0)
    kpos = lax.broadcasted_iota(jnp.int32, s.shape, 1)
    shift = int(math.log2(CHUNK))
    allowed = (kpos >> shift) <= (qpos >> shift)
    s = jnp.where(allowed, s, -1e30)
    p = jnp.exp(s - jnp.max(s, axis=-1, keepdims=True))
    p = p / jnp.sum(p, axis=-1, keepdims=True)
    return mm(p, v)


def _mla_prep(pm, qg, kvg, wq, wkv, cos, sin):
    cq = pm[:, 128:512]
    ckv = pm[:, 512:768]
    qf = mm(_rmsnorm(cq, qg), wq)
    parts = []
    for h in range(MLA_HEADS):
        parts.append(qf[:, h * 256:h * 256 + 128])
        parts.append(rope128(qf[:, h * 256 + 128:h * 256 + 256], cos, sin))
    kvf = mm(_rmsnorm(ckv, kvg), wkv)
    return jnp.concatenate(parts, axis=1), kvf, rope128(pm[:, 768:896], cos, sin)


def _ffn_act(ug, uu, wg, wu, bg, bu):
    return _silu(dwconv(ug, wg) + bg) * (dwconv(uu, wu) + bu)


def _head_loss(h1, ffn, gpre, pp, bgate, g2, b2, target):
    gate = jax.nn.sigmoid(gpre + bgate)
    h2 = _layernorm(ALPHA * h1 + ffn + gate * pp, g2, b2)
    err = h2 - target
    return 0.5 * jnp.sum(jnp.sum(err * err, axis=1, keepdims=True), axis=0, keepdims=True) / D_MODEL


ROW_TILE_VREGS = 32
CONV_HALO = 16


def _rows_per_tile(n_rows, cols):
    tile = min(n_rows, ROW_TILE_VREGS * 8 * LANES // cols)
    assert n_rows % tile == 0 and tile % CONV_HALO == 0, (n_rows, cols)
    return tile


def _tile_inputs(loads, t0, first, halo, tile):
    if halo == 0:
        return [ld(pl.ds(t0, tile)) for ld in loads]
    if first:
        xs = [ld(pl.ds(0, tile)) for ld in loads]
        return [jnp.concatenate([jnp.zeros((halo, x.shape[1]), x.dtype), x], axis=0) for x in xs]
    return [ld(pl.ds(pl.multiple_of(t0 - halo, CONV_HALO), tile + halo)) for ld in loads]


def _rows_apply(fn, loads, consts, store, n_rows, cols, halo):
    tile = _rows_per_tile(n_rows, cols)

    def one(t0, first):
        y = fn(*_tile_inputs(loads, t0, first, halo, tile), *consts)
        store(pl.ds(t0, tile), y[halo:] if halo else y)

    one(0, True)

    def step(i, carry):
        one(pl.multiple_of(i * tile, tile), False)
        return carry

    lax.fori_loop(1, n_rows // tile, step, 0)


def _rows_vjp(fn, loads, consts, load_dy, stores, n_rows, cols, halo):
    tile = _rows_per_tile(n_rows, cols)

    def one(t0, first, dconsts):
        xs = _tile_inputs(loads, t0, first, halo, tile)
        _, vjp = jax.vjp(lambda *a: fn(*a)[halo:] if halo else fn(*a), *xs, *consts)
        grads = vjp(load_dy(pl.ds(t0, tile)))
        for st, dx in zip(stores, grads[:len(xs)]):
            st(pl.ds(t0, tile), dx[halo:] if halo else dx, False)
            if halo and not first:
                st(pl.ds(pl.multiple_of(t0 - halo, CONV_HALO), halo), dx[:halo], True)
        return tuple(a + b for a, b in zip(dconsts, grads[len(xs):]))

    dconsts = one(0, True, tuple(jnp.zeros_like(c) for c in consts))
    return lax.fori_loop(1, n_rows // tile, lambda i, dc: one(pl.multiple_of(i * tile, tile), False, dc), dconsts)


def _params(sem):
    return pltpu.CompilerParams(dimension_semantics=sem, vmem_limit_bytes=VMEM_LIMIT)


def _matmul(a, b, *, name, ta=False, tb=False, tm=512, tn=512, tk=1024, add=None, add_scale=1.0,
            a_halves=False, b_halves=False, ride=None, out_dtype=F32):
    assert not (a_halves and ta) and not (b_halves and tb)
    a_shape = (a.shape[1], 2 * a.shape[2]) if a_halves else a.shape
    b_shape = (b.shape[1], 2 * b.shape[2]) if b_halves else b.shape
    (k_dim, m) = a_shape if ta else a_shape[::-1]
    (n, k2) = b_shape if tb else b_shape[::-1]
    assert k_dim == k2, (a.shape, b.shape)
    tm, tn, tk = min(tm, m), min(tn, n), min(tk, k_dim)
    assert m % tm == 0 and n % tn == 0 and k_dim % tk == 0, (name, m, n, k_dim, tm, tn, tk)
    nk = k_dim // tk
    ca, cb = (0 if ta else 1), (1 if tb else 0)

    def body(*refs):
        if add is None:
            a_ref, b_ref, o_ref, acc = refs
        else:
            a_ref, b_ref, c_ref, o_ref, acc = refs
        kk = pl.program_id(2)

        @pl.when(kk == 0)
        def _():
            acc[...] = jnp.zeros_like(acc)

        acc[...] += _dot(a_ref[...], b_ref[...], ca, cb)

        @pl.when(kk == nk - 1)
        def _():
            r = acc[...]
            if add is not None:
                r = r + add_scale * c_ref[...]
            o_ref[...] = r.astype(out_dtype)

    spec = pl.BlockSpec
    a_spec = spec((tk, tm), lambda i, j, k: (k, i)) if ta else spec((tm, tk), lambda i, j, k: (i, k))
    b_spec = spec((tn, tk), lambda i, j, k: (j, k)) if tb else spec((tk, tn), lambda i, j, k: (k, j))
    if a_halves:
        kh = k_dim // 2 // tk
        assert kh * tk * 2 == k_dim
        a_spec = spec((None, tm, tk), lambda i, j, k: (k // kh, i, k % kh))
    if b_halves:
        nh = n // 2 // tn
        assert nh * tn * 2 == n
        b_spec = spec((None, tk, tn), lambda i, j, k: (j // nh, k, j % nh))
    in_specs = [a_spec, b_spec]
    args = [a, b]
    if add is not None:
        in_specs.append(pl.BlockSpec((tm, tn), lambda i, j, k: (i, j)))
        args.append(add)
    grid = (m // tm, n // tn, nk)
    r_in, r_out, r_shapes, r_sems, r_args = _ride_specs(ride)
    outs = pl.pallas_call(
        _riding(body, len(args), 1, 1, ride, grid), name=name, grid=grid,
        in_specs=in_specs + r_in, out_specs=[pl.BlockSpec((tm, tn), lambda i, j, k: (i, j))] + r_out,
        out_shape=[jax.ShapeDtypeStruct((m, n), out_dtype)] + r_shapes,
        scratch_shapes=[pltpu.VMEM((tm, tn), F32)] + r_sems,
        compiler_params=_params(("parallel", "parallel", "arbitrary") if ride is None else ("arbitrary",) * 3),
    )(*args, *r_args)
    return outs[0] if ride is None else (outs[0], list(outs[1:]))


def _riding(core, n_in, n_out, n_scratch, ride, grid):
    if ride is None:
        return core
    copies, nr = ride[1], len(ride[0])
    steps = int(np.prod(grid))
    pass_step = min(max(int(steps * ride[4]), 1), steps - 2)
    assert steps >= 3, grid

    def body(*refs):
        cuts = np.cumsum([0, n_in, nr, n_out, nr, n_scratch])
        ins, rin, outs, rout, scratch = (refs[a:b] for a, b in zip(cuts[:-1], cuts[1:]))
        sems = refs[cuts[-1]:]
        step = 0
        for axis, size in enumerate(grid):
            step = step * size + pl.program_id(axis)

        @pl.when(step == 0)
        def _():
            _exchange_start(copies(rin, rout, sems))

        @pl.when(step == pass_step)
        def _():
            _exchange_pass_on(copies(rin, rout, sems))

        core(*ins, *outs, *scratch)

        @pl.when(step == steps - 1)
        def _():
            _exchange_finish(copies(rin, rout, sems))

    return body


def _ride_specs(ride):
    if ride is None:
        return [], [], [], [], []
    arrays, _, sems, shapes, _ = ride
    return [HBM_REF] * len(arrays), [HBM_REF] * len(arrays), list(shapes), sems(len(arrays)), list(arrays)


def _gdn_fwd(proj, conv_w, sc, norm_g, bl, s, ride=None):
    nc = s // CHUNK

    def core(ph_ref, ab_ref, cwq_ref, cwk_ref, cwv_ref, sc_ref, ng_ref, cat_ref, o_ref, st_ref, q_s, k_s, v_s, gb_s):
        def into(ref):
            def store(rows, value):
                ref[rows, :] = value.astype(ref.dtype)
            return store

        for fn, col, cw_ref, val_s in [(_gdn_q, 0, cwq_ref, q_s), (_gdn_k, 128, cwk_ref, k_s), (_gdn_v, 256, cwv_ref, v_s)]:
            _rows_apply(fn, [lambda r, col=col: ph_ref[r, col:col + 128]], [cw_ref[...]], into(val_s), s, LANES, CONV_HALO)
        _rows_apply(functools.partial(_gdn_gate, head=pl.program_id(1)), [lambda r: ab_ref[r, :]], [sc_ref[...]], into(gb_s), s, LANES, 0)

        group = math.gcd(nc, GDN_FWD_GROUP)

        def chunks(i, state):
            ns = [i * group + j for j in range(group)]
            rows = [pl.ds(pl.multiple_of(n * CHUNK, CHUNK), CHUNK) for n in ns]
            local = _gdn_local_group([q_s[r, :] for r in rows], [k_s[r, :] for r in rows], [v_s[r, :] for r in rows],
                                     [gb_s[r, :] for r in rows])
            for n, r, loc in zip(ns, rows, local):
                st_ref[n] = state
                o_ref[r, :], state = _gdn_state_step(*loc, state)
            return state

        lax.fori_loop(0, nc // group, chunks, jnp.zeros((GDN_DK, GDN_DV), F32))
        _rows_apply(_gdn_post, [lambda r: o_ref[r, :], lambda r: ph_ref[r, 384:512]], [ng_ref[...]], into(cat_ref), s, LANES, 0)

    t = bl * s
    r_in, r_out, r_shapes, r_sems, r_args = _ride_specs(ride)
    outs = pl.pallas_call(
        _riding(core, 7, 7, 0, ride, (bl, GDN_HEADS)), name="gdn_fwd", grid=(bl, GDN_HEADS),
        in_specs=[
            pl.BlockSpec((s, 512), lambda b, h: (b, h)),
            pl.BlockSpec((s, 128), lambda b, h: (b, P_MLA // 128)),
            pl.BlockSpec((GDN_CONV, 128), lambda b, h: (0, h)),
            pl.BlockSpec((GDN_CONV, 128), lambda b, h: (0, GDN_HEADS + h)),
            pl.BlockSpec((GDN_CONV, 128), lambda b, h: (0, 2 * GDN_HEADS + h)),
            pl.BlockSpec((8, 128), lambda b, h: (0, 0)),
            pl.BlockSpec((1, 128), lambda b, h: (0, 0)),
        ] + r_in,
        out_specs=[
            pl.BlockSpec((s, 128), lambda b, h: (b, h)),
            pl.BlockSpec((s, 128), lambda b, h: (b, h)),
            pl.BlockSpec((None, None, nc, GDN_DK, GDN_DV), lambda b, h: (b, h, 0, 0, 0)),
        ] + [pl.BlockSpec((s, 128), lambda b, h: (b, h))] * 4 + r_out,
        out_shape=[
            jax.ShapeDtypeStruct((t, 2 * GDN_VW), BF16),
            jax.ShapeDtypeStruct((t, GDN_VW), F32),
            jax.ShapeDtypeStruct((bl, GDN_HEADS, nc, GDN_DK, GDN_DV), F32),
        ] + [jax.ShapeDtypeStruct((t, GDN_VW), F32)] * 4 + r_shapes,
        scratch_shapes=r_sems,
        compiler_params=_params(("arbitrary", "arbitrary")),
    )(proj, proj, conv_w, conv_w, conv_w, sc, norm_g, *r_args)
    return outs[0], outs[1], outs[2], tuple(outs[3:7]), list(outs[7:])


def _gdn_bwd(proj, conv_w, sc, norm_g, o_raw, states, qkvg, dcat, bl, s, ride=None):
    nc = s // CHUNK

    def core(ph_ref, ab_ref, cwq_ref, cwk_ref, cwv_ref, sc_ref, ng_ref, o_ref, st_ref, dc_ref, q_in, k_in, v_in, gb_in,
             dph_ref, dab_ref, dcwq_ref, dcwk_ref, dcwv_ref, dsc_ref, dng_ref, q_s, k_s, v_s, gb_s, do_s):
        head = pl.program_id(1)
        gate = functools.partial(_gdn_gate, head=head)
        paths = [(_gdn_q, 0, cwq_ref, q_s, dcwq_ref), (_gdn_k, 128, cwk_ref, k_s, dcwk_ref), (_gdn_v, 256, cwv_ref, v_s, dcwv_ref)]
        def into(ref, cols=slice(None)):
            def store(rows, value, add=False):
                if add:
                    ref[rows, cols] += value.astype(ref.dtype)
                else:
                    ref[rows, cols] = value.astype(ref.dtype)
            return store

        (dng,) = _rows_vjp(_gdn_post, [lambda r: o_ref[r, :], lambda r: ph_ref[r, 384:512]], [ng_ref[...]],
                           lambda r: dc_ref[r, :], [into(do_s), into(dph_ref, slice(384, 512))], s, LANES, 0)
        dng_ref[...] = jnp.broadcast_to(dng, dng_ref.shape)

        group = math.gcd(nc, GDN_BWD_GROUP)

        def chunks(i, dstate):
            ns = [nc - 1 - (i * group + j) for j in range(group)]
            rows = [pl.ds(pl.multiple_of(n * CHUNK, CHUNK), CHUNK) for n in ns]
            local, local_vjp = jax.vjp(_gdn_local_group, [q_in[r, :] for r in rows], [k_in[r, :] for r in rows],
                                       [v_in[r, :] for r in rows], [gb_in[r, :] for r in rows])
            d_os = [do_s[r, :] for r in rows]
            dlocal = []
            for n, loc, d_o in zip(ns, local, d_os):
                _, step_vjp = jax.vjp(_gdn_state_step, *loc, st_ref[n])
                *dloc, dstate = step_vjp((d_o, dstate))
                dlocal.append(tuple(dloc))
            dqs, dks, dvs, dgbs = local_vjp(dlocal)
            for r, dq, dk, dv, dgb in zip(rows, dqs, dks, dvs, dgbs):
                q_s[r, :], k_s[r, :], v_s[r, :], gb_s[r, :] = dq, dk, dv, dgb
            return dstate

        lax.fori_loop(0, nc // group, chunks, jnp.zeros((GDN_DK, GDN_DV), F32))
        for fn, col, cw_ref, val_s, dcw_ref in paths:
            (dcw_ref[...],) = _rows_vjp(fn, [lambda r, col=col: ph_ref[r, col:col + 128]], [cw_ref[...]],
                                        lambda r, val_s=val_s: val_s[r, :], [into(val_s)], s, LANES, CONV_HALO)
            dph_ref[:, col:col + 128] = val_s[...].astype(BF16)

        @pl.when(head == 0)
        def _():
            dab_ref[...] = jnp.zeros_like(dab_ref)

        def add_dab(rows, value, add=False):
            dab_ref[rows, :] += value

        (dsc_ref[...],) = _rows_vjp(gate, [lambda r: ab_ref[r, :]], [sc_ref[...]], lambda r: gb_s[r, :], [add_dab], s, LANES, 0)

    t = bl * s
    cw_out = pl.BlockSpec((None, GDN_CONV, 128), lambda b, h: (b, 0, h))
    part = pl.BlockSpec((None, None, 8, 128), lambda b, h: (b, h, 0, 0))
    r_in, r_out, r_shapes, r_sems, r_args = _ride_specs(ride)
    outs = pl.pallas_call(
        _riding(core, 14, 7, 5, ride, (bl, GDN_HEADS)), name="gdn_bwd", grid=(bl, GDN_HEADS),
        in_specs=[
            pl.BlockSpec((s, 512), lambda b, h: (b, h)),
            pl.BlockSpec((s, 128), lambda b, h: (b, P_MLA // 128)),
            pl.BlockSpec((GDN_CONV, 128), lambda b, h: (0, h)),
            pl.BlockSpec((GDN_CONV, 128), lambda b, h: (0, GDN_HEADS + h)),
            pl.BlockSpec((GDN_CONV, 128), lambda b, h: (0, 2 * GDN_HEADS + h)),
            pl.BlockSpec((8, 128), lambda b, h: (0, 0)),
            pl.BlockSpec((1, 128), lambda b, h: (0, 0)),
            pl.BlockSpec((s, 128), lambda b, h: (b, h)),
            pl.BlockSpec((None, None, nc, GDN_DK, GDN_DV), lambda b, h: (b, h, 0, 0, 0)),
        ] + [pl.BlockSpec((s, 128), lambda b, h: (b, h))] * 5 + r_in,
        out_specs=[
            pl.BlockSpec((s, 512), lambda b, h: (b, h)),
            pl.BlockSpec((s, 128), lambda b, h: (b, 0)),
            cw_out, cw_out, cw_out, part, part,
        ] + r_out,
        out_shape=[
            jax.ShapeDtypeStruct((t, P_WIDTH), BF16),
            jax.ShapeDtypeStruct((t, 128), F32),
            jax.ShapeDtypeStruct((bl, GDN_CONV, 512), F32),
            jax.ShapeDtypeStruct((bl, GDN_CONV, 512), F32),
            jax.ShapeDtypeStruct((bl, GDN_CONV, 512), F32),
            jax.ShapeDtypeStruct((bl, GDN_HEADS, 8, 128), F32),
            jax.ShapeDtypeStruct((bl, GDN_HEADS, 8, 128), F32),
        ] + r_shapes,
        scratch_shapes=[pltpu.VMEM((s, 128), F32)] * 5 + r_sems,
        compiler_params=_params(("arbitrary", "arbitrary")),
    )(proj, proj, conv_w, conv_w, conv_w, sc, norm_g, o_raw, states, dcat, *qkvg, *r_args)
    return tuple(outs[:7]) + (list(outs[7:]),)


def _mla_prep_fwd(proj, qg, kvg, wq, wkv, cos, sin, s, tm):
    t = proj.shape[0]
    tm = min(tm, s)
    nps = s // tm
    const = lambda shape: pl.BlockSpec(shape, lambda i: (0, 0))

    def body(pm_ref, qg_ref, kvg_ref, wq_ref, wkv_ref, cos_ref, sin_ref, qf_ref, kvf_ref, kr_ref):
        qf, kvf, kr = _mla_prep(pm_ref[...], qg_ref[...], kvg_ref[...], wq_ref[...], wkv_ref[...], cos_ref[...], sin_ref[...])
        qf_ref[...], kvf_ref[...], kr_ref[...] = qf.astype(BF16), kvf.astype(BF16), kr.astype(BF16)

    return pl.pallas_call(
        body, name="mla_prep_fwd", grid=(t // tm,),
        in_specs=[
            pl.BlockSpec((tm, 1024), lambda i: (i, P_MLA // 1024)),
            const((1, MLA_Q_LORA)), const((1, MLA_KV_LORA)), const(wq.shape), const(wkv.shape),
            pl.BlockSpec((tm, 128), lambda i: (i % nps, 0)), pl.BlockSpec((tm, 128), lambda i: (i % nps, 0)),
        ],
        out_specs=[pl.BlockSpec((tm, 1024), lambda i: (i, 0)), pl.BlockSpec((tm, 1024), lambda i: (i, 0)),
                   pl.BlockSpec((tm, 128), lambda i: (i, 0))],
        out_shape=[jax.ShapeDtypeStruct((t, 1024), BF16), jax.ShapeDtypeStruct((t, 1024), BF16),
                   jax.ShapeDtypeStruct((t, 128), BF16)],
        compiler_params=_params(("parallel",)),
    )(proj, qg, kvg, wq, wkv, cos, sin)


def _mla_prep_bwd(proj, qg, kvg, wq, wkv, cos, sin, dqf, dkvf, dkr, dab, dproj, s, tm):
    t = proj.shape[0]
    tm = min(tm, s)
    nps = s // tm
    const = lambda shape: pl.BlockSpec(shape, lambda i: (0, 0))

    def body(pm_ref, qg_ref, kvg_ref, wq_ref, wkv_ref, cos_ref, sin_ref, dqf_ref, dkvf_ref, dkr_ref, dab_ref, dp_in,
             dp_ref, dqg_ref, dkvg_ref, dwq_ref, dwkv_ref):
        del dp_in
        fn = lambda pm, qg_, kvg_, wq_, wkv_: _mla_prep(pm, qg_, kvg_, wq_, wkv_, cos_ref[...], sin_ref[...])
        _, vjp = jax.vjp(fn, pm_ref[...], qg_ref[...], kvg_ref[...], wq_ref[...].astype(F32), wkv_ref[...].astype(F32))
        dpm, dqg, dkvg, dwq, dwkv = vjp((dqf_ref[...], dkvf_ref[...], dkr_ref[...]))
        dp_ref[...] = jnp.concatenate([dab_ref[...], dpm[:, 128:]], axis=1).astype(BF16)

        @pl.when(pl.program_id(0) == 0)
        def _():
            dqg_ref[...] = jnp.zeros_like(dqg_ref)
            dkvg_ref[...] = jnp.zeros_like(dkvg_ref)
            dwq_ref[...] = jnp.zeros_like(dwq_ref)
            dwkv_ref[...] = jnp.zeros_like(dwkv_ref)

        dqg_ref[...] += dqg
        dkvg_ref[...] += dkvg
        dwq_ref[...] += dwq
        dwkv_ref[...] += dwkv

    rows = lambda w: pl.BlockSpec((tm, w), lambda i: (i, 0))
    return pl.pallas_call(
        body, name="mla_prep_bwd", grid=(t // tm,),
        in_specs=[
            pl.BlockSpec((tm, 1024), lambda i: (i, P_MLA // 1024)),
            const((1, MLA_Q_LORA)), const((1, MLA_KV_LORA)), const(wq.shape), const(wkv.shape),
            pl.BlockSpec((tm, 128), lambda i: (i % nps, 0)), pl.BlockSpec((tm, 128), lambda i: (i % nps, 0)),
            rows(1024), rows(1024), rows(128), rows(128),
            pl.BlockSpec(memory_space=pl.ANY),
        ],
        out_specs=[pl.BlockSpec((tm, 1024), lambda i: (i, P_MLA // 1024)),
                   const((1, MLA_Q_LORA)), const((1, MLA_KV_LORA)), const(wq.shape), const(wkv.shape)],
        out_shape=[jax.ShapeDtypeStruct(dproj.shape, dproj.dtype),
                   jax.ShapeDtypeStruct((1, MLA_Q_LORA), F32), jax.ShapeDtypeStruct((1, MLA_KV_LORA), F32),
                   jax.ShapeDtypeStruct(wq.shape, F32), jax.ShapeDtypeStruct(wkv.shape, F32)],
        input_output_aliases={11: 0},
        compiler_params=_params(("arbitrary",)),
    )(proj, qg, kvg, wq, wkv, cos, sin, dqf, dkvf, dkr, dab, dproj)


def _attn_fwd(qf, kvf, kr, cat, bl, s, tq):
    tq = min(tq, s)
    nq = s // tq

    def body(q_ref, kv_ref, kr_ref, cat_in, o_ref):
        del cat_in
        for i in range(nq):
            rows, keys = slice(i * tq, (i + 1) * tq), slice(0, (i + 1) * tq)
            o = _attn_block(q_ref[rows, :], kv_ref[keys, 0:128], kr_ref[keys, :], kv_ref[keys, 128:256], i * tq)
            o_ref[rows, :] = o.astype(o_ref.dtype)

    return pl.pallas_call(
        body, name="attn_fwd", grid=(bl, MLA_HEADS),
        in_specs=[
            pl.BlockSpec((s, 256), lambda b, h: (b, h)),
            pl.BlockSpec((s, 256), lambda b, h: (b, h)),
            pl.BlockSpec((s, 128), lambda b, h: (b, 0)),
            pl.BlockSpec(memory_space=pl.ANY),
        ],
        out_specs=pl.BlockSpec((s, 128), lambda b, h: (b, GDN_HEADS + h)),
        out_shape=jax.ShapeDtypeStruct(cat.shape, cat.dtype),
        input_output_aliases={3: 0},
        compiler_params=_params(("parallel", "parallel")),
    )(qf, kvf, kr, cat)


def _attn_bwd(qf, kvf, kr, dcat, bl, s, tq):
    tq = min(tq, s)
    nq = s // tq

    def body(q_ref, kv_ref, kr_ref, do_ref, dq_ref, dkv_ref, dkr_ref):
        dkv_ref[...] = jnp.zeros_like(dkv_ref)

        @pl.when(pl.program_id(1) == 0)
        def _():
            dkr_ref[...] = jnp.zeros_like(dkr_ref)

        for i in range(nq):
            rows, keys = slice(i * tq, (i + 1) * tq), slice(0, (i + 1) * tq)
            fn = functools.partial(_attn_block, q0=i * tq)
            f32 = lambda a: a.astype(F32)
            _, vjp = jax.vjp(fn, f32(q_ref[rows, :]), f32(kv_ref[keys, 0:128]), f32(kr_ref[keys, :]), f32(kv_ref[keys, 128:256]))
            dq_ref[rows, :], dkn, dkr, dv = vjp(do_ref[rows, :])
            dkv_ref[keys, 0:128] += dkn
            dkv_ref[keys, 128:256] += dv
            dkr_ref[keys, :] += dkr

    t = bl * s
    return pl.pallas_call(
        body, name="attn_bwd", grid=(bl, MLA_HEADS),
        in_specs=[
            pl.BlockSpec((s, 256), lambda b, h: (b, h)),
            pl.BlockSpec((s, 256), lambda b, h: (b, h)),
            pl.BlockSpec((s, 128), lambda b, h: (b, 0)),
            pl.BlockSpec((s, 128), lambda b, h: (b, GDN_HEADS + h)),
        ],
        out_specs=[
            pl.BlockSpec((s, 256), lambda b, h: (b, h)),
            pl.BlockSpec((s, 256), lambda b, h: (b, h)),
            pl.BlockSpec((s, 128), lambda b, h: (b, 0)),
        ],
        out_shape=[jax.ShapeDtypeStruct((t, 1024), F32), jax.ShapeDtypeStruct((t, 1024), F32),
                   jax.ShapeDtypeStruct((t, 128), F32)],
        compiler_params=_params(("parallel", "arbitrary")),
    )(qf, kvf, kr, dcat)


def _ln1_fwd(x, mix, g, b, tm):
    t = x.shape[0]
    tm = min(tm, t)

    def body(x_ref, mix_ref, g_ref, b_ref, r_ref, h_ref, hb_ref, xb_ref):
        r = ALPHA * x_ref[...] + mix_ref[...]
        r_ref[...] = r
        h = _layernorm(r, g_ref[...], b_ref[...])
        h_ref[...] = h
        hb_ref[...] = h.astype(BF16)
        xb_ref[...] = x_ref[...].astype(BF16)

    rows = pl.BlockSpec((tm, D_MODEL), lambda i: (i, 0))
    vec = pl.BlockSpec((1, D_MODEL), lambda i: (0, 0))
    return pl.pallas_call(
        body, name="ln1_fwd", grid=(t // tm,), in_specs=[rows, rows, vec, vec], out_specs=[rows] * 4,
        out_shape=[jax.ShapeDtypeStruct(x.shape, F32)] * 2 + [jax.ShapeDtypeStruct(x.shape, BF16)] * 2,
        compiler_params=_params(("parallel",)),
    )(x, mix, g, b)


def _ln1_bwd(r1, dr2, da, db_, g, b, tm):
    t = r1.shape[0]
    tm = min(tm, t)

    def body(r_ref, d2_ref, da_ref, db_ref, g_ref, b_ref, dr_ref, drb_ref, dg_ref, dbias_ref):
        dh = ALPHA * d2_ref[...] + da_ref[...] + db_ref[...]
        _, vjp = jax.vjp(_layernorm, r_ref[...], g_ref[...], b_ref[...])
        dr, dg, dbias = vjp(dh)
        dr_ref[...] = dr
        drb_ref[...] = dr.astype(BF16)

        @pl.when(pl.program_id(0) == 0)
        def _():
            dg_ref[...] = jnp.zeros_like(dg_ref)
            dbias_ref[...] = jnp.zeros_like(dbias_ref)

        dg_ref[...] += dg
        dbias_ref[...] += dbias

    rows = pl.BlockSpec((tm, D_MODEL), lambda i: (i, 0))
    vec = pl.BlockSpec((1, D_MODEL), lambda i: (0, 0))
    return pl.pallas_call(
        body, name="ln1_bwd", grid=(t // tm,), in_specs=[rows] * 4 + [vec, vec], out_specs=[rows, rows, vec, vec],
        out_shape=[jax.ShapeDtypeStruct(r1.shape, F32), jax.ShapeDtypeStruct(r1.shape, BF16)]
        + [jax.ShapeDtypeStruct((1, D_MODEL), F32)] * 2,
        compiler_params=_params(("arbitrary",)),
    )(r1, dr2, da, db_, g, b)


def _ffn_act_fwd(u, conv_w, conv_b, bl, s, cb):
    nj = D_FF // cb

    def body(ug_ref, uu_ref, wg_ref, wu_ref, bg_ref, bu_ref, act_ref):
        def store(rows, act):
            act_ref[rows, :] = act.astype(BF16)

        _rows_apply(_ffn_act, [lambda r: ug_ref[r, :].astype(F32), lambda r: uu_ref[r, :].astype(F32)],
                    [wg_ref[...], wu_ref[...], bg_ref[...], bu_ref[...]], store, s, cb, CONV_HALO)

    return pl.pallas_call(
        body, name="ffn_act_fwd", grid=(bl, nj),
        in_specs=[
            pl.BlockSpec((s, cb), lambda b, j: (b, j)), pl.BlockSpec((s, cb), lambda b, j: (b, nj + j)),
            pl.BlockSpec((FFN_CONV, cb), lambda b, j: (0, j)), pl.BlockSpec((FFN_CONV, cb), lambda b, j: (0, nj + j)),
            pl.BlockSpec((1, cb), lambda b, j: (0, j)), pl.BlockSpec((1, cb), lambda b, j: (0, nj + j)),
        ],
        out_specs=pl.BlockSpec((s, cb), lambda b, j: (b, j)),
        out_shape=jax.ShapeDtypeStruct((bl * s, D_FF), BF16),
        compiler_params=_params(("parallel", "parallel")),
    )(u, u, conv_w, conv_w, conv_b, conv_b)


def _ffn_act_bwd(u, conv_w, conv_b, dact, bl, s, cb):
    nj = D_FF // cb

    def body(ug_ref, uu_ref, wg_ref, wu_ref, bg_ref, bu_ref, da_ref, du_ref, dwg_ref, dwu_ref, dbg_ref, dbu_ref, acc):
        def store_into(half):
            def store(rows, value, add):
                if add:
                    acc[half, rows, :] += value
                else:
                    acc[half, rows, :] = value
            return store

        dwg_ref[...], dwu_ref[...], dbg_ref[...], dbu_ref[...] = _rows_vjp(
            _ffn_act, [lambda r: ug_ref[r, :].astype(F32), lambda r: uu_ref[r, :].astype(F32)],
            [wg_ref[...], wu_ref[...], bg_ref[...], bu_ref[...]],
            lambda r: da_ref[r, :], [store_into(0), store_into(1)], s, cb, CONV_HALO)
        du_ref[...] = acc[...].astype(BF16)

    t = bl * s
    blk = pl.BlockSpec((s, cb), lambda b, j: (b, j))
    wpart = pl.BlockSpec((None, FFN_CONV, cb), lambda b, j: (b, 0, j))
    bpart = pl.BlockSpec((None, 1, cb), lambda b, j: (b, 0, j))
    return pl.pallas_call(
        body, name="ffn_act_bwd", grid=(bl, nj),
        in_specs=[
            blk, pl.BlockSpec((s, cb), lambda b, j: (b, nj + j)),
            pl.BlockSpec((FFN_CONV, cb), lambda b, j: (0, j)), pl.BlockSpec((FFN_CONV, cb), lambda b, j: (0, nj + j)),
            pl.BlockSpec((1, cb), lambda b, j: (0, j)), pl.BlockSpec((1, cb), lambda b, j: (0, nj + j)),
            blk,
        ],
        out_specs=[pl.BlockSpec((2, s, cb), lambda b, j: (0, b, j)), wpart, wpart, bpart, bpart],
        out_shape=[jax.ShapeDtypeStruct((2, t, D_FF), BF16)] + [jax.ShapeDtypeStruct((bl, FFN_CONV, D_FF), F32)] * 2
        + [jax.ShapeDtypeStruct((bl, 1, D_FF), F32)] * 2,
        scratch_shapes=[pltpu.VMEM((2, s, cb), F32)],
        compiler_params=_params(("parallel", "parallel")),
    )(u, u, conv_w, conv_w, conv_b, conv_b, dact)


def _head(h1, ffn, gpre, pp, bgate, g2, b2, target, tm):
    t = h1.shape[0]
    tm = min(tm, t)

    def body(h1_ref, ffn_ref, gp_ref, pp_ref, bg_ref, g2_ref, b2_ref, tg_ref,
             dr_ref, drb_ref, dgp_ref, dpp_ref, loss_ref, dbg_ref, dg2_ref, db2_ref):
        fn = functools.partial(_head_loss, target=tg_ref[...])
        loss, vjp = jax.vjp(fn, h1_ref[...], ffn_ref[...], gp_ref[...], pp_ref[...], bg_ref[...], g2_ref[...], b2_ref[...])
        _, dffn, dgp, dpp, dbg, dg2, db2 = vjp(jnp.ones((1, 1), F32))
        dr_ref[...] = dffn
        drb_ref[...], dgp_ref[...], dpp_ref[...] = dffn.astype(BF16), dgp.astype(BF16), dpp.astype(BF16)

        @pl.when(pl.program_id(0) == 0)
        def _():
            loss_ref[...] = jnp.zeros_like(loss_ref)
            dbg_ref[...] = jnp.zeros_like(dbg_ref)
            dg2_ref[...] = jnp.zeros_like(dg2_ref)
            db2_ref[...] = jnp.zeros_like(db2_ref)

        loss_ref[...] += jnp.broadcast_to(loss, loss_ref.shape)
        dbg_ref[...] += dbg
        dg2_ref[...] += dg2
        db2_ref[...] += db2

    rows = pl.BlockSpec((tm, D_MODEL), lambda i: (i, 0))
    vec = pl.BlockSpec((1, D_MODEL), lambda i: (0, 0))
    return pl.pallas_call(
        body, name="head", grid=(t // tm,), in_specs=[rows] * 4 + [vec] * 3 + [rows],
        out_specs=[rows] * 4 + [pl.BlockSpec((8, 128), lambda i: (0, 0))] + [vec] * 3,
        out_shape=[jax.ShapeDtypeStruct(h1.shape, F32)] + [jax.ShapeDtypeStruct(h1.shape, BF16)] * 3
        + [jax.ShapeDtypeStruct((8, 128), F32)]
        + [jax.ShapeDtypeStruct((1, D_MODEL), F32)] * 3,
        compiler_params=_params(("arbitrary",)),
    )(h1, ffn, gpre, pp, bgate, g2, b2, target)


def _adam_update(g, w_ref, m_ref, v_ref, g_ref, d_ref, nm_ref, nv_ref):
    m2 = ADAM_B1 * m_ref[...] + (1.0 - ADAM_B1) * g
    v2 = ADAM_B2 * v_ref[...] + (1.0 - ADAM_B2) * jnp.square(g)
    m_hat = m2 / (1.0 - ADAM_B1 ** ADAM_STEP)
    v_hat = v2 / (1.0 - ADAM_B2 ** ADAM_STEP)
    g_ref[...] = g
    d_ref[...] = -ADAM_LR * (m_hat / (jnp.sqrt(v_hat) + ADAM_EPS) + ADAM_WD * w_ref[...])
    nm_ref[...] = m2
    nv_ref[...] = v2


def _row_tile(rows, cols, limit_bytes=256 * 1024):
    best = None
    for t in range(8, rows + 1, 8):
        if rows % t == 0 and t * cols * 4 <= limit_bytes:
            best = t
    return best or rows


def _adamw_reduced(own, recv, w, m, v, name):
    a, b = w.shape
    ta = _row_tile(a, b)

    def body(own_ref, recv_ref, w_ref, m_ref, v_ref, g_ref, d_ref, nm_ref, nv_ref):
        c = lax.axis_index("c")
        for core in range(2):
            @pl.when(c == core)
            def _():
                got = [recv_ref[k].astype(F32) for k in range(N_DEV - 1)]
                same = [own_ref[...].astype(F32), got[0], got[1], got[2]]
                other = got[3:]
                core0, core1 = (same, other) if core == 0 else (other, same)
                g = core0[0] + core1[0]
                for r in range(1, N_CHIPS):
                    g = (g + core0[r]) + core1[r]
                _adam_update(g, w_ref, m_ref, v_ref, g_ref, d_ref, nm_ref, nv_ref)

    blk = pl.BlockSpec((ta, b), lambda i: (i, 0))
    return pl.pallas_call(
        body, name=name, grid=(a // ta,),
        in_specs=[blk, pl.BlockSpec((7, ta, b), lambda i: (0, i, 0)), blk, blk, blk], out_specs=[blk] * 4,
        out_shape=[jax.ShapeDtypeStruct(w.shape, F32)] * 4, compiler_params=_params(("parallel",)),
    )(own, recv, w, m, v)


def _adamw_small(g, w, m, v):
    def body(g_in, w_ref, m_ref, v_ref, g_ref, d_ref, nm_ref, nv_ref):
        _adam_update(g_in[...], w_ref, m_ref, v_ref, g_ref, d_ref, nm_ref, nv_ref)

    blk = pl.BlockSpec(w.shape, lambda i: (0, 0))
    return pl.pallas_call(
        body, name="adamw_small", grid=(1,), in_specs=[blk] * 4, out_specs=[blk] * 4,
        out_shape=[jax.ShapeDtypeStruct(w.shape, F32)] * 4, compiler_params=_params(("arbitrary",)),
    )(g, w, m, v)


def _remote(src, dst, send_sem, recv_sem, device):
    return pltpu.make_async_remote_copy(src_ref=src, dst_ref=dst, send_sem=send_sem, recv_sem=recv_sem,
                                        device_id=device, device_id_type=MESH)


def _place():
    x, y, c = lax.axis_index("x"), lax.axis_index("y"), lax.axis_index("c")
    return x, y, c, 2 * x + y, [(1 - x, y), (x, 1 - y), (1 - x, 1 - y)]


HBM_REF = pl.BlockSpec(memory_space=pl.ANY)
HALF_ROWS_QUANTUM = 16


def _gather_sems(n):
    return [pltpu.SemaphoreType.DMA((3 * n,))] * 4 + [pltpu.SemaphoreType.DMA((n,))]


def _gather_copies(ins, outs, sems):
    send_s, recv_s, fsend_s, frecv_s, local_s = sems
    x, y, c, me, chips = _place()
    local, sends, steps = [], [], []
    for i, (src, dst) in enumerate(zip(ins, outs)):
        local.append(pltpu.make_async_copy(src, dst.at[me], local_s.at[i]))
        half = src.shape[0] // 2
        split = src.shape[0] % (2 * HALF_ROWS_QUANTUM) == 0
        if split:
            mine = pl.ds(pl.multiple_of(c * half, HALF_ROWS_QUANTUM), half)
            theirs = pl.ds(pl.multiple_of((1 - c) * half, HALF_ROWS_QUANTUM), half)
        for r, (px, py) in enumerate(chips):
            k, peer = 3 * i + r, 2 * px + py
            if split:
                sends.append(_remote(src.at[mine], dst.at[me, mine], send_s.at[k], recv_s.at[k], (px, py, c)))
                landed = dst.at[peer, mine]
                steps.append((_remote(src.at[mine], landed, send_s.at[k], recv_s.at[k], (px, py, c)),
                              _remote(landed, landed, fsend_s.at[k], frecv_s.at[k], (x, y, 1 - c)),
                              _remote(dst.at[peer, theirs], dst.at[peer, theirs], fsend_s.at[k], frecv_s.at[k], (x, y, 1 - c))))
            else:
                sends.append(_remote(src, dst.at[me], send_s.at[k], recv_s.at[k], (px, py, c)))
                steps.append((_remote(src, dst.at[peer], send_s.at[k], recv_s.at[k], (px, py, c)), None, None))
    return local, sends, steps


def _scatter_sems(n):
    return [pltpu.SemaphoreType.DMA((4 * n,))] * 2 + [pltpu.SemaphoreType.DMA((3 * n,))] * 2


def _scatter_copies(ins, outs, sems):
    send_s, recv_s, fsend_s, frecv_s = sems
    x, y, c, me, chips = _place()
    sends, steps = [], []
    for i, (src, dst) in enumerate(zip(ins, outs)):
        for r, (px, py) in enumerate(chips):
            k = 4 * i + r
            cp = _remote(src.at[2 * px + py], dst.at[r], send_s.at[k], recv_s.at[k], (px, py, c))
            fwd = _remote(dst.at[r], dst.at[4 + r], fsend_s.at[3 * i + r], frecv_s.at[3 * i + r], (x, y, 1 - c))
            sends.append(cp)
            steps.append((cp, fwd, fwd))
        k = 4 * i + 3
        cp = _remote(src.at[me], dst.at[3], send_s.at[k], recv_s.at[k], (x, y, 1 - c))
        sends.append(cp)
        steps.append((cp, None, None))
    return [], sends, steps


def _exchange_start(plan):
    local, sends, _ = plan
    for cp in local + sends:
        cp.start()


def _exchange_pass_on(plan):
    for arrival, pass_on, _ in plan[2]:
        arrival.wait_recv()
        if pass_on is not None:
            pass_on.start()


def _exchange_finish(plan):
    local, sends, steps = plan
    for _, pass_on, passed in steps:
        if pass_on is not None:
            passed.wait_recv()
    for cp in sends:
        cp.wait_send()
    for _, pass_on, _ in steps:
        if pass_on is not None:
            pass_on.wait_send()
    for cp in local:
        cp.wait()


def _exchange_call(arrays, copies, sems, out_shapes, name):
    n = len(arrays)

    def body(*refs):
        plan = copies(refs[:n], refs[n:2 * n], refs[2 * n:])
        _exchange_start(plan)
        _exchange_pass_on(plan)
        _exchange_finish(plan)

    return pl.pallas_call(
        body, name=name, in_specs=[HBM_REF] * n, out_specs=[HBM_REF] * n, out_shape=out_shapes,
        scratch_shapes=sems(n), compiler_params=pltpu.CompilerParams(has_side_effects=True),
    )(*arrays)


def _gather_call(shards, name):
    shapes = [jax.ShapeDtypeStruct((N_CHIPS,) + a.shape, a.dtype) for a in shards]
    return _exchange_call(shards, _gather_copies, _gather_sems, shapes, name)


def _all_reduce_small(a):
    def body(in_ref, out_ref, slots, send_sems, recv_sems):
        x, y, c = lax.axis_index("x"), lax.axis_index("y"), lax.axis_index("c")
        me = 4 * x + 2 * y + c
        slots[0] = in_ref[...]
        sends = []
        for r in range(1, N_DEV):
            peer = (x ^ (r >> 2), y ^ ((r >> 1) & 1), c ^ (r & 1))
            sends.append(pltpu.make_async_remote_copy(src_ref=in_ref, dst_ref=slots.at[r], send_sem=send_sems.at[r],
                                                      recv_sem=recv_sems.at[r], device_id=peer, device_id_type=MESH))
        for cp in sends:
            cp.start()
        for cp in sends:
            cp.wait_recv()
        acc = slots[me]
        for dev in range(1, N_DEV):
            acc = acc + slots[dev ^ me]
        out_ref[...] = acc
        for cp in sends:
            cp.wait_send()

    return pl.pallas_call(
        body, name="small_all_reduce",
        in_specs=[pl.BlockSpec(memory_space=pltpu.VMEM)], out_specs=pl.BlockSpec(memory_space=pltpu.VMEM),
        out_shape=jax.ShapeDtypeStruct(a.shape, a.dtype),
        scratch_shapes=[pltpu.VMEM((N_DEV,) + a.shape, a.dtype), pltpu.SemaphoreType.DMA((N_DEV,)),
                        pltpu.SemaphoreType.DMA((N_DEV,))],
        compiler_params=pltpu.CompilerParams(has_side_effects=True),
    )(a)


SHARDED = ["w_in", "mla_w_q_up", "mla_w_kv_up", "w_out", "ffn_w_up", "ffn_w_down", "ple_w_gate", "ple_w_proj",
           "gdn_conv_w", "ffn_conv_w"]
SHARD_AXIS = {"w_in": 1, "mla_w_q_up": 1, "mla_w_kv_up": 1, "w_out": 0, "ffn_w_up": 1, "ffn_w_down": 0,
              "ple_w_gate": 0, "ple_w_proj": 1, "gdn_conv_w": 1, "ffn_conv_w": 1}
SMALL = ["gdn_a_log", "gdn_dt_bias", "gdn_norm_g", "mla_q_norm_g", "mla_kv_norm_g", "ln1_g", "ln1_b", "ffn_conv_b",
         "ple_b_gate", "ln2_g", "ln2_b"]
WEIGHTS = ["w_in", "gdn_conv_w", "gdn_a_log", "gdn_dt_bias", "gdn_norm_g", "mla_q_norm_g", "mla_w_q_up", "mla_kv_norm_g",
           "mla_w_kv_up", "w_out", "ln1_g", "ln1_b", "ffn_w_up", "ffn_conv_w", "ffn_conv_b", "ffn_w_down", "ple_w_gate",
           "ple_b_gate", "ple_w_proj", "ln2_g", "ln2_b"]
F32_ON_WIRE = ("gdn_conv_w", "ffn_conv_w")
GATHER_EARLY = ["w_in", "gdn_conv_w"]
GATHER_LATE = ["mla_w_q_up", "mla_w_kv_up", "w_out", "ffn_w_up", "ffn_conv_w", "ffn_w_down", "ple_w_gate", "ple_w_proj"]
SCATTER_EARLY = ["ffn_w_up", "ffn_conv_w", "ffn_w_down", "ple_w_gate", "ple_w_proj", "w_out"]
SCATTER_LATE = ["w_in", "gdn_conv_w", "mla_w_q_up", "mla_w_kv_up"]
PACK_COLS = 1024
PACK_ROW_TILE = 8


def _join_blocks(blocks, axis):
    n, a, b = blocks.shape
    if axis == 0:
        return blocks.reshape(n * a, b)
    return jnp.transpose(blocks, (1, 0, 2)).reshape(a, n * b)


def _split_blocks(full, axis):
    if axis == 0:
        return full.reshape(N_CHIPS, full.shape[0] // N_CHIPS, full.shape[1])
    a, nb = full.shape
    return jnp.transpose(full.reshape(a, N_CHIPS, nb // N_CHIPS), (1, 0, 2))


def _pack(arrays):
    flat = jnp.concatenate([a.reshape(-1) for a in arrays])
    quantum = PACK_COLS * PACK_ROW_TILE
    padded = -(-flat.shape[0] // quantum) * quantum
    return jnp.pad(flat, (0, padded - flat.shape[0])).reshape(-1, PACK_COLS)


def _unpack(packed, shapes):
    flat = packed.reshape(-1)
    out, off = [], 0
    for shp in shapes:
        n = int(np.prod(shp))
        out.append(flat[off:off + n].reshape(shp))
        off += n
    return out


def kernel(x, p, w_in, gdn_conv_w, gdn_a_log, gdn_dt_bias, gdn_norm_g, mla_q_norm_g, mla_w_q_up, mla_kv_norm_g, mla_w_kv_up, w_out, ln1_g, ln1_b, ffn_w_up, ffn_conv_w, ffn_conv_b, ffn_w_down, ple_w_gate, ple_b_gate, ple_w_proj, ln2_g, ln2_b, loss_target, m_w_in, m_gdn_conv_w, m_gdn_a_log, m_gdn_dt_bias, m_gdn_norm_g, m_mla_q_norm_g, m_mla_w_q_up, m_mla_kv_norm_g, m_mla_w_kv_up, m_w_out, m_ln1_g, m_ln1_b, m_ffn_w_up, m_ffn_conv_w, m_ffn_conv_b, m_ffn_w_down, m_ple_w_gate, m_ple_b_gate, m_ple_w_proj, m_ln2_g, m_ln2_b, v_w_in, v_gdn_conv_w, v_gdn_a_log, v_gdn_dt_bias, v_gdn_norm_g, v_mla_q_norm_g, v_mla_w_q_up, v_mla_kv_norm_g, v_mla_w_kv_up, v_w_out, v_ln1_g, v_ln1_b, v_ffn_w_up, v_ffn_conv_w, v_ffn_conv_b, v_ffn_w_down, v_ple_w_gate, v_ple_b_gate, v_ple_w_proj, v_ln2_g, v_ln2_b):
    given = dict(locals())
    wsh = {n: given[n][0] for n in WEIGHTS}
    msh = {n: given["m_" + n][0] for n in WEIGHTS}
    vsh = {n: given["v_" + n][0] for n in WEIGHTS}
    bl, s, _ = x.shape
    t = bl * s
    xt = x.reshape(t, D_MODEL)
    pt = p.reshape(t, PLE_DIM)
    target = loss_target.reshape(t, D_MODEL)

    wire = lambda n: wsh[n] if n in F32_ON_WIRE else wsh[n].astype(BF16)
    early = _gather_call([wire(n) for n in GATHER_EARLY], "weights_gather_early")
    full = {n: _join_blocks(g, SHARD_AXIS[n]) for n, g in zip(GATHER_EARLY, early)}
    late_shards = [wire(n) for n in GATHER_LATE]
    late_ride = (late_shards, _gather_copies, _gather_sems,
                 [jax.ShapeDtypeStruct((N_CHIPS,) + a.shape, a.dtype) for a in late_shards], 0.75)

    in_cols, q_cols = _w_in_cols(), _w_q_cols()
    w_in_p = _pad_cols(full["w_in"], in_cols)
    gconv = full["gdn_conv_w"]
    row = lambda a: a.reshape(1, -1)
    sc = jnp.zeros((8, 128), F32).at[0, :GDN_HEADS].set(wsh["gdn_a_log"]).at[1, :GDN_HEADS].set(wsh["gdn_dt_bias"])
    norm_g, qg, kvg = row(wsh["gdn_norm_g"]), row(wsh["mla_q_norm_g"]), row(wsh["mla_kv_norm_g"])
    g1, b1, g2, b2 = row(wsh["ln1_g"]), row(wsh["ln1_b"]), row(wsh["ln2_g"]), row(wsh["ln2_b"])
    fbias, bgate = row(wsh["ffn_conv_b"]), row(wsh["ple_b_gate"])

    inv = ROPE_THETA ** (-jnp.arange(0, MLA_ROPE, 2, dtype=F32) / MLA_ROPE)
    ang = jnp.arange(s, dtype=F32)[:, None] * inv[None, :]
    zero = jnp.zeros_like(ang)
    cos_t = jnp.concatenate([jnp.cos(ang), zero, jnp.cos(ang), zero], axis=1)
    sin_t = jnp.concatenate([-jnp.sin(ang), zero, jnp.sin(ang), zero], axis=1)

    proj = _matmul(xt, w_in_p, name="proj", tm=1024)
    cat, o_raw, states, qkvg, late = _gdn_fwd(proj, gconv, sc, norm_g, bl, s, late_ride)
    full.update({n: _join_blocks(g, SHARD_AXIS[n]) for n, g in zip(GATHER_LATE, late)})
    w_o, w_up, w_down = full["w_out"], full["ffn_w_up"], full["ffn_w_down"]
    w_gate, w_proj, fconv = full["ple_w_gate"], full["ple_w_proj"], full["ffn_conv_w"]
    w_q_p, w_kv = _pad_cols(full["mla_w_q_up"], q_cols), full["mla_w_kv_up"]
    qf, kvf, kr = _mla_prep_fwd(proj, qg, kvg, w_q_p, w_kv, cos_t, sin_t, s, 256)
    cat = _attn_fwd(qf, kvf, kr, cat, bl, s, 256)
    wide = dict(tm=1024, tn=1024)
    mix = _matmul(cat, w_o, name="mix", **wide)
    r1, h1, h1b, xb = _ln1_fwd(xt, mix, g1, b1, 256)
    u = _matmul(h1b, w_up, name="ffn_up", tm=1024, tn=1408, out_dtype=BF16)
    act = _ffn_act_fwd(u, fconv, fbias, bl, s, 256)
    ffn = _matmul(act, w_down, name="ffn_down", tk=1408, **wide)
    gpre = _matmul(h1b, w_gate, name="ple_gate", **wide)
    pp = _matmul(pt, w_proj, name="ple_proj", **wide)
    dr2, dr2b, dgpre, dpp, loss_acc, dbgate, dg2, db2 = _head(h1, ffn, gpre, pp, bgate, g2, b2, target, 256)

    dact = _matmul(dr2b, w_down, name="d_act", tb=True, tm=1024, tn=1408)
    long_k = dict(ta=True, tk=2048)
    d_w_down = _matmul(act, dr2b, name="dw_down", tm=1408, tn=1024, **long_k)
    du, dfcw_g, dfcw_u, dfcb_g, dfcb_u = _ffn_act_bwd(u, fconv, fbias, dact, bl, s, 256)
    dh1_a = _matmul(du, w_up, name="dh1_ffn", tb=True, tk=1408, a_halves=True, **wide)
    dh1_b = _matmul(dgpre, w_gate, name="dh1_ple", tb=True, **wide)
    d_w_up = _matmul(h1b, du, name="dw_up", tn=1408, b_halves=True, **long_k)
    d_w_gate = _matmul(h1b, dgpre, name="dw_gate", **long_k, **wide)
    d_w_proj = _matmul(pt, dpp, name="dw_proj", ta=True, tn=1024)
    dr1, dr1b, dg1, db1 = _ln1_bwd(r1, dr2, dh1_a, dh1_b, g1, b1, 256)
    dcat = _matmul(dr1b, w_o, name="d_cat", tb=True, **wide)
    d_w_o = _matmul(cat, dr1b, name="dw_out", **long_k, **wide)

    gfull = {
        "ffn_w_up": d_w_up, "ffn_w_down": d_w_down, "ple_w_gate": d_w_gate, "ple_w_proj": d_w_proj, "w_out": d_w_o,
        "ffn_conv_w": jnp.concatenate([jnp.sum(dfcw_g, 0), jnp.sum(dfcw_u, 0)], axis=1),
    }
    slabs = {n: _split_blocks(gfull[n], SHARD_AXIS[n]).astype(BF16) for n in SCATTER_EARLY}
    early_slabs = [slabs[n] for n in SCATTER_EARLY]
    early_ride = (early_slabs, _scatter_copies, _scatter_sems,
                  [jax.ShapeDtypeStruct((N_DEV - 1,) + a.shape[1:], a.dtype) for a in early_slabs], 0.7)
    dproj, dab, dcwq, dcwk, dcwv, dsc, dng, early_recv = _gdn_bwd(proj, gconv, sc, norm_g, o_raw, states, qkvg, dcat, bl, s,
                                                                  early_ride)
    received = dict(zip(SCATTER_EARLY, early_recv))
    dqf, dkvf, dkr = _attn_bwd(qf, kvf, kr, dcat, bl, s, 256)
    dproj, dqg, dkvg, d_w_q_p, d_w_kv = _mla_prep_bwd(proj, qg, kvg, w_q_p, w_kv, cos_t, sin_t, dqf, dkvf, dkr, dab, dproj, s, 256)
    d_w_in_p = _matmul(xb, dproj, name="dw_in", **long_k, **wide)

    gfull.update({
        "w_in": _unpad_cols(d_w_in_p, in_cols, D_IN),
        "mla_w_q_up": _unpad_cols(d_w_q_p, q_cols, MLA_HEADS * (MLA_NOPE + MLA_ROPE)),
        "mla_w_kv_up": d_w_kv,
        "gdn_conv_w": jnp.concatenate([jnp.sum(dcwq, 0), jnp.sum(dcwk, 0), jnp.sum(dcwv, 0)], axis=1),
    })
    slabs.update({n: _split_blocks(gfull[n], SHARD_AXIS[n]).astype(BF16) for n in SCATTER_LATE})
    late_slabs = [slabs[n] for n in SCATTER_LATE]
    late_scatter = (late_slabs, _scatter_copies, _scatter_sems,
                    [jax.ShapeDtypeStruct((N_DEV - 1,) + a.shape[1:], a.dtype) for a in late_slabs], 0.85)
    grad_x, late_recv = _matmul(dproj, w_in_p, name="d_x", tb=True, add=dr1, add_scale=ALPHA, ride=late_scatter, **wide)
    received.update(zip(SCATTER_LATE, late_recv))
    dsc_sum = jnp.sum(dsc, axis=(0, 1))
    gsmall = {
        "gdn_a_log": dsc_sum[0, :GDN_HEADS], "gdn_dt_bias": dsc_sum[1, :GDN_HEADS],
        "gdn_norm_g": jnp.sum(dng[:, :, 0, :], axis=(0, 1)),
        "mla_q_norm_g": dqg[0], "mla_kv_norm_g": dkvg[0], "ln1_g": dg1[0], "ln1_b": db1[0],
        "ffn_conv_b": jnp.concatenate([jnp.sum(dfcb_g, 0), jnp.sum(dfcb_u, 0)], axis=1)[0],
        "ple_b_gate": dbgate[0], "ln2_g": dg2[0], "ln2_b": db2[0],
    }

    me_chip = 2 * lax.axis_index("x") + lax.axis_index("y")
    big = [{}, {}, {}, {}]
    for n in SHARDED:
        own = lax.dynamic_index_in_dim(slabs[n], me_chip, 0, keepdims=False)
        for kind, val in enumerate(_adamw_reduced(own, received[n], wsh[n], msh[n], vsh[n], "adamw_" + n)):
            big[kind][n] = val

    small_shapes = [wsh[n].shape for n in SMALL]
    gsum = _all_reduce_small(_pack([gsmall[n] for n in SMALL]))
    spacks = _adamw_small(gsum, _pack([wsh[n] for n in SMALL]), _pack([msh[n] for n in SMALL]), _pack([vsh[n] for n in SMALL]))
    small = [dict(zip(SMALL, _unpack(pk, small_shapes))) for pk in spacks]

    loss = lax.psum(loss_acc[0, 0], ("x", "y", "c"))
    outs = [loss, grad_x.reshape(x.shape)]
    for kind in range(4):
        for n in WEIGHTS:
            val = big[kind][n] if n in big[kind] else small[kind][n]
            outs.append(val[None])
    return tuple(outs)
```

```python
import functools
import math

import numpy as np
import jax
import jax.numpy as jnp
from jax import lax
from jax.experimental import pallas as pl
from jax.experimental.pallas import tpu as pltpu

F32 = jnp.float32
BF16 = jnp.bfloat16

D_MODEL = 1024
CHUNK = 64
PLE_DIM = 256
GDN_HEADS = 4
GDN_DK = 128
GDN_DV = 128
GDN_CONV = 4
MLA_HEADS = 4
MLA_NOPE = 128
MLA_ROPE = 64
MLA_V = 128
MLA_Q_LORA = 384
MLA_KV_LORA = 256
ROPE_THETA = 10000.0
D_FF = 2816
FFN_CONV = 3
DEPTH = 1
ALPHA = (2.0 * DEPTH) ** 0.25
NORM_EPS = 1e-6
GDN_QK = GDN_HEADS * GDN_DK
GDN_VW = GDN_HEADS * GDN_DV
D_IN = 2 * GDN_QK + 2 * GDN_VW + 2 * GDN_HEADS + MLA_Q_LORA + MLA_KV_LORA + MLA_ROPE
ATT_SCALE = (MLA_NOPE + MLA_ROPE) ** -0.5

ADAM_LR = 0.001
ADAM_B1 = 0.9
ADAM_B2 = 0.999
ADAM_EPS = 1e-08
ADAM_WD = 0.01
ADAM_STEP = 10

LANES = 128
VMEM_LIMIT = 60 * 1024 * 1024
GDN_FWD_GROUP = 16
GDN_BWD_GROUP = 16
N_CHIPS = 4
N_DEV = 8

P_WIDTH = 3072
P_MLA = 2048
MESH = pl.DeviceIdType.MESH


def _rope_slot(j):
    return j if j < MLA_ROPE // 2 else 64 + (j - MLA_ROPE // 2)


def _w_in_cols():
    idx = -np.ones((P_WIDTH,), np.int64)
    for h in range(GDN_HEADS):
        base = h * 512
        idx[base:base + 128] = np.arange(128) + h * GDN_DK
        idx[base + 128:base + 256] = np.arange(128) + GDN_QK + h * GDN_DK
        idx[base + 256:base + 384] = np.arange(128) + 2 * GDN_QK + h * GDN_DV
        idx[base + 384:base + 512] = np.arange(128) + 2 * GDN_QK + GDN_VW + h * GDN_DV
    o_a = 2 * GDN_QK + 2 * GDN_VW
    idx[P_MLA:P_MLA + 2 * GDN_HEADS] = np.arange(2 * GDN_HEADS) + o_a
    o_cq = o_a + 2 * GDN_HEADS
    idx[P_MLA + 128:P_MLA + 512] = np.arange(MLA_Q_LORA) + o_cq
    o_ckv = o_cq + MLA_Q_LORA
    idx[P_MLA + 512:P_MLA + 768] = np.arange(MLA_KV_LORA) + o_ckv
    o_kr = o_ckv + MLA_KV_LORA
    for j in range(MLA_ROPE):
        idx[P_MLA + 768 + _rope_slot(j)] = o_kr + j
    return idx


def _w_q_cols():
    idx = -np.ones((MLA_HEADS * 256,), np.int64)
    for h in range(MLA_HEADS):
        o = h * (MLA_NOPE + MLA_ROPE)
        idx[h * 256:h * 256 + 128] = np.arange(128) + o
        for j in range(MLA_ROPE):
            idx[h * 256 + 128 + _rope_slot(j)] = o + MLA_NOPE + j
    return idx


def _pad_cols(w, idx):
    safe = np.where(idx >= 0, idx, 0)
    return jnp.where(jnp.asarray(idx >= 0)[None, :], w[:, safe], 0.0)


def _unpad_cols(wp, idx, n):
    inv = np.zeros((n,), np.int64)
    inv[idx[idx >= 0]] = np.nonzero(idx >= 0)[0]
    return wp[:, inv]


def _dot(a, b, ca, cb, precision=None):
    if precision is None:
        a = a.astype(BF16)
        b = b.astype(BF16)
    return lax.dot_general(a, b, (((ca,), (cb,)), ((), ())), preferred_element_type=F32, precision=precision)


@jax.custom_vjp
def mm(a, b):
    return _dot(a, b, 1, 0)


@jax.custom_vjp
def mm_nt(a, b):
    return _dot(a, b, 1, 1)


@jax.custom_vjp
def mm_tn(a, b):
    return _dot(a, b, 0, 0)


mm.defvjp(lambda a, b: (mm(a, b), (a, b)), lambda r, g: (mm_nt(g, r[1]), mm_tn(r[0], g)))
mm_nt.defvjp(lambda a, b: (mm_nt(a, b), (a, b)), lambda r, g: (mm(g, r[1]), mm_tn(g, r[0])))
mm_tn.defvjp(lambda a, b: (mm_tn(a, b), (a, b)), lambda r, g: (mm_nt(r[1], g), mm(r[0], g)))

def _split(a):
    hi = a.astype(BF16)
    return hi, (a - hi.astype(F32)).astype(BF16)


def _dot3(a, b, ca, cb):
    a_hi, a_lo = _split(a)
    b_hi, b_lo = _split(b)
    return (_dot(a_hi, b_hi, ca, cb) + _dot(a_hi, b_lo, ca, cb)) + _dot(a_lo, b_hi, ca, cb)


def _shift_rows(x, s):
    return x if s == 0 else pltpu.roll(x, s % x.shape[0], 0)


def _row(w, j):
    tap = lax.broadcasted_iota(jnp.int32, w.shape, 0)
    return jnp.sum(jnp.where(tap == j, w, 0.0), axis=0, keepdims=True)


@jax.custom_vjp
def dwconv(x, w):
    k = w.shape[0]
    y = _row(w, k - 1) * x
    for j in range(k - 1):
        y = y + _row(w, j) * _shift_rows(x, k - 1 - j)
    return y


def _dwconv_fwd(x, w):
    return dwconv(x, w), (x, w)


def _dwconv_bwd(res, dy):
    x, w = res
    k = w.shape[0]
    dx = _row(w, k - 1) * dy
    tap = lax.broadcasted_iota(jnp.int32, w.shape, 0)
    dw = jnp.where(tap == k - 1, jnp.sum(dy * x, axis=0, keepdims=True), 0.0)
    for j in range(k - 1):
        dx = dx + _row(w, j) * _shift_rows(dy, -(k - 1 - j))
        dw = dw + jnp.where(tap == j, jnp.sum(dy * _shift_rows(x, k - 1 - j), axis=0, keepdims=True), 0.0)
    return dx, dw


dwconv.defvjp(_dwconv_fwd, _dwconv_bwd)


@jax.custom_vjp
def rope128(x, cos, sin):
    return x * cos + pltpu.roll(x, 64, 1) * sin


rope128.defvjp(lambda x, c, s: (rope128(x, c, s), (c, s)),
               lambda r, g: (g * r[0] + pltpu.roll(g * r[1], 64, 1), jnp.zeros_like(r[0]), jnp.zeros_like(r[1])))


def _silu(x):
    return x * jax.nn.sigmoid(x)


def _softplus(x):
    return jnp.maximum(x, 0.0) + jnp.log(1.0 + jnp.exp(-jnp.abs(x)))


def _rmsnorm(x, g):
    return x * lax.rsqrt(jnp.mean(x * x, axis=-1, keepdims=True) + NORM_EPS) * g


def _layernorm(x, g, b):
    mu = jnp.mean(x, axis=-1, keepdims=True)
    xc = x - mu
    var = jnp.mean(xc * xc, axis=-1, keepdims=True)
    return xc * lax.rsqrt(var + NORM_EPS) * g + b


def _pick_lane(row, lane):
    idx = lax.broadcasted_iota(jnp.int32, row.shape, 1)
    return jnp.sum(jnp.where(idx == lane, row, 0.0), axis=1, keepdims=True)


def _gdn_q(pq, cw):
    h = _silu(dwconv(pq, cw))
    return h * lax.rsqrt(jnp.sum(h * h, axis=-1, keepdims=True) + NORM_EPS) * (GDN_DK ** -0.5)


def _gdn_k(pk, cw):
    h = _silu(dwconv(pk, cw))
    return h * lax.rsqrt(jnp.sum(h * h, axis=-1, keepdims=True) + NORM_EPS)


def _gdn_v(pv, cw):
    return _silu(dwconv(pv, cw))


def _gdn_gate(ab, sc, head):
    a = _pick_lane(ab, head)
    b = _pick_lane(ab, GDN_HEADS + head)
    a_log = _pick_lane(_row(sc, 0), head)
    dt_bias = _pick_lane(_row(sc, 1), head)
    beta = jax.nn.sigmoid(b)
    g = -jnp.exp(a_log) * _softplus(a + dt_bias)
    return _two_lanes(g, beta)


def _two_lanes(c0, c1):
    lane = lax.broadcasted_iota(jnp.int32, (c0.shape[0], LANES), 1)
    return jnp.where(lane == 0, c0, jnp.where(lane == 1, c1, 0.0))


def _inverse_group(lows):
    n = lows[0].shape[0]
    ii = lax.broadcasted_iota(jnp.int32, (n, n), 0)
    jj = lax.broadcasted_iota(jnp.int32, (n, n), 1)
    eye = jnp.where(ii == jj, 1.0, 0.0)
    invs = [eye - low for low in lows]
    powers = [_dot3(low, low, 1, 0) for low in lows]
    k = 2
    while k < n:
        invs = [inv + _dot3(inv, p, 1, 0) for inv, p in zip(invs, powers)]
        k *= 2
        if k < n:
            powers = [_dot3(p, p, 1, 0) for p in powers]
    return invs


@jax.custom_vjp
def solve_group(lows, rhss):
    return [_dot3(inv, rhs, 1, 0) for inv, rhs in zip(_inverse_group(lows), rhss)]


def _solve_group_fwd(lows, rhss):
    invs = _inverse_group(lows)
    xs = [_dot3(inv, rhs, 1, 0) for inv, rhs in zip(invs, rhss)]
    return xs, (invs, xs)


def _solve_group_bwd(res, dxs):
    invs, xs = res
    n = invs[0].shape[0]
    strict = lax.broadcasted_iota(jnp.int32, (n, n), 0) > lax.broadcasted_iota(jnp.int32, (n, n), 1)
    drhss = [_dot3(inv, dx, 0, 0) for inv, dx in zip(invs, dxs)]
    dlows = [jnp.where(strict, -_dot3(drhs, x, 1, 1), 0.0) for drhs, x in zip(drhss, xs)]
    return dlows, drhss


solve_group.defvjp(_solve_group_fwd, _solve_group_bwd)


def _gdn_local_group(qs, ks, vs, gbs):
    c = qs[0].shape[0]
    ii = lax.broadcasted_iota(jnp.int32, (c, c), 0)
    jj = lax.broadcasted_iota(jnp.int32, (c, c), 1)
    incl = ii >= jj
    gs = [_pick_lane(gb, 0) for gb in gbs]
    betas = [_pick_lane(gb, 1) for gb in gbs]
    g_rows = [jnp.sum(jnp.where(ii == jj, g, 0.0), axis=0, keepdims=True) for g in gs]
    gc_cols = [jnp.sum(jnp.where(incl, g_row, 0.0), axis=1, keepdims=True) for g_row in g_rows]
    gc_rows = [jnp.sum(jnp.where(jj >= ii, g, 0.0), axis=0, keepdims=True) for g in gs]
    decays = [jnp.where(incl, jnp.exp(jnp.where(incl, gc - gr, 0.0)), 0.0) for gc, gr in zip(gc_cols, gc_rows)]
    kbs = [k * beta for k, beta in zip(ks, betas)]
    lows = [jnp.where(ii > jj, mm_nt(kb, k) * decay, 0.0) for kb, k, decay in zip(kbs, ks, decays)]
    egs = [jnp.exp(gc) for gc in gc_cols]
    wus = solve_group(lows, [jnp.concatenate([kb * eg, v * beta], axis=1) for kb, eg, v, beta in zip(kbs, egs, vs, betas)])
    qks = [mm_nt(q, k) * decay for q, k, decay in zip(qs, ks, decays)]
    g_lasts = [jnp.sum(g_row, axis=1, keepdims=True) for g_row in g_rows]
    kds = [k * jnp.exp(gl - gc) for k, gl, gc in zip(ks, g_lasts, gc_cols)]
    ws, us = [wu[:, :GDN_DK] for wu in wus], [wu[:, GDN_DK:] for wu in wus]
    q_effs = [q * eg - mm(qk, w) for q, eg, qk, w in zip(qs, egs, qks, ws)]
    o_locals = [mm(qk, u) for qk, u in zip(qks, us)]
    mixes = [mm_tn(kd, w) for kd, w in zip(kds, ws)]
    adds = [mm_tn(kd, u) for kd, u in zip(kds, us)]
    return [(q_eff, o_loc, mix, add, jnp.exp(gl))
            for q_eff, o_loc, mix, add, gl in zip(q_effs, o_locals, mixes, adds, g_lasts)]


def _gdn_state_step(q_eff, o_local, mix, add, eg_last, state):
    return mm(q_eff, state) + o_local, state * eg_last - mm(mix, state) + add


def _gdn_post(o, z, norm_g):
    return _rmsnorm(o, norm_g) * _silu(z)


def _attn_block(q, kn, kr, v, q0):
    s = (mm_nt(q[:, :128], kn) + mm_nt(q[:, 128:], kr)) * ATT_SCALE
    qpos = q0 + lax.broadcasted_iota(jnp.int32, s.shape, 0)
    kpos = lax.broadcasted_iota(jnp.int32, s.shape, 1)
    shift = int(math.log2(CHUNK))
    allowed = (kpos >> shift) <= (qpos >> shift)
    s = jnp.where(allowed, s, -1e30)
    p = jnp.exp(s - jnp.max(s, axis=-1, keepdims=True))
    p = p / jnp.sum(p, axis=-1, keepdims=True)
    return mm(p, v)


def _mla_prep(pm, qg, kvg, wq, wkv, cos, sin):
    cq = pm[:, 128:512]
    ckv = pm[:, 512:768]
    qf = mm(_rmsnorm(cq, qg), wq)
    parts = []
    for h in range(MLA_HEADS):
        parts.append(qf[:, h * 256:h * 256 + 128])
        parts.append(rope128(qf[:, h * 256 + 128:h * 256 + 256], cos, sin))
    kvf = mm(_rmsnorm(ckv, kvg), wkv)
    return jnp.concatenate(parts, axis=1), kvf, rope128(pm[:, 768:896], cos, sin)


def _ffn_act(ug, uu, wg, wu, bg, bu):
    return _silu(dwconv(ug, wg) + bg) * (dwconv(uu, wu) + bu)


def _head_loss(h1, ffn, gpre, pp, bgate, g2, b2, target):
    gate = jax.nn.sigmoid(gpre + bgate)
    h2 = _layernorm(ALPHA * h1 + ffn + gate * pp, g2, b2)
    err = h2 - target
    return 0.5 * jnp.sum(jnp.sum(err * err, axis=1, keepdims=True), axis=0, keepdims=True) / D_MODEL


ROW_TILE_VREGS = 32
CONV_HALO = 8


def _rows_per_tile(n_rows, cols):
    tile = min(n_rows, ROW_TILE_VREGS * 8 * LANES // cols)
    assert n_rows % tile == 0 and tile % CONV_HALO == 0, (n_rows, cols)
    return tile


def _tile_inputs(loads, t0, first, halo, tile):
    if halo == 0:
        return [ld(pl.ds(t0, tile)) for ld in loads]
    if first:
        xs = [ld(pl.ds(0, tile)) for ld in loads]
        return [jnp.concatenate([jnp.zeros((halo, x.shape[1]), x.dtype), x], axis=0) for x in xs]
    return [ld(pl.ds(pl.multiple_of(t0 - halo, CONV_HALO), tile + halo)) for ld in loads]


def _rows_apply(fn, loads, consts, store, n_rows, cols, halo):
    tile = _rows_per_tile(n_rows, cols)

    def one(t0, first):
        y = fn(*_tile_inputs(loads, t0, first, halo, tile), *consts)
        store(pl.ds(t0, tile), y[halo:] if halo else y)

    one(0, True)

    def step(i, carry):
        one(pl.multiple_of(i * tile, tile), False)
        return carry

    lax.fori_loop(1, n_rows // tile, step, 0)


def _rows_vjp(fn, loads, consts, load_dy, stores, n_rows, cols, halo):
    tile = _rows_per_tile(n_rows, cols)

    def one(t0, first, dconsts):
        xs = _tile_inputs(loads, t0, first, halo, tile)
        _, vjp = jax.vjp(lambda *a: fn(*a)[halo:] if halo else fn(*a), *xs, *consts)
        grads = vjp(load_dy(pl.ds(t0, tile)))
        for st, dx in zip(stores, grads[:len(xs)]):
            st(pl.ds(t0, tile), dx[halo:] if halo else dx, False)
            if halo and not first:
                st(pl.ds(pl.multiple_of(t0 - halo, CONV_HALO), halo), dx[:halo], True)
        return tuple(a + b for a, b in zip(dconsts, grads[len(xs):]))

    dconsts = one(0, True, tuple(jnp.zeros_like(c) for c in consts))
    return lax.fori_loop(1, n_rows // tile, lambda i, dc: one(pl.multiple_of(i * tile, tile), False, dc), dconsts)


def _params(sem):
    return pltpu.CompilerParams(dimension_semantics=sem, vmem_limit_bytes=VMEM_LIMIT)


def _matmul(a, b, *, name, ta=False, tb=False, tm=512, tn=512, tk=1024, add=None, add_scale=1.0,
            a_halves=False, b_halves=False, ride=None, out_dtype=F32):
    assert not (a_halves and ta) and not (b_halves and tb)
    a_shape = (a.shape[1], 2 * a.shape[2]) if a_halves else a.shape
    b_shape = (b.shape[1], 2 * b.shape[2]) if b_halves else b.shape
    (k_dim, m) = a_shape if ta else a_shape[::-1]
    (n, k2) = b_shape if tb else b_shape[::-1]
    assert k_dim == k2, (a.shape, b.shape)
    tm, tn, tk = min(tm, m), min(tn, n), min(tk, k_dim)
    assert m % tm == 0 and n % tn == 0 and k_dim % tk == 0, (name, m, n, k_dim, tm, tn, tk)
    nk = k_dim // tk
    ca, cb = (0 if ta else 1), (1 if tb else 0)

    def body(*refs):
        if add is None:
            a_ref, b_ref, o_ref, acc = refs
        else:
            a_ref, b_ref, c_ref, o_ref, acc = refs
        kk = pl.program_id(2)

        @pl.when(kk == 0)
        def _():
            acc[...] = jnp.zeros_like(acc)

        acc[...] += _dot(a_ref[...], b_ref[...], ca, cb)

        @pl.when(kk == nk - 1)
        def _():
            r = acc[...]
            if add is not None:
                r = r + add_scale * c_ref[...]
            o_ref[...] = r.astype(out_dtype)

    spec = pl.BlockSpec
    a_spec = spec((tk, tm), lambda i, j, k: (k, i)) if ta else spec((tm, tk), lambda i, j, k: (i, k))
    b_spec = spec((tn, tk), lambda i, j, k: (j, k)) if tb else spec((tk, tn), lambda i, j, k: (k, j))
    if a_halves:
        kh = k_dim // 2 // tk
        assert kh * tk * 2 == k_dim
        a_spec = spec((None, tm, tk), lambda i, j, k: (k // kh, i, k % kh))
    if b_halves:
        nh = n // 2 // tn
        assert nh * tn * 2 == n
        b_spec = spec((None, tk, tn), lambda i, j, k: (j // nh, k, j % nh))
    in_specs = [a_spec, b_spec]
    args = [a, b]
    if add is not None:
        in_specs.append(pl.BlockSpec((tm, tn), lambda i, j, k: (i, j)))
        args.append(add)
    grid = (m // tm, n // tn, nk)
    r_in, r_out, r_shapes, r_sems, r_args = _ride_specs(ride)
    outs = pl.pallas_call(
        _riding(body, len(args), 1, 1, ride, grid), name=name, grid=grid,
        in_specs=in_specs + r_in, out_specs=[pl.BlockSpec((tm, tn), lambda i, j, k: (i, j))] + r_out,
        out_shape=[jax.ShapeDtypeStruct((m, n), out_dtype)] + r_shapes,
        scratch_shapes=[pltpu.VMEM((tm, tn), F32)] + r_sems,
        compiler_params=_params(("parallel", "parallel", "arbitrary") if ride is None else ("arbitrary",) * 3),
    )(*args, *r_args)
    return outs[0] if ride is None else (outs[0], list(outs[1:]))


def _riding(core, n_in, n_out, n_scratch, ride, grid):
    if ride is None:
        return core
    copies, nr = ride[1], len(ride[0])
    steps = int(np.prod(grid))
    pass_step = min(max(int(steps * ride[4]), 1), steps - 2)
    assert steps >= 3, grid

    def body(*refs):
        cuts = np.cumsum([0, n_in, nr, n_out, nr, n_scratch])
        ins, rin, outs, rout, scratch = (refs[a:b] for a, b in zip(cuts[:-1], cuts[1:]))
        sems = refs[cuts[-1]:]
        step = 0
        for axis, size in enumerate(grid):
            step = step * size + pl.program_id(axis)

        @pl.when(step == 0)
        def _():
            _exchange_start(copies(rin, rout, sems))

        @pl.when(step == pass_step)
        def _():
            _exchange_pass_on(copies(rin, rout, sems))

        core(*ins, *outs, *scratch)

        @pl.when(step == steps - 1)
        def _():
            _exchange_finish(copies(rin, rout, sems))

    return body


def _ride_specs(ride):
    if ride is None:
        return [], [], [], [], []
    arrays, _, sems, shapes, _ = ride
    return [HBM_REF] * len(arrays), [HBM_REF] * len(arrays), list(shapes), sems(len(arrays)), list(arrays)


def _gdn_fwd(proj, conv_w, sc, norm_g, bl, s, ride=None):
    nc = s // CHUNK

    def core(ph_ref, ab_ref, cwq_ref, cwk_ref, cwv_ref, sc_ref, ng_ref, cat_ref, o_ref, st_ref, q_s, k_s, v_s, gb_s):
        def into(ref):
            def store(rows, value):
                ref[rows, :] = value.astype(ref.dtype)
            return store

        for fn, col, cw_ref, val_s in [(_gdn_q, 0, cwq_ref, q_s), (_gdn_k, 128, cwk_ref, k_s), (_gdn_v, 256, cwv_ref, v_s)]:
            _rows_apply(fn, [lambda r, col=col: ph_ref[r, col:col + 128]], [cw_ref[...]], into(val_s), s, LANES, CONV_HALO)
        _rows_apply(functools.partial(_gdn_gate, head=pl.program_id(1)), [lambda r: ab_ref[r, :]], [sc_ref[...]], into(gb_s), s, LANES, 0)

        group = math.gcd(nc, GDN_FWD_GROUP)

        def chunks(i, state):
            ns = [i * group + j for j in range(group)]
            rows = [pl.ds(pl.multiple_of(n * CHUNK, CHUNK), CHUNK) for n in ns]
            local = _gdn_local_group([q_s[r, :] for r in rows], [k_s[r, :] for r in rows], [v_s[r, :] for r in rows],
                                     [gb_s[r, :] for r in rows])
            for n, r, loc in zip(ns, rows, local):
                st_ref[n] = state
                o_ref[r, :], state = _gdn_state_step(*loc, state)
            return state

        lax.fori_loop(0, nc // group, chunks, jnp.zeros((GDN_DK, GDN_DV), F32))
        _rows_apply(_gdn_post, [lambda r: o_ref[r, :], lambda r: ph_ref[r, 384:512]], [ng_ref[...]], into(cat_ref), s, LANES, 0)

    t = bl * s
    r_in, r_out, r_shapes, r_sems, r_args = _ride_specs(ride)
    outs = pl.pallas_call(
        _riding(core, 7, 7, 0, ride, (bl, GDN_HEADS)), name="gdn_fwd", grid=(bl, GDN_HEADS),
        in_specs=[
            pl.BlockSpec((s, 512), lambda b, h: (b, h)),
            pl.BlockSpec((s, 128), lambda b, h: (b, P_MLA // 128)),
            pl.BlockSpec((GDN_CONV, 128), lambda b, h: (0, h)),
            pl.BlockSpec((GDN_CONV, 128), lambda b, h: (0, GDN_HEADS + h)),
            pl.BlockSpec((GDN_CONV, 128), lambda b, h: (0, 2 * GDN_HEADS + h)),
            pl.BlockSpec((8, 128), lambda b, h: (0, 0)),
            pl.BlockSpec((1, 128), lambda b, h: (0, 0)),
        ] + r_in,
        out_specs=[
            pl.BlockSpec((s, 128), lambda b, h: (b, h)),
            pl.BlockSpec((s, 128), lambda b, h: (b, h)),
            pl.BlockSpec((None, None, nc, GDN_DK, GDN_DV), lambda b, h: (b, h, 0, 0, 0)),
        ] + [pl.BlockSpec((s, 128), lambda b, h: (b, h))] * 4 + r_out,
        out_shape=[
            jax.ShapeDtypeStruct((t, 2 * GDN_VW), BF16),
            jax.ShapeDtypeStruct((t, GDN_VW), F32),
            jax.ShapeDtypeStruct((bl, GDN_HEADS, nc, GDN_DK, GDN_DV), F32),
        ] + [jax.ShapeDtypeStruct((t, GDN_VW), F32)] * 4 + r_shapes,
        scratch_shapes=r_sems,
        compiler_params=_params(("arbitrary", "arbitrary")),
    )(proj, proj, conv_w, conv_w, conv_w, sc, norm_g, *r_args)
    return outs[0], outs[1], outs[2], tuple(outs[3:7]), list(outs[7:])


def _gdn_bwd(proj, conv_w, sc, norm_g, o_raw, states, qkvg, dcat, bl, s, ride=None):
    nc = s // CHUNK

    def core(ph_ref, ab_ref, cwq_ref, cwk_ref, cwv_ref, sc_ref, ng_ref, o_ref, st_ref, dc_ref, q_in, k_in, v_in, gb_in,
             dph_ref, dab_ref, dcwq_ref, dcwk_ref, dcwv_ref, dsc_ref, dng_ref, q_s, k_s, v_s, gb_s, do_s):
        head = pl.program_id(1)
        gate = functools.partial(_gdn_gate, head=head)
        paths = [(_gdn_q, 0, cwq_ref, q_s, dcwq_ref), (_gdn_k, 128, cwk_ref, k_s, dcwk_ref), (_gdn_v, 256, cwv_ref, v_s, dcwv_ref)]
        def into(ref, cols=slice(None)):
            def store(rows, value, add=False):
                if add:
                    ref[rows, cols] += value.astype(ref.dtype)
                else:
                    ref[rows, cols] = value.astype(ref.dtype)
            return store

        (dng,) = _rows_vjp(_gdn_post, [lambda r: o_ref[r, :], lambda r: ph_ref[r, 384:512]], [ng_ref[...]],
                           lambda r: dc_ref[r, :], [into(do_s), into(dph_ref, slice(384, 512))], s, LANES, 0)
        dng_ref[...] = jnp.broadcast_to(dng, dng_ref.shape)

        group = math.gcd(nc, GDN_BWD_GROUP)

        def chunks(i, dstate):
            ns = [nc - 1 - (i * group + j) for j in range(group)]
            rows = [pl.ds(pl.multiple_of(n * CHUNK, CHUNK), CHUNK) for n in ns]
            local, local_vjp = jax.vjp(_gdn_local_group, [q_in[r, :] for r in rows], [k_in[r, :] for r in rows],
                                       [v_in[r, :] for r in rows], [gb_in[r, :] for r in rows])
            d_os = [do_s[r, :] for r in rows]
            dlocal = []
            for n, loc, d_o in zip(ns, local, d_os):
                _, step_vjp = jax.vjp(_gdn_state_step, *loc, st_ref[n])
                *dloc, dstate = step_vjp((d_o, dstate))
                dlocal.append(tuple(dloc))
            dqs, dks, dvs, dgbs = local_vjp(dlocal)
            for r, dq, dk, dv, dgb in zip(rows, dqs, dks, dvs, dgbs):
                q_s[r, :], k_s[r, :], v_s[r, :], gb_s[r, :] = dq, dk, dv, dgb
            return dstate

        lax.fori_loop(0, nc // group, chunks, jnp.zeros((GDN_DK, GDN_DV), F32))
        for fn, col, cw_ref, val_s, dcw_ref in paths:
            (dcw_ref[...],) = _rows_vjp(fn, [lambda r, col=col: ph_ref[r, col:col + 128]], [cw_ref[...]],
                                        lambda r, val_s=val_s: val_s[r, :], [into(val_s)], s, LANES, CONV_HALO)
            dph_ref[:, col:col + 128] = val_s[...].astype(BF16)

        @pl.when(head == 0)
        def _():
            dab_ref[...] = jnp.zeros_like(dab_ref)

        def add_dab(rows, value, add=False):
            dab_ref[rows, :] += value

        (dsc_ref[...],) = _rows_vjp(gate, [lambda r: ab_ref[r, :]], [sc_ref[...]], lambda r: gb_s[r, :], [add_dab], s, LANES, 0)

    t = bl * s
    cw_out = pl.BlockSpec((None, GDN_CONV, 128), lambda b, h: (b, 0, h))
    part = pl.BlockSpec((None, None, 8, 128), lambda b, h: (b, h, 0, 0))
    r_in, r_out, r_shapes, r_sems, r_args = _ride_specs(ride)
    outs = pl.pallas_call(
        _riding(core, 14, 7, 5, ride, (bl, GDN_HEADS)), name="gdn_bwd", grid=(bl, GDN_HEADS),
        in_specs=[
            pl.BlockSpec((s, 512), lambda b, h: (b, h)),
            pl.BlockSpec((s, 128), lambda b, h: (b, P_MLA // 128)),
            pl.BlockSpec((GDN_CONV, 128), lambda b, h: (0, h)),
            pl.BlockSpec((GDN_CONV, 128), lambda b, h: (0, GDN_HEADS + h)),
            pl.BlockSpec((GDN_CONV, 128), lambda b, h: (0, 2 * GDN_HEADS + h)),
            pl.BlockSpec((8, 128), lambda b, h: (0, 0)),
            pl.BlockSpec((1, 128), lambda b, h: (0, 0)),
            pl.BlockSpec((s, 128), lambda b, h: (b, h)),
            pl.BlockSpec((None, None, nc, GDN_DK, GDN_DV), lambda b, h: (b, h, 0, 0, 0)),
        ] + [pl.BlockSpec((s, 128), lambda b, h: (b, h))] * 5 + r_in,
        out_specs=[
            pl.BlockSpec((s, 512), lambda b, h: (b, h)),
            pl.BlockSpec((s, 128), lambda b, h: (b, 0)),
            cw_out, cw_out, cw_out, part, part,
        ] + r_out,
        out_shape=[
            jax.ShapeDtypeStruct((t, P_WIDTH), BF16),
            jax.ShapeDtypeStruct((t, 128), F32),
            jax.ShapeDtypeStruct((bl, GDN_CONV, 512), F32),
            jax.ShapeDtypeStruct((bl, GDN_CONV, 512), F32),
            jax.ShapeDtypeStruct((bl, GDN_CONV, 512), F32),
            jax.ShapeDtypeStruct((bl, GDN_HEADS, 8, 128), F32),
            jax.ShapeDtypeStruct((bl, GDN_HEADS, 8, 128), F32),
        ] + r_shapes,
        scratch_shapes=[pltpu.VMEM((s, 128), F32)] * 5 + r_sems,
        compiler_params=_params(("arbitrary", "arbitrary")),
    )(proj, proj, conv_w, conv_w, conv_w, sc, norm_g, o_raw, states, dcat, *qkvg, *r_args)
    return tuple(outs[:7]) + (list(outs[7:]),)


def _mla_prep_fwd(proj, qg, kvg, wq, wkv, cos, sin, s, tm):
    t = proj.shape[0]
    tm = min(tm, s)
    nps = s // tm
    const = lambda shape: pl.BlockSpec(shape, lambda i: (0, 0))

    def body(pm_ref, qg_ref, kvg_ref, wq_ref, wkv_ref, cos_ref, sin_ref, qf_ref, kvf_ref, kr_ref):
        qf, kvf, kr = _mla_prep(pm_ref[...], qg_ref[...], kvg_ref[...], wq_ref[...], wkv_ref[...], cos_ref[...], sin_ref[...])
        qf_ref[...], kvf_ref[...], kr_ref[...] = qf.astype(BF16), kvf.astype(BF16), kr.astype(BF16)

    return pl.pallas_call(
        body, name="mla_prep_fwd", grid=(t // tm,),
        in_specs=[
            pl.BlockSpec((tm, 1024), lambda i: (i, P_MLA // 1024)),
            const((1, MLA_Q_LORA)), const((1, MLA_KV_LORA)), const(wq.shape), const(wkv.shape),
            pl.BlockSpec((tm, 128), lambda i: (i % nps, 0)), pl.BlockSpec((tm, 128), lambda i: (i % nps, 0)),
        ],
        out_specs=[pl.BlockSpec((tm, 1024), lambda i: (i, 0)), pl.BlockSpec((tm, 1024), lambda i: (i, 0)),
                   pl.BlockSpec((tm, 128), lambda i: (i, 0))],
        out_shape=[jax.ShapeDtypeStruct((t, 1024), BF16), jax.ShapeDtypeStruct((t, 1024), BF16),
                   jax.ShapeDtypeStruct((t, 128), BF16)],
        compiler_params=_params(("parallel",)),
    )(proj, qg, kvg, wq, wkv, cos, sin)


def _mla_prep_bwd(proj, qg, kvg, wq, wkv, cos, sin, dqf, dkvf, dkr, dab, dproj, s, tm):
    t = proj.shape[0]
    tm = min(tm, s)
    nps = s // tm
    const = lambda shape: pl.BlockSpec(shape, lambda i: (0, 0))

    def body(pm_ref, qg_ref, kvg_ref, wq_ref, wkv_ref, cos_ref, sin_ref, dqf_ref, dkvf_ref, dkr_ref, dab_ref, dp_in,
             dp_ref, dqg_ref, dkvg_ref, dwq_ref, dwkv_ref):
        del dp_in
        fn = lambda pm, qg_, kvg_, wq_, wkv_: _mla_prep(pm, qg_, kvg_, wq_, wkv_, cos_ref[...], sin_ref[...])
        _, vjp = jax.vjp(fn, pm_ref[...], qg_ref[...], kvg_ref[...], wq_ref[...].astype(F32), wkv_ref[...].astype(F32))
        dpm, dqg, dkvg, dwq, dwkv = vjp((dqf_ref[...], dkvf_ref[...], dkr_ref[...]))
        dp_ref[...] = jnp.concatenate([dab_ref[...], dpm[:, 128:]], axis=1).astype(BF16)

        @pl.when(pl.program_id(0) == 0)
        def _():
            dqg_ref[...] = jnp.zeros_like(dqg_ref)
            dkvg_ref[...] = jnp.zeros_like(dkvg_ref)
            dwq_ref[...] = jnp.zeros_like(dwq_ref)
            dwkv_ref[...] = jnp.zeros_like(dwkv_ref)

        dqg_ref[...] += dqg
        dkvg_ref[...] += dkvg
        dwq_ref[...] += dwq
        dwkv_ref[...] += dwkv

    rows = lambda w: pl.BlockSpec((tm, w), lambda i: (i, 0))
    return pl.pallas_call(
        body, name="mla_prep_bwd", grid=(t // tm,),
        in_specs=[
            pl.BlockSpec((tm, 1024), lambda i: (i, P_MLA // 1024)),
            const((1, MLA_Q_LORA)), const((1, MLA_KV_LORA)), const(wq.shape), const(wkv.shape),
            pl.BlockSpec((tm, 128), lambda i: (i % nps, 0)), pl.BlockSpec((tm, 128), lambda i: (i % nps, 0)),
            rows(1024), rows(1024), rows(128), rows(128),
            pl.BlockSpec(memory_space=pl.ANY),
        ],
        out_specs=[pl.BlockSpec((tm, 1024), lambda i: (i, P_MLA // 1024)),
                   const((1, MLA_Q_LORA)), const((1, MLA_KV_LORA)), const(wq.shape), const(wkv.shape)],
        out_shape=[jax.ShapeDtypeStruct(dproj.shape, dproj.dtype),
                   jax.ShapeDtypeStruct((1, MLA_Q_LORA), F32), jax.ShapeDtypeStruct((1, MLA_KV_LORA), F32),
                   jax.ShapeDtypeStruct(wq.shape, F32), jax.ShapeDtypeStruct(wkv.shape, F32)],
        input_output_aliases={11: 0},
        compiler_params=_params(("arbitrary",)),
    )(proj, qg, kvg, wq, wkv, cos, sin, dqf, dkvf, dkr, dab, dproj)


def _attn_fwd(qf, kvf, kr, cat, bl, s, tq):
    tq = min(tq, s)
    nq = s // tq

    def body(q_ref, kv_ref, kr_ref, cat_in, o_ref):
        del cat_in
        for i in range(nq):
            rows, keys = slice(i * tq, (i + 1) * tq), slice(0, (i + 1) * tq)
            o = _attn_block(q_ref[rows, :], kv_ref[keys, 0:128], kr_ref[keys, :], kv_ref[keys, 128:256], i * tq)
            o_ref[rows, :] = o.astype(o_ref.dtype)

    return pl.pallas_call(
        body, name="attn_fwd", grid=(bl, MLA_HEADS),
        in_specs=[
            pl.BlockSpec((s, 256), lambda b, h: (b, h)),
            pl.BlockSpec((s, 256), lambda b, h: (b, h)),
            pl.BlockSpec((s, 128), lambda b, h: (b, 0)),
            pl.BlockSpec(memory_space=pl.ANY),
        ],
        out_specs=pl.BlockSpec((s, 128), lambda b, h: (b, GDN_HEADS + h)),
        out_shape=jax.ShapeDtypeStruct(cat.shape, cat.dtype),
        input_output_aliases={3: 0},
        compiler_params=_params(("parallel", "parallel")),
    )(qf, kvf, kr, cat)


def _attn_bwd(qf, kvf, kr, dcat, bl, s, tq):
    tq = min(tq, s)
    nq = s // tq

    def body(q_ref, kv_ref, kr_ref, do_ref, dq_ref, dkv_ref, dkr_ref):
        dkv_ref[...] = jnp.zeros_like(dkv_ref)

        @pl.when(pl.program_id(1) == 0)
        def _():
            dkr_ref[...] = jnp.zeros_like(dkr_ref)

        for i in range(nq):
            rows, keys = slice(i * tq, (i + 1) * tq), slice(0, (i + 1) * tq)
            fn = functools.partial(_attn_block, q0=i * tq)
            f32 = lambda a: a.astype(F32)
            _, vjp = jax.vjp(fn, f32(q_ref[rows, :]), f32(kv_ref[keys, 0:128]), f32(kr_ref[keys, :]), f32(kv_ref[keys, 128:256]))
            dq_ref[rows, :], dkn, dkr, dv = vjp(do_ref[rows, :])
            dkv_ref[keys, 0:128] += dkn
            dkv_ref[keys, 128:256] += dv
            dkr_ref[keys, :] += dkr

    t = bl * s
    return pl.pallas_call(
        body, name="attn_bwd", grid=(bl, MLA_HEADS),
        in_specs=[
            pl.BlockSpec((s, 256), lambda b, h: (b, h)),
            pl.BlockSpec((s, 256), lambda b, h: (b, h)),
            pl.BlockSpec((s, 128), lambda b, h: (b, 0)),
            pl.BlockSpec((s, 128), lambda b, h: (b, GDN_HEADS + h)),
        ],
        out_specs=[
            pl.BlockSpec((s, 256), lambda b, h: (b, h)),
            pl.BlockSpec((s, 256), lambda b, h: (b, h)),
            pl.BlockSpec((s, 128), lambda b, h: (b, 0)),
        ],
        out_shape=[jax.ShapeDtypeStruct((t, 1024), F32), jax.ShapeDtypeStruct((t, 1024), F32),
                   jax.ShapeDtypeStruct((t, 128), F32)],
        compiler_params=_params(("parallel", "arbitrary")),
    )(qf, kvf, kr, dcat)


def _ln1_fwd(x, mix, g, b, tm):
    t = x.shape[0]
    tm = min(tm, t)

    def body(x_ref, mix_ref, g_ref, b_ref, r_ref, h_ref, hb_ref, xb_ref):
        r = ALPHA * x_ref[...] + mix_ref[...]
        r_ref[...] = r
        h = _layernorm(r, g_ref[...], b_ref[...])
        h_ref[...] = h
        hb_ref[...] = h.astype(BF16)
        xb_ref[...] = x_ref[...].astype(BF16)

    rows = pl.BlockSpec((tm, D_MODEL), lambda i: (i, 0))
    vec = pl.BlockSpec((1, D_MODEL), lambda i: (0, 0))
    return pl.pallas_call(
        body, name="ln1_fwd", grid=(t // tm,), in_specs=[rows, rows, vec, vec], out_specs=[rows] * 4,
        out_shape=[jax.ShapeDtypeStruct(x.shape, F32)] * 2 + [jax.ShapeDtypeStruct(x.shape, BF16)] * 2,
        compiler_params=_params(("parallel",)),
    )(x, mix, g, b)


def _ln1_bwd(r1, dr2, da, db_, g, b, tm):
    t = r1.shape[0]
    tm = min(tm, t)

    def body(r_ref, d2_ref, da_ref, db_ref, g_ref, b_ref, dr_ref, drb_ref, dg_ref, dbias_ref):
        dh = ALPHA * d2_ref[...] + da_ref[...] + db_ref[...]
        _, vjp = jax.vjp(_layernorm, r_ref[...], g_ref[...], b_ref[...])
        dr, dg, dbias = vjp(dh)
        dr_ref[...] = dr
        drb_ref[...] = dr.astype(BF16)

        @pl.when(pl.program_id(0) == 0)
        def _():
            dg_ref[...] = jnp.zeros_like(dg_ref)
            dbias_ref[...] = jnp.zeros_like(dbias_ref)

        dg_ref[...] += dg
        dbias_ref[...] += dbias

    rows = pl.BlockSpec((tm, D_MODEL), lambda i: (i, 0))
    vec = pl.BlockSpec((1, D_MODEL), lambda i: (0, 0))
    return pl.pallas_call(
        body, name="ln1_bwd", grid=(t // tm,), in_specs=[rows] * 4 + [vec, vec], out_specs=[rows, rows, vec, vec],
        out_shape=[jax.ShapeDtypeStruct(r1.shape, F32), jax.ShapeDtypeStruct(r1.shape, BF16)]
        + [jax.ShapeDtypeStruct((1, D_MODEL), F32)] * 2,
        compiler_params=_params(("arbitrary",)),
    )(r1, dr2, da, db_, g, b)


def _ffn_act_fwd(u, conv_w, conv_b, bl, s, cb):
    nj = D_FF // cb

    def body(ug_ref, uu_ref, wg_ref, wu_ref, bg_ref, bu_ref, act_ref):
        def store(rows, act):
            act_ref[rows, :] = act.astype(BF16)

        _rows_apply(_ffn_act, [lambda r: ug_ref[r, :], lambda r: uu_ref[r, :]],
                    [wg_ref[...], wu_ref[...], bg_ref[...], bu_ref[...]], store, s, cb, CONV_HALO)

    return pl.pallas_call(
        body, name="ffn_act_fwd", grid=(bl, nj),
        in_specs=[
            pl.BlockSpec((s, cb), lambda b, j: (b, j)), pl.BlockSpec((s, cb), lambda b, j: (b, nj + j)),
            pl.BlockSpec((FFN_CONV, cb), lambda b, j: (0, j)), pl.BlockSpec((FFN_CONV, cb), lambda b, j: (0, nj + j)),
            pl.BlockSpec((1, cb), lambda b, j: (0, j)), pl.BlockSpec((1, cb), lambda b, j: (0, nj + j)),
        ],
        out_specs=pl.BlockSpec((s, cb), lambda b, j: (b, j)),
        out_shape=jax.ShapeDtypeStruct((bl * s, D_FF), BF16),
        compiler_params=_params(("parallel", "parallel")),
    )(u, u, conv_w, conv_w, conv_b, conv_b)


def _ffn_act_bwd(u, conv_w, conv_b, dact, bl, s, cb):
    nj = D_FF // cb

    def body(ug_ref, uu_ref, wg_ref, wu_ref, bg_ref, bu_ref, da_ref, du_ref, dwg_ref, dwu_ref, dbg_ref, dbu_ref, acc):
        def store_into(half):
            def store(rows, value, add):
                if add:
                    acc[half, rows, :] += value
                else:
                    acc[half, rows, :] = value
            return store

        dwg_ref[...], dwu_ref[...], dbg_ref[...], dbu_ref[...] = _rows_vjp(
            _ffn_act, [lambda r: ug_ref[r, :], lambda r: uu_ref[r, :]], [wg_ref[...], wu_ref[...], bg_ref[...], bu_ref[...]],
            lambda r: da_ref[r, :], [store_into(0), store_into(1)], s, cb, CONV_HALO)
        du_ref[...] = acc[...].astype(BF16)

    t = bl * s
    blk = pl.BlockSpec((s, cb), lambda b, j: (b, j))
    wpart = pl.BlockSpec((None, FFN_CONV, cb), lambda b, j: (b, 0, j))
    bpart = pl.BlockSpec((None, 1, cb), lambda b, j: (b, 0, j))
    return pl.pallas_call(
        body, name="ffn_act_bwd", grid=(bl, nj),
        in_specs=[
            blk, pl.BlockSpec((s, cb), lambda b, j: (b, nj + j)),
            pl.BlockSpec((FFN_CONV, cb), lambda b, j: (0, j)), pl.BlockSpec((FFN_CONV, cb), lambda b, j: (0, nj + j)),
            pl.BlockSpec((1, cb), lambda b, j: (0, j)), pl.BlockSpec((1, cb), lambda b, j: (0, nj + j)),
            blk,
        ],
        out_specs=[pl.BlockSpec((2, s, cb), lambda b, j: (0, b, j)), wpart, wpart, bpart, bpart],
        out_shape=[jax.ShapeDtypeStruct((2, t, D_FF), BF16)] + [jax.ShapeDtypeStruct((bl, FFN_CONV, D_FF), F32)] * 2
        + [jax.ShapeDtypeStruct((bl, 1, D_FF), F32)] * 2,
        scratch_shapes=[pltpu.VMEM((2, s, cb), F32)],
        compiler_params=_params(("parallel", "parallel")),
    )(u, u, conv_w, conv_w, conv_b, conv_b, dact)


def _head(h1, ffn, gpre, pp, bgate, g2, b2, target, tm):
    t = h1.shape[0]
    tm = min(tm, t)

    def body(h1_ref, ffn_ref, gp_ref, pp_ref, bg_ref, g2_ref, b2_ref, tg_ref,
             dr_ref, drb_ref, dgp_ref, dpp_ref, loss_ref, dbg_ref, dg2_ref, db2_ref):
        fn = functools.partial(_head_loss, target=tg_ref[...])
        loss, vjp = jax.vjp(fn, h1_ref[...], ffn_ref[...], gp_ref[...], pp_ref[...], bg_ref[...], g2_ref[...], b2_ref[...])
        _, dffn, dgp, dpp, dbg, dg2, db2 = vjp(jnp.ones((1, 1), F32))
        dr_ref[...] = dffn
        drb_ref[...], dgp_ref[...], dpp_ref[...] = dffn.astype(BF16), dgp.astype(BF16), dpp.astype(BF16)

        @pl.when(pl.program_id(0) == 0)
        def _():
            loss_ref[...] = jnp.zeros_like(loss_ref)
            dbg_ref[...] = jnp.zeros_like(dbg_ref)
            dg2_ref[...] = jnp.zeros_like(dg2_ref)
            db2_ref[...] = jnp.zeros_like(db2_ref)

        loss_ref[...] += jnp.broadcast_to(loss, loss_ref.shape)
        dbg_ref[...] += dbg
        dg2_ref[...] += dg2
        db2_ref[...] += db2

    rows = pl.BlockSpec((tm, D_MODEL), lambda i: (i, 0))
    vec = pl.BlockSpec((1, D_MODEL), lambda i: (0, 0))
    return pl.pallas_call(
        body, name="head", grid=(t // tm,), in_specs=[rows] * 4 + [vec] * 3 + [rows],
        out_specs=[rows] * 4 + [pl.BlockSpec((8, 128), lambda i: (0, 0))] + [vec] * 3,
        out_shape=[jax.ShapeDtypeStruct(h1.shape, F32)] + [jax.ShapeDtypeStruct(h1.shape, BF16)] * 3
        + [jax.ShapeDtypeStruct((8, 128), F32)]
        + [jax.ShapeDtypeStruct((1, D_MODEL), F32)] * 3,
        compiler_params=_params(("arbitrary",)),
    )(h1, ffn, gpre, pp, bgate, g2, b2, target)


def _adam_update(g, w_ref, m_ref, v_ref, g_ref, d_ref, nm_ref, nv_ref):
    m2 = ADAM_B1 * m_ref[...] + (1.0 - ADAM_B1) * g
    v2 = ADAM_B2 * v_ref[...] + (1.0 - ADAM_B2) * jnp.square(g)
    m_hat = m2 / (1.0 - ADAM_B1 ** ADAM_STEP)
    v_hat = v2 / (1.0 - ADAM_B2 ** ADAM_STEP)
    g_ref[...] = g
    d_ref[...] = -ADAM_LR * (m_hat / (jnp.sqrt(v_hat) + ADAM_EPS) + ADAM_WD * w_ref[...])
    nm_ref[...] = m2
    nv_ref[...] = v2


def _row_tile(rows, cols, limit_bytes=512 * 1024):
    best = None
    for t in range(HALF_ROWS_QUANTUM, rows + 1, HALF_ROWS_QUANTUM):
        if rows % t == 0 and t * cols * 4 <= limit_bytes:
            best = t
    return best or rows


def _adamw_reduced(recv, w, m, v, name):
    a, b = w.shape
    ta = _row_tile(a, b)

    def body(recv_ref, w_ref, m_ref, v_ref, g_ref, d_ref, nm_ref, nv_ref):
        c = lax.axis_index("c")
        for core in range(2):
            @pl.when(c == core)
            def _():
                got = [recv_ref[k].astype(F32) for k in range(N_DEV)]
                same = [got[7], got[0], got[1], got[2]]
                other = got[3:7]
                core0, core1 = (same, other) if core == 0 else (other, same)
                g = core0[0] + core1[0]
                for r in range(1, N_CHIPS):
                    g = (g + core0[r]) + core1[r]
                _adam_update(g, w_ref, m_ref, v_ref, g_ref, d_ref, nm_ref, nv_ref)

    blk = pl.BlockSpec((ta, b), lambda i: (i, 0))
    return pl.pallas_call(
        body, name=name, grid=(a // ta,),
        in_specs=[pl.BlockSpec((N_DEV, ta, b), lambda i: (0, i, 0)), blk, blk, blk], out_specs=[blk] * 4,
        out_shape=[jax.ShapeDtypeStruct(w.shape, F32)] * 4, compiler_params=_params(("parallel",)),
    )(recv, w, m, v)


def _adamw_small(g, w, m, v):
    def body(g_in, w_ref, m_ref, v_ref, g_ref, d_ref, nm_ref, nv_ref):
        _adam_update(g_in[...], w_ref, m_ref, v_ref, g_ref, d_ref, nm_ref, nv_ref)

    blk = pl.BlockSpec(w.shape, lambda i: (0, 0))
    return pl.pallas_call(
        body, name="adamw_small", grid=(1,), in_specs=[blk] * 4, out_specs=[blk] * 4,
        out_shape=[jax.ShapeDtypeStruct(w.shape, F32)] * 4, compiler_params=_params(("arbitrary",)),
    )(g, w, m, v)


def _remote(src, dst, send_sem, recv_sem, device):
    return pltpu.make_async_remote_copy(src_ref=src, dst_ref=dst, send_sem=send_sem, recv_sem=recv_sem,
                                        device_id=device, device_id_type=MESH)


def _place():
    x, y, c = lax.axis_index("x"), lax.axis_index("y"), lax.axis_index("c")
    return x, y, c, 2 * x + y, [(1 - x, y), (x, 1 - y), (1 - x, 1 - y)]


HBM_REF = pl.BlockSpec(memory_space=pl.ANY)
HALF_ROWS_QUANTUM = 16


def _gather_sems(n):
    return [pltpu.SemaphoreType.DMA((3 * n,))] * 4 + [pltpu.SemaphoreType.DMA((n,))]


def _gather_copies(ins, outs, sems):
    send_s, recv_s, fsend_s, frecv_s, local_s = sems
    x, y, c, me, chips = _place()
    local, sends, steps = [], [], []
    for i, (src, dst) in enumerate(zip(ins, outs)):
        local.append(pltpu.make_async_copy(src, dst.at[me], local_s.at[i]))
        half = src.shape[0] // 2
        split = src.shape[0] % (2 * HALF_ROWS_QUANTUM) == 0
        if split:
            mine = pl.ds(pl.multiple_of(c * half, HALF_ROWS_QUANTUM), half)
            theirs = pl.ds(pl.multiple_of((1 - c) * half, HALF_ROWS_QUANTUM), half)
        for r, (px, py) in enumerate(chips):
            k, peer = 3 * i + r, 2 * px + py
            if split:
                sends.append(_remote(src.at[mine], dst.at[me, mine], send_s.at[k], recv_s.at[k], (px, py, c)))
                landed = dst.at[peer, mine]
                steps.append((_remote(src.at[mine], landed, send_s.at[k], recv_s.at[k], (px, py, c)),
                              _remote(landed, landed, fsend_s.at[k], frecv_s.at[k], (x, y, 1 - c)),
                              _remote(dst.at[peer, theirs], dst.at[peer, theirs], fsend_s.at[k], frecv_s.at[k], (x, y, 1 - c))))
            else:
                sends.append(_remote(src, dst.at[me], send_s.at[k], recv_s.at[k], (px, py, c)))
                steps.append((_remote(src, dst.at[peer], send_s.at[k], recv_s.at[k], (px, py, c)), None, None))
    return local, sends, steps


def _scatter_sems(n):
    return [pltpu.SemaphoreType.DMA((4 * n,))] * 2 + [pltpu.SemaphoreType.DMA((3 * n,))] * 2 + [pltpu.SemaphoreType.DMA((n,))]


def _scatter_copies(ins, outs, sems):
    send_s, recv_s, fsend_s, frecv_s, local_s = sems
    x, y, c, me, chips = _place()
    local, sends, steps = [], [], []
    for i, (src, dst) in enumerate(zip(ins, outs)):
        local.append(pltpu.make_async_copy(src.at[me], dst.at[N_DEV - 1], local_s.at[i]))
        for r, (px, py) in enumerate(chips):
            k = 4 * i + r
            cp = _remote(src.at[2 * px + py], dst.at[r], send_s.at[k], recv_s.at[k], (px, py, c))
            fwd = _remote(dst.at[r], dst.at[4 + r], fsend_s.at[3 * i + r], frecv_s.at[3 * i + r], (x, y, 1 - c))
            sends.append(cp)
            steps.append((cp, fwd, fwd))
        k = 4 * i + 3
        cp = _remote(src.at[me], dst.at[3], send_s.at[k], recv_s.at[k], (x, y, 1 - c))
        sends.append(cp)
        steps.append((cp, None, None))
    return local, sends, steps


def _exchange_start(plan):
    local, sends, _ = plan
    for cp in local + sends:
        cp.start()


def _exchange_pass_on(plan):
    for arrival, pass_on, _ in plan[2]:
        arrival.wait_recv()
        if pass_on is not None:
            pass_on.start()


def _exchange_finish(plan):
    local, sends, steps = plan
    for _, pass_on, passed in steps:
        if pass_on is not None:
            passed.wait_recv()
    for cp in sends:
        cp.wait_send()
    for _, pass_on, _ in steps:
        if pass_on is not None:
            pass_on.wait_send()
    for cp in local:
        cp.wait()


def _exchange_call(arrays, copies, sems, out_shapes, name):
    n = len(arrays)

    def body(*refs):
        plan = copies(refs[:n], refs[n:2 * n], refs[2 * n:])
        _exchange_start(plan)
        _exchange_pass_on(plan)
        _exchange_finish(plan)

    return pl.pallas_call(
        body, name=name, in_specs=[HBM_REF] * n, out_specs=[HBM_REF] * n, out_shape=out_shapes,
        scratch_shapes=sems(n), compiler_params=pltpu.CompilerParams(has_side_effects=True),
    )(*arrays)


def _gather_call(shards, name):
    shapes = [jax.ShapeDtypeStruct((N_CHIPS,) + a.shape, a.dtype) for a in shards]
    return _exchange_call(shards, _gather_copies, _gather_sems, shapes, name)


def _all_reduce_small(a):
    def body(in_ref, out_ref, slots, send_sems, recv_sems):
        x, y, c = lax.axis_index("x"), lax.axis_index("y"), lax.axis_index("c")
        me = 4 * x + 2 * y + c
        slots[0] = in_ref[...]
        sends = []
        for r in range(1, N_DEV):
            peer = (x ^ (r >> 2), y ^ ((r >> 1) & 1), c ^ (r & 1))
            sends.append(pltpu.make_async_remote_copy(src_ref=in_ref, dst_ref=slots.at[r], send_sem=send_sems.at[r],
                                                      recv_sem=recv_sems.at[r], device_id=peer, device_id_type=MESH))
        for cp in sends:
            cp.start()
        for cp in sends:
            cp.wait_recv()
        acc = slots[me]
        for dev in range(1, N_DEV):
            acc = acc + slots[dev ^ me]
        out_ref[...] = acc
        for cp in sends:
            cp.wait_send()

    return pl.pallas_call(
        body, name="small_all_reduce",
        in_specs=[pl.BlockSpec(memory_space=pltpu.VMEM)], out_specs=pl.BlockSpec(memory_space=pltpu.VMEM),
        out_shape=jax.ShapeDtypeStruct(a.shape, a.dtype),
        scratch_shapes=[pltpu.VMEM((N_DEV,) + a.shape, a.dtype), pltpu.SemaphoreType.DMA((N_DEV,)),
                        pltpu.SemaphoreType.DMA((N_DEV,))],
        compiler_params=pltpu.CompilerParams(has_side_effects=True),
    )(a)


SHARDED = ["w_in", "mla_w_q_up", "mla_w_kv_up", "w_out", "ffn_w_up", "ffn_w_down", "ple_w_gate", "ple_w_proj",
           "gdn_conv_w", "ffn_conv_w"]
SHARD_AXIS = {"w_in": 1, "mla_w_q_up": 1, "mla_w_kv_up": 1, "w_out": 0, "ffn_w_up": 1, "ffn_w_down": 0,
              "ple_w_gate": 0, "ple_w_proj": 1, "gdn_conv_w": 1, "ffn_conv_w": 1}
SMALL = ["gdn_a_log", "gdn_dt_bias", "gdn_norm_g", "mla_q_norm_g", "mla_kv_norm_g", "ln1_g", "ln1_b", "ffn_conv_b",
         "ple_b_gate", "ln2_g", "ln2_b"]
WEIGHTS = ["w_in", "gdn_conv_w", "gdn_a_log", "gdn_dt_bias", "gdn_norm_g", "mla_q_norm_g", "mla_w_q_up", "mla_kv_norm_g",
           "mla_w_kv_up", "w_out", "ln1_g", "ln1_b", "ffn_w_up", "ffn_conv_w", "ffn_conv_b", "ffn_w_down", "ple_w_gate",
           "ple_b_gate", "ple_w_proj", "ln2_g", "ln2_b"]
F32_ON_WIRE = ("gdn_conv_w", "ffn_conv_w")
GATHER_EARLY = ["w_in", "gdn_conv_w"]
GATHER_LATE = ["mla_w_q_up", "mla_w_kv_up", "w_out", "ffn_w_up", "ffn_conv_w", "ffn_w_down", "ple_w_gate", "ple_w_proj"]
SCATTER_EARLY = ["ffn_w_up", "ffn_conv_w", "ffn_w_down", "ple_w_gate", "ple_w_proj", "w_out"]
SCATTER_LATE = ["w_in", "gdn_conv_w", "mla_w_q_up", "mla_w_kv_up"]
PACK_COLS = 1024
PACK_ROW_TILE = 8


def _join_blocks(blocks, axis):
    n, a, b = blocks.shape
    if axis == 0:
        return blocks.reshape(n * a, b)
    return jnp.transpose(blocks, (1, 0, 2)).reshape(a, n * b)


def _split_blocks(full, axis):
    if axis == 0:
        return full.reshape(N_CHIPS, full.shape[0] // N_CHIPS, full.shape[1])
    a, nb = full.shape
    return jnp.transpose(full.reshape(a, N_CHIPS, nb // N_CHIPS), (1, 0, 2))


def _pack(arrays):
    flat = jnp.concatenate([a.reshape(-1) for a in arrays])
    quantum = PACK_COLS * PACK_ROW_TILE
    padded = -(-flat.shape[0] // quantum) * quantum
    return jnp.pad(flat, (0, padded - flat.shape[0])).reshape(-1, PACK_COLS)


def _unpack(packed, shapes):
    flat = packed.reshape(-1)
    out, off = [], 0
    for shp in shapes:
        n = int(np.prod(shp))
        out.append(flat[off:off + n].reshape(shp))
        off += n
    return out


def kernel(x, p, w_in, gdn_conv_w, gdn_a_log, gdn_dt_bias, gdn_norm_g, mla_q_norm_g, mla_w_q_up, mla_kv_norm_g, mla_w_kv_up, w_out, ln1_g, ln1_b, ffn_w_up, ffn_conv_w, ffn_conv_b, ffn_w_down, ple_w_gate, ple_b_gate, ple_w_proj, ln2_g, ln2_b, loss_target, m_w_in, m_gdn_conv_w, m_gdn_a_log, m_gdn_dt_bias, m_gdn_norm_g, m_mla_q_norm_g, m_mla_w_q_up, m_mla_kv_norm_g, m_mla_w_kv_up, m_w_out, m_ln1_g, m_ln1_b, m_ffn_w_up, m_ffn_conv_w, m_ffn_conv_b, m_ffn_w_down, m_ple_w_gate, m_ple_b_gate, m_ple_w_proj, m_ln2_g, m_ln2_b, v_w_in, v_gdn_conv_w, v_gdn_a_log, v_gdn_dt_bias, v_gdn_norm_g, v_mla_q_norm_g, v_mla_w_q_up, v_mla_kv_norm_g, v_mla_w_kv_up, v_w_out, v_ln1_g, v_ln1_b, v_ffn_w_up, v_ffn_conv_w, v_ffn_conv_b, v_ffn_w_down, v_ple_w_gate, v_ple_b_gate, v_ple_w_proj, v_ln2_g, v_ln2_b):
    given = dict(locals())
    wsh = {n: given[n][0] for n in WEIGHTS}
    msh = {n: given["m_" + n][0] for n in WEIGHTS}
    vsh = {n: given["v_" + n][0] for n in WEIGHTS}
    bl, s, _ = x.shape
    t = bl * s
    xt = x.reshape(t, D_MODEL)
    pt = p.reshape(t, PLE_DIM)
    target = loss_target.reshape(t, D_MODEL)

    wire = lambda n: wsh[n] if n in F32_ON_WIRE else wsh[n].astype(BF16)
    early = _gather_call([wire(n) for n in GATHER_EARLY], "weights_gather_early")
    full = {n: _join_blocks(g, SHARD_AXIS[n]) for n, g in zip(GATHER_EARLY, early)}
    late_shards = [wire(n) for n in GATHER_LATE]
    late_ride = (late_shards, _gather_copies, _gather_sems,
                 [jax.ShapeDtypeStruct((N_CHIPS,) + a.shape, a.dtype) for a in late_shards], 0.75)

    in_cols, q_cols = _w_in_cols(), _w_q_cols()
    w_in_p = _pad_cols(full["w_in"], in_cols)
    gconv = full["gdn_conv_w"]
    row = lambda a: a.reshape(1, -1)
    sc = jnp.zeros((8, 128), F32).at[0, :GDN_HEADS].set(wsh["gdn_a_log"]).at[1, :GDN_HEADS].set(wsh["gdn_dt_bias"])
    norm_g, qg, kvg = row(wsh["gdn_norm_g"]), row(wsh["mla_q_norm_g"]), row(wsh["mla_kv_norm_g"])
    g1, b1, g2, b2 = row(wsh["ln1_g"]), row(wsh["ln1_b"]), row(wsh["ln2_g"]), row(wsh["ln2_b"])
    fbias, bgate = row(wsh["ffn_conv_b"]), row(wsh["ple_b_gate"])

    inv = ROPE_THETA ** (-jnp.arange(0, MLA_ROPE, 2, dtype=F32) / MLA_ROPE)
    ang = jnp.arange(s, dtype=F32)[:, None] * inv[None, :]
    zero = jnp.zeros_like(ang)
    cos_t = jnp.concatenate([jnp.cos(ang), zero, jnp.cos(ang), zero], axis=1)
    sin_t = jnp.concatenate([-jnp.sin(ang), zero, jnp.sin(ang), zero], axis=1)

    proj = _matmul(xt, w_in_p, name="proj", tm=1024)
    cat, o_raw, states, qkvg, late = _gdn_fwd(proj, gconv, sc, norm_g, bl, s, late_ride)
    full.update({n: _join_blocks(g, SHARD_AXIS[n]) for n, g in zip(GATHER_LATE, late)})
    w_o, w_up, w_down = full["w_out"], full["ffn_w_up"], full["ffn_w_down"]
    w_gate, w_proj, fconv = full["ple_w_gate"], full["ple_w_proj"], full["ffn_conv_w"]
    w_q_p, w_kv = _pad_cols(full["mla_w_q_up"], q_cols), full["mla_w_kv_up"]
    qf, kvf, kr = _mla_prep_fwd(proj, qg, kvg, w_q_p, w_kv, cos_t, sin_t, s, 256)
    cat = _attn_fwd(qf, kvf, kr, cat, bl, s, 512)
    wide = dict(tm=1024, tn=1024)
    mix = _matmul(cat, w_o, name="mix", **wide)
    r1, h1, h1b, xb = _ln1_fwd(xt, mix, g1, b1, 256)
    u = _matmul(h1b, w_up, name="ffn_up", tm=1024, tn=1408)
    act = _ffn_act_fwd(u, fconv, fbias, bl, s, 256)
    ffn = _matmul(act, w_down, name="ffn_down", tk=1408, **wide)
    gpre = _matmul(h1b, w_gate, name="ple_gate", **wide)
    pp = _matmul(pt, w_proj, name="ple_proj", **wide)
    dr2, dr2b, dgpre, dpp, loss_acc, dbgate, dg2, db2 = _head(h1, ffn, gpre, pp, bgate, g2, b2, target, 256)

    dact = _matmul(dr2b, w_down, name="d_act", tb=True, tm=1024, tn=1408)
    long_k = dict(ta=True, tk=2048)
    d_w_down = _matmul(act, dr2b, name="dw_down", tm=1408, tn=1024, **long_k)
    du, dfcw_g, dfcw_u, dfcb_g, dfcb_u = _ffn_act_bwd(u, fconv, fbias, dact, bl, s, 256)
    dh1_a = _matmul(du, w_up, name="dh1_ffn", tb=True, tk=1408, a_halves=True, **wide)
    dh1_b = _matmul(dgpre, w_gate, name="dh1_ple", tb=True, **wide)
    d_w_up = _matmul(h1b, du, name="dw_up", tn=1408, b_halves=True, **long_k)
    d_w_gate = _matmul(h1b, dgpre, name="dw_gate", **long_k, **wide)
    d_w_proj = _matmul(pt, dpp, name="dw_proj", ta=True, tn=1024)
    dr1, dr1b, dg1, db1 = _ln1_bwd(r1, dr2, dh1_a, dh1_b, g1, b1, 256)
    dcat = _matmul(dr1b, w_o, name="d_cat", tb=True, **wide)
    d_w_o = _matmul(cat, dr1b, name="dw_out", **long_k, **wide)

    gfull = {
        "ffn_w_up": d_w_up, "ffn_w_down": d_w_down, "ple_w_gate": d_w_gate, "ple_w_proj": d_w_proj, "w_out": d_w_o,
        "ffn_conv_w": jnp.concatenate([jnp.sum(dfcw_g, 0), jnp.sum(dfcw_u, 0)], axis=1),
    }
    slabs = {n: _split_blocks(gfull[n], SHARD_AXIS[n]).astype(BF16) for n in SCATTER_EARLY}
    early_slabs = [slabs[n] for n in SCATTER_EARLY]
    early_ride = (early_slabs, _scatter_copies, _scatter_sems,
                  [jax.ShapeDtypeStruct((N_DEV,) + a.shape[1:], a.dtype) for a in early_slabs], 0.7)
    dproj, dab, dcwq, dcwk, dcwv, dsc, dng, early_recv = _gdn_bwd(proj, gconv, sc, norm_g, o_raw, states, qkvg, dcat, bl, s,
                                                                  early_ride)
    received = dict(zip(SCATTER_EARLY, early_recv))
    dqf, dkvf, dkr = _attn_bwd(qf, kvf, kr, dcat, bl, s, 256)
    dproj, dqg, dkvg, d_w_q_p, d_w_kv = _mla_prep_bwd(proj, qg, kvg, w_q_p, w_kv, cos_t, sin_t, dqf, dkvf, dkr, dab, dproj, s, 256)
    d_w_in_p = _matmul(xb, dproj, name="dw_in", **long_k, **wide)

    gfull.update({
        "w_in": _unpad_cols(d_w_in_p, in_cols, D_IN),
        "mla_w_q_up": _unpad_cols(d_w_q_p, q_cols, MLA_HEADS * (MLA_NOPE + MLA_ROPE)),
        "mla_w_kv_up": d_w_kv,
        "gdn_conv_w": jnp.concatenate([jnp.sum(dcwq, 0), jnp.sum(dcwk, 0), jnp.sum(dcwv, 0)], axis=1),
    })
    slabs.update({n: _split_blocks(gfull[n], SHARD_AXIS[n]).astype(BF16) for n in SCATTER_LATE})
    late_slabs = [slabs[n] for n in SCATTER_LATE]
    late_scatter = (late_slabs, _scatter_copies, _scatter_sems,
                    [jax.ShapeDtypeStruct((N_DEV,) + a.shape[1:], a.dtype) for a in late_slabs], 0.85)
    grad_x, late_recv = _matmul(dproj, w_in_p, name="d_x", tb=True, add=dr1, add_scale=ALPHA, ride=late_scatter, **wide)
    received.update(zip(SCATTER_LATE, late_recv))
    dsc_sum = jnp.sum(dsc, axis=(0, 1))
    gsmall = {
        "gdn_a_log": dsc_sum[0, :GDN_HEADS], "gdn_dt_bias": dsc_sum[1, :GDN_HEADS],
        "gdn_norm_g": jnp.sum(dng[:, :, 0, :], axis=(0, 1)),
        "mla_q_norm_g": dqg[0], "mla_kv_norm_g": dkvg[0], "ln1_g": dg1[0], "ln1_b": db1[0],
        "ffn_conv_b": jnp.concatenate([jnp.sum(dfcb_g, 0), jnp.sum(dfcb_u, 0)], axis=1)[0],
        "ple_b_gate": dbgate[0], "ln2_g": dg2[0], "ln2_b": db2[0],
    }

    big = [{}, {}, {}, {}]
    for n in SHARDED:
        for kind, val in enumerate(_adamw_reduced(received[n], wsh[n], msh[n], vsh[n], "adamw_" + n)):
            big[kind][n] = val

    small_shapes = [wsh[n].shape for n in SMALL]
    gsum = _all_reduce_small(_pack([gsmall[n] for n in SMALL]))
    spacks = _adamw_small(gsum, _pack([wsh[n] for n in SMALL]), _pack([msh[n] for n in SMALL]), _pack([vsh[n] for n in SMALL]))
    small = [dict(zip(SMALL, _unpack(pk, small_shapes))) for pk in spacks]

    loss = lax.psum(loss_acc[0, 0], ("x", "y", "c"))
    outs = [loss, grad_x.reshape(x.shape)]
    for kind in range(4):
        for n in WEIGHTS:
            val = big[kind][n] if n in big[kind] else small[kind][n]
            outs.append(val[None])
    return tuple(outs)
```

```python
import functools
import math

import numpy as np
import jax
import jax.numpy as jnp
from jax import lax
from jax.experimental import pallas as pl
from jax.experimental.pallas import tpu as pltpu

F32 = jnp.float32
BF16 = jnp.bfloat16

D_MODEL = 1024
CHUNK = 64
PLE_DIM = 256
GDN_HEADS = 4
GDN_DK = 128
GDN_DV = 128
GDN_CONV = 4
MLA_HEADS = 4
MLA_NOPE = 128
MLA_ROPE = 64
MLA_V = 128
MLA_Q_LORA = 384
MLA_KV_LORA = 256
ROPE_THETA = 10000.0
D_FF = 2816
FFN_CONV = 3
DEPTH = 1
ALPHA = (2.0 * DEPTH) ** 0.25
NORM_EPS = 1e-6
GDN_QK = GDN_HEADS * GDN_DK
GDN_VW = GDN_HEADS * GDN_DV
D_IN = 2 * GDN_QK + 2 * GDN_VW + 2 * GDN_HEADS + MLA_Q_LORA + MLA_KV_LORA + MLA_ROPE
ATT_SCALE = (MLA_NOPE + MLA_ROPE) ** -0.5

ADAM_LR = 0.001
ADAM_B1 = 0.9
ADAM_B2 = 0.999
ADAM_EPS = 1e-08
ADAM_WD = 0.01
ADAM_STEP = 10

LANES = 128
VMEM_LIMIT = 60 * 1024 * 1024
GDN_FWD_GROUP = 16
GDN_BWD_GROUP = 16
N_CHIPS = 4
N_DEV = 8

P_WIDTH = 3072
P_MLA = 2048
MESH = pl.DeviceIdType.MESH


def _rope_slot(j):
    return j if j < MLA_ROPE // 2 else 64 + (j - MLA_ROPE // 2)


def _w_in_cols():
    idx = -np.ones((P_WIDTH,), np.int64)
    for h in range(GDN_HEADS):
        base = h * 512
        idx[base:base + 128] = np.arange(128) + h * GDN_DK
        idx[base + 128:base + 256] = np.arange(128) + GDN_QK + h * GDN_DK
        idx[base + 256:base + 384] = np.arange(128) + 2 * GDN_QK + h * GDN_DV
        idx[base + 384:base + 512] = np.arange(128) + 2 * GDN_QK + GDN_VW + h * GDN_DV
    o_a = 2 * GDN_QK + 2 * GDN_VW
    idx[P_MLA:P_MLA + 2 * GDN_HEADS] = np.arange(2 * GDN_HEADS) + o_a
    o_cq = o_a + 2 * GDN_HEADS
    idx[P_MLA + 128:P_MLA + 512] = np.arange(MLA_Q_LORA) + o_cq
    o_ckv = o_cq + MLA_Q_LORA
    idx[P_MLA + 512:P_MLA + 768] = np.arange(MLA_KV_LORA) + o_ckv
    o_kr = o_ckv + MLA_KV_LORA
    for j in range(MLA_ROPE):
        idx[P_MLA + 768 + _rope_slot(j)] = o_kr + j
    return idx


def _w_q_cols():
    idx = -np.ones((MLA_HEADS * 256,), np.int64)
    for h in range(MLA_HEADS):
        o = h * (MLA_NOPE + MLA_ROPE)
        idx[h * 256:h * 256 + 128] = np.arange(128) + o
        for j in range(MLA_ROPE):
            idx[h * 256 + 128 + _rope_slot(j)] = o + MLA_NOPE + j
    return idx


def _pad_cols(w, idx):
    safe = np.where(idx >= 0, idx, 0)
    return jnp.where(jnp.asarray(idx >= 0)[None, :], w[:, safe], 0.0)


def _unpad_cols(wp, idx, n):
    inv = np.zeros((n,), np.int64)
    inv[idx[idx >= 0]] = np.nonzero(idx >= 0)[0]
    return wp[:, inv]


def _dot(a, b, ca, cb, precision=None):
    if precision is None:
        a = a.astype(BF16)
        b = b.astype(BF16)
    return lax.dot_general(a, b, (((ca,), (cb,)), ((), ())), preferred_element_type=F32, precision=precision)


@jax.custom_vjp
def mm(a, b):
    return _dot(a, b, 1, 0)


@jax.custom_vjp
def mm_nt(a, b):
    return _dot(a, b, 1, 1)


@jax.custom_vjp
def mm_tn(a, b):
    return _dot(a, b, 0, 0)


mm.defvjp(lambda a, b: (mm(a, b), (a, b)), lambda r, g: (mm_nt(g, r[1]), mm_tn(r[0], g)))
mm_nt.defvjp(lambda a, b: (mm_nt(a, b), (a, b)), lambda r, g: (mm(g, r[1]), mm_tn(g, r[0])))
mm_tn.defvjp(lambda a, b: (mm_tn(a, b), (a, b)), lambda r, g: (mm_nt(r[1], g), mm(r[0], g)))

def _split(a):
    hi = a.astype(BF16)
    return hi, (a - hi.astype(F32)).astype(BF16)


def _dot3(a, b, ca, cb):
    a_hi, a_lo = _split(a)
    b_hi, b_lo = _split(b)
    return (_dot(a_hi, b_hi, ca, cb) + _dot(a_hi, b_lo, ca, cb)) + _dot(a_lo, b_hi, ca, cb)


def _shift_rows(x, s):
    return x if s == 0 else pltpu.roll(x, s % x.shape[0], 0)


def _row(w, j):
    tap = lax.broadcasted_iota(jnp.int32, w.shape, 0)
    return jnp.sum(jnp.where(tap == j, w, 0.0), axis=0, keepdims=True)


@jax.custom_vjp
def dwconv(x, w):
    k = w.shape[0]
    y = _row(w, k - 1) * x
    for j in range(k - 1):
        y = y + _row(w, j) * _shift_rows(x, k - 1 - j)
    return y


def _dwconv_fwd(x, w):
    return dwconv(x, w), (x, w)


def _dwconv_bwd(res, dy):
    x, w = res
    k = w.shape[0]
    dx = _row(w, k - 1) * dy
    tap = lax.broadcasted_iota(jnp.int32, w.shape, 0)
    dw = jnp.where(tap == k - 1, jnp.sum(dy * x, axis=0, keepdims=True), 0.0)
    for j in range(k - 1):
        dx = dx + _row(w, j) * _shift_rows(dy, -(k - 1 - j))
        dw = dw + jnp.where(tap == j, jnp.sum(dy * _shift_rows(x, k - 1 - j), axis=0, keepdims=True), 0.0)
    return dx, dw


dwconv.defvjp(_dwconv_fwd, _dwconv_bwd)


@jax.custom_vjp
def rope128(x, cos, sin):
    return x * cos + pltpu.roll(x, 64, 1) * sin


rope128.defvjp(lambda x, c, s: (rope128(x, c, s), (c, s)),
               lambda r, g: (g * r[0] + pltpu.roll(g * r[1], 64, 1), jnp.zeros_like(r[0]), jnp.zeros_like(r[1])))


def _silu(x):
    return x * jax.nn.sigmoid(x)


def _softplus(x):
    return jnp.maximum(x, 0.0) + jnp.log(1.0 + jnp.exp(-jnp.abs(x)))


def _rmsnorm(x, g):
    return x * lax.rsqrt(jnp.mean(x * x, axis=-1, keepdims=True) + NORM_EPS) * g


def _layernorm(x, g, b):
    mu = jnp.mean(x, axis=-1, keepdims=True)
    xc = x - mu
    var = jnp.mean(xc * xc, axis=-1, keepdims=True)
    return xc * lax.rsqrt(var + NORM_EPS) * g + b


def _pick_lane(row, lane):
    idx = lax.broadcasted_iota(jnp.int32, row.shape, 1)
    return jnp.sum(jnp.where(idx == lane, row, 0.0), axis=1, keepdims=True)


def _gdn_q(pq, cw):
    h = _silu(dwconv(pq, cw))
    return h * lax.rsqrt(jnp.sum(h * h, axis=-1, keepdims=True) + NORM_EPS) * (GDN_DK ** -0.5)


def _gdn_k(pk, cw):
    h = _silu(dwconv(pk, cw))
    return h * lax.rsqrt(jnp.sum(h * h, axis=-1, keepdims=True) + NORM_EPS)


def _gdn_v(pv, cw):
    return _silu(dwconv(pv, cw))


def _gdn_gate(ab, sc, head):
    a = _pick_lane(ab, head)
    b = _pick_lane(ab, GDN_HEADS + head)
    a_log = _pick_lane(_row(sc, 0), head)
    dt_bias = _pick_lane(_row(sc, 1), head)
    beta = jax.nn.sigmoid(b)
    g = -jnp.exp(a_log) * _softplus(a + dt_bias)
    return _two_lanes(g, beta)


def _two_lanes(c0, c1):
    lane = lax.broadcasted_iota(jnp.int32, (c0.shape[0], LANES), 1)
    return jnp.where(lane == 0, c0, jnp.where(lane == 1, c1, 0.0))


def _inverse_group(lows):
    n = lows[0].shape[0]
    ii = lax.broadcasted_iota(jnp.int32, (n, n), 0)
    jj = lax.broadcasted_iota(jnp.int32, (n, n), 1)
    eye = jnp.where(ii == jj, 1.0, 0.0)
    invs = [eye - low for low in lows]
    powers = [_dot3(low, low, 1, 0) for low in lows]
    k = 2
    while k < n:
        invs = [inv + _dot3(inv, p, 1, 0) for inv, p in zip(invs, powers)]
        k *= 2
        if k < n:
            powers = [_dot3(p, p, 1, 0) for p in powers]
    return invs


@jax.custom_vjp
def solve_group(lows, rhss):
    return [_dot3(inv, rhs, 1, 0) for inv, rhs in zip(_inverse_group(lows), rhss)]


def _solve_group_fwd(lows, rhss):
    invs = _inverse_group(lows)
    xs = [_dot3(inv, rhs, 1, 0) for inv, rhs in zip(invs, rhss)]
    return xs, (invs, xs)


def _solve_group_bwd(res, dxs):
    invs, xs = res
    n = invs[0].shape[0]
    strict = lax.broadcasted_iota(jnp.int32, (n, n), 0) > lax.broadcasted_iota(jnp.int32, (n, n), 1)
    drhss = [_dot3(inv, dx, 0, 0) for inv, dx in zip(invs, dxs)]
    dlows = [jnp.where(strict, -_dot3(drhs, x, 1, 1), 0.0) for drhs, x in zip(drhss, xs)]
    return dlows, drhss


solve_group.defvjp(_solve_group_fwd, _solve_group_bwd)


def _gdn_local_group(qs, ks, vs, gbs):
    c = qs[0].shape[0]
    ii = lax.broadcasted_iota(jnp.int32, (c, c), 0)
    jj = lax.broadcasted_iota(jnp.int32, (c, c), 1)
    incl = ii >= jj
    gs = [_pick_lane(gb, 0) for gb in gbs]
    betas = [_pick_lane(gb, 1) for gb in gbs]
    g_rows = [jnp.sum(jnp.where(ii == jj, g, 0.0), axis=0, keepdims=True) for g in gs]
    gc_cols = [jnp.sum(jnp.where(incl, g_row, 0.0), axis=1, keepdims=True) for g_row in g_rows]
    gc_rows = [jnp.sum(jnp.where(jj >= ii, g, 0.0), axis=0, keepdims=True) for g in gs]
    decays = [jnp.where(incl, jnp.exp(jnp.where(incl, gc - gr, 0.0)), 0.0) for gc, gr in zip(gc_cols, gc_rows)]
    kbs = [k * beta for k, beta in zip(ks, betas)]
    lows = [jnp.where(ii > jj, mm_nt(kb, k) * decay, 0.0) for kb, k, decay in zip(kbs, ks, decays)]
    egs = [jnp.exp(gc) for gc in gc_cols]
    wus = solve_group(lows, [jnp.concatenate([kb * eg, v * beta], axis=1) for kb, eg, v, beta in zip(kbs, egs, vs, betas)])
    qks = [mm_nt(q, k) * decay for q, k, decay in zip(qs, ks, decays)]
    g_lasts = [jnp.sum(g_row, axis=1, keepdims=True) for g_row in g_rows]
    kds = [k * jnp.exp(gl - gc) for k, gl, gc in zip(ks, g_lasts, gc_cols)]
    ws, us = [wu[:, :GDN_DK] for wu in wus], [wu[:, GDN_DK:] for wu in wus]
    q_effs = [q * eg - mm(qk, w) for q, eg, qk, w in zip(qs, egs, qks, ws)]
    o_locals = [mm(qk, u) for qk, u in zip(qks, us)]
    mixes = [mm_tn(kd, w) for kd, w in zip(kds, ws)]
    adds = [mm_tn(kd, u) for kd, u in zip(kds, us)]
    return [(q_eff, o_loc, mix, add, jnp.exp(gl))
            for q_eff, o_loc, mix, add, gl in zip(q_effs, o_locals, mixes, adds, g_lasts)]


def _gdn_state_step(q_eff, o_local, mix, add, eg_last, state):
    return mm(q_eff, state) + o_local, state * eg_last - mm(mix, state) + add


def _gdn_post(o, z, norm_g):
    return _rmsnorm(o, norm_g) * _silu(z)


MASKED = -1e30


def _scores(q, kn, kr, q0, k0):
    s = (mm_nt(q[:, :128], kn) + mm_nt(q[:, 128:], kr)) * ATT_SCALE
    if k0 + kn.shape[0] <= q0:
        return s
    qpos = q0 + lax.broadcasted_iota(jnp.int32, s.shape, 0)
    kpos = k0 + lax.broadcasted_iota(jnp.int32, s.shape, 1)
    shift = int(math.log2(CHUNK))
    return jnp.where((kpos >> shift) <= (qpos >> shift), s, MASKED)


def _softmax_times(s, v):
    top = jnp.max(s, axis=-1, keepdims=True)
    p = jnp.exp(s - top)
    norm = jnp.sum(p, axis=-1, keepdims=True)
    return mm(p / norm, v), top + jnp.log(norm)


def _mla_prep(pm, qg, kvg, wq, wkv, cos, sin):
    cq = pm[:, 128:512]
    ckv = pm[:, 512:768]
    qf = mm(_rmsnorm(cq, qg), wq)
    parts = []
    for h in range(MLA_HEADS):
        parts.append(qf[:, h * 256:h * 256 + 128])
        parts.append(rope128(qf[:, h * 256 + 128:h * 256 + 256], cos, sin))
    kvf = mm(_rmsnorm(ckv, kvg), wkv)
    return jnp.concatenate(parts, axis=1), kvf, rope128(pm[:, 768:896], cos, sin)


def _ffn_act(ug, uu, wg, wu, bg, bu):
    return _silu(dwconv(ug, wg) + bg) * (dwconv(uu, wu) + bu)


def _head_loss(h1, ffn, gpre, pp, bgate, g2, b2, target):
    gate = jax.nn.sigmoid(gpre + bgate)
    h2 = _layernorm(ALPHA * h1 + ffn + gate * pp, g2, b2)
    err = h2 - target
    return 0.5 * jnp.sum(jnp.sum(err * err, axis=1, keepdims=True), axis=0, keepdims=True) / D_MODEL


ROW_TILE_VREGS = 32
CONV_HALO = 8


def _rows_per_tile(n_rows, cols):
    tile = min(n_rows, ROW_TILE_VREGS * 8 * LANES // cols)
    assert n_rows % tile == 0 and tile % CONV_HALO == 0, (n_rows, cols)
    return tile


def _tile_inputs(loads, t0, first, halo, tile):
    if halo == 0:
        return [ld(pl.ds(t0, tile)) for ld in loads]
    if first:
        xs = [ld(pl.ds(0, tile)) for ld in loads]
        return [jnp.concatenate([jnp.zeros((halo, x.shape[1]), x.dtype), x], axis=0) for x in xs]
    return [ld(pl.ds(pl.multiple_of(t0 - halo, CONV_HALO), tile + halo)) for ld in loads]


def _rows_apply(fn, loads, consts, store, n_rows, cols, halo):
    tile = _rows_per_tile(n_rows, cols)

    def one(t0, first):
        y = fn(*_tile_inputs(loads, t0, first, halo, tile), *consts)
        store(pl.ds(t0, tile), y[halo:] if halo else y)

    one(0, True)

    def step(i, carry):
        one(pl.multiple_of(i * tile, tile), False)
        return carry

    lax.fori_loop(1, n_rows // tile, step, 0)


def _rows_vjp(fn, loads, consts, load_dy, stores, n_rows, cols, halo):
    tile = _rows_per_tile(n_rows, cols)

    def one(t0, first, dconsts):
        xs = _tile_inputs(loads, t0, first, halo, tile)
        _, vjp = jax.vjp(lambda *a: fn(*a)[halo:] if halo else fn(*a), *xs, *consts)
        grads = vjp(load_dy(pl.ds(t0, tile)))
        for st, dx in zip(stores, grads[:len(xs)]):
            st(pl.ds(t0, tile), dx[halo:] if halo else dx, False)
            if halo and not first:
                st(pl.ds(pl.multiple_of(t0 - halo, CONV_HALO), halo), dx[:halo], True)
        return tuple(a + b for a, b in zip(dconsts, grads[len(xs):]))

    dconsts = one(0, True, tuple(jnp.zeros_like(c) for c in consts))
    return lax.fori_loop(1, n_rows // tile, lambda i, dc: one(pl.multiple_of(i * tile, tile), False, dc), dconsts)


def _params(sem):
    return pltpu.CompilerParams(dimension_semantics=sem, vmem_limit_bytes=VMEM_LIMIT)


def _matmul(a, b, *, name, ta=False, tb=False, tm=512, tn=512, tk=1024, add=None, add_scale=1.0,
            a_halves=False, b_halves=False, ride=None, out_dtype=F32):
    assert not (a_halves and ta) and not (b_halves and tb)
    a_shape = (a.shape[1], 2 * a.shape[2]) if a_halves else a.shape
    b_shape = (b.shape[1], 2 * b.shape[2]) if b_halves else b.shape
    (k_dim, m) = a_shape if ta else a_shape[::-1]
    (n, k2) = b_shape if tb else b_shape[::-1]
    assert k_dim == k2, (a.shape, b.shape)
    tm, tn, tk = min(tm, m), min(tn, n), min(tk, k_dim)
    assert m % tm == 0 and n % tn == 0 and k_dim % tk == 0, (name, m, n, k_dim, tm, tn, tk)
    nk = k_dim // tk
    ca, cb = (0 if ta else 1), (1 if tb else 0)

    def body(*refs):
        if add is None:
            a_ref, b_ref, o_ref, acc = refs
        else:
            a_ref, b_ref, c_ref, o_ref, acc = refs
        kk = pl.program_id(2)

        @pl.when(kk == 0)
        def _():
            acc[...] = jnp.zeros_like(acc)

        acc[...] += _dot(a_ref[...], b_ref[...], ca, cb)

        @pl.when(kk == nk - 1)
        def _():
            r = acc[...]
            if add is not None:
                r = r + add_scale * c_ref[...]
            o_ref[...] = r.astype(out_dtype)

    spec = pl.BlockSpec
    a_spec = spec((tk, tm), lambda i, j, k: (k, i)) if ta else spec((tm, tk), lambda i, j, k: (i, k))
    b_spec = spec((tn, tk), lambda i, j, k: (j, k)) if tb else spec((tk, tn), lambda i, j, k: (k, j))
    if a_halves:
        kh = k_dim // 2 // tk
        assert kh * tk * 2 == k_dim
        a_spec = spec((None, tm, tk), lambda i, j, k: (k // kh, i, k % kh))
    if b_halves:
        nh = n // 2 // tn
        assert nh * tn * 2 == n
        b_spec = spec((None, tk, tn), lambda i, j, k: (j // nh, k, j % nh))
    in_specs = [a_spec, b_spec]
    args = [a, b]
    if add is not None:
        in_specs.append(pl.BlockSpec((tm, tn), lambda i, j, k: (i, j)))
        args.append(add)
    grid = (m // tm, n // tn, nk)
    r_in, r_out, r_shapes, r_sems, r_args = _ride_specs(ride)
    outs = pl.pallas_call(
        _riding(body, len(args), 1, 1, ride, grid), name=name, grid=grid,
        in_specs=in_specs + r_in, out_specs=[pl.BlockSpec((tm, tn), lambda i, j, k: (i, j))] + r_out,
        out_shape=[jax.ShapeDtypeStruct((m, n), out_dtype)] + r_shapes,
        scratch_shapes=[pltpu.VMEM((tm, tn), F32)] + r_sems,
        compiler_params=_params(("parallel", "parallel", "arbitrary") if ride is None else ("arbitrary",) * 3),
    )(*args, *r_args)
    return outs[0] if ride is None else (outs[0], list(outs[1:]))


def _riding(core, n_in, n_out, n_scratch, ride, grid):
    if ride is None:
        return core
    copies, nr = ride[1], len(ride[0])
    steps = int(np.prod(grid))
    pass_step = min(max(int(steps * ride[4]), 1), steps - 2)
    assert steps >= 3, grid

    def body(*refs):
        cuts = np.cumsum([0, n_in, nr, n_out, nr, n_scratch])
        ins, rin, outs, rout, scratch = (refs[a:b] for a, b in zip(cuts[:-1], cuts[1:]))
        sems = refs[cuts[-1]:]
        step = 0
        for axis, size in enumerate(grid):
            step = step * size + pl.program_id(axis)

        @pl.when(step == 0)
        def _():
            _exchange_start(copies(rin, rout, sems))

        @pl.when(step == pass_step)
        def _():
            _exchange_pass_on(copies(rin, rout, sems))

        core(*ins, *outs, *scratch)

        @pl.when(step == steps - 1)
        def _():
            _exchange_finish(copies(rin, rout, sems))

    return body


def _ride_specs(ride):
    if ride is None:
        return [], [], [], [], []
    arrays, _, sems, shapes, _ = ride
    return [HBM_REF] * len(arrays), [HBM_REF] * len(arrays), list(shapes), sems(len(arrays)), list(arrays)


def _gdn_fwd(proj, conv_w, sc, norm_g, bl, s, ride=None):
    nc = s // CHUNK

    def core(ph_ref, ab_ref, cwq_ref, cwk_ref, cwv_ref, sc_ref, ng_ref, cat_ref, o_ref, st_ref, q_s, k_s, v_s, gb_s):
        def into(ref):
            def store(rows, value):
                ref[rows, :] = value.astype(ref.dtype)
            return store

        for fn, col, cw_ref, val_s in [(_gdn_q, 0, cwq_ref, q_s), (_gdn_k, 128, cwk_ref, k_s), (_gdn_v, 256, cwv_ref, v_s)]:
            _rows_apply(fn, [lambda r, col=col: ph_ref[r, col:col + 128]], [cw_ref[...]], into(val_s), s, LANES, CONV_HALO)
        _rows_apply(functools.partial(_gdn_gate, head=pl.program_id(1)), [lambda r: ab_ref[r, :]], [sc_ref[...]], into(gb_s), s, LANES, 0)

        group = math.gcd(nc, GDN_FWD_GROUP)

        def chunks(i, state):
            ns = [i * group + j for j in range(group)]
            rows = [pl.ds(pl.multiple_of(n * CHUNK, CHUNK), CHUNK) for n in ns]
            local = _gdn_local_group([q_s[r, :] for r in rows], [k_s[r, :] for r in rows], [v_s[r, :] for r in rows],
                                     [gb_s[r, :] for r in rows])
            for n, r, loc in zip(ns, rows, local):
                st_ref[n] = state
                o_ref[r, :], state = _gdn_state_step(*loc, state)
            return state

        lax.fori_loop(0, nc // group, chunks, jnp.zeros((GDN_DK, GDN_DV), F32))
        _rows_apply(_gdn_post, [lambda r: o_ref[r, :], lambda r: ph_ref[r, 384:512]], [ng_ref[...]], into(cat_ref), s, LANES, 0)

    t = bl * s
    r_in, r_out, r_shapes, r_sems, r_args = _ride_specs(ride)
    outs = pl.pallas_call(
        _riding(core, 7, 7, 0, ride, (bl, GDN_HEADS)), name="gdn_fwd", grid=(bl, GDN_HEADS),
        in_specs=[
            pl.BlockSpec((s, 512), lambda b, h: (b, h)),
            pl.BlockSpec((s, 128), lambda b, h: (b, P_MLA // 128)),
            pl.BlockSpec((GDN_CONV, 128), lambda b, h: (0, h)),
            pl.BlockSpec((GDN_CONV, 128), lambda b, h: (0, GDN_HEADS + h)),
            pl.BlockSpec((GDN_CONV, 128), lambda b, h: (0, 2 * GDN_HEADS + h)),
            pl.BlockSpec((8, 128), lambda b, h: (0, 0)),
            pl.BlockSpec((1, 128), lambda b, h: (0, 0)),
        ] + r_in,
        out_specs=[
            pl.BlockSpec((s, 128), lambda b, h: (b, h)),
            pl.BlockSpec((s, 128), lambda b, h: (b, h)),
            pl.BlockSpec((None, None, nc, GDN_DK, GDN_DV), lambda b, h: (b, h, 0, 0, 0)),
        ] + [pl.BlockSpec((s, 128), lambda b, h: (b, h))] * 4 + r_out,
        out_shape=[
            jax.ShapeDtypeStruct((t, 2 * GDN_VW), BF16),
            jax.ShapeDtypeStruct((t, GDN_VW), F32),
            jax.ShapeDtypeStruct((bl, GDN_HEADS, nc, GDN_DK, GDN_DV), F32),
        ] + [jax.ShapeDtypeStruct((t, GDN_VW), F32)] * 4 + r_shapes,
        scratch_shapes=r_sems,
        compiler_params=_params(("arbitrary", "arbitrary")),
    )(proj, proj, conv_w, conv_w, conv_w, sc, norm_g, *r_args)
    return outs[0], outs[1], outs[2], tuple(outs[3:7]), list(outs[7:])


def _gdn_bwd(proj, conv_w, sc, norm_g, o_raw, states, qkvg, dcat, bl, s, ride=None):
    nc = s // CHUNK

    def core(ph_ref, ab_ref, cwq_ref, cwk_ref, cwv_ref, sc_ref, ng_ref, o_ref, st_ref, dc_ref, q_in, k_in, v_in, gb_in,
             dph_ref, dab_ref, dcwq_ref, dcwk_ref, dcwv_ref, dsc_ref, dng_ref, q_s, k_s, v_s, gb_s, do_s):
        head = pl.program_id(1)
        gate = functools.partial(_gdn_gate, head=head)
        paths = [(_gdn_q, 0, cwq_ref, q_s, dcwq_ref), (_gdn_k, 128, cwk_ref, k_s, dcwk_ref), (_gdn_v, 256, cwv_ref, v_s, dcwv_ref)]
        def into(ref, cols=slice(None)):
            def store(rows, value, add=False):
                if add:
                    ref[rows, cols] += value.astype(ref.dtype)
                else:
                    ref[rows, cols] = value.astype(ref.dtype)
            return store

        (dng,) = _rows_vjp(_gdn_post, [lambda r: o_ref[r, :], lambda r: ph_ref[r, 384:512]], [ng_ref[...]],
                           lambda r: dc_ref[r, :], [into(do_s), into(dph_ref, slice(384, 512))], s, LANES, 0)
        dng_ref[...] = jnp.broadcast_to(dng, dng_ref.shape)

        group = math.gcd(nc, GDN_BWD_GROUP)

        def chunks(i, dstate):
            ns = [nc - 1 - (i * group + j) for j in range(group)]
            rows = [pl.ds(pl.multiple_of(n * CHUNK, CHUNK), CHUNK) for n in ns]
            local, local_vjp = jax.vjp(_gdn_local_group, [q_in[r, :] for r in rows], [k_in[r, :] for r in rows],
                                       [v_in[r, :] for r in rows], [gb_in[r, :] for r in rows])
            d_os = [do_s[r, :] for r in rows]
            dlocal = []
            for n, loc, d_o in zip(ns, local, d_os):
                _, step_vjp = jax.vjp(_gdn_state_step, *loc, st_ref[n])
                *dloc, dstate = step_vjp((d_o, dstate))
                dlocal.append(tuple(dloc))
            dqs, dks, dvs, dgbs = local_vjp(dlocal)
            for r, dq, dk, dv, dgb in zip(rows, dqs, dks, dvs, dgbs):
                q_s[r, :], k_s[r, :], v_s[r, :], gb_s[r, :] = dq, dk, dv, dgb
            return dstate

        lax.fori_loop(0, nc // group, chunks, jnp.zeros((GDN_DK, GDN_DV), F32))
        for fn, col, cw_ref, val_s, dcw_ref in paths:
            (dcw_ref[...],) = _rows_vjp(fn, [lambda r, col=col: ph_ref[r, col:col + 128]], [cw_ref[...]],
                                        lambda r, val_s=val_s: val_s[r, :], [into(val_s)], s, LANES, CONV_HALO)
            dph_ref[:, col:col + 128] = val_s[...].astype(BF16)

        @pl.when(head == 0)
        def _():
            dab_ref[...] = jnp.zeros_like(dab_ref)

        def add_dab(rows, value, add=False):
            dab_ref[rows, :] += value

        (dsc_ref[...],) = _rows_vjp(gate, [lambda r: ab_ref[r, :]], [sc_ref[...]], lambda r: gb_s[r, :], [add_dab], s, LANES, 0)

    t = bl * s
    cw_out = pl.BlockSpec((None, GDN_CONV, 128), lambda b, h: (b, 0, h))
    part = pl.BlockSpec((None, None, 8, 128), lambda b, h: (b, h, 0, 0))
    r_in, r_out, r_shapes, r_sems, r_args = _ride_specs(ride)
    outs = pl.pallas_call(
        _riding(core, 14, 7, 5, ride, (bl, GDN_HEADS)), name="gdn_bwd", grid=(bl, GDN_HEADS),
        in_specs=[
            pl.BlockSpec((s, 512), lambda b, h: (b, h)),
            pl.BlockSpec((s, 128), lambda b, h: (b, P_MLA // 128)),
            pl.BlockSpec((GDN_CONV, 128), lambda b, h: (0, h)),
            pl.BlockSpec((GDN_CONV, 128), lambda b, h: (0, GDN_HEADS + h)),
            pl.BlockSpec((GDN_CONV, 128), lambda b, h: (0, 2 * GDN_HEADS + h)),
            pl.BlockSpec((8, 128), lambda b, h: (0, 0)),
            pl.BlockSpec((1, 128), lambda b, h: (0, 0)),
            pl.BlockSpec((s, 128), lambda b, h: (b, h)),
            pl.BlockSpec((None, None, nc, GDN_DK, GDN_DV), lambda b, h: (b, h, 0, 0, 0)),
        ] + [pl.BlockSpec((s, 128), lambda b, h: (b, h))] * 5 + r_in,
        out_specs=[
            pl.BlockSpec((s, 512), lambda b, h: (b, h)),
            pl.BlockSpec((s, 128), lambda b, h: (b, 0)),
            cw_out, cw_out, cw_out, part, part,
        ] + r_out,
        out_shape=[
            jax.ShapeDtypeStruct((t, P_WIDTH), BF16),
            jax.ShapeDtypeStruct((t, 128), F32),
            jax.ShapeDtypeStruct((bl, GDN_CONV, 512), F32),
            jax.ShapeDtypeStruct((bl, GDN_CONV, 512), F32),
            jax.ShapeDtypeStruct((bl, GDN_CONV, 512), F32),
            jax.ShapeDtypeStruct((bl, GDN_HEADS, 8, 128), F32),
            jax.ShapeDtypeStruct((bl, GDN_HEADS, 8, 128), F32),
        ] + r_shapes,
        scratch_shapes=[pltpu.VMEM((s, 128), F32)] * 5 + r_sems,
        compiler_params=_params(("arbitrary", "arbitrary")),
    )(proj, proj, conv_w, conv_w, conv_w, sc, norm_g, o_raw, states, dcat, *qkvg, *r_args)
    return tuple(outs[:7]) + (list(outs[7:]),)


def _mla_prep_fwd(proj, qg, kvg, wq, wkv, cos, sin, s, tm):
    t = proj.shape[0]
    tm = min(tm, s)
    nps = s // tm
    const = lambda shape: pl.BlockSpec(shape, lambda i: (0, 0))

    def body(pm_ref, qg_ref, kvg_ref, wq_ref, wkv_ref, cos_ref, sin_ref, qf_ref, kvf_ref, kr_ref):
        qf, kvf, kr = _mla_prep(pm_ref[...], qg_ref[...], kvg_ref[...], wq_ref[...], wkv_ref[...], cos_ref[...], sin_ref[...])
        qf_ref[...], kvf_ref[...], kr_ref[...] = qf.astype(BF16), kvf.astype(BF16), kr.astype(BF16)

    return pl.pallas_call(
        body, name="mla_prep_fwd", grid=(t // tm,),
        in_specs=[
            pl.BlockSpec((tm, 1024), lambda i: (i, P_MLA // 1024)),
            const((1, MLA_Q_LORA)), const((1, MLA_KV_LORA)), const(wq.shape), const(wkv.shape),
            pl.BlockSpec((tm, 128), lambda i: (i % nps, 0)), pl.BlockSpec((tm, 128), lambda i: (i % nps, 0)),
        ],
        out_specs=[pl.BlockSpec((tm, 1024), lambda i: (i, 0)), pl.BlockSpec((tm, 1024), lambda i: (i, 0)),
                   pl.BlockSpec((tm, 128), lambda i: (i, 0))],
        out_shape=[jax.ShapeDtypeStruct((t, 1024), BF16), jax.ShapeDtypeStruct((t, 1024), BF16),
                   jax.ShapeDtypeStruct((t, 128), BF16)],
        compiler_params=_params(("parallel",)),
    )(proj, qg, kvg, wq, wkv, cos, sin)


def _mla_prep_bwd(proj, qg, kvg, wq, wkv, cos, sin, dqf, dkvf, dkr, dab, dproj, s, tm):
    t = proj.shape[0]
    tm = min(tm, s)
    nps = s // tm
    const = lambda shape: pl.BlockSpec(shape, lambda i: (0, 0))

    def body(pm_ref, qg_ref, kvg_ref, wq_ref, wkv_ref, cos_ref, sin_ref, dqf_ref, dkvf_ref, dkr_ref, dab_ref, dp_in,
             dp_ref, dqg_ref, dkvg_ref, dwq_ref, dwkv_ref):
        del dp_in
        fn = lambda pm, qg_, kvg_, wq_, wkv_: _mla_prep(pm, qg_, kvg_, wq_, wkv_, cos_ref[...], sin_ref[...])
        _, vjp = jax.vjp(fn, pm_ref[...], qg_ref[...], kvg_ref[...], wq_ref[...].astype(F32), wkv_ref[...].astype(F32))
        dpm, dqg, dkvg, dwq, dwkv = vjp((dqf_ref[...], dkvf_ref[...], dkr_ref[...]))
        dp_ref[...] = jnp.concatenate([dab_ref[...], dpm[:, 128:]], axis=1).astype(BF16)

        @pl.when(pl.program_id(0) == 0)
        def _():
            dqg_ref[...] = jnp.zeros_like(dqg_ref)
            dkvg_ref[...] = jnp.zeros_like(dkvg_ref)
            dwq_ref[...] = jnp.zeros_like(dwq_ref)
            dwkv_ref[...] = jnp.zeros_like(dwkv_ref)

        dqg_ref[...] += dqg
        dkvg_ref[...] += dkvg
        dwq_ref[...] += dwq
        dwkv_ref[...] += dwkv

    rows = lambda w: pl.BlockSpec((tm, w), lambda i: (i, 0))
    return pl.pallas_call(
        body, name="mla_prep_bwd", grid=(t // tm,),
        in_specs=[
            pl.BlockSpec((tm, 1024), lambda i: (i, P_MLA // 1024)),
            const((1, MLA_Q_LORA)), const((1, MLA_KV_LORA)), const(wq.shape), const(wkv.shape),
            pl.BlockSpec((tm, 128), lambda i: (i % nps, 0)), pl.BlockSpec((tm, 128), lambda i: (i % nps, 0)),
            rows(1024), rows(1024), rows(128), rows(128),
            pl.BlockSpec(memory_space=pl.ANY),
        ],
        out_specs=[pl.BlockSpec((tm, 1024), lambda i: (i, P_MLA // 1024)),
                   const((1, MLA_Q_LORA)), const((1, MLA_KV_LORA)), const(wq.shape), const(wkv.shape)],
        out_shape=[jax.ShapeDtypeStruct(dproj.shape, dproj.dtype),
                   jax.ShapeDtypeStruct((1, MLA_Q_LORA), F32), jax.ShapeDtypeStruct((1, MLA_KV_LORA), F32),
                   jax.ShapeDtypeStruct(wq.shape, F32), jax.ShapeDtypeStruct(wkv.shape, F32)],
        input_output_aliases={11: 0},
        compiler_params=_params(("arbitrary",)),
    )(proj, qg, kvg, wq, wkv, cos, sin, dqf, dkvf, dkr, dab, dproj)


def _attn_fwd(qf, kvf, kr, cat, bl, s, tq):
    tq = min(tq, s)
    nq = s // tq

    def body(q_ref, kv_ref, kr_ref, cat_in, o_ref, o32_ref, lse_ref):
        del cat_in

        def scores_of(i):
            keys = slice(0, (i + 1) * tq)
            return _scores(q_ref[i * tq:(i + 1) * tq, :], kv_ref[keys, 0:128], kr_ref[keys, :], i * tq, 0)

        ready = scores_of(0)
        for i in range(nq):
            scores = ready
            if i + 1 < nq:
                ready = scores_of(i + 1)
            rows = slice(i * tq, (i + 1) * tq)
            o, lse = _softmax_times(scores, kv_ref[0:(i + 1) * tq, 128:256])
            o_ref[rows, :] = o.astype(o_ref.dtype)
            o32_ref[rows, :] = o
            lse_ref[rows, :] = jnp.broadcast_to(lse, o.shape)

    t = bl * s
    head_cols = pl.BlockSpec((s, 128), lambda b, h: (b, h))
    return pl.pallas_call(
        body, name="attn_fwd", grid=(bl, MLA_HEADS),
        in_specs=[
            pl.BlockSpec((s, 256), lambda b, h: (b, h)),
            pl.BlockSpec((s, 256), lambda b, h: (b, h)),
            pl.BlockSpec((s, 128), lambda b, h: (b, 0)),
            pl.BlockSpec(memory_space=pl.ANY),
        ],
        out_specs=[pl.BlockSpec((s, 128), lambda b, h: (b, GDN_HEADS + h)), head_cols, head_cols],
        out_shape=[jax.ShapeDtypeStruct(cat.shape, cat.dtype)] + [jax.ShapeDtypeStruct((t, MLA_HEADS * MLA_V), F32)] * 2,
        input_output_aliases={3: 0},
        compiler_params=_params(("parallel", "parallel")),
    )(qf, kvf, kr, cat)


def _attn_bwd(qf, kvf, kr, dcat, o32, lse, bl, s, tq):
    tq = min(tq, s)
    nq = s // tq

    def body(q_ref, kv_ref, kr_ref, do_ref, o_ref, lse_ref, dq_ref, dkv_ref, dkr_ref):
        dkv_ref[...] = jnp.zeros_like(dkv_ref)

        @pl.when(pl.program_id(1) == 0)
        def _():
            dkr_ref[...] = jnp.zeros_like(dkr_ref)

        def block(i):
            rows = slice(i * tq, (i + 1) * tq)
            return q_ref[rows, :], do_ref[rows, :]

        def first_products(i, j):
            (q, d_o), keys = block(i), slice(j * tq, (j + 1) * tq)
            return _scores(q, kv_ref[keys, 0:128], kr_ref[keys, :], i * tq, j * tq), mm_nt(d_o, kv_ref[keys, 128:256])

        tiles = [(i, j) for i in range(nq) for j in range(i + 1)]
        ready = first_products(*tiles[0])
        for t, (i, j) in enumerate(tiles):
            scores, dp = ready
            if t + 1 < len(tiles):
                ready = first_products(*tiles[t + 1])
            rows, keys = slice(i * tq, (i + 1) * tq), slice(j * tq, (j + 1) * tq)
            q, d_o = block(i)
            if j == 0:
                delta = jnp.sum(d_o * o_ref[rows, :], axis=-1, keepdims=True)
                lse_i = jnp.max(lse_ref[rows, :], axis=-1, keepdims=True)
                dqn, dqr = jnp.zeros((tq, MLA_NOPE), F32), jnp.zeros((tq, 128), F32)
            p = jnp.exp(scores - lse_i)
            ds = p * (dp - delta) * ATT_SCALE
            dkv_ref[keys, 128:256] += mm_tn(p, d_o)
            dkv_ref[keys, 0:128] += mm_tn(ds, q[:, :128])
            dkr_ref[keys, :] += mm_tn(ds, q[:, 128:])
            dqn = dqn + mm(ds, kv_ref[keys, 0:128])
            dqr = dqr + mm(ds, kr_ref[keys, :])
            if j == i:
                dq_ref[rows, 0:128], dq_ref[rows, 128:256] = dqn, dqr

    t = bl * s
    head_cols = pl.BlockSpec((s, 128), lambda b, h: (b, h))
    return pl.pallas_call(
        body, name="attn_bwd", grid=(bl, MLA_HEADS),
        in_specs=[
            pl.BlockSpec((s, 256), lambda b, h: (b, h)),
            pl.BlockSpec((s, 256), lambda b, h: (b, h)),
            pl.BlockSpec((s, 128), lambda b, h: (b, 0)),
            pl.BlockSpec((s, 128), lambda b, h: (b, GDN_HEADS + h)),
            head_cols, head_cols,
        ],
        out_specs=[
            pl.BlockSpec((s, 256), lambda b, h: (b, h)),
            pl.BlockSpec((s, 256), lambda b, h: (b, h)),
            pl.BlockSpec((s, 128), lambda b, h: (b, 0)),
        ],
        out_shape=[jax.ShapeDtypeStruct((t, 1024), F32), jax.ShapeDtypeStruct((t, 1024), F32),
                   jax.ShapeDtypeStruct((t, 128), F32)],
        compiler_params=_params(("parallel", "arbitrary")),
    )(qf, kvf, kr, dcat, o32, lse)


def _ln1_fwd(x, mix, g, b, tm):
    t = x.shape[0]
    tm = min(tm, t)

    def body(x_ref, mix_ref, g_ref, b_ref, r_ref, h_ref, hb_ref, xb_ref):
        r = ALPHA * x_ref[...] + mix_ref[...]
        r_ref[...] = r
        h = _layernorm(r, g_ref[...], b_ref[...])
        h_ref[...] = h
        hb_ref[...] = h.astype(BF16)
        xb_ref[...] = x_ref[...].astype(BF16)

    rows = pl.BlockSpec((tm, D_MODEL), lambda i: (i, 0))
    vec = pl.BlockSpec((1, D_MODEL), lambda i: (0, 0))
    return pl.pallas_call(
        body, name="ln1_fwd", grid=(t // tm,), in_specs=[rows, rows, vec, vec], out_specs=[rows] * 4,
        out_shape=[jax.ShapeDtypeStruct(x.shape, F32)] * 2 + [jax.ShapeDtypeStruct(x.shape, BF16)] * 2,
        compiler_params=_params(("parallel",)),
    )(x, mix, g, b)


def _ln1_bwd(r1, dr2, da, db_, g, b, tm):
    t = r1.shape[0]
    tm = min(tm, t)

    def body(r_ref, d2_ref, da_ref, db_ref, g_ref, b_ref, dr_ref, drb_ref, dg_ref, dbias_ref):
        dh = ALPHA * d2_ref[...] + da_ref[...] + db_ref[...]
        _, vjp = jax.vjp(_layernorm, r_ref[...], g_ref[...], b_ref[...])
        dr, dg, dbias = vjp(dh)
        dr_ref[...] = dr
        drb_ref[...] = dr.astype(BF16)

        @pl.when(pl.program_id(0) == 0)
        def _():
            dg_ref[...] = jnp.zeros_like(dg_ref)
            dbias_ref[...] = jnp.zeros_like(dbias_ref)

        dg_ref[...] += dg
        dbias_ref[...] += dbias

    rows = pl.BlockSpec((tm, D_MODEL), lambda i: (i, 0))
    vec = pl.BlockSpec((1, D_MODEL), lambda i: (0, 0))
    return pl.pallas_call(
        body, name="ln1_bwd", grid=(t // tm,), in_specs=[rows] * 4 + [vec, vec], out_specs=[rows, rows, vec, vec],
        out_shape=[jax.ShapeDtypeStruct(r1.shape, F32), jax.ShapeDtypeStruct(r1.shape, BF16)]
        + [jax.ShapeDtypeStruct((1, D_MODEL), F32)] * 2,
        compiler_params=_params(("arbitrary",)),
    )(r1, dr2, da, db_, g, b)


def _ffn_act_fwd(u, conv_w, conv_b, bl, s, cb):
    nj = D_FF // cb

    def body(ug_ref, uu_ref, wg_ref, wu_ref, bg_ref, bu_ref, act_ref):
        def store(rows, act):
            act_ref[rows, :] = act.astype(BF16)

        _rows_apply(_ffn_act, [lambda r: ug_ref[r, :], lambda r: uu_ref[r, :]],
                    [wg_ref[...], wu_ref[...], bg_ref[...], bu_ref[...]], store, s, cb, CONV_HALO)

    return pl.pallas_call(
        body, name="ffn_act_fwd", grid=(bl, nj),
        in_specs=[
            pl.BlockSpec((s, cb), lambda b, j: (b, j)), pl.BlockSpec((s, cb), lambda b, j: (b, nj + j)),
            pl.BlockSpec((FFN_CONV, cb), lambda b, j: (0, j)), pl.BlockSpec((FFN_CONV, cb), lambda b, j: (0, nj + j)),
            pl.BlockSpec((1, cb), lambda b, j: (0, j)), pl.BlockSpec((1, cb), lambda b, j: (0, nj + j)),
        ],
        out_specs=pl.BlockSpec((s, cb), lambda b, j: (b, j)),
        out_shape=jax.ShapeDtypeStruct((bl * s, D_FF), BF16),
        compiler_params=_params(("parallel", "parallel")),
    )(u, u, conv_w, conv_w, conv_b, conv_b)


def _ffn_act_bwd(u, conv_w, conv_b, dact, bl, s, cb):
    nj = D_FF // cb

    def body(ug_ref, uu_ref, wg_ref, wu_ref, bg_ref, bu_ref, da_ref, du_ref, dwg_ref, dwu_ref, dbg_ref, dbu_ref, acc):
        def store_into(half):
            def store(rows, value, add):
                if add:
                    acc[half, rows, :] += value
                else:
                    acc[half, rows, :] = value
            return store

        dwg_ref[...], dwu_ref[...], dbg_ref[...], dbu_ref[...] = _rows_vjp(
            _ffn_act, [lambda r: ug_ref[r, :], lambda r: uu_ref[r, :]], [wg_ref[...], wu_ref[...], bg_ref[...], bu_ref[...]],
            lambda r: da_ref[r, :], [store_into(0), store_into(1)], s, cb, CONV_HALO)
        du_ref[...] = acc[...].astype(BF16)

    t = bl * s
    blk = pl.BlockSpec((s, cb), lambda b, j: (b, j))
    wpart = pl.BlockSpec((None, FFN_CONV, cb), lambda b, j: (b, 0, j))
    bpart = pl.BlockSpec((None, 1, cb), lambda b, j: (b, 0, j))
    return pl.pallas_call(
        body, name="ffn_act_bwd", grid=(bl, nj),
        in_specs=[
            blk, pl.BlockSpec((s, cb), lambda b, j: (b, nj + j)),
            pl.BlockSpec((FFN_CONV, cb), lambda b, j: (0, j)), pl.BlockSpec((FFN_CONV, cb), lambda b, j: (0, nj + j)),
            pl.BlockSpec((1, cb), lambda b, j: (0, j)), pl.BlockSpec((1, cb), lambda b, j: (0, nj + j)),
            blk,
        ],
        out_specs=[pl.BlockSpec((2, s, cb), lambda b, j: (0, b, j)), wpart, wpart, bpart, bpart],
        out_shape=[jax.ShapeDtypeStruct((2, t, D_FF), BF16)] + [jax.ShapeDtypeStruct((bl, FFN_CONV, D_FF), F32)] * 2
        + [jax.ShapeDtypeStruct((bl, 1, D_FF), F32)] * 2,
        scratch_shapes=[pltpu.VMEM((2, s, cb), F32)],
        compiler_params=_params(("parallel", "parallel")),
    )(u, u, conv_w, conv_w, conv_b, conv_b, dact)


def _head(h1, ffn, gpre, pp, bgate, g2, b2, target, tm):
    t = h1.shape[0]
    tm = min(tm, t)

    def body(h1_ref, ffn_ref, gp_ref, pp_ref, bg_ref, g2_ref, b2_ref, tg_ref,
             dr_ref, drb_ref, dgp_ref, dpp_ref, loss_ref, dbg_ref, dg2_ref, db2_ref):
        fn = functools.partial(_head_loss, target=tg_ref[...])
        loss, vjp = jax.vjp(fn, h1_ref[...], ffn_ref[...], gp_ref[...], pp_ref[...], bg_ref[...], g2_ref[...], b2_ref[...])
        _, dffn, dgp, dpp, dbg, dg2, db2 = vjp(jnp.ones((1, 1), F32))
        dr_ref[...] = dffn
        drb_ref[...], dgp_ref[...], dpp_ref[...] = dffn.astype(BF16), dgp.astype(BF16), dpp.astype(BF16)

        @pl.when(pl.program_id(0) == 0)
        def _():
            loss_ref[...] = jnp.zeros_like(loss_ref)
            dbg_ref[...] = jnp.zeros_like(dbg_ref)
            dg2_ref[...] = jnp.zeros_like(dg2_ref)
            db2_ref[...] = jnp.zeros_like(db2_ref)

        loss_ref[...] += jnp.broadcast_to(loss, loss_ref.shape)
        dbg_ref[...] += dbg
        dg2_ref[...] += dg2
        db2_ref[...] += db2

    rows = pl.BlockSpec((tm, D_MODEL), lambda i: (i, 0))
    vec = pl.BlockSpec((1, D_MODEL), lambda i: (0, 0))
    return pl.pallas_call(
        body, name="head", grid=(t // tm,), in_specs=[rows] * 4 + [vec] * 3 + [rows],
        out_specs=[rows] * 4 + [pl.BlockSpec((8, 128), lambda i: (0, 0))] + [vec] * 3,
        out_shape=[jax.ShapeDtypeStruct(h1.shape, F32)] + [jax.ShapeDtypeStruct(h1.shape, BF16)] * 3
        + [jax.ShapeDtypeStruct((8, 128), F32)]
        + [jax.ShapeDtypeStruct((1, D_MODEL), F32)] * 3,
        compiler_params=_params(("arbitrary",)),
    )(h1, ffn, gpre, pp, bgate, g2, b2, target)


def _adam_update(g, w_ref, m_ref, v_ref, g_ref, d_ref, nm_ref, nv_ref):
    m2 = ADAM_B1 * m_ref[...] + (1.0 - ADAM_B1) * g
    v2 = ADAM_B2 * v_ref[...] + (1.0 - ADAM_B2) * jnp.square(g)
    m_hat = m2 / (1.0 - ADAM_B1 ** ADAM_STEP)
    v_hat = v2 / (1.0 - ADAM_B2 ** ADAM_STEP)
    g_ref[...] = g
    d_ref[...] = -ADAM_LR * (m_hat / (jnp.sqrt(v_hat) + ADAM_EPS) + ADAM_WD * w_ref[...])
    nm_ref[...] = m2
    nv_ref[...] = v2


def _row_tile(rows, cols, limit_bytes=512 * 1024):
    best = None
    for t in range(HALF_ROWS_QUANTUM, rows + 1, HALF_ROWS_QUANTUM):
        if rows % t == 0 and t * cols * 4 <= limit_bytes:
            best = t
    return best or rows


def _adamw_reduced(recv, w, m, v, name):
    a, b = w.shape
    ta = _row_tile(a, b)

    def body(recv_ref, w_ref, m_ref, v_ref, g_ref, d_ref, nm_ref, nv_ref):
        c = lax.axis_index("c")
        for core in range(2):
            @pl.when(c == core)
            def _():
                got = [recv_ref[k].astype(F32) for k in range(N_DEV)]
                same = [got[7], got[0], got[1], got[2]]
                other = got[3:7]
                core0, core1 = (same, other) if core == 0 else (other, same)
                g = core0[0] + core1[0]
                for r in range(1, N_CHIPS):
                    g = (g + core0[r]) + core1[r]
                _adam_update(g, w_ref, m_ref, v_ref, g_ref, d_ref, nm_ref, nv_ref)

    blk = pl.BlockSpec((ta, b), lambda i: (i, 0))
    return pl.pallas_call(
        body, name=name, grid=(a // ta,),
        in_specs=[pl.BlockSpec((N_DEV, ta, b), lambda i: (0, i, 0)), blk, blk, blk], out_specs=[blk] * 4,
        out_shape=[jax.ShapeDtypeStruct(w.shape, F32)] * 4, compiler_params=_params(("parallel",)),
    )(recv, w, m, v)


def _adamw_small(g, w, m, v):
    def body(g_in, w_ref, m_ref, v_ref, g_ref, d_ref, nm_ref, nv_ref):
        _adam_update(g_in[...], w_ref, m_ref, v_ref, g_ref, d_ref, nm_ref, nv_ref)

    blk = pl.BlockSpec(w.shape, lambda i: (0, 0))
    return pl.pallas_call(
        body, name="adamw_small", grid=(1,), in_specs=[blk] * 4, out_specs=[blk] * 4,
        out_shape=[jax.ShapeDtypeStruct(w.shape, F32)] * 4, compiler_params=_params(("arbitrary",)),
    )(g, w, m, v)


def _remote(src, dst, send_sem, recv_sem, device):
    return pltpu.make_async_remote_copy(src_ref=src, dst_ref=dst, send_sem=send_sem, recv_sem=recv_sem,
                                        device_id=device, device_id_type=MESH)


def _place():
    x, y, c = lax.axis_index("x"), lax.axis_index("y"), lax.axis_index("c")
    return x, y, c, 2 * x + y, [(1 - x, y), (x, 1 - y), (1 - x, 1 - y)]


HBM_REF = pl.BlockSpec(memory_space=pl.ANY)
HALF_ROWS_QUANTUM = 16


def _gather_sems(n):
    return [pltpu.SemaphoreType.DMA((3 * n,))] * 4 + [pltpu.SemaphoreType.DMA((n,))]


def _gather_copies(ins, outs, sems):
    send_s, recv_s, fsend_s, frecv_s, local_s = sems
    x, y, c, me, chips = _place()
    local, sends, steps = [], [], []
    for i, (src, dst) in enumerate(zip(ins, outs)):
        local.append(pltpu.make_async_copy(src, dst.at[me], local_s.at[i]))
        half = src.shape[0] // 2
        split = src.shape[0] % (2 * HALF_ROWS_QUANTUM) == 0
        if split:
            mine = pl.ds(pl.multiple_of(c * half, HALF_ROWS_QUANTUM), half)
            theirs = pl.ds(pl.multiple_of((1 - c) * half, HALF_ROWS_QUANTUM), half)
        for r, (px, py) in enumerate(chips):
            k, peer = 3 * i + r, 2 * px + py
            if split:
                sends.append(_remote(src.at[mine], dst.at[me, mine], send_s.at[k], recv_s.at[k], (px, py, c)))
                landed = dst.at[peer, mine]
                steps.append((_remote(src.at[mine], landed, send_s.at[k], recv_s.at[k], (px, py, c)),
                              _remote(landed, landed, fsend_s.at[k], frecv_s.at[k], (x, y, 1 - c)),
                              _remote(dst.at[peer, theirs], dst.at[peer, theirs], fsend_s.at[k], frecv_s.at[k], (x, y, 1 - c))))
            else:
                sends.append(_remote(src, dst.at[me], send_s.at[k], recv_s.at[k], (px, py, c)))
                steps.append((_remote(src, dst.at[peer], send_s.at[k], recv_s.at[k], (px, py, c)), None, None))
    return local, sends, steps


def _scatter_sems(n):
    return [pltpu.SemaphoreType.DMA((4 * n,))] * 2 + [pltpu.SemaphoreType.DMA((3 * n,))] * 2 + [pltpu.SemaphoreType.DMA((n,))]


def _scatter_copies(ins, outs, sems):
    send_s, recv_s, fsend_s, frecv_s, local_s = sems
    x, y, c, me, chips = _place()
    local, sends, steps = [], [], []
    for i, (src, dst) in enumerate(zip(ins, outs)):
        local.append(pltpu.make_async_copy(src.at[me], dst.at[N_DEV - 1], local_s.at[i]))
        for r, (px, py) in enumerate(chips):
            k = 4 * i + r
            cp = _remote(src.at[2 * px + py], dst.at[r], send_s.at[k], recv_s.at[k], (px, py, c))
            fwd = _remote(dst.at[r], dst.at[4 + r], fsend_s.at[3 * i + r], frecv_s.at[3 * i + r], (x, y, 1 - c))
            sends.append(cp)
            steps.append((cp, fwd, fwd))
        k = 4 * i + 3
        cp = _remote(src.at[me], dst.at[3], send_s.at[k], recv_s.at[k], (x, y, 1 - c))
        sends.append(cp)
        steps.append((cp, None, None))
    return local, sends, steps


def _exchange_start(plan):
    local, sends, _ = plan
    for cp in local + sends:
        cp.start()


def _exchange_pass_on(plan):
    for arrival, pass_on, _ in plan[2]:
        arrival.wait_recv()
        if pass_on is not None:
            pass_on.start()


def _exchange_finish(plan):
    local, sends, steps = plan
    for _, pass_on, passed in steps:
        if pass_on is not None:
            passed.wait_recv()
    for cp in sends:
        cp.wait_send()
    for _, pass_on, _ in steps:
        if pass_on is not None:
            pass_on.wait_send()
    for cp in local:
        cp.wait()


def _exchange_call(arrays, copies, sems, out_shapes, name):
    n = len(arrays)

    def body(*refs):
        plan = copies(refs[:n], refs[n:2 * n], refs[2 * n:])
        _exchange_start(plan)
        _exchange_pass_on(plan)
        _exchange_finish(plan)

    return pl.pallas_call(
        body, name=name, in_specs=[HBM_REF] * n, out_specs=[HBM_REF] * n, out_shape=out_shapes,
        scratch_shapes=sems(n), compiler_params=pltpu.CompilerParams(has_side_effects=True),
    )(*arrays)


def _gather_call(shards, name):
    shapes = [jax.ShapeDtypeStruct((N_CHIPS,) + a.shape, a.dtype) for a in shards]
    return _exchange_call(shards, _gather_copies, _gather_sems, shapes, name)


def _all_reduce_small(a):
    def body(in_ref, out_ref, slots, send_sems, recv_sems):
        x, y, c = lax.axis_index("x"), lax.axis_index("y"), lax.axis_index("c")
        me = 4 * x + 2 * y + c
        slots[0] = in_ref[...]
        sends = []
        for r in range(1, N_DEV):
            peer = (x ^ (r >> 2), y ^ ((r >> 1) & 1), c ^ (r & 1))
            sends.append(pltpu.make_async_remote_copy(src_ref=in_ref, dst_ref=slots.at[r], send_sem=send_sems.at[r],
                                                      recv_sem=recv_sems.at[r], device_id=peer, device_id_type=MESH))
        for cp in sends:
            cp.start()
        for cp in sends:
            cp.wait_recv()
        acc = slots[me]
        for dev in range(1, N_DEV):
            acc = acc + slots[dev ^ me]
        out_ref[...] = acc
        for cp in sends:
            cp.wait_send()

    return pl.pallas_call(
        body, name="small_all_reduce",
        in_specs=[pl.BlockSpec(memory_space=pltpu.VMEM)], out_specs=pl.BlockSpec(memory_space=pltpu.VMEM),
        out_shape=jax.ShapeDtypeStruct(a.shape, a.dtype),
        scratch_shapes=[pltpu.VMEM((N_DEV,) + a.shape, a.dtype), pltpu.SemaphoreType.DMA((N_DEV,)),
                        pltpu.SemaphoreType.DMA((N_DEV,))],
        compiler_params=pltpu.CompilerParams(has_side_effects=True),
    )(a)


SHARDED = ["w_in", "mla_w_q_up", "mla_w_kv_up", "w_out", "ffn_w_up", "ffn_w_down", "ple_w_gate", "ple_w_proj",
           "gdn_conv_w", "ffn_conv_w"]
SHARD_AXIS = {"w_in": 1, "mla_w_q_up": 1, "mla_w_kv_up": 1, "w_out": 0, "ffn_w_up": 1, "ffn_w_down": 0,
              "ple_w_gate": 0, "ple_w_proj": 1, "gdn_conv_w": 1, "ffn_conv_w": 1}
SMALL = ["gdn_a_log", "gdn_dt_bias", "gdn_norm_g", "mla_q_norm_g", "mla_kv_norm_g", "ln1_g", "ln1_b", "ffn_conv_b",
         "ple_b_gate", "ln2_g", "ln2_b"]
WEIGHTS = ["w_in", "gdn_conv_w", "gdn_a_log", "gdn_dt_bias", "gdn_norm_g", "mla_q_norm_g", "mla_w_q_up", "mla_kv_norm_g",
           "mla_w_kv_up", "w_out", "ln1_g", "ln1_b", "ffn_w_up", "ffn_conv_w", "ffn_conv_b", "ffn_w_down", "ple_w_gate",
           "ple_b_gate", "ple_w_proj", "ln2_g", "ln2_b"]
F32_ON_WIRE = ("gdn_conv_w", "ffn_conv_w")
GATHER_EARLY = ["w_in", "gdn_conv_w"]
GATHER_LATE = ["mla_w_q_up", "mla_w_kv_up", "w_out", "ffn_w_up", "ffn_conv_w", "ffn_w_down", "ple_w_gate", "ple_w_proj"]
SCATTER_EARLY = ["ffn_w_up", "ffn_conv_w", "ffn_w_down", "ple_w_gate", "ple_w_proj", "w_out"]
SCATTER_LATE = ["w_in", "gdn_conv_w", "mla_w_q_up", "mla_w_kv_up"]
PACK_COLS = 1024
PACK_ROW_TILE = 8


def _join_blocks(blocks, axis):
    n, a, b = blocks.shape
    if axis == 0:
        return blocks.reshape(n * a, b)
    return jnp.transpose(blocks, (1, 0, 2)).reshape(a, n * b)


def _split_blocks(full, axis):
    if axis == 0:
        return full.reshape(N_CHIPS, full.shape[0] // N_CHIPS, full.shape[1])
    a, nb = full.shape
    return jnp.transpose(full.reshape(a, N_CHIPS, nb // N_CHIPS), (1, 0, 2))


def _pack(arrays):
    flat = jnp.concatenate([a.reshape(-1) for a in arrays])
    quantum = PACK_COLS * PACK_ROW_TILE
    padded = -(-flat.shape[0] // quantum) * quantum
    return jnp.pad(flat, (0, padded - flat.shape[0])).reshape(-1, PACK_COLS)


def _unpack(packed, shapes):
    flat = packed.reshape(-1)
    out, off = [], 0
    for shp in shapes:
        n = int(np.prod(shp))
        out.append(flat[off:off + n].reshape(shp))
        off += n
    return out


def kernel(x, p, w_in, gdn_conv_w, gdn_a_log, gdn_dt_bias, gdn_norm_g, mla_q_norm_g, mla_w_q_up, mla_kv_norm_g, mla_w_kv_up, w_out, ln1_g, ln1_b, ffn_w_up, ffn_conv_w, ffn_conv_b, ffn_w_down, ple_w_gate, ple_b_gate, ple_w_proj, ln2_g, ln2_b, loss_target, m_w_in, m_gdn_conv_w, m_gdn_a_log, m_gdn_dt_bias, m_gdn_norm_g, m_mla_q_norm_g, m_mla_w_q_up, m_mla_kv_norm_g, m_mla_w_kv_up, m_w_out, m_ln1_g, m_ln1_b, m_ffn_w_up, m_ffn_conv_w, m_ffn_conv_b, m_ffn_w_down, m_ple_w_gate, m_ple_b_gate, m_ple_w_proj, m_ln2_g, m_ln2_b, v_w_in, v_gdn_conv_w, v_gdn_a_log, v_gdn_dt_bias, v_gdn_norm_g, v_mla_q_norm_g, v_mla_w_q_up, v_mla_kv_norm_g, v_mla_w_kv_up, v_w_out, v_ln1_g, v_ln1_b, v_ffn_w_up, v_ffn_conv_w, v_ffn_conv_b, v_ffn_w_down, v_ple_w_gate, v_ple_b_gate, v_ple_w_proj, v_ln2_g, v_ln2_b):
    given = dict(locals())
    wsh = {n: given[n][0] for n in WEIGHTS}
    msh = {n: given["m_" + n][0] for n in WEIGHTS}
    vsh = {n: given["v_" + n][0] for n in WEIGHTS}
    bl, s, _ = x.shape
    t = bl * s
    xt = x.reshape(t, D_MODEL)
    pt = p.reshape(t, PLE_DIM)
    target = loss_target.reshape(t, D_MODEL)

    wire = lambda n: wsh[n] if n in F32_ON_WIRE else wsh[n].astype(BF16)
    early = _gather_call([wire(n) for n in GATHER_EARLY], "weights_gather_early")
    full = {n: _join_blocks(g, SHARD_AXIS[n]) for n, g in zip(GATHER_EARLY, early)}
    late_shards = [wire(n) for n in GATHER_LATE]
    late_ride = (late_shards, _gather_copies, _gather_sems,
                 [jax.ShapeDtypeStruct((N_CHIPS,) + a.shape, a.dtype) for a in late_shards], 0.75)

    in_cols, q_cols = _w_in_cols(), _w_q_cols()
    w_in_p = _pad_cols(full["w_in"], in_cols)
    gconv = full["gdn_conv_w"]
    row = lambda a: a.reshape(1, -1)
    sc = jnp.zeros((8, 128), F32).at[0, :GDN_HEADS].set(wsh["gdn_a_log"]).at[1, :GDN_HEADS].set(wsh["gdn_dt_bias"])
    norm_g, qg, kvg = row(wsh["gdn_norm_g"]), row(wsh["mla_q_norm_g"]), row(wsh["mla_kv_norm_g"])
    g1, b1, g2, b2 = row(wsh["ln1_g"]), row(wsh["ln1_b"]), row(wsh["ln2_g"]), row(wsh["ln2_b"])
    fbias, bgate = row(wsh["ffn_conv_b"]), row(wsh["ple_b_gate"])

    inv = ROPE_THETA ** (-jnp.arange(0, MLA_ROPE, 2, dtype=F32) / MLA_ROPE)
    ang = jnp.arange(s, dtype=F32)[:, None] * inv[None, :]
    zero = jnp.zeros_like(ang)
    cos_t = jnp.concatenate([jnp.cos(ang), zero, jnp.cos(ang), zero], axis=1)
    sin_t = jnp.concatenate([-jnp.sin(ang), zero, jnp.sin(ang), zero], axis=1)

    proj = _matmul(xt, w_in_p, name="proj", tm=1024)
    cat, o_raw, states, qkvg, late = _gdn_fwd(proj, gconv, sc, norm_g, bl, s, late_ride)
    full.update({n: _join_blocks(g, SHARD_AXIS[n]) for n, g in zip(GATHER_LATE, late)})
    w_o, w_up, w_down = full["w_out"], full["ffn_w_up"], full["ffn_w_down"]
    w_gate, w_proj, fconv = full["ple_w_gate"], full["ple_w_proj"], full["ffn_conv_w"]
    w_q_p, w_kv = _pad_cols(full["mla_w_q_up"], q_cols), full["mla_w_kv_up"]
    qf, kvf, kr = _mla_prep_fwd(proj, qg, kvg, w_q_p, w_kv, cos_t, sin_t, s, 256)
    cat, attn_o32, attn_lse = _attn_fwd(qf, kvf, kr, cat, bl, s, 512)
    wide = dict(tm=1024, tn=1024)
    mix = _matmul(cat, w_o, name="mix", **wide)
    r1, h1, h1b, xb = _ln1_fwd(xt, mix, g1, b1, 256)
    u = _matmul(h1b, w_up, name="ffn_up", tm=1024, tn=1408)
    act = _ffn_act_fwd(u, fconv, fbias, bl, s, 256)
    ffn = _matmul(act, w_down, name="ffn_down", tk=1408, **wide)
    gpre = _matmul(h1b, w_gate, name="ple_gate", **wide)
    pp = _matmul(pt, w_proj, name="ple_proj", **wide)
    dr2, dr2b, dgpre, dpp, loss_acc, dbgate, dg2, db2 = _head(h1, ffn, gpre, pp, bgate, g2, b2, target, 256)

    dact = _matmul(dr2b, w_down, name="d_act", tb=True, tm=1024, tn=1408)
    long_k = dict(ta=True, tk=2048)
    d_w_down = _matmul(act, dr2b, name="dw_down", tm=1408, tn=1024, **long_k)
    du, dfcw_g, dfcw_u, dfcb_g, dfcb_u = _ffn_act_bwd(u, fconv, fbias, dact, bl, s, 256)
    dh1_a = _matmul(du, w_up, name="dh1_ffn", tb=True, tk=1408, a_halves=True, **wide)
    dh1_b = _matmul(dgpre, w_gate, name="dh1_ple", tb=True, **wide)
    d_w_up = _matmul(h1b, du, name="dw_up", tn=1408, b_halves=True, **long_k)
    d_w_gate = _matmul(h1b, dgpre, name="dw_gate", **long_k, **wide)
    d_w_proj = _matmul(pt, dpp, name="dw_proj", ta=True, tn=1024)
    dr1, dr1b, dg1, db1 = _ln1_bwd(r1, dr2, dh1_a, dh1_b, g1, b1, 256)
    dcat = _matmul(dr1b, w_o, name="d_cat", tb=True, **wide)
    d_w_o = _matmul(cat, dr1b, name="dw_out", **long_k, **wide)

    gfull = {
        "ffn_w_up": d_w_up, "ffn_w_down": d_w_down, "ple_w_gate": d_w_gate, "ple_w_proj": d_w_proj, "w_out": d_w_o,
        "ffn_conv_w": jnp.concatenate([jnp.sum(dfcw_g, 0), jnp.sum(dfcw_u, 0)], axis=1),
    }
    slabs = {n: _split_blocks(gfull[n], SHARD_AXIS[n]).astype(BF16) for n in SCATTER_EARLY}
    early_slabs = [slabs[n] for n in SCATTER_EARLY]
    early_ride = (early_slabs, _scatter_copies, _scatter_sems,
                  [jax.ShapeDtypeStruct((N_DEV,) + a.shape[1:], a.dtype) for a in early_slabs], 0.7)
    dproj, dab, dcwq, dcwk, dcwv, dsc, dng, early_recv = _gdn_bwd(proj, gconv, sc, norm_g, o_raw, states, qkvg, dcat, bl, s,
                                                                  early_ride)
    received = dict(zip(SCATTER_EARLY, early_recv))
    dqf, dkvf, dkr = _attn_bwd(qf, kvf, kr, dcat, attn_o32, attn_lse, bl, s, 256)
    dproj, dqg, dkvg, d_w_q_p, d_w_kv = _mla_prep_bwd(proj, qg, kvg, w_q_p, w_kv, cos_t, sin_t, dqf, dkvf, dkr, dab, dproj, s, 256)
    d_w_in_p = _matmul(xb, dproj, name="dw_in", **long_k, **wide)

    gfull.update({
        "w_in": _unpad_cols(d_w_in_p, in_cols, D_IN),
        "mla_w_q_up": _unpad_cols(d_w_q_p, q_cols, MLA_HEADS * (MLA_NOPE + MLA_ROPE)),
        "mla_w_kv_up": d_w_kv,
        "gdn_conv_w": jnp.concatenate([jnp.sum(dcwq, 0), jnp.sum(dcwk, 0), jnp.sum(dcwv, 0)], axis=1),
    })
    slabs.update({n: _split_blocks(gfull[n], SHARD_AXIS[n]).astype(BF16) for n in SCATTER_LATE})
    late_slabs = [slabs[n] for n in SCATTER_LATE]
    late_scatter = (late_slabs, _scatter_copies, _scatter_sems,
                    [jax.ShapeDtypeStruct((N_DEV,) + a.shape[1:], a.dtype) for a in late_slabs], 0.85)
    grad_x, late_recv = _matmul(dproj, w_in_p, name="d_x", tb=True, add=dr1, add_scale=ALPHA, ride=late_scatter, **wide)
    received.update(zip(SCATTER_LATE, late_recv))
    dsc_sum = jnp.sum(dsc, axis=(0, 1))
    gsmall = {
        "gdn_a_log": dsc_sum[0, :GDN_HEADS], "gdn_dt_bias": dsc_sum[1, :GDN_HEADS],
        "gdn_norm_g": jnp.sum(dng[:, :, 0, :], axis=(0, 1)),
        "mla_q_norm_g": dqg[0], "mla_kv_norm_g": dkvg[0], "ln1_g": dg1[0], "ln1_b": db1[0],
        "ffn_conv_b": jnp.concatenate([jnp.sum(dfcb_g, 0), jnp.sum(dfcb_u, 0)], axis=1)[0],
        "ple_b_gate": dbgate[0], "ln2_g": dg2[0], "ln2_b": db2[0],
    }

    big = [{}, {}, {}, {}]
    for n in SHARDED:
        for kind, val in enumerate(_adamw_reduced(received[n], wsh[n], msh[n], vsh[n], "adamw_" + n)):
            big[kind][n] = val

    small_shapes = [wsh[n].shape for n in SMALL]
    gsum = _all_reduce_small(_pack([gsmall[n] for n in SMALL]))
    spacks = _adamw_small(gsum, _pack([wsh[n] for n in SMALL]), _pack([msh[n] for n in SMALL]), _pack([vsh[n] for n in SMALL]))
    small = [dict(zip(SMALL, _unpack(pk, small_shapes))) for pk in spacks]

    loss = lax.psum(loss_acc[0, 0], ("x", "y", "c"))
    outs = [loss, grad_x.reshape(x.shape)]
    for kind in range(4):
        for n in WEIGHTS:
            val = big[kind][n] if n in big[kind] else small[kind][n]
            outs.append(val[None])
    return tuple(outs)
```

```python
import functools
import math

import numpy as np
import jax
import jax.numpy as jnp
from jax import lax
from jax.experimental import pallas as pl
from jax.experimental.pallas import tpu as pltpu

F32 = jnp.float32
BF16 = jnp.bfloat16

D_MODEL = 1024
CHUNK = 64
PLE_DIM = 256
GDN_HEADS = 4
GDN_DK = 128
GDN_DV = 128
GDN_CONV = 4
MLA_HEADS = 4
MLA_NOPE = 128
MLA_ROPE = 64
MLA_V = 128
MLA_Q_LORA = 384
MLA_KV_LORA = 256
ROPE_THETA = 10000.0
D_FF = 2816
FFN_CONV = 3
DEPTH = 1
ALPHA = (2.0 * DEPTH) ** 0.25
NORM_EPS = 1e-6
GDN_QK = GDN_HEADS * GDN_DK
GDN_VW = GDN_HEADS * GDN_DV
D_IN = 2 * GDN_QK + 2 * GDN_VW + 2 * GDN_HEADS + MLA_Q_LORA + MLA_KV_LORA + MLA_ROPE
ATT_SCALE = (MLA_NOPE + MLA_ROPE) ** -0.5

ADAM_LR = 0.001
ADAM_B1 = 0.9
ADAM_B2 = 0.999
ADAM_EPS = 1e-08
ADAM_WD = 0.01
ADAM_STEP = 10

LANES = 128
VMEM_LIMIT = 60 * 1024 * 1024
GDN_FWD_GROUP = 16
GDN_BWD_GROUP = 16
N_CHIPS = 4
N_DEV = 8

P_WIDTH = 3072
P_MLA = 2048
MESH = pl.DeviceIdType.MESH


def _rope_slot(j):
    return j if j < MLA_ROPE // 2 else 64 + (j - MLA_ROPE // 2)


def _w_in_cols():
    idx = -np.ones((P_WIDTH,), np.int64)
    for h in range(GDN_HEADS):
        base = h * 512
        idx[base:base + 128] = np.arange(128) + h * GDN_DK
        idx[base + 128:base + 256] = np.arange(128) + GDN_QK + h * GDN_DK
        idx[base + 256:base + 384] = np.arange(128) + 2 * GDN_QK + h * GDN_DV
        idx[base + 384:base + 512] = np.arange(128) + 2 * GDN_QK + GDN_VW + h * GDN_DV
    o_a = 2 * GDN_QK + 2 * GDN_VW
    idx[P_MLA:P_MLA + 2 * GDN_HEADS] = np.arange(2 * GDN_HEADS) + o_a
    o_cq = o_a + 2 * GDN_HEADS
    idx[P_MLA + 128:P_MLA + 512] = np.arange(MLA_Q_LORA) + o_cq
    o_ckv = o_cq + MLA_Q_LORA
    idx[P_MLA + 512:P_MLA + 768] = np.arange(MLA_KV_LORA) + o_ckv
    o_kr = o_ckv + MLA_KV_LORA
    for j in range(MLA_ROPE):
        idx[P_MLA + 768 + _rope_slot(j)] = o_kr + j
    return idx


def _w_q_cols():
    idx = -np.ones((MLA_HEADS * 256,), np.int64)
    for h in range(MLA_HEADS):
        o = h * (MLA_NOPE + MLA_ROPE)
        idx[h * 256:h * 256 + 128] = np.arange(128) + o
        for j in range(MLA_ROPE):
            idx[h * 256 + 128 + _rope_slot(j)] = o + MLA_NOPE + j
    return idx


def _pad_cols(w, idx):
    safe = np.where(idx >= 0, idx, 0)
    return jnp.where(jnp.asarray(idx >= 0)[None, :], w[:, safe], 0.0)


def _unpad_cols(wp, idx, n):
    inv = np.zeros((n,), np.int64)
    inv[idx[idx >= 0]] = np.nonzero(idx >= 0)[0]
    return wp[:, inv]


def _dot(a, b, ca, cb, precision=None):
    if precision is None:
        a = a.astype(BF16)
        b = b.astype(BF16)
    return lax.dot_general(a, b, (((ca,), (cb,)), ((), ())), preferred_element_type=F32, precision=precision)


@jax.custom_vjp
def mm(a, b):
    return _dot(a, b, 1, 0)


@jax.custom_vjp
def mm_nt(a, b):
    return _dot(a, b, 1, 1)


@jax.custom_vjp
def mm_tn(a, b):
    return _dot(a, b, 0, 0)


mm.defvjp(lambda a, b: (mm(a, b), (a, b)), lambda r, g: (mm_nt(g, r[1]), mm_tn(r[0], g)))
mm_nt.defvjp(lambda a, b: (mm_nt(a, b), (a, b)), lambda r, g: (mm(g, r[1]), mm_tn(g, r[0])))
mm_tn.defvjp(lambda a, b: (mm_tn(a, b), (a, b)), lambda r, g: (mm_nt(r[1], g), mm(r[0], g)))

def _split(a):
    hi = a.astype(BF16)
    return hi, (a - hi.astype(F32)).astype(BF16)


def _dot3(a, b, ca, cb):
    a_hi, a_lo = _split(a)
    b_hi, b_lo = _split(b)
    return (_dot(a_hi, b_hi, ca, cb) + _dot(a_hi, b_lo, ca, cb)) + _dot(a_lo, b_hi, ca, cb)


def _shift_rows(x, s):
    return x if s == 0 else pltpu.roll(x, s % x.shape[0], 0)


def _row(w, j):
    tap = lax.broadcasted_iota(jnp.int32, w.shape, 0)
    return jnp.sum(jnp.where(tap == j, w, 0.0), axis=0, keepdims=True)


@jax.custom_vjp
def dwconv(x, w):
    k = w.shape[0]
    y = _row(w, k - 1) * x
    for j in range(k - 1):
        y = y + _row(w, j) * _shift_rows(x, k - 1 - j)
    return y


def _dwconv_fwd(x, w):
    return dwconv(x, w), (x, w)


def _dwconv_bwd(res, dy):
    x, w = res
    k = w.shape[0]
    dx = _row(w, k - 1) * dy
    tap = lax.broadcasted_iota(jnp.int32, w.shape, 0)
    dw = jnp.where(tap == k - 1, jnp.sum(dy * x, axis=0, keepdims=True), 0.0)
    for j in range(k - 1):
        dx = dx + _row(w, j) * _shift_rows(dy, -(k - 1 - j))
        dw = dw + jnp.where(tap == j, jnp.sum(dy * _shift_rows(x, k - 1 - j), axis=0, keepdims=True), 0.0)
    return dx, dw


dwconv.defvjp(_dwconv_fwd, _dwconv_bwd)


@jax.custom_vjp
def rope128(x, cos, sin):
    return x * cos + pltpu.roll(x, 64, 1) * sin


rope128.defvjp(lambda x, c, s: (rope128(x, c, s), (c, s)),
               lambda r, g: (g * r[0] + pltpu.roll(g * r[1], 64, 1), jnp.zeros_like(r[0]), jnp.zeros_like(r[1])))


def _silu(x):
    return x * jax.nn.sigmoid(x)


def _softplus(x):
    return jnp.maximum(x, 0.0) + jnp.log(1.0 + jnp.exp(-jnp.abs(x)))


def _rmsnorm(x, g):
    return x * lax.rsqrt(jnp.mean(x * x, axis=-1, keepdims=True) + NORM_EPS) * g


def _layernorm(x, g, b):
    mu = jnp.mean(x, axis=-1, keepdims=True)
    xc = x - mu
    var = jnp.mean(xc * xc, axis=-1, keepdims=True)
    return xc * lax.rsqrt(var + NORM_EPS) * g + b


def _pick_lane(row, lane):
    idx = lax.broadcasted_iota(jnp.int32, row.shape, 1)
    return jnp.sum(jnp.where(idx == lane, row, 0.0), axis=1, keepdims=True)


def _gdn_q(pq, cw):
    h = _silu(dwconv(pq, cw))
    return h * lax.rsqrt(jnp.sum(h * h, axis=-1, keepdims=True) + NORM_EPS) * (GDN_DK ** -0.5)


def _gdn_k(pk, cw):
    h = _silu(dwconv(pk, cw))
    return h * lax.rsqrt(jnp.sum(h * h, axis=-1, keepdims=True) + NORM_EPS)


def _gdn_v(pv, cw):
    return _silu(dwconv(pv, cw))


def _gdn_gate(ab, sc, head):
    a = _pick_lane(ab, head)
    b = _pick_lane(ab, GDN_HEADS + head)
    a_log = _pick_lane(_row(sc, 0), head)
    dt_bias = _pick_lane(_row(sc, 1), head)
    beta = jax.nn.sigmoid(b)
    g = -jnp.exp(a_log) * _softplus(a + dt_bias)
    return _two_lanes(g, beta)


def _two_lanes(c0, c1):
    lane = lax.broadcasted_iota(jnp.int32, (c0.shape[0], LANES), 1)
    return jnp.where(lane == 0, c0, jnp.where(lane == 1, c1, 0.0))


def _run(levels):
    try:
        while True:
            next(levels)
    except StopIteration as stop:
        return stop.value


def _inverse_levels(lows):
    n = lows[0].shape[0]
    ii = lax.broadcasted_iota(jnp.int32, (n, n), 0)
    jj = lax.broadcasted_iota(jnp.int32, (n, n), 1)
    eye = jnp.where(ii == jj, 1.0, 0.0)
    invs = [eye - low for low in lows]
    powers = [_dot3(low, low, 1, 0) for low in lows]
    yield
    k = 2
    while k < n:
        invs = [inv + _dot3(inv, p, 1, 0) for inv, p in zip(invs, powers)]
        yield
        k *= 2
        if k < n:
            powers = [_dot3(p, p, 1, 0) for p in powers]
            yield
    return invs


def _inverse_group(lows):
    return _run(_inverse_levels(lows))


@jax.custom_vjp
def solve_group(lows, rhss):
    return [_dot3(inv, rhs, 1, 0) for inv, rhs in zip(_inverse_group(lows), rhss)]


def _solve_whole(lows, rhss):
    return solve_group(lows, rhss)
    yield


def _solve_levels(lows, rhss):
    invs = yield from _inverse_levels(lows)
    return [_dot3(inv, rhs, 1, 0) for inv, rhs in zip(invs, rhss)]


def _solve_group_fwd(lows, rhss):
    invs = _inverse_group(lows)
    xs = [_dot3(inv, rhs, 1, 0) for inv, rhs in zip(invs, rhss)]
    return xs, (invs, xs)


def _solve_group_bwd(res, dxs):
    invs, xs = res
    n = invs[0].shape[0]
    strict = lax.broadcasted_iota(jnp.int32, (n, n), 0) > lax.broadcasted_iota(jnp.int32, (n, n), 1)
    drhss = [_dot3(inv, dx, 0, 0) for inv, dx in zip(invs, dxs)]
    dlows = [jnp.where(strict, -_dot3(drhs, x, 1, 1), 0.0) for drhs, x in zip(drhss, xs)]
    return dlows, drhss


solve_group.defvjp(_solve_group_fwd, _solve_group_bwd)


def _gdn_local_group(qs, ks, vs, gbs):
    return _run(_gdn_local_levels(qs, ks, vs, gbs, _solve_whole))


def _gdn_local_levels(qs, ks, vs, gbs, solve):
    c = qs[0].shape[0]
    ii = lax.broadcasted_iota(jnp.int32, (c, c), 0)
    jj = lax.broadcasted_iota(jnp.int32, (c, c), 1)
    incl = ii >= jj
    gs = [_pick_lane(gb, 0) for gb in gbs]
    betas = [_pick_lane(gb, 1) for gb in gbs]
    g_rows = [jnp.sum(jnp.where(ii == jj, g, 0.0), axis=0, keepdims=True) for g in gs]
    gc_cols = [jnp.sum(jnp.where(incl, g_row, 0.0), axis=1, keepdims=True) for g_row in g_rows]
    gc_rows = [jnp.sum(jnp.where(jj >= ii, g, 0.0), axis=0, keepdims=True) for g in gs]
    decays = [jnp.where(incl, jnp.exp(jnp.where(incl, gc - gr, 0.0)), 0.0) for gc, gr in zip(gc_cols, gc_rows)]
    yield
    kbs = [k * beta for k, beta in zip(ks, betas)]
    lows = [jnp.where(ii > jj, mm_nt(kb, k) * decay, 0.0) for kb, k, decay in zip(kbs, ks, decays)]
    egs = [jnp.exp(gc) for gc in gc_cols]
    yield
    wus = yield from solve(lows, [jnp.concatenate([kb * eg, v * beta], axis=1) for kb, eg, v, beta in zip(kbs, egs, vs, betas)])
    yield
    qks = [mm_nt(q, k) * decay for q, k, decay in zip(qs, ks, decays)]
    g_lasts = [jnp.sum(g_row, axis=1, keepdims=True) for g_row in g_rows]
    kds = [k * jnp.exp(gl - gc) for k, gl, gc in zip(ks, g_lasts, gc_cols)]
    yield
    ws, us = [wu[:, :GDN_DK] for wu in wus], [wu[:, GDN_DK:] for wu in wus]
    q_effs = [q * eg - mm(qk, w) for q, eg, qk, w in zip(qs, egs, qks, ws)]
    yield
    o_locals = [mm(qk, u) for qk, u in zip(qks, us)]
    yield
    mixes = [mm_tn(kd, w) for kd, w in zip(kds, ws)]
    yield
    adds = [mm_tn(kd, u) for kd, u in zip(kds, us)]
    return [(q_eff, o_loc, mix, add, jnp.exp(gl))
            for q_eff, o_loc, mix, add, gl in zip(q_effs, o_locals, mixes, adds, g_lasts)]


def _gdn_state_step(q_eff, o_local, mix, add, eg_last, state):
    return mm(q_eff, state) + o_local, state * eg_last - mm(mix, state) + add


def _gdn_post(o, z, norm_g):
    return _rmsnorm(o, norm_g) * _silu(z)


MASKED = -1e30


def _scores(q, kn, kr, q0, k0):
    s = (mm_nt(q[:, :128], kn) + mm_nt(q[:, 128:], kr)) * ATT_SCALE
    if k0 + kn.shape[0] <= q0:
        return s
    qpos = q0 + lax.broadcasted_iota(jnp.int32, s.shape, 0)
    kpos = k0 + lax.broadcasted_iota(jnp.int32, s.shape, 1)
    shift = int(math.log2(CHUNK))
    return jnp.where((kpos >> shift) <= (qpos >> shift), s, MASKED)


def _softmax_times(s, v):
    top = jnp.max(s, axis=-1, keepdims=True)
    p = jnp.exp(s - top)
    norm = jnp.sum(p, axis=-1, keepdims=True)
    return mm(p / norm, v), top + jnp.log(norm)


def _mla_prep(pm, qg, kvg, wq, wkv, cos, sin):
    cq = pm[:, 128:512]
    ckv = pm[:, 512:768]
    qf = mm(_rmsnorm(cq, qg), wq)
    parts = []
    for h in range(MLA_HEADS):
        parts.append(qf[:, h * 256:h * 256 + 128])
        parts.append(rope128(qf[:, h * 256 + 128:h * 256 + 256], cos, sin))
    kvf = mm(_rmsnorm(ckv, kvg), wkv)
    return jnp.concatenate(parts, axis=1), kvf, rope128(pm[:, 768:896], cos, sin)


def _ffn_act(ug, uu, wg, wu, bg, bu):
    return _silu(dwconv(ug, wg) + bg) * (dwconv(uu, wu) + bu)


def _head_loss(h1, ffn, gpre, pp, bgate, g2, b2, target):
    gate = jax.nn.sigmoid(gpre + bgate)
    h2 = _layernorm(ALPHA * h1 + ffn + gate * pp, g2, b2)
    err = h2 - target
    return 0.5 * jnp.sum(jnp.sum(err * err, axis=1, keepdims=True), axis=0, keepdims=True) / D_MODEL


ROW_TILE_VREGS = 32
CONV_HALO = 8


def _rows_per_tile(n_rows, cols):
    tile = min(n_rows, ROW_TILE_VREGS * 8 * LANES // cols)
    assert n_rows % tile == 0 and tile % CONV_HALO == 0, (n_rows, cols)
    return tile


def _tile_inputs(loads, t0, first, halo, tile):
    if halo == 0:
        return [ld(pl.ds(t0, tile)) for ld in loads]
    if first:
        xs = [ld(pl.ds(0, tile)) for ld in loads]
        return [jnp.concatenate([jnp.zeros((halo, x.shape[1]), x.dtype), x], axis=0) for x in xs]
    return [ld(pl.ds(pl.multiple_of(t0 - halo, CONV_HALO), tile + halo)) for ld in loads]


def _rows_apply(fn, loads, consts, store, n_rows, cols, halo):
    tile = _rows_per_tile(n_rows, cols)

    def one(t0, first):
        y = fn(*_tile_inputs(loads, t0, first, halo, tile), *consts)
        store(pl.ds(t0, tile), y[halo:] if halo else y)

    one(0, True)

    def step(i, carry):
        one(pl.multiple_of(i * tile, tile), False)
        return carry

    lax.fori_loop(1, n_rows // tile, step, 0)


def _rows_vjp(fn, loads, consts, load_dy, stores, n_rows, cols, halo):
    tile = _rows_per_tile(n_rows, cols)

    def one(t0, first, dconsts):
        xs = _tile_inputs(loads, t0, first, halo, tile)
        _, vjp = jax.vjp(lambda *a: fn(*a)[halo:] if halo else fn(*a), *xs, *consts)
        grads = vjp(load_dy(pl.ds(t0, tile)))
        for st, dx in zip(stores, grads[:len(xs)]):
            st(pl.ds(t0, tile), dx[halo:] if halo else dx, False)
            if halo and not first:
                st(pl.ds(pl.multiple_of(t0 - halo, CONV_HALO), halo), dx[:halo], True)
        return tuple(a + b for a, b in zip(dconsts, grads[len(xs):]))

    dconsts = one(0, True, tuple(jnp.zeros_like(c) for c in consts))
    return lax.fori_loop(1, n_rows // tile, lambda i, dc: one(pl.multiple_of(i * tile, tile), False, dc), dconsts)


def _params(sem):
    return pltpu.CompilerParams(dimension_semantics=sem, vmem_limit_bytes=VMEM_LIMIT)


def _matmul(a, b, *, name, ta=False, tb=False, tm=512, tn=512, tk=1024, add=None, add_scale=1.0,
            a_halves=False, b_halves=False, ride=None, out_dtype=F32):
    assert not (a_halves and ta) and not (b_halves and tb)
    a_shape = (a.shape[1], 2 * a.shape[2]) if a_halves else a.shape
    b_shape = (b.shape[1], 2 * b.shape[2]) if b_halves else b.shape
    (k_dim, m) = a_shape if ta else a_shape[::-1]
    (n, k2) = b_shape if tb else b_shape[::-1]
    assert k_dim == k2, (a.shape, b.shape)
    tm, tn, tk = min(tm, m), min(tn, n), min(tk, k_dim)
    assert m % tm == 0 and n % tn == 0 and k_dim % tk == 0, (name, m, n, k_dim, tm, tn, tk)
    nk = k_dim // tk
    ca, cb = (0 if ta else 1), (1 if tb else 0)

    def body(*refs):
        if add is None:
            a_ref, b_ref, o_ref, acc = refs
        else:
            a_ref, b_ref, c_ref, o_ref, acc = refs
        kk = pl.program_id(2)

        @pl.when(kk == 0)
        def _():
            acc[...] = jnp.zeros_like(acc)

        acc[...] += _dot(a_ref[...], b_ref[...], ca, cb)

        @pl.when(kk == nk - 1)
        def _():
            r = acc[...]
            if add is not None:
                r = r + add_scale * c_ref[...]
            o_ref[...] = r.astype(out_dtype)

    spec = pl.BlockSpec
    a_spec = spec((tk, tm), lambda i, j, k: (k, i)) if ta else spec((tm, tk), lambda i, j, k: (i, k))
    b_spec = spec((tn, tk), lambda i, j, k: (j, k)) if tb else spec((tk, tn), lambda i, j, k: (k, j))
    if a_halves:
        kh = k_dim // 2 // tk
        assert kh * tk * 2 == k_dim
        a_spec = spec((None, tm, tk), lambda i, j, k: (k // kh, i, k % kh))
    if b_halves:
        nh = n // 2 // tn
        assert nh * tn * 2 == n
        b_spec = spec((None, tk, tn), lambda i, j, k: (j // nh, k, j % nh))
    in_specs = [a_spec, b_spec]
    args = [a, b]
    if add is not None:
        in_specs.append(pl.BlockSpec((tm, tn), lambda i, j, k: (i, j)))
        args.append(add)
    grid = (m // tm, n // tn, nk)
    r_in, r_out, r_shapes, r_sems, r_args = _ride_specs(ride)
    outs = pl.pallas_call(
        _riding(body, len(args), 1, 1, ride, grid), name=name, grid=grid,
        in_specs=in_specs + r_in, out_specs=[pl.BlockSpec((tm, tn), lambda i, j, k: (i, j))] + r_out,
        out_shape=[jax.ShapeDtypeStruct((m, n), out_dtype)] + r_shapes,
        scratch_shapes=[pltpu.VMEM((tm, tn), F32)] + r_sems,
        compiler_params=_params(("parallel", "parallel", "arbitrary") if ride is None else ("arbitrary",) * 3),
    )(*args, *r_args)
    return outs[0] if ride is None else (outs[0], list(outs[1:]))


def _riding(core, n_in, n_out, n_scratch, ride, grid):
    if ride is None:
        return core
    copies, nr = ride[1], len(ride[0])
    steps = int(np.prod(grid))
    pass_step = min(max(int(steps * ride[4]), 1), steps - 2)
    assert steps >= 3, grid

    def body(*refs):
        cuts = np.cumsum([0, n_in, nr, n_out, nr, n_scratch])
        ins, rin, outs, rout, scratch = (refs[a:b] for a, b in zip(cuts[:-1], cuts[1:]))
        sems = refs[cuts[-1]:]
        step = 0
        for axis, size in enumerate(grid):
            step = step * size + pl.program_id(axis)

        @pl.when(step == 0)
        def _():
            _exchange_start(copies(rin, rout, sems))

        @pl.when(step == pass_step)
        def _():
            _exchange_pass_on(copies(rin, rout, sems))

        core(*ins, *outs, *scratch)

        @pl.when(step == steps - 1)
        def _():
            _exchange_finish(copies(rin, rout, sems))

    return body


def _ride_specs(ride):
    if ride is None:
        return [], [], [], [], []
    arrays, _, sems, shapes, _ = ride
    return [HBM_REF] * len(arrays), [HBM_REF] * len(arrays), list(shapes), sems(len(arrays)), list(arrays)


def _gdn_fwd(proj, conv_w, sc, norm_g, bl, s, ride=None):
    nc = s // CHUNK

    def core(ph_ref, ab_ref, cwq_ref, cwk_ref, cwv_ref, sc_ref, ng_ref, cat_ref, o_ref, st_ref, q_s, k_s, v_s, gb_s):
        def into(ref):
            def store(rows, value):
                ref[rows, :] = value.astype(ref.dtype)
            return store

        for fn, col, cw_ref, val_s in [(_gdn_q, 0, cwq_ref, q_s), (_gdn_k, 128, cwk_ref, k_s), (_gdn_v, 256, cwv_ref, v_s)]:
            _rows_apply(fn, [lambda r, col=col: ph_ref[r, col:col + 128]], [cw_ref[...]], into(val_s), s, LANES, CONV_HALO)
        _rows_apply(functools.partial(_gdn_gate, head=pl.program_id(1)), [lambda r: ab_ref[r, :]], [sc_ref[...]], into(gb_s), s, LANES, 0)

        group = math.gcd(nc, GDN_FWD_GROUP)

        def rows_of(n):
            return slice(n * CHUNK, (n + 1) * CHUNK)

        def levels_of(g):
            rows = [rows_of(g * group + j) for j in range(group)]
            return _gdn_local_levels([q_s[r, :] for r in rows], [k_s[r, :] for r in rows], [v_s[r, :] for r in rows],
                                     [gb_s[r, :] for r in rows], _solve_levels)

        local, state = _run(levels_of(0)), jnp.zeros((GDN_DK, GDN_DV), F32)
        for g in range(nc // group):
            ahead = levels_of(g + 1) if g + 1 < nc // group else None
            following = None
            for j, loc in enumerate(local):
                n = g * group + j
                st_ref[n] = state
                o_ref[rows_of(n), :], state = _gdn_state_step(*loc, state)
                if ahead is not None and following is None:
                    try:
                        next(ahead)
                    except StopIteration as stop:
                        following = stop.value
            if ahead is not None and following is None:
                following = _run(ahead)
            local = following
        _rows_apply(_gdn_post, [lambda r: o_ref[r, :], lambda r: ph_ref[r, 384:512]], [ng_ref[...]], into(cat_ref), s, LANES, 0)

    t = bl * s
    r_in, r_out, r_shapes, r_sems, r_args = _ride_specs(ride)
    outs = pl.pallas_call(
        _riding(core, 7, 7, 0, ride, (bl, GDN_HEADS)), name="gdn_fwd", grid=(bl, GDN_HEADS),
        in_specs=[
            pl.BlockSpec((s, 512), lambda b, h: (b, h)),
            pl.BlockSpec((s, 128), lambda b, h: (b, P_MLA // 128)),
            pl.BlockSpec((GDN_CONV, 128), lambda b, h: (0, h)),
            pl.BlockSpec((GDN_CONV, 128), lambda b, h: (0, GDN_HEADS + h)),
            pl.BlockSpec((GDN_CONV, 128), lambda b, h: (0, 2 * GDN_HEADS + h)),
            pl.BlockSpec((8, 128), lambda b, h: (0, 0)),
            pl.BlockSpec((1, 128), lambda b, h: (0, 0)),
        ] + r_in,
        out_specs=[
            pl.BlockSpec((s, 128), lambda b, h: (b, h)),
            pl.BlockSpec((s, 128), lambda b, h: (b, h)),
            pl.BlockSpec((None, None, nc, GDN_DK, GDN_DV), lambda b, h: (b, h, 0, 0, 0)),
        ] + [pl.BlockSpec((s, 128), lambda b, h: (b, h))] * 4 + r_out,
        out_shape=[
            jax.ShapeDtypeStruct((t, 2 * GDN_VW), BF16),
            jax.ShapeDtypeStruct((t, GDN_VW), F32),
            jax.ShapeDtypeStruct((bl, GDN_HEADS, nc, GDN_DK, GDN_DV), F32),
        ] + [jax.ShapeDtypeStruct((t, GDN_VW), F32)] * 4 + r_shapes,
        scratch_shapes=r_sems,
        compiler_params=_params(("arbitrary", "arbitrary")),
    )(proj, proj, conv_w, conv_w, conv_w, sc, norm_g, *r_args)
    return outs[0], outs[1], outs[2], tuple(outs[3:7]), list(outs[7:])


def _gdn_bwd(proj, conv_w, sc, norm_g, o_raw, states, qkvg, dcat, bl, s, ride=None):
    nc = s // CHUNK

    def core(ph_ref, ab_ref, cwq_ref, cwk_ref, cwv_ref, sc_ref, ng_ref, o_ref, st_ref, dc_ref, q_in, k_in, v_in, gb_in,
             dph_ref, dab_ref, dcwq_ref, dcwk_ref, dcwv_ref, dsc_ref, dng_ref, q_s, k_s, v_s, gb_s, do_s):
        head = pl.program_id(1)
        gate = functools.partial(_gdn_gate, head=head)
        paths = [(_gdn_q, 0, cwq_ref, q_s, dcwq_ref), (_gdn_k, 128, cwk_ref, k_s, dcwk_ref), (_gdn_v, 256, cwv_ref, v_s, dcwv_ref)]
        def into(ref, cols=slice(None)):
            def store(rows, value, add=False):
                if add:
                    ref[rows, cols] += value.astype(ref.dtype)
                else:
                    ref[rows, cols] = value.astype(ref.dtype)
            return store

        (dng,) = _rows_vjp(_gdn_post, [lambda r: o_ref[r, :], lambda r: ph_ref[r, 384:512]], [ng_ref[...]],
                           lambda r: dc_ref[r, :], [into(do_s), into(dph_ref, slice(384, 512))], s, LANES, 0)
        dng_ref[...] = jnp.broadcast_to(dng, dng_ref.shape)

        group = math.gcd(nc, GDN_BWD_GROUP)

        def chunks(i, dstate):
            ns = [nc - 1 - (i * group + j) for j in range(group)]
            rows = [pl.ds(pl.multiple_of(n * CHUNK, CHUNK), CHUNK) for n in ns]
            local, local_vjp = jax.vjp(_gdn_local_group, [q_in[r, :] for r in rows], [k_in[r, :] for r in rows],
                                       [v_in[r, :] for r in rows], [gb_in[r, :] for r in rows])
            d_os = [do_s[r, :] for r in rows]
            dlocal = []
            for n, loc, d_o in zip(ns, local, d_os):
                _, step_vjp = jax.vjp(_gdn_state_step, *loc, st_ref[n])
                *dloc, dstate = step_vjp((d_o, dstate))
                dlocal.append(tuple(dloc))
            dqs, dks, dvs, dgbs = local_vjp(dlocal)
            for r, dq, dk, dv, dgb in zip(rows, dqs, dks, dvs, dgbs):
                q_s[r, :], k_s[r, :], v_s[r, :], gb_s[r, :] = dq, dk, dv, dgb
            return dstate

        lax.fori_loop(0, nc // group, chunks, jnp.zeros((GDN_DK, GDN_DV), F32))
        for fn, col, cw_ref, val_s, dcw_ref in paths:
            (dcw_ref[...],) = _rows_vjp(fn, [lambda r, col=col: ph_ref[r, col:col + 128]], [cw_ref[...]],
                                        lambda r, val_s=val_s: val_s[r, :], [into(val_s)], s, LANES, CONV_HALO)
            dph_ref[:, col:col + 128] = val_s[...].astype(BF16)

        @pl.when(head == 0)
        def _():
            dab_ref[...] = jnp.zeros_like(dab_ref)

        def add_dab(rows, value, add=False):
            dab_ref[rows, :] += value

        (dsc_ref[...],) = _rows_vjp(gate, [lambda r: ab_ref[r, :]], [sc_ref[...]], lambda r: gb_s[r, :], [add_dab], s, LANES, 0)

    t = bl * s
    cw_out = pl.BlockSpec((None, GDN_CONV, 128), lambda b, h: (b, 0, h))
    part = pl.BlockSpec((None, None, 8, 128), lambda b, h: (b, h, 0, 0))
    r_in, r_out, r_shapes, r_sems, r_args = _ride_specs(ride)
    outs = pl.pallas_call(
        _riding(core, 14, 7, 5, ride, (bl, GDN_HEADS)), name="gdn_bwd", grid=(bl, GDN_HEADS),
        in_specs=[
            pl.BlockSpec((s, 512), lambda b, h: (b, h)),
            pl.BlockSpec((s, 128), lambda b, h: (b, P_MLA // 128)),
            pl.BlockSpec((GDN_CONV, 128), lambda b, h: (0, h)),
            pl.BlockSpec((GDN_CONV, 128), lambda b, h: (0, GDN_HEADS + h)),
            pl.BlockSpec((GDN_CONV, 128), lambda b, h: (0, 2 * GDN_HEADS + h)),
            pl.BlockSpec((8, 128), lambda b, h: (0, 0)),
            pl.BlockSpec((1, 128), lambda b, h: (0, 0)),
            pl.BlockSpec((s, 128), lambda b, h: (b, h)),
            pl.BlockSpec((None, None, nc, GDN_DK, GDN_DV), lambda b, h: (b, h, 0, 0, 0)),
        ] + [pl.BlockSpec((s, 128), lambda b, h: (b, h))] * 5 + r_in,
        out_specs=[
            pl.BlockSpec((s, 512), lambda b, h: (b, h)),
            pl.BlockSpec((s, 128), lambda b, h: (b, 0)),
            cw_out, cw_out, cw_out, part, part,
        ] + r_out,
        out_shape=[
            jax.ShapeDtypeStruct((t, P_WIDTH), BF16),
            jax.ShapeDtypeStruct((t, 128), F32),
            jax.ShapeDtypeStruct((bl, GDN_CONV, 512), F32),
            jax.ShapeDtypeStruct((bl, GDN_CONV, 512), F32),
            jax.ShapeDtypeStruct((bl, GDN_CONV, 512), F32),
            jax.ShapeDtypeStruct((bl, GDN_HEADS, 8, 128), F32),
            jax.ShapeDtypeStruct((bl, GDN_HEADS, 8, 128), F32),
        ] + r_shapes,
        scratch_shapes=[pltpu.VMEM((s, 128), F32)] * 5 + r_sems,
        compiler_params=_params(("arbitrary", "arbitrary")),
    )(proj, proj, conv_w, conv_w, conv_w, sc, norm_g, o_raw, states, dcat, *qkvg, *r_args)
    return tuple(outs[:7]) + (list(outs[7:]),)


def _mla_prep_fwd(proj, qg, kvg, wq, wkv, cos, sin, s, tm):
    t = proj.shape[0]
    tm = min(tm, s)
    nps = s // tm
    const = lambda shape: pl.BlockSpec(shape, lambda i: (0, 0))

    def body(pm_ref, qg_ref, kvg_ref, wq_ref, wkv_ref, cos_ref, sin_ref, qf_ref, kvf_ref, kr_ref):
        qf, kvf, kr = _mla_prep(pm_ref[...], qg_ref[...], kvg_ref[...], wq_ref[...], wkv_ref[...], cos_ref[...], sin_ref[...])
        qf_ref[...], kvf_ref[...], kr_ref[...] = qf.astype(BF16), kvf.astype(BF16), kr.astype(BF16)

    return pl.pallas_call(
        body, name="mla_prep_fwd", grid=(t // tm,),
        in_specs=[
            pl.BlockSpec((tm, 1024), lambda i: (i, P_MLA // 1024)),
            const((1, MLA_Q_LORA)), const((1, MLA_KV_LORA)), const(wq.shape), const(wkv.shape),
            pl.BlockSpec((tm, 128), lambda i: (i % nps, 0)), pl.BlockSpec((tm, 128), lambda i: (i % nps, 0)),
        ],
        out_specs=[pl.BlockSpec((tm, 1024), lambda i: (i, 0)), pl.BlockSpec((tm, 1024), lambda i: (i, 0)),
                   pl.BlockSpec((tm, 128), lambda i: (i, 0))],
        out_shape=[jax.ShapeDtypeStruct((t, 1024), BF16), jax.ShapeDtypeStruct((t, 1024), BF16),
                   jax.ShapeDtypeStruct((t, 128), BF16)],
        compiler_params=_params(("parallel",)),
    )(proj, qg, kvg, wq, wkv, cos, sin)


def _mla_prep_bwd(proj, qg, kvg, wq, wkv, cos, sin, dqf, dkvf, dkr, dab, dproj, s, tm):
    t = proj.shape[0]
    tm = min(tm, s)
    nps = s // tm
    const = lambda shape: pl.BlockSpec(shape, lambda i: (0, 0))

    def body(pm_ref, qg_ref, kvg_ref, wq_ref, wkv_ref, cos_ref, sin_ref, dqf_ref, dkvf_ref, dkr_ref, dab_ref, dp_in,
             dp_ref, dqg_ref, dkvg_ref, dwq_ref, dwkv_ref):
        del dp_in
        fn = lambda pm, qg_, kvg_, wq_, wkv_: _mla_prep(pm, qg_, kvg_, wq_, wkv_, cos_ref[...], sin_ref[...])
        _, vjp = jax.vjp(fn, pm_ref[...], qg_ref[...], kvg_ref[...], wq_ref[...].astype(F32), wkv_ref[...].astype(F32))
        dpm, dqg, dkvg, dwq, dwkv = vjp((dqf_ref[...], dkvf_ref[...], dkr_ref[...]))
        dp_ref[...] = jnp.concatenate([dab_ref[...], dpm[:, 128:]], axis=1).astype(BF16)

        @pl.when(pl.program_id(0) == 0)
        def _():
            dqg_ref[...] = jnp.zeros_like(dqg_ref)
            dkvg_ref[...] = jnp.zeros_like(dkvg_ref)
            dwq_ref[...] = jnp.zeros_like(dwq_ref)
            dwkv_ref[...] = jnp.zeros_like(dwkv_ref)

        dqg_ref[...] += dqg
        dkvg_ref[...] += dkvg
        dwq_ref[...] += dwq
        dwkv_ref[...] += dwkv

    rows = lambda w: pl.BlockSpec((tm, w), lambda i: (i, 0))
    return pl.pallas_call(
        body, name="mla_prep_bwd", grid=(t // tm,),
        in_specs=[
            pl.BlockSpec((tm, 1024), lambda i: (i, P_MLA // 1024)),
            const((1, MLA_Q_LORA)), const((1, MLA_KV_LORA)), const(wq.shape), const(wkv.shape),
            pl.BlockSpec((tm, 128), lambda i: (i % nps, 0)), pl.BlockSpec((tm, 128), lambda i: (i % nps, 0)),
            rows(1024), rows(1024), rows(128), rows(128),
            pl.BlockSpec(memory_space=pl.ANY),
        ],
        out_specs=[pl.BlockSpec((tm, 1024), lambda i: (i, P_MLA // 1024)),
                   const((1, MLA_Q_LORA)), const((1, MLA_KV_LORA)), const(wq.shape), const(wkv.shape)],
        out_shape=[jax.ShapeDtypeStruct(dproj.shape, dproj.dtype),
                   jax.ShapeDtypeStruct((1, MLA_Q_LORA), F32), jax.ShapeDtypeStruct((1, MLA_KV_LORA), F32),
                   jax.ShapeDtypeStruct(wq.shape, F32), jax.ShapeDtypeStruct(wkv.shape, F32)],
        input_output_aliases={11: 0},
        compiler_params=_params(("arbitrary",)),
    )(proj, qg, kvg, wq, wkv, cos, sin, dqf, dkvf, dkr, dab, dproj)


def _attn_fwd(qf, kvf, kr, cat, bl, s, tq):
    tq = min(tq, s)
    nq = s // tq

    def body(q_ref, kv_ref, kr_ref, cat_in, o_ref, o32_ref, lse_ref):
        del cat_in

        def scores_of(i):
            keys = slice(0, (i + 1) * tq)
            return _scores(q_ref[i * tq:(i + 1) * tq, :], kv_ref[keys, 0:128], kr_ref[keys, :], i * tq, 0)

        ready = scores_of(0)
        for i in range(nq):
            scores = ready
            if i + 1 < nq:
                ready = scores_of(i + 1)
            rows = slice(i * tq, (i + 1) * tq)
            o, lse = _softmax_times(scores, kv_ref[0:(i + 1) * tq, 128:256])
            o_ref[rows, :] = o.astype(o_ref.dtype)
            o32_ref[rows, :] = o
            lse_ref[rows, :] = jnp.broadcast_to(lse, o.shape)

    t = bl * s
    head_cols = pl.BlockSpec((s, 128), lambda b, h: (b, h))
    return pl.pallas_call(
        body, name="attn_fwd", grid=(bl, MLA_HEADS),
        in_specs=[
            pl.BlockSpec((s, 256), lambda b, h: (b, h)),
            pl.BlockSpec((s, 256), lambda b, h: (b, h)),
            pl.BlockSpec((s, 128), lambda b, h: (b, 0)),
            pl.BlockSpec(memory_space=pl.ANY),
        ],
        out_specs=[pl.BlockSpec((s, 128), lambda b, h: (b, GDN_HEADS + h)), head_cols, head_cols],
        out_shape=[jax.ShapeDtypeStruct(cat.shape, cat.dtype)] + [jax.ShapeDtypeStruct((t, MLA_HEADS * MLA_V), F32)] * 2,
        input_output_aliases={3: 0},
        compiler_params=_params(("parallel", "parallel")),
    )(qf, kvf, kr, cat)


def _attn_bwd(qf, kvf, kr, dcat, o32, lse, bl, s, tq):
    tq = min(tq, s)
    nq = s // tq

    def body(q_ref, kv_ref, kr_ref, do_ref, o_ref, lse_ref, dq_ref, dkv_ref, dkr_ref):
        dkv_ref[...] = jnp.zeros_like(dkv_ref)

        @pl.when(pl.program_id(1) == 0)
        def _():
            dkr_ref[...] = jnp.zeros_like(dkr_ref)

        def block(i):
            rows = slice(i * tq, (i + 1) * tq)
            return q_ref[rows, :], do_ref[rows, :]

        def first_products(i, j):
            (q, d_o), keys = block(i), slice(j * tq, (j + 1) * tq)
            return _scores(q, kv_ref[keys, 0:128], kr_ref[keys, :], i * tq, j * tq), mm_nt(d_o, kv_ref[keys, 128:256])

        tiles = [(i, j) for i in range(nq) for j in range(i + 1)]
        ready = first_products(*tiles[0])
        for t, (i, j) in enumerate(tiles):
            scores, dp = ready
            if t + 1 < len(tiles):
                ready = first_products(*tiles[t + 1])
            rows, keys = slice(i * tq, (i + 1) * tq), slice(j * tq, (j + 1) * tq)
            q, d_o = block(i)
            if j == 0:
                delta = jnp.sum(d_o * o_ref[rows, :], axis=-1, keepdims=True)
                lse_i = jnp.max(lse_ref[rows, :], axis=-1, keepdims=True)
                dqn, dqr = jnp.zeros((tq, MLA_NOPE), F32), jnp.zeros((tq, 128), F32)
            p = jnp.exp(scores - lse_i)
            ds = p * (dp - delta) * ATT_SCALE
            dkv_ref[keys, 128:256] += mm_tn(p, d_o)
            dkv_ref[keys, 0:128] += mm_tn(ds, q[:, :128])
            dkr_ref[keys, :] += mm_tn(ds, q[:, 128:])
            dqn = dqn + mm(ds, kv_ref[keys, 0:128])
            dqr = dqr + mm(ds, kr_ref[keys, :])
            if j == i:
                dq_ref[rows, 0:128], dq_ref[rows, 128:256] = dqn, dqr

    t = bl * s
    head_cols = pl.BlockSpec((s, 128), lambda b, h: (b, h))
    return pl.pallas_call(
        body, name="attn_bwd", grid=(bl, MLA_HEADS),
        in_specs=[
            pl.BlockSpec((s, 256), lambda b, h: (b, h)),
            pl.BlockSpec((s, 256), lambda b, h: (b, h)),
            pl.BlockSpec((s, 128), lambda b, h: (b, 0)),
            pl.BlockSpec((s, 128), lambda b, h: (b, GDN_HEADS + h)),
            head_cols, head_cols,
        ],
        out_specs=[
            pl.BlockSpec((s, 256), lambda b, h: (b, h)),
            pl.BlockSpec((s, 256), lambda b, h: (b, h)),
            pl.BlockSpec((s, 128), lambda b, h: (b, 0)),
        ],
        out_shape=[jax.ShapeDtypeStruct((t, 1024), F32), jax.ShapeDtypeStruct((t, 1024), F32),
                   jax.ShapeDtypeStruct((t, 128), F32)],
        compiler_params=_params(("parallel", "arbitrary")),
    )(qf, kvf, kr, dcat, o32, lse)


def _ln1_fwd(x, mix, g, b, tm):
    t = x.shape[0]
    tm = min(tm, t)

    def body(x_ref, mix_ref, g_ref, b_ref, r_ref, h_ref, hb_ref, xb_ref):
        r = ALPHA * x_ref[...] + mix_ref[...]
        r_ref[...] = r
        h = _layernorm(r, g_ref[...], b_ref[...])
        h_ref[...] = h
        hb_ref[...] = h.astype(BF16)
        xb_ref[...] = x_ref[...].astype(BF16)

    rows = pl.BlockSpec((tm, D_MODEL), lambda i: (i, 0))
    vec = pl.BlockSpec((1, D_MODEL), lambda i: (0, 0))
    return pl.pallas_call(
        body, name="ln1_fwd", grid=(t // tm,), in_specs=[rows, rows, vec, vec], out_specs=[rows] * 4,
        out_shape=[jax.ShapeDtypeStruct(x.shape, F32)] * 2 + [jax.ShapeDtypeStruct(x.shape, BF16)] * 2,
        compiler_params=_params(("parallel",)),
    )(x, mix, g, b)


def _ln1_bwd(r1, dr2, da, db_, g, b, tm):
    t = r1.shape[0]
    tm = min(tm, t)

    def body(r_ref, d2_ref, da_ref, db_ref, g_ref, b_ref, dr_ref, drb_ref, dg_ref, dbias_ref):
        dh = ALPHA * d2_ref[...] + da_ref[...] + db_ref[...]
        _, vjp = jax.vjp(_layernorm, r_ref[...], g_ref[...], b_ref[...])
        dr, dg, dbias = vjp(dh)
        dr_ref[...] = dr
        drb_ref[...] = dr.astype(BF16)

        @pl.when(pl.program_id(0) == 0)
        def _():
            dg_ref[...] = jnp.zeros_like(dg_ref)
            dbias_ref[...] = jnp.zeros_like(dbias_ref)

        dg_ref[...] += dg
        dbias_ref[...] += dbias

    rows = pl.BlockSpec((tm, D_MODEL), lambda i: (i, 0))
    vec = pl.BlockSpec((1, D_MODEL), lambda i: (0, 0))
    return pl.pallas_call(
        body, name="ln1_bwd", grid=(t // tm,), in_specs=[rows] * 4 + [vec, vec], out_specs=[rows, rows, vec, vec],
        out_shape=[jax.ShapeDtypeStruct(r1.shape, F32), jax.ShapeDtypeStruct(r1.shape, BF16)]
        + [jax.ShapeDtypeStruct((1, D_MODEL), F32)] * 2,
        compiler_params=_params(("arbitrary",)),
    )(r1, dr2, da, db_, g, b)


def _ffn_act_fwd(u, conv_w, conv_b, bl, s, cb):
    nj = D_FF // cb

    def body(ug_ref, uu_ref, wg_ref, wu_ref, bg_ref, bu_ref, act_ref):
        def store(rows, act):
            act_ref[rows, :] = act.astype(BF16)

        _rows_apply(_ffn_act, [lambda r: ug_ref[r, :], lambda r: uu_ref[r, :]],
                    [wg_ref[...], wu_ref[...], bg_ref[...], bu_ref[...]], store, s, cb, CONV_HALO)

    return pl.pallas_call(
        body, name="ffn_act_fwd", grid=(bl, nj),
        in_specs=[
            pl.BlockSpec((s, cb), lambda b, j: (b, j)), pl.BlockSpec((s, cb), lambda b, j: (b, nj + j)),
            pl.BlockSpec((FFN_CONV, cb), lambda b, j: (0, j)), pl.BlockSpec((FFN_CONV, cb), lambda b, j: (0, nj + j)),
            pl.BlockSpec((1, cb), lambda b, j: (0, j)), pl.BlockSpec((1, cb), lambda b, j: (0, nj + j)),
        ],
        out_specs=pl.BlockSpec((s, cb), lambda b, j: (b, j)),
        out_shape=jax.ShapeDtypeStruct((bl * s, D_FF), BF16),
        compiler_params=_params(("parallel", "parallel")),
    )(u, u, conv_w, conv_w, conv_b, conv_b)


def _ffn_act_bwd(u, conv_w, conv_b, dact, bl, s, cb):
    nj = D_FF // cb

    def body(ug_ref, uu_ref, wg_ref, wu_ref, bg_ref, bu_ref, da_ref, du_ref, dwg_ref, dwu_ref, dbg_ref, dbu_ref, acc):
        def store_into(half):
            def store(rows, value, add):
                if add:
                    acc[half, rows, :] += value
                else:
                    acc[half, rows, :] = value
            return store

        dwg_ref[...], dwu_ref[...], dbg_ref[...], dbu_ref[...] = _rows_vjp(
            _ffn_act, [lambda r: ug_ref[r, :], lambda r: uu_ref[r, :]], [wg_ref[...], wu_ref[...], bg_ref[...], bu_ref[...]],
            lambda r: da_ref[r, :], [store_into(0), store_into(1)], s, cb, CONV_HALO)
        du_ref[...] = acc[...].astype(BF16)

    t = bl * s
    blk = pl.BlockSpec((s, cb), lambda b, j: (b, j))
    wpart = pl.BlockSpec((None, FFN_CONV, cb), lambda b, j: (b, 0, j))
    bpart = pl.BlockSpec((None, 1, cb), lambda b, j: (b, 0, j))
    return pl.pallas_call(
        body, name="ffn_act_bwd", grid=(bl, nj),
        in_specs=[
            blk, pl.BlockSpec((s, cb), lambda b, j: (b, nj + j)),
            pl.BlockSpec((FFN_CONV, cb), lambda b, j: (0, j)), pl.BlockSpec((FFN_CONV, cb), lambda b, j: (0, nj + j)),
            pl.BlockSpec((1, cb), lambda b, j: (0, j)), pl.BlockSpec((1, cb), lambda b, j: (0, nj + j)),
            blk,
        ],
        out_specs=[pl.BlockSpec((2, s, cb), lambda b, j: (0, b, j)), wpart, wpart, bpart, bpart],
        out_shape=[jax.ShapeDtypeStruct((2, t, D_FF), BF16)] + [jax.ShapeDtypeStruct((bl, FFN_CONV, D_FF), F32)] * 2
        + [jax.ShapeDtypeStruct((bl, 1, D_FF), F32)] * 2,
        scratch_shapes=[pltpu.VMEM((2, s, cb), F32)],
        compiler_params=_params(("parallel", "parallel")),
    )(u, u, conv_w, conv_w, conv_b, conv_b, dact)


def _head(h1, ffn, gpre, pp, bgate, g2, b2, target, tm):
    t = h1.shape[0]
    tm = min(tm, t)

    def body(h1_ref, ffn_ref, gp_ref, pp_ref, bg_ref, g2_ref, b2_ref, tg_ref,
             dr_ref, drb_ref, dgp_ref, dpp_ref, loss_ref, dbg_ref, dg2_ref, db2_ref):
        fn = functools.partial(_head_loss, target=tg_ref[...])
        loss, vjp = jax.vjp(fn, h1_ref[...], ffn_ref[...], gp_ref[...], pp_ref[...], bg_ref[...], g2_ref[...], b2_ref[...])
        _, dffn, dgp, dpp, dbg, dg2, db2 = vjp(jnp.ones((1, 1), F32))
        dr_ref[...] = dffn
        drb_ref[...], dgp_ref[...], dpp_ref[...] = dffn.astype(BF16), dgp.astype(BF16), dpp.astype(BF16)

        @pl.when(pl.program_id(0) == 0)
        def _():
            loss_ref[...] = jnp.zeros_like(loss_ref)
            dbg_ref[...] = jnp.zeros_like(dbg_ref)
            dg2_ref[...] = jnp.zeros_like(dg2_ref)
            db2_ref[...] = jnp.zeros_like(db2_ref)

        loss_ref[...] += jnp.broadcast_to(loss, loss_ref.shape)
        dbg_ref[...] += dbg
        dg2_ref[...] += dg2
        db2_ref[...] += db2

    rows = pl.BlockSpec((tm, D_MODEL), lambda i: (i, 0))
    vec = pl.BlockSpec((1, D_MODEL), lambda i: (0, 0))
    return pl.pallas_call(
        body, name="head", grid=(t // tm,), in_specs=[rows] * 4 + [vec] * 3 + [rows],
        out_specs=[rows] * 4 + [pl.BlockSpec((8, 128), lambda i: (0, 0))] + [vec] * 3,
        out_shape=[jax.ShapeDtypeStruct(h1.shape, F32)] + [jax.ShapeDtypeStruct(h1.shape, BF16)] * 3
        + [jax.ShapeDtypeStruct((8, 128), F32)]
        + [jax.ShapeDtypeStruct((1, D_MODEL), F32)] * 3,
        compiler_params=_params(("arbitrary",)),
    )(h1, ffn, gpre, pp, bgate, g2, b2, target)


def _adam_update(g, w_ref, m_ref, v_ref, g_ref, d_ref, nm_ref, nv_ref):
    m2 = ADAM_B1 * m_ref[...] + (1.0 - ADAM_B1) * g
    v2 = ADAM_B2 * v_ref[...] + (1.0 - ADAM_B2) * jnp.square(g)
    m_hat = m2 / (1.0 - ADAM_B1 ** ADAM_STEP)
    v_hat = v2 / (1.0 - ADAM_B2 ** ADAM_STEP)
    g_ref[...] = g
    d_ref[...] = -ADAM_LR * (m_hat / (jnp.sqrt(v_hat) + ADAM_EPS) + ADAM_WD * w_ref[...])
    nm_ref[...] = m2
    nv_ref[...] = v2


def _row_tile(rows, cols, limit_bytes=512 * 1024):
    best = None
    for t in range(HALF_ROWS_QUANTUM, rows + 1, HALF_ROWS_QUANTUM):
        if rows % t == 0 and t * cols * 4 <= limit_bytes:
            best = t
    return best or rows


def _adamw_reduced(recv, w, m, v, name):
    a, b = w.shape
    ta = _row_tile(a, b)

    def body(recv_ref, w_ref, m_ref, v_ref, g_ref, d_ref, nm_ref, nv_ref):
        c = lax.axis_index("c")
        for core in range(2):
            @pl.when(c == core)
            def _():
                got = [recv_ref[k].astype(F32) for k in range(N_DEV)]
                same = [got[7], got[0], got[1], got[2]]
                other = got[3:7]
                core0, core1 = (same, other) if core == 0 else (other, same)
                g = core0[0] + core1[0]
                for r in range(1, N_CHIPS):
                    g = (g + core0[r]) + core1[r]
                _adam_update(g, w_ref, m_ref, v_ref, g_ref, d_ref, nm_ref, nv_ref)

    blk = pl.BlockSpec((ta, b), lambda i: (i, 0))
    return pl.pallas_call(
        body, name=name, grid=(a // ta,),
        in_specs=[pl.BlockSpec((N_DEV, ta, b), lambda i: (0, i, 0)), blk, blk, blk], out_specs=[blk] * 4,
        out_shape=[jax.ShapeDtypeStruct(w.shape, F32)] * 4, compiler_params=_params(("parallel",)),
    )(recv, w, m, v)


def _adamw_small(g, w, m, v):
    def body(g_in, w_ref, m_ref, v_ref, g_ref, d_ref, nm_ref, nv_ref):
        _adam_update(g_in[...], w_ref, m_ref, v_ref, g_ref, d_ref, nm_ref, nv_ref)

    blk = pl.BlockSpec(w.shape, lambda i: (0, 0))
    return pl.pallas_call(
        body, name="adamw_small", grid=(1,), in_specs=[blk] * 4, out_specs=[blk] * 4,
        out_shape=[jax.ShapeDtypeStruct(w.shape, F32)] * 4, compiler_params=_params(("arbitrary",)),
    )(g, w, m, v)


def _remote(src, dst, send_sem, recv_sem, device):
    return pltpu.make_async_remote_copy(src_ref=src, dst_ref=dst, send_sem=send_sem, recv_sem=recv_sem,
                                        device_id=device, device_id_type=MESH)


def _place():
    x, y, c = lax.axis_index("x"), lax.axis_index("y"), lax.axis_index("c")
    return x, y, c, 2 * x + y, [(1 - x, y), (x, 1 - y), (1 - x, 1 - y)]


HBM_REF = pl.BlockSpec(memory_space=pl.ANY)
HALF_ROWS_QUANTUM = 16


def _gather_sems(n):
    return [pltpu.SemaphoreType.DMA((3 * n,))] * 4 + [pltpu.SemaphoreType.DMA((n,))]


def _gather_copies(ins, outs, sems):
    send_s, recv_s, fsend_s, frecv_s, local_s = sems
    x, y, c, me, chips = _place()
    local, sends, steps = [], [], []
    for i, (src, dst) in enumerate(zip(ins, outs)):
        local.append(pltpu.make_async_copy(src, dst.at[me], local_s.at[i]))
        half = src.shape[0] // 2
        split = src.shape[0] % (2 * HALF_ROWS_QUANTUM) == 0
        if split:
            mine = pl.ds(pl.multiple_of(c * half, HALF_ROWS_QUANTUM), half)
            theirs = pl.ds(pl.multiple_of((1 - c) * half, HALF_ROWS_QUANTUM), half)
        for r, (px, py) in enumerate(chips):
            k, peer = 3 * i + r, 2 * px + py
            if split:
                sends.append(_remote(src.at[mine], dst.at[me, mine], send_s.at[k], recv_s.at[k], (px, py, c)))
                landed = dst.at[peer, mine]
                steps.append((_remote(src.at[mine], landed, send_s.at[k], recv_s.at[k], (px, py, c)),
                              _remote(landed, landed, fsend_s.at[k], frecv_s.at[k], (x, y, 1 - c)),
                              _remote(dst.at[peer, theirs], dst.at[peer, theirs], fsend_s.at[k], frecv_s.at[k], (x, y, 1 - c))))
            else:
                sends.append(_remote(src, dst.at[me], send_s.at[k], recv_s.at[k], (px, py, c)))
                steps.append((_remote(src, dst.at[peer], send_s.at[k], recv_s.at[k], (px, py, c)), None, None))
    return local, sends, steps


def _scatter_sems(n):
    return [pltpu.SemaphoreType.DMA((4 * n,))] * 2 + [pltpu.SemaphoreType.DMA((3 * n,))] * 2 + [pltpu.SemaphoreType.DMA((n,))]


def _scatter_copies(ins, outs, sems):
    send_s, recv_s, fsend_s, frecv_s, local_s = sems
    x, y, c, me, chips = _place()
    local, sends, steps = [], [], []
    for i, (src, dst) in enumerate(zip(ins, outs)):
        local.append(pltpu.make_async_copy(src.at[me], dst.at[N_DEV - 1], local_s.at[i]))
        for r, (px, py) in enumerate(chips):
            k = 4 * i + r
            cp = _remote(src.at[2 * px + py], dst.at[r], send_s.at[k], recv_s.at[k], (px, py, c))
            fwd = _remote(dst.at[r], dst.at[4 + r], fsend_s.at[3 * i + r], frecv_s.at[3 * i + r], (x, y, 1 - c))
            sends.append(cp)
            steps.append((cp, fwd, fwd))
        k = 4 * i + 3
        cp = _remote(src.at[me], dst.at[3], send_s.at[k], recv_s.at[k], (x, y, 1 - c))
        sends.append(cp)
        steps.append((cp, None, None))
    return local, sends, steps


def _exchange_start(plan):
    local, sends, _ = plan
    for cp in local + sends:
        cp.start()


def _exchange_pass_on(plan):
    for arrival, pass_on, _ in plan[2]:
        arrival.wait_recv()
        if pass_on is not None:
            pass_on.start()


def _exchange_finish(plan):
    local, sends, steps = plan
    for _, pass_on, passed in steps:
        if pass_on is not None:
            passed.wait_recv()
    for cp in sends:
        cp.wait_send()
    for _, pass_on, _ in steps:
        if pass_on is not None:
            pass_on.wait_send()
    for cp in local:
        cp.wait()


def _exchange_call(arrays, copies, sems, out_shapes, name):
    n = len(arrays)

    def body(*refs):
        plan = copies(refs[:n], refs[n:2 * n], refs[2 * n:])
        _exchange_start(plan)
        _exchange_pass_on(plan)
        _exchange_finish(plan)

    return pl.pallas_call(
        body, name=name, in_specs=[HBM_REF] * n, out_specs=[HBM_REF] * n, out_shape=out_shapes,
        scratch_shapes=sems(n), compiler_params=pltpu.CompilerParams(has_side_effects=True),
    )(*arrays)


def _gather_call(shards, name):
    shapes = [jax.ShapeDtypeStruct((N_CHIPS,) + a.shape, a.dtype) for a in shards]
    return _exchange_call(shards, _gather_copies, _gather_sems, shapes, name)


def _all_reduce_small(a):
    def body(in_ref, out_ref, slots, send_sems, recv_sems):
        x, y, c = lax.axis_index("x"), lax.axis_index("y"), lax.axis_index("c")
        me = 4 * x + 2 * y + c
        slots[0] = in_ref[...]
        sends = []
        for r in range(1, N_DEV):
            peer = (x ^ (r >> 2), y ^ ((r >> 1) & 1), c ^ (r & 1))
            sends.append(pltpu.make_async_remote_copy(src_ref=in_ref, dst_ref=slots.at[r], send_sem=send_sems.at[r],
                                                      recv_sem=recv_sems.at[r], device_id=peer, device_id_type=MESH))
        for cp in sends:
            cp.start()
        for cp in sends:
            cp.wait_recv()
        acc = slots[me]
        for dev in range(1, N_DEV):
            acc = acc + slots[dev ^ me]
        out_ref[...] = acc
        for cp in sends:
            cp.wait_send()

    return pl.pallas_call(
        body, name="small_all_reduce",
        in_specs=[pl.BlockSpec(memory_space=pltpu.VMEM)], out_specs=pl.BlockSpec(memory_space=pltpu.VMEM),
        out_shape=jax.ShapeDtypeStruct(a.shape, a.dtype),
        scratch_shapes=[pltpu.VMEM((N_DEV,) + a.shape, a.dtype), pltpu.SemaphoreType.DMA((N_DEV,)),
                        pltpu.SemaphoreType.DMA((N_DEV,))],
        compiler_params=pltpu.CompilerParams(has_side_effects=True),
    )(a)


SHARDED = ["w_in", "mla_w_q_up", "mla_w_kv_up", "w_out", "ffn_w_up", "ffn_w_down", "ple_w_gate", "ple_w_proj",
           "gdn_conv_w", "ffn_conv_w"]
SHARD_AXIS = {"w_in": 1, "mla_w_q_up": 1, "mla_w_kv_up": 1, "w_out": 0, "ffn_w_up": 1, "ffn_w_down": 0,
              "ple_w_gate": 0, "ple_w_proj": 1, "gdn_conv_w": 1, "ffn_conv_w": 1}
SMALL = ["gdn_a_log", "gdn_dt_bias", "gdn_norm_g", "mla_q_norm_g", "mla_kv_norm_g", "ln1_g", "ln1_b", "ffn_conv_b",
         "ple_b_gate", "ln2_g", "ln2_b"]
WEIGHTS = ["w_in", "gdn_conv_w", "gdn_a_log", "gdn_dt_bias", "gdn_norm_g", "mla_q_norm_g", "mla_w_q_up", "mla_kv_norm_g",
           "mla_w_kv_up", "w_out", "ln1_g", "ln1_b", "ffn_w_up", "ffn_conv_w", "ffn_conv_b", "ffn_w_down", "ple_w_gate",
           "ple_b_gate", "ple_w_proj", "ln2_g", "ln2_b"]
F32_ON_WIRE = ("gdn_conv_w", "ffn_conv_w")
GATHER_EARLY = ["w_in", "gdn_conv_w"]
GATHER_LATE = ["mla_w_q_up", "mla_w_kv_up", "w_out", "ffn_w_up", "ffn_conv_w", "ffn_w_down", "ple_w_gate", "ple_w_proj"]
SCATTER_EARLY = ["ffn_w_up", "ffn_conv_w", "ffn_w_down", "ple_w_gate", "ple_w_proj", "w_out"]
SCATTER_LATE = ["w_in", "gdn_conv_w", "mla_w_q_up", "mla_w_kv_up"]
PACK_COLS = 1024
PACK_ROW_TILE = 8


def _join_blocks(blocks, axis):
    n, a, b = blocks.shape
    if axis == 0:
        return blocks.reshape(n * a, b)
    return jnp.transpose(blocks, (1, 0, 2)).reshape(a, n * b)


def _split_blocks(full, axis):
    if axis == 0:
        return full.reshape(N_CHIPS, full.shape[0] // N_CHIPS, full.shape[1])
    a, nb = full.shape
    return jnp.transpose(full.reshape(a, N_CHIPS, nb // N_CHIPS), (1, 0, 2))


def _pack(arrays):
    flat = jnp.concatenate([a.reshape(-1) for a in arrays])
    quantum = PACK_COLS * PACK_ROW_TILE
    padded = -(-flat.shape[0] // quantum) * quantum
    return jnp.pad(flat, (0, padded - flat.shape[0])).reshape(-1, PACK_COLS)


def _unpack(packed, shapes):
    flat = packed.reshape(-1)
    out, off = [], 0
    for shp in shapes:
        n = int(np.prod(shp))
        out.append(flat[off:off + n].reshape(shp))
        off += n
    return out


def kernel(x, p, w_in, gdn_conv_w, gdn_a_log, gdn_dt_bias, gdn_norm_g, mla_q_norm_g, mla_w_q_up, mla_kv_norm_g, mla_w_kv_up, w_out, ln1_g, ln1_b, ffn_w_up, ffn_conv_w, ffn_conv_b, ffn_w_down, ple_w_gate, ple_b_gate, ple_w_proj, ln2_g, ln2_b, loss_target, m_w_in, m_gdn_conv_w, m_gdn_a_log, m_gdn_dt_bias, m_gdn_norm_g, m_mla_q_norm_g, m_mla_w_q_up, m_mla_kv_norm_g, m_mla_w_kv_up, m_w_out, m_ln1_g, m_ln1_b, m_ffn_w_up, m_ffn_conv_w, m_ffn_conv_b, m_ffn_w_down, m_ple_w_gate, m_ple_b_gate, m_ple_w_proj, m_ln2_g, m_ln2_b, v_w_in, v_gdn_conv_w, v_gdn_a_log, v_gdn_dt_bias, v_gdn_norm_g, v_mla_q_norm_g, v_mla_w_q_up, v_mla_kv_norm_g, v_mla_w_kv_up, v_w_out, v_ln1_g, v_ln1_b, v_ffn_w_up, v_ffn_conv_w, v_ffn_conv_b, v_ffn_w_down, v_ple_w_gate, v_ple_b_gate, v_ple_w_proj, v_ln2_g, v_ln2_b):
    given = dict(locals())
    wsh = {n: given[n][0] for n in WEIGHTS}
    msh = {n: given["m_" + n][0] for n in WEIGHTS}
    vsh = {n: given["v_" + n][0] for n in WEIGHTS}
    bl, s, _ = x.shape
    t = bl * s
    xt = x.reshape(t, D_MODEL)
    pt = p.reshape(t, PLE_DIM)
    target = loss_target.reshape(t, D_MODEL)

    wire = lambda n: wsh[n] if n in F32_ON_WIRE else wsh[n].astype(BF16)
    early = _gather_call([wire(n) for n in GATHER_EARLY], "weights_gather_early")
    full = {n: _join_blocks(g, SHARD_AXIS[n]) for n, g in zip(GATHER_EARLY, early)}
    late_shards = [wire(n) for n in GATHER_LATE]
    late_ride = (late_shards, _gather_copies, _gather_sems,
                 [jax.ShapeDtypeStruct((N_CHIPS,) + a.shape, a.dtype) for a in late_shards], 0.75)

    in_cols, q_cols = _w_in_cols(), _w_q_cols()
    w_in_p = _pad_cols(full["w_in"], in_cols)
    gconv = full["gdn_conv_w"]
    row = lambda a: a.reshape(1, -1)
    sc = jnp.zeros((8, 128), F32).at[0, :GDN_HEADS].set(wsh["gdn_a_log"]).at[1, :GDN_HEADS].set(wsh["gdn_dt_bias"])
    norm_g, qg, kvg = row(wsh["gdn_norm_g"]), row(wsh["mla_q_norm_g"]), row(wsh["mla_kv_norm_g"])
    g1, b1, g2, b2 = row(wsh["ln1_g"]), row(wsh["ln1_b"]), row(wsh["ln2_g"]), row(wsh["ln2_b"])
    fbias, bgate = row(wsh["ffn_conv_b"]), row(wsh["ple_b_gate"])

    inv = ROPE_THETA ** (-jnp.arange(0, MLA_ROPE, 2, dtype=F32) / MLA_ROPE)
    ang = jnp.arange(s, dtype=F32)[:, None] * inv[None, :]
    zero = jnp.zeros_like(ang)
    cos_t = jnp.concatenate([jnp.cos(ang), zero, jnp.cos(ang), zero], axis=1)
    sin_t = jnp.concatenate([-jnp.sin(ang), zero, jnp.sin(ang), zero], axis=1)

    proj = _matmul(xt, w_in_p, name="proj", tm=1024)
    cat, o_raw, states, qkvg, late = _gdn_fwd(proj, gconv, sc, norm_g, bl, s, late_ride)
    full.update({n: _join_blocks(g, SHARD_AXIS[n]) for n, g in zip(GATHER_LATE, late)})
    w_o, w_up, w_down = full["w_out"], full["ffn_w_up"], full["ffn_w_down"]
    w_gate, w_proj, fconv = full["ple_w_gate"], full["ple_w_proj"], full["ffn_conv_w"]
    w_q_p, w_kv = _pad_cols(full["mla_w_q_up"], q_cols), full["mla_w_kv_up"]
    qf, kvf, kr = _mla_prep_fwd(proj, qg, kvg, w_q_p, w_kv, cos_t, sin_t, s, 256)
    cat, attn_o32, attn_lse = _attn_fwd(qf, kvf, kr, cat, bl, s, 512)
    wide = dict(tm=1024, tn=1024)
    mix = _matmul(cat, w_o, name="mix", **wide)
    r1, h1, h1b, xb = _ln1_fwd(xt, mix, g1, b1, 256)
    u = _matmul(h1b, w_up, name="ffn_up", tm=1024, tn=1408)
    act = _ffn_act_fwd(u, fconv, fbias, bl, s, 256)
    ffn = _matmul(act, w_down, name="ffn_down", tk=1408, **wide)
    gpre = _matmul(h1b, w_gate, name="ple_gate", **wide)
    pp = _matmul(pt, w_proj, name="ple_proj", **wide)
    dr2, dr2b, dgpre, dpp, loss_acc, dbgate, dg2, db2 = _head(h1, ffn, gpre, pp, bgate, g2, b2, target, 256)

    dact = _matmul(dr2b, w_down, name="d_act", tb=True, tm=1024, tn=1408)
    long_k = dict(ta=True, tk=2048)
    d_w_down = _matmul(act, dr2b, name="dw_down", tm=1408, tn=1024, **long_k)
    du, dfcw_g, dfcw_u, dfcb_g, dfcb_u = _ffn_act_bwd(u, fconv, fbias, dact, bl, s, 256)
    dh1_a = _matmul(du, w_up, name="dh1_ffn", tb=True, tk=1408, a_halves=True, **wide)
    dh1_b = _matmul(dgpre, w_gate, name="dh1_ple", tb=True, **wide)
    d_w_up = _matmul(h1b, du, name="dw_up", tn=1408, b_halves=True, **long_k)
    d_w_gate = _matmul(h1b, dgpre, name="dw_gate", **long_k, **wide)
    d_w_proj = _matmul(pt, dpp, name="dw_proj", ta=True, tn=1024)
    dr1, dr1b, dg1, db1 = _ln1_bwd(r1, dr2, dh1_a, dh1_b, g1, b1, 256)
    dcat = _matmul(dr1b, w_o, name="d_cat", tb=True, **wide)
    d_w_o = _matmul(cat, dr1b, name="dw_out", **long_k, **wide)

    gfull = {
        "ffn_w_up": d_w_up, "ffn_w_down": d_w_down, "ple_w_gate": d_w_gate, "ple_w_proj": d_w_proj, "w_out": d_w_o,
        "ffn_conv_w": jnp.concatenate([jnp.sum(dfcw_g, 0), jnp.sum(dfcw_u, 0)], axis=1),
    }
    slabs = {n: _split_blocks(gfull[n], SHARD_AXIS[n]).astype(BF16) for n in SCATTER_EARLY}
    early_slabs = [slabs[n] for n in SCATTER_EARLY]
    early_ride = (early_slabs, _scatter_copies, _scatter_sems,
                  [jax.ShapeDtypeStruct((N_DEV,) + a.shape[1:], a.dtype) for a in early_slabs], 0.7)
    dproj, dab, dcwq, dcwk, dcwv, dsc, dng, early_recv = _gdn_bwd(proj, gconv, sc, norm_g, o_raw, states, qkvg, dcat, bl, s,
                                                                  early_ride)
    received = dict(zip(SCATTER_EARLY, early_recv))
    dqf, dkvf, dkr = _attn_bwd(qf, kvf, kr, dcat, attn_o32, attn_lse, bl, s, 256)
    dproj, dqg, dkvg, d_w_q_p, d_w_kv = _mla_prep_bwd(proj, qg, kvg, w_q_p, w_kv, cos_t, sin_t, dqf, dkvf, dkr, dab, dproj, s, 256)
    d_w_in_p = _matmul(xb, dproj, name="dw_in", **long_k, **wide)

    gfull.update({
        "w_in": _unpad_cols(d_w_in_p, in_cols, D_IN),
        "mla_w_q_up": _unpad_cols(d_w_q_p, q_cols, MLA_HEADS * (MLA_NOPE + MLA_ROPE)),
        "mla_w_kv_up": d_w_kv,
        "gdn_conv_w": jnp.concatenate([jnp.sum(dcwq, 0), jnp.sum(dcwk, 0), jnp.sum(dcwv, 0)], axis=1),
    })
    slabs.update({n: _split_blocks(gfull[n], SHARD_AXIS[n]).astype(BF16) for n in SCATTER_LATE})
    late_slabs = [slabs[n] for n in SCATTER_LATE]
    late_scatter = (late_slabs, _scatter_copies, _scatter_sems,
                    [jax.ShapeDtypeStruct((N_DEV,) + a.shape[1:], a.dtype) for a in late_slabs], 0.85)
    grad_x, late_recv = _matmul(dproj, w_in_p, name="d_x", tb=True, add=dr1, add_scale=ALPHA, ride=late_scatter, **wide)
    received.update(zip(SCATTER_LATE, late_recv))
    dsc_sum = jnp.sum(dsc, axis=(0, 1))
    gsmall = {
        "gdn_a_log": dsc_sum[0, :GDN_HEADS], "gdn_dt_bias": dsc_sum[1, :GDN_HEADS],
        "gdn_norm_g": jnp.sum(dng[:, :, 0, :], axis=(0, 1)),
        "mla_q_norm_g": dqg[0], "mla_kv_norm_g": dkvg[0], "ln1_g": dg1[0], "ln1_b": db1[0],
        "ffn_conv_b": jnp.concatenate([jnp.sum(dfcb_g, 0), jnp.sum(dfcb_u, 0)], axis=1)[0],
        "ple_b_gate": dbgate[0], "ln2_g": dg2[0], "ln2_b": db2[0],
    }

    big = [{}, {}, {}, {}]
    for n in SHARDED:
        for kind, val in enumerate(_adamw_reduced(received[n], wsh[n], msh[n], vsh[n], "adamw_" + n)):
            big[kind][n] = val

    small_shapes = [wsh[n].shape for n in SMALL]
    gsum = _all_reduce_small(_pack([gsmall[n] for n in SMALL]))
    spacks = _adamw_small(gsum, _pack([wsh[n] for n in SMALL]), _pack([msh[n] for n in SMALL]), _pack([vsh[n] for n in SMALL]))
    small = [dict(zip(SMALL, _unpack(pk, small_shapes))) for pk in spacks]

    loss = lax.psum(loss_acc[0, 0], ("x", "y", "c"))
    outs = [loss, grad_x.reshape(x.shape)]
    for kind in range(4):
        for n in WEIGHTS:
            val = big[kind][n] if n in big[kind] else small[kind][n]
            outs.append(val[None])
    return tuple(outs)
```

```python
import functools
import math

import numpy as np
import jax
import jax.numpy as jnp
from jax import lax
from jax.experimental import pallas as pl
from jax.experimental.pallas import tpu as pltpu

F32 = jnp.float32
BF16 = jnp.bfloat16

D_MODEL = 1024
CHUNK = 64
PLE_DIM = 256
GDN_HEADS = 4
GDN_DK = 128
GDN_DV = 128
GDN_CONV = 4
MLA_HEADS = 4
MLA_NOPE = 128
MLA_ROPE = 64
MLA_V = 128
MLA_Q_LORA = 384
MLA_KV_LORA = 256
ROPE_THETA = 10000.0
D_FF = 2816
FFN_CONV = 3
DEPTH = 1
ALPHA = (2.0 * DEPTH) ** 0.25
NORM_EPS = 1e-6
GDN_QK = GDN_HEADS * GDN_DK
GDN_VW = GDN_HEADS * GDN_DV
D_IN = 2 * GDN_QK + 2 * GDN_VW + 2 * GDN_HEADS + MLA_Q_LORA + MLA_KV_LORA + MLA_ROPE
ATT_SCALE = (MLA_NOPE + MLA_ROPE) ** -0.5

ADAM_LR = 0.001
ADAM_B1 = 0.9
ADAM_B2 = 0.999
ADAM_EPS = 1e-08
ADAM_WD = 0.01
ADAM_STEP = 10

LANES = 128
VMEM_LIMIT = 60 * 1024 * 1024
GDN_FWD_GROUP = 16
GDN_BWD_GROUP = 16
N_CHIPS = 4
N_DEV = 8

P_WIDTH = 3072
P_MLA = 2048
MESH = pl.DeviceIdType.MESH


def _rope_slot(j):
    return j if j < MLA_ROPE // 2 else 64 + (j - MLA_ROPE // 2)


def _w_in_cols():
    idx = -np.ones((P_WIDTH,), np.int64)
    for h in range(GDN_HEADS):
        base = h * 512
        idx[base:base + 128] = np.arange(128) + h * GDN_DK
        idx[base + 128:base + 256] = np.arange(128) + GDN_QK + h * GDN_DK
        idx[base + 256:base + 384] = np.arange(128) + 2 * GDN_QK + h * GDN_DV
        idx[base + 384:base + 512] = np.arange(128) + 2 * GDN_QK + GDN_VW + h * GDN_DV
    o_a = 2 * GDN_QK + 2 * GDN_VW
    idx[P_MLA:P_MLA + 2 * GDN_HEADS] = np.arange(2 * GDN_HEADS) + o_a
    o_cq = o_a + 2 * GDN_HEADS
    idx[P_MLA + 128:P_MLA + 512] = np.arange(MLA_Q_LORA) + o_cq
    o_ckv = o_cq + MLA_Q_LORA
    idx[P_MLA + 512:P_MLA + 768] = np.arange(MLA_KV_LORA) + o_ckv
    o_kr = o_ckv + MLA_KV_LORA
    for j in range(MLA_ROPE):
        idx[P_MLA + 768 + _rope_slot(j)] = o_kr + j
    return idx


def _w_q_cols():
    idx = -np.ones((MLA_HEADS * 256,), np.int64)
    for h in range(MLA_HEADS):
        o = h * (MLA_NOPE + MLA_ROPE)
        idx[h * 256:h * 256 + 128] = np.arange(128) + o
        for j in range(MLA_ROPE):
            idx[h * 256 + 128 + _rope_slot(j)] = o + MLA_NOPE + j
    return idx


def _pad_cols(w, idx):
    safe = np.where(idx >= 0, idx, 0)
    return jnp.where(jnp.asarray(idx >= 0)[None, :], w[:, safe], 0.0)


def _unpad_cols(wp, idx, n):
    inv = np.zeros((n,), np.int64)
    inv[idx[idx >= 0]] = np.nonzero(idx >= 0)[0]
    return wp[:, inv]


def _dot(a, b, ca, cb, precision=None):
    if precision is None:
        a = a.astype(BF16)
        b = b.astype(BF16)
    return lax.dot_general(a, b, (((ca,), (cb,)), ((), ())), preferred_element_type=F32, precision=precision)


@jax.custom_vjp
def mm(a, b):
    return _dot(a, b, 1, 0)


@jax.custom_vjp
def mm_nt(a, b):
    return _dot(a, b, 1, 1)


@jax.custom_vjp
def mm_tn(a, b):
    return _dot(a, b, 0, 0)


mm.defvjp(lambda a, b: (mm(a, b), (a, b)), lambda r, g: (mm_nt(g, r[1]), mm_tn(r[0], g)))
mm_nt.defvjp(lambda a, b: (mm_nt(a, b), (a, b)), lambda r, g: (mm(g, r[1]), mm_tn(g, r[0])))
mm_tn.defvjp(lambda a, b: (mm_tn(a, b), (a, b)), lambda r, g: (mm_nt(r[1], g), mm(r[0], g)))

def _split(a):
    hi = a.astype(BF16)
    return hi, (a - hi.astype(F32)).astype(BF16)


def _dot3(a, b, ca, cb):
    a_hi, a_lo = _split(a)
    b_hi, b_lo = _split(b)
    return (_dot(a_hi, b_hi, ca, cb) + _dot(a_hi, b_lo, ca, cb)) + _dot(a_lo, b_hi, ca, cb)


def _shift_rows(x, s):
    return x if s == 0 else pltpu.roll(x, s % x.shape[0], 0)


def _row(w, j):
    tap = lax.broadcasted_iota(jnp.int32, w.shape, 0)
    return jnp.sum(jnp.where(tap == j, w, 0.0), axis=0, keepdims=True)


@jax.custom_vjp
def dwconv(x, w):
    k = w.shape[0]
    y = _row(w, k - 1) * x
    for j in range(k - 1):
        y = y + _row(w, j) * _shift_rows(x, k - 1 - j)
    return y


def _dwconv_fwd(x, w):
    return dwconv(x, w), (x, w)


def _dwconv_bwd(res, dy):
    x, w = res
    k = w.shape[0]
    dx = _row(w, k - 1) * dy
    tap = lax.broadcasted_iota(jnp.int32, w.shape, 0)
    dw = jnp.where(tap == k - 1, jnp.sum(dy * x, axis=0, keepdims=True), 0.0)
    for j in range(k - 1):
        dx = dx + _row(w, j) * _shift_rows(dy, -(k - 1 - j))
        dw = dw + jnp.where(tap == j, jnp.sum(dy * _shift_rows(x, k - 1 - j), axis=0, keepdims=True), 0.0)
    return dx, dw


dwconv.defvjp(_dwconv_fwd, _dwconv_bwd)


@jax.custom_vjp
def rope128(x, cos, sin):
    return x * cos + pltpu.roll(x, 64, 1) * sin


rope128.defvjp(lambda x, c, s: (rope128(x, c, s), (c, s)),
               lambda r, g: (g * r[0] + pltpu.roll(g * r[1], 64, 1), jnp.zeros_like(r[0]), jnp.zeros_like(r[1])))


def _silu(x):
    return x * jax.nn.sigmoid(x)


def _softplus(x):
    return jnp.maximum(x, 0.0) + jnp.log(1.0 + jnp.exp(-jnp.abs(x)))


def _rmsnorm(x, g):
    return x * lax.rsqrt(jnp.mean(x * x, axis=-1, keepdims=True) + NORM_EPS) * g


def _layernorm(x, g, b):
    mu = jnp.mean(x, axis=-1, keepdims=True)
    xc = x - mu
    var = jnp.mean(xc * xc, axis=-1, keepdims=True)
    return xc * lax.rsqrt(var + NORM_EPS) * g + b


def _pick_lane(row, lane):
    idx = lax.broadcasted_iota(jnp.int32, row.shape, 1)
    return jnp.sum(jnp.where(idx == lane, row, 0.0), axis=1, keepdims=True)


def _gdn_q(pq, cw):
    h = _silu(dwconv(pq, cw))
    return h * lax.rsqrt(jnp.sum(h * h, axis=-1, keepdims=True) + NORM_EPS) * (GDN_DK ** -0.5)


def _gdn_k(pk, cw):
    h = _silu(dwconv(pk, cw))
    return h * lax.rsqrt(jnp.sum(h * h, axis=-1, keepdims=True) + NORM_EPS)


def _gdn_v(pv, cw):
    return _silu(dwconv(pv, cw))


def _gdn_gate(ab, sc, head):
    a = _pick_lane(ab, head)
    b = _pick_lane(ab, GDN_HEADS + head)
    a_log = _pick_lane(_row(sc, 0), head)
    dt_bias = _pick_lane(_row(sc, 1), head)
    beta = jax.nn.sigmoid(b)
    g = -jnp.exp(a_log) * _softplus(a + dt_bias)
    return _two_lanes(g, beta)


def _two_lanes(c0, c1):
    lane = lax.broadcasted_iota(jnp.int32, (c0.shape[0], LANES), 1)
    return jnp.where(lane == 0, c0, jnp.where(lane == 1, c1, 0.0))


def _run(levels):
    try:
        while True:
            next(levels)
    except StopIteration as stop:
        return stop.value


def _inverse_levels(lows):
    n = lows[0].shape[0]
    ii = lax.broadcasted_iota(jnp.int32, (n, n), 0)
    jj = lax.broadcasted_iota(jnp.int32, (n, n), 1)
    eye = jnp.where(ii == jj, 1.0, 0.0)
    invs = [eye - low for low in lows]
    powers = [_dot3(low, low, 1, 0) for low in lows]
    yield
    k = 2
    while k < n:
        invs = [inv + _dot3(inv, p, 1, 0) for inv, p in zip(invs, powers)]
        yield
        k *= 2
        if k < n:
            powers = [_dot3(p, p, 1, 0) for p in powers]
            yield
    return invs


def _inverse_group(lows):
    return _run(_inverse_levels(lows))


@jax.custom_vjp
def solve_group(lows, rhss):
    return [_dot3(inv, rhs, 1, 0) for inv, rhs in zip(_inverse_group(lows), rhss)]


def _solve_whole(lows, rhss):
    return solve_group(lows, rhss)
    yield


def _solve_levels(lows, rhss):
    invs = yield from _inverse_levels(lows)
    return [_dot3(inv, rhs, 1, 0) for inv, rhs in zip(invs, rhss)]


def _solve_group_fwd(lows, rhss):
    invs = _inverse_group(lows)
    xs = [_dot3(inv, rhs, 1, 0) for inv, rhs in zip(invs, rhss)]
    return xs, (invs, xs)


def _solve_group_bwd(res, dxs):
    invs, xs = res
    n = invs[0].shape[0]
    strict = lax.broadcasted_iota(jnp.int32, (n, n), 0) > lax.broadcasted_iota(jnp.int32, (n, n), 1)
    drhss = [_dot3(inv, dx, 0, 0) for inv, dx in zip(invs, dxs)]
    dlows = [jnp.where(strict, -_dot3(drhs, x, 1, 1), 0.0) for drhs, x in zip(drhss, xs)]
    return dlows, drhss


solve_group.defvjp(_solve_group_fwd, _solve_group_bwd)


def _gdn_local_group(qs, ks, vs, gbs):
    return _run(_gdn_local_levels(qs, ks, vs, gbs, _solve_whole))


def _gdn_local_levels(qs, ks, vs, gbs, solve):
    c = qs[0].shape[0]
    ii = lax.broadcasted_iota(jnp.int32, (c, c), 0)
    jj = lax.broadcasted_iota(jnp.int32, (c, c), 1)
    incl = ii >= jj
    gs = [_pick_lane(gb, 0) for gb in gbs]
    betas = [_pick_lane(gb, 1) for gb in gbs]
    g_rows = [jnp.sum(jnp.where(ii == jj, g, 0.0), axis=0, keepdims=True) for g in gs]
    gc_cols = [jnp.sum(jnp.where(incl, g_row, 0.0), axis=1, keepdims=True) for g_row in g_rows]
    gc_rows = [jnp.sum(jnp.where(jj >= ii, g, 0.0), axis=0, keepdims=True) for g in gs]
    decays = [jnp.where(incl, jnp.exp(jnp.where(incl, gc - gr, 0.0)), 0.0) for gc, gr in zip(gc_cols, gc_rows)]
    yield
    kbs = [k * beta for k, beta in zip(ks, betas)]
    lows = [jnp.where(ii > jj, mm_nt(kb, k) * decay, 0.0) for kb, k, decay in zip(kbs, ks, decays)]
    egs = [jnp.exp(gc) for gc in gc_cols]
    yield
    wus = yield from solve(lows, [jnp.concatenate([kb * eg, v * beta], axis=1) for kb, eg, v, beta in zip(kbs, egs, vs, betas)])
    yield
    qks = [mm_nt(q, k) * decay for q, k, decay in zip(qs, ks, decays)]
    g_lasts = [jnp.sum(g_row, axis=1, keepdims=True) for g_row in g_rows]
    kds = [k * jnp.exp(gl - gc) for k, gl, gc in zip(ks, g_lasts, gc_cols)]
    yield
    ws, us = [wu[:, :GDN_DK] for wu in wus], [wu[:, GDN_DK:] for wu in wus]
    q_effs = [q * eg - mm(qk, w) for q, eg, qk, w in zip(qs, egs, qks, ws)]
    yield
    o_locals = [mm(qk, u) for qk, u in zip(qks, us)]
    yield
    mixes = [mm_tn(kd, w) for kd, w in zip(kds, ws)]
    yield
    adds = [mm_tn(kd, u) for kd, u in zip(kds, us)]
    return [(q_eff, o_loc, mix, add, jnp.exp(gl))
            for q_eff, o_loc, mix, add, gl in zip(q_effs, o_locals, mixes, adds, g_lasts)]


def _gdn_state_step(q_eff, o_local, mix, add, eg_last, state):
    return mm(q_eff, state) + o_local, state * eg_last - mm(mix, state) + add


def _gdn_post(o, z, norm_g):
    return _rmsnorm(o, norm_g) * _silu(z)


MASKED = -1e30


def _scores(q, kn, kr, q0, k0):
    s = (mm_nt(q[:, :128], kn) + mm_nt(q[:, 128:], kr)) * ATT_SCALE
    if k0 + kn.shape[0] <= q0:
        return s
    qpos = q0 + lax.broadcasted_iota(jnp.int32, s.shape, 0)
    kpos = k0 + lax.broadcasted_iota(jnp.int32, s.shape, 1)
    shift = int(math.log2(CHUNK))
    return jnp.where((kpos >> shift) <= (qpos >> shift), s, MASKED)


def _softmax_times(s, v):
    top = jnp.max(s, axis=-1, keepdims=True)
    p = jnp.exp(s - top)
    norm = jnp.sum(p, axis=-1, keepdims=True)
    return mm(p / norm, v), top + jnp.log(norm)


def _mla_prep(pm, qg, kvg, wq, wkv, cos, sin):
    cq = pm[:, 128:512]
    ckv = pm[:, 512:768]
    qf = mm(_rmsnorm(cq, qg), wq)
    parts = []
    for h in range(MLA_HEADS):
        parts.append(qf[:, h * 256:h * 256 + 128])
        parts.append(rope128(qf[:, h * 256 + 128:h * 256 + 256], cos, sin))
    kvf = mm(_rmsnorm(ckv, kvg), wkv)
    return jnp.concatenate(parts, axis=1), kvf, rope128(pm[:, 768:896], cos, sin)


def _ffn_act(ug, uu, wg, wu, bg, bu):
    return _silu(dwconv(ug, wg) + bg) * (dwconv(uu, wu) + bu)


def _head_loss(h1, ffn, gpre, pp, bgate, g2, b2, target):
    gate = jax.nn.sigmoid(gpre + bgate)
    h2 = _layernorm(ALPHA * h1 + ffn + gate * pp, g2, b2)
    err = h2 - target
    return 0.5 * jnp.sum(jnp.sum(err * err, axis=1, keepdims=True), axis=0, keepdims=True) / D_MODEL


ROW_TILE_VREGS = 32
CONV_HALO = 8


def _rows_per_tile(n_rows, cols):
    tile = min(n_rows, ROW_TILE_VREGS * 8 * LANES // cols)
    assert n_rows % tile == 0 and tile % CONV_HALO == 0, (n_rows, cols)
    return tile


def _tile_inputs(loads, t0, first, halo, tile):
    if halo == 0:
        return [ld(pl.ds(t0, tile)) for ld in loads]
    if first:
        xs = [ld(pl.ds(0, tile)) for ld in loads]
        return [jnp.concatenate([jnp.zeros((halo, x.shape[1]), x.dtype), x], axis=0) for x in xs]
    return [ld(pl.ds(pl.multiple_of(t0 - halo, CONV_HALO), tile + halo)) for ld in loads]


def _rows_apply(fn, loads, consts, store, n_rows, cols, halo):
    tile = _rows_per_tile(n_rows, cols)

    def one(t0, first):
        y = fn(*_tile_inputs(loads, t0, first, halo, tile), *consts)
        store(pl.ds(t0, tile), y[halo:] if halo else y)

    one(0, True)

    def step(i, carry):
        one(pl.multiple_of(i * tile, tile), False)
        return carry

    lax.fori_loop(1, n_rows // tile, step, 0)


def _rows_vjp(fn, loads, consts, load_dy, stores, n_rows, cols, halo):
    tile = _rows_per_tile(n_rows, cols)

    def one(t0, first, dconsts):
        xs = _tile_inputs(loads, t0, first, halo, tile)
        _, vjp = jax.vjp(lambda *a: fn(*a)[halo:] if halo else fn(*a), *xs, *consts)
        grads = vjp(load_dy(pl.ds(t0, tile)))
        for st, dx in zip(stores, grads[:len(xs)]):
            st(pl.ds(t0, tile), dx[halo:] if halo else dx, False)
            if halo and not first:
                st(pl.ds(pl.multiple_of(t0 - halo, CONV_HALO), halo), dx[:halo], True)
        return tuple(a + b for a, b in zip(dconsts, grads[len(xs):]))

    dconsts = one(0, True, tuple(jnp.zeros_like(c) for c in consts))
    return lax.fori_loop(1, n_rows // tile, lambda i, dc: one(pl.multiple_of(i * tile, tile), False, dc), dconsts)


def _params(sem):
    return pltpu.CompilerParams(dimension_semantics=sem, vmem_limit_bytes=VMEM_LIMIT)


def _matmul(a, b, *, name, ta=False, tb=False, tm=512, tn=512, tk=1024, add=None, add_scale=1.0,
            a_halves=False, b_parts=0, out_parts=0, ride=None, out_dtype=F32):
    assert not (a_halves and ta)
    a_shape = (a.shape[1], 2 * a.shape[2]) if a_halves else a.shape
    b_shape = (b.shape[1], b_parts * b.shape[2]) if b_parts else b.shape
    (k_dim, m) = a_shape if ta else a_shape[::-1]
    (n, k2) = b_shape if tb else b_shape[::-1]
    assert k_dim == k2, (a.shape, b.shape)
    tm, tn, tk = min(tm, m), min(tn, n), min(tk, k_dim)
    assert m % tm == 0 and n % tn == 0 and k_dim % tk == 0, (name, m, n, k_dim, tm, tn, tk)
    nk = k_dim // tk
    ca, cb = (0 if ta else 1), (1 if tb else 0)

    def body(*refs):
        if add is None:
            a_ref, b_ref, o_ref, acc = refs
        else:
            a_ref, b_ref, c_ref, o_ref, acc = refs
        kk = pl.program_id(2)

        @pl.when(kk == 0)
        def _():
            acc[...] = jnp.zeros_like(acc)

        acc[...] += _dot(a_ref[...], b_ref[...], ca, cb)

        @pl.when(kk == nk - 1)
        def _():
            r = acc[...]
            if add is not None:
                r = r + add_scale * c_ref[...]
            o_ref[...] = r.astype(out_dtype)

    def per_part(total, parts, tile):
        per = total // parts // tile
        assert per * tile * parts == total, (name, total, parts, tile)
        return per

    spec = pl.BlockSpec
    a_spec = spec((tk, tm), lambda i, j, k: (k, i)) if ta else spec((tm, tk), lambda i, j, k: (i, k))
    b_spec = spec((tn, tk), lambda i, j, k: (j, k)) if tb else spec((tk, tn), lambda i, j, k: (k, j))
    if a_halves:
        kh = per_part(k_dim, 2, tk)
        a_spec = spec((None, tm, tk), lambda i, j, k: (k // kh, i, k % kh))
    if b_parts and tb:
        kp = per_part(k_dim, b_parts, tk)
        b_spec = spec((None, tn, tk), lambda i, j, k: (k // kp, j, k % kp))
    elif b_parts:
        np_ = per_part(n, b_parts, tn)
        b_spec = spec((None, tk, tn), lambda i, j, k: (j // np_, k, j % np_))
    out_spec, out_shape = spec((tm, tn), lambda i, j, k: (i, j)), (m, n)
    if out_parts:
        op = per_part(n, out_parts, tn)
        out_spec, out_shape = spec((None, tm, tn), lambda i, j, k: (j // op, i, j % op)), (out_parts, m, n // out_parts)
    in_specs = [a_spec, b_spec]
    args = [a, b]
    if add is not None:
        in_specs.append(pl.BlockSpec((tm, tn), lambda i, j, k: (i, j)))
        args.append(add)
    grid = (m // tm, n // tn, nk)
    r_in, r_out, r_shapes, r_sems, r_args = _ride_specs(ride)
    outs = pl.pallas_call(
        _riding(body, len(args), 1, 1, ride, grid), name=name, grid=grid,
        in_specs=in_specs + r_in, out_specs=[out_spec] + r_out,
        out_shape=[jax.ShapeDtypeStruct(out_shape, out_dtype)] + r_shapes,
        scratch_shapes=[pltpu.VMEM((tm, tn), F32)] + r_sems,
        compiler_params=_params(("parallel", "parallel", "arbitrary") if ride is None else ("arbitrary",) * 3),
    )(*args, *r_args)
    return outs[0] if ride is None else (outs[0], list(outs[1:]))


def _riding(core, n_in, n_out, n_scratch, ride, grid):
    if ride is None:
        return core
    copies, nr = ride[1], len(ride[0])
    steps = int(np.prod(grid))
    pass_step = min(max(int(steps * ride[4]), 1), steps - 2)
    assert steps >= 3, grid

    def body(*refs):
        cuts = np.cumsum([0, n_in, nr, n_out, nr, n_scratch])
        ins, rin, outs, rout, scratch = (refs[a:b] for a, b in zip(cuts[:-1], cuts[1:]))
        sems = refs[cuts[-1]:]
        step = 0
        for axis, size in enumerate(grid):
            step = step * size + pl.program_id(axis)

        @pl.when(step == 0)
        def _():
            _exchange_start(copies(rin, rout, sems))

        @pl.when(step == pass_step)
        def _():
            _exchange_pass_on(copies(rin, rout, sems))

        core(*ins, *outs, *scratch)

        @pl.when(step == steps - 1)
        def _():
            _exchange_finish(copies(rin, rout, sems))

    return body


def _ride_specs(ride):
    if ride is None:
        return [], [], [], [], []
    arrays, _, sems, shapes, _ = ride
    return [HBM_REF] * len(arrays), [HBM_REF] * len(arrays), list(shapes), sems(len(arrays)), list(arrays)


def _gdn_fwd(proj, conv_w, sc, norm_g, bl, s, ride=None):
    nc = s // CHUNK

    def core(ph_ref, ab_ref, cwq_ref, cwk_ref, cwv_ref, sc_ref, ng_ref, cat_ref, o_ref, st_ref, q_s, k_s, v_s, gb_s):
        def into(ref):
            def store(rows, value):
                ref[rows, :] = value.astype(ref.dtype)
            return store

        for fn, col, cw_ref, val_s in [(_gdn_q, 0, cwq_ref, q_s), (_gdn_k, 128, cwk_ref, k_s), (_gdn_v, 256, cwv_ref, v_s)]:
            _rows_apply(fn, [lambda r, col=col: ph_ref[r, col:col + 128]], [cw_ref[...]], into(val_s), s, LANES, CONV_HALO)
        _rows_apply(functools.partial(_gdn_gate, head=pl.program_id(1)), [lambda r: ab_ref[r, :]], [sc_ref[...]], into(gb_s), s, LANES, 0)

        group = math.gcd(nc, GDN_FWD_GROUP)

        def rows_of(n):
            return slice(n * CHUNK, (n + 1) * CHUNK)

        def levels_of(g):
            rows = [rows_of(g * group + j) for j in range(group)]
            return _gdn_local_levels([q_s[r, :] for r in rows], [k_s[r, :] for r in rows], [v_s[r, :] for r in rows],
                                     [gb_s[r, :] for r in rows], _solve_levels)

        local, state = _run(levels_of(0)), jnp.zeros((GDN_DK, GDN_DV), F32)
        for g in range(nc // group):
            ahead = levels_of(g + 1) if g + 1 < nc // group else None
            following = None
            for j, loc in enumerate(local):
                n = g * group + j
                st_ref[n] = state
                o_ref[rows_of(n), :], state = _gdn_state_step(*loc, state)
                if ahead is not None and following is None:
                    try:
                        next(ahead)
                    except StopIteration as stop:
                        following = stop.value
            if ahead is not None and following is None:
                following = _run(ahead)
            local = following
        _rows_apply(_gdn_post, [lambda r: o_ref[r, :], lambda r: ph_ref[r, 384:512]], [ng_ref[...]], into(cat_ref), s, LANES, 0)

    t = bl * s
    r_in, r_out, r_shapes, r_sems, r_args = _ride_specs(ride)
    outs = pl.pallas_call(
        _riding(core, 7, 7, 0, ride, (bl, GDN_HEADS)), name="gdn_fwd", grid=(bl, GDN_HEADS),
        in_specs=[
            pl.BlockSpec((s, 512), lambda b, h: (b, h)),
            pl.BlockSpec((s, 128), lambda b, h: (b, P_MLA // 128)),
            pl.BlockSpec((GDN_CONV, 128), lambda b, h: (0, h)),
            pl.BlockSpec((GDN_CONV, 128), lambda b, h: (0, GDN_HEADS + h)),
            pl.BlockSpec((GDN_CONV, 128), lambda b, h: (0, 2 * GDN_HEADS + h)),
            pl.BlockSpec((8, 128), lambda b, h: (0, 0)),
            pl.BlockSpec((1, 128), lambda b, h: (0, 0)),
        ] + r_in,
        out_specs=[
            pl.BlockSpec((s, 128), lambda b, h: (b, h)),
            pl.BlockSpec((s, 128), lambda b, h: (b, h)),
            pl.BlockSpec((None, None, nc, GDN_DK, GDN_DV), lambda b, h: (b, h, 0, 0, 0)),
        ] + [pl.BlockSpec((s, 128), lambda b, h: (b, h))] * 4 + r_out,
        out_shape=[
            jax.ShapeDtypeStruct((t, 2 * GDN_VW), BF16),
            jax.ShapeDtypeStruct((t, GDN_VW), F32),
            jax.ShapeDtypeStruct((bl, GDN_HEADS, nc, GDN_DK, GDN_DV), F32),
        ] + [jax.ShapeDtypeStruct((t, GDN_VW), F32)] * 4 + r_shapes,
        scratch_shapes=r_sems,
        compiler_params=_params(("arbitrary", "arbitrary")),
    )(proj, proj, conv_w, conv_w, conv_w, sc, norm_g, *r_args)
    return outs[0], outs[1], outs[2], tuple(outs[3:7]), list(outs[7:])


def _gdn_bwd(proj, conv_w, sc, norm_g, o_raw, states, qkvg, dcat, bl, s, ride=None):
    nc = s // CHUNK

    def core(ph_ref, ab_ref, cwq_ref, cwk_ref, cwv_ref, sc_ref, ng_ref, o_ref, st_ref, dc_ref, q_in, k_in, v_in, gb_in,
             dph_ref, dab_ref, dcwq_ref, dcwk_ref, dcwv_ref, dsc_ref, dng_ref, q_s, k_s, v_s, gb_s, do_s):
        head = pl.program_id(1)
        gate = functools.partial(_gdn_gate, head=head)
        paths = [(_gdn_q, 0, cwq_ref, q_s, dcwq_ref), (_gdn_k, 128, cwk_ref, k_s, dcwk_ref), (_gdn_v, 256, cwv_ref, v_s, dcwv_ref)]
        def into(ref, cols=slice(None)):
            def store(rows, value, add=False):
                if add:
                    ref[rows, cols] += value.astype(ref.dtype)
                else:
                    ref[rows, cols] = value.astype(ref.dtype)
            return store

        (dng,) = _rows_vjp(_gdn_post, [lambda r: o_ref[r, :], lambda r: ph_ref[r, 384:512]], [ng_ref[...]],
                           lambda r: dc_ref[r, :], [into(do_s), into(dph_ref, slice(384, 512))], s, LANES, 0)
        dng_ref[...] = jnp.broadcast_to(dng, dng_ref.shape)

        group = math.gcd(nc, GDN_BWD_GROUP)

        def chunks(i, dstate):
            ns = [nc - 1 - (i * group + j) for j in range(group)]
            rows = [pl.ds(pl.multiple_of(n * CHUNK, CHUNK), CHUNK) for n in ns]
            local, local_vjp = jax.vjp(_gdn_local_group, [q_in[r, :] for r in rows], [k_in[r, :] for r in rows],
                                       [v_in[r, :] for r in rows], [gb_in[r, :] for r in rows])
            d_os = [do_s[r, :] for r in rows]
            dlocal = []
            for n, loc, d_o in zip(ns, local, d_os):
                _, step_vjp = jax.vjp(_gdn_state_step, *loc, st_ref[n])
                *dloc, dstate = step_vjp((d_o, dstate))
                dlocal.append(tuple(dloc))
            dqs, dks, dvs, dgbs = local_vjp(dlocal)
            for r, dq, dk, dv, dgb in zip(rows, dqs, dks, dvs, dgbs):
                q_s[r, :], k_s[r, :], v_s[r, :], gb_s[r, :] = dq, dk, dv, dgb
            return dstate

        lax.fori_loop(0, nc // group, chunks, jnp.zeros((GDN_DK, GDN_DV), F32))
        for fn, col, cw_ref, val_s, dcw_ref in paths:
            (dcw_ref[...],) = _rows_vjp(fn, [lambda r, col=col: ph_ref[r, col:col + 128]], [cw_ref[...]],
                                        lambda r, val_s=val_s: val_s[r, :], [into(val_s)], s, LANES, CONV_HALO)
            dph_ref[:, col:col + 128] = val_s[...].astype(BF16)

        @pl.when(head == 0)
        def _():
            dab_ref[...] = jnp.zeros_like(dab_ref)

        def add_dab(rows, value, add=False):
            dab_ref[rows, :] += value

        (dsc_ref[...],) = _rows_vjp(gate, [lambda r: ab_ref[r, :]], [sc_ref[...]], lambda r: gb_s[r, :], [add_dab], s, LANES, 0)

    t = bl * s
    cw_out = pl.BlockSpec((None, GDN_CONV, 128), lambda b, h: (b, 0, h))
    part = pl.BlockSpec((None, None, 8, 128), lambda b, h: (b, h, 0, 0))
    r_in, r_out, r_shapes, r_sems, r_args = _ride_specs(ride)
    outs = pl.pallas_call(
        _riding(core, 14, 7, 5, ride, (bl, GDN_HEADS)), name="gdn_bwd", grid=(bl, GDN_HEADS),
        in_specs=[
            pl.BlockSpec((s, 512), lambda b, h: (b, h)),
            pl.BlockSpec((s, 128), lambda b, h: (b, P_MLA // 128)),
            pl.BlockSpec((GDN_CONV, 128), lambda b, h: (0, h)),
            pl.BlockSpec((GDN_CONV, 128), lambda b, h: (0, GDN_HEADS + h)),
            pl.BlockSpec((GDN_CONV, 128), lambda b, h: (0, 2 * GDN_HEADS + h)),
            pl.BlockSpec((8, 128), lambda b, h: (0, 0)),
            pl.BlockSpec((1, 128), lambda b, h: (0, 0)),
            pl.BlockSpec((s, 128), lambda b, h: (b, h)),
            pl.BlockSpec((None, None, nc, GDN_DK, GDN_DV), lambda b, h: (b, h, 0, 0, 0)),
        ] + [pl.BlockSpec((s, 128), lambda b, h: (b, h))] * 5 + r_in,
        out_specs=[
            pl.BlockSpec((s, 512), lambda b, h: (b, h)),
            pl.BlockSpec((s, 128), lambda b, h: (b, 0)),
            cw_out, cw_out, cw_out, part, part,
        ] + r_out,
        out_shape=[
            jax.ShapeDtypeStruct((t, P_WIDTH), BF16),
            jax.ShapeDtypeStruct((t, 128), F32),
            jax.ShapeDtypeStruct((bl, GDN_CONV, 512), F32),
            jax.ShapeDtypeStruct((bl, GDN_CONV, 512), F32),
            jax.ShapeDtypeStruct((bl, GDN_CONV, 512), F32),
            jax.ShapeDtypeStruct((bl, GDN_HEADS, 8, 128), F32),
            jax.ShapeDtypeStruct((bl, GDN_HEADS, 8, 128), F32),
        ] + r_shapes,
        scratch_shapes=[pltpu.VMEM((s, 128), F32)] * 5 + r_sems,
        compiler_params=_params(("arbitrary", "arbitrary")),
    )(proj, proj, conv_w, conv_w, conv_w, sc, norm_g, o_raw, states, dcat, *qkvg, *r_args)
    return tuple(outs[:7]) + (list(outs[7:]),)


def _mla_prep_fwd(proj, qg, kvg, wq, wkv, cos, sin, s, tm):
    t = proj.shape[0]
    tm = min(tm, s)
    nps = s // tm
    const = lambda shape: pl.BlockSpec(shape, lambda i: (0, 0))

    def body(pm_ref, qg_ref, kvg_ref, wq_ref, wkv_ref, cos_ref, sin_ref, qf_ref, kvf_ref, kr_ref):
        qf, kvf, kr = _mla_prep(pm_ref[...], qg_ref[...], kvg_ref[...], wq_ref[...], wkv_ref[...], cos_ref[...], sin_ref[...])
        qf_ref[...], kvf_ref[...], kr_ref[...] = qf.astype(BF16), kvf.astype(BF16), kr.astype(BF16)

    return pl.pallas_call(
        body, name="mla_prep_fwd", grid=(t // tm,),
        in_specs=[
            pl.BlockSpec((tm, 1024), lambda i: (i, P_MLA // 1024)),
            const((1, MLA_Q_LORA)), const((1, MLA_KV_LORA)), const(wq.shape), const(wkv.shape),
            pl.BlockSpec((tm, 128), lambda i: (i % nps, 0)), pl.BlockSpec((tm, 128), lambda i: (i % nps, 0)),
        ],
        out_specs=[pl.BlockSpec((tm, 1024), lambda i: (i, 0)), pl.BlockSpec((tm, 1024), lambda i: (i, 0)),
                   pl.BlockSpec((tm, 128), lambda i: (i, 0))],
        out_shape=[jax.ShapeDtypeStruct((t, 1024), BF16), jax.ShapeDtypeStruct((t, 1024), BF16),
                   jax.ShapeDtypeStruct((t, 128), BF16)],
        compiler_params=_params(("parallel",)),
    )(proj, qg, kvg, wq, wkv, cos, sin)


def _mla_prep_bwd(proj, qg, kvg, wq, wkv, cos, sin, dqf, dkvf, dkr, dab, dproj, s, tm):
    t = proj.shape[0]
    tm = min(tm, s)
    nps = s // tm
    const = lambda shape: pl.BlockSpec(shape, lambda i: (0, 0))

    def body(pm_ref, qg_ref, kvg_ref, wq_ref, wkv_ref, cos_ref, sin_ref, dqf_ref, dkvf_ref, dkr_ref, dab_ref, dp_in,
             dp_ref, dqg_ref, dkvg_ref, dwq_ref, dwkv_ref):
        del dp_in
        fn = lambda pm, qg_, kvg_, wq_, wkv_: _mla_prep(pm, qg_, kvg_, wq_, wkv_, cos_ref[...], sin_ref[...])
        _, vjp = jax.vjp(fn, pm_ref[...], qg_ref[...], kvg_ref[...], wq_ref[...].astype(F32), wkv_ref[...].astype(F32))
        dpm, dqg, dkvg, dwq, dwkv = vjp((dqf_ref[...], dkvf_ref[...], dkr_ref[...]))
        dp_ref[...] = jnp.concatenate([dab_ref[...], dpm[:, 128:]], axis=1).astype(BF16)

        @pl.when(pl.program_id(0) == 0)
        def _():
            dqg_ref[...] = jnp.zeros_like(dqg_ref)
            dkvg_ref[...] = jnp.zeros_like(dkvg_ref)
            dwq_ref[...] = jnp.zeros_like(dwq_ref)
            dwkv_ref[...] = jnp.zeros_like(dwkv_ref)

        dqg_ref[...] += dqg
        dkvg_ref[...] += dkvg
        dwq_ref[...] += dwq
        dwkv_ref[...] += dwkv

    rows = lambda w: pl.BlockSpec((tm, w), lambda i: (i, 0))
    return pl.pallas_call(
        body, name="mla_prep_bwd", grid=(t // tm,),
        in_specs=[
            pl.BlockSpec((tm, 1024), lambda i: (i, P_MLA // 1024)),
            const((1, MLA_Q_LORA)), const((1, MLA_KV_LORA)), const(wq.shape), const(wkv.shape),
            pl.BlockSpec((tm, 128), lambda i: (i % nps, 0)), pl.BlockSpec((tm, 128), lambda i: (i % nps, 0)),
            rows(1024), rows(1024), rows(128), rows(128),
            pl.BlockSpec(memory_space=pl.ANY),
        ],
        out_specs=[pl.BlockSpec((tm, 1024), lambda i: (i, P_MLA // 1024)),
                   const((1, MLA_Q_LORA)), const((1, MLA_KV_LORA)), const(wq.shape), const(wkv.shape)],
        out_shape=[jax.ShapeDtypeStruct(dproj.shape, dproj.dtype),
                   jax.ShapeDtypeStruct((1, MLA_Q_LORA), F32), jax.ShapeDtypeStruct((1, MLA_KV_LORA), F32),
                   jax.ShapeDtypeStruct(wq.shape, F32), jax.ShapeDtypeStruct(wkv.shape, F32)],
        input_output_aliases={11: 0},
        compiler_params=_params(("arbitrary",)),
    )(proj, qg, kvg, wq, wkv, cos, sin, dqf, dkvf, dkr, dab, dproj)


def _attn_fwd(qf, kvf, kr, cat, bl, s, tq):
    tq = min(tq, s)
    nq = s // tq

    def body(q_ref, kv_ref, kr_ref, cat_in, o_ref, o32_ref, lse_ref):
        del cat_in

        def scores_of(i):
            keys = slice(0, (i + 1) * tq)
            return _scores(q_ref[i * tq:(i + 1) * tq, :], kv_ref[keys, 0:128], kr_ref[keys, :], i * tq, 0)

        ready = scores_of(0)
        for i in range(nq):
            scores = ready
            if i + 1 < nq:
                ready = scores_of(i + 1)
            rows = slice(i * tq, (i + 1) * tq)
            o, lse = _softmax_times(scores, kv_ref[0:(i + 1) * tq, 128:256])
            o_ref[rows, :] = o.astype(o_ref.dtype)
            o32_ref[rows, :] = o
            lse_ref[rows, :] = jnp.broadcast_to(lse, o.shape)

    t = bl * s
    head_cols = pl.BlockSpec((s, 128), lambda b, h: (b, h))
    return pl.pallas_call(
        body, name="attn_fwd", grid=(bl, MLA_HEADS),
        in_specs=[
            pl.BlockSpec((s, 256), lambda b, h: (b, h)),
            pl.BlockSpec((s, 256), lambda b, h: (b, h)),
            pl.BlockSpec((s, 128), lambda b, h: (b, 0)),
            pl.BlockSpec(memory_space=pl.ANY),
        ],
        out_specs=[pl.BlockSpec((s, 128), lambda b, h: (b, GDN_HEADS + h)), head_cols, head_cols],
        out_shape=[jax.ShapeDtypeStruct(cat.shape, cat.dtype)] + [jax.ShapeDtypeStruct((t, MLA_HEADS * MLA_V), F32)] * 2,
        input_output_aliases={3: 0},
        compiler_params=_params(("parallel", "parallel")),
    )(qf, kvf, kr, cat)


def _attn_bwd(qf, kvf, kr, dcat, o32, lse, bl, s, tq):
    tq = min(tq, s)
    nq = s // tq

    def body(q_ref, kv_ref, kr_ref, do_ref, o_ref, lse_ref, dq_ref, dkv_ref, dkr_ref):
        dkv_ref[...] = jnp.zeros_like(dkv_ref)

        @pl.when(pl.program_id(1) == 0)
        def _():
            dkr_ref[...] = jnp.zeros_like(dkr_ref)

        def block(i):
            rows = slice(i * tq, (i + 1) * tq)
            return q_ref[rows, :], do_ref[rows, :]

        def first_products(i, j):
            (q, d_o), keys = block(i), slice(j * tq, (j + 1) * tq)
            return _scores(q, kv_ref[keys, 0:128], kr_ref[keys, :], i * tq, j * tq), mm_nt(d_o, kv_ref[keys, 128:256])

        tiles = [(i, j) for i in range(nq) for j in range(i + 1)]
        ready = first_products(*tiles[0])
        for t, (i, j) in enumerate(tiles):
            scores, dp = ready
            if t + 1 < len(tiles):
                ready = first_products(*tiles[t + 1])
            rows, keys = slice(i * tq, (i + 1) * tq), slice(j * tq, (j + 1) * tq)
            q, d_o = block(i)
            if j == 0:
                delta = jnp.sum(d_o * o_ref[rows, :], axis=-1, keepdims=True)
                lse_i = jnp.max(lse_ref[rows, :], axis=-1, keepdims=True)
                dqn, dqr = jnp.zeros((tq, MLA_NOPE), F32), jnp.zeros((tq, 128), F32)
            p = jnp.exp(scores - lse_i)
            ds = p * (dp - delta) * ATT_SCALE
            dkv_ref[keys, 128:256] += mm_tn(p, d_o)
            dkv_ref[keys, 0:128] += mm_tn(ds, q[:, :128])
            dkr_ref[keys, :] += mm_tn(ds, q[:, 128:])
            dqn = dqn + mm(ds, kv_ref[keys, 0:128])
            dqr = dqr + mm(ds, kr_ref[keys, :])
            if j == i:
                dq_ref[rows, 0:128], dq_ref[rows, 128:256] = dqn, dqr

    t = bl * s
    head_cols = pl.BlockSpec((s, 128), lambda b, h: (b, h))
    return pl.pallas_call(
        body, name="attn_bwd", grid=(bl, MLA_HEADS),
        in_specs=[
            pl.BlockSpec((s, 256), lambda b, h: (b, h)),
            pl.BlockSpec((s, 256), lambda b, h: (b, h)),
            pl.BlockSpec((s, 128), lambda b, h: (b, 0)),
            pl.BlockSpec((s, 128), lambda b, h: (b, GDN_HEADS + h)),
            head_cols, head_cols,
        ],
        out_specs=[
            pl.BlockSpec((s, 256), lambda b, h: (b, h)),
            pl.BlockSpec((s, 256), lambda b, h: (b, h)),
            pl.BlockSpec((s, 128), lambda b, h: (b, 0)),
        ],
        out_shape=[jax.ShapeDtypeStruct((t, 1024), F32), jax.ShapeDtypeStruct((t, 1024), F32),
                   jax.ShapeDtypeStruct((t, 128), F32)],
        compiler_params=_params(("parallel", "arbitrary")),
    )(qf, kvf, kr, dcat, o32, lse)


def _mix_ln1_fwd(x, cat, w_o, g, b, tm):
    t = x.shape[0]
    tm = min(tm, t)

    def body(x_ref, cat_ref, w_ref, g_ref, b_ref, r_ref, h_ref, hb_ref, xb_ref):
        r = ALPHA * x_ref[...] + _dot(cat_ref[...], w_ref[...], 1, 0)
        r_ref[...] = r
        h = _layernorm(r, g_ref[...], b_ref[...])
        h_ref[...] = h
        hb_ref[...] = h.astype(BF16)
        xb_ref[...] = x_ref[...].astype(BF16)

    rows = pl.BlockSpec((tm, D_MODEL), lambda i: (i, 0))
    vec = pl.BlockSpec((1, D_MODEL), lambda i: (0, 0))
    return pl.pallas_call(
        body, name="mix_ln1_fwd", grid=(t // tm,),
        in_specs=[rows, pl.BlockSpec((tm, cat.shape[1]), lambda i: (i, 0)), pl.BlockSpec(w_o.shape, lambda i: (0, 0)), vec, vec],
        out_specs=[rows] * 4,
        out_shape=[jax.ShapeDtypeStruct(x.shape, F32)] * 2 + [jax.ShapeDtypeStruct(x.shape, BF16)] * 2,
        compiler_params=_params(("parallel",)),
    )(x, cat, w_o, g, b)


def _ln1_bwd(r1, dr2, da, db_, g, b, tm):
    t = r1.shape[0]
    tm = min(tm, t)

    def body(r_ref, d2_ref, da_ref, db_ref, g_ref, b_ref, dr_ref, drb_ref, dg_ref, dbias_ref):
        dh = ALPHA * d2_ref[...] + da_ref[...] + db_ref[...]
        _, vjp = jax.vjp(_layernorm, r_ref[...], g_ref[...], b_ref[...])
        dr, dg, dbias = vjp(dh)
        dr_ref[...] = dr
        drb_ref[...] = dr.astype(BF16)

        @pl.when(pl.program_id(0) == 0)
        def _():
            dg_ref[...] = jnp.zeros_like(dg_ref)
            dbias_ref[...] = jnp.zeros_like(dbias_ref)

        dg_ref[...] += dg
        dbias_ref[...] += dbias

    rows = pl.BlockSpec((tm, D_MODEL), lambda i: (i, 0))
    vec = pl.BlockSpec((1, D_MODEL), lambda i: (0, 0))
    return pl.pallas_call(
        body, name="ln1_bwd", grid=(t // tm,), in_specs=[rows] * 4 + [vec, vec], out_specs=[rows, rows, vec, vec],
        out_shape=[jax.ShapeDtypeStruct(r1.shape, F32), jax.ShapeDtypeStruct(r1.shape, BF16)]
        + [jax.ShapeDtypeStruct((1, D_MODEL), F32)] * 2,
        compiler_params=_params(("arbitrary",)),
    )(r1, dr2, da, db_, g, b)


def _ffn_act_fwd(u, conv_w, conv_b, bl, s, cb):
    nj = D_FF // cb

    def body(ug_ref, uu_ref, wg_ref, wu_ref, bg_ref, bu_ref, act_ref):
        def store(rows, act):
            act_ref[rows, :] = act.astype(BF16)

        _rows_apply(_ffn_act, [lambda r: ug_ref[r, :], lambda r: uu_ref[r, :]],
                    [wg_ref[...], wu_ref[...], bg_ref[...], bu_ref[...]], store, s, cb, CONV_HALO)

    return pl.pallas_call(
        body, name="ffn_act_fwd", grid=(bl, nj),
        in_specs=[
            pl.BlockSpec((s, cb), lambda b, j: (b, j)), pl.BlockSpec((s, cb), lambda b, j: (b, nj + j)),
            pl.BlockSpec((FFN_CONV, cb), lambda b, j: (0, j)), pl.BlockSpec((FFN_CONV, cb), lambda b, j: (0, nj + j)),
            pl.BlockSpec((1, cb), lambda b, j: (0, j)), pl.BlockSpec((1, cb), lambda b, j: (0, nj + j)),
        ],
        out_specs=pl.BlockSpec((s, cb), lambda b, j: (b, j)),
        out_shape=jax.ShapeDtypeStruct((bl * s, D_FF), BF16),
        compiler_params=_params(("parallel", "parallel")),
    )(u, u, conv_w, conv_w, conv_b, conv_b)


def _ffn_act_bwd(u, conv_w, conv_b, dact, bl, s, cb):
    nj = D_FF // cb

    def body(ug_ref, uu_ref, wg_ref, wu_ref, bg_ref, bu_ref, da_ref, du_ref, dwg_ref, dwu_ref, dbg_ref, dbu_ref, acc):
        def store_into(half):
            def store(rows, value, add):
                if add:
                    acc[half, rows, :] += value
                else:
                    acc[half, rows, :] = value
            return store

        dwg_ref[...], dwu_ref[...], dbg_ref[...], dbu_ref[...] = _rows_vjp(
            _ffn_act, [lambda r: ug_ref[r, :], lambda r: uu_ref[r, :]], [wg_ref[...], wu_ref[...], bg_ref[...], bu_ref[...]],
            lambda r: da_ref[r, :], [store_into(0), store_into(1)], s, cb, CONV_HALO)
        du_ref[...] = acc[...].astype(BF16)

    t = bl * s
    blk = pl.BlockSpec((s, cb), lambda b, j: (b, j))
    wpart = pl.BlockSpec((None, FFN_CONV, cb), lambda b, j: (b, 0, j))
    bpart = pl.BlockSpec((None, 1, cb), lambda b, j: (b, 0, j))
    return pl.pallas_call(
        body, name="ffn_act_bwd", grid=(bl, nj),
        in_specs=[
            blk, pl.BlockSpec((s, cb), lambda b, j: (b, nj + j)),
            pl.BlockSpec((FFN_CONV, cb), lambda b, j: (0, j)), pl.BlockSpec((FFN_CONV, cb), lambda b, j: (0, nj + j)),
            pl.BlockSpec((1, cb), lambda b, j: (0, j)), pl.BlockSpec((1, cb), lambda b, j: (0, nj + j)),
            blk,
        ],
        out_specs=[pl.BlockSpec((2, s, cb), lambda b, j: (0, b, j)), wpart, wpart, bpart, bpart],
        out_shape=[jax.ShapeDtypeStruct((2, t, D_FF), BF16)] + [jax.ShapeDtypeStruct((bl, FFN_CONV, D_FF), F32)] * 2
        + [jax.ShapeDtypeStruct((bl, 1, D_FF), F32)] * 2,
        scratch_shapes=[pltpu.VMEM((2, s, cb), F32)],
        compiler_params=_params(("parallel", "parallel")),
    )(u, u, conv_w, conv_w, conv_b, conv_b, dact)


def _head(h1, ffn, gpre, pp, bgate, g2, b2, target, tm):
    t = h1.shape[0]
    tm = min(tm, t)

    def body(h1_ref, ffn_ref, gp_ref, pp_ref, bg_ref, g2_ref, b2_ref, tg_ref,
             dr_ref, drb_ref, dgp_ref, dpp_ref, loss_ref, dbg_ref, dg2_ref, db2_ref):
        fn = functools.partial(_head_loss, target=tg_ref[...])
        loss, vjp = jax.vjp(fn, h1_ref[...], ffn_ref[...], gp_ref[...], pp_ref[...], bg_ref[...], g2_ref[...], b2_ref[...])
        _, dffn, dgp, dpp, dbg, dg2, db2 = vjp(jnp.ones((1, 1), F32))
        dr_ref[...] = dffn
        drb_ref[...], dgp_ref[...], dpp_ref[...] = dffn.astype(BF16), dgp.astype(BF16), dpp.astype(BF16)

        @pl.when(pl.program_id(0) == 0)
        def _():
            loss_ref[...] = jnp.zeros_like(loss_ref)
            dbg_ref[...] = jnp.zeros_like(dbg_ref)
            dg2_ref[...] = jnp.zeros_like(dg2_ref)
            db2_ref[...] = jnp.zeros_like(db2_ref)

        loss_ref[...] += jnp.broadcast_to(loss, loss_ref.shape)
        dbg_ref[...] += dbg
        dg2_ref[...] += dg2
        db2_ref[...] += db2

    rows = pl.BlockSpec((tm, D_MODEL), lambda i: (i, 0))
    vec = pl.BlockSpec((1, D_MODEL), lambda i: (0, 0))
    return pl.pallas_call(
        body, name="head", grid=(t // tm,), in_specs=[rows] * 4 + [vec] * 3 + [rows],
        out_specs=[rows] * 4 + [pl.BlockSpec((8, 128), lambda i: (0, 0))] + [vec] * 3,
        out_shape=[jax.ShapeDtypeStruct(h1.shape, F32)] + [jax.ShapeDtypeStruct(h1.shape, BF16)] * 3
        + [jax.ShapeDtypeStruct((8, 128), F32)]
        + [jax.ShapeDtypeStruct((1, D_MODEL), F32)] * 3,
        compiler_params=_params(("arbitrary",)),
    )(h1, ffn, gpre, pp, bgate, g2, b2, target)


def _adam_update(g, w_ref, m_ref, v_ref, g_ref, d_ref, nm_ref, nv_ref):
    m2 = ADAM_B1 * m_ref[...] + (1.0 - ADAM_B1) * g
    v2 = ADAM_B2 * v_ref[...] + (1.0 - ADAM_B2) * jnp.square(g)
    m_hat = m2 / (1.0 - ADAM_B1 ** ADAM_STEP)
    v_hat = v2 / (1.0 - ADAM_B2 ** ADAM_STEP)
    g_ref[...] = g
    d_ref[...] = -ADAM_LR * (m_hat / (jnp.sqrt(v_hat) + ADAM_EPS) + ADAM_WD * w_ref[...])
    nm_ref[...] = m2
    nv_ref[...] = v2


def _row_tile(rows, cols, limit_bytes=512 * 1024):
    best = None
    for t in range(HALF_ROWS_QUANTUM, rows + 1, HALF_ROWS_QUANTUM):
        if rows % t == 0 and t * cols * 4 <= limit_bytes:
            best = t
    return best or rows


def _adamw_reduced(recv, w, m, v, name):
    a, b = w.shape
    ta = _row_tile(a, b)

    def body(recv_ref, w_ref, m_ref, v_ref, g_ref, d_ref, nm_ref, nv_ref):
        c = lax.axis_index("c")
        for core in range(2):
            @pl.when(c == core)
            def _():
                got = [recv_ref[k].astype(F32) for k in range(N_DEV)]
                same = [got[7], got[0], got[1], got[2]]
                other = got[3:7]
                core0, core1 = (same, other) if core == 0 else (other, same)
                g = core0[0] + core1[0]
                for r in range(1, N_CHIPS):
                    g = (g + core0[r]) + core1[r]
                _adam_update(g, w_ref, m_ref, v_ref, g_ref, d_ref, nm_ref, nv_ref)

    blk = pl.BlockSpec((ta, b), lambda i: (i, 0))
    return pl.pallas_call(
        body, name=name, grid=(a // ta,),
        in_specs=[pl.BlockSpec((N_DEV, ta, b), lambda i: (0, i, 0)), blk, blk, blk], out_specs=[blk] * 4,
        out_shape=[jax.ShapeDtypeStruct(w.shape, F32)] * 4, compiler_params=_params(("parallel",)),
    )(recv, w, m, v)


def _adamw_small(g, w, m, v):
    def body(g_in, w_ref, m_ref, v_ref, g_ref, d_ref, nm_ref, nv_ref):
        _adam_update(g_in[...], w_ref, m_ref, v_ref, g_ref, d_ref, nm_ref, nv_ref)

    blk = pl.BlockSpec(w.shape, lambda i: (0, 0))
    return pl.pallas_call(
        body, name="adamw_small", grid=(1,), in_specs=[blk] * 4, out_specs=[blk] * 4,
        out_shape=[jax.ShapeDtypeStruct(w.shape, F32)] * 4, compiler_params=_params(("arbitrary",)),
    )(g, w, m, v)


def _remote(src, dst, send_sem, recv_sem, device):
    return pltpu.make_async_remote_copy(src_ref=src, dst_ref=dst, send_sem=send_sem, recv_sem=recv_sem,
                                        device_id=device, device_id_type=MESH)


def _place():
    x, y, c = lax.axis_index("x"), lax.axis_index("y"), lax.axis_index("c")
    return x, y, c, 2 * x + y, [(1 - x, y), (x, 1 - y), (1 - x, 1 - y)]


HBM_REF = pl.BlockSpec(memory_space=pl.ANY)
HALF_ROWS_QUANTUM = 16


def _gather_sems(n):
    return [pltpu.SemaphoreType.DMA((3 * n,))] * 4 + [pltpu.SemaphoreType.DMA((n,))]


def _gather_copies(ins, outs, sems):
    send_s, recv_s, fsend_s, frecv_s, local_s = sems
    x, y, c, me, chips = _place()
    local, sends, steps = [], [], []
    for i, (src, dst) in enumerate(zip(ins, outs)):
        local.append(pltpu.make_async_copy(src, dst.at[me], local_s.at[i]))
        half = src.shape[0] // 2
        split = src.shape[0] % (2 * HALF_ROWS_QUANTUM) == 0
        if split:
            mine = pl.ds(pl.multiple_of(c * half, HALF_ROWS_QUANTUM), half)
            theirs = pl.ds(pl.multiple_of((1 - c) * half, HALF_ROWS_QUANTUM), half)
        for r, (px, py) in enumerate(chips):
            k, peer = 3 * i + r, 2 * px + py
            if split:
                sends.append(_remote(src.at[mine], dst.at[me, mine], send_s.at[k], recv_s.at[k], (px, py, c)))
                landed = dst.at[peer, mine]
                steps.append((_remote(src.at[mine], landed, send_s.at[k], recv_s.at[k], (px, py, c)),
                              _remote(landed, landed, fsend_s.at[k], frecv_s.at[k], (x, y, 1 - c)),
                              _remote(dst.at[peer, theirs], dst.at[peer, theirs], fsend_s.at[k], frecv_s.at[k], (x, y, 1 - c))))
            else:
                sends.append(_remote(src, dst.at[me], send_s.at[k], recv_s.at[k], (px, py, c)))
                steps.append((_remote(src, dst.at[peer], send_s.at[k], recv_s.at[k], (px, py, c)), None, None))
    return local, sends, steps


def _scatter_sems(n):
    return [pltpu.SemaphoreType.DMA((4 * n,))] * 2 + [pltpu.SemaphoreType.DMA((3 * n,))] * 2 + [pltpu.SemaphoreType.DMA((n,))]


def _scatter_copies(ins, outs, sems):
    send_s, recv_s, fsend_s, frecv_s, local_s = sems
    x, y, c, me, chips = _place()
    local, sends, steps = [], [], []
    for i, (src, dst) in enumerate(zip(ins, outs)):
        local.append(pltpu.make_async_copy(src.at[me], dst.at[N_DEV - 1], local_s.at[i]))
        for r, (px, py) in enumerate(chips):
            k = 4 * i + r
            cp = _remote(src.at[2 * px + py], dst.at[r], send_s.at[k], recv_s.at[k], (px, py, c))
            fwd = _remote(dst.at[r], dst.at[4 + r], fsend_s.at[3 * i + r], frecv_s.at[3 * i + r], (x, y, 1 - c))
            sends.append(cp)
            steps.append((cp, fwd, fwd))
        k = 4 * i + 3
        cp = _remote(src.at[me], dst.at[3], send_s.at[k], recv_s.at[k], (x, y, 1 - c))
        sends.append(cp)
        steps.append((cp, None, None))
    return local, sends, steps


def _exchange_start(plan):
    local, sends, _ = plan
    for cp in local + sends:
        cp.start()


def _exchange_pass_on(plan):
    for arrival, pass_on, _ in plan[2]:
        arrival.wait_recv()
        if pass_on is not None:
            pass_on.start()


def _exchange_finish(plan):
    local, sends, steps = plan
    for _, pass_on, passed in steps:
        if pass_on is not None:
            passed.wait_recv()
    for cp in sends:
        cp.wait_send()
    for _, pass_on, _ in steps:
        if pass_on is not None:
            pass_on.wait_send()
    for cp in local:
        cp.wait()


def _exchange_call(arrays, copies, sems, out_shapes, name):
    n = len(arrays)

    def body(*refs):
        plan = copies(refs[:n], refs[n:2 * n], refs[2 * n:])
        _exchange_start(plan)
        _exchange_pass_on(plan)
        _exchange_finish(plan)

    return pl.pallas_call(
        body, name=name, in_specs=[HBM_REF] * n, out_specs=[HBM_REF] * n, out_shape=out_shapes,
        scratch_shapes=sems(n), compiler_params=pltpu.CompilerParams(has_side_effects=True),
    )(*arrays)


def _gather_call(shards, name):
    shapes = [jax.ShapeDtypeStruct((N_CHIPS,) + a.shape, a.dtype) for a in shards]
    return _exchange_call(shards, _gather_copies, _gather_sems, shapes, name)


def _all_reduce_small(a):
    def body(in_ref, out_ref, slots, send_sems, recv_sems):
        x, y, c = lax.axis_index("x"), lax.axis_index("y"), lax.axis_index("c")
        me = 4 * x + 2 * y + c
        slots[0] = in_ref[...]
        sends = []
        for r in range(1, N_DEV):
            peer = (x ^ (r >> 2), y ^ ((r >> 1) & 1), c ^ (r & 1))
            sends.append(pltpu.make_async_remote_copy(src_ref=in_ref, dst_ref=slots.at[r], send_sem=send_sems.at[r],
                                                      recv_sem=recv_sems.at[r], device_id=peer, device_id_type=MESH))
        for cp in sends:
            cp.start()
        for cp in sends:
            cp.wait_recv()
        acc = slots[me]
        for dev in range(1, N_DEV):
            acc = acc + slots[dev ^ me]
        out_ref[...] = acc
        for cp in sends:
            cp.wait_send()

    return pl.pallas_call(
        body, name="small_all_reduce",
        in_specs=[pl.BlockSpec(memory_space=pltpu.VMEM)], out_specs=pl.BlockSpec(memory_space=pltpu.VMEM),
        out_shape=jax.ShapeDtypeStruct(a.shape, a.dtype),
        scratch_shapes=[pltpu.VMEM((N_DEV,) + a.shape, a.dtype), pltpu.SemaphoreType.DMA((N_DEV,)),
                        pltpu.SemaphoreType.DMA((N_DEV,))],
        compiler_params=pltpu.CompilerParams(has_side_effects=True),
    )(a)


SHARDED = ["w_in", "mla_w_q_up", "mla_w_kv_up", "w_out", "ffn_w_up", "ffn_w_down", "ple_w_gate", "ple_w_proj",
           "gdn_conv_w", "ffn_conv_w"]
SHARD_AXIS = {"w_in": 1, "mla_w_q_up": 1, "mla_w_kv_up": 1, "w_out": 0, "ffn_w_up": 1, "ffn_w_down": 0,
              "ple_w_gate": 0, "ple_w_proj": 1, "gdn_conv_w": 1, "ffn_conv_w": 1}
SMALL = ["gdn_a_log", "gdn_dt_bias", "gdn_norm_g", "mla_q_norm_g", "mla_kv_norm_g", "ln1_g", "ln1_b", "ffn_conv_b",
         "ple_b_gate", "ln2_g", "ln2_b"]
WEIGHTS = ["w_in", "gdn_conv_w", "gdn_a_log", "gdn_dt_bias", "gdn_norm_g", "mla_q_norm_g", "mla_w_q_up", "mla_kv_norm_g",
           "mla_w_kv_up", "w_out", "ln1_g", "ln1_b", "ffn_w_up", "ffn_conv_w", "ffn_conv_b", "ffn_w_down", "ple_w_gate",
           "ple_b_gate", "ple_w_proj", "ln2_g", "ln2_b"]
F32_ON_WIRE = ("gdn_conv_w", "ffn_conv_w")
GATHER_EARLY = ["w_in", "gdn_conv_w"]
GATHER_LATE = ["mla_w_q_up", "mla_w_kv_up", "w_out", "ffn_w_up", "ffn_conv_w", "ffn_w_down", "ple_w_gate", "ple_w_proj"]
SCATTER_EARLY = ["ffn_w_up", "ffn_conv_w", "ffn_w_down", "ple_w_gate", "ple_w_proj", "w_out"]
SCATTER_LATE = ["w_in", "gdn_conv_w", "mla_w_q_up", "mla_w_kv_up"]
PACK_COLS = 1024
PACK_ROW_TILE = 8


def _join_blocks(blocks, axis):
    n, a, b = blocks.shape
    if axis == 0:
        return blocks.reshape(n * a, b)
    return jnp.transpose(blocks, (1, 0, 2)).reshape(a, n * b)


def _split_blocks(full, axis):
    if axis == 0:
        return full.reshape(N_CHIPS, full.shape[0] // N_CHIPS, full.shape[1])
    a, nb = full.shape
    return jnp.transpose(full.reshape(a, N_CHIPS, nb // N_CHIPS), (1, 0, 2))


def _pack(arrays):
    flat = jnp.concatenate([a.reshape(-1) for a in arrays])
    quantum = PACK_COLS * PACK_ROW_TILE
    padded = -(-flat.shape[0] // quantum) * quantum
    return jnp.pad(flat, (0, padded - flat.shape[0])).reshape(-1, PACK_COLS)


def _unpack(packed, shapes):
    flat = packed.reshape(-1)
    out, off = [], 0
    for shp in shapes:
        n = int(np.prod(shp))
        out.append(flat[off:off + n].reshape(shp))
        off += n
    return out


def kernel(x, p, w_in, gdn_conv_w, gdn_a_log, gdn_dt_bias, gdn_norm_g, mla_q_norm_g, mla_w_q_up, mla_kv_norm_g, mla_w_kv_up, w_out, ln1_g, ln1_b, ffn_w_up, ffn_conv_w, ffn_conv_b, ffn_w_down, ple_w_gate, ple_b_gate, ple_w_proj, ln2_g, ln2_b, loss_target, m_w_in, m_gdn_conv_w, m_gdn_a_log, m_gdn_dt_bias, m_gdn_norm_g, m_mla_q_norm_g, m_mla_w_q_up, m_mla_kv_norm_g, m_mla_w_kv_up, m_w_out, m_ln1_g, m_ln1_b, m_ffn_w_up, m_ffn_conv_w, m_ffn_conv_b, m_ffn_w_down, m_ple_w_gate, m_ple_b_gate, m_ple_w_proj, m_ln2_g, m_ln2_b, v_w_in, v_gdn_conv_w, v_gdn_a_log, v_gdn_dt_bias, v_gdn_norm_g, v_mla_q_norm_g, v_mla_w_q_up, v_mla_kv_norm_g, v_mla_w_kv_up, v_w_out, v_ln1_g, v_ln1_b, v_ffn_w_up, v_ffn_conv_w, v_ffn_conv_b, v_ffn_w_down, v_ple_w_gate, v_ple_b_gate, v_ple_w_proj, v_ln2_g, v_ln2_b):
    given = dict(locals())
    wsh = {n: given[n][0] for n in WEIGHTS}
    msh = {n: given["m_" + n][0] for n in WEIGHTS}
    vsh = {n: given["v_" + n][0] for n in WEIGHTS}
    bl, s, _ = x.shape
    t = bl * s
    xt = x.reshape(t, D_MODEL)
    pt = p.reshape(t, PLE_DIM)
    target = loss_target.reshape(t, D_MODEL)

    wire = lambda n: wsh[n] if n in F32_ON_WIRE else wsh[n].astype(BF16)
    early = _gather_call([wire(n) for n in GATHER_EARLY], "weights_gather_early")
    full = {n: _join_blocks(g, SHARD_AXIS[n]) for n, g in zip(GATHER_EARLY, early)}
    late_shards = [wire(n) for n in GATHER_LATE]
    late_ride = (late_shards, _gather_copies, _gather_sems,
                 [jax.ShapeDtypeStruct((N_CHIPS,) + a.shape, a.dtype) for a in late_shards], 0.75)

    in_cols, q_cols = _w_in_cols(), _w_q_cols()
    w_in_p = _pad_cols(full["w_in"], in_cols)
    gconv = full["gdn_conv_w"]
    row = lambda a: a.reshape(1, -1)
    sc = jnp.zeros((8, 128), F32).at[0, :GDN_HEADS].set(wsh["gdn_a_log"]).at[1, :GDN_HEADS].set(wsh["gdn_dt_bias"])
    norm_g, qg, kvg = row(wsh["gdn_norm_g"]), row(wsh["mla_q_norm_g"]), row(wsh["mla_kv_norm_g"])
    g1, b1, g2, b2 = row(wsh["ln1_g"]), row(wsh["ln1_b"]), row(wsh["ln2_g"]), row(wsh["ln2_b"])
    fbias, bgate = row(wsh["ffn_conv_b"]), row(wsh["ple_b_gate"])

    inv = ROPE_THETA ** (-jnp.arange(0, MLA_ROPE, 2, dtype=F32) / MLA_ROPE)
    ang = jnp.arange(s, dtype=F32)[:, None] * inv[None, :]
    zero = jnp.zeros_like(ang)
    cos_t = jnp.concatenate([jnp.cos(ang), zero, jnp.cos(ang), zero], axis=1)
    sin_t = jnp.concatenate([-jnp.sin(ang), zero, jnp.sin(ang), zero], axis=1)

    proj = _matmul(xt, w_in_p, name="proj", tm=1024)
    cat, o_raw, states, qkvg, late = _gdn_fwd(proj, gconv, sc, norm_g, bl, s, late_ride)
    w_up = late[GATHER_LATE.index("ffn_w_up")]
    full.update({n: _join_blocks(g, SHARD_AXIS[n]) for n, g in zip(GATHER_LATE, late) if n != "ffn_w_up"})
    w_o, w_down = full["w_out"], full["ffn_w_down"]
    w_gate, w_proj, fconv = full["ple_w_gate"], full["ple_w_proj"], full["ffn_conv_w"]
    w_q_p, w_kv = _pad_cols(full["mla_w_q_up"], q_cols), full["mla_w_kv_up"]
    qf, kvf, kr = _mla_prep_fwd(proj, qg, kvg, w_q_p, w_kv, cos_t, sin_t, s, 256)
    cat, attn_o32, attn_lse = _attn_fwd(qf, kvf, kr, cat, bl, s, 512)
    wide = dict(tm=1024, tn=1024)
    r1, h1, h1b, xb = _mix_ln1_fwd(xt, cat, w_o, g1, b1, 512)
    u = _matmul(h1b, w_up, name="ffn_up", tm=1024, tn=1408, b_parts=N_CHIPS)
    act = _ffn_act_fwd(u, fconv, fbias, bl, s, 256)
    ffn = _matmul(act, w_down, name="ffn_down", tk=1408, **wide)
    gpre = _matmul(h1b, w_gate, name="ple_gate", **wide)
    pp = _matmul(pt, w_proj, name="ple_proj", **wide)
    dr2, dr2b, dgpre, dpp, loss_acc, dbgate, dg2, db2 = _head(h1, ffn, gpre, pp, bgate, g2, b2, target, 256)

    dact = _matmul(dr2b, w_down, name="d_act", tb=True, tm=1024, tn=1408)
    long_k = dict(ta=True, tk=2048)
    d_w_down = _matmul(act, dr2b, name="dw_down", tm=1408, tn=1024, **long_k)
    du, dfcw_g, dfcw_u, dfcb_g, dfcb_u = _ffn_act_bwd(u, fconv, fbias, dact, bl, s, 256)
    dh1_a = _matmul(du, w_up, name="dh1_ffn", tb=True, tk=1408, a_halves=True, b_parts=N_CHIPS, **wide)
    dh1_b = _matmul(dgpre, w_gate, name="dh1_ple", tb=True, **wide)
    d_w_up = _matmul(h1b, du, name="dw_up", tn=1408, b_parts=2, out_parts=N_CHIPS, out_dtype=BF16, **long_k)
    d_w_gate = _matmul(h1b, dgpre, name="dw_gate", **long_k, **wide)
    d_w_proj = _matmul(pt, dpp, name="dw_proj", ta=True, tn=1024)
    dr1, dr1b, dg1, db1 = _ln1_bwd(r1, dr2, dh1_a, dh1_b, g1, b1, 256)
    dcat = _matmul(dr1b, w_o, name="d_cat", tb=True, **wide)
    d_w_o = _matmul(cat, dr1b, name="dw_out", **long_k, **wide)

    gfull = {
        "ffn_w_down": d_w_down, "ple_w_gate": d_w_gate, "ple_w_proj": d_w_proj, "w_out": d_w_o,
        "ffn_conv_w": jnp.concatenate([jnp.sum(dfcw_g, 0), jnp.sum(dfcw_u, 0)], axis=1),
    }
    slabs = {n: _split_blocks(g, SHARD_AXIS[n]).astype(BF16) for n, g in gfull.items()}
    slabs["ffn_w_up"] = d_w_up
    early_slabs = [slabs[n] for n in SCATTER_EARLY]
    early_ride = (early_slabs, _scatter_copies, _scatter_sems,
                  [jax.ShapeDtypeStruct((N_DEV,) + a.shape[1:], a.dtype) for a in early_slabs], 0.7)
    dproj, dab, dcwq, dcwk, dcwv, dsc, dng, early_recv = _gdn_bwd(proj, gconv, sc, norm_g, o_raw, states, qkvg, dcat, bl, s,
                                                                  early_ride)
    received = dict(zip(SCATTER_EARLY, early_recv))
    dqf, dkvf, dkr = _attn_bwd(qf, kvf, kr, dcat, attn_o32, attn_lse, bl, s, 256)
    dproj, dqg, dkvg, d_w_q_p, d_w_kv = _mla_prep_bwd(proj, qg, kvg, w_q_p, w_kv, cos_t, sin_t, dqf, dkvf, dkr, dab, dproj, s, 256)
    d_w_in_p = _matmul(xb, dproj, name="dw_in", **long_k, **wide)

    gfull.update({
        "w_in": _unpad_cols(d_w_in_p, in_cols, D_IN),
        "mla_w_q_up": _unpad_cols(d_w_q_p, q_cols, MLA_HEADS * (MLA_NOPE + MLA_ROPE)),
        "mla_w_kv_up": d_w_kv,
        "gdn_conv_w": jnp.concatenate([jnp.sum(dcwq, 0), jnp.sum(dcwk, 0), jnp.sum(dcwv, 0)], axis=1),
    })
    slabs.update({n: _split_blocks(gfull[n], SHARD_AXIS[n]).astype(BF16) for n in SCATTER_LATE})
    late_slabs = [slabs[n] for n in SCATTER_LATE]
    late_scatter = (late_slabs, _scatter_copies, _scatter_sems,
                    [jax.ShapeDtypeStruct((N_DEV,) + a.shape[1:], a.dtype) for a in late_slabs], 0.85)
    grad_x, late_recv = _matmul(dproj, w_in_p, name="d_x", tb=True, add=dr1, add_scale=ALPHA, ride=late_scatter, **wide)
    received.update(zip(SCATTER_LATE, late_recv))
    dsc_sum = jnp.sum(dsc, axis=(0, 1))
    gsmall = {
        "gdn_a_log": dsc_sum[0, :GDN_HEADS], "gdn_dt_bias": dsc_sum[1, :GDN_HEADS],
        "gdn_norm_g": jnp.sum(dng[:, :, 0, :], axis=(0, 1)),
        "mla_q_norm_g": dqg[0], "mla_kv_norm_g": dkvg[0], "ln1_g": dg1[0], "ln1_b": db1[0],
        "ffn_conv_b": jnp.concatenate([jnp.sum(dfcb_g, 0), jnp.sum(dfcb_u, 0)], axis=1)[0],
        "ple_b_gate": dbgate[0], "ln2_g": dg2[0], "ln2_b": db2[0],
    }

    big = [{}, {}, {}, {}]
    for n in SHARDED:
        for kind, val in enumerate(_adamw_reduced(received[n], wsh[n], msh[n], vsh[n], "adamw_" + n)):
            big[kind][n] = val

    small_shapes = [wsh[n].shape for n in SMALL]
    gsum = _all_reduce_small(_pack([gsmall[n] for n in SMALL]))
    spacks = _adamw_small(gsum, _pack([wsh[n] for n in SMALL]), _pack([msh[n] for n in SMALL]), _pack([vsh[n] for n in SMALL]))
    small = [dict(zip(SMALL, _unpack(pk, small_shapes))) for pk in spacks]

    loss = lax.psum(loss_acc[0, 0], ("x", "y", "c"))
    outs = [loss, grad_x.reshape(x.shape)]
    for kind in range(4):
        for n in WEIGHTS:
            val = big[kind][n] if n in big[kind] else small[kind][n]
            outs.append(val[None])
    return tuple(outs)
```

```python
import functools
import math

import numpy as np
import jax
import jax.numpy as jnp
from jax import lax
from jax.experimental import pallas as pl
from jax.experimental.pallas import tpu as pltpu

F32 = jnp.float32
BF16 = jnp.bfloat16

D_MODEL = 1024
CHUNK = 64
PLE_DIM = 256
GDN_HEADS = 4
GDN_DK = 128
GDN_DV = 128
GDN_CONV = 4
MLA_HEADS = 4
MLA_NOPE = 128
MLA_ROPE = 64
MLA_V = 128
MLA_Q_LORA = 384
MLA_KV_LORA = 256
ROPE_THETA = 10000.0
D_FF = 2816
FFN_CONV = 3
DEPTH = 1
ALPHA = (2.0 * DEPTH) ** 0.25
NORM_EPS = 1e-6
GDN_QK = GDN_HEADS * GDN_DK
GDN_VW = GDN_HEADS * GDN_DV
D_IN = 2 * GDN_QK + 2 * GDN_VW + 2 * GDN_HEADS + MLA_Q_LORA + MLA_KV_LORA + MLA_ROPE
ATT_SCALE = (MLA_NOPE + MLA_ROPE) ** -0.5

ADAM_LR = 0.001
ADAM_B1 = 0.9
ADAM_B2 = 0.999
ADAM_EPS = 1e-08
ADAM_WD = 0.01
ADAM_STEP = 10

LANES = 128
VMEM_LIMIT = 60 * 1024 * 1024
GDN_FWD_GROUP = 16
GDN_BWD_GROUP = 16
N_CHIPS = 4
N_DEV = 8

P_WIDTH = 3072
P_MLA = 2048
MESH = pl.DeviceIdType.MESH


def _rope_slot(j):
    return j if j < MLA_ROPE // 2 else 64 + (j - MLA_ROPE // 2)


def _w_in_cols():
    idx = -np.ones((P_WIDTH,), np.int64)
    for h in range(GDN_HEADS):
        base = h * 512
        idx[base:base + 128] = np.arange(128) + h * GDN_DK
        idx[base + 128:base + 256] = np.arange(128) + GDN_QK + h * GDN_DK
        idx[base + 256:base + 384] = np.arange(128) + 2 * GDN_QK + h * GDN_DV
        idx[base + 384:base + 512] = np.arange(128) + 2 * GDN_QK + GDN_VW + h * GDN_DV
    o_a = 2 * GDN_QK + 2 * GDN_VW
    idx[P_MLA:P_MLA + 2 * GDN_HEADS] = np.arange(2 * GDN_HEADS) + o_a
    o_cq = o_a + 2 * GDN_HEADS
    idx[P_MLA + 128:P_MLA + 512] = np.arange(MLA_Q_LORA) + o_cq
    o_ckv = o_cq + MLA_Q_LORA
    idx[P_MLA + 512:P_MLA + 768] = np.arange(MLA_KV_LORA) + o_ckv
    o_kr = o_ckv + MLA_KV_LORA
    for j in range(MLA_ROPE):
        idx[P_MLA + 768 + _rope_slot(j)] = o_kr + j
    return idx


def _w_q_cols():
    idx = -np.ones((MLA_HEADS * 256,), np.int64)
    for h in range(MLA_HEADS):
        o = h * (MLA_NOPE + MLA_ROPE)
        idx[h * 256:h * 256 + 128] = np.arange(128) + o
        for j in range(MLA_ROPE):
            idx[h * 256 + 128 + _rope_slot(j)] = o + MLA_NOPE + j
    return idx


def _pad_cols(w, idx):
    safe = np.where(idx >= 0, idx, 0)
    return jnp.where(jnp.asarray(idx >= 0)[None, :], w[:, safe], 0.0)


def _unpad_cols(wp, idx, n):
    inv = np.zeros((n,), np.int64)
    inv[idx[idx >= 0]] = np.nonzero(idx >= 0)[0]
    return wp[:, inv]


def _dot(a, b, ca, cb, precision=None):
    if precision is None:
        a = a.astype(BF16)
        b = b.astype(BF16)
    return lax.dot_general(a, b, (((ca,), (cb,)), ((), ())), preferred_element_type=F32, precision=precision)


@jax.custom_vjp
def mm(a, b):
    return _dot(a, b, 1, 0)


@jax.custom_vjp
def mm_nt(a, b):
    return _dot(a, b, 1, 1)


@jax.custom_vjp
def mm_tn(a, b):
    return _dot(a, b, 0, 0)


mm.defvjp(lambda a, b: (mm(a, b), (a, b)), lambda r, g: (mm_nt(g, r[1]), mm_tn(r[0], g)))
mm_nt.defvjp(lambda a, b: (mm_nt(a, b), (a, b)), lambda r, g: (mm(g, r[1]), mm_tn(g, r[0])))
mm_tn.defvjp(lambda a, b: (mm_tn(a, b), (a, b)), lambda r, g: (mm_nt(r[1], g), mm(r[0], g)))

def _split(a):
    hi = a.astype(BF16)
    return hi, (a - hi.astype(F32)).astype(BF16)


def _dot3(a, b, ca, cb):
    a_hi, a_lo = _split(a)
    b_hi, b_lo = _split(b)
    return (_dot(a_hi, b_hi, ca, cb) + _dot(a_hi, b_lo, ca, cb)) + _dot(a_lo, b_hi, ca, cb)


def _shift_rows(x, s):
    return x if s == 0 else pltpu.roll(x, s % x.shape[0], 0)


def _row(w, j):
    tap = lax.broadcasted_iota(jnp.int32, w.shape, 0)
    return jnp.sum(jnp.where(tap == j, w, 0.0), axis=0, keepdims=True)


@jax.custom_vjp
def dwconv(x, w):
    k = w.shape[0]
    y = _row(w, k - 1) * x
    for j in range(k - 1):
        y = y + _row(w, j) * _shift_rows(x, k - 1 - j)
    return y


def _dwconv_fwd(x, w):
    return dwconv(x, w), (x, w)


def _dwconv_bwd(res, dy):
    x, w = res
    k = w.shape[0]
    dx = _row(w, k - 1) * dy
    tap = lax.broadcasted_iota(jnp.int32, w.shape, 0)
    dw = jnp.where(tap == k - 1, jnp.sum(dy * x, axis=0, keepdims=True), 0.0)
    for j in range(k - 1):
        dx = dx + _row(w, j) * _shift_rows(dy, -(k - 1 - j))
        dw = dw + jnp.where(tap == j, jnp.sum(dy * _shift_rows(x, k - 1 - j), axis=0, keepdims=True), 0.0)
    return dx, dw


dwconv.defvjp(_dwconv_fwd, _dwconv_bwd)


@jax.custom_vjp
def rope128(x, cos, sin):
    return x * cos + pltpu.roll(x, 64, 1) * sin


rope128.defvjp(lambda x, c, s: (rope128(x, c, s), (c, s)),
               lambda r, g: (g * r[0] + pltpu.roll(g * r[1], 64, 1), jnp.zeros_like(r[0]), jnp.zeros_like(r[1])))


def _silu(x):
    return x * jax.nn.sigmoid(x)


def _softplus(x):
    return jnp.maximum(x, 0.0) + jnp.log(1.0 + jnp.exp(-jnp.abs(x)))


def _rmsnorm(x, g):
    return x * lax.rsqrt(jnp.mean(x * x, axis=-1, keepdims=True) + NORM_EPS) * g


def _layernorm(x, g, b):
    mu = jnp.mean(x, axis=-1, keepdims=True)
    xc = x - mu
    var = jnp.mean(xc * xc, axis=-1, keepdims=True)
    return xc * lax.rsqrt(var + NORM_EPS) * g + b


def _pick_lane(row, lane):
    idx = lax.broadcasted_iota(jnp.int32, row.shape, 1)
    return jnp.sum(jnp.where(idx == lane, row, 0.0), axis=1, keepdims=True)


def _gdn_q(pq, cw):
    h = _silu(dwconv(pq, cw))
    return h * lax.rsqrt(jnp.sum(h * h, axis=-1, keepdims=True) + NORM_EPS) * (GDN_DK ** -0.5)


def _gdn_k(pk, cw):
    h = _silu(dwconv(pk, cw))
    return h * lax.rsqrt(jnp.sum(h * h, axis=-1, keepdims=True) + NORM_EPS)


def _gdn_v(pv, cw):
    return _silu(dwconv(pv, cw))


def _gdn_gate(ab, sc, head):
    a = _pick_lane(ab, head)
    b = _pick_lane(ab, GDN_HEADS + head)
    a_log = _pick_lane(_row(sc, 0), head)
    dt_bias = _pick_lane(_row(sc, 1), head)
    beta = jax.nn.sigmoid(b)
    g = -jnp.exp(a_log) * _softplus(a + dt_bias)
    return _two_lanes(g, beta)


def _two_lanes(c0, c1):
    lane = lax.broadcasted_iota(jnp.int32, (c0.shape[0], LANES), 1)
    return jnp.where(lane == 0, c0, jnp.where(lane == 1, c1, 0.0))


def _run(levels):
    try:
        while True:
            next(levels)
    except StopIteration as stop:
        return stop.value


def _inverse_levels(lows):
    n = lows[0].shape[0]
    ii = lax.broadcasted_iota(jnp.int32, (n, n), 0)
    jj = lax.broadcasted_iota(jnp.int32, (n, n), 1)
    eye = jnp.where(ii == jj, 1.0, 0.0)
    invs = [eye - low for low in lows]
    powers = [_dot3(low, low, 1, 0) for low in lows]
    yield
    k = 2
    while k < n:
        invs = [inv + _dot3(inv, p, 1, 0) for inv, p in zip(invs, powers)]
        yield
        k *= 2
        if k < n:
            powers = [_dot3(p, p, 1, 0) for p in powers]
            yield
    return invs


def _inverse_group(lows):
    return _run(_inverse_levels(lows))


@jax.custom_vjp
def solve_group(lows, rhss):
    return [_dot3(inv, rhs, 1, 0) for inv, rhs in zip(_inverse_group(lows), rhss)]


def _solve_whole(lows, rhss):
    return solve_group(lows, rhss)
    yield


def _solve_levels(lows, rhss):
    invs = yield from _inverse_levels(lows)
    return [_dot3(inv, rhs, 1, 0) for inv, rhs in zip(invs, rhss)]


def _solve_group_fwd(lows, rhss):
    invs = _inverse_group(lows)
    xs = [_dot3(inv, rhs, 1, 0) for inv, rhs in zip(invs, rhss)]
    return xs, (invs, xs)


def _solve_group_bwd(res, dxs):
    invs, xs = res
    n = invs[0].shape[0]
    strict = lax.broadcasted_iota(jnp.int32, (n, n), 0) > lax.broadcasted_iota(jnp.int32, (n, n), 1)
    drhss = [_dot3(inv, dx, 0, 0) for inv, dx in zip(invs, dxs)]
    dlows = [jnp.where(strict, -_dot3(drhs, x, 1, 1), 0.0) for drhs, x in zip(drhss, xs)]
    return dlows, drhss


solve_group.defvjp(_solve_group_fwd, _solve_group_bwd)


def _gdn_local_group(qs, ks, vs, gbs):
    return _run(_gdn_local_levels(qs, ks, vs, gbs, _solve_whole))


def _gdn_local_levels(qs, ks, vs, gbs, solve):
    c = qs[0].shape[0]
    ii = lax.broadcasted_iota(jnp.int32, (c, c), 0)
    jj = lax.broadcasted_iota(jnp.int32, (c, c), 1)
    incl = ii >= jj
    gs = [_pick_lane(gb, 0) for gb in gbs]
    betas = [_pick_lane(gb, 1) for gb in gbs]
    g_rows = [jnp.sum(jnp.where(ii == jj, g, 0.0), axis=0, keepdims=True) for g in gs]
    gc_cols = [jnp.sum(jnp.where(incl, g_row, 0.0), axis=1, keepdims=True) for g_row in g_rows]
    gc_rows = [jnp.sum(jnp.where(jj >= ii, g, 0.0), axis=0, keepdims=True) for g in gs]
    decays = [jnp.where(incl, jnp.exp(jnp.where(incl, gc - gr, 0.0)), 0.0) for gc, gr in zip(gc_cols, gc_rows)]
    yield
    kbs = [k * beta for k, beta in zip(ks, betas)]
    lows = [jnp.where(ii > jj, mm_nt(kb, k) * decay, 0.0) for kb, k, decay in zip(kbs, ks, decays)]
    egs = [jnp.exp(gc) for gc in gc_cols]
    yield
    wus = yield from solve(lows, [jnp.concatenate([kb * eg, v * beta], axis=1) for kb, eg, v, beta in zip(kbs, egs, vs, betas)])
    yield
    qks = [mm_nt(q, k) * decay for q, k, decay in zip(qs, ks, decays)]
    g_lasts = [jnp.sum(g_row, axis=1, keepdims=True) for g_row in g_rows]
    kds = [k * jnp.exp(gl - gc) for k, gl, gc in zip(ks, g_lasts, gc_cols)]
    yield
    ws, us = [wu[:, :GDN_DK] for wu in wus], [wu[:, GDN_DK:] for wu in wus]
    q_effs = [q * eg - mm(qk, w) for q, eg, qk, w in zip(qs, egs, qks, ws)]
    yield
    o_locals = [mm(qk, u) for qk, u in zip(qks, us)]
    yield
    mixes = [mm_tn(kd, w) for kd, w in zip(kds, ws)]
    yield
    adds = [mm_tn(kd, u) for kd, u in zip(kds, us)]
    return [(q_eff, o_loc, mix, add, jnp.exp(gl))
            for q_eff, o_loc, mix, add, gl in zip(q_effs, o_locals, mixes, adds, g_lasts)]


def _gdn_state_step(q_eff, o_local, mix, add, eg_last, state):
    return mm(q_eff, state) + o_local, state * eg_last - mm(mix, state) + add


def _gdn_post(o, z, norm_g):
    return _rmsnorm(o, norm_g) * _silu(z)


MASKED = -1e30


def _scores(q, kn, kr, q0, k0):
    s = (mm_nt(q[:, :128], kn) + mm_nt(q[:, 128:], kr)) * ATT_SCALE
    if k0 + kn.shape[0] <= q0:
        return s
    qpos = q0 + lax.broadcasted_iota(jnp.int32, s.shape, 0)
    kpos = k0 + lax.broadcasted_iota(jnp.int32, s.shape, 1)
    shift = int(math.log2(CHUNK))
    return jnp.where((kpos >> shift) <= (qpos >> shift), s, MASKED)


def _softmax_times(s, v):
    top = jnp.max(s, axis=-1, keepdims=True)
    p = jnp.exp(s - top)
    norm = jnp.sum(p, axis=-1, keepdims=True)
    return mm(p / norm, v), top + jnp.log(norm)


def _mla_prep(pm, qg, kvg, wq, wkv, cos, sin):
    cq = pm[:, 128:512]
    ckv = pm[:, 512:768]
    qf = mm(_rmsnorm(cq, qg), wq)
    parts = []
    for h in range(MLA_HEADS):
        parts.append(qf[:, h * 256:h * 256 + 128])
        parts.append(rope128(qf[:, h * 256 + 128:h * 256 + 256], cos, sin))
    kvf = mm(_rmsnorm(ckv, kvg), wkv)
    return jnp.concatenate(parts, axis=1), kvf, rope128(pm[:, 768:896], cos, sin)


def _ffn_act(ug, uu, wg, wu, bg, bu):
    return _silu(dwconv(ug, wg) + bg) * (dwconv(uu, wu) + bu)


def _head_loss(h1, ffn, gpre, pp, bgate, g2, b2, target):
    gate = jax.nn.sigmoid(gpre + bgate)
    h2 = _layernorm(ALPHA * h1 + ffn + gate * pp, g2, b2)
    err = h2 - target
    return 0.5 * jnp.sum(jnp.sum(err * err, axis=1, keepdims=True), axis=0, keepdims=True) / D_MODEL


ROW_TILE_VREGS = 32
CONV_HALO = 8


def _rows_per_tile(n_rows, cols):
    tile = min(n_rows, ROW_TILE_VREGS * 8 * LANES // cols)
    assert n_rows % tile == 0 and tile % CONV_HALO == 0, (n_rows, cols)
    return tile


def _tile_inputs(loads, t0, first, halo, tile):
    if halo == 0:
        return [ld(pl.ds(t0, tile)) for ld in loads]
    if first:
        xs = [ld(pl.ds(0, tile)) for ld in loads]
        return [jnp.concatenate([jnp.zeros((halo, x.shape[1]), x.dtype), x], axis=0) for x in xs]
    return [ld(pl.ds(pl.multiple_of(t0 - halo, CONV_HALO), tile + halo)) for ld in loads]


def _rows_apply(fn, loads, consts, store, n_rows, cols, halo):
    tile = _rows_per_tile(n_rows, cols)

    def one(t0, first):
        y = fn(*_tile_inputs(loads, t0, first, halo, tile), *consts)
        store(pl.ds(t0, tile), y[halo:] if halo else y)

    one(0, True)

    def step(i, carry):
        one(pl.multiple_of(i * tile, tile), False)
        return carry

    lax.fori_loop(1, n_rows // tile, step, 0)


def _rows_vjp(fn, loads, consts, load_dy, stores, n_rows, cols, halo):
    tile = _rows_per_tile(n_rows, cols)

    def one(t0, first, dconsts):
        xs = _tile_inputs(loads, t0, first, halo, tile)
        _, vjp = jax.vjp(lambda *a: fn(*a)[halo:] if halo else fn(*a), *xs, *consts)
        grads = vjp(load_dy(pl.ds(t0, tile)))
        for st, dx in zip(stores, grads[:len(xs)]):
            st(pl.ds(t0, tile), dx[halo:] if halo else dx, False)
            if halo and not first:
                st(pl.ds(pl.multiple_of(t0 - halo, CONV_HALO), halo), dx[:halo], True)
        return tuple(a + b for a, b in zip(dconsts, grads[len(xs):]))

    dconsts = one(0, True, tuple(jnp.zeros_like(c) for c in consts))
    return lax.fori_loop(1, n_rows // tile, lambda i, dc: one(pl.multiple_of(i * tile, tile), False, dc), dconsts)


def _params(sem):
    return pltpu.CompilerParams(dimension_semantics=sem, vmem_limit_bytes=VMEM_LIMIT)


def _matmul(a, b, *, name, ta=False, tb=False, tm=512, tn=512, tk=1024, add=None, add_scale=1.0,
            a_halves=False, b_parts=0, out_parts=0, ride=None, out_dtype=F32):
    assert not (a_halves and ta)
    a_shape = (a.shape[1], 2 * a.shape[2]) if a_halves else a.shape
    b_shape = (b.shape[1], b_parts * b.shape[2]) if b_parts else b.shape
    (k_dim, m) = a_shape if ta else a_shape[::-1]
    (n, k2) = b_shape if tb else b_shape[::-1]
    assert k_dim == k2, (a.shape, b.shape)
    tm, tn, tk = min(tm, m), min(tn, n), min(tk, k_dim)
    assert m % tm == 0 and n % tn == 0 and k_dim % tk == 0, (name, m, n, k_dim, tm, tn, tk)
    nk = k_dim // tk
    ca, cb = (0 if ta else 1), (1 if tb else 0)

    def body(*refs):
        if add is None:
            a_ref, b_ref, o_ref, acc = refs
        else:
            a_ref, b_ref, c_ref, o_ref, acc = refs
        kk = pl.program_id(2)

        @pl.when(kk == 0)
        def _():
            acc[...] = jnp.zeros_like(acc)

        acc[...] += _dot(a_ref[...], b_ref[...], ca, cb)

        @pl.when(kk == nk - 1)
        def _():
            r = acc[...]
            if add is not None:
                r = r + add_scale * c_ref[...]
            o_ref[...] = r.astype(out_dtype)

    def per_part(total, parts, tile):
        per = total // parts // tile
        assert per * tile * parts == total, (name, total, parts, tile)
        return per

    spec = pl.BlockSpec
    a_spec = spec((tk, tm), lambda i, j, k: (k, i)) if ta else spec((tm, tk), lambda i, j, k: (i, k))
    b_spec = spec((tn, tk), lambda i, j, k: (j, k)) if tb else spec((tk, tn), lambda i, j, k: (k, j))
    if a_halves:
        kh = per_part(k_dim, 2, tk)
        a_spec = spec((None, tm, tk), lambda i, j, k: (k // kh, i, k % kh))
    if b_parts and tb:
        kp = per_part(k_dim, b_parts, tk)
        b_spec = spec((None, tn, tk), lambda i, j, k: (k // kp, j, k % kp))
    elif b_parts:
        np_ = per_part(n, b_parts, tn)
        b_spec = spec((None, tk, tn), lambda i, j, k: (j // np_, k, j % np_))
    out_spec, out_shape = spec((tm, tn), lambda i, j, k: (i, j)), (m, n)
    if out_parts:
        op = per_part(n, out_parts, tn)
        out_spec, out_shape = spec((None, tm, tn), lambda i, j, k: (j // op, i, j % op)), (out_parts, m, n // out_parts)
    in_specs = [a_spec, b_spec]
    args = [a, b]
    if add is not None:
        in_specs.append(pl.BlockSpec((tm, tn), lambda i, j, k: (i, j)))
        args.append(add)
    grid = (m // tm, n // tn, nk)
    r_in, r_out, r_shapes, r_sems, r_args = _ride_specs(ride)
    outs = pl.pallas_call(
        _riding(body, len(args), 1, 1, ride, grid), name=name, grid=grid,
        in_specs=in_specs + r_in, out_specs=[out_spec] + r_out,
        out_shape=[jax.ShapeDtypeStruct(out_shape, out_dtype)] + r_shapes,
        scratch_shapes=[pltpu.VMEM((tm, tn), F32)] + r_sems,
        compiler_params=_params(("parallel", "parallel", "arbitrary") if ride is None else ("arbitrary",) * 3),
    )(*args, *r_args)
    return outs[0] if ride is None else (outs[0], list(outs[1:]))


def _riding(core, n_in, n_out, n_scratch, ride, grid):
    if ride is None:
        return core
    copies, nr = ride[1], len(ride[0])
    steps = int(np.prod(grid))
    pass_step = min(max(int(steps * ride[4]), 1), steps - 2)
    assert steps >= 3, grid

    def body(*refs):
        cuts = np.cumsum([0, n_in, nr, n_out, nr, n_scratch])
        ins, rin, outs, rout, scratch = (refs[a:b] for a, b in zip(cuts[:-1], cuts[1:]))
        sems = refs[cuts[-1]:]
        step = 0
        for axis, size in enumerate(grid):
            step = step * size + pl.program_id(axis)

        @pl.when(step == 0)
        def _():
            _exchange_start(copies(rin, rout, sems))

        @pl.when(step == pass_step)
        def _():
            _exchange_pass_on(copies(rin, rout, sems))

        core(*ins, *outs, *scratch)

        @pl.when(step == steps - 1)
        def _():
            _exchange_finish(copies(rin, rout, sems))

    return body


def _ride_specs(ride):
    if ride is None:
        return [], [], [], [], []
    arrays, _, sems, shapes, _ = ride
    return [HBM_REF] * len(arrays), [HBM_REF] * len(arrays), list(shapes), sems(len(arrays)), list(arrays)


def _gdn_fwd(proj, conv_w, sc, norm_g, bl, s, ride=None):
    nc = s // CHUNK

    def core(ph_ref, ab_ref, cwq_ref, cwk_ref, cwv_ref, sc_ref, ng_ref, cat_ref, o_ref, st_ref, q_s, k_s, v_s, gb_s):
        def into(ref):
            def store(rows, value):
                ref[rows, :] = value.astype(ref.dtype)
            return store

        for fn, col, cw_ref, val_s in [(_gdn_q, 0, cwq_ref, q_s), (_gdn_k, 128, cwk_ref, k_s), (_gdn_v, 256, cwv_ref, v_s)]:
            _rows_apply(fn, [lambda r, col=col: ph_ref[r, col:col + 128]], [cw_ref[...]], into(val_s), s, LANES, CONV_HALO)
        _rows_apply(functools.partial(_gdn_gate, head=pl.program_id(1)), [lambda r: ab_ref[r, :]], [sc_ref[...]], into(gb_s), s, LANES, 0)

        group = math.gcd(nc, GDN_FWD_GROUP)

        def rows_of(n):
            return slice(n * CHUNK, (n + 1) * CHUNK)

        def levels_of(g):
            rows = [rows_of(g * group + j) for j in range(group)]
            return _gdn_local_levels([q_s[r, :] for r in rows], [k_s[r, :] for r in rows], [v_s[r, :] for r in rows],
                                     [gb_s[r, :] for r in rows], _solve_levels)

        local, state = _run(levels_of(0)), jnp.zeros((GDN_DK, GDN_DV), F32)
        for g in range(nc // group):
            ahead = levels_of(g + 1) if g + 1 < nc // group else None
            following = None
            for j, loc in enumerate(local):
                n = g * group + j
                st_ref[n] = state
                o_ref[rows_of(n), :], state = _gdn_state_step(*loc, state)
                if ahead is not None and following is None:
                    try:
                        next(ahead)
                    except StopIteration as stop:
                        following = stop.value
            if ahead is not None and following is None:
                following = _run(ahead)
            local = following
        _rows_apply(_gdn_post, [lambda r: o_ref[r, :], lambda r: ph_ref[r, 384:512]], [ng_ref[...]], into(cat_ref), s, LANES, 0)

    t = bl * s
    r_in, r_out, r_shapes, r_sems, r_args = _ride_specs(ride)
    outs = pl.pallas_call(
        _riding(core, 7, 7, 0, ride, (bl, GDN_HEADS)), name="gdn_fwd", grid=(bl, GDN_HEADS),
        in_specs=[
            pl.BlockSpec((s, 512), lambda b, h: (b, h)),
            pl.BlockSpec((s, 128), lambda b, h: (b, P_MLA // 128)),
            pl.BlockSpec((GDN_CONV, 128), lambda b, h: (0, h)),
            pl.BlockSpec((GDN_CONV, 128), lambda b, h: (0, GDN_HEADS + h)),
            pl.BlockSpec((GDN_CONV, 128), lambda b, h: (0, 2 * GDN_HEADS + h)),
            pl.BlockSpec((8, 128), lambda b, h: (0, 0)),
            pl.BlockSpec((1, 128), lambda b, h: (0, 0)),
        ] + r_in,
        out_specs=[
            pl.BlockSpec((s, 128), lambda b, h: (b, h)),
            pl.BlockSpec((s, 128), lambda b, h: (b, h)),
            pl.BlockSpec((None, None, nc, GDN_DK, GDN_DV), lambda b, h: (b, h, 0, 0, 0)),
        ] + [pl.BlockSpec((s, 128), lambda b, h: (b, h))] * 4 + r_out,
        out_shape=[
            jax.ShapeDtypeStruct((t, 2 * GDN_VW), BF16),
            jax.ShapeDtypeStruct((t, GDN_VW), F32),
            jax.ShapeDtypeStruct((bl, GDN_HEADS, nc, GDN_DK, GDN_DV), F32),
        ] + [jax.ShapeDtypeStruct((t, GDN_VW), F32)] * 4 + r_shapes,
        scratch_shapes=r_sems,
        compiler_params=_params(("arbitrary", "arbitrary")),
    )(proj, proj, conv_w, conv_w, conv_w, sc, norm_g, *r_args)
    return outs[0], outs[1], outs[2], tuple(outs[3:7]), list(outs[7:])


def _gdn_bwd(proj, conv_w, sc, norm_g, o_raw, states, qkvg, dcat, bl, s, ride=None):
    nc = s // CHUNK

    def core(ph_ref, ab_ref, cwq_ref, cwk_ref, cwv_ref, sc_ref, ng_ref, o_ref, st_ref, dc_ref, q_in, k_in, v_in, gb_in,
             dph_ref, dab_ref, dcwq_ref, dcwk_ref, dcwv_ref, dsc_ref, dng_ref, q_s, k_s, v_s, gb_s, do_s):
        head = pl.program_id(1)
        gate = functools.partial(_gdn_gate, head=head)
        paths = [(_gdn_q, 0, cwq_ref, q_s, dcwq_ref), (_gdn_k, 128, cwk_ref, k_s, dcwk_ref), (_gdn_v, 256, cwv_ref, v_s, dcwv_ref)]
        def into(ref, cols=slice(None)):
            def store(rows, value, add=False):
                if add:
                    ref[rows, cols] += value.astype(ref.dtype)
                else:
                    ref[rows, cols] = value.astype(ref.dtype)
            return store

        (dng,) = _rows_vjp(_gdn_post, [lambda r: o_ref[r, :], lambda r: ph_ref[r, 384:512]], [ng_ref[...]],
                           lambda r: dc_ref[r, :], [into(do_s), into(dph_ref, slice(384, 512))], s, LANES, 0)
        dng_ref[...] = jnp.broadcast_to(dng, dng_ref.shape)

        group = math.gcd(nc, GDN_BWD_GROUP)

        def chunks(i, dstate):
            ns = [nc - 1 - (i * group + j) for j in range(group)]
            rows = [pl.ds(pl.multiple_of(n * CHUNK, CHUNK), CHUNK) for n in ns]
            local, local_vjp = jax.vjp(_gdn_local_group, [q_in[r, :] for r in rows], [k_in[r, :] for r in rows],
                                       [v_in[r, :] for r in rows], [gb_in[r, :] for r in rows])
            d_os = [do_s[r, :] for r in rows]
            dlocal = []
            for n, loc, d_o in zip(ns, local, d_os):
                _, step_vjp = jax.vjp(_gdn_state_step, *loc, st_ref[n])
                *dloc, dstate = step_vjp((d_o, dstate))
                dlocal.append(tuple(dloc))
            dqs, dks, dvs, dgbs = local_vjp(dlocal)
            for r, dq, dk, dv, dgb in zip(rows, dqs, dks, dvs, dgbs):
                q_s[r, :], k_s[r, :], v_s[r, :], gb_s[r, :] = dq, dk, dv, dgb
            return dstate

        lax.fori_loop(0, nc // group, chunks, jnp.zeros((GDN_DK, GDN_DV), F32))
        for fn, col, cw_ref, val_s, dcw_ref in paths:
            (dcw_ref[...],) = _rows_vjp(fn, [lambda r, col=col: ph_ref[r, col:col + 128]], [cw_ref[...]],
                                        lambda r, val_s=val_s: val_s[r, :], [into(val_s)], s, LANES, CONV_HALO)
            dph_ref[:, col:col + 128] = val_s[...].astype(BF16)

        @pl.when(head == 0)
        def _():
            dab_ref[...] = jnp.zeros_like(dab_ref)

        def add_dab(rows, value, add=False):
            dab_ref[rows, :] += value

        (dsc_ref[...],) = _rows_vjp(gate, [lambda r: ab_ref[r, :]], [sc_ref[...]], lambda r: gb_s[r, :], [add_dab], s, LANES, 0)

    t = bl * s
    cw_out = pl.BlockSpec((None, GDN_CONV, 128), lambda b, h: (b, 0, h))
    part = pl.BlockSpec((None, None, 8, 128), lambda b, h: (b, h, 0, 0))
    r_in, r_out, r_shapes, r_sems, r_args = _ride_specs(ride)
    outs = pl.pallas_call(
        _riding(core, 14, 7, 5, ride, (bl, GDN_HEADS)), name="gdn_bwd", grid=(bl, GDN_HEADS),
        in_specs=[
            pl.BlockSpec((s, 512), lambda b, h: (b, h)),
            pl.BlockSpec((s, 128), lambda b, h: (b, P_MLA // 128)),
            pl.BlockSpec((GDN_CONV, 128), lambda b, h: (0, h)),
            pl.BlockSpec((GDN_CONV, 128), lambda b, h: (0, GDN_HEADS + h)),
            pl.BlockSpec((GDN_CONV, 128), lambda b, h: (0, 2 * GDN_HEADS + h)),
            pl.BlockSpec((8, 128), lambda b, h: (0, 0)),
            pl.BlockSpec((1, 128), lambda b, h: (0, 0)),
            pl.BlockSpec((s, 128), lambda b, h: (b, h)),
            pl.BlockSpec((None, None, nc, GDN_DK, GDN_DV), lambda b, h: (b, h, 0, 0, 0)),
        ] + [pl.BlockSpec((s, 128), lambda b, h: (b, h))] * 5 + r_in,
        out_specs=[
            pl.BlockSpec((s, 512), lambda b, h: (b, h)),
            pl.BlockSpec((s, 128), lambda b, h: (b, 0)),
            cw_out, cw_out, cw_out, part, part,
        ] + r_out,
        out_shape=[
            jax.ShapeDtypeStruct((t, P_WIDTH), BF16),
            jax.ShapeDtypeStruct((t, 128), F32),
            jax.ShapeDtypeStruct((bl, GDN_CONV, 512), F32),
            jax.ShapeDtypeStruct((bl, GDN_CONV, 512), F32),
            jax.ShapeDtypeStruct((bl, GDN_CONV, 512), F32),
            jax.ShapeDtypeStruct((bl, GDN_HEADS, 8, 128), F32),
            jax.ShapeDtypeStruct((bl, GDN_HEADS, 8, 128), F32),
        ] + r_shapes,
        scratch_shapes=[pltpu.VMEM((s, 128), F32)] * 5 + r_sems,
        compiler_params=_params(("arbitrary", "arbitrary")),
    )(proj, proj, conv_w, conv_w, conv_w, sc, norm_g, o_raw, states, dcat, *qkvg, *r_args)
    return tuple(outs[:7]) + (list(outs[7:]),)


def _mla_prep_fwd(proj, qg, kvg, wq, wkv, cos, sin, s, tm):
    t = proj.shape[0]
    tm = min(tm, s)
    nps = s // tm
    const = lambda shape: pl.BlockSpec(shape, lambda i: (0, 0))

    def body(pm_ref, qg_ref, kvg_ref, wq_ref, wkv_ref, cos_ref, sin_ref, qf_ref, kvf_ref, kr_ref):
        qf, kvf, kr = _mla_prep(pm_ref[...], qg_ref[...], kvg_ref[...], wq_ref[...], wkv_ref[...], cos_ref[...], sin_ref[...])
        qf_ref[...], kvf_ref[...], kr_ref[...] = qf.astype(BF16), kvf.astype(BF16), kr.astype(BF16)

    return pl.pallas_call(
        body, name="mla_prep_fwd", grid=(t // tm,),
        in_specs=[
            pl.BlockSpec((tm, 1024), lambda i: (i, P_MLA // 1024)),
            const((1, MLA_Q_LORA)), const((1, MLA_KV_LORA)), const(wq.shape), const(wkv.shape),
            pl.BlockSpec((tm, 128), lambda i: (i % nps, 0)), pl.BlockSpec((tm, 128), lambda i: (i % nps, 0)),
        ],
        out_specs=[pl.BlockSpec((tm, 1024), lambda i: (i, 0)), pl.BlockSpec((tm, 1024), lambda i: (i, 0)),
                   pl.BlockSpec((tm, 128), lambda i: (i, 0))],
        out_shape=[jax.ShapeDtypeStruct((t, 1024), BF16), jax.ShapeDtypeStruct((t, 1024), BF16),
                   jax.ShapeDtypeStruct((t, 128), BF16)],
        compiler_params=_params(("parallel",)),
    )(proj, qg, kvg, wq, wkv, cos, sin)


def _mla_prep_bwd(proj, qg, kvg, wq, wkv, cos, sin, dqf, dkvf, dkr, dab, dproj, s, tm):
    t = proj.shape[0]
    tm = min(tm, s)
    nps = s // tm
    const = lambda shape: pl.BlockSpec(shape, lambda i: (0, 0))

    def body(pm_ref, qg_ref, kvg_ref, wq_ref, wkv_ref, cos_ref, sin_ref, dqf_ref, dkvf_ref, dkr_ref, dab_ref, dp_in,
             dp_ref, dqg_ref, dkvg_ref, dwq_ref, dwkv_ref):
        del dp_in
        fn = lambda pm, qg_, kvg_, wq_, wkv_: _mla_prep(pm, qg_, kvg_, wq_, wkv_, cos_ref[...], sin_ref[...])
        _, vjp = jax.vjp(fn, pm_ref[...], qg_ref[...], kvg_ref[...], wq_ref[...].astype(F32), wkv_ref[...].astype(F32))
        dpm, dqg, dkvg, dwq, dwkv = vjp((dqf_ref[...], dkvf_ref[...], dkr_ref[...]))
        dp_ref[...] = jnp.concatenate([dab_ref[...], dpm[:, 128:]], axis=1).astype(BF16)

        @pl.when(pl.program_id(0) == 0)
        def _():
            dqg_ref[...] = jnp.zeros_like(dqg_ref)
            dkvg_ref[...] = jnp.zeros_like(dkvg_ref)
            dwq_ref[...] = jnp.zeros_like(dwq_ref)
            dwkv_ref[...] = jnp.zeros_like(dwkv_ref)

        dqg_ref[...] += dqg
        dkvg_ref[...] += dkvg
        dwq_ref[...] += dwq
        dwkv_ref[...] += dwkv

    rows = lambda w: pl.BlockSpec((tm, w), lambda i: (i, 0))
    return pl.pallas_call(
        body, name="mla_prep_bwd", grid=(t // tm,),
        in_specs=[
            pl.BlockSpec((tm, 1024), lambda i: (i, P_MLA // 1024)),
            const((1, MLA_Q_LORA)), const((1, MLA_KV_LORA)), const(wq.shape), const(wkv.shape),
            pl.BlockSpec((tm, 128), lambda i: (i % nps, 0)), pl.BlockSpec((tm, 128), lambda i: (i % nps, 0)),
            rows(1024), rows(1024), rows(128), rows(128),
            pl.BlockSpec(memory_space=pl.ANY),
        ],
        out_specs=[pl.BlockSpec((tm, 1024), lambda i: (i, P_MLA // 1024)),
                   const((1, MLA_Q_LORA)), const((1, MLA_KV_LORA)), const(wq.shape), const(wkv.shape)],
        out_shape=[jax.ShapeDtypeStruct(dproj.shape, dproj.dtype),
                   jax.ShapeDtypeStruct((1, MLA_Q_LORA), F32), jax.ShapeDtypeStruct((1, MLA_KV_LORA), F32),
                   jax.ShapeDtypeStruct(wq.shape, F32), jax.ShapeDtypeStruct(wkv.shape, F32)],
        input_output_aliases={11: 0},
        compiler_params=_params(("arbitrary",)),
    )(proj, qg, kvg, wq, wkv, cos, sin, dqf, dkvf, dkr, dab, dproj)


def _attn_fwd(qf, kvf, kr, cat, bl, s, tq):
    tq = min(tq, s)
    nq = s // tq

    def body(q_ref, kv_ref, kr_ref, cat_in, o_ref, o32_ref, lse_ref):
        del cat_in

        def scores_of(i):
            keys = slice(0, (i + 1) * tq)
            return _scores(q_ref[i * tq:(i + 1) * tq, :], kv_ref[keys, 0:128], kr_ref[keys, :], i * tq, 0)

        ready = scores_of(0)
        for i in range(nq):
            scores = ready
            if i + 1 < nq:
                ready = scores_of(i + 1)
            rows = slice(i * tq, (i + 1) * tq)
            o, lse = _softmax_times(scores, kv_ref[0:(i + 1) * tq, 128:256])
            o_ref[rows, :] = o.astype(o_ref.dtype)
            o32_ref[rows, :] = o
            lse_ref[rows, :] = jnp.broadcast_to(lse, o.shape)

    t = bl * s
    head_cols = pl.BlockSpec((s, 128), lambda b, h: (b, h))
    return pl.pallas_call(
        body, name="attn_fwd", grid=(bl, MLA_HEADS),
        in_specs=[
            pl.BlockSpec((s, 256), lambda b, h: (b, h)),
            pl.BlockSpec((s, 256), lambda b, h: (b, h)),
            pl.BlockSpec((s, 128), lambda b, h: (b, 0)),
            pl.BlockSpec(memory_space=pl.ANY),
        ],
        out_specs=[pl.BlockSpec((s, 128), lambda b, h: (b, GDN_HEADS + h)), head_cols, head_cols],
        out_shape=[jax.ShapeDtypeStruct(cat.shape, cat.dtype)] + [jax.ShapeDtypeStruct((t, MLA_HEADS * MLA_V), F32)] * 2,
        input_output_aliases={3: 0},
        compiler_params=_params(("parallel", "parallel")),
    )(qf, kvf, kr, cat)


def _attn_bwd(qf, kvf, kr, dcat, o32, lse, bl, s, tq):
    tq = min(tq, s)
    nq = s // tq

    def body(q_ref, kv_ref, kr_ref, do_ref, o_ref, lse_ref, dq_ref, dkv_ref, dkr_ref):
        dkv_ref[...] = jnp.zeros_like(dkv_ref)

        @pl.when(pl.program_id(1) == 0)
        def _():
            dkr_ref[...] = jnp.zeros_like(dkr_ref)

        def block(i):
            rows = slice(i * tq, (i + 1) * tq)
            return q_ref[rows, :], do_ref[rows, :]

        def first_products(i, j):
            (q, d_o), keys = block(i), slice(j * tq, (j + 1) * tq)
            return _scores(q, kv_ref[keys, 0:128], kr_ref[keys, :], i * tq, j * tq), mm_nt(d_o, kv_ref[keys, 128:256])

        tiles = [(i, j) for i in range(nq) for j in range(i + 1)]
        ready = first_products(*tiles[0])
        for t, (i, j) in enumerate(tiles):
            scores, dp = ready
            if t + 1 < len(tiles):
                ready = first_products(*tiles[t + 1])
            rows, keys = slice(i * tq, (i + 1) * tq), slice(j * tq, (j + 1) * tq)
            q, d_o = block(i)
            if j == 0:
                delta = jnp.sum(d_o * o_ref[rows, :], axis=-1, keepdims=True)
                lse_i = jnp.max(lse_ref[rows, :], axis=-1, keepdims=True)
                dqn, dqr = jnp.zeros((tq, MLA_NOPE), F32), jnp.zeros((tq, 128), F32)
            p = jnp.exp(scores - lse_i)
            ds = p * (dp - delta) * ATT_SCALE
            dkv_ref[keys, 128:256] += mm_tn(p, d_o)
            dkv_ref[keys, 0:128] += mm_tn(ds, q[:, :128])
            dkr_ref[keys, :] += mm_tn(ds, q[:, 128:])
            dqn = dqn + mm(ds, kv_ref[keys, 0:128])
            dqr = dqr + mm(ds, kr_ref[keys, :])
            if j == i:
                dq_ref[rows, 0:128], dq_ref[rows, 128:256] = dqn, dqr

    t = bl * s
    head_cols = pl.BlockSpec((s, 128), lambda b, h: (b, h))
    return pl.pallas_call(
        body, name="attn_bwd", grid=(bl, MLA_HEADS),
        in_specs=[
            pl.BlockSpec((s, 256), lambda b, h: (b, h)),
            pl.BlockSpec((s, 256), lambda b, h: (b, h)),
            pl.BlockSpec((s, 128), lambda b, h: (b, 0)),
            pl.BlockSpec((s, 128), lambda b, h: (b, GDN_HEADS + h)),
            head_cols, head_cols,
        ],
        out_specs=[
            pl.BlockSpec((s, 256), lambda b, h: (b, h)),
            pl.BlockSpec((s, 256), lambda b, h: (b, h)),
            pl.BlockSpec((s, 128), lambda b, h: (b, 0)),
        ],
        out_shape=[jax.ShapeDtypeStruct((t, 1024), F32), jax.ShapeDtypeStruct((t, 1024), F32),
                   jax.ShapeDtypeStruct((t, 128), F32)],
        compiler_params=_params(("parallel", "arbitrary")),
    )(qf, kvf, kr, dcat, o32, lse)


def _mix_ln1_fwd(x, cat, w_o, g, b, tm):
    t = x.shape[0]
    tm = min(tm, t)

    def body(x_ref, cat_ref, w_ref, g_ref, b_ref, r_ref, h_ref, hb_ref, xb_ref):
        r = ALPHA * x_ref[...] + _dot(cat_ref[...], w_ref[...], 1, 0)
        r_ref[...] = r
        h = _layernorm(r, g_ref[...], b_ref[...])
        h_ref[...] = h
        hb_ref[...] = h.astype(BF16)
        xb_ref[...] = x_ref[...].astype(BF16)

    rows = pl.BlockSpec((tm, D_MODEL), lambda i: (i, 0))
    vec = pl.BlockSpec((1, D_MODEL), lambda i: (0, 0))
    return pl.pallas_call(
        body, name="mix_ln1_fwd", grid=(t // tm,),
        in_specs=[rows, pl.BlockSpec((tm, cat.shape[1]), lambda i: (i, 0)), pl.BlockSpec(w_o.shape, lambda i: (0, 0)), vec, vec],
        out_specs=[rows] * 4,
        out_shape=[jax.ShapeDtypeStruct(x.shape, F32)] * 2 + [jax.ShapeDtypeStruct(x.shape, BF16)] * 2,
        compiler_params=_params(("parallel",)),
    )(x, cat, w_o, g, b)


def _ln1_bwd(r1, dr2, da, dgpre, w_gate, g, b, tm):
    t = r1.shape[0]
    tm = min(tm, t)

    def body(r_ref, d2_ref, da_ref, dgp_ref, wg_ref, g_ref, b_ref, dr_ref, drb_ref, dg_ref, dbias_ref):
        dh = ALPHA * d2_ref[...] + da_ref[...] + _dot(dgp_ref[...], wg_ref[...], 1, 1)
        _, vjp = jax.vjp(_layernorm, r_ref[...], g_ref[...], b_ref[...])
        dr, dg, dbias = vjp(dh)
        dr_ref[...] = dr
        drb_ref[...] = dr.astype(BF16)

        @pl.when(pl.program_id(0) == 0)
        def _():
            dg_ref[...] = jnp.zeros_like(dg_ref)
            dbias_ref[...] = jnp.zeros_like(dbias_ref)

        dg_ref[...] += dg
        dbias_ref[...] += dbias

    rows = pl.BlockSpec((tm, D_MODEL), lambda i: (i, 0))
    vec = pl.BlockSpec((1, D_MODEL), lambda i: (0, 0))
    return pl.pallas_call(
        body, name="ln1_bwd", grid=(t // tm,),
        in_specs=[rows] * 4 + [pl.BlockSpec(w_gate.shape, lambda i: (0, 0)), vec, vec], out_specs=[rows, rows, vec, vec],
        out_shape=[jax.ShapeDtypeStruct(r1.shape, F32), jax.ShapeDtypeStruct(r1.shape, BF16)]
        + [jax.ShapeDtypeStruct((1, D_MODEL), F32)] * 2,
        compiler_params=_params(("arbitrary",)),
    )(r1, dr2, da, dgpre, w_gate, g, b)


def _ffn_act_fwd(u, conv_w, conv_b, bl, s, cb):
    nj = D_FF // cb

    def body(ug_ref, uu_ref, wg_ref, wu_ref, bg_ref, bu_ref, act_ref):
        def store(rows, act):
            act_ref[rows, :] = act.astype(BF16)

        _rows_apply(_ffn_act, [lambda r: ug_ref[r, :], lambda r: uu_ref[r, :]],
                    [wg_ref[...], wu_ref[...], bg_ref[...], bu_ref[...]], store, s, cb, CONV_HALO)

    return pl.pallas_call(
        body, name="ffn_act_fwd", grid=(bl, nj),
        in_specs=[
            pl.BlockSpec((s, cb), lambda b, j: (b, j)), pl.BlockSpec((s, cb), lambda b, j: (b, nj + j)),
            pl.BlockSpec((FFN_CONV, cb), lambda b, j: (0, j)), pl.BlockSpec((FFN_CONV, cb), lambda b, j: (0, nj + j)),
            pl.BlockSpec((1, cb), lambda b, j: (0, j)), pl.BlockSpec((1, cb), lambda b, j: (0, nj + j)),
        ],
        out_specs=pl.BlockSpec((s, cb), lambda b, j: (b, j)),
        out_shape=jax.ShapeDtypeStruct((bl * s, D_FF), BF16),
        compiler_params=_params(("parallel", "parallel")),
    )(u, u, conv_w, conv_w, conv_b, conv_b)


def _ffn_act_bwd(u, conv_w, conv_b, dact, bl, s, cb):
    nj = D_FF // cb

    def body(ug_ref, uu_ref, wg_ref, wu_ref, bg_ref, bu_ref, da_ref, du_ref, dwg_ref, dwu_ref, dbg_ref, dbu_ref, acc):
        def store_into(half):
            def store(rows, value, add):
                if add:
                    acc[half, rows, :] += value
                else:
                    acc[half, rows, :] = value
            return store

        dwg_ref[...], dwu_ref[...], dbg_ref[...], dbu_ref[...] = _rows_vjp(
            _ffn_act, [lambda r: ug_ref[r, :], lambda r: uu_ref[r, :]], [wg_ref[...], wu_ref[...], bg_ref[...], bu_ref[...]],
            lambda r: da_ref[r, :], [store_into(0), store_into(1)], s, cb, CONV_HALO)
        du_ref[...] = acc[...].astype(BF16)

    t = bl * s
    blk = pl.BlockSpec((s, cb), lambda b, j: (b, j))
    wpart = pl.BlockSpec((None, FFN_CONV, cb), lambda b, j: (b, 0, j))
    bpart = pl.BlockSpec((None, 1, cb), lambda b, j: (b, 0, j))
    return pl.pallas_call(
        body, name="ffn_act_bwd", grid=(bl, nj),
        in_specs=[
            blk, pl.BlockSpec((s, cb), lambda b, j: (b, nj + j)),
            pl.BlockSpec((FFN_CONV, cb), lambda b, j: (0, j)), pl.BlockSpec((FFN_CONV, cb), lambda b, j: (0, nj + j)),
            pl.BlockSpec((1, cb), lambda b, j: (0, j)), pl.BlockSpec((1, cb), lambda b, j: (0, nj + j)),
            blk,
        ],
        out_specs=[pl.BlockSpec((2, s, cb), lambda b, j: (0, b, j)), wpart, wpart, bpart, bpart],
        out_shape=[jax.ShapeDtypeStruct((2, t, D_FF), BF16)] + [jax.ShapeDtypeStruct((bl, FFN_CONV, D_FF), F32)] * 2
        + [jax.ShapeDtypeStruct((bl, 1, D_FF), F32)] * 2,
        scratch_shapes=[pltpu.VMEM((2, s, cb), F32)],
        compiler_params=_params(("parallel", "parallel")),
    )(u, u, conv_w, conv_w, conv_b, conv_b, dact)


def _head(h1, h1b, ffn, pt, w_gate, w_proj, bgate, g2, b2, target, tm):
    t = h1.shape[0]
    tm = min(tm, t)

    def body(h1_ref, h1b_ref, ffn_ref, pt_ref, wg_ref, wp_ref, bg_ref, g2_ref, b2_ref, tg_ref,
             dr_ref, drb_ref, dgp_ref, dpp_ref, loss_ref, dbg_ref, dg2_ref, db2_ref):
        fn = functools.partial(_head_loss, target=tg_ref[...])
        gpre, pp = _dot(h1b_ref[...], wg_ref[...], 1, 0), _dot(pt_ref[...], wp_ref[...], 1, 0)
        loss, vjp = jax.vjp(fn, h1_ref[...], ffn_ref[...], gpre, pp, bg_ref[...], g2_ref[...], b2_ref[...])
        _, dffn, dgp, dpp, dbg, dg2, db2 = vjp(jnp.ones((1, 1), F32))
        dr_ref[...] = dffn
        drb_ref[...], dgp_ref[...], dpp_ref[...] = dffn.astype(BF16), dgp.astype(BF16), dpp.astype(BF16)

        @pl.when(pl.program_id(0) == 0)
        def _():
            loss_ref[...] = jnp.zeros_like(loss_ref)
            dbg_ref[...] = jnp.zeros_like(dbg_ref)
            dg2_ref[...] = jnp.zeros_like(dg2_ref)
            db2_ref[...] = jnp.zeros_like(db2_ref)

        loss_ref[...] += jnp.broadcast_to(loss, loss_ref.shape)
        dbg_ref[...] += dbg
        dg2_ref[...] += dg2
        db2_ref[...] += db2

    rows = pl.BlockSpec((tm, D_MODEL), lambda i: (i, 0))
    vec = pl.BlockSpec((1, D_MODEL), lambda i: (0, 0))
    return pl.pallas_call(
        body, name="head", grid=(t // tm,),
        in_specs=[rows, rows, rows, pl.BlockSpec((tm, pt.shape[1]), lambda i: (i, 0)),
                  pl.BlockSpec(w_gate.shape, lambda i: (0, 0)), pl.BlockSpec(w_proj.shape, lambda i: (0, 0))]
        + [vec] * 3 + [rows],
        out_specs=[rows] * 4 + [pl.BlockSpec((8, 128), lambda i: (0, 0))] + [vec] * 3,
        out_shape=[jax.ShapeDtypeStruct(h1.shape, F32)] + [jax.ShapeDtypeStruct(h1.shape, BF16)] * 3
        + [jax.ShapeDtypeStruct((8, 128), F32)]
        + [jax.ShapeDtypeStruct((1, D_MODEL), F32)] * 3,
        compiler_params=_params(("arbitrary",)),
    )(h1, h1b, ffn, pt, w_gate, w_proj, bgate, g2, b2, target)


def _adam_update(g, w_ref, m_ref, v_ref, g_ref, d_ref, nm_ref, nv_ref):
    m2 = ADAM_B1 * m_ref[...] + (1.0 - ADAM_B1) * g
    v2 = ADAM_B2 * v_ref[...] + (1.0 - ADAM_B2) * jnp.square(g)
    m_hat = m2 / (1.0 - ADAM_B1 ** ADAM_STEP)
    v_hat = v2 / (1.0 - ADAM_B2 ** ADAM_STEP)
    g_ref[...] = g
    d_ref[...] = -ADAM_LR * (m_hat / (jnp.sqrt(v_hat) + ADAM_EPS) + ADAM_WD * w_ref[...])
    nm_ref[...] = m2
    nv_ref[...] = v2


def _row_tile(rows, cols, limit_bytes=512 * 1024):
    best = None
    for t in range(HALF_ROWS_QUANTUM, rows + 1, HALF_ROWS_QUANTUM):
        if rows % t == 0 and t * cols * 4 <= limit_bytes:
            best = t
    return best or rows


def _adamw_reduced(recv, w, m, v, name):
    a, b = w.shape
    ta = _row_tile(a, b)

    def body(recv_ref, w_ref, m_ref, v_ref, g_ref, d_ref, nm_ref, nv_ref):
        c = lax.axis_index("c")
        for core in range(2):
            @pl.when(c == core)
            def _():
                got = [recv_ref[k].astype(F32) for k in range(N_DEV)]
                same = [got[7], got[0], got[1], got[2]]
                other = got[3:7]
                core0, core1 = (same, other) if core == 0 else (other, same)
                g = core0[0] + core1[0]
                for r in range(1, N_CHIPS):
                    g = (g + core0[r]) + core1[r]
                _adam_update(g, w_ref, m_ref, v_ref, g_ref, d_ref, nm_ref, nv_ref)

    blk = pl.BlockSpec((ta, b), lambda i: (i, 0))
    return pl.pallas_call(
        body, name=name, grid=(a // ta,),
        in_specs=[pl.BlockSpec((N_DEV, ta, b), lambda i: (0, i, 0)), blk, blk, blk], out_specs=[blk] * 4,
        out_shape=[jax.ShapeDtypeStruct(w.shape, F32)] * 4, compiler_params=_params(("parallel",)),
    )(recv, w, m, v)


def _adamw_small(g, w, m, v):
    def body(g_in, w_ref, m_ref, v_ref, g_ref, d_ref, nm_ref, nv_ref):
        _adam_update(g_in[...], w_ref, m_ref, v_ref, g_ref, d_ref, nm_ref, nv_ref)

    blk = pl.BlockSpec(w.shape, lambda i: (0, 0))
    return pl.pallas_call(
        body, name="adamw_small", grid=(1,), in_specs=[blk] * 4, out_specs=[blk] * 4,
        out_shape=[jax.ShapeDtypeStruct(w.shape, F32)] * 4, compiler_params=_params(("arbitrary",)),
    )(g, w, m, v)


def _remote(src, dst, send_sem, recv_sem, device):
    return pltpu.make_async_remote_copy(src_ref=src, dst_ref=dst, send_sem=send_sem, recv_sem=recv_sem,
                                        device_id=device, device_id_type=MESH)


def _place():
    x, y, c = lax.axis_index("x"), lax.axis_index("y"), lax.axis_index("c")
    return x, y, c, 2 * x + y, [(1 - x, y), (x, 1 - y), (1 - x, 1 - y)]


HBM_REF = pl.BlockSpec(memory_space=pl.ANY)
HALF_ROWS_QUANTUM = 16


def _gather_sems(n):
    return [pltpu.SemaphoreType.DMA((3 * n,))] * 4 + [pltpu.SemaphoreType.DMA((n,))]


def _gather_copies(ins, outs, sems):
    send_s, recv_s, fsend_s, frecv_s, local_s = sems
    x, y, c, me, chips = _place()
    local, sends, steps = [], [], []
    for i, (src, dst) in enumerate(zip(ins, outs)):
        local.append(pltpu.make_async_copy(src, dst.at[me], local_s.at[i]))
        half = src.shape[0] // 2
        split = src.shape[0] % (2 * HALF_ROWS_QUANTUM) == 0
        if split:
            mine = pl.ds(pl.multiple_of(c * half, HALF_ROWS_QUANTUM), half)
            theirs = pl.ds(pl.multiple_of((1 - c) * half, HALF_ROWS_QUANTUM), half)
        for r, (px, py) in enumerate(chips):
            k, peer = 3 * i + r, 2 * px + py
            if split:
                sends.append(_remote(src.at[mine], dst.at[me, mine], send_s.at[k], recv_s.at[k], (px, py, c)))
                landed = dst.at[peer, mine]
                steps.append((_remote(src.at[mine], landed, send_s.at[k], recv_s.at[k], (px, py, c)),
                              _remote(landed, landed, fsend_s.at[k], frecv_s.at[k], (x, y, 1 - c)),
                              _remote(dst.at[peer, theirs], dst.at[peer, theirs], fsend_s.at[k], frecv_s.at[k], (x, y, 1 - c))))
            else:
                sends.append(_remote(src, dst.at[me], send_s.at[k], recv_s.at[k], (px, py, c)))
                steps.append((_remote(src, dst.at[peer], send_s.at[k], recv_s.at[k], (px, py, c)), None, None))
    return local, sends, steps


def _scatter_sems(n):
    return [pltpu.SemaphoreType.DMA((4 * n,))] * 2 + [pltpu.SemaphoreType.DMA((3 * n,))] * 2 + [pltpu.SemaphoreType.DMA((n,))]


def _scatter_copies(ins, outs, sems):
    send_s, recv_s, fsend_s, frecv_s, local_s = sems
    x, y, c, me, chips = _place()
    local, sends, steps = [], [], []
    for i, (src, dst) in enumerate(zip(ins, outs)):
        local.append(pltpu.make_async_copy(src.at[me], dst.at[N_DEV - 1], local_s.at[i]))
        for r, (px, py) in enumerate(chips):
            k = 4 * i + r
            cp = _remote(src.at[2 * px + py], dst.at[r], send_s.at[k], recv_s.at[k], (px, py, c))
            fwd = _remote(dst.at[r], dst.at[4 + r], fsend_s.at[3 * i + r], frecv_s.at[3 * i + r], (x, y, 1 - c))
            sends.append(cp)
            steps.append((cp, fwd, fwd))
        k = 4 * i + 3
        cp = _remote(src.at[me], dst.at[3], send_s.at[k], recv_s.at[k], (x, y, 1 - c))
        sends.append(cp)
        steps.append((cp, None, None))
    return local, sends, steps


def _exchange_start(plan):
    local, sends, _ = plan
    for cp in local + sends:
        cp.start()


def _exchange_pass_on(plan):
    for arrival, pass_on, _ in plan[2]:
        arrival.wait_recv()
        if pass_on is not None:
            pass_on.start()


def _exchange_finish(plan):
    local, sends, steps = plan
    for _, pass_on, passed in steps:
        if pass_on is not None:
            passed.wait_recv()
    for cp in sends:
        cp.wait_send()
    for _, pass_on, _ in steps:
        if pass_on is not None:
            pass_on.wait_send()
    for cp in local:
        cp.wait()


def _exchange_call(arrays, copies, sems, out_shapes, name):
    n = len(arrays)

    def body(*refs):
        plan = copies(refs[:n], refs[n:2 * n], refs[2 * n:])
        _exchange_start(plan)
        _exchange_pass_on(plan)
        _exchange_finish(plan)

    return pl.pallas_call(
        body, name=name, in_specs=[HBM_REF] * n, out_specs=[HBM_REF] * n, out_shape=out_shapes,
        scratch_shapes=sems(n), compiler_params=pltpu.CompilerParams(has_side_effects=True),
    )(*arrays)


def _gather_call(shards, name):
    shapes = [jax.ShapeDtypeStruct((N_CHIPS,) + a.shape, a.dtype) for a in shards]
    return _exchange_call(shards, _gather_copies, _gather_sems, shapes, name)


def _all_reduce_small(a):
    def body(in_ref, out_ref, slots, send_sems, recv_sems):
        x, y, c = lax.axis_index("x"), lax.axis_index("y"), lax.axis_index("c")
        me = 4 * x + 2 * y + c
        slots[0] = in_ref[...]
        sends = []
        for r in range(1, N_DEV):
            peer = (x ^ (r >> 2), y ^ ((r >> 1) & 1), c ^ (r & 1))
            sends.append(pltpu.make_async_remote_copy(src_ref=in_ref, dst_ref=slots.at[r], send_sem=send_sems.at[r],
                                                      recv_sem=recv_sems.at[r], device_id=peer, device_id_type=MESH))
        for cp in sends:
            cp.start()
        for cp in sends:
            cp.wait_recv()
        acc = slots[me]
        for dev in range(1, N_DEV):
            acc = acc + slots[dev ^ me]
        out_ref[...] = acc
        for cp in sends:
            cp.wait_send()

    return pl.pallas_call(
        body, name="small_all_reduce",
        in_specs=[pl.BlockSpec(memory_space=pltpu.VMEM)], out_specs=pl.BlockSpec(memory_space=pltpu.VMEM),
        out_shape=jax.ShapeDtypeStruct(a.shape, a.dtype),
        scratch_shapes=[pltpu.VMEM((N_DEV,) + a.shape, a.dtype), pltpu.SemaphoreType.DMA((N_DEV,)),
                        pltpu.SemaphoreType.DMA((N_DEV,))],
        compiler_params=pltpu.CompilerParams(has_side_effects=True),
    )(a)


SHARDED = ["w_in", "mla_w_q_up", "mla_w_kv_up", "w_out", "ffn_w_up", "ffn_w_down", "ple_w_gate", "ple_w_proj",
           "gdn_conv_w", "ffn_conv_w"]
SHARD_AXIS = {"w_in": 1, "mla_w_q_up": 1, "mla_w_kv_up": 1, "w_out": 0, "ffn_w_up": 1, "ffn_w_down": 0,
              "ple_w_gate": 0, "ple_w_proj": 1, "gdn_conv_w": 1, "ffn_conv_w": 1}
SMALL = ["gdn_a_log", "gdn_dt_bias", "gdn_norm_g", "mla_q_norm_g", "mla_kv_norm_g", "ln1_g", "ln1_b", "ffn_conv_b",
         "ple_b_gate", "ln2_g", "ln2_b"]
WEIGHTS = ["w_in", "gdn_conv_w", "gdn_a_log", "gdn_dt_bias", "gdn_norm_g", "mla_q_norm_g", "mla_w_q_up", "mla_kv_norm_g",
           "mla_w_kv_up", "w_out", "ln1_g", "ln1_b", "ffn_w_up", "ffn_conv_w", "ffn_conv_b", "ffn_w_down", "ple_w_gate",
           "ple_b_gate", "ple_w_proj", "ln2_g", "ln2_b"]
F32_ON_WIRE = ("gdn_conv_w", "ffn_conv_w")
GATHER_EARLY = ["w_in", "gdn_conv_w"]
GATHER_LATE = ["mla_w_q_up", "mla_w_kv_up", "w_out", "ffn_w_up", "ffn_conv_w", "ffn_w_down", "ple_w_gate", "ple_w_proj"]
SCATTER_EARLY = ["ffn_w_up", "ffn_conv_w", "ffn_w_down", "ple_w_gate", "ple_w_proj", "w_out"]
SCATTER_LATE = ["w_in", "gdn_conv_w", "mla_w_q_up", "mla_w_kv_up"]
PACK_COLS = 1024
PACK_ROW_TILE = 8


def _join_blocks(blocks, axis):
    n, a, b = blocks.shape
    if axis == 0:
        return blocks.reshape(n * a, b)
    return jnp.transpose(blocks, (1, 0, 2)).reshape(a, n * b)


def _split_blocks(full, axis):
    if axis == 0:
        return full.reshape(N_CHIPS, full.shape[0] // N_CHIPS, full.shape[1])
    a, nb = full.shape
    return jnp.transpose(full.reshape(a, N_CHIPS, nb // N_CHIPS), (1, 0, 2))


def _pack(arrays):
    flat = jnp.concatenate([a.reshape(-1) for a in arrays])
    quantum = PACK_COLS * PACK_ROW_TILE
    padded = -(-flat.shape[0] // quantum) * quantum
    return jnp.pad(flat, (0, padded - flat.shape[0])).reshape(-1, PACK_COLS)


def _unpack(packed, shapes):
    flat = packed.reshape(-1)
    out, off = [], 0
    for shp in shapes:
        n = int(np.prod(shp))
        out.append(flat[off:off + n].reshape(shp))
        off += n
    return out


def kernel(x, p, w_in, gdn_conv_w, gdn_a_log, gdn_dt_bias, gdn_norm_g, mla_q_norm_g, mla_w_q_up, mla_kv_norm_g, mla_w_kv_up, w_out, ln1_g, ln1_b, ffn_w_up, ffn_conv_w, ffn_conv_b, ffn_w_down, ple_w_gate, ple_b_gate, ple_w_proj, ln2_g, ln2_b, loss_target, m_w_in, m_gdn_conv_w, m_gdn_a_log, m_gdn_dt_bias, m_gdn_norm_g, m_mla_q_norm_g, m_mla_w_q_up, m_mla_kv_norm_g, m_mla_w_kv_up, m_w_out, m_ln1_g, m_ln1_b, m_ffn_w_up, m_ffn_conv_w, m_ffn_conv_b, m_ffn_w_down, m_ple_w_gate, m_ple_b_gate, m_ple_w_proj, m_ln2_g, m_ln2_b, v_w_in, v_gdn_conv_w, v_gdn_a_log, v_gdn_dt_bias, v_gdn_norm_g, v_mla_q_norm_g, v_mla_w_q_up, v_mla_kv_norm_g, v_mla_w_kv_up, v_w_out, v_ln1_g, v_ln1_b, v_ffn_w_up, v_ffn_conv_w, v_ffn_conv_b, v_ffn_w_down, v_ple_w_gate, v_ple_b_gate, v_ple_w_proj, v_ln2_g, v_ln2_b):
    given = dict(locals())
    wsh = {n: given[n][0] for n in WEIGHTS}
    msh = {n: given["m_" + n][0] for n in WEIGHTS}
    vsh = {n: given["v_" + n][0] for n in WEIGHTS}
    bl, s, _ = x.shape
    t = bl * s
    xt = x.reshape(t, D_MODEL)
    pt = p.reshape(t, PLE_DIM)
    target = loss_target.reshape(t, D_MODEL)

    wire = lambda n: wsh[n] if n in F32_ON_WIRE else wsh[n].astype(BF16)
    early = _gather_call([wire(n) for n in GATHER_EARLY], "weights_gather_early")
    full = {n: _join_blocks(g, SHARD_AXIS[n]) for n, g in zip(GATHER_EARLY, early)}
    late_shards = [wire(n) for n in GATHER_LATE]
    late_ride = (late_shards, _gather_copies, _gather_sems,
                 [jax.ShapeDtypeStruct((N_CHIPS,) + a.shape, a.dtype) for a in late_shards], 0.75)

    in_cols, q_cols = _w_in_cols(), _w_q_cols()
    w_in_p = _pad_cols(full["w_in"], in_cols)
    gconv = full["gdn_conv_w"]
    row = lambda a: a.reshape(1, -1)
    sc = jnp.zeros((8, 128), F32).at[0, :GDN_HEADS].set(wsh["gdn_a_log"]).at[1, :GDN_HEADS].set(wsh["gdn_dt_bias"])
    norm_g, qg, kvg = row(wsh["gdn_norm_g"]), row(wsh["mla_q_norm_g"]), row(wsh["mla_kv_norm_g"])
    g1, b1, g2, b2 = row(wsh["ln1_g"]), row(wsh["ln1_b"]), row(wsh["ln2_g"]), row(wsh["ln2_b"])
    fbias, bgate = row(wsh["ffn_conv_b"]), row(wsh["ple_b_gate"])

    inv = ROPE_THETA ** (-jnp.arange(0, MLA_ROPE, 2, dtype=F32) / MLA_ROPE)
    ang = jnp.arange(s, dtype=F32)[:, None] * inv[None, :]
    zero = jnp.zeros_like(ang)
    cos_t = jnp.concatenate([jnp.cos(ang), zero, jnp.cos(ang), zero], axis=1)
    sin_t = jnp.concatenate([-jnp.sin(ang), zero, jnp.sin(ang), zero], axis=1)

    proj = _matmul(xt, w_in_p, name="proj", tm=1024)
    cat, o_raw, states, qkvg, late = _gdn_fwd(proj, gconv, sc, norm_g, bl, s, late_ride)
    w_up = late[GATHER_LATE.index("ffn_w_up")]
    full.update({n: _join_blocks(g, SHARD_AXIS[n]) for n, g in zip(GATHER_LATE, late) if n != "ffn_w_up"})
    w_o, w_down = full["w_out"], full["ffn_w_down"]
    w_gate, w_proj, fconv = full["ple_w_gate"], full["ple_w_proj"], full["ffn_conv_w"]
    w_q_p, w_kv = _pad_cols(full["mla_w_q_up"], q_cols), full["mla_w_kv_up"]
    qf, kvf, kr = _mla_prep_fwd(proj, qg, kvg, w_q_p, w_kv, cos_t, sin_t, s, 256)
    cat, attn_o32, attn_lse = _attn_fwd(qf, kvf, kr, cat, bl, s, 512)
    wide = dict(tm=1024, tn=1024)
    r1, h1, h1b, xb = _mix_ln1_fwd(xt, cat, w_o, g1, b1, 512)
    u = _matmul(h1b, w_up, name="ffn_up", tm=1024, tn=1408, b_parts=N_CHIPS)
    act = _ffn_act_fwd(u, fconv, fbias, bl, s, 256)
    ffn = _matmul(act, w_down, name="ffn_down", tk=1408, **wide)
    dr2, dr2b, dgpre, dpp, loss_acc, dbgate, dg2, db2 = _head(h1, h1b, ffn, pt, w_gate, w_proj, bgate, g2, b2, target, 256)

    dact = _matmul(dr2b, w_down, name="d_act", tb=True, tm=1024, tn=1408)
    long_k = dict(ta=True, tk=2048)
    d_w_down = _matmul(act, dr2b, name="dw_down", tm=1408, tn=1024, **long_k)
    du, dfcw_g, dfcw_u, dfcb_g, dfcb_u = _ffn_act_bwd(u, fconv, fbias, dact, bl, s, 256)
    dh1_a = _matmul(du, w_up, name="dh1_ffn", tb=True, tk=1408, a_halves=True, b_parts=N_CHIPS, **wide)
    d_w_up = _matmul(h1b, du, name="dw_up", tn=1408, b_parts=2, out_parts=N_CHIPS, out_dtype=BF16, **long_k)
    d_w_gate = _matmul(h1b, dgpre, name="dw_gate", **long_k, **wide)
    d_w_proj = _matmul(pt, dpp, name="dw_proj", ta=True, tn=1024)
    dr1, dr1b, dg1, db1 = _ln1_bwd(r1, dr2, dh1_a, dgpre, w_gate, g1, b1, 256)
    dcat = _matmul(dr1b, w_o, name="d_cat", tb=True, **wide)
    d_w_o = _matmul(cat, dr1b, name="dw_out", **long_k, **wide)

    gfull = {
        "ffn_w_down": d_w_down, "ple_w_gate": d_w_gate, "ple_w_proj": d_w_proj, "w_out": d_w_o,
        "ffn_conv_w": jnp.concatenate([jnp.sum(dfcw_g, 0), jnp.sum(dfcw_u, 0)], axis=1),
    }
    slabs = {n: _split_blocks(g, SHARD_AXIS[n]).astype(BF16) for n, g in gfull.items()}
    slabs["ffn_w_up"] = d_w_up
    early_slabs = [slabs[n] for n in SCATTER_EARLY]
    early_ride = (early_slabs, _scatter_copies, _scatter_sems,
                  [jax.ShapeDtypeStruct((N_DEV,) + a.shape[1:], a.dtype) for a in early_slabs], 0.7)
    dproj, dab, dcwq, dcwk, dcwv, dsc, dng, early_recv = _gdn_bwd(proj, gconv, sc, norm_g, o_raw, states, qkvg, dcat, bl, s,
                                                                  early_ride)
    received = dict(zip(SCATTER_EARLY, early_recv))
    dqf, dkvf, dkr = _attn_bwd(qf, kvf, kr, dcat, attn_o32, attn_lse, bl, s, 256)
    dproj, dqg, dkvg, d_w_q_p, d_w_kv = _mla_prep_bwd(proj, qg, kvg, w_q_p, w_kv, cos_t, sin_t, dqf, dkvf, dkr, dab, dproj, s, 256)
    d_w_in_p = _matmul(xb, dproj, name="dw_in", **long_k, **wide)

    gfull.update({
        "w_in": _unpad_cols(d_w_in_p, in_cols, D_IN),
        "mla_w_q_up": _unpad_cols(d_w_q_p, q_cols, MLA_HEADS * (MLA_NOPE + MLA_ROPE)),
        "mla_w_kv_up": d_w_kv,
        "gdn_conv_w": jnp.concatenate([jnp.sum(dcwq, 0), jnp.sum(dcwk, 0), jnp.sum(dcwv, 0)], axis=1),
    })
    slabs.update({n: _split_blocks(gfull[n], SHARD_AXIS[n]).astype(BF16) for n in SCATTER_LATE})
    late_slabs = [slabs[n] for n in SCATTER_LATE]
    late_scatter = (late_slabs, _scatter_copies, _scatter_sems,
                    [jax.ShapeDtypeStruct((N_DEV,) + a.shape[1:], a.dtype) for a in late_slabs], 0.85)
    grad_x, late_recv = _matmul(dproj, w_in_p, name="d_x", tb=True, add=dr1, add_scale=ALPHA, ride=late_scatter, **wide)
    received.update(zip(SCATTER_LATE, late_recv))
    dsc_sum = jnp.sum(dsc, axis=(0, 1))
    gsmall = {
        "gdn_a_log": dsc_sum[0, :GDN_HEADS], "gdn_dt_bias": dsc_sum[1, :GDN_HEADS],
        "gdn_norm_g": jnp.sum(dng[:, :, 0, :], axis=(0, 1)),
        "mla_q_norm_g": dqg[0], "mla_kv_norm_g": dkvg[0], "ln1_g": dg1[0], "ln1_b": db1[0],
        "ffn_conv_b": jnp.concatenate([jnp.sum(dfcb_g, 0), jnp.sum(dfcb_u, 0)], axis=1)[0],
        "ple_b_gate": dbgate[0], "ln2_g": dg2[0], "ln2_b": db2[0],
    }

    big = [{}, {}, {}, {}]
    for n in SHARDED:
        for kind, val in enumerate(_adamw_reduced(received[n], wsh[n], msh[n], vsh[n], "adamw_" + n)):
            big[kind][n] = val

    small_shapes = [wsh[n].shape for n in SMALL]
    gsum = _all_reduce_small(_pack([gsmall[n] for n in SMALL]))
    spacks = _adamw_small(gsum, _pack([wsh[n] for n in SMALL]), _pack([msh[n] for n in SMALL]), _pack([vsh[n] for n in SMALL]))
    small = [dict(zip(SMALL, _unpack(pk, small_shapes))) for pk in spacks]

    loss = lax.psum(loss_acc[0, 0], ("x", "y", "c"))
    outs = [loss, grad_x.reshape(x.shape)]
    for kind in range(4):
        for n in WEIGHTS:
            val = big[kind][n] if n in big[kind] else small[kind][n]
            outs.append(val[None])
    return tuple(outs)
```

```python
import functools
import math

import numpy as np
import jax
import jax.numpy as jnp
from jax import lax
from jax.experimental import pallas as pl
from jax.experimental.pallas import tpu as pltpu

F32 = jnp.float32
BF16 = jnp.bfloat16

D_MODEL = 1024
CHUNK = 64
PLE_DIM = 256
GDN_HEADS = 4
GDN_DK = 128
GDN_DV = 128
GDN_CONV = 4
MLA_HEADS = 4
MLA_NOPE = 128
MLA_ROPE = 64
MLA_V = 128
MLA_Q_LORA = 384
MLA_KV_LORA = 256
ROPE_THETA = 10000.0
D_FF = 2816
FFN_CONV = 3
DEPTH = 1
ALPHA = (2.0 * DEPTH) ** 0.25
NORM_EPS = 1e-6
GDN_QK = GDN_HEADS * GDN_DK
GDN_VW = GDN_HEADS * GDN_DV
D_IN = 2 * GDN_QK + 2 * GDN_VW + 2 * GDN_HEADS + MLA_Q_LORA + MLA_KV_LORA + MLA_ROPE
ATT_SCALE = (MLA_NOPE + MLA_ROPE) ** -0.5

ADAM_LR = 0.001
ADAM_B1 = 0.9
ADAM_B2 = 0.999
ADAM_EPS = 1e-08
ADAM_WD = 0.01
ADAM_STEP = 10

LANES = 128
VMEM_LIMIT = 60 * 1024 * 1024
GDN_FWD_GROUP = 16
GDN_BWD_GROUP = 16
N_CHIPS = 4
N_DEV = 8

P_WIDTH = 3072
P_MLA = 2048
MESH = pl.DeviceIdType.MESH


def _rope_slot(j):
    return j if j < MLA_ROPE // 2 else 64 + (j - MLA_ROPE // 2)


def _w_in_cols():
    idx = -np.ones((P_WIDTH,), np.int64)
    for h in range(GDN_HEADS):
        base = h * 512
        idx[base:base + 128] = np.arange(128) + h * GDN_DK
        idx[base + 128:base + 256] = np.arange(128) + GDN_QK + h * GDN_DK
        idx[base + 256:base + 384] = np.arange(128) + 2 * GDN_QK + h * GDN_DV
        idx[base + 384:base + 512] = np.arange(128) + 2 * GDN_QK + GDN_VW + h * GDN_DV
    o_a = 2 * GDN_QK + 2 * GDN_VW
    idx[P_MLA:P_MLA + 2 * GDN_HEADS] = np.arange(2 * GDN_HEADS) + o_a
    o_cq = o_a + 2 * GDN_HEADS
    idx[P_MLA + 128:P_MLA + 512] = np.arange(MLA_Q_LORA) + o_cq
    o_ckv = o_cq + MLA_Q_LORA
    idx[P_MLA + 512:P_MLA + 768] = np.arange(MLA_KV_LORA) + o_ckv
    o_kr = o_ckv + MLA_KV_LORA
    for j in range(MLA_ROPE):
        idx[P_MLA + 768 + _rope_slot(j)] = o_kr + j
    return idx


def _w_q_cols():
    idx = -np.ones((MLA_HEADS * 256,), np.int64)
    for h in range(MLA_HEADS):
        o = h * (MLA_NOPE + MLA_ROPE)
        idx[h * 256:h * 256 + 128] = np.arange(128) + o
        for j in range(MLA_ROPE):
            idx[h * 256 + 128 + _rope_slot(j)] = o + MLA_NOPE + j
    return idx


def _pad_cols(w, idx):
    safe = np.where(idx >= 0, idx, 0)
    return jnp.where(jnp.asarray(idx >= 0)[None, :], w[:, safe], 0.0)


def _unpad_cols(wp, idx, n):
    inv = np.zeros((n,), np.int64)
    inv[idx[idx >= 0]] = np.nonzero(idx >= 0)[0]
    return wp[:, inv]


def _dot(a, b, ca, cb, precision=None):
    if precision is None:
        a = a.astype(BF16)
        b = b.astype(BF16)
    return lax.dot_general(a, b, (((ca,), (cb,)), ((), ())), preferred_element_type=F32, precision=precision)


@jax.custom_vjp
def mm(a, b):
    return _dot(a, b, 1, 0)


@jax.custom_vjp
def mm_nt(a, b):
    return _dot(a, b, 1, 1)


@jax.custom_vjp
def mm_tn(a, b):
    return _dot(a, b, 0, 0)


mm.defvjp(lambda a, b: (mm(a, b), (a, b)), lambda r, g: (mm_nt(g, r[1]), mm_tn(r[0], g)))
mm_nt.defvjp(lambda a, b: (mm_nt(a, b), (a, b)), lambda r, g: (mm(g, r[1]), mm_tn(g, r[0])))
mm_tn.defvjp(lambda a, b: (mm_tn(a, b), (a, b)), lambda r, g: (mm_nt(r[1], g), mm(r[0], g)))

def _split(a):
    hi = a.astype(BF16)
    return hi, (a - hi.astype(F32)).astype(BF16)


def _dot3(a, b, ca, cb):
    a_hi, a_lo = _split(a)
    b_hi, b_lo = _split(b)
    return (_dot(a_hi, b_hi, ca, cb) + _dot(a_hi, b_lo, ca, cb)) + _dot(a_lo, b_hi, ca, cb)


def _shift_rows(x, s):
    return x if s == 0 else pltpu.roll(x, s % x.shape[0], 0)


def _row(w, j):
    tap = lax.broadcasted_iota(jnp.int32, w.shape, 0)
    return jnp.sum(jnp.where(tap == j, w, 0.0), axis=0, keepdims=True)


@jax.custom_vjp
def dwconv(x, w):
    k = w.shape[0]
    y = _row(w, k - 1) * x
    for j in range(k - 1):
        y = y + _row(w, j) * _shift_rows(x, k - 1 - j)
    return y


def _dwconv_fwd(x, w):
    return dwconv(x, w), (x, w)


def _dwconv_bwd(res, dy):
    x, w = res
    k = w.shape[0]
    dx = _row(w, k - 1) * dy
    tap = lax.broadcasted_iota(jnp.int32, w.shape, 0)
    dw = jnp.where(tap == k - 1, jnp.sum(dy * x, axis=0, keepdims=True), 0.0)
    for j in range(k - 1):
        dx = dx + _row(w, j) * _shift_rows(dy, -(k - 1 - j))
        dw = dw + jnp.where(tap == j, jnp.sum(dy * _shift_rows(x, k - 1 - j), axis=0, keepdims=True), 0.0)
    return dx, dw


dwconv.defvjp(_dwconv_fwd, _dwconv_bwd)


@jax.custom_vjp
def rope128(x, cos, sin):
    return x * cos + pltpu.roll(x, 64, 1) * sin


rope128.defvjp(lambda x, c, s: (rope128(x, c, s), (c, s)),
               lambda r, g: (g * r[0] + pltpu.roll(g * r[1], 64, 1), jnp.zeros_like(r[0]), jnp.zeros_like(r[1])))


def _silu(x):
    return x * jax.nn.sigmoid(x)


def _softplus(x):
    return jnp.maximum(x, 0.0) + jnp.log(1.0 + jnp.exp(-jnp.abs(x)))


def _rmsnorm(x, g):
    return x * lax.rsqrt(jnp.mean(x * x, axis=-1, keepdims=True) + NORM_EPS) * g


def _layernorm(x, g, b):
    mu = jnp.mean(x, axis=-1, keepdims=True)
    xc = x - mu
    var = jnp.mean(xc * xc, axis=-1, keepdims=True)
    return xc * lax.rsqrt(var + NORM_EPS) * g + b


def _pick_lane(row, lane):
    idx = lax.broadcasted_iota(jnp.int32, row.shape, 1)
    return jnp.sum(jnp.where(idx == lane, row, 0.0), axis=1, keepdims=True)


def _gdn_q(pq, cw):
    h = _silu(dwconv(pq, cw))
    return h * lax.rsqrt(jnp.sum(h * h, axis=-1, keepdims=True) + NORM_EPS) * (GDN_DK ** -0.5)


def _gdn_k(pk, cw):
    h = _silu(dwconv(pk, cw))
    return h * lax.rsqrt(jnp.sum(h * h, axis=-1, keepdims=True) + NORM_EPS)


def _gdn_v(pv, cw):
    return _silu(dwconv(pv, cw))


def _gdn_gate(ab, sc, head):
    a = _pick_lane(ab, head)
    b = _pick_lane(ab, GDN_HEADS + head)
    a_log = _pick_lane(_row(sc, 0), head)
    dt_bias = _pick_lane(_row(sc, 1), head)
    beta = jax.nn.sigmoid(b)
    g = -jnp.exp(a_log) * _softplus(a + dt_bias)
    return _two_lanes(g, beta)


def _two_lanes(c0, c1):
    lane = lax.broadcasted_iota(jnp.int32, (c0.shape[0], LANES), 1)
    return jnp.where(lane == 0, c0, jnp.where(lane == 1, c1, 0.0))


def _run(levels):
    try:
        while True:
            next(levels)
    except StopIteration as stop:
        return stop.value


def _inverse_levels(lows):
    n = lows[0].shape[0]
    ii = lax.broadcasted_iota(jnp.int32, (n, n), 0)
    jj = lax.broadcasted_iota(jnp.int32, (n, n), 1)
    eye = jnp.where(ii == jj, 1.0, 0.0)
    invs = [eye - low for low in lows]
    powers = [_dot3(low, low, 1, 0) for low in lows]
    yield
    k = 2
    while k < n:
        invs = [inv + _dot3(inv, p, 1, 0) for inv, p in zip(invs, powers)]
        yield
        k *= 2
        if k < n:
            powers = [_dot3(p, p, 1, 0) for p in powers]
            yield
    return invs


def _inverse_group(lows):
    return _run(_inverse_levels(lows))


@jax.custom_vjp
def solve_group(lows, rhss):
    return [_dot3(inv, rhs, 1, 0) for inv, rhs in zip(_inverse_group(lows), rhss)]


def _solve_whole(lows, rhss):
    return solve_group(lows, rhss)
    yield


def _solve_levels(lows, rhss):
    invs = yield from _inverse_levels(lows)
    return [_dot3(inv, rhs, 1, 0) for inv, rhs in zip(invs, rhss)]


def _solve_group_fwd(lows, rhss):
    invs = _inverse_group(lows)
    xs = [_dot3(inv, rhs, 1, 0) for inv, rhs in zip(invs, rhss)]
    return xs, (invs, xs)


def _solve_group_bwd(res, dxs):
    invs, xs = res
    n = invs[0].shape[0]
    strict = lax.broadcasted_iota(jnp.int32, (n, n), 0) > lax.broadcasted_iota(jnp.int32, (n, n), 1)
    drhss = [_dot3(inv, dx, 0, 0) for inv, dx in zip(invs, dxs)]
    dlows = [jnp.where(strict, -_dot3(drhs, x, 1, 1), 0.0) for drhs, x in zip(drhss, xs)]
    return dlows, drhss


solve_group.defvjp(_solve_group_fwd, _solve_group_bwd)


def _gdn_local_group(qs, ks, vs, gbs):
    return _run(_gdn_local_levels(qs, ks, vs, gbs, _solve_whole))


def _gdn_local_levels(qs, ks, vs, gbs, solve):
    c = qs[0].shape[0]
    ii = lax.broadcasted_iota(jnp.int32, (c, c), 0)
    jj = lax.broadcasted_iota(jnp.int32, (c, c), 1)
    incl = ii >= jj
    gs = [_pick_lane(gb, 0) for gb in gbs]
    betas = [_pick_lane(gb, 1) for gb in gbs]
    g_rows = [jnp.sum(jnp.where(ii == jj, g, 0.0), axis=0, keepdims=True) for g in gs]
    gc_cols = [jnp.sum(jnp.where(incl, g_row, 0.0), axis=1, keepdims=True) for g_row in g_rows]
    gc_rows = [jnp.sum(jnp.where(jj >= ii, g, 0.0), axis=0, keepdims=True) for g in gs]
    decays = [jnp.where(incl, jnp.exp(jnp.where(incl, gc - gr, 0.0)), 0.0) for gc, gr in zip(gc_cols, gc_rows)]
    yield
    kbs = [k * beta for k, beta in zip(ks, betas)]
    lows = [jnp.where(ii > jj, mm_nt(kb, k) * decay, 0.0) for kb, k, decay in zip(kbs, ks, decays)]
    egs = [jnp.exp(gc) for gc in gc_cols]
    yield
    wus = yield from solve(lows, [jnp.concatenate([kb * eg, v * beta], axis=1) for kb, eg, v, beta in zip(kbs, egs, vs, betas)])
    yield
    qks = [mm_nt(q, k) * decay for q, k, decay in zip(qs, ks, decays)]
    g_lasts = [jnp.sum(g_row, axis=1, keepdims=True) for g_row in g_rows]
    kds = [k * jnp.exp(gl - gc) for k, gl, gc in zip(ks, g_lasts, gc_cols)]
    yield
    ws, us = [wu[:, :GDN_DK] for wu in wus], [wu[:, GDN_DK:] for wu in wus]
    q_effs = [q * eg - mm(qk, w) for q, eg, qk, w in zip(qs, egs, qks, ws)]
    yield
    o_locals = [mm(qk, u) for qk, u in zip(qks, us)]
    yield
    mixes = [mm_tn(kd, w) for kd, w in zip(kds, ws)]
    yield
    adds = [mm_tn(kd, u) for kd, u in zip(kds, us)]
    return [(q_eff, o_loc, mix, add, jnp.exp(gl))
            for q_eff, o_loc, mix, add, gl in zip(q_effs, o_locals, mixes, adds, g_lasts)]


def _gdn_state_step(q_eff, o_local, mix, add, eg_last, state):
    return mm(q_eff, state) + o_local, state * eg_last - mm(mix, state) + add


def _gdn_post(o, z, norm_g):
    return _rmsnorm(o, norm_g) * _silu(z)


MASKED = -1e30


def _scores(q, kn, kr, q0, k0):
    s = (mm_nt(q[:, :128], kn) + mm_nt(q[:, 128:], kr)) * ATT_SCALE
    if k0 + kn.shape[0] <= q0:
        return s
    qpos = q0 + lax.broadcasted_iota(jnp.int32, s.shape, 0)
    kpos = k0 + lax.broadcasted_iota(jnp.int32, s.shape, 1)
    shift = int(math.log2(CHUNK))
    return jnp.where((kpos >> shift) <= (qpos >> shift), s, MASKED)


def _softmax_times(s, v):
    top = jnp.max(s, axis=-1, keepdims=True)
    p = jnp.exp(s - top)
    norm = jnp.sum(p, axis=-1, keepdims=True)
    return mm(p / norm, v), top + jnp.log(norm)


def _mla_prep(pm, qg, kvg, wq, wkv, cos, sin):
    cq = pm[:, 128:512]
    ckv = pm[:, 512:768]
    qf = mm(_rmsnorm(cq, qg), wq)
    parts = []
    for h in range(MLA_HEADS):
        parts.append(qf[:, h * 256:h * 256 + 128])
        parts.append(rope128(qf[:, h * 256 + 128:h * 256 + 256], cos, sin))
    kvf = mm(_rmsnorm(ckv, kvg), wkv)
    return jnp.concatenate(parts, axis=1), kvf, rope128(pm[:, 768:896], cos, sin)


def _ffn_act(ug, uu, wg, wu, bg, bu):
    return _silu(dwconv(ug, wg) + bg) * (dwconv(uu, wu) + bu)


def _head_loss(h1, ffn, gpre, pp, bgate, g2, b2, target):
    gate = jax.nn.sigmoid(gpre + bgate)
    h2 = _layernorm(ALPHA * h1 + ffn + gate * pp, g2, b2)
    err = h2 - target
    return 0.5 * jnp.sum(jnp.sum(err * err, axis=1, keepdims=True), axis=0, keepdims=True) / D_MODEL


ROW_TILE_VREGS = 32
CONV_HALO = 8


def _rows_per_tile(n_rows, cols):
    tile = min(n_rows, ROW_TILE_VREGS * 8 * LANES // cols)
    assert n_rows % tile == 0 and tile % CONV_HALO == 0, (n_rows, cols)
    return tile


def _tile_inputs(loads, t0, first, halo, tile):
    if halo == 0:
        return [ld(pl.ds(t0, tile)) for ld in loads]
    if first:
        xs = [ld(pl.ds(0, tile)) for ld in loads]
        return [jnp.concatenate([jnp.zeros((halo, x.shape[1]), x.dtype), x], axis=0) for x in xs]
    return [ld(pl.ds(pl.multiple_of(t0 - halo, CONV_HALO), tile + halo)) for ld in loads]


def _rows_apply(fn, loads, consts, store, n_rows, cols, halo):
    tile = _rows_per_tile(n_rows, cols)

    def one(t0, first):
        y = fn(*_tile_inputs(loads, t0, first, halo, tile), *consts)
        store(pl.ds(t0, tile), y[halo:] if halo else y)

    one(0, True)

    def step(i, carry):
        one(pl.multiple_of(i * tile, tile), False)
        return carry

    lax.fori_loop(1, n_rows // tile, step, 0)


def _rows_vjp(fn, loads, consts, load_dy, stores, n_rows, cols, halo):
    tile = _rows_per_tile(n_rows, cols)

    def one(t0, first, dconsts):
        xs = _tile_inputs(loads, t0, first, halo, tile)
        _, vjp = jax.vjp(lambda *a: fn(*a)[halo:] if halo else fn(*a), *xs, *consts)
        grads = vjp(load_dy(pl.ds(t0, tile)))
        for st, dx in zip(stores, grads[:len(xs)]):
            st(pl.ds(t0, tile), dx[halo:] if halo else dx, False)
            if halo and not first:
                st(pl.ds(pl.multiple_of(t0 - halo, CONV_HALO), halo), dx[:halo], True)
        return tuple(a + b for a, b in zip(dconsts, grads[len(xs):]))

    dconsts = one(0, True, tuple(jnp.zeros_like(c) for c in consts))
    return lax.fori_loop(1, n_rows // tile, lambda i, dc: one(pl.multiple_of(i * tile, tile), False, dc), dconsts)


def _params(sem):
    return pltpu.CompilerParams(dimension_semantics=sem, vmem_limit_bytes=VMEM_LIMIT)


def _matmul(a, b, *, name, ta=False, tb=False, tm=512, tn=512, tk=1024, add=None, add_scale=1.0,
            a_halves=False, b_parts=0, out_parts=0, ride=None, out_dtype=F32):
    assert not (a_halves and ta)
    a_shape = (a.shape[1], 2 * a.shape[2]) if a_halves else a.shape
    b_shape = (b.shape[1], b_parts * b.shape[2]) if b_parts else b.shape
    (k_dim, m) = a_shape if ta else a_shape[::-1]
    (n, k2) = b_shape if tb else b_shape[::-1]
    assert k_dim == k2, (a.shape, b.shape)
    tm, tn, tk = min(tm, m), min(tn, n), min(tk, k_dim)
    assert m % tm == 0 and n % tn == 0 and k_dim % tk == 0, (name, m, n, k_dim, tm, tn, tk)
    nk = k_dim // tk
    ca, cb = (0 if ta else 1), (1 if tb else 0)

    def body(*refs):
        if add is None:
            a_ref, b_ref, o_ref, acc = refs
        else:
            a_ref, b_ref, c_ref, o_ref, acc = refs
        kk = pl.program_id(2)

        @pl.when(kk == 0)
        def _():
            acc[...] = jnp.zeros_like(acc)

        acc[...] += _dot(a_ref[...], b_ref[...], ca, cb)

        @pl.when(kk == nk - 1)
        def _():
            r = acc[...]
            if add is not None:
                r = r + add_scale * c_ref[...]
            o_ref[...] = r.astype(out_dtype)

    def per_part(total, parts, tile):
        per = total // parts // tile
        assert per * tile * parts == total, (name, total, parts, tile)
        return per

    spec = pl.BlockSpec
    a_spec = spec((tk, tm), lambda i, j, k: (k, i)) if ta else spec((tm, tk), lambda i, j, k: (i, k))
    b_spec = spec((tn, tk), lambda i, j, k: (j, k)) if tb else spec((tk, tn), lambda i, j, k: (k, j))
    if a_halves:
        kh = per_part(k_dim, 2, tk)
        a_spec = spec((None, tm, tk), lambda i, j, k: (k // kh, i, k % kh))
    if b_parts and tb:
        kp = per_part(k_dim, b_parts, tk)
        b_spec = spec((None, tn, tk), lambda i, j, k: (k // kp, j, k % kp))
    elif b_parts:
        np_ = per_part(n, b_parts, tn)
        b_spec = spec((None, tk, tn), lambda i, j, k: (j // np_, k, j % np_))
    out_spec, out_shape = spec((tm, tn), lambda i, j, k: (i, j)), (m, n)
    if out_parts:
        op = per_part(n, out_parts, tn)
        out_spec, out_shape = spec((None, tm, tn), lambda i, j, k: (j // op, i, j % op)), (out_parts, m, n // out_parts)
    in_specs = [a_spec, b_spec]
    args = [a, b]
    if add is not None:
        in_specs.append(pl.BlockSpec((tm, tn), lambda i, j, k: (i, j)))
        args.append(add)
    grid = (m // tm, n // tn, nk)
    r_in, r_out, r_shapes, r_sems, r_args = _ride_specs(ride)
    outs = pl.pallas_call(
        _riding(body, len(args), 1, 1, ride, grid), name=name, grid=grid,
        in_specs=in_specs + r_in, out_specs=[out_spec] + r_out,
        out_shape=[jax.ShapeDtypeStruct(out_shape, out_dtype)] + r_shapes,
        scratch_shapes=[pltpu.VMEM((tm, tn), F32)] + r_sems,
        compiler_params=_params(("parallel", "parallel", "arbitrary") if ride is None else ("arbitrary",) * 3),
    )(*args, *r_args)
    return outs[0] if ride is None else (outs[0], list(outs[1:]))


def _riding(core, n_in, n_out, n_scratch, ride, grid):
    if ride is None:
        return core
    copies, nr = ride[1], len(ride[0])
    steps = int(np.prod(grid))
    pass_step = min(max(int(steps * ride[4]), 1), steps - 2)
    assert steps >= 3, grid

    def body(*refs):
        cuts = np.cumsum([0, n_in, nr, n_out, nr, n_scratch])
        ins, rin, outs, rout, scratch = (refs[a:b] for a, b in zip(cuts[:-1], cuts[1:]))
        sems = refs[cuts[-1]:]
        step = 0
        for axis, size in enumerate(grid):
            step = step * size + pl.program_id(axis)

        @pl.when(step == 0)
        def _():
            _exchange_start(copies(rin, rout, sems))

        @pl.when(step == pass_step)
        def _():
            _exchange_pass_on(copies(rin, rout, sems))

        core(*ins, *outs, *scratch)

        @pl.when(step == steps - 1)
        def _():
            _exchange_finish(copies(rin, rout, sems))

    return body


def _ride_specs(ride):
    if ride is None:
        return [], [], [], [], []
    arrays, _, sems, shapes, _ = ride
    return [HBM_REF] * len(arrays), [HBM_REF] * len(arrays), list(shapes), sems(len(arrays)), list(arrays)


def _gdn_fwd(proj, conv_w, sc, norm_g, bl, s, ride=None):
    nc = s // CHUNK

    def core(ph_ref, ab_ref, cwq_ref, cwk_ref, cwv_ref, sc_ref, ng_ref, cat_ref, o_ref, st_ref, q_s, k_s, v_s, gb_s):
        def into(ref):
            def store(rows, value):
                ref[rows, :] = value.astype(ref.dtype)
            return store

        for fn, col, cw_ref, val_s in [(_gdn_q, 0, cwq_ref, q_s), (_gdn_k, 128, cwk_ref, k_s), (_gdn_v, 256, cwv_ref, v_s)]:
            _rows_apply(fn, [lambda r, col=col: ph_ref[r, col:col + 128]], [cw_ref[...]], into(val_s), s, LANES, CONV_HALO)
        _rows_apply(functools.partial(_gdn_gate, head=pl.program_id(1)), [lambda r: ab_ref[r, :]], [sc_ref[...]], into(gb_s), s, LANES, 0)

        group = math.gcd(nc, GDN_FWD_GROUP)

        def rows_of(n):
            return slice(n * CHUNK, (n + 1) * CHUNK)

        def levels_of(g):
            rows = [rows_of(g * group + j) for j in range(group)]
            return _gdn_local_levels([q_s[r, :] for r in rows], [k_s[r, :] for r in rows], [v_s[r, :] for r in rows],
                                     [gb_s[r, :] for r in rows], _solve_levels)

        local, state = _run(levels_of(0)), jnp.zeros((GDN_DK, GDN_DV), F32)
        for g in range(nc // group):
            ahead = levels_of(g + 1) if g + 1 < nc // group else None
            following = None
            for j, loc in enumerate(local):
                n = g * group + j
                st_ref[n] = state
                o_ref[rows_of(n), :], state = _gdn_state_step(*loc, state)
                if ahead is not None and following is None:
                    try:
                        next(ahead)
                    except StopIteration as stop:
                        following = stop.value
            if ahead is not None and following is None:
                following = _run(ahead)
            local = following
        _rows_apply(_gdn_post, [lambda r: o_ref[r, :], lambda r: ph_ref[r, 384:512]], [ng_ref[...]], into(cat_ref), s, LANES, 0)

    t = bl * s
    r_in, r_out, r_shapes, r_sems, r_args = _ride_specs(ride)
    outs = pl.pallas_call(
        _riding(core, 7, 7, 0, ride, (bl, GDN_HEADS)), name="gdn_fwd", grid=(bl, GDN_HEADS),
        in_specs=[
            pl.BlockSpec((s, 512), lambda b, h: (b, h)),
            pl.BlockSpec((s, 128), lambda b, h: (b, P_MLA // 128)),
            pl.BlockSpec((GDN_CONV, 128), lambda b, h: (0, h)),
            pl.BlockSpec((GDN_CONV, 128), lambda b, h: (0, GDN_HEADS + h)),
            pl.BlockSpec((GDN_CONV, 128), lambda b, h: (0, 2 * GDN_HEADS + h)),
            pl.BlockSpec((8, 128), lambda b, h: (0, 0)),
            pl.BlockSpec((1, 128), lambda b, h: (0, 0)),
        ] + r_in,
        out_specs=[
            pl.BlockSpec((s, 128), lambda b, h: (b, h)),
            pl.BlockSpec((s, 128), lambda b, h: (b, h)),
            pl.BlockSpec((None, None, nc, GDN_DK, GDN_DV), lambda b, h: (b, h, 0, 0, 0)),
        ] + [pl.BlockSpec((s, 128), lambda b, h: (b, h))] * 4 + r_out,
        out_shape=[
            jax.ShapeDtypeStruct((t, 2 * GDN_VW), BF16),
            jax.ShapeDtypeStruct((t, GDN_VW), F32),
            jax.ShapeDtypeStruct((bl, GDN_HEADS, nc, GDN_DK, GDN_DV), F32),
        ] + [jax.ShapeDtypeStruct((t, GDN_VW), F32)] * 4 + r_shapes,
        scratch_shapes=r_sems,
        compiler_params=_params(("arbitrary", "arbitrary")),
    )(proj, proj, conv_w, conv_w, conv_w, sc, norm_g, *r_args)
    return outs[0], outs[1], outs[2], tuple(outs[3:7]), list(outs[7:])


def _gdn_bwd(proj, conv_w, sc, norm_g, o_raw, states, qkvg, dcat, bl, s, ride=None):
    nc = s // CHUNK

    def core(ph_ref, ab_ref, cwq_ref, cwk_ref, cwv_ref, sc_ref, ng_ref, o_ref, st_ref, dc_ref, q_in, k_in, v_in, gb_in,
             dph_ref, dab_ref, dcwq_ref, dcwk_ref, dcwv_ref, dsc_ref, dng_ref, q_s, k_s, v_s, gb_s, do_s):
        head = pl.program_id(1)
        gate = functools.partial(_gdn_gate, head=head)
        paths = [(_gdn_q, 0, cwq_ref, q_s, dcwq_ref), (_gdn_k, 128, cwk_ref, k_s, dcwk_ref), (_gdn_v, 256, cwv_ref, v_s, dcwv_ref)]
        def into(ref, cols=slice(None)):
            def store(rows, value, add=False):
                if add:
                    ref[rows, cols] += value.astype(ref.dtype)
                else:
                    ref[rows, cols] = value.astype(ref.dtype)
            return store

        (dng,) = _rows_vjp(_gdn_post, [lambda r: o_ref[r, :], lambda r: ph_ref[r, 384:512]], [ng_ref[...]],
                           lambda r: dc_ref[r, :], [into(do_s), into(dph_ref, slice(384, 512))], s, LANES, 0)
        dng_ref[...] = jnp.broadcast_to(dng, dng_ref.shape)

        group = math.gcd(nc, GDN_BWD_GROUP)

        def chunks(i, dstate):
            ns = [nc - 1 - (i * group + j) for j in range(group)]
            rows = [pl.ds(pl.multiple_of(n * CHUNK, CHUNK), CHUNK) for n in ns]
            local, local_vjp = jax.vjp(_gdn_local_group, [q_in[r, :] for r in rows], [k_in[r, :] for r in rows],
                                       [v_in[r, :] for r in rows], [gb_in[r, :] for r in rows])
            d_os = [do_s[r, :] for r in rows]
            dlocal = []
            for n, loc, d_o in zip(ns, local, d_os):
                _, step_vjp = jax.vjp(_gdn_state_step, *loc, st_ref[n])
                *dloc, dstate = step_vjp((d_o, dstate))
                dlocal.append(tuple(dloc))
            dqs, dks, dvs, dgbs = local_vjp(dlocal)
            for r, dq, dk, dv, dgb in zip(rows, dqs, dks, dvs, dgbs):
                q_s[r, :], k_s[r, :], v_s[r, :], gb_s[r, :] = dq, dk, dv, dgb
            return dstate

        lax.fori_loop(0, nc // group, chunks, jnp.zeros((GDN_DK, GDN_DV), F32))
        for fn, col, cw_ref, val_s, dcw_ref in paths:
            (dcw_ref[...],) = _rows_vjp(fn, [lambda r, col=col: ph_ref[r, col:col + 128]], [cw_ref[...]],
                                        lambda r, val_s=val_s: val_s[r, :], [into(val_s)], s, LANES, CONV_HALO)
            dph_ref[:, col:col + 128] = val_s[...].astype(BF16)

        @pl.when(head == 0)
        def _():
            dab_ref[...] = jnp.zeros_like(dab_ref)

        def add_dab(rows, value, add=False):
            dab_ref[rows, :] += value

        (dsc_ref[...],) = _rows_vjp(gate, [lambda r: ab_ref[r, :]], [sc_ref[...]], lambda r: gb_s[r, :], [add_dab], s, LANES, 0)

    t = bl * s
    cw_out = pl.BlockSpec((None, GDN_CONV, 128), lambda b, h: (b, 0, h))
    part = pl.BlockSpec((None, None, 8, 128), lambda b, h: (b, h, 0, 0))
    r_in, r_out, r_shapes, r_sems, r_args = _ride_specs(ride)
    outs = pl.pallas_call(
        _riding(core, 14, 7, 5, ride, (bl, GDN_HEADS)), name="gdn_bwd", grid=(bl, GDN_HEADS),
        in_specs=[
            pl.BlockSpec((s, 512), lambda b, h: (b, h)),
            pl.BlockSpec((s, 128), lambda b, h: (b, P_MLA // 128)),
            pl.BlockSpec((GDN_CONV, 128), lambda b, h: (0, h)),
            pl.BlockSpec((GDN_CONV, 128), lambda b, h: (0, GDN_HEADS + h)),
            pl.BlockSpec((GDN_CONV, 128), lambda b, h: (0, 2 * GDN_HEADS + h)),
            pl.BlockSpec((8, 128), lambda b, h: (0, 0)),
            pl.BlockSpec((1, 128), lambda b, h: (0, 0)),
            pl.BlockSpec((s, 128), lambda b, h: (b, h)),
            pl.BlockSpec((None, None, nc, GDN_DK, GDN_DV), lambda b, h: (b, h, 0, 0, 0)),
        ] + [pl.BlockSpec((s, 128), lambda b, h: (b, h))] * 5 + r_in,
        out_specs=[
            pl.BlockSpec((s, 512), lambda b, h: (b, h)),
            pl.BlockSpec((s, 128), lambda b, h: (b, 0)),
            cw_out, cw_out, cw_out, part, part,
        ] + r_out,
        out_shape=[
            jax.ShapeDtypeStruct((t, P_WIDTH), BF16),
            jax.ShapeDtypeStruct((t, 128), F32),
            jax.ShapeDtypeStruct((bl, GDN_CONV, 512), F32),
            jax.ShapeDtypeStruct((bl, GDN_CONV, 512), F32),
            jax.ShapeDtypeStruct((bl, GDN_CONV, 512), F32),
            jax.ShapeDtypeStruct((bl, GDN_HEADS, 8, 128), F32),
            jax.ShapeDtypeStruct((bl, GDN_HEADS, 8, 128), F32),
        ] + r_shapes,
        scratch_shapes=[pltpu.VMEM((s, 128), F32)] * 5 + r_sems,
        compiler_params=_params(("arbitrary", "arbitrary")),
    )(proj, proj, conv_w, conv_w, conv_w, sc, norm_g, o_raw, states, dcat, *qkvg, *r_args)
    return tuple(outs[:7]) + (list(outs[7:]),)


def _mla_prep_fwd(proj, qg, kvg, wq, wkv, cos, sin, s, tm):
    t = proj.shape[0]
    tm = min(tm, s)
    nps = s // tm
    const = lambda shape: pl.BlockSpec(shape, lambda i: (0, 0))

    def body(pm_ref, qg_ref, kvg_ref, wq_ref, wkv_ref, cos_ref, sin_ref, qf_ref, kvf_ref, kr_ref):
        qf, kvf, kr = _mla_prep(pm_ref[...], qg_ref[...], kvg_ref[...], wq_ref[...], wkv_ref[...], cos_ref[...], sin_ref[...])
        qf_ref[...], kvf_ref[...], kr_ref[...] = qf.astype(BF16), kvf.astype(BF16), kr.astype(BF16)

    return pl.pallas_call(
        body, name="mla_prep_fwd", grid=(t // tm,),
        in_specs=[
            pl.BlockSpec((tm, 1024), lambda i: (i, P_MLA // 1024)),
            const((1, MLA_Q_LORA)), const((1, MLA_KV_LORA)), const(wq.shape), const(wkv.shape),
            pl.BlockSpec((tm, 128), lambda i: (i % nps, 0)), pl.BlockSpec((tm, 128), lambda i: (i % nps, 0)),
        ],
        out_specs=[pl.BlockSpec((tm, 1024), lambda i: (i, 0)), pl.BlockSpec((tm, 1024), lambda i: (i, 0)),
                   pl.BlockSpec((tm, 128), lambda i: (i, 0))],
        out_shape=[jax.ShapeDtypeStruct((t, 1024), BF16), jax.ShapeDtypeStruct((t, 1024), BF16),
                   jax.ShapeDtypeStruct((t, 128), BF16)],
        compiler_params=_params(("parallel",)),
    )(proj, qg, kvg, wq, wkv, cos, sin)


def _mla_prep_bwd(proj, qg, kvg, wq, wkv, cos, sin, dqf, dkvf, dkr, dab, dproj, s, tm):
    t = proj.shape[0]
    tm = min(tm, s)
    nps = s // tm
    const = lambda shape: pl.BlockSpec(shape, lambda i: (0, 0))

    def body(pm_ref, qg_ref, kvg_ref, wq_ref, wkv_ref, cos_ref, sin_ref, dqf_ref, dkvf_ref, dkr_ref, dab_ref, dp_in,
             dp_ref, dqg_ref, dkvg_ref, dwq_ref, dwkv_ref):
        del dp_in
        fn = lambda pm, qg_, kvg_, wq_, wkv_: _mla_prep(pm, qg_, kvg_, wq_, wkv_, cos_ref[...], sin_ref[...])
        _, vjp = jax.vjp(fn, pm_ref[...], qg_ref[...], kvg_ref[...], wq_ref[...].astype(F32), wkv_ref[...].astype(F32))
        dpm, dqg, dkvg, dwq, dwkv = vjp((dqf_ref[...], dkvf_ref[...], dkr_ref[...]))
        dp_ref[...] = jnp.concatenate([dab_ref[...], dpm[:, 128:]], axis=1).astype(BF16)

        @pl.when(pl.program_id(0) == 0)
        def _():
            dqg_ref[...] = jnp.zeros_like(dqg_ref)
            dkvg_ref[...] = jnp.zeros_like(dkvg_ref)
            dwq_ref[...] = jnp.zeros_like(dwq_ref)
            dwkv_ref[...] = jnp.zeros_like(dwkv_ref)

        dqg_ref[...] += dqg
        dkvg_ref[...] += dkvg
        dwq_ref[...] += dwq
        dwkv_ref[...] += dwkv

    rows = lambda w: pl.BlockSpec((tm, w), lambda i: (i, 0))
    return pl.pallas_call(
        body, name="mla_prep_bwd", grid=(t // tm,),
        in_specs=[
            pl.BlockSpec((tm, 1024), lambda i: (i, P_MLA // 1024)),
            const((1, MLA_Q_LORA)), const((1, MLA_KV_LORA)), const(wq.shape), const(wkv.shape),
            pl.BlockSpec((tm, 128), lambda i: (i % nps, 0)), pl.BlockSpec((tm, 128), lambda i: (i % nps, 0)),
            rows(1024), rows(1024), rows(128), rows(128),
            pl.BlockSpec(memory_space=pl.ANY),
        ],
        out_specs=[pl.BlockSpec((tm, 1024), lambda i: (i, P_MLA // 1024)),
                   const((1, MLA_Q_LORA)), const((1, MLA_KV_LORA)), const(wq.shape), const(wkv.shape)],
        out_shape=[jax.ShapeDtypeStruct(dproj.shape, dproj.dtype),
                   jax.ShapeDtypeStruct((1, MLA_Q_LORA), F32), jax.ShapeDtypeStruct((1, MLA_KV_LORA), F32),
                   jax.ShapeDtypeStruct(wq.shape, F32), jax.ShapeDtypeStruct(wkv.shape, F32)],
        input_output_aliases={11: 0},
        compiler_params=_params(("arbitrary",)),
    )(proj, qg, kvg, wq, wkv, cos, sin, dqf, dkvf, dkr, dab, dproj)


def _attn_fwd(qf, kvf, kr, cat, bl, s, tq):
    tq = min(tq, s)
    nq = s // tq

    def body(q_ref, kv_ref, kr_ref, cat_in, o_ref, o32_ref, lse_ref):
        del cat_in

        def scores_of(i):
            keys = slice(0, (i + 1) * tq)
            return _scores(q_ref[i * tq:(i + 1) * tq, :], kv_ref[keys, 0:128], kr_ref[keys, :], i * tq, 0)

        ready = scores_of(0)
        for i in range(nq):
            scores = ready
            if i + 1 < nq:
                ready = scores_of(i + 1)
            rows = slice(i * tq, (i + 1) * tq)
            o, lse = _softmax_times(scores, kv_ref[0:(i + 1) * tq, 128:256])
            o_ref[rows, :] = o.astype(o_ref.dtype)
            o32_ref[rows, :] = o
            lse_ref[rows, :] = jnp.broadcast_to(lse, o.shape)

    t = bl * s
    head_cols = pl.BlockSpec((s, 128), lambda b, h: (b, h))
    return pl.pallas_call(
        body, name="attn_fwd", grid=(bl, MLA_HEADS),
        in_specs=[
            pl.BlockSpec((s, 256), lambda b, h: (b, h)),
            pl.BlockSpec((s, 256), lambda b, h: (b, h)),
            pl.BlockSpec((s, 128), lambda b, h: (b, 0)),
            pl.BlockSpec(memory_space=pl.ANY),
        ],
        out_specs=[pl.BlockSpec((s, 128), lambda b, h: (b, GDN_HEADS + h)), head_cols, head_cols],
        out_shape=[jax.ShapeDtypeStruct(cat.shape, cat.dtype)] + [jax.ShapeDtypeStruct((t, MLA_HEADS * MLA_V), F32)] * 2,
        input_output_aliases={3: 0},
        compiler_params=_params(("parallel", "parallel")),
    )(qf, kvf, kr, cat)


def _attn_bwd(qf, kvf, kr, dcat, o32, lse, bl, s, tq):
    tq = min(tq, s)
    nq = s // tq

    def body(q_ref, kv_ref, kr_ref, do_ref, o_ref, lse_ref, dq_ref, dkv_ref, dkr_ref):
        dkv_ref[...] = jnp.zeros_like(dkv_ref)

        @pl.when(pl.program_id(1) == 0)
        def _():
            dkr_ref[...] = jnp.zeros_like(dkr_ref)

        def block(i):
            rows = slice(i * tq, (i + 1) * tq)
            return q_ref[rows, :], do_ref[rows, :]

        def first_products(i, j):
            (q, d_o), keys = block(i), slice(j * tq, (j + 1) * tq)
            return _scores(q, kv_ref[keys, 0:128], kr_ref[keys, :], i * tq, j * tq), mm_nt(d_o, kv_ref[keys, 128:256])

        tiles = [(i, j) for i in range(nq) for j in range(i + 1)]
        ready = first_products(*tiles[0])
        for t, (i, j) in enumerate(tiles):
            scores, dp = ready
            if t + 1 < len(tiles):
                ready = first_products(*tiles[t + 1])
            rows, keys = slice(i * tq, (i + 1) * tq), slice(j * tq, (j + 1) * tq)
            q, d_o = block(i)
            if j == 0:
                delta = jnp.sum(d_o * o_ref[rows, :], axis=-1, keepdims=True)
                lse_i = jnp.max(lse_ref[rows, :], axis=-1, keepdims=True)
                dqn, dqr = jnp.zeros((tq, MLA_NOPE), F32), jnp.zeros((tq, 128), F32)
            p = jnp.exp(scores - lse_i)
            ds = p * (dp - delta) * ATT_SCALE
            dkv_ref[keys, 128:256] += mm_tn(p, d_o)
            dkv_ref[keys, 0:128] += mm_tn(ds, q[:, :128])
            dkr_ref[keys, :] += mm_tn(ds, q[:, 128:])
            dqn = dqn + mm(ds, kv_ref[keys, 0:128])
            dqr = dqr + mm(ds, kr_ref[keys, :])
            if j == i:
                dq_ref[rows, 0:128], dq_ref[rows, 128:256] = dqn, dqr

    t = bl * s
    head_cols = pl.BlockSpec((s, 128), lambda b, h: (b, h))
    return pl.pallas_call(
        body, name="attn_bwd", grid=(bl, MLA_HEADS),
        in_specs=[
            pl.BlockSpec((s, 256), lambda b, h: (b, h)),
            pl.BlockSpec((s, 256), lambda b, h: (b, h)),
            pl.BlockSpec((s, 128), lambda b, h: (b, 0)),
            pl.BlockSpec((s, 128), lambda b, h: (b, GDN_HEADS + h)),
            head_cols, head_cols,
        ],
        out_specs=[
            pl.BlockSpec((s, 256), lambda b, h: (b, h)),
            pl.BlockSpec((s, 256), lambda b, h: (b, h)),
            pl.BlockSpec((s, 128), lambda b, h: (b, 0)),
        ],
        out_shape=[jax.ShapeDtypeStruct((t, 1024), F32), jax.ShapeDtypeStruct((t, 1024), F32),
                   jax.ShapeDtypeStruct((t, 128), F32)],
        compiler_params=_params(("parallel", "arbitrary")),
    )(qf, kvf, kr, dcat, o32, lse)


def _mix_ln1_fwd(x, cat, w_o, g, b, tm):
    t = x.shape[0]
    tm = min(tm, t)

    def body(x_ref, cat_ref, w_ref, g_ref, b_ref, r_ref, h_ref, hb_ref, xb_ref):
        r = ALPHA * x_ref[...] + _dot(cat_ref[...], w_ref[...], 1, 0)
        r_ref[...] = r
        h = _layernorm(r, g_ref[...], b_ref[...])
        h_ref[...] = h
        hb_ref[...] = h.astype(BF16)
        xb_ref[...] = x_ref[...].astype(BF16)

    rows = pl.BlockSpec((tm, D_MODEL), lambda i: (i, 0))
    vec = pl.BlockSpec((1, D_MODEL), lambda i: (0, 0))
    return pl.pallas_call(
        body, name="mix_ln1_fwd", grid=(t // tm,),
        in_specs=[rows, pl.BlockSpec((tm, cat.shape[1]), lambda i: (i, 0)), pl.BlockSpec(w_o.shape, lambda i: (0, 0)), vec, vec],
        out_specs=[rows] * 4,
        out_shape=[jax.ShapeDtypeStruct(x.shape, F32)] * 2 + [jax.ShapeDtypeStruct(x.shape, BF16)] * 2,
        compiler_params=_params(("parallel",)),
    )(x, cat, w_o, g, b)


def _ln1_bwd(r1, dr2, da, dgpre, w_gate, w_o, g, b, tm):
    t = r1.shape[0]
    tm = min(tm, t)

    def body(r_ref, d2_ref, da_ref, dgp_ref, wg_ref, wo_ref, g_ref, b_ref, dr_ref, drb_ref, dcat_ref, dg_ref, dbias_ref):
        dh = ALPHA * d2_ref[...] + da_ref[...] + _dot(dgp_ref[...], wg_ref[...], 1, 1)
        _, vjp = jax.vjp(_layernorm, r_ref[...], g_ref[...], b_ref[...])
        dr, dg, dbias = vjp(dh)
        dr_ref[...] = dr
        drb_ref[...] = dr.astype(BF16)
        dcat_ref[...] = _dot(dr, wo_ref[...], 1, 1)

        @pl.when(pl.program_id(0) == 0)
        def _():
            dg_ref[...] = jnp.zeros_like(dg_ref)
            dbias_ref[...] = jnp.zeros_like(dbias_ref)

        dg_ref[...] += dg
        dbias_ref[...] += dbias

    rows = pl.BlockSpec((tm, D_MODEL), lambda i: (i, 0))
    vec = pl.BlockSpec((1, D_MODEL), lambda i: (0, 0))
    return pl.pallas_call(
        body, name="ln1_bwd", grid=(t // tm,),
        in_specs=[rows] * 4 + [pl.BlockSpec(w_gate.shape, lambda i: (0, 0)), pl.BlockSpec(w_o.shape, lambda i: (0, 0)), vec, vec],
        out_specs=[rows, rows, pl.BlockSpec((tm, w_o.shape[0]), lambda i: (i, 0)), vec, vec],
        out_shape=[jax.ShapeDtypeStruct(r1.shape, F32), jax.ShapeDtypeStruct(r1.shape, BF16),
                   jax.ShapeDtypeStruct((t, w_o.shape[0]), F32)] + [jax.ShapeDtypeStruct((1, D_MODEL), F32)] * 2,
        compiler_params=_params(("arbitrary",)),
    )(r1, dr2, da, dgpre, w_gate, w_o, g, b)


def _ffn_act_fwd(u, conv_w, conv_b, bl, s, cb):
    nj = D_FF // cb

    def body(ug_ref, uu_ref, wg_ref, wu_ref, bg_ref, bu_ref, act_ref):
        def store(rows, act):
            act_ref[rows, :] = act.astype(BF16)

        _rows_apply(_ffn_act, [lambda r: ug_ref[r, :], lambda r: uu_ref[r, :]],
                    [wg_ref[...], wu_ref[...], bg_ref[...], bu_ref[...]], store, s, cb, CONV_HALO)

    return pl.pallas_call(
        body, name="ffn_act_fwd", grid=(bl, nj),
        in_specs=[
            pl.BlockSpec((s, cb), lambda b, j: (b, j)), pl.BlockSpec((s, cb), lambda b, j: (b, nj + j)),
            pl.BlockSpec((FFN_CONV, cb), lambda b, j: (0, j)), pl.BlockSpec((FFN_CONV, cb), lambda b, j: (0, nj + j)),
            pl.BlockSpec((1, cb), lambda b, j: (0, j)), pl.BlockSpec((1, cb), lambda b, j: (0, nj + j)),
        ],
        out_specs=pl.BlockSpec((s, cb), lambda b, j: (b, j)),
        out_shape=jax.ShapeDtypeStruct((bl * s, D_FF), BF16),
        compiler_params=_params(("parallel", "parallel")),
    )(u, u, conv_w, conv_w, conv_b, conv_b)


def _ffn_act_bwd(u, conv_w, conv_b, dact, bl, s, cb):
    nj = D_FF // cb

    def body(ug_ref, uu_ref, wg_ref, wu_ref, bg_ref, bu_ref, da_ref, du_ref, dwg_ref, dwu_ref, dbg_ref, dbu_ref, acc):
        def store_into(half):
            def store(rows, value, add):
                if add:
                    acc[half, rows, :] += value
                else:
                    acc[half, rows, :] = value
            return store

        dwg_ref[...], dwu_ref[...], dbg_ref[...], dbu_ref[...] = _rows_vjp(
            _ffn_act, [lambda r: ug_ref[r, :], lambda r: uu_ref[r, :]], [wg_ref[...], wu_ref[...], bg_ref[...], bu_ref[...]],
            lambda r: da_ref[r, :], [store_into(0), store_into(1)], s, cb, CONV_HALO)
        du_ref[...] = acc[...].astype(BF16)

    t = bl * s
    blk = pl.BlockSpec((s, cb), lambda b, j: (b, j))
    wpart = pl.BlockSpec((None, FFN_CONV, cb), lambda b, j: (b, 0, j))
    bpart = pl.BlockSpec((None, 1, cb), lambda b, j: (b, 0, j))
    return pl.pallas_call(
        body, name="ffn_act_bwd", grid=(bl, nj),
        in_specs=[
            blk, pl.BlockSpec((s, cb), lambda b, j: (b, nj + j)),
            pl.BlockSpec((FFN_CONV, cb), lambda b, j: (0, j)), pl.BlockSpec((FFN_CONV, cb), lambda b, j: (0, nj + j)),
            pl.BlockSpec((1, cb), lambda b, j: (0, j)), pl.BlockSpec((1, cb), lambda b, j: (0, nj + j)),
            blk,
        ],
        out_specs=[pl.BlockSpec((2, s, cb), lambda b, j: (0, b, j)), wpart, wpart, bpart, bpart],
        out_shape=[jax.ShapeDtypeStruct((2, t, D_FF), BF16)] + [jax.ShapeDtypeStruct((bl, FFN_CONV, D_FF), F32)] * 2
        + [jax.ShapeDtypeStruct((bl, 1, D_FF), F32)] * 2,
        scratch_shapes=[pltpu.VMEM((2, s, cb), F32)],
        compiler_params=_params(("parallel", "parallel")),
    )(u, u, conv_w, conv_w, conv_b, conv_b, dact)


def _head(h1, h1b, act, pt, w_down, w_gate, w_proj, bgate, g2, b2, target, tm):
    t = h1.shape[0]
    tm = min(tm, t)

    def body(h1_ref, h1b_ref, act_ref, pt_ref, wd_ref, wg_ref, wp_ref, bg_ref, g2_ref, b2_ref, tg_ref,
             dr_ref, drb_ref, dgp_ref, dpp_ref, loss_ref, dbg_ref, dg2_ref, db2_ref):
        fn = functools.partial(_head_loss, target=tg_ref[...])
        ffn = _dot(act_ref[...], wd_ref[...], 1, 0)
        gpre, pp = _dot(h1b_ref[...], wg_ref[...], 1, 0), _dot(pt_ref[...], wp_ref[...], 1, 0)
        loss, vjp = jax.vjp(fn, h1_ref[...], ffn, gpre, pp, bg_ref[...], g2_ref[...], b2_ref[...])
        _, dffn, dgp, dpp, dbg, dg2, db2 = vjp(jnp.ones((1, 1), F32))
        dr_ref[...] = dffn
        drb_ref[...], dgp_ref[...], dpp_ref[...] = dffn.astype(BF16), dgp.astype(BF16), dpp.astype(BF16)

        @pl.when(pl.program_id(0) == 0)
        def _():
            loss_ref[...] = jnp.zeros_like(loss_ref)
            dbg_ref[...] = jnp.zeros_like(dbg_ref)
            dg2_ref[...] = jnp.zeros_like(dg2_ref)
            db2_ref[...] = jnp.zeros_like(db2_ref)

        loss_ref[...] += jnp.broadcast_to(loss, loss_ref.shape)
        dbg_ref[...] += dbg
        dg2_ref[...] += dg2
        db2_ref[...] += db2

    rows = pl.BlockSpec((tm, D_MODEL), lambda i: (i, 0))
    vec = pl.BlockSpec((1, D_MODEL), lambda i: (0, 0))
    return pl.pallas_call(
        body, name="head", grid=(t // tm,),
        in_specs=[rows, rows, pl.BlockSpec((tm, act.shape[1]), lambda i: (i, 0)), pl.BlockSpec((tm, pt.shape[1]), lambda i: (i, 0))]
        + [pl.BlockSpec(w.shape, lambda i: (0, 0)) for w in (w_down, w_gate, w_proj)] + [vec] * 3 + [rows],
        out_specs=[rows] * 4 + [pl.BlockSpec((8, 128), lambda i: (0, 0))] + [vec] * 3,
        out_shape=[jax.ShapeDtypeStruct(h1.shape, F32)] + [jax.ShapeDtypeStruct(h1.shape, BF16)] * 3
        + [jax.ShapeDtypeStruct((8, 128), F32)]
        + [jax.ShapeDtypeStruct((1, D_MODEL), F32)] * 3,
        compiler_params=_params(("arbitrary",)),
    )(h1, h1b, act, pt, w_down, w_gate, w_proj, bgate, g2, b2, target)


def _adam_update(g, w_ref, m_ref, v_ref, g_ref, d_ref, nm_ref, nv_ref):
    m2 = ADAM_B1 * m_ref[...] + (1.0 - ADAM_B1) * g
    v2 = ADAM_B2 * v_ref[...] + (1.0 - ADAM_B2) * jnp.square(g)
    m_hat = m2 / (1.0 - ADAM_B1 ** ADAM_STEP)
    v_hat = v2 / (1.0 - ADAM_B2 ** ADAM_STEP)
    g_ref[...] = g
    d_ref[...] = -ADAM_LR * (m_hat / (jnp.sqrt(v_hat) + ADAM_EPS) + ADAM_WD * w_ref[...])
    nm_ref[...] = m2
    nv_ref[...] = v2


def _row_tile(rows, cols, limit_bytes=512 * 1024):
    best = None
    for t in range(HALF_ROWS_QUANTUM, rows + 1, HALF_ROWS_QUANTUM):
        if rows % t == 0 and t * cols * 4 <= limit_bytes:
            best = t
    return best or rows


def _adamw_reduced(recv, w, m, v, name):
    a, b = w.shape
    ta = _row_tile(a, b)

    def body(recv_ref, w_ref, m_ref, v_ref, g_ref, d_ref, nm_ref, nv_ref):
        c = lax.axis_index("c")
        for core in range(2):
            @pl.when(c == core)
            def _():
                got = [recv_ref[k].astype(F32) for k in range(N_DEV)]
                same = [got[7], got[0], got[1], got[2]]
                other = got[3:7]
                core0, core1 = (same, other) if core == 0 else (other, same)
                g = core0[0] + core1[0]
                for r in range(1, N_CHIPS):
                    g = (g + core0[r]) + core1[r]
                _adam_update(g, w_ref, m_ref, v_ref, g_ref, d_ref, nm_ref, nv_ref)

    blk = pl.BlockSpec((ta, b), lambda i: (i, 0))
    return pl.pallas_call(
        body, name=name, grid=(a // ta,),
        in_specs=[pl.BlockSpec((N_DEV, ta, b), lambda i: (0, i, 0)), blk, blk, blk], out_specs=[blk] * 4,
        out_shape=[jax.ShapeDtypeStruct(w.shape, F32)] * 4, compiler_params=_params(("parallel",)),
    )(recv, w, m, v)


def _adamw_small(g, w, m, v):
    def body(g_in, w_ref, m_ref, v_ref, g_ref, d_ref, nm_ref, nv_ref):
        _adam_update(g_in[...], w_ref, m_ref, v_ref, g_ref, d_ref, nm_ref, nv_ref)

    blk = pl.BlockSpec(w.shape, lambda i: (0, 0))
    return pl.pallas_call(
        body, name="adamw_small", grid=(1,), in_specs=[blk] * 4, out_specs=[blk] * 4,
        out_shape=[jax.ShapeDtypeStruct(w.shape, F32)] * 4, compiler_params=_params(("arbitrary",)),
    )(g, w, m, v)


def _remote(src, dst, send_sem, recv_sem, device):
    return pltpu.make_async_remote_copy(src_ref=src, dst_ref=dst, send_sem=send_sem, recv_sem=recv_sem,
                                        device_id=device, device_id_type=MESH)


def _place():
    x, y, c = lax.axis_index("x"), lax.axis_index("y"), lax.axis_index("c")
    return x, y, c, 2 * x + y, [(1 - x, y), (x, 1 - y), (1 - x, 1 - y)]


HBM_REF = pl.BlockSpec(memory_space=pl.ANY)
HALF_ROWS_QUANTUM = 16


def _gather_sems(n):
    return [pltpu.SemaphoreType.DMA((3 * n,))] * 4 + [pltpu.SemaphoreType.DMA((n,))]


def _gather_copies(ins, outs, sems):
    send_s, recv_s, fsend_s, frecv_s, local_s = sems
    x, y, c, me, chips = _place()
    local, sends, steps = [], [], []
    for i, (src, dst) in enumerate(zip(ins, outs)):
        local.append(pltpu.make_async_copy(src, dst.at[me], local_s.at[i]))
        half = src.shape[0] // 2
        split = src.shape[0] % (2 * HALF_ROWS_QUANTUM) == 0
        if split:
            mine = pl.ds(pl.multiple_of(c * half, HALF_ROWS_QUANTUM), half)
            theirs = pl.ds(pl.multiple_of((1 - c) * half, HALF_ROWS_QUANTUM), half)
        for r, (px, py) in enumerate(chips):
            k, peer = 3 * i + r, 2 * px + py
            if split:
                sends.append(_remote(src.at[mine], dst.at[me, mine], send_s.at[k], recv_s.at[k], (px, py, c)))
                landed = dst.at[peer, mine]
                steps.append((_remote(src.at[mine], landed, send_s.at[k], recv_s.at[k], (px, py, c)),
                              _remote(landed, landed, fsend_s.at[k], frecv_s.at[k], (x, y, 1 - c)),
                              _remote(dst.at[peer, theirs], dst.at[peer, theirs], fsend_s.at[k], frecv_s.at[k], (x, y, 1 - c))))
            else:
                sends.append(_remote(src, dst.at[me], send_s.at[k], recv_s.at[k], (px, py, c)))
                steps.append((_remote(src, dst.at[peer], send_s.at[k], recv_s.at[k], (px, py, c)), None, None))
    return local, sends, steps


def _scatter_sems(n):
    return [pltpu.SemaphoreType.DMA((4 * n,))] * 2 + [pltpu.SemaphoreType.DMA((3 * n,))] * 2 + [pltpu.SemaphoreType.DMA((n,))]


def _scatter_copies(ins, outs, sems):
    send_s, recv_s, fsend_s, frecv_s, local_s = sems
    x, y, c, me, chips = _place()
    local, sends, steps = [], [], []
    for i, (src, dst) in enumerate(zip(ins, outs)):
        local.append(pltpu.make_async_copy(src.at[me], dst.at[N_DEV - 1], local_s.at[i]))
        for r, (px, py) in enumerate(chips):
            k = 4 * i + r
            cp = _remote(src.at[2 * px + py], dst.at[r], send_s.at[k], recv_s.at[k], (px, py, c))
            fwd = _remote(dst.at[r], dst.at[4 + r], fsend_s.at[3 * i + r], frecv_s.at[3 * i + r], (x, y, 1 - c))
            sends.append(cp)
            steps.append((cp, fwd, fwd))
        k = 4 * i + 3
        cp = _remote(src.at[me], dst.at[3], send_s.at[k], recv_s.at[k], (x, y, 1 - c))
        sends.append(cp)
        steps.append((cp, None, None))
    return local, sends, steps


def _exchange_start(plan):
    local, sends, _ = plan
    for cp in local + sends:
        cp.start()


def _exchange_pass_on(plan):
    for arrival, pass_on, _ in plan[2]:
        arrival.wait_recv()
        if pass_on is not None:
            pass_on.start()


def _exchange_finish(plan):
    local, sends, steps = plan
    for _, pass_on, passed in steps:
        if pass_on is not None:
            passed.wait_recv()
    for cp in sends:
        cp.wait_send()
    for _, pass_on, _ in steps:
        if pass_on is not None:
            pass_on.wait_send()
    for cp in local:
        cp.wait()


def _exchange_call(arrays, copies, sems, out_shapes, name):
    n = len(arrays)

    def body(*refs):
        plan = copies(refs[:n], refs[n:2 * n], refs[2 * n:])
        _exchange_start(plan)
        _exchange_pass_on(plan)
        _exchange_finish(plan)

    return pl.pallas_call(
        body, name=name, in_specs=[HBM_REF] * n, out_specs=[HBM_REF] * n, out_shape=out_shapes,
        scratch_shapes=sems(n), compiler_params=pltpu.CompilerParams(has_side_effects=True),
    )(*arrays)


def _gather_call(shards, name):
    shapes = [jax.ShapeDtypeStruct((N_CHIPS,) + a.shape, a.dtype) for a in shards]
    return _exchange_call(shards, _gather_copies, _gather_sems, shapes, name)


def _all_reduce_small(a):
    def body(in_ref, out_ref, slots, send_sems, recv_sems):
        x, y, c = lax.axis_index("x"), lax.axis_index("y"), lax.axis_index("c")
        me = 4 * x + 2 * y + c
        slots[0] = in_ref[...]
        sends = []
        for r in range(1, N_DEV):
            peer = (x ^ (r >> 2), y ^ ((r >> 1) & 1), c ^ (r & 1))
            sends.append(pltpu.make_async_remote_copy(src_ref=in_ref, dst_ref=slots.at[r], send_sem=send_sems.at[r],
                                                      recv_sem=recv_sems.at[r], device_id=peer, device_id_type=MESH))
        for cp in sends:
            cp.start()
        for cp in sends:
            cp.wait_recv()
        acc = slots[me]
        for dev in range(1, N_DEV):
            acc = acc + slots[dev ^ me]
        out_ref[...] = acc
        for cp in sends:
            cp.wait_send()

    return pl.pallas_call(
        body, name="small_all_reduce",
        in_specs=[pl.BlockSpec(memory_space=pltpu.VMEM)], out_specs=pl.BlockSpec(memory_space=pltpu.VMEM),
        out_shape=jax.ShapeDtypeStruct(a.shape, a.dtype),
        scratch_shapes=[pltpu.VMEM((N_DEV,) + a.shape, a.dtype), pltpu.SemaphoreType.DMA((N_DEV,)),
                        pltpu.SemaphoreType.DMA((N_DEV,))],
        compiler_params=pltpu.CompilerParams(has_side_effects=True),
    )(a)


SHARDED = ["w_in", "mla_w_q_up", "mla_w_kv_up", "w_out", "ffn_w_up", "ffn_w_down", "ple_w_gate", "ple_w_proj",
           "gdn_conv_w", "ffn_conv_w"]
SHARD_AXIS = {"w_in": 1, "mla_w_q_up": 1, "mla_w_kv_up": 1, "w_out": 0, "ffn_w_up": 1, "ffn_w_down": 0,
              "ple_w_gate": 0, "ple_w_proj": 1, "gdn_conv_w": 1, "ffn_conv_w": 1}
SMALL = ["gdn_a_log", "gdn_dt_bias", "gdn_norm_g", "mla_q_norm_g", "mla_kv_norm_g", "ln1_g", "ln1_b", "ffn_conv_b",
         "ple_b_gate", "ln2_g", "ln2_b"]
WEIGHTS = ["w_in", "gdn_conv_w", "gdn_a_log", "gdn_dt_bias", "gdn_norm_g", "mla_q_norm_g", "mla_w_q_up", "mla_kv_norm_g",
           "mla_w_kv_up", "w_out", "ln1_g", "ln1_b", "ffn_w_up", "ffn_conv_w", "ffn_conv_b", "ffn_w_down", "ple_w_gate",
           "ple_b_gate", "ple_w_proj", "ln2_g", "ln2_b"]
F32_ON_WIRE = ("gdn_conv_w", "ffn_conv_w")
GATHER_EARLY = ["w_in", "gdn_conv_w"]
GATHER_LATE = ["mla_w_q_up", "mla_w_kv_up", "w_out", "ffn_w_up", "ffn_conv_w", "ffn_w_down", "ple_w_gate", "ple_w_proj"]
SCATTER_EARLY = ["ffn_w_up", "ffn_conv_w", "ffn_w_down", "ple_w_gate", "ple_w_proj", "w_out"]
SCATTER_LATE = ["w_in", "gdn_conv_w", "mla_w_q_up", "mla_w_kv_up"]
PACK_COLS = 1024
PACK_ROW_TILE = 8


def _join_blocks(blocks, axis):
    n, a, b = blocks.shape
    if axis == 0:
        return blocks.reshape(n * a, b)
    return jnp.transpose(blocks, (1, 0, 2)).reshape(a, n * b)


def _split_blocks(full, axis):
    if axis == 0:
        return full.reshape(N_CHIPS, full.shape[0] // N_CHIPS, full.shape[1])
    a, nb = full.shape
    return jnp.transpose(full.reshape(a, N_CHIPS, nb // N_CHIPS), (1, 0, 2))


def _pack(arrays):
    flat = jnp.concatenate([a.reshape(-1) for a in arrays])
    quantum = PACK_COLS * PACK_ROW_TILE
    padded = -(-flat.shape[0] // quantum) * quantum
    return jnp.pad(flat, (0, padded - flat.shape[0])).reshape(-1, PACK_COLS)


def _unpack(packed, shapes):
    flat = packed.reshape(-1)
    out, off = [], 0
    for shp in shapes:
        n = int(np.prod(shp))
        out.append(flat[off:off + n].reshape(shp))
        off += n
    return out


def kernel(x, p, w_in, gdn_conv_w, gdn_a_log, gdn_dt_bias, gdn_norm_g, mla_q_norm_g, mla_w_q_up, mla_kv_norm_g, mla_w_kv_up, w_out, ln1_g, ln1_b, ffn_w_up, ffn_conv_w, ffn_conv_b, ffn_w_down, ple_w_gate, ple_b_gate, ple_w_proj, ln2_g, ln2_b, loss_target, m_w_in, m_gdn_conv_w, m_gdn_a_log, m_gdn_dt_bias, m_gdn_norm_g, m_mla_q_norm_g, m_mla_w_q_up, m_mla_kv_norm_g, m_mla_w_kv_up, m_w_out, m_ln1_g, m_ln1_b, m_ffn_w_up, m_ffn_conv_w, m_ffn_conv_b, m_ffn_w_down, m_ple_w_gate, m_ple_b_gate, m_ple_w_proj, m_ln2_g, m_ln2_b, v_w_in, v_gdn_conv_w, v_gdn_a_log, v_gdn_dt_bias, v_gdn_norm_g, v_mla_q_norm_g, v_mla_w_q_up, v_mla_kv_norm_g, v_mla_w_kv_up, v_w_out, v_ln1_g, v_ln1_b, v_ffn_w_up, v_ffn_conv_w, v_ffn_conv_b, v_ffn_w_down, v_ple_w_gate, v_ple_b_gate, v_ple_w_proj, v_ln2_g, v_ln2_b):
    given = dict(locals())
    wsh = {n: given[n][0] for n in WEIGHTS}
    msh = {n: given["m_" + n][0] for n in WEIGHTS}
    vsh = {n: given["v_" + n][0] for n in WEIGHTS}
    bl, s, _ = x.shape
    t = bl * s
    xt = x.reshape(t, D_MODEL)
    pt = p.reshape(t, PLE_DIM)
    target = loss_target.reshape(t, D_MODEL)

    wire = lambda n: wsh[n] if n in F32_ON_WIRE else wsh[n].astype(BF16)
    early = _gather_call([wire(n) for n in GATHER_EARLY], "weights_gather_early")
    full = {n: _join_blocks(g, SHARD_AXIS[n]) for n, g in zip(GATHER_EARLY, early)}
    late_shards = [wire(n) for n in GATHER_LATE]
    late_ride = (late_shards, _gather_copies, _gather_sems,
                 [jax.ShapeDtypeStruct((N_CHIPS,) + a.shape, a.dtype) for a in late_shards], 0.75)

    in_cols, q_cols = _w_in_cols(), _w_q_cols()
    w_in_p = _pad_cols(full["w_in"], in_cols)
    gconv = full["gdn_conv_w"]
    row = lambda a: a.reshape(1, -1)
    sc = jnp.zeros((8, 128), F32).at[0, :GDN_HEADS].set(wsh["gdn_a_log"]).at[1, :GDN_HEADS].set(wsh["gdn_dt_bias"])
    norm_g, qg, kvg = row(wsh["gdn_norm_g"]), row(wsh["mla_q_norm_g"]), row(wsh["mla_kv_norm_g"])
    g1, b1, g2, b2 = row(wsh["ln1_g"]), row(wsh["ln1_b"]), row(wsh["ln2_g"]), row(wsh["ln2_b"])
    fbias, bgate = row(wsh["ffn_conv_b"]), row(wsh["ple_b_gate"])

    inv = ROPE_THETA ** (-jnp.arange(0, MLA_ROPE, 2, dtype=F32) / MLA_ROPE)
    ang = jnp.arange(s, dtype=F32)[:, None] * inv[None, :]
    zero = jnp.zeros_like(ang)
    cos_t = jnp.concatenate([jnp.cos(ang), zero, jnp.cos(ang), zero], axis=1)
    sin_t = jnp.concatenate([-jnp.sin(ang), zero, jnp.sin(ang), zero], axis=1)

    proj = _matmul(xt, w_in_p, name="proj", tm=1024)
    cat, o_raw, states, qkvg, late = _gdn_fwd(proj, gconv, sc, norm_g, bl, s, late_ride)
    w_up = late[GATHER_LATE.index("ffn_w_up")]
    full.update({n: _join_blocks(g, SHARD_AXIS[n]) for n, g in zip(GATHER_LATE, late) if n != "ffn_w_up"})
    w_o, w_down = full["w_out"], full["ffn_w_down"]
    w_gate, w_proj, fconv = full["ple_w_gate"], full["ple_w_proj"], full["ffn_conv_w"]
    w_q_p, w_kv = _pad_cols(full["mla_w_q_up"], q_cols), full["mla_w_kv_up"]
    qf, kvf, kr = _mla_prep_fwd(proj, qg, kvg, w_q_p, w_kv, cos_t, sin_t, s, 256)
    cat, attn_o32, attn_lse = _attn_fwd(qf, kvf, kr, cat, bl, s, 512)
    wide = dict(tm=1024, tn=1024)
    r1, h1, h1b, xb = _mix_ln1_fwd(xt, cat, w_o, g1, b1, 512)
    u = _matmul(h1b, w_up, name="ffn_up", tm=1024, tn=1408, b_parts=N_CHIPS)
    act = _ffn_act_fwd(u, fconv, fbias, bl, s, 256)
    dr2, dr2b, dgpre, dpp, loss_acc, dbgate, dg2, db2 = _head(h1, h1b, act, pt, w_down, w_gate, w_proj, bgate, g2, b2, target, 256)

    dact = _matmul(dr2b, w_down, name="d_act", tb=True, tm=1024, tn=1408)
    long_k = dict(ta=True, tk=2048)
    d_w_down = _matmul(act, dr2b, name="dw_down", tm=1408, tn=1024, **long_k)
    du, dfcw_g, dfcw_u, dfcb_g, dfcb_u = _ffn_act_bwd(u, fconv, fbias, dact, bl, s, 256)
    dh1_a = _matmul(du, w_up, name="dh1_ffn", tb=True, tk=1408, a_halves=True, b_parts=N_CHIPS, **wide)
    d_w_up = _matmul(h1b, du, name="dw_up", tn=1408, b_parts=2, out_parts=N_CHIPS, out_dtype=BF16, **long_k)
    d_w_gate = _matmul(h1b, dgpre, name="dw_gate", **long_k, **wide)
    d_w_proj = _matmul(pt, dpp, name="dw_proj", ta=True, tn=1024)
    dr1, dr1b, dcat, dg1, db1 = _ln1_bwd(r1, dr2, dh1_a, dgpre, w_gate, w_o, g1, b1, 256)
    d_w_o = _matmul(cat, dr1b, name="dw_out", **long_k, **wide)

    gfull = {
        "ffn_w_down": d_w_down, "ple_w_gate": d_w_gate, "ple_w_proj": d_w_proj, "w_out": d_w_o,
        "ffn_conv_w": jnp.concatenate([jnp.sum(dfcw_g, 0), jnp.sum(dfcw_u, 0)], axis=1),
    }
    slabs = {n: _split_blocks(g, SHARD_AXIS[n]).astype(BF16) for n, g in gfull.items()}
    slabs["ffn_w_up"] = d_w_up
    early_slabs = [slabs[n] for n in SCATTER_EARLY]
    early_ride = (early_slabs, _scatter_copies, _scatter_sems,
                  [jax.ShapeDtypeStruct((N_DEV,) + a.shape[1:], a.dtype) for a in early_slabs], 0.7)
    dproj, dab, dcwq, dcwk, dcwv, dsc, dng, early_recv = _gdn_bwd(proj, gconv, sc, norm_g, o_raw, states, qkvg, dcat, bl, s,
                                                                  early_ride)
    received = dict(zip(SCATTER_EARLY, early_recv))
    dqf, dkvf, dkr = _attn_bwd(qf, kvf, kr, dcat, attn_o32, attn_lse, bl, s, 256)
    dproj, dqg, dkvg, d_w_q_p, d_w_kv = _mla_prep_bwd(proj, qg, kvg, w_q_p, w_kv, cos_t, sin_t, dqf, dkvf, dkr, dab, dproj, s, 256)
    d_w_in_p = _matmul(xb, dproj, name="dw_in", **long_k, **wide)

    gfull.update({
        "w_in": _unpad_cols(d_w_in_p, in_cols, D_IN),
        "mla_w_q_up": _unpad_cols(d_w_q_p, q_cols, MLA_HEADS * (MLA_NOPE + MLA_ROPE)),
        "mla_w_kv_up": d_w_kv,
        "gdn_conv_w": jnp.concatenate([jnp.sum(dcwq, 0), jnp.sum(dcwk, 0), jnp.sum(dcwv, 0)], axis=1),
    })
    slabs.update({n: _split_blocks(gfull[n], SHARD_AXIS[n]).astype(BF16) for n in SCATTER_LATE})
    late_slabs = [slabs[n] for n in SCATTER_LATE]
    late_scatter = (late_slabs, _scatter_copies, _scatter_sems,
                    [jax.ShapeDtypeStruct((N_DEV,) + a.shape[1:], a.dtype) for a in late_slabs], 0.85)
    grad_x, late_recv = _matmul(dproj, w_in_p, name="d_x", tb=True, add=dr1, add_scale=ALPHA, ride=late_scatter, **wide)
    received.update(zip(SCATTER_LATE, late_recv))
    dsc_sum = jnp.sum(dsc, axis=(0, 1))
    gsmall = {
        "gdn_a_log": dsc_sum[0, :GDN_HEADS], "gdn_dt_bias": dsc_sum[1, :GDN_HEADS],
        "gdn_norm_g": jnp.sum(dng[:, :, 0, :], axis=(0, 1)),
        "mla_q_norm_g": dqg[0], "mla_kv_norm_g": dkvg[0], "ln1_g": dg1[0], "ln1_b": db1[0],
        "ffn_conv_b": jnp.concatenate([jnp.sum(dfcb_g, 0), jnp.sum(dfcb_u, 0)], axis=1)[0],
        "ple_b_gate": dbgate[0], "ln2_g": dg2[0], "ln2_b": db2[0],
    }

    big = [{}, {}, {}, {}]
    for n in SHARDED:
        for kind, val in enumerate(_adamw_reduced(received[n], wsh[n], msh[n], vsh[n], "adamw_" + n)):
            big[kind][n] = val

    small_shapes = [wsh[n].shape for n in SMALL]
    gsum = _all_reduce_small(_pack([gsmall[n] for n in SMALL]))
    spacks = _adamw_small(gsum, _pack([wsh[n] for n in SMALL]), _pack([msh[n] for n in SMALL]), _pack([vsh[n] for n in SMALL]))
    small = [dict(zip(SMALL, _unpack(pk, small_shapes))) for pk in spacks]

    loss = lax.psum(loss_acc[0, 0], ("x", "y", "c"))
    outs = [loss, grad_x.reshape(x.shape)]
    for kind in range(4):
        for n in WEIGHTS:
            val = big[kind][n] if n in big[kind] else small[kind][n]
            outs.append(val[None])
    return tuple(outs)
```

```python
import functools
import math

import numpy as np
import jax
import jax.numpy as jnp
from jax import lax
from jax.experimental import pallas as pl
from jax.experimental.pallas import tpu as pltpu

F32 = jnp.float32
BF16 = jnp.bfloat16

D_MODEL = 1024
CHUNK = 64
PLE_DIM = 256
GDN_HEADS = 4
GDN_DK = 128
GDN_DV = 128
GDN_CONV = 4
MLA_HEADS = 4
MLA_NOPE = 128
MLA_ROPE = 64
MLA_V = 128
MLA_Q_LORA = 384
MLA_KV_LORA = 256
ROPE_THETA = 10000.0
D_FF = 2816
FFN_CONV = 3
DEPTH = 1
ALPHA = (2.0 * DEPTH) ** 0.25
NORM_EPS = 1e-6
GDN_QK = GDN_HEADS * GDN_DK
GDN_VW = GDN_HEADS * GDN_DV
D_IN = 2 * GDN_QK + 2 * GDN_VW + 2 * GDN_HEADS + MLA_Q_LORA + MLA_KV_LORA + MLA_ROPE
ATT_SCALE = (MLA_NOPE + MLA_ROPE) ** -0.5

ADAM_LR = 0.001
ADAM_B1 = 0.9
ADAM_B2 = 0.999
ADAM_EPS = 1e-08
ADAM_WD = 0.01
ADAM_STEP = 10

LANES = 128
VMEM_LIMIT = 60 * 1024 * 1024
GDN_FWD_GROUP = 16
GDN_BWD_GROUP = 16
N_CHIPS = 4
N_DEV = 8

P_WIDTH = 3072
P_MLA = 2048
MESH = pl.DeviceIdType.MESH


def _rope_slot(j):
    return j if j < MLA_ROPE // 2 else 64 + (j - MLA_ROPE // 2)


def _w_in_cols():
    idx = -np.ones((P_WIDTH,), np.int64)
    for h in range(GDN_HEADS):
        base = h * 512
        idx[base:base + 128] = np.arange(128) + h * GDN_DK
        idx[base + 128:base + 256] = np.arange(128) + GDN_QK + h * GDN_DK
        idx[base + 256:base + 384] = np.arange(128) + 2 * GDN_QK + h * GDN_DV
        idx[base + 384:base + 512] = np.arange(128) + 2 * GDN_QK + GDN_VW + h * GDN_DV
    o_a = 2 * GDN_QK + 2 * GDN_VW
    idx[P_MLA:P_MLA + 2 * GDN_HEADS] = np.arange(2 * GDN_HEADS) + o_a
    o_cq = o_a + 2 * GDN_HEADS
    idx[P_MLA + 128:P_MLA + 512] = np.arange(MLA_Q_LORA) + o_cq
    o_ckv = o_cq + MLA_Q_LORA
    idx[P_MLA + 512:P_MLA + 768] = np.arange(MLA_KV_LORA) + o_ckv
    o_kr = o_ckv + MLA_KV_LORA
    for j in range(MLA_ROPE):
        idx[P_MLA + 768 + _rope_slot(j)] = o_kr + j
    return idx


def _w_q_cols():
    idx = -np.ones((MLA_HEADS * 256,), np.int64)
    for h in range(MLA_HEADS):
        o = h * (MLA_NOPE + MLA_ROPE)
        idx[h * 256:h * 256 + 128] = np.arange(128) + o
        for j in range(MLA_ROPE):
            idx[h * 256 + 128 + _rope_slot(j)] = o + MLA_NOPE + j
    return idx


def _pad_cols(w, idx):
    safe = np.where(idx >= 0, idx, 0)
    return jnp.where(jnp.asarray(idx >= 0)[None, :], w[:, safe], 0.0)


def _unpad_cols(wp, idx, n):
    inv = np.zeros((n,), np.int64)
    inv[idx[idx >= 0]] = np.nonzero(idx >= 0)[0]
    return wp[:, inv]


def _dot(a, b, ca, cb, precision=None):
    if precision is None:
        a = a.astype(BF16)
        b = b.astype(BF16)
    return lax.dot_general(a, b, (((ca,), (cb,)), ((), ())), preferred_element_type=F32, precision=precision)


@jax.custom_vjp
def mm(a, b):
    return _dot(a, b, 1, 0)


@jax.custom_vjp
def mm_nt(a, b):
    return _dot(a, b, 1, 1)


@jax.custom_vjp
def mm_tn(a, b):
    return _dot(a, b, 0, 0)


mm.defvjp(lambda a, b: (mm(a, b), (a, b)), lambda r, g: (mm_nt(g, r[1]), mm_tn(r[0], g)))
mm_nt.defvjp(lambda a, b: (mm_nt(a, b), (a, b)), lambda r, g: (mm(g, r[1]), mm_tn(g, r[0])))
mm_tn.defvjp(lambda a, b: (mm_tn(a, b), (a, b)), lambda r, g: (mm_nt(r[1], g), mm(r[0], g)))

def _split(a):
    hi = a.astype(BF16)
    return hi, (a - hi.astype(F32)).astype(BF16)


def _dot3(a, b, ca, cb):
    a_hi, a_lo = _split(a)
    b_hi, b_lo = _split(b)
    return (_dot(a_hi, b_hi, ca, cb) + _dot(a_hi, b_lo, ca, cb)) + _dot(a_lo, b_hi, ca, cb)


def _shift_rows(x, s):
    return x if s == 0 else pltpu.roll(x, s % x.shape[0], 0)


def _row(w, j):
    tap = lax.broadcasted_iota(jnp.int32, w.shape, 0)
    return jnp.sum(jnp.where(tap == j, w, 0.0), axis=0, keepdims=True)


@jax.custom_vjp
def dwconv(x, w):
    k = w.shape[0]
    y = _row(w, k - 1) * x
    for j in range(k - 1):
        y = y + _row(w, j) * _shift_rows(x, k - 1 - j)
    return y


def _dwconv_fwd(x, w):
    return dwconv(x, w), (x, w)


def _dwconv_bwd(res, dy):
    x, w = res
    k = w.shape[0]
    dx = _row(w, k - 1) * dy
    tap = lax.broadcasted_iota(jnp.int32, w.shape, 0)
    dw = jnp.where(tap == k - 1, jnp.sum(dy * x, axis=0, keepdims=True), 0.0)
    for j in range(k - 1):
        dx = dx + _row(w, j) * _shift_rows(dy, -(k - 1 - j))
        dw = dw + jnp.where(tap == j, jnp.sum(dy * _shift_rows(x, k - 1 - j), axis=0, keepdims=True), 0.0)
    return dx, dw


dwconv.defvjp(_dwconv_fwd, _dwconv_bwd)


@jax.custom_vjp
def rope128(x, cos, sin):
    return x * cos + pltpu.roll(x, 64, 1) * sin


rope128.defvjp(lambda x, c, s: (rope128(x, c, s), (c, s)),
               lambda r, g: (g * r[0] + pltpu.roll(g * r[1], 64, 1), jnp.zeros_like(r[0]), jnp.zeros_like(r[1])))


def _silu(x):
    return x * jax.nn.sigmoid(x)


def _softplus(x):
    return jnp.maximum(x, 0.0) + jnp.log(1.0 + jnp.exp(-jnp.abs(x)))


def _rmsnorm(x, g):
    return x * lax.rsqrt(jnp.mean(x * x, axis=-1, keepdims=True) + NORM_EPS) * g


def _layernorm(x, g, b):
    mu = jnp.mean(x, axis=-1, keepdims=True)
    xc = x - mu
    var = jnp.mean(xc * xc, axis=-1, keepdims=True)
    return xc * lax.rsqrt(var + NORM_EPS) * g + b


def _pick_lane(row, lane):
    idx = lax.broadcasted_iota(jnp.int32, row.shape, 1)
    return jnp.sum(jnp.where(idx == lane, row, 0.0), axis=1, keepdims=True)


def _gdn_q(pq, cw):
    h = _silu(dwconv(pq, cw))
    return h * lax.rsqrt(jnp.sum(h * h, axis=-1, keepdims=True) + NORM_EPS) * (GDN_DK ** -0.5)


def _gdn_k(pk, cw):
    h = _silu(dwconv(pk, cw))
    return h * lax.rsqrt(jnp.sum(h * h, axis=-1, keepdims=True) + NORM_EPS)


def _gdn_v(pv, cw):
    return _silu(dwconv(pv, cw))


def _gdn_gate(ab, sc, head):
    a = _pick_lane(ab, head)
    b = _pick_lane(ab, GDN_HEADS + head)
    a_log = _pick_lane(_row(sc, 0), head)
    dt_bias = _pick_lane(_row(sc, 1), head)
    beta = jax.nn.sigmoid(b)
    g = -jnp.exp(a_log) * _softplus(a + dt_bias)
    return _two_lanes(g, beta)


def _two_lanes(c0, c1):
    lane = lax.broadcasted_iota(jnp.int32, (c0.shape[0], LANES), 1)
    return jnp.where(lane == 0, c0, jnp.where(lane == 1, c1, 0.0))


def _run(levels):
    try:
        while True:
            next(levels)
    except StopIteration as stop:
        return stop.value


def _inverse_levels(lows):
    n = lows[0].shape[0]
    ii = lax.broadcasted_iota(jnp.int32, (n, n), 0)
    jj = lax.broadcasted_iota(jnp.int32, (n, n), 1)
    eye = jnp.where(ii == jj, 1.0, 0.0)
    invs = [eye - low for low in lows]
    powers = [_dot3(low, low, 1, 0) for low in lows]
    yield
    k = 2
    while k < n:
        invs = [inv + _dot3(inv, p, 1, 0) for inv, p in zip(invs, powers)]
        yield
        k *= 2
        if k < n:
            powers = [_dot3(p, p, 1, 0) for p in powers]
            yield
    return invs


def _inverse_group(lows):
    return _run(_inverse_levels(lows))


@jax.custom_vjp
def solve_group(lows, rhss):
    return [_dot3(inv, rhs, 1, 0) for inv, rhs in zip(_inverse_group(lows), rhss)]


def _solve_whole(lows, rhss):
    return solve_group(lows, rhss)
    yield


def _solve_levels(lows, rhss):
    invs = yield from _inverse_levels(lows)
    return [_dot3(inv, rhs, 1, 0) for inv, rhs in zip(invs, rhss)]


def _solve_group_fwd(lows, rhss):
    invs = _inverse_group(lows)
    xs = [_dot3(inv, rhs, 1, 0) for inv, rhs in zip(invs, rhss)]
    return xs, (invs, xs)


def _solve_group_bwd(res, dxs):
    invs, xs = res
    n = invs[0].shape[0]
    strict = lax.broadcasted_iota(jnp.int32, (n, n), 0) > lax.broadcasted_iota(jnp.int32, (n, n), 1)
    drhss = [_dot3(inv, dx, 0, 0) for inv, dx in zip(invs, dxs)]
    dlows = [jnp.where(strict, -_dot3(drhs, x, 1, 1), 0.0) for drhs, x in zip(drhss, xs)]
    return dlows, drhss


solve_group.defvjp(_solve_group_fwd, _solve_group_bwd)


def _gdn_local_group(qs, ks, vs, gbs):
    return _run(_gdn_local_levels(qs, ks, vs, gbs, _solve_whole))


def _gdn_local_levels(qs, ks, vs, gbs, solve):
    c = qs[0].shape[0]
    ii = lax.broadcasted_iota(jnp.int32, (c, c), 0)
    jj = lax.broadcasted_iota(jnp.int32, (c, c), 1)
    incl = ii >= jj
    gs = [_pick_lane(gb, 0) for gb in gbs]
    betas = [_pick_lane(gb, 1) for gb in gbs]
    g_rows = [jnp.sum(jnp.where(ii == jj, g, 0.0), axis=0, keepdims=True) for g in gs]
    gc_cols = [jnp.sum(jnp.where(incl, g_row, 0.0), axis=1, keepdims=True) for g_row in g_rows]
    gc_rows = [jnp.sum(jnp.where(jj >= ii, g, 0.0), axis=0, keepdims=True) for g in gs]
    decays = [jnp.where(incl, jnp.exp(jnp.where(incl, gc - gr, 0.0)), 0.0) for gc, gr in zip(gc_cols, gc_rows)]
    yield
    kbs = [k * beta for k, beta in zip(ks, betas)]
    lows = [jnp.where(ii > jj, mm_nt(kb, k) * decay, 0.0) for kb, k, decay in zip(kbs, ks, decays)]
    egs = [jnp.exp(gc) for gc in gc_cols]
    yield
    wus = yield from solve(lows, [jnp.concatenate([kb * eg, v * beta], axis=1) for kb, eg, v, beta in zip(kbs, egs, vs, betas)])
    yield
    qks = [mm_nt(q, k) * decay for q, k, decay in zip(qs, ks, decays)]
    g_lasts = [jnp.sum(g_row, axis=1, keepdims=True) for g_row in g_rows]
    kds = [k * jnp.exp(gl - gc) for k, gl, gc in zip(ks, g_lasts, gc_cols)]
    yield
    ws, us = [wu[:, :GDN_DK] for wu in wus], [wu[:, GDN_DK:] for wu in wus]
    q_effs = [q * eg - mm(qk, w) for q, eg, qk, w in zip(qs, egs, qks, ws)]
    yield
    o_locals = [mm(qk, u) for qk, u in zip(qks, us)]
    yield
    mixes = [mm_tn(kd, w) for kd, w in zip(kds, ws)]
    yield
    adds = [mm_tn(kd, u) for kd, u in zip(kds, us)]
    return [(q_eff, o_loc, mix, add, jnp.exp(gl))
            for q_eff, o_loc, mix, add, gl in zip(q_effs, o_locals, mixes, adds, g_lasts)]


def _gdn_state_step(q_eff, o_local, mix, add, eg_last, state):
    return mm(q_eff, state) + o_local, state * eg_last - mm(mix, state) + add


def _gdn_post(o, z, norm_g):
    return _rmsnorm(o, norm_g) * _silu(z)


MASKED = -1e30


def _scores(q, kn, kr, q0, k0):
    s = (mm_nt(q[:, :128], kn) + mm_nt(q[:, 128:], kr)) * ATT_SCALE
    if k0 + kn.shape[0] <= q0:
        return s
    qpos = q0 + lax.broadcasted_iota(jnp.int32, s.shape, 0)
    kpos = k0 + lax.broadcasted_iota(jnp.int32, s.shape, 1)
    shift = int(math.log2(CHUNK))
    return jnp.where((kpos >> shift) <= (qpos >> shift), s, MASKED)


def _softmax_times(s, v):
    top = jnp.max(s, axis=-1, keepdims=True)
    p = jnp.exp(s - top)
    norm = jnp.sum(p, axis=-1, keepdims=True)
    return mm(p / norm, v), top + jnp.log(norm)


def _mla_prep(pm, qg, kvg, wq, wkv, cos, sin):
    cq = pm[:, 128:512]
    ckv = pm[:, 512:768]
    qf = mm(_rmsnorm(cq, qg), wq)
    parts = []
    for h in range(MLA_HEADS):
        parts.append(qf[:, h * 256:h * 256 + 128])
        parts.append(rope128(qf[:, h * 256 + 128:h * 256 + 256], cos, sin))
    kvf = mm(_rmsnorm(ckv, kvg), wkv)
    return jnp.concatenate(parts, axis=1), kvf, rope128(pm[:, 768:896], cos, sin)


def _ffn_act(ug, uu, wg, wu, bg, bu):
    return _silu(dwconv(ug, wg) + bg) * (dwconv(uu, wu) + bu)


def _head_loss(h1, ffn, gpre, pp, bgate, g2, b2, target):
    gate = jax.nn.sigmoid(gpre + bgate)
    h2 = _layernorm(ALPHA * h1 + ffn + gate * pp, g2, b2)
    err = h2 - target
    return 0.5 * jnp.sum(jnp.sum(err * err, axis=1, keepdims=True), axis=0, keepdims=True) / D_MODEL


ROW_TILE_VREGS = 32
CONV_HALO = 8


def _rows_per_tile(n_rows, cols):
    tile = min(n_rows, ROW_TILE_VREGS * 8 * LANES // cols)
    assert n_rows % tile == 0 and tile % CONV_HALO == 0, (n_rows, cols)
    return tile


def _tile_inputs(loads, t0, first, halo, tile):
    if halo == 0:
        return [ld(pl.ds(t0, tile)) for ld in loads]
    if first:
        xs = [ld(pl.ds(0, tile)) for ld in loads]
        return [jnp.concatenate([jnp.zeros((halo, x.shape[1]), x.dtype), x], axis=0) for x in xs]
    return [ld(pl.ds(pl.multiple_of(t0 - halo, CONV_HALO), tile + halo)) for ld in loads]


def _rows_apply(fn, loads, consts, store, n_rows, cols, halo):
    tile = _rows_per_tile(n_rows, cols)

    def one(t0, first):
        y = fn(*_tile_inputs(loads, t0, first, halo, tile), *consts)
        store(pl.ds(t0, tile), y[halo:] if halo else y)

    one(0, True)

    def step(i, carry):
        one(pl.multiple_of(i * tile, tile), False)
        return carry

    lax.fori_loop(1, n_rows // tile, step, 0)


def _rows_vjp(fn, loads, consts, load_dy, stores, n_rows, cols, halo):
    tile = _rows_per_tile(n_rows, cols)

    def one(t0, first, dconsts):
        xs = _tile_inputs(loads, t0, first, halo, tile)
        _, vjp = jax.vjp(lambda *a: fn(*a)[halo:] if halo else fn(*a), *xs, *consts)
        grads = vjp(load_dy(pl.ds(t0, tile)))
        for st, dx in zip(stores, grads[:len(xs)]):
            st(pl.ds(t0, tile), dx[halo:] if halo else dx, False)
            if halo and not first:
                st(pl.ds(pl.multiple_of(t0 - halo, CONV_HALO), halo), dx[:halo], True)
        return tuple(a + b for a, b in zip(dconsts, grads[len(xs):]))

    dconsts = one(0, True, tuple(jnp.zeros_like(c) for c in consts))
    return lax.fori_loop(1, n_rows // tile, lambda i, dc: one(pl.multiple_of(i * tile, tile), False, dc), dconsts)


def _params(sem):
    return pltpu.CompilerParams(dimension_semantics=sem, vmem_limit_bytes=VMEM_LIMIT)


def _matmul(a, b, *, name, ta=False, tb=False, tm=512, tn=512, tk=1024, add=None, add_scale=1.0,
            a_halves=False, b_parts=0, out_parts=0, ride=None, out_dtype=F32):
    assert not (a_halves and ta)
    a_shape = (a.shape[1], 2 * a.shape[2]) if a_halves else a.shape
    b_shape = (b.shape[1], b_parts * b.shape[2]) if b_parts else b.shape
    (k_dim, m) = a_shape if ta else a_shape[::-1]
    (n, k2) = b_shape if tb else b_shape[::-1]
    assert k_dim == k2, (a.shape, b.shape)
    tm, tn, tk = min(tm, m), min(tn, n), min(tk, k_dim)
    assert m % tm == 0 and n % tn == 0 and k_dim % tk == 0, (name, m, n, k_dim, tm, tn, tk)
    nk = k_dim // tk
    ca, cb = (0 if ta else 1), (1 if tb else 0)

    def body(*refs):
        if add is None:
            a_ref, b_ref, o_ref, acc = refs
        else:
            a_ref, b_ref, c_ref, o_ref, acc = refs
        kk = pl.program_id(2)

        @pl.when(kk == 0)
        def _():
            acc[...] = jnp.zeros_like(acc)

        acc[...] += _dot(a_ref[...], b_ref[...], ca, cb)

        @pl.when(kk == nk - 1)
        def _():
            r = acc[...]
            if add is not None:
                r = r + add_scale * c_ref[...]
            o_ref[...] = r.astype(out_dtype)

    def per_part(total, parts, tile):
        per = total // parts // tile
        assert per * tile * parts == total, (name, total, parts, tile)
        return per

    spec = pl.BlockSpec
    a_spec = spec((tk, tm), lambda i, j, k: (k, i)) if ta else spec((tm, tk), lambda i, j, k: (i, k))
    b_spec = spec((tn, tk), lambda i, j, k: (j, k)) if tb else spec((tk, tn), lambda i, j, k: (k, j))
    if a_halves:
        kh = per_part(k_dim, 2, tk)
        a_spec = spec((None, tm, tk), lambda i, j, k: (k // kh, i, k % kh))
    if b_parts and tb:
        kp = per_part(k_dim, b_parts, tk)
        b_spec = spec((None, tn, tk), lambda i, j, k: (k // kp, j, k % kp))
    elif b_parts:
        np_ = per_part(n, b_parts, tn)
        b_spec = spec((None, tk, tn), lambda i, j, k: (j // np_, k, j % np_))
    out_spec, out_shape = spec((tm, tn), lambda i, j, k: (i, j)), (m, n)
    if out_parts:
        op = per_part(n, out_parts, tn)
        out_spec, out_shape = spec((None, tm, tn), lambda i, j, k: (j // op, i, j % op)), (out_parts, m, n // out_parts)
    in_specs = [a_spec, b_spec]
    args = [a, b]
    if add is not None:
        in_specs.append(pl.BlockSpec((tm, tn), lambda i, j, k: (i, j)))
        args.append(add)
    grid = (m // tm, n // tn, nk)
    r_in, r_out, r_shapes, r_sems, r_args = _ride_specs(ride)
    outs = pl.pallas_call(
        _riding(body, len(args), 1, 1, ride, grid), name=name, grid=grid,
        in_specs=in_specs + r_in, out_specs=[out_spec] + r_out,
        out_shape=[jax.ShapeDtypeStruct(out_shape, out_dtype)] + r_shapes,
        scratch_shapes=[pltpu.VMEM((tm, tn), F32)] + r_sems,
        compiler_params=_params(("parallel", "parallel", "arbitrary") if ride is None else ("arbitrary",) * 3),
    )(*args, *r_args)
    return outs[0] if ride is None else (outs[0], list(outs[1:]))


def _riding(core, n_in, n_out, n_scratch, ride, grid):
    if ride is None:
        return core
    copies, nr = ride[1], len(ride[0])
    steps = int(np.prod(grid))
    pass_step = min(max(int(steps * ride[4]), 1), steps - 2)
    assert steps >= 3, grid

    def body(*refs):
        cuts = np.cumsum([0, n_in, nr, n_out, nr, n_scratch])
        ins, rin, outs, rout, scratch = (refs[a:b] for a, b in zip(cuts[:-1], cuts[1:]))
        sems = refs[cuts[-1]:]
        step = 0
        for axis, size in enumerate(grid):
            step = step * size + pl.program_id(axis)

        @pl.when(step == 0)
        def _():
            _exchange_start(copies(rin, rout, sems))

        @pl.when(step == pass_step)
        def _():
            _exchange_pass_on(copies(rin, rout, sems))

        core(*ins, *outs, *scratch)

        @pl.when(step == steps - 1)
        def _():
            _exchange_finish(copies(rin, rout, sems))

    return body


def _ride_specs(ride):
    if ride is None:
        return [], [], [], [], []
    arrays, _, sems, shapes, _ = ride
    return [HBM_REF] * len(arrays), [HBM_REF] * len(arrays), list(shapes), sems(len(arrays)), list(arrays)


def _gdn_fwd(proj, conv_w, sc, norm_g, bl, s, ride=None):
    nc = s // CHUNK

    def core(ph_ref, ab_ref, cwq_ref, cwk_ref, cwv_ref, sc_ref, ng_ref, cat_ref, o_ref, st_ref, q_s, k_s, v_s, gb_s):
        def into(ref):
            def store(rows, value):
                ref[rows, :] = value.astype(ref.dtype)
            return store

        for fn, col, cw_ref, val_s in [(_gdn_q, 0, cwq_ref, q_s), (_gdn_k, 128, cwk_ref, k_s), (_gdn_v, 256, cwv_ref, v_s)]:
            _rows_apply(fn, [lambda r, col=col: ph_ref[r, col:col + 128]], [cw_ref[...]], into(val_s), s, LANES, CONV_HALO)
        _rows_apply(functools.partial(_gdn_gate, head=pl.program_id(1)), [lambda r: ab_ref[r, :]], [sc_ref[...]], into(gb_s), s, LANES, 0)

        group = math.gcd(nc, GDN_FWD_GROUP)

        def rows_of(n):
            return slice(n * CHUNK, (n + 1) * CHUNK)

        def levels_of(g):
            rows = [rows_of(g * group + j) for j in range(group)]
            return _gdn_local_levels([q_s[r, :] for r in rows], [k_s[r, :] for r in rows], [v_s[r, :] for r in rows],
                                     [gb_s[r, :] for r in rows], _solve_levels)

        local, state = _run(levels_of(0)), jnp.zeros((GDN_DK, GDN_DV), F32)
        for g in range(nc // group):
            ahead = levels_of(g + 1) if g + 1 < nc // group else None
            following = None
            for j, loc in enumerate(local):
                n = g * group + j
                st_ref[n] = state
                o_ref[rows_of(n), :], state = _gdn_state_step(*loc, state)
                if ahead is not None and following is None:
                    try:
                        next(ahead)
                    except StopIteration as stop:
                        following = stop.value
            if ahead is not None and following is None:
                following = _run(ahead)
            local = following
        _rows_apply(_gdn_post, [lambda r: o_ref[r, :], lambda r: ph_ref[r, 384:512]], [ng_ref[...]], into(cat_ref), s, LANES, 0)

    t = bl * s
    r_in, r_out, r_shapes, r_sems, r_args = _ride_specs(ride)
    outs = pl.pallas_call(
        _riding(core, 7, 7, 0, ride, (bl, GDN_HEADS)), name="gdn_fwd", grid=(bl, GDN_HEADS),
        in_specs=[
            pl.BlockSpec((s, 512), lambda b, h: (b, h)),
            pl.BlockSpec((s, 128), lambda b, h: (b, P_MLA // 128)),
            pl.BlockSpec((GDN_CONV, 128), lambda b, h: (0, h)),
            pl.BlockSpec((GDN_CONV, 128), lambda b, h: (0, GDN_HEADS + h)),
            pl.BlockSpec((GDN_CONV, 128), lambda b, h: (0, 2 * GDN_HEADS + h)),
            pl.BlockSpec((8, 128), lambda b, h: (0, 0)),
            pl.BlockSpec((1, 128), lambda b, h: (0, 0)),
        ] + r_in,
        out_specs=[
            pl.BlockSpec((s, 128), lambda b, h: (b, h)),
            pl.BlockSpec((s, 128), lambda b, h: (b, h)),
            pl.BlockSpec((None, None, nc, GDN_DK, GDN_DV), lambda b, h: (b, h, 0, 0, 0)),
        ] + [pl.BlockSpec((s, 128), lambda b, h: (b, h))] * 4 + r_out,
        out_shape=[
            jax.ShapeDtypeStruct((t, 2 * GDN_VW), BF16),
            jax.ShapeDtypeStruct((t, GDN_VW), F32),
            jax.ShapeDtypeStruct((bl, GDN_HEADS, nc, GDN_DK, GDN_DV), F32),
        ] + [jax.ShapeDtypeStruct((t, GDN_VW), F32)] * 4 + r_shapes,
        scratch_shapes=r_sems,
        compiler_params=_params(("arbitrary", "arbitrary")),
    )(proj, proj, conv_w, conv_w, conv_w, sc, norm_g, *r_args)
    return outs[0], outs[1], outs[2], tuple(outs[3:7]), list(outs[7:])


def _gdn_bwd(proj, conv_w, sc, norm_g, o_raw, states, qkvg, dcat, bl, s, ride=None):
    nc = s // CHUNK

    def core(ph_ref, ab_ref, cwq_ref, cwk_ref, cwv_ref, sc_ref, ng_ref, o_ref, st_ref, dc_ref, q_in, k_in, v_in, gb_in,
             dph_ref, dab_ref, dcwq_ref, dcwk_ref, dcwv_ref, dsc_ref, dng_ref, q_s, k_s, v_s, gb_s, do_s):
        head = pl.program_id(1)
        gate = functools.partial(_gdn_gate, head=head)
        paths = [(_gdn_q, 0, cwq_ref, q_s, dcwq_ref), (_gdn_k, 128, cwk_ref, k_s, dcwk_ref), (_gdn_v, 256, cwv_ref, v_s, dcwv_ref)]
        def into(ref, cols=slice(None)):
            def store(rows, value, add=False):
                if add:
                    ref[rows, cols] += value.astype(ref.dtype)
                else:
                    ref[rows, cols] = value.astype(ref.dtype)
            return store

        (dng,) = _rows_vjp(_gdn_post, [lambda r: o_ref[r, :], lambda r: ph_ref[r, 384:512]], [ng_ref[...]],
                           lambda r: dc_ref[r, :], [into(do_s), into(dph_ref, slice(384, 512))], s, LANES, 0)
        dng_ref[...] = jnp.broadcast_to(dng, dng_ref.shape)

        group = math.gcd(nc, GDN_BWD_GROUP)

        def chunks(i, dstate):
            ns = [nc - 1 - (i * group + j) for j in range(group)]
            rows = [pl.ds(pl.multiple_of(n * CHUNK, CHUNK), CHUNK) for n in ns]
            local, local_vjp = jax.vjp(_gdn_local_group, [q_in[r, :] for r in rows], [k_in[r, :] for r in rows],
                                       [v_in[r, :] for r in rows], [gb_in[r, :] for r in rows])
            d_os = [do_s[r, :] for r in rows]
            dlocal = []
            for n, loc, d_o in zip(ns, local, d_os):
                _, step_vjp = jax.vjp(_gdn_state_step, *loc, st_ref[n])
                *dloc, dstate = step_vjp((d_o, dstate))
                dlocal.append(tuple(dloc))
            dqs, dks, dvs, dgbs = local_vjp(dlocal)
            for r, dq, dk, dv, dgb in zip(rows, dqs, dks, dvs, dgbs):
                q_s[r, :], k_s[r, :], v_s[r, :], gb_s[r, :] = dq, dk, dv, dgb
            return dstate

        lax.fori_loop(0, nc // group, chunks, jnp.zeros((GDN_DK, GDN_DV), F32))
        for fn, col, cw_ref, val_s, dcw_ref in paths:
            (dcw_ref[...],) = _rows_vjp(fn, [lambda r, col=col: ph_ref[r, col:col + 128]], [cw_ref[...]],
                                        lambda r, val_s=val_s: val_s[r, :], [into(val_s)], s, LANES, CONV_HALO)
            dph_ref[:, col:col + 128] = val_s[...].astype(BF16)

        @pl.when(head == 0)
        def _():
            dab_ref[...] = jnp.zeros_like(dab_ref)

        def add_dab(rows, value, add=False):
            dab_ref[rows, :] += value

        (dsc_ref[...],) = _rows_vjp(gate, [lambda r: ab_ref[r, :]], [sc_ref[...]], lambda r: gb_s[r, :], [add_dab], s, LANES, 0)

    t = bl * s
    cw_out = pl.BlockSpec((None, GDN_CONV, 128), lambda b, h: (b, 0, h))
    part = pl.BlockSpec((None, None, 8, 128), lambda b, h: (b, h, 0, 0))
    r_in, r_out, r_shapes, r_sems, r_args = _ride_specs(ride)
    outs = pl.pallas_call(
        _riding(core, 14, 7, 5, ride, (bl, GDN_HEADS)), name="gdn_bwd", grid=(bl, GDN_HEADS),
        in_specs=[
            pl.BlockSpec((s, 512), lambda b, h: (b, h)),
            pl.BlockSpec((s, 128), lambda b, h: (b, P_MLA // 128)),
            pl.BlockSpec((GDN_CONV, 128), lambda b, h: (0, h)),
            pl.BlockSpec((GDN_CONV, 128), lambda b, h: (0, GDN_HEADS + h)),
            pl.BlockSpec((GDN_CONV, 128), lambda b, h: (0, 2 * GDN_HEADS + h)),
            pl.BlockSpec((8, 128), lambda b, h: (0, 0)),
            pl.BlockSpec((1, 128), lambda b, h: (0, 0)),
            pl.BlockSpec((s, 128), lambda b, h: (b, h)),
            pl.BlockSpec((None, None, nc, GDN_DK, GDN_DV), lambda b, h: (b, h, 0, 0, 0)),
        ] + [pl.BlockSpec((s, 128), lambda b, h: (b, h))] * 5 + r_in,
        out_specs=[
            pl.BlockSpec((s, 512), lambda b, h: (b, h)),
            pl.BlockSpec((s, 128), lambda b, h: (b, 0)),
            cw_out, cw_out, cw_out, part, part,
        ] + r_out,
        out_shape=[
            jax.ShapeDtypeStruct((t, P_WIDTH), BF16),
            jax.ShapeDtypeStruct((t, 128), F32),
            jax.ShapeDtypeStruct((bl, GDN_CONV, 512), F32),
            jax.ShapeDtypeStruct((bl, GDN_CONV, 512), F32),
            jax.ShapeDtypeStruct((bl, GDN_CONV, 512), F32),
            jax.ShapeDtypeStruct((bl, GDN_HEADS, 8, 128), F32),
            jax.ShapeDtypeStruct((bl, GDN_HEADS, 8, 128), F32),
        ] + r_shapes,
        scratch_shapes=[pltpu.VMEM((s, 128), F32)] * 5 + r_sems,
        compiler_params=_params(("arbitrary", "arbitrary")),
    )(proj, proj, conv_w, conv_w, conv_w, sc, norm_g, o_raw, states, dcat, *qkvg, *r_args)
    return tuple(outs[:7]) + (list(outs[7:]),)


def _mla_prep_fwd(proj, qg, kvg, wq, wkv, cos, sin, s, tm):
    t = proj.shape[0]
    tm = min(tm, s)
    nps = s // tm
    const = lambda shape: pl.BlockSpec(shape, lambda i: (0, 0))

    def body(pm_ref, qg_ref, kvg_ref, wq_ref, wkv_ref, cos_ref, sin_ref, qf_ref, kvf_ref, kr_ref):
        qf, kvf, kr = _mla_prep(pm_ref[...], qg_ref[...], kvg_ref[...], wq_ref[...], wkv_ref[...], cos_ref[...], sin_ref[...])
        qf_ref[...], kvf_ref[...], kr_ref[...] = qf.astype(BF16), kvf.astype(BF16), kr.astype(BF16)

    return pl.pallas_call(
        body, name="mla_prep_fwd", grid=(t // tm,),
        in_specs=[
            pl.BlockSpec((tm, 1024), lambda i: (i, P_MLA // 1024)),
            const((1, MLA_Q_LORA)), const((1, MLA_KV_LORA)), const(wq.shape), const(wkv.shape),
            pl.BlockSpec((tm, 128), lambda i: (i % nps, 0)), pl.BlockSpec((tm, 128), lambda i: (i % nps, 0)),
        ],
        out_specs=[pl.BlockSpec((tm, 1024), lambda i: (i, 0)), pl.BlockSpec((tm, 1024), lambda i: (i, 0)),
                   pl.BlockSpec((tm, 128), lambda i: (i, 0))],
        out_shape=[jax.ShapeDtypeStruct((t, 1024), BF16), jax.ShapeDtypeStruct((t, 1024), BF16),
                   jax.ShapeDtypeStruct((t, 128), BF16)],
        compiler_params=_params(("parallel",)),
    )(proj, qg, kvg, wq, wkv, cos, sin)


def _mla_prep_bwd(proj, qg, kvg, wq, wkv, cos, sin, dqf, dkvf, dkr, dab, dproj, s, tm):
    t = proj.shape[0]
    tm = min(tm, s)
    nps = s // tm
    const = lambda shape: pl.BlockSpec(shape, lambda i: (0, 0))

    def body(pm_ref, qg_ref, kvg_ref, wq_ref, wkv_ref, cos_ref, sin_ref, dqf_ref, dkvf_ref, dkr_ref, dab_ref, dp_in,
             dp_ref, dqg_ref, dkvg_ref, dwq_ref, dwkv_ref):
        del dp_in
        fn = lambda pm, qg_, kvg_, wq_, wkv_: _mla_prep(pm, qg_, kvg_, wq_, wkv_, cos_ref[...], sin_ref[...])
        _, vjp = jax.vjp(fn, pm_ref[...], qg_ref[...], kvg_ref[...], wq_ref[...].astype(F32), wkv_ref[...].astype(F32))
        dpm, dqg, dkvg, dwq, dwkv = vjp((dqf_ref[...], dkvf_ref[...], dkr_ref[...]))
        dp_ref[...] = jnp.concatenate([dab_ref[...], dpm[:, 128:]], axis=1).astype(BF16)

        @pl.when(pl.program_id(0) == 0)
        def _():
            dqg_ref[...] = jnp.zeros_like(dqg_ref)
            dkvg_ref[...] = jnp.zeros_like(dkvg_ref)
            dwq_ref[...] = jnp.zeros_like(dwq_ref)
            dwkv_ref[...] = jnp.zeros_like(dwkv_ref)

        dqg_ref[...] += dqg
        dkvg_ref[...] += dkvg
        dwq_ref[...] += dwq
        dwkv_ref[...] += dwkv

    rows = lambda w: pl.BlockSpec((tm, w), lambda i: (i, 0))
    return pl.pallas_call(
        body, name="mla_prep_bwd", grid=(t // tm,),
        in_specs=[
            pl.BlockSpec((tm, 1024), lambda i: (i, P_MLA // 1024)),
            const((1, MLA_Q_LORA)), const((1, MLA_KV_LORA)), const(wq.shape), const(wkv.shape),
            pl.BlockSpec((tm, 128), lambda i: (i % nps, 0)), pl.BlockSpec((tm, 128), lambda i: (i % nps, 0)),
            rows(1024), rows(1024), rows(128), rows(128),
            pl.BlockSpec(memory_space=pl.ANY),
        ],
        out_specs=[pl.BlockSpec((tm, 1024), lambda i: (i, P_MLA // 1024)),
                   const((1, MLA_Q_LORA)), const((1, MLA_KV_LORA)), const(wq.shape), const(wkv.shape)],
        out_shape=[jax.ShapeDtypeStruct(dproj.shape, dproj.dtype),
                   jax.ShapeDtypeStruct((1, MLA_Q_LORA), F32), jax.ShapeDtypeStruct((1, MLA_KV_LORA), F32),
                   jax.ShapeDtypeStruct(wq.shape, F32), jax.ShapeDtypeStruct(wkv.shape, F32)],
        input_output_aliases={11: 0},
        compiler_params=_params(("arbitrary",)),
    )(proj, qg, kvg, wq, wkv, cos, sin, dqf, dkvf, dkr, dab, dproj)


def _attn_fwd(qf, kvf, kr, cat, bl, s, tq):
    tq = min(tq, s)
    nq = s // tq

    def body(q_ref, kv_ref, kr_ref, cat_in, o_ref, o32_ref, lse_ref):
        del cat_in

        def scores_of(i):
            keys = slice(0, (i + 1) * tq)
            return _scores(q_ref[i * tq:(i + 1) * tq, :], kv_ref[keys, 0:128], kr_ref[keys, :], i * tq, 0)

        ready = scores_of(0)
        for i in range(nq):
            scores = ready
            if i + 1 < nq:
                ready = scores_of(i + 1)
            rows = slice(i * tq, (i + 1) * tq)
            o, lse = _softmax_times(scores, kv_ref[0:(i + 1) * tq, 128:256])
            o_ref[rows, :] = o.astype(o_ref.dtype)
            o32_ref[rows, :] = o
            lse_ref[rows, :] = jnp.broadcast_to(lse, o.shape)

    t = bl * s
    head_cols = pl.BlockSpec((s, 128), lambda b, h: (b, h))
    return pl.pallas_call(
        body, name="attn_fwd", grid=(bl, MLA_HEADS),
        in_specs=[
            pl.BlockSpec((s, 256), lambda b, h: (b, h)),
            pl.BlockSpec((s, 256), lambda b, h: (b, h)),
            pl.BlockSpec((s, 128), lambda b, h: (b, 0)),
            pl.BlockSpec(memory_space=pl.ANY),
        ],
        out_specs=[pl.BlockSpec((s, 128), lambda b, h: (b, GDN_HEADS + h)), head_cols, head_cols],
        out_shape=[jax.ShapeDtypeStruct(cat.shape, cat.dtype)] + [jax.ShapeDtypeStruct((t, MLA_HEADS * MLA_V), F32)] * 2,
        input_output_aliases={3: 0},
        compiler_params=_params(("parallel", "parallel")),
    )(qf, kvf, kr, cat)


def _attn_bwd(qf, kvf, kr, dcat, o32, lse, bl, s, tq):
    tq = min(tq, s)
    nq = s // tq

    def body(q_ref, kv_ref, kr_ref, do_ref, o_ref, lse_ref, dq_ref, dkv_ref, dkr_ref):
        dkv_ref[...] = jnp.zeros_like(dkv_ref)

        @pl.when(pl.program_id(1) == 0)
        def _():
            dkr_ref[...] = jnp.zeros_like(dkr_ref)

        def block(i):
            rows = slice(i * tq, (i + 1) * tq)
            return q_ref[rows, :], do_ref[rows, :]

        def first_products(i, j):
            (q, d_o), keys = block(i), slice(j * tq, (j + 1) * tq)
            return _scores(q, kv_ref[keys, 0:128], kr_ref[keys, :], i * tq, j * tq), mm_nt(d_o, kv_ref[keys, 128:256])

        tiles = [(i, j) for i in range(nq) for j in range(i + 1)]
        ready = first_products(*tiles[0])
        for t, (i, j) in enumerate(tiles):
            scores, dp = ready
            if t + 1 < len(tiles):
                ready = first_products(*tiles[t + 1])
            rows, keys = slice(i * tq, (i + 1) * tq), slice(j * tq, (j + 1) * tq)
            q, d_o = block(i)
            if j == 0:
                delta = jnp.sum(d_o * o_ref[rows, :], axis=-1, keepdims=True)
                lse_i = jnp.max(lse_ref[rows, :], axis=-1, keepdims=True)
                dqn, dqr = jnp.zeros((tq, MLA_NOPE), F32), jnp.zeros((tq, 128), F32)
            p = jnp.exp(scores - lse_i)
            ds = p * (dp - delta) * ATT_SCALE
            dkv_ref[keys, 128:256] += mm_tn(p, d_o)
            dkv_ref[keys, 0:128] += mm_tn(ds, q[:, :128])
            dkr_ref[keys, :] += mm_tn(ds, q[:, 128:])
            dqn = dqn + mm(ds, kv_ref[keys, 0:128])
            dqr = dqr + mm(ds, kr_ref[keys, :])
            if j == i:
                dq_ref[rows, 0:128], dq_ref[rows, 128:256] = dqn, dqr

    t = bl * s
    head_cols = pl.BlockSpec((s, 128), lambda b, h: (b, h))
    return pl.pallas_call(
        body, name="attn_bwd", grid=(bl, MLA_HEADS),
        in_specs=[
            pl.BlockSpec((s, 256), lambda b, h: (b, h)),
            pl.BlockSpec((s, 256), lambda b, h: (b, h)),
            pl.BlockSpec((s, 128), lambda b, h: (b, 0)),
            pl.BlockSpec((s, 128), lambda b, h: (b, GDN_HEADS + h)),
            head_cols, head_cols,
        ],
        out_specs=[
            pl.BlockSpec((s, 256), lambda b, h: (b, h)),
            pl.BlockSpec((s, 256), lambda b, h: (b, h)),
            pl.BlockSpec((s, 128), lambda b, h: (b, 0)),
        ],
        out_shape=[jax.ShapeDtypeStruct((t, 1024), F32), jax.ShapeDtypeStruct((t, 1024), F32),
                   jax.ShapeDtypeStruct((t, 128), F32)],
        compiler_params=_params(("parallel", "arbitrary")),
    )(qf, kvf, kr, dcat, o32, lse)


def _mix_ln1_fwd(x, cat, w_o, g, b, tm):
    t = x.shape[0]
    tm = min(tm, t)

    def body(x_ref, cat_ref, w_ref, g_ref, b_ref, r_ref, h_ref, hb_ref, xb_ref):
        r = ALPHA * x_ref[...] + _dot(cat_ref[...], w_ref[...], 1, 0)
        r_ref[...] = r
        h = _layernorm(r, g_ref[...], b_ref[...])
        h_ref[...] = h
        hb_ref[...] = h.astype(BF16)
        xb_ref[...] = x_ref[...].astype(BF16)

    rows = pl.BlockSpec((tm, D_MODEL), lambda i: (i, 0))
    vec = pl.BlockSpec((1, D_MODEL), lambda i: (0, 0))
    return pl.pallas_call(
        body, name="mix_ln1_fwd", grid=(t // tm,),
        in_specs=[rows, pl.BlockSpec((tm, cat.shape[1]), lambda i: (i, 0)), pl.BlockSpec(w_o.shape, lambda i: (0, 0)), vec, vec],
        out_specs=[rows] * 4,
        out_shape=[jax.ShapeDtypeStruct(x.shape, F32)] * 2 + [jax.ShapeDtypeStruct(x.shape, BF16)] * 2,
        compiler_params=_params(("parallel",)),
    )(x, cat, w_o, g, b)


def _ln1_bwd(r1, dr2, da, dgpre, w_gate, w_o, g, b, tm):
    t = r1.shape[0]
    tm = min(tm, t)

    def body(r_ref, d2_ref, da_ref, dgp_ref, wg_ref, wo_ref, g_ref, b_ref, dr_ref, drb_ref, dcat_ref, dg_ref, dbias_ref):
        dh = ALPHA * d2_ref[...] + da_ref[...] + _dot(dgp_ref[...], wg_ref[...], 1, 1)
        _, vjp = jax.vjp(_layernorm, r_ref[...], g_ref[...], b_ref[...])
        dr, dg, dbias = vjp(dh)
        dr_ref[...] = dr
        drb_ref[...] = dr.astype(BF16)
        dcat_ref[...] = _dot(dr, wo_ref[...], 1, 1)

        @pl.when(pl.program_id(0) == 0)
        def _():
            dg_ref[...] = jnp.zeros_like(dg_ref)
            dbias_ref[...] = jnp.zeros_like(dbias_ref)

        dg_ref[...] += dg
        dbias_ref[...] += dbias

    rows = pl.BlockSpec((tm, D_MODEL), lambda i: (i, 0))
    vec = pl.BlockSpec((1, D_MODEL), lambda i: (0, 0))
    return pl.pallas_call(
        body, name="ln1_bwd", grid=(t // tm,),
        in_specs=[rows] * 4 + [pl.BlockSpec(w_gate.shape, lambda i: (0, 0)), pl.BlockSpec(w_o.shape, lambda i: (0, 0)), vec, vec],
        out_specs=[rows, rows, pl.BlockSpec((tm, w_o.shape[0]), lambda i: (i, 0)), vec, vec],
        out_shape=[jax.ShapeDtypeStruct(r1.shape, F32), jax.ShapeDtypeStruct(r1.shape, BF16),
                   jax.ShapeDtypeStruct((t, w_o.shape[0]), F32)] + [jax.ShapeDtypeStruct((1, D_MODEL), F32)] * 2,
        compiler_params=_params(("arbitrary",)),
    )(r1, dr2, da, dgpre, w_gate, w_o, g, b)


def _ffn_act_fwd(u, conv_w, conv_b, bl, s, cb):
    nj = D_FF // cb

    def body(ug_ref, uu_ref, wg_ref, wu_ref, bg_ref, bu_ref, act_ref):
        def store(rows, act):
            act_ref[rows, :] = act.astype(BF16)

        _rows_apply(_ffn_act, [lambda r: ug_ref[r, :], lambda r: uu_ref[r, :]],
                    [wg_ref[...], wu_ref[...], bg_ref[...], bu_ref[...]], store, s, cb, CONV_HALO)

    return pl.pallas_call(
        body, name="ffn_act_fwd", grid=(bl, nj),
        in_specs=[
            pl.BlockSpec((s, cb), lambda b, j: (b, j)), pl.BlockSpec((s, cb), lambda b, j: (b, nj + j)),
            pl.BlockSpec((FFN_CONV, cb), lambda b, j: (0, j)), pl.BlockSpec((FFN_CONV, cb), lambda b, j: (0, nj + j)),
            pl.BlockSpec((1, cb), lambda b, j: (0, j)), pl.BlockSpec((1, cb), lambda b, j: (0, nj + j)),
        ],
        out_specs=pl.BlockSpec((s, cb), lambda b, j: (b, j)),
        out_shape=jax.ShapeDtypeStruct((bl * s, D_FF), BF16),
        compiler_params=_params(("parallel", "parallel")),
    )(u, u, conv_w, conv_w, conv_b, conv_b)


def _ffn_act_bwd(u, conv_w, conv_b, dact, bl, s, cb):
    nj = D_FF // cb

    def body(ug_ref, uu_ref, wg_ref, wu_ref, bg_ref, bu_ref, da_ref, du_ref, dwg_ref, dwu_ref, dbg_ref, dbu_ref, acc):
        def store_into(half):
            def store(rows, value, add):
                if add:
                    acc[half, rows, :] += value
                else:
                    acc[half, rows, :] = value
            return store

        dwg_ref[...], dwu_ref[...], dbg_ref[...], dbu_ref[...] = _rows_vjp(
            _ffn_act, [lambda r: ug_ref[r, :], lambda r: uu_ref[r, :]], [wg_ref[...], wu_ref[...], bg_ref[...], bu_ref[...]],
            lambda r: da_ref[r, :], [store_into(0), store_into(1)], s, cb, CONV_HALO)
        du_ref[...] = acc[...].astype(BF16)

    t = bl * s
    blk = pl.BlockSpec((s, cb), lambda b, j: (b, j))
    wpart = pl.BlockSpec((None, FFN_CONV, cb), lambda b, j: (b, 0, j))
    bpart = pl.BlockSpec((None, 1, cb), lambda b, j: (b, 0, j))
    return pl.pallas_call(
        body, name="ffn_act_bwd", grid=(bl, nj),
        in_specs=[
            blk, pl.BlockSpec((s, cb), lambda b, j: (b, nj + j)),
            pl.BlockSpec((FFN_CONV, cb), lambda b, j: (0, j)), pl.BlockSpec((FFN_CONV, cb), lambda b, j: (0, nj + j)),
            pl.BlockSpec((1, cb), lambda b, j: (0, j)), pl.BlockSpec((1, cb), lambda b, j: (0, nj + j)),
            blk,
        ],
        out_specs=[pl.BlockSpec((2, s, cb), lambda b, j: (0, b, j)), wpart, wpart, bpart, bpart],
        out_shape=[jax.ShapeDtypeStruct((2, t, D_FF), BF16)] + [jax.ShapeDtypeStruct((bl, FFN_CONV, D_FF), F32)] * 2
        + [jax.ShapeDtypeStruct((bl, 1, D_FF), F32)] * 2,
        scratch_shapes=[pltpu.VMEM((2, s, cb), F32)],
        compiler_params=_params(("parallel", "parallel")),
    )(u, u, conv_w, conv_w, conv_b, conv_b, dact)


def _head(h1, h1b, act, pt, w_down, w_gate, w_proj, bgate, g2, b2, target, tm):
    t = h1.shape[0]
    tm = min(tm, t)

    def body(h1_ref, h1b_ref, act_ref, pt_ref, wd_ref, wg_ref, wp_ref, bg_ref, g2_ref, b2_ref, tg_ref,
             dr_ref, drb_ref, dgp_ref, dpp_ref, loss_ref, dbg_ref, dg2_ref, db2_ref):
        fn = functools.partial(_head_loss, target=tg_ref[...])
        ffn = _dot(act_ref[...], wd_ref[...], 1, 0)
        gpre, pp = _dot(h1b_ref[...], wg_ref[...], 1, 0), _dot(pt_ref[...], wp_ref[...], 1, 0)
        loss, vjp = jax.vjp(fn, h1_ref[...], ffn, gpre, pp, bg_ref[...], g2_ref[...], b2_ref[...])
        _, dffn, dgp, dpp, dbg, dg2, db2 = vjp(jnp.ones((1, 1), F32))
        dr_ref[...] = dffn
        drb_ref[...], dgp_ref[...], dpp_ref[...] = dffn.astype(BF16), dgp.astype(BF16), dpp.astype(BF16)

        @pl.when(pl.program_id(0) == 0)
        def _():
            loss_ref[...] = jnp.zeros_like(loss_ref)
            dbg_ref[...] = jnp.zeros_like(dbg_ref)
            dg2_ref[...] = jnp.zeros_like(dg2_ref)
            db2_ref[...] = jnp.zeros_like(db2_ref)

        loss_ref[...] += jnp.broadcast_to(loss, loss_ref.shape)
        dbg_ref[...] += dbg
        dg2_ref[...] += dg2
        db2_ref[...] += db2

    rows = pl.BlockSpec((tm, D_MODEL), lambda i: (i, 0))
    vec = pl.BlockSpec((1, D_MODEL), lambda i: (0, 0))
    return pl.pallas_call(
        body, name="head", grid=(t // tm,),
        in_specs=[rows, rows, pl.BlockSpec((tm, act.shape[1]), lambda i: (i, 0)), pl.BlockSpec((tm, pt.shape[1]), lambda i: (i, 0))]
        + [pl.BlockSpec(w.shape, lambda i: (0, 0)) for w in (w_down, w_gate, w_proj)] + [vec] * 3 + [rows],
        out_specs=[rows] * 4 + [pl.BlockSpec((8, 128), lambda i: (0, 0))] + [vec] * 3,
        out_shape=[jax.ShapeDtypeStruct(h1.shape, F32)] + [jax.ShapeDtypeStruct(h1.shape, BF16)] * 3
        + [jax.ShapeDtypeStruct((8, 128), F32)]
        + [jax.ShapeDtypeStruct((1, D_MODEL), F32)] * 3,
        compiler_params=_params(("arbitrary",)),
    )(h1, h1b, act, pt, w_down, w_gate, w_proj, bgate, g2, b2, target)


def _adam_update(g, w_ref, m_ref, v_ref, g_ref, d_ref, nm_ref, nv_ref):
    m2 = ADAM_B1 * m_ref[...] + (1.0 - ADAM_B1) * g
    v2 = ADAM_B2 * v_ref[...] + (1.0 - ADAM_B2) * jnp.square(g)
    m_hat = m2 / (1.0 - ADAM_B1 ** ADAM_STEP)
    v_hat = v2 / (1.0 - ADAM_B2 ** ADAM_STEP)
    g_ref[...] = g
    d_ref[...] = -ADAM_LR * (m_hat / (jnp.sqrt(v_hat) + ADAM_EPS) + ADAM_WD * w_ref[...])
    nm_ref[...] = m2
    nv_ref[...] = v2


def _row_tile(rows, cols, limit_bytes=1024 * 1024):
    best = None
    for t in range(HALF_ROWS_QUANTUM, rows + 1, HALF_ROWS_QUANTUM):
        if rows % t == 0 and t * cols * 4 <= limit_bytes:
            best = t
    return best or rows


def _adamw_reduced(recv, w, m, v, name):
    a, b = w.shape
    ta = _row_tile(a, b)

    def body(recv_ref, w_ref, m_ref, v_ref, g_ref, d_ref, nm_ref, nv_ref):
        c = lax.axis_index("c")
        for core in range(2):
            @pl.when(c == core)
            def _():
                got = [recv_ref[k].astype(F32) for k in range(N_DEV)]
                same = [got[7], got[0], got[1], got[2]]
                other = got[3:7]
                core0, core1 = (same, other) if core == 0 else (other, same)
                g = core0[0] + core1[0]
                for r in range(1, N_CHIPS):
                    g = (g + core0[r]) + core1[r]
                _adam_update(g, w_ref, m_ref, v_ref, g_ref, d_ref, nm_ref, nv_ref)

    blk = pl.BlockSpec((ta, b), lambda i: (i, 0))
    return pl.pallas_call(
        body, name=name, grid=(a // ta,),
        in_specs=[pl.BlockSpec((N_DEV, ta, b), lambda i: (0, i, 0)), blk, blk, blk], out_specs=[blk] * 4,
        out_shape=[jax.ShapeDtypeStruct(w.shape, F32)] * 4, compiler_params=_params(("parallel",)),
    )(recv, w, m, v)


def _adamw_small(g, w, m, v):
    def body(g_in, w_ref, m_ref, v_ref, g_ref, d_ref, nm_ref, nv_ref):
        _adam_update(g_in[...], w_ref, m_ref, v_ref, g_ref, d_ref, nm_ref, nv_ref)

    blk = pl.BlockSpec(w.shape, lambda i: (0, 0))
    return pl.pallas_call(
        body, name="adamw_small", grid=(1,), in_specs=[blk] * 4, out_specs=[blk] * 4,
        out_shape=[jax.ShapeDtypeStruct(w.shape, F32)] * 4, compiler_params=_params(("arbitrary",)),
    )(g, w, m, v)


def _remote(src, dst, send_sem, recv_sem, device):
    return pltpu.make_async_remote_copy(src_ref=src, dst_ref=dst, send_sem=send_sem, recv_sem=recv_sem,
                                        device_id=device, device_id_type=MESH)


def _place():
    x, y, c = lax.axis_index("x"), lax.axis_index("y"), lax.axis_index("c")
    return x, y, c, 2 * x + y, [(1 - x, y), (x, 1 - y), (1 - x, 1 - y)]


HBM_REF = pl.BlockSpec(memory_space=pl.ANY)
HALF_ROWS_QUANTUM = 16


def _gather_sems(n):
    return [pltpu.SemaphoreType.DMA((3 * n,))] * 4 + [pltpu.SemaphoreType.DMA((n,))]


def _gather_copies(ins, outs, sems):
    send_s, recv_s, fsend_s, frecv_s, local_s = sems
    x, y, c, me, chips = _place()
    local, sends, steps = [], [], []
    for i, (src, dst) in enumerate(zip(ins, outs)):
        local.append(pltpu.make_async_copy(src, dst.at[me], local_s.at[i]))
        half = src.shape[0] // 2
        split = src.shape[0] % (2 * HALF_ROWS_QUANTUM) == 0
        if split:
            mine = pl.ds(pl.multiple_of(c * half, HALF_ROWS_QUANTUM), half)
            theirs = pl.ds(pl.multiple_of((1 - c) * half, HALF_ROWS_QUANTUM), half)
        for r, (px, py) in enumerate(chips):
            k, peer = 3 * i + r, 2 * px + py
            if split:
                sends.append(_remote(src.at[mine], dst.at[me, mine], send_s.at[k], recv_s.at[k], (px, py, c)))
                landed = dst.at[peer, mine]
                steps.append((_remote(src.at[mine], landed, send_s.at[k], recv_s.at[k], (px, py, c)),
                              _remote(landed, landed, fsend_s.at[k], frecv_s.at[k], (x, y, 1 - c)),
                              _remote(dst.at[peer, theirs], dst.at[peer, theirs], fsend_s.at[k], frecv_s.at[k], (x, y, 1 - c))))
            else:
                sends.append(_remote(src, dst.at[me], send_s.at[k], recv_s.at[k], (px, py, c)))
                steps.append((_remote(src, dst.at[peer], send_s.at[k], recv_s.at[k], (px, py, c)), None, None))
    return local, sends, steps


def _scatter_sems(n):
    return [pltpu.SemaphoreType.DMA((4 * n,))] * 2 + [pltpu.SemaphoreType.DMA((3 * n,))] * 2 + [pltpu.SemaphoreType.DMA((n,))]


def _scatter_copies(ins, outs, sems):
    send_s, recv_s, fsend_s, frecv_s, local_s = sems
    x, y, c, me, chips = _place()
    local, sends, steps = [], [], []
    for i, (src, dst) in enumerate(zip(ins, outs)):
        local.append(pltpu.make_async_copy(src.at[me], dst.at[N_DEV - 1], local_s.at[i]))
        for r, (px, py) in enumerate(chips):
            k = 4 * i + r
            cp = _remote(src.at[2 * px + py], dst.at[r], send_s.at[k], recv_s.at[k], (px, py, c))
            fwd = _remote(dst.at[r], dst.at[4 + r], fsend_s.at[3 * i + r], frecv_s.at[3 * i + r], (x, y, 1 - c))
            sends.append(cp)
            steps.append((cp, fwd, fwd))
        k = 4 * i + 3
        cp = _remote(src.at[me], dst.at[3], send_s.at[k], recv_s.at[k], (x, y, 1 - c))
        sends.append(cp)
        steps.append((cp, None, None))
    return local, sends, steps


def _exchange_start(plan):
    local, sends, _ = plan
    for cp in local + sends:
        cp.start()


def _exchange_pass_on(plan):
    for arrival, pass_on, _ in plan[2]:
        arrival.wait_recv()
        if pass_on is not None:
            pass_on.start()


def _exchange_finish(plan):
    local, sends, steps = plan
    for _, pass_on, passed in steps:
        if pass_on is not None:
            passed.wait_recv()
    for cp in sends:
        cp.wait_send()
    for _, pass_on, _ in steps:
        if pass_on is not None:
            pass_on.wait_send()
    for cp in local:
        cp.wait()


def _exchange_call(arrays, copies, sems, out_shapes, name):
    n = len(arrays)

    def body(*refs):
        plan = copies(refs[:n], refs[n:2 * n], refs[2 * n:])
        _exchange_start(plan)
        _exchange_pass_on(plan)
        _exchange_finish(plan)

    return pl.pallas_call(
        body, name=name, in_specs=[HBM_REF] * n, out_specs=[HBM_REF] * n, out_shape=out_shapes,
        scratch_shapes=sems(n), compiler_params=pltpu.CompilerParams(has_side_effects=True),
    )(*arrays)


def _gather_call(shards, name):
    shapes = [jax.ShapeDtypeStruct((N_CHIPS,) + a.shape, a.dtype) for a in shards]
    return _exchange_call(shards, _gather_copies, _gather_sems, shapes, name)


def _all_reduce_small(a):
    def body(in_ref, out_ref, slots, send_sems, recv_sems):
        x, y, c = lax.axis_index("x"), lax.axis_index("y"), lax.axis_index("c")
        me = 4 * x + 2 * y + c
        slots[0] = in_ref[...]
        sends = []
        for r in range(1, N_DEV):
            peer = (x ^ (r >> 2), y ^ ((r >> 1) & 1), c ^ (r & 1))
            sends.append(pltpu.make_async_remote_copy(src_ref=in_ref, dst_ref=slots.at[r], send_sem=send_sems.at[r],
                                                      recv_sem=recv_sems.at[r], device_id=peer, device_id_type=MESH))
        for cp in sends:
            cp.start()
        for cp in sends:
            cp.wait_recv()
        acc = slots[me]
        for dev in range(1, N_DEV):
            acc = acc + slots[dev ^ me]
        out_ref[...] = acc
        for cp in sends:
            cp.wait_send()

    return pl.pallas_call(
        body, name="small_all_reduce",
        in_specs=[pl.BlockSpec(memory_space=pltpu.VMEM)], out_specs=pl.BlockSpec(memory_space=pltpu.VMEM),
        out_shape=jax.ShapeDtypeStruct(a.shape, a.dtype),
        scratch_shapes=[pltpu.VMEM((N_DEV,) + a.shape, a.dtype), pltpu.SemaphoreType.DMA((N_DEV,)),
                        pltpu.SemaphoreType.DMA((N_DEV,))],
        compiler_params=pltpu.CompilerParams(has_side_effects=True),
    )(a)


SHARDED = ["w_in", "mla_w_q_up", "mla_w_kv_up", "w_out", "ffn_w_up", "ffn_w_down", "ple_w_gate", "ple_w_proj",
           "gdn_conv_w", "ffn_conv_w"]
SHARD_AXIS = {"w_in": 1, "mla_w_q_up": 1, "mla_w_kv_up": 1, "w_out": 0, "ffn_w_up": 1, "ffn_w_down": 0,
              "ple_w_gate": 0, "ple_w_proj": 1, "gdn_conv_w": 1, "ffn_conv_w": 1}
SMALL = ["gdn_a_log", "gdn_dt_bias", "gdn_norm_g", "mla_q_norm_g", "mla_kv_norm_g", "ln1_g", "ln1_b", "ffn_conv_b",
         "ple_b_gate", "ln2_g", "ln2_b"]
WEIGHTS = ["w_in", "gdn_conv_w", "gdn_a_log", "gdn_dt_bias", "gdn_norm_g", "mla_q_norm_g", "mla_w_q_up", "mla_kv_norm_g",
           "mla_w_kv_up", "w_out", "ln1_g", "ln1_b", "ffn_w_up", "ffn_conv_w", "ffn_conv_b", "ffn_w_down", "ple_w_gate",
           "ple_b_gate", "ple_w_proj", "ln2_g", "ln2_b"]
F32_ON_WIRE = ("gdn_conv_w", "ffn_conv_w")
GATHER_EARLY = ["w_in", "gdn_conv_w"]
GATHER_LATE = ["mla_w_q_up", "mla_w_kv_up", "w_out", "ffn_w_up", "ffn_conv_w", "ffn_w_down", "ple_w_gate", "ple_w_proj"]
SCATTER_EARLY = ["ffn_w_up", "ffn_conv_w", "ffn_w_down", "ple_w_gate", "ple_w_proj", "w_out"]
SCATTER_LATE = ["w_in", "gdn_conv_w", "mla_w_q_up", "mla_w_kv_up"]
PACK_COLS = 1024
PACK_ROW_TILE = 8


def _join_blocks(blocks, axis):
    n, a, b = blocks.shape
    if axis == 0:
        return blocks.reshape(n * a, b)
    return jnp.transpose(blocks, (1, 0, 2)).reshape(a, n * b)


def _split_blocks(full, axis):
    if axis == 0:
        return full.reshape(N_CHIPS, full.shape[0] // N_CHIPS, full.shape[1])
    a, nb = full.shape
    return jnp.transpose(full.reshape(a, N_CHIPS, nb // N_CHIPS), (1, 0, 2))


def _pack(arrays):
    flat = jnp.concatenate([a.reshape(-1) for a in arrays])
    quantum = PACK_COLS * PACK_ROW_TILE
    padded = -(-flat.shape[0] // quantum) * quantum
    return jnp.pad(flat, (0, padded - flat.shape[0])).reshape(-1, PACK_COLS)


def _unpack(packed, shapes):
    flat = packed.reshape(-1)
    out, off = [], 0
    for shp in shapes:
        n = int(np.prod(shp))
        out.append(flat[off:off + n].reshape(shp))
        off += n
    return out


def kernel(x, p, w_in, gdn_conv_w, gdn_a_log, gdn_dt_bias, gdn_norm_g, mla_q_norm_g, mla_w_q_up, mla_kv_norm_g, mla_w_kv_up, w_out, ln1_g, ln1_b, ffn_w_up, ffn_conv_w, ffn_conv_b, ffn_w_down, ple_w_gate, ple_b_gate, ple_w_proj, ln2_g, ln2_b, loss_target, m_w_in, m_gdn_conv_w, m_gdn_a_log, m_gdn_dt_bias, m_gdn_norm_g, m_mla_q_norm_g, m_mla_w_q_up, m_mla_kv_norm_g, m_mla_w_kv_up, m_w_out, m_ln1_g, m_ln1_b, m_ffn_w_up, m_ffn_conv_w, m_ffn_conv_b, m_ffn_w_down, m_ple_w_gate, m_ple_b_gate, m_ple_w_proj, m_ln2_g, m_ln2_b, v_w_in, v_gdn_conv_w, v_gdn_a_log, v_gdn_dt_bias, v_gdn_norm_g, v_mla_q_norm_g, v_mla_w_q_up, v_mla_kv_norm_g, v_mla_w_kv_up, v_w_out, v_ln1_g, v_ln1_b, v_ffn_w_up, v_ffn_conv_w, v_ffn_conv_b, v_ffn_w_down, v_ple_w_gate, v_ple_b_gate, v_ple_w_proj, v_ln2_g, v_ln2_b):
    given = dict(locals())
    wsh = {n: given[n][0] for n in WEIGHTS}
    msh = {n: given["m_" + n][0] for n in WEIGHTS}
    vsh = {n: given["v_" + n][0] for n in WEIGHTS}
    bl, s, _ = x.shape
    t = bl * s
    xt = x.reshape(t, D_MODEL)
    pt = p.reshape(t, PLE_DIM)
    target = loss_target.reshape(t, D_MODEL)

    wire = lambda n: wsh[n] if n in F32_ON_WIRE else wsh[n].astype(BF16)
    early = _gather_call([wire(n) for n in GATHER_EARLY], "weights_gather_early")
    full = {n: _join_blocks(g, SHARD_AXIS[n]) for n, g in zip(GATHER_EARLY, early)}
    late_shards = [wire(n) for n in GATHER_LATE]
    late_ride = (late_shards, _gather_copies, _gather_sems,
                 [jax.ShapeDtypeStruct((N_CHIPS,) + a.shape, a.dtype) for a in late_shards], 0.75)

    in_cols, q_cols = _w_in_cols(), _w_q_cols()
    w_in_p = _pad_cols(full["w_in"], in_cols)
    gconv = full["gdn_conv_w"]
    row = lambda a: a.reshape(1, -1)
    sc = jnp.zeros((8, 128), F32).at[0, :GDN_HEADS].set(wsh["gdn_a_log"]).at[1, :GDN_HEADS].set(wsh["gdn_dt_bias"])
    norm_g, qg, kvg = row(wsh["gdn_norm_g"]), row(wsh["mla_q_norm_g"]), row(wsh["mla_kv_norm_g"])
    g1, b1, g2, b2 = row(wsh["ln1_g"]), row(wsh["ln1_b"]), row(wsh["ln2_g"]), row(wsh["ln2_b"])
    fbias, bgate = row(wsh["ffn_conv_b"]), row(wsh["ple_b_gate"])

    inv = ROPE_THETA ** (-jnp.arange(0, MLA_ROPE, 2, dtype=F32) / MLA_ROPE)
    ang = jnp.arange(s, dtype=F32)[:, None] * inv[None, :]
    zero = jnp.zeros_like(ang)
    cos_t = jnp.concatenate([jnp.cos(ang), zero, jnp.cos(ang), zero], axis=1)
    sin_t = jnp.concatenate([-jnp.sin(ang), zero, jnp.sin(ang), zero], axis=1)

    proj = _matmul(xt, w_in_p, name="proj", tm=1024)
    cat, o_raw, states, qkvg, late = _gdn_fwd(proj, gconv, sc, norm_g, bl, s, late_ride)
    w_up = late[GATHER_LATE.index("ffn_w_up")]
    full.update({n: _join_blocks(g, SHARD_AXIS[n]) for n, g in zip(GATHER_LATE, late) if n != "ffn_w_up"})
    w_o, w_down = full["w_out"], full["ffn_w_down"]
    w_gate, w_proj, fconv = full["ple_w_gate"], full["ple_w_proj"], full["ffn_conv_w"]
    w_q_p, w_kv = _pad_cols(full["mla_w_q_up"], q_cols), full["mla_w_kv_up"]
    qf, kvf, kr = _mla_prep_fwd(proj, qg, kvg, w_q_p, w_kv, cos_t, sin_t, s, 512)
    cat, attn_o32, attn_lse = _attn_fwd(qf, kvf, kr, cat, bl, s, 512)
    wide = dict(tm=1024, tn=1024)
    r1, h1, h1b, xb = _mix_ln1_fwd(xt, cat, w_o, g1, b1, 512)
    u = _matmul(h1b, w_up, name="ffn_up", tm=1024, tn=1408, b_parts=N_CHIPS)
    act = _ffn_act_fwd(u, fconv, fbias, bl, s, 256)
    dr2, dr2b, dgpre, dpp, loss_acc, dbgate, dg2, db2 = _head(h1, h1b, act, pt, w_down, w_gate, w_proj, bgate, g2, b2, target, 256)

    dact = _matmul(dr2b, w_down, name="d_act", tb=True, tm=1024, tn=1408)
    long_k = dict(ta=True, tk=2048)
    d_w_down = _matmul(act, dr2b, name="dw_down", tm=1408, tn=1024, **long_k)
    du, dfcw_g, dfcw_u, dfcb_g, dfcb_u = _ffn_act_bwd(u, fconv, fbias, dact, bl, s, 256)
    dh1_a = _matmul(du, w_up, name="dh1_ffn", tb=True, tk=1408, a_halves=True, b_parts=N_CHIPS, **wide)
    d_w_up = _matmul(h1b, du, name="dw_up", tn=1408, b_parts=2, out_parts=N_CHIPS, out_dtype=BF16, **long_k)
    d_w_gate = _matmul(h1b, dgpre, name="dw_gate", **long_k, **wide)
    d_w_proj = _matmul(pt, dpp, name="dw_proj", ta=True, tn=1024)
    dr1, dr1b, dcat, dg1, db1 = _ln1_bwd(r1, dr2, dh1_a, dgpre, w_gate, w_o, g1, b1, 256)
    d_w_o = _matmul(cat, dr1b, name="dw_out", **long_k, **wide)

    gfull = {
        "ffn_w_down": d_w_down, "ple_w_gate": d_w_gate, "ple_w_proj": d_w_proj, "w_out": d_w_o,
        "ffn_conv_w": jnp.concatenate([jnp.sum(dfcw_g, 0), jnp.sum(dfcw_u, 0)], axis=1),
    }
    slabs = {n: _split_blocks(g, SHARD_AXIS[n]).astype(BF16) for n, g in gfull.items()}
    slabs["ffn_w_up"] = d_w_up
    early_slabs = [slabs[n] for n in SCATTER_EARLY]
    early_ride = (early_slabs, _scatter_copies, _scatter_sems,
                  [jax.ShapeDtypeStruct((N_DEV,) + a.shape[1:], a.dtype) for a in early_slabs], 0.7)
    dproj, dab, dcwq, dcwk, dcwv, dsc, dng, early_recv = _gdn_bwd(proj, gconv, sc, norm_g, o_raw, states, qkvg, dcat, bl, s,
                                                                  early_ride)
    received = dict(zip(SCATTER_EARLY, early_recv))
    dqf, dkvf, dkr = _attn_bwd(qf, kvf, kr, dcat, attn_o32, attn_lse, bl, s, 256)
    dproj, dqg, dkvg, d_w_q_p, d_w_kv = _mla_prep_bwd(proj, qg, kvg, w_q_p, w_kv, cos_t, sin_t, dqf, dkvf, dkr, dab, dproj, s, 256)
    d_w_in_p = _matmul(xb, dproj, name="dw_in", **long_k, **wide)

    gfull.update({
        "w_in": _unpad_cols(d_w_in_p, in_cols, D_IN),
        "mla_w_q_up": _unpad_cols(d_w_q_p, q_cols, MLA_HEADS * (MLA_NOPE + MLA_ROPE)),
        "mla_w_kv_up": d_w_kv,
        "gdn_conv_w": jnp.concatenate([jnp.sum(dcwq, 0), jnp.sum(dcwk, 0), jnp.sum(dcwv, 0)], axis=1),
    })
    slabs.update({n: _split_blocks(gfull[n], SHARD_AXIS[n]).astype(BF16) for n in SCATTER_LATE})
    late_slabs = [slabs[n] for n in SCATTER_LATE]
    late_scatter = (late_slabs, _scatter_copies, _scatter_sems,
                    [jax.ShapeDtypeStruct((N_DEV,) + a.shape[1:], a.dtype) for a in late_slabs], 0.85)
    grad_x, late_recv = _matmul(dproj, w_in_p, name="d_x", tb=True, add=dr1, add_scale=ALPHA, ride=late_scatter, **wide)
    received.update(zip(SCATTER_LATE, late_recv))
    dsc_sum = jnp.sum(dsc, axis=(0, 1))
    gsmall = {
        "gdn_a_log": dsc_sum[0, :GDN_HEADS], "gdn_dt_bias": dsc_sum[1, :GDN_HEADS],
        "gdn_norm_g": jnp.sum(dng[:, :, 0, :], axis=(0, 1)),
        "mla_q_norm_g": dqg[0], "mla_kv_norm_g": dkvg[0], "ln1_g": dg1[0], "ln1_b": db1[0],
        "ffn_conv_b": jnp.concatenate([jnp.sum(dfcb_g, 0), jnp.sum(dfcb_u, 0)], axis=1)[0],
        "ple_b_gate": dbgate[0], "ln2_g": dg2[0], "ln2_b": db2[0],
    }

    big = [{}, {}, {}, {}]
    for n in SHARDED:
        for kind, val in enumerate(_adamw_reduced(received[n], wsh[n], msh[n], vsh[n], "adamw_" + n)):
            big[kind][n] = val

    small_shapes = [wsh[n].shape for n in SMALL]
    gsum = _all_reduce_small(_pack([gsmall[n] for n in SMALL]))
    spacks = _adamw_small(gsum, _pack([wsh[n] for n in SMALL]), _pack([msh[n] for n in SMALL]), _pack([vsh[n] for n in SMALL]))
    small = [dict(zip(SMALL, _unpack(pk, small_shapes))) for pk in spacks]

    loss = lax.psum(loss_acc[0, 0], ("x", "y", "c"))
    outs = [loss, grad_x.reshape(x.shape)]
    for kind in range(4):
        for n in WEIGHTS:
            val = big[kind][n] if n in big[kind] else small[kind][n]
            outs.append(val[None])
    return tuple(outs)
```

```python
import functools
import math

import numpy as np
import jax
import jax.numpy as jnp
from jax import lax
from jax.experimental import pallas as pl
from jax.experimental.pallas import tpu as pltpu

F32 = jnp.float32
BF16 = jnp.bfloat16

D_MODEL = 1024
CHUNK = 64
PLE_DIM = 256
GDN_HEADS = 4
GDN_DK = 128
GDN_DV = 128
GDN_CONV = 4
MLA_HEADS = 4
MLA_NOPE = 128
MLA_ROPE = 64
MLA_V = 128
MLA_Q_LORA = 384
MLA_KV_LORA = 256
ROPE_THETA = 10000.0
D_FF = 2816
FFN_CONV = 3
DEPTH = 1
ALPHA = (2.0 * DEPTH) ** 0.25
NORM_EPS = 1e-6
GDN_QK = GDN_HEADS * GDN_DK
GDN_VW = GDN_HEADS * GDN_DV
D_IN = 2 * GDN_QK + 2 * GDN_VW + 2 * GDN_HEADS + MLA_Q_LORA + MLA_KV_LORA + MLA_ROPE
ATT_SCALE = (MLA_NOPE + MLA_ROPE) ** -0.5

ADAM_LR = 0.001
ADAM_B1 = 0.9
ADAM_B2 = 0.999
ADAM_EPS = 1e-08
ADAM_WD = 0.01
ADAM_STEP = 10

LANES = 128
VMEM_LIMIT = 60 * 1024 * 1024
GDN_FWD_GROUP = 16
GDN_BWD_GROUP = 16
N_CHIPS = 4
N_DEV = 8

P_WIDTH = 3072
P_MLA = 2048
MESH = pl.DeviceIdType.MESH


def _rope_slot(j):
    return j if j < MLA_ROPE // 2 else 64 + (j - MLA_ROPE // 2)


def _w_in_cols():
    idx = -np.ones((P_WIDTH,), np.int64)
    for h in range(GDN_HEADS):
        base = h * 512
        idx[base:base + 128] = np.arange(128) + h * GDN_DK
        idx[base + 128:base + 256] = np.arange(128) + GDN_QK + h * GDN_DK
        idx[base + 256:base + 384] = np.arange(128) + 2 * GDN_QK + h * GDN_DV
        idx[base + 384:base + 512] = np.arange(128) + 2 * GDN_QK + GDN_VW + h * GDN_DV
    o_a = 2 * GDN_QK + 2 * GDN_VW
    idx[P_MLA:P_MLA + 2 * GDN_HEADS] = np.arange(2 * GDN_HEADS) + o_a
    o_cq = o_a + 2 * GDN_HEADS
    idx[P_MLA + 128:P_MLA + 512] = np.arange(MLA_Q_LORA) + o_cq
    o_ckv = o_cq + MLA_Q_LORA
    idx[P_MLA + 512:P_MLA + 768] = np.arange(MLA_KV_LORA) + o_ckv
    o_kr = o_ckv + MLA_KV_LORA
    for j in range(MLA_ROPE):
        idx[P_MLA + 768 + _rope_slot(j)] = o_kr + j
    return idx


def _w_q_cols():
    idx = -np.ones((MLA_HEADS * 256,), np.int64)
    for h in range(MLA_HEADS):
        o = h * (MLA_NOPE + MLA_ROPE)
        idx[h * 256:h * 256 + 128] = np.arange(128) + o
        for j in range(MLA_ROPE):
            idx[h * 256 + 128 + _rope_slot(j)] = o + MLA_NOPE + j
    return idx


def _pad_cols(w, idx):
    safe = np.where(idx >= 0, idx, 0)
    return jnp.where(jnp.asarray(idx >= 0)[None, :], w[:, safe], 0.0)


def _unpad_cols(wp, idx, n):
    inv = np.zeros((n,), np.int64)
    inv[idx[idx >= 0]] = np.nonzero(idx >= 0)[0]
    return wp[:, inv]


def _dot(a, b, ca, cb, precision=None):
    if precision is None:
        a = a.astype(BF16)
        b = b.astype(BF16)
    return lax.dot_general(a, b, (((ca,), (cb,)), ((), ())), preferred_element_type=F32, precision=precision)


@jax.custom_vjp
def mm(a, b):
    return _dot(a, b, 1, 0)


@jax.custom_vjp
def mm_nt(a, b):
    return _dot(a, b, 1, 1)


@jax.custom_vjp
def mm_tn(a, b):
    return _dot(a, b, 0, 0)


mm.defvjp(lambda a, b: (mm(a, b), (a, b)), lambda r, g: (mm_nt(g, r[1]), mm_tn(r[0], g)))
mm_nt.defvjp(lambda a, b: (mm_nt(a, b), (a, b)), lambda r, g: (mm(g, r[1]), mm_tn(g, r[0])))
mm_tn.defvjp(lambda a, b: (mm_tn(a, b), (a, b)), lambda r, g: (mm_nt(r[1], g), mm(r[0], g)))

def _split(a):
    hi = a.astype(BF16)
    return hi, (a - hi.astype(F32)).astype(BF16)


def _dot3(a, b, ca, cb):
    a_hi, a_lo = _split(a)
    b_hi, b_lo = _split(b)
    return (_dot(a_hi, b_hi, ca, cb) + _dot(a_hi, b_lo, ca, cb)) + _dot(a_lo, b_hi, ca, cb)


def _shift_rows(x, s):
    return x if s == 0 else pltpu.roll(x, s % x.shape[0], 0)


def _row(w, j):
    tap = lax.broadcasted_iota(jnp.int32, w.shape, 0)
    return jnp.sum(jnp.where(tap == j, w, 0.0), axis=0, keepdims=True)


@jax.custom_vjp
def dwconv(x, w):
    k = w.shape[0]
    y = _row(w, k - 1) * x
    for j in range(k - 1):
        y = y + _row(w, j) * _shift_rows(x, k - 1 - j)
    return y


def _dwconv_fwd(x, w):
    return dwconv(x, w), (x, w)


def _dwconv_bwd(res, dy):
    x, w = res
    k = w.shape[0]
    dx = _row(w, k - 1) * dy
    tap = lax.broadcasted_iota(jnp.int32, w.shape, 0)
    dw = jnp.where(tap == k - 1, jnp.sum(dy * x, axis=0, keepdims=True), 0.0)
    for j in range(k - 1):
        dx = dx + _row(w, j) * _shift_rows(dy, -(k - 1 - j))
        dw = dw + jnp.where(tap == j, jnp.sum(dy * _shift_rows(x, k - 1 - j), axis=0, keepdims=True), 0.0)
    return dx, dw


dwconv.defvjp(_dwconv_fwd, _dwconv_bwd)


@jax.custom_vjp
def rope128(x, cos, sin):
    return x * cos + pltpu.roll(x, 64, 1) * sin


rope128.defvjp(lambda x, c, s: (rope128(x, c, s), (c, s)),
               lambda r, g: (g * r[0] + pltpu.roll(g * r[1], 64, 1), jnp.zeros_like(r[0]), jnp.zeros_like(r[1])))


def _silu(x):
    return x * jax.nn.sigmoid(x)


def _softplus(x):
    return jnp.maximum(x, 0.0) + jnp.log(1.0 + jnp.exp(-jnp.abs(x)))


def _rmsnorm(x, g):
    return x * lax.rsqrt(jnp.mean(x * x, axis=-1, keepdims=True) + NORM_EPS) * g


def _layernorm(x, g, b):
    mu = jnp.mean(x, axis=-1, keepdims=True)
    xc = x - mu
    var = jnp.mean(xc * xc, axis=-1, keepdims=True)
    return xc * lax.rsqrt(var + NORM_EPS) * g + b


def _pick_lane(row, lane):
    idx = lax.broadcasted_iota(jnp.int32, row.shape, 1)
    return jnp.sum(jnp.where(idx == lane, row, 0.0), axis=1, keepdims=True)


def _gdn_q(pq, cw):
    h = _silu(dwconv(pq, cw))
    return h * lax.rsqrt(jnp.sum(h * h, axis=-1, keepdims=True) + NORM_EPS) * (GDN_DK ** -0.5)


def _gdn_k(pk, cw):
    h = _silu(dwconv(pk, cw))
    return h * lax.rsqrt(jnp.sum(h * h, axis=-1, keepdims=True) + NORM_EPS)


def _gdn_v(pv, cw):
    return _silu(dwconv(pv, cw))


def _gdn_gate(ab, sc, head):
    a = _pick_lane(ab, head)
    b = _pick_lane(ab, GDN_HEADS + head)
    a_log = _pick_lane(_row(sc, 0), head)
    dt_bias = _pick_lane(_row(sc, 1), head)
    beta = jax.nn.sigmoid(b)
    g = -jnp.exp(a_log) * _softplus(a + dt_bias)
    return _two_lanes(g, beta)


def _two_lanes(c0, c1):
    lane = lax.broadcasted_iota(jnp.int32, (c0.shape[0], LANES), 1)
    return jnp.where(lane == 0, c0, jnp.where(lane == 1, c1, 0.0))


def _run(levels):
    try:
        while True:
            next(levels)
    except StopIteration as stop:
        return stop.value


def _inverse_levels(lows):
    n = lows[0].shape[0]
    ii = lax.broadcasted_iota(jnp.int32, (n, n), 0)
    jj = lax.broadcasted_iota(jnp.int32, (n, n), 1)
    eye = jnp.where(ii == jj, 1.0, 0.0)
    invs = [eye - low for low in lows]
    powers = [_dot3(low, low, 1, 0) for low in lows]
    yield
    k = 2
    while k < n:
        invs = [inv + _dot3(inv, p, 1, 0) for inv, p in zip(invs, powers)]
        yield
        k *= 2
        if k < n:
            powers = [_dot3(p, p, 1, 0) for p in powers]
            yield
    return invs


def _inverse_group(lows):
    return _run(_inverse_levels(lows))


@jax.custom_vjp
def solve_group(lows, rhss):
    return [_dot3(inv, rhs, 1, 0) for inv, rhs in zip(_inverse_group(lows), rhss)]


def _solve_whole(lows, rhss):
    return solve_group(lows, rhss)
    yield


def _solve_levels(lows, rhss):
    invs = yield from _inverse_levels(lows)
    return [_dot3(inv, rhs, 1, 0) for inv, rhs in zip(invs, rhss)]


def _solve_group_fwd(lows, rhss):
    invs = _inverse_group(lows)
    xs = [_dot3(inv, rhs, 1, 0) for inv, rhs in zip(invs, rhss)]
    return xs, (invs, xs)


def _solve_group_bwd(res, dxs):
    invs, xs = res
    n = invs[0].shape[0]
    strict = lax.broadcasted_iota(jnp.int32, (n, n), 0) > lax.broadcasted_iota(jnp.int32, (n, n), 1)
    drhss = [_dot3(inv, dx, 0, 0) for inv, dx in zip(invs, dxs)]
    dlows = [jnp.where(strict, -_dot3(drhs, x, 1, 1), 0.0) for drhs, x in zip(drhss, xs)]
    return dlows, drhss


solve_group.defvjp(_solve_group_fwd, _solve_group_bwd)


def _gdn_local_group(qs, ks, vs, gbs):
    return _run(_gdn_local_levels(qs, ks, vs, gbs, _solve_whole))


def _gdn_local_levels(qs, ks, vs, gbs, solve):
    c = qs[0].shape[0]
    ii = lax.broadcasted_iota(jnp.int32, (c, c), 0)
    jj = lax.broadcasted_iota(jnp.int32, (c, c), 1)
    incl = ii >= jj
    gs = [_pick_lane(gb, 0) for gb in gbs]
    betas = [_pick_lane(gb, 1) for gb in gbs]
    g_rows = [jnp.sum(jnp.where(ii == jj, g, 0.0), axis=0, keepdims=True) for g in gs]
    gc_cols = [jnp.sum(jnp.where(incl, g_row, 0.0), axis=1, keepdims=True) for g_row in g_rows]
    gc_rows = [jnp.sum(jnp.where(jj >= ii, g, 0.0), axis=0, keepdims=True) for g in gs]
    decays = [jnp.where(incl, jnp.exp(jnp.where(incl, gc - gr, 0.0)), 0.0) for gc, gr in zip(gc_cols, gc_rows)]
    yield
    kbs = [k * beta for k, beta in zip(ks, betas)]
    lows = [jnp.where(ii > jj, mm_nt(kb, k) * decay, 0.0) for kb, k, decay in zip(kbs, ks, decays)]
    egs = [jnp.exp(gc) for gc in gc_cols]
    yield
    wus = yield from solve(lows, [jnp.concatenate([kb * eg, v * beta], axis=1) for kb, eg, v, beta in zip(kbs, egs, vs, betas)])
    yield
    qks = [mm_nt(q, k) * decay for q, k, decay in zip(qs, ks, decays)]
    g_lasts = [jnp.sum(g_row, axis=1, keepdims=True) for g_row in g_rows]
    kds = [k * jnp.exp(gl - gc) for k, gl, gc in zip(ks, g_lasts, gc_cols)]
    yield
    ws, us = [wu[:, :GDN_DK] for wu in wus], [wu[:, GDN_DK:] for wu in wus]
    q_effs = [q * eg - mm(qk, w) for q, eg, qk, w in zip(qs, egs, qks, ws)]
    yield
    o_locals = [mm(qk, u) for qk, u in zip(qks, us)]
    yield
    mixes = [mm_tn(kd, w) for kd, w in zip(kds, ws)]
    yield
    adds = [mm_tn(kd, u) for kd, u in zip(kds, us)]
    return [(q_eff, o_loc, mix, add, jnp.exp(gl))
            for q_eff, o_loc, mix, add, gl in zip(q_effs, o_locals, mixes, adds, g_lasts)]


def _gdn_state_step(q_eff, o_local, mix, add, eg_last, state):
    return mm(q_eff, state) + o_local, state * eg_last - mm(mix, state) + add


def _gdn_post(o, z, norm_g):
    return _rmsnorm(o, norm_g) * _silu(z)


MASKED = -1e30


def _scores(q, kn, kr, q0, k0):
    s = (mm_nt(q[:, :128], kn) + mm_nt(q[:, 128:], kr)) * ATT_SCALE
    if k0 + kn.shape[0] <= q0:
        return s
    qpos = q0 + lax.broadcasted_iota(jnp.int32, s.shape, 0)
    kpos = k0 + lax.broadcasted_iota(jnp.int32, s.shape, 1)
    shift = int(math.log2(CHUNK))
    return jnp.where((kpos >> shift) <= (qpos >> shift), s, MASKED)


def _softmax_times(s, v):
    top = jnp.max(s, axis=-1, keepdims=True)
    p = jnp.exp(s - top)
    norm = jnp.sum(p, axis=-1, keepdims=True)
    return mm(p / norm, v), top + jnp.log(norm)


def _mla_prep(pm, qg, kvg, wq, wkv, cos, sin):
    cq = pm[:, 128:512]
    ckv = pm[:, 512:768]
    qf = mm(_rmsnorm(cq, qg), wq)
    parts = []
    for h in range(MLA_HEADS):
        parts.append(qf[:, h * 256:h * 256 + 128])
        parts.append(rope128(qf[:, h * 256 + 128:h * 256 + 256], cos, sin))
    kvf = mm(_rmsnorm(ckv, kvg), wkv)
    return jnp.concatenate(parts, axis=1), kvf, rope128(pm[:, 768:896], cos, sin)


def _ffn_act(ug, uu, wg, wu, bg, bu):
    return _silu(dwconv(ug, wg) + bg) * (dwconv(uu, wu) + bu)


def _head_loss(h1, ffn, gpre, pp, bgate, g2, b2, target):
    gate = jax.nn.sigmoid(gpre + bgate)
    h2 = _layernorm(ALPHA * h1 + ffn + gate * pp, g2, b2)
    err = h2 - target
    return 0.5 * jnp.sum(jnp.sum(err * err, axis=1, keepdims=True), axis=0, keepdims=True) / D_MODEL


ROW_TILE_VREGS = 32
CONV_HALO = 8


def _rows_per_tile(n_rows, cols):
    tile = min(n_rows, ROW_TILE_VREGS * 8 * LANES // cols)
    assert n_rows % tile == 0 and tile % CONV_HALO == 0, (n_rows, cols)
    return tile


def _tile_inputs(loads, t0, first, halo, tile):
    if halo == 0:
        return [ld(pl.ds(t0, tile)) for ld in loads]
    if first:
        xs = [ld(pl.ds(0, tile)) for ld in loads]
        return [jnp.concatenate([jnp.zeros((halo, x.shape[1]), x.dtype), x], axis=0) for x in xs]
    return [ld(pl.ds(pl.multiple_of(t0 - halo, CONV_HALO), tile + halo)) for ld in loads]


def _rows_apply(fn, loads, consts, store, n_rows, cols, halo):
    tile = _rows_per_tile(n_rows, cols)

    def one(t0, first):
        y = fn(*_tile_inputs(loads, t0, first, halo, tile), *consts)
        store(pl.ds(t0, tile), y[halo:] if halo else y)

    one(0, True)

    def step(i, carry):
        one(pl.multiple_of(i * tile, tile), False)
        return carry

    lax.fori_loop(1, n_rows // tile, step, 0)


def _rows_vjp(fn, loads, consts, load_dy, stores, n_rows, cols, halo):
    tile = _rows_per_tile(n_rows, cols)

    def one(t0, first, dconsts):
        xs = _tile_inputs(loads, t0, first, halo, tile)
        _, vjp = jax.vjp(lambda *a: fn(*a)[halo:] if halo else fn(*a), *xs, *consts)
        grads = vjp(load_dy(pl.ds(t0, tile)))
        for st, dx in zip(stores, grads[:len(xs)]):
            st(pl.ds(t0, tile), dx[halo:] if halo else dx, False)
            if halo and not first:
                st(pl.ds(pl.multiple_of(t0 - halo, CONV_HALO), halo), dx[:halo], True)
        return tuple(a + b for a, b in zip(dconsts, grads[len(xs):]))

    dconsts = one(0, True, tuple(jnp.zeros_like(c) for c in consts))
    return lax.fori_loop(1, n_rows // tile, lambda i, dc: one(pl.multiple_of(i * tile, tile), False, dc), dconsts)


def _params(sem):
    return pltpu.CompilerParams(dimension_semantics=sem, vmem_limit_bytes=VMEM_LIMIT)


def _matmul(a, b, *, name, ta=False, tb=False, tm=512, tn=512, tk=1024, add=None, add_scale=1.0,
            a_halves=False, b_parts=0, out_parts=0, ride=None, out_dtype=F32):
    assert not (a_halves and ta)
    a_shape = (a.shape[1], 2 * a.shape[2]) if a_halves else a.shape
    b_shape = (b.shape[1], b_parts * b.shape[2]) if b_parts else b.shape
    (k_dim, m) = a_shape if ta else a_shape[::-1]
    (n, k2) = b_shape if tb else b_shape[::-1]
    assert k_dim == k2, (a.shape, b.shape)
    tm, tn, tk = min(tm, m), min(tn, n), min(tk, k_dim)
    assert m % tm == 0 and n % tn == 0 and k_dim % tk == 0, (name, m, n, k_dim, tm, tn, tk)
    nk = k_dim // tk
    ca, cb = (0 if ta else 1), (1 if tb else 0)

    def body(*refs):
        if add is None:
            a_ref, b_ref, o_ref, acc = refs
        else:
            a_ref, b_ref, c_ref, o_ref, acc = refs
        kk = pl.program_id(2)

        @pl.when(kk == 0)
        def _():
            acc[...] = jnp.zeros_like(acc)

        acc[...] += _dot(a_ref[...], b_ref[...], ca, cb)

        @pl.when(kk == nk - 1)
        def _():
            r = acc[...]
            if add is not None:
                r = r + add_scale * c_ref[...]
            o_ref[...] = r.astype(out_dtype)

    def per_part(total, parts, tile):
        per = total // parts // tile
        assert per * tile * parts == total, (name, total, parts, tile)
        return per

    spec = pl.BlockSpec
    a_spec = spec((tk, tm), lambda i, j, k: (k, i)) if ta else spec((tm, tk), lambda i, j, k: (i, k))
    b_spec = spec((tn, tk), lambda i, j, k: (j, k)) if tb else spec((tk, tn), lambda i, j, k: (k, j))
    if a_halves:
        kh = per_part(k_dim, 2, tk)
        a_spec = spec((None, tm, tk), lambda i, j, k: (k // kh, i, k % kh))
    if b_parts and tb:
        kp = per_part(k_dim, b_parts, tk)
        b_spec = spec((None, tn, tk), lambda i, j, k: (k // kp, j, k % kp))
    elif b_parts:
        np_ = per_part(n, b_parts, tn)
        b_spec = spec((None, tk, tn), lambda i, j, k: (j // np_, k, j % np_))
    out_spec, out_shape = spec((tm, tn), lambda i, j, k: (i, j)), (m, n)
    if out_parts:
        op = per_part(n, out_parts, tn)
        out_spec, out_shape = spec((None, tm, tn), lambda i, j, k: (j // op, i, j % op)), (out_parts, m, n // out_parts)
    in_specs = [a_spec, b_spec]
    args = [a, b]
    if add is not None:
        in_specs.append(pl.BlockSpec((tm, tn), lambda i, j, k: (i, j)))
        args.append(add)
    grid = (m // tm, n // tn, nk)
    r_in, r_out, r_shapes, r_sems, r_args = _ride_specs(ride)
    outs = pl.pallas_call(
        _riding(body, len(args), 1, 1, ride, grid), name=name, grid=grid,
        in_specs=in_specs + r_in, out_specs=[out_spec] + r_out,
        out_shape=[jax.ShapeDtypeStruct(out_shape, out_dtype)] + r_shapes,
        scratch_shapes=[pltpu.VMEM((tm, tn), F32)] + r_sems,
        compiler_params=_params(("parallel", "parallel", "arbitrary") if ride is None else ("arbitrary",) * 3),
    )(*args, *r_args)
    return outs[0] if ride is None else (outs[0], list(outs[1:]))


def _riding(core, n_in, n_out, n_scratch, ride, grid):
    if ride is None:
        return core
    copies, nr = ride[1], len(ride[0])
    steps = int(np.prod(grid))
    pass_step = min(max(int(steps * ride[4]), 1), steps - 2)
    assert steps >= 3, grid

    def body(*refs):
        cuts = np.cumsum([0, n_in, nr, n_out, nr, n_scratch])
        ins, rin, outs, rout, scratch = (refs[a:b] for a, b in zip(cuts[:-1], cuts[1:]))
        sems = refs[cuts[-1]:]
        step = 0
        for axis, size in enumerate(grid):
            step = step * size + pl.program_id(axis)

        @pl.when(step == 0)
        def _():
            _exchange_start(copies(rin, rout, sems))

        @pl.when(step == pass_step)
        def _():
            _exchange_pass_on(copies(rin, rout, sems))

        core(*ins, *outs, *scratch)

        @pl.when(step == steps - 1)
        def _():
            _exchange_finish(copies(rin, rout, sems))

    return body


def _ride_specs(ride):
    if ride is None:
        return [], [], [], [], []
    arrays, _, sems, shapes, _ = ride
    return [HBM_REF] * len(arrays), [HBM_REF] * len(arrays), list(shapes), sems(len(arrays)), list(arrays)


def _gdn_fwd(proj, conv_w, sc, norm_g, bl, s, ride=None):
    nc = s // CHUNK

    def core(ph_ref, ab_ref, cwq_ref, cwk_ref, cwv_ref, sc_ref, ng_ref, cat_ref, o_ref, st_ref, q_s, k_s, v_s, gb_s):
        def into(ref):
            def store(rows, value):
                ref[rows, :] = value.astype(ref.dtype)
            return store

        for fn, col, cw_ref, val_s in [(_gdn_q, 0, cwq_ref, q_s), (_gdn_k, 128, cwk_ref, k_s), (_gdn_v, 256, cwv_ref, v_s)]:
            _rows_apply(fn, [lambda r, col=col: ph_ref[r, col:col + 128]], [cw_ref[...]], into(val_s), s, LANES, CONV_HALO)
        _rows_apply(functools.partial(_gdn_gate, head=pl.program_id(1)), [lambda r: ab_ref[r, :]], [sc_ref[...]], into(gb_s), s, LANES, 0)

        group = math.gcd(nc, GDN_FWD_GROUP)

        def rows_of(n):
            return slice(n * CHUNK, (n + 1) * CHUNK)

        def levels_of(g):
            rows = [rows_of(g * group + j) for j in range(group)]
            return _gdn_local_levels([q_s[r, :] for r in rows], [k_s[r, :] for r in rows], [v_s[r, :] for r in rows],
                                     [gb_s[r, :] for r in rows], _solve_levels)

        local, state = _run(levels_of(0)), jnp.zeros((GDN_DK, GDN_DV), F32)
        for g in range(nc // group):
            ahead = levels_of(g + 1) if g + 1 < nc // group else None
            following = None
            for j, loc in enumerate(local):
                n = g * group + j
                st_ref[n] = state
                o_ref[rows_of(n), :], state = _gdn_state_step(*loc, state)
                if ahead is not None and following is None:
                    try:
                        next(ahead)
                    except StopIteration as stop:
                        following = stop.value
            if ahead is not None and following is None:
                following = _run(ahead)
            local = following
        _rows_apply(_gdn_post, [lambda r: o_ref[r, :], lambda r: ph_ref[r, 384:512]], [ng_ref[...]], into(cat_ref), s, LANES, 0)

    t = bl * s
    r_in, r_out, r_shapes, r_sems, r_args = _ride_specs(ride)
    outs = pl.pallas_call(
        _riding(core, 7, 7, 0, ride, (bl, GDN_HEADS)), name="gdn_fwd", grid=(bl, GDN_HEADS),
        in_specs=[
            pl.BlockSpec((s, 512), lambda b, h: (b, h)),
            pl.BlockSpec((s, 128), lambda b, h: (b, P_MLA // 128)),
            pl.BlockSpec((GDN_CONV, 128), lambda b, h: (0, h)),
            pl.BlockSpec((GDN_CONV, 128), lambda b, h: (0, GDN_HEADS + h)),
            pl.BlockSpec((GDN_CONV, 128), lambda b, h: (0, 2 * GDN_HEADS + h)),
            pl.BlockSpec((8, 128), lambda b, h: (0, 0)),
            pl.BlockSpec((1, 128), lambda b, h: (0, 0)),
        ] + r_in,
        out_specs=[
            pl.BlockSpec((s, 128), lambda b, h: (b, h)),
            pl.BlockSpec((s, 128), lambda b, h: (b, h)),
            pl.BlockSpec((None, None, nc, GDN_DK, GDN_DV), lambda b, h: (b, h, 0, 0, 0)),
        ] + [pl.BlockSpec((s, 128), lambda b, h: (b, h))] * 4 + r_out,
        out_shape=[
            jax.ShapeDtypeStruct((t, 2 * GDN_VW), BF16),
            jax.ShapeDtypeStruct((t, GDN_VW), F32),
            jax.ShapeDtypeStruct((bl, GDN_HEADS, nc, GDN_DK, GDN_DV), F32),
        ] + [jax.ShapeDtypeStruct((t, GDN_VW), F32)] * 4 + r_shapes,
        scratch_shapes=r_sems,
        compiler_params=_params(("arbitrary", "arbitrary")),
    )(proj, proj, conv_w, conv_w, conv_w, sc, norm_g, *r_args)
    return outs[0], outs[1], outs[2], tuple(outs[3:7]), list(outs[7:])


def _gdn_bwd(proj, conv_w, sc, norm_g, o_raw, states, qkvg, dcat, bl, s, ride=None):
    nc = s // CHUNK

    def core(ph_ref, ab_ref, cwq_ref, cwk_ref, cwv_ref, sc_ref, ng_ref, o_ref, st_ref, dc_ref, q_in, k_in, v_in, gb_in,
             dph_ref, dab_ref, dcwq_ref, dcwk_ref, dcwv_ref, dsc_ref, dng_ref, q_s, k_s, v_s, gb_s, do_s):
        head = pl.program_id(1)
        gate = functools.partial(_gdn_gate, head=head)
        paths = [(_gdn_q, 0, cwq_ref, q_s, dcwq_ref), (_gdn_k, 128, cwk_ref, k_s, dcwk_ref), (_gdn_v, 256, cwv_ref, v_s, dcwv_ref)]
        def into(ref, cols=slice(None)):
            def store(rows, value, add=False):
                if add:
                    ref[rows, cols] += value.astype(ref.dtype)
                else:
                    ref[rows, cols] = value.astype(ref.dtype)
            return store

        (dng,) = _rows_vjp(_gdn_post, [lambda r: o_ref[r, :], lambda r: ph_ref[r, 384:512]], [ng_ref[...]],
                           lambda r: dc_ref[r, :], [into(do_s), into(dph_ref, slice(384, 512))], s, LANES, 0)
        dng_ref[...] = jnp.broadcast_to(dng, dng_ref.shape)

        group = math.gcd(nc, GDN_BWD_GROUP)

        def chunks(i, dstate):
            ns = [nc - 1 - (i * group + j) for j in range(group)]
            rows = [pl.ds(pl.multiple_of(n * CHUNK, CHUNK), CHUNK) for n in ns]
            local, local_vjp = jax.vjp(_gdn_local_group, [q_in[r, :] for r in rows], [k_in[r, :] for r in rows],
                                       [v_in[r, :] for r in rows], [gb_in[r, :] for r in rows])
            d_os = [do_s[r, :] for r in rows]
            dlocal = []
            for n, loc, d_o in zip(ns, local, d_os):
                _, step_vjp = jax.vjp(_gdn_state_step, *loc, st_ref[n])
                *dloc, dstate = step_vjp((d_o, dstate))
                dlocal.append(tuple(dloc))
            dqs, dks, dvs, dgbs = local_vjp(dlocal)
            for r, dq, dk, dv, dgb in zip(rows, dqs, dks, dvs, dgbs):
                q_s[r, :], k_s[r, :], v_s[r, :], gb_s[r, :] = dq, dk, dv, dgb
            return dstate

        lax.fori_loop(0, nc // group, chunks, jnp.zeros((GDN_DK, GDN_DV), F32))
        for fn, col, cw_ref, val_s, dcw_ref in paths:
            (dcw_ref[...],) = _rows_vjp(fn, [lambda r, col=col: ph_ref[r, col:col + 128]], [cw_ref[...]],
                                        lambda r, val_s=val_s: val_s[r, :], [into(val_s)], s, LANES, CONV_HALO)
            dph_ref[:, col:col + 128] = val_s[...].astype(BF16)

        @pl.when(head == 0)
        def _():
            dab_ref[...] = jnp.zeros_like(dab_ref)

        def add_dab(rows, value, add=False):
            dab_ref[rows, :] += value

        (dsc_ref[...],) = _rows_vjp(gate, [lambda r: ab_ref[r, :]], [sc_ref[...]], lambda r: gb_s[r, :], [add_dab], s, LANES, 0)

    t = bl * s
    cw_out = pl.BlockSpec((None, GDN_CONV, 128), lambda b, h: (b, 0, h))
    part = pl.BlockSpec((None, None, 8, 128), lambda b, h: (b, h, 0, 0))
    r_in, r_out, r_shapes, r_sems, r_args = _ride_specs(ride)
    outs = pl.pallas_call(
        _riding(core, 14, 7, 5, ride, (bl, GDN_HEADS)), name="gdn_bwd", grid=(bl, GDN_HEADS),
        in_specs=[
            pl.BlockSpec((s, 512), lambda b, h: (b, h)),
            pl.BlockSpec((s, 128), lambda b, h: (b, P_MLA // 128)),
            pl.BlockSpec((GDN_CONV, 128), lambda b, h: (0, h)),
            pl.BlockSpec((GDN_CONV, 128), lambda b, h: (0, GDN_HEADS + h)),
            pl.BlockSpec((GDN_CONV, 128), lambda b, h: (0, 2 * GDN_HEADS + h)),
            pl.BlockSpec((8, 128), lambda b, h: (0, 0)),
            pl.BlockSpec((1, 128), lambda b, h: (0, 0)),
            pl.BlockSpec((s, 128), lambda b, h: (b, h)),
            pl.BlockSpec((None, None, nc, GDN_DK, GDN_DV), lambda b, h: (b, h, 0, 0, 0)),
        ] + [pl.BlockSpec((s, 128), lambda b, h: (b, h))] * 5 + r_in,
        out_specs=[
            pl.BlockSpec((s, 512), lambda b, h: (b, h)),
            pl.BlockSpec((s, 128), lambda b, h: (b, 0)),
            cw_out, cw_out, cw_out, part, part,
        ] + r_out,
        out_shape=[
            jax.ShapeDtypeStruct((t, P_WIDTH), BF16),
            jax.ShapeDtypeStruct((t, 128), F32),
            jax.ShapeDtypeStruct((bl, GDN_CONV, 512), F32),
            jax.ShapeDtypeStruct((bl, GDN_CONV, 512), F32),
            jax.ShapeDtypeStruct((bl, GDN_CONV, 512), F32),
            jax.ShapeDtypeStruct((bl, GDN_HEADS, 8, 128), F32),
            jax.ShapeDtypeStruct((bl, GDN_HEADS, 8, 128), F32),
        ] + r_shapes,
        scratch_shapes=[pltpu.VMEM((s, 128), F32)] * 5 + r_sems,
        compiler_params=_params(("arbitrary", "arbitrary")),
    )(proj, proj, conv_w, conv_w, conv_w, sc, norm_g, o_raw, states, dcat, *qkvg, *r_args)
    return tuple(outs[:7]) + (list(outs[7:]),)


def _mla_prep_fwd(proj, qg, kvg, wq, wkv, cos, sin, s, tm):
    t = proj.shape[0]
    tm = min(tm, s)
    nps = s // tm
    const = lambda shape: pl.BlockSpec(shape, lambda i: (0, 0))

    def body(pm_ref, qg_ref, kvg_ref, wq_ref, wkv_ref, cos_ref, sin_ref, qf_ref, kvf_ref, kr_ref):
        qf, kvf, kr = _mla_prep(pm_ref[...], qg_ref[...], kvg_ref[...], wq_ref[...], wkv_ref[...], cos_ref[...], sin_ref[...])
        qf_ref[...], kvf_ref[...], kr_ref[...] = qf.astype(BF16), kvf.astype(BF16), kr.astype(BF16)

    return pl.pallas_call(
        body, name="mla_prep_fwd", grid=(t // tm,),
        in_specs=[
            pl.BlockSpec((tm, 1024), lambda i: (i, P_MLA // 1024)),
            const((1, MLA_Q_LORA)), const((1, MLA_KV_LORA)), const(wq.shape), const(wkv.shape),
            pl.BlockSpec((tm, 128), lambda i: (i % nps, 0)), pl.BlockSpec((tm, 128), lambda i: (i % nps, 0)),
        ],
        out_specs=[pl.BlockSpec((tm, 1024), lambda i: (i, 0)), pl.BlockSpec((tm, 1024), lambda i: (i, 0)),
                   pl.BlockSpec((tm, 128), lambda i: (i, 0))],
        out_shape=[jax.ShapeDtypeStruct((t, 1024), BF16), jax.ShapeDtypeStruct((t, 1024), BF16),
                   jax.ShapeDtypeStruct((t, 128), BF16)],
        compiler_params=_params(("parallel",)),
    )(proj, qg, kvg, wq, wkv, cos, sin)


def _mla_prep_bwd(proj, qg, kvg, wq, wkv, cos, sin, dqf, dkvf, dkr, dab, dproj, s, tm):
    t = proj.shape[0]
    tm = min(tm, s)
    nps = s // tm
    const = lambda shape: pl.BlockSpec(shape, lambda i: (0, 0))

    def body(pm_ref, qg_ref, kvg_ref, wq_ref, wkv_ref, cos_ref, sin_ref, dqf_ref, dkvf_ref, dkr_ref, dab_ref, dp_in,
             dp_ref, dqg_ref, dkvg_ref, dwq_ref, dwkv_ref):
        del dp_in
        fn = lambda pm, qg_, kvg_, wq_, wkv_: _mla_prep(pm, qg_, kvg_, wq_, wkv_, cos_ref[...], sin_ref[...])
        _, vjp = jax.vjp(fn, pm_ref[...], qg_ref[...], kvg_ref[...], wq_ref[...].astype(F32), wkv_ref[...].astype(F32))
        dpm, dqg, dkvg, dwq, dwkv = vjp((dqf_ref[...], dkvf_ref[...], dkr_ref[...]))
        dp_ref[...] = jnp.concatenate([dab_ref[...], dpm[:, 128:]], axis=1).astype(BF16)

        @pl.when(pl.program_id(0) == 0)
        def _():
            dqg_ref[...] = jnp.zeros_like(dqg_ref)
            dkvg_ref[...] = jnp.zeros_like(dkvg_ref)
            dwq_ref[...] = jnp.zeros_like(dwq_ref)
            dwkv_ref[...] = jnp.zeros_like(dwkv_ref)

        dqg_ref[...] += dqg
        dkvg_ref[...] += dkvg
        dwq_ref[...] += dwq
        dwkv_ref[...] += dwkv

    rows = lambda w: pl.BlockSpec((tm, w), lambda i: (i, 0))
    return pl.pallas_call(
        body, name="mla_prep_bwd", grid=(t // tm,),
        in_specs=[
            pl.BlockSpec((tm, 1024), lambda i: (i, P_MLA // 1024)),
            const((1, MLA_Q_LORA)), const((1, MLA_KV_LORA)), const(wq.shape), const(wkv.shape),
            pl.BlockSpec((tm, 128), lambda i: (i % nps, 0)), pl.BlockSpec((tm, 128), lambda i: (i % nps, 0)),
            rows(1024), rows(1024), rows(128), rows(128),
            pl.BlockSpec(memory_space=pl.ANY),
        ],
        out_specs=[pl.BlockSpec((tm, 1024), lambda i: (i, P_MLA // 1024)),
                   const((1, MLA_Q_LORA)), const((1, MLA_KV_LORA)), const(wq.shape), const(wkv.shape)],
        out_shape=[jax.ShapeDtypeStruct(dproj.shape, dproj.dtype),
                   jax.ShapeDtypeStruct((1, MLA_Q_LORA), F32), jax.ShapeDtypeStruct((1, MLA_KV_LORA), F32),
                   jax.ShapeDtypeStruct(wq.shape, F32), jax.ShapeDtypeStruct(wkv.shape, F32)],
        input_output_aliases={11: 0},
        compiler_params=_params(("arbitrary",)),
    )(proj, qg, kvg, wq, wkv, cos, sin, dqf, dkvf, dkr, dab, dproj)


def _attn_fwd(qf, kvf, kr, cat, bl, s, tq):
    tq = min(tq, s)
    nq = s // tq

    def body(q_ref, kv_ref, kr_ref, cat_in, o_ref, o32_ref, lse_ref):
        del cat_in

        def scores_of(i):
            keys = slice(0, (i + 1) * tq)
            return _scores(q_ref[i * tq:(i + 1) * tq, :], kv_ref[keys, 0:128], kr_ref[keys, :], i * tq, 0)

        ready = scores_of(0)
        for i in range(nq):
            scores = ready
            if i + 1 < nq:
                ready = scores_of(i + 1)
            rows = slice(i * tq, (i + 1) * tq)
            o, lse = _softmax_times(scores, kv_ref[0:(i + 1) * tq, 128:256])
            o_ref[rows, :] = o.astype(o_ref.dtype)
            o32_ref[rows, :] = o
            lse_ref[rows, :] = jnp.broadcast_to(lse, o.shape)

    t = bl * s
    head_cols = pl.BlockSpec((s, 128), lambda b, h: (b, h))
    return pl.pallas_call(
        body, name="attn_fwd", grid=(bl, MLA_HEADS),
        in_specs=[
            pl.BlockSpec((s, 256), lambda b, h: (b, h)),
            pl.BlockSpec((s, 256), lambda b, h: (b, h)),
            pl.BlockSpec((s, 128), lambda b, h: (b, 0)),
            pl.BlockSpec(memory_space=pl.ANY),
        ],
        out_specs=[pl.BlockSpec((s, 128), lambda b, h: (b, GDN_HEADS + h)), head_cols, head_cols],
        out_shape=[jax.ShapeDtypeStruct(cat.shape, cat.dtype)] + [jax.ShapeDtypeStruct((t, MLA_HEADS * MLA_V), F32)] * 2,
        input_output_aliases={3: 0},
        compiler_params=_params(("parallel", "parallel")),
    )(qf, kvf, kr, cat)


def _attn_bwd(qf, kvf, kr, dcat, o32, lse, bl, s, tq):
    tq = min(tq, s)
    nq = s // tq

    def body(q_ref, kv_ref, kr_ref, do_ref, o_ref, lse_ref, dq_ref, dkv_ref, dkr_ref):
        dkv_ref[...] = jnp.zeros_like(dkv_ref)

        @pl.when(pl.program_id(1) == 0)
        def _():
            dkr_ref[...] = jnp.zeros_like(dkr_ref)

        def block(i):
            rows = slice(i * tq, (i + 1) * tq)
            return q_ref[rows, :], do_ref[rows, :]

        def first_products(i, j):
            (q, d_o), keys = block(i), slice(j * tq, (j + 1) * tq)
            return _scores(q, kv_ref[keys, 0:128], kr_ref[keys, :], i * tq, j * tq), mm_nt(d_o, kv_ref[keys, 128:256])

        tiles = [(i, j) for i in range(nq) for j in range(i + 1)]
        ready = first_products(*tiles[0])
        for t, (i, j) in enumerate(tiles):
            scores, dp = ready
            if t + 1 < len(tiles):
                ready = first_products(*tiles[t + 1])
            rows, keys = slice(i * tq, (i + 1) * tq), slice(j * tq, (j + 1) * tq)
            q, d_o = block(i)
            if j == 0:
                delta = jnp.sum(d_o * o_ref[rows, :], axis=-1, keepdims=True)
                lse_i = jnp.max(lse_ref[rows, :], axis=-1, keepdims=True)
                dqn, dqr = jnp.zeros((tq, MLA_NOPE), F32), jnp.zeros((tq, 128), F32)
            p = jnp.exp(scores - lse_i)
            ds = p * (dp - delta) * ATT_SCALE
            dkv_ref[keys, 128:256] += mm_tn(p, d_o)
            dkv_ref[keys, 0:128] += mm_tn(ds, q[:, :128])
            dkr_ref[keys, :] += mm_tn(ds, q[:, 128:])
            dqn = dqn + mm(ds, kv_ref[keys, 0:128])
            dqr = dqr + mm(ds, kr_ref[keys, :])
            if j == i:
                dq_ref[rows, 0:128], dq_ref[rows, 128:256] = dqn, dqr

    t = bl * s
    head_cols = pl.BlockSpec((s, 128), lambda b, h: (b, h))
    return pl.pallas_call(
        body, name="attn_bwd", grid=(bl, MLA_HEADS),
        in_specs=[
            pl.BlockSpec((s, 256), lambda b, h: (b, h)),
            pl.BlockSpec((s, 256), lambda b, h: (b, h)),
            pl.BlockSpec((s, 128), lambda b, h: (b, 0)),
            pl.BlockSpec((s, 128), lambda b, h: (b, GDN_HEADS + h)),
            head_cols, head_cols,
        ],
        out_specs=[
            pl.BlockSpec((s, 256), lambda b, h: (b, h)),
            pl.BlockSpec((s, 256), lambda b, h: (b, h)),
            pl.BlockSpec((s, 128), lambda b, h: (b, 0)),
        ],
        out_shape=[jax.ShapeDtypeStruct((t, 1024), F32), jax.ShapeDtypeStruct((t, 1024), F32),
                   jax.ShapeDtypeStruct((t, 128), F32)],
        compiler_params=_params(("parallel", "arbitrary")),
    )(qf, kvf, kr, dcat, o32, lse)


def _mix_ln1_fwd(x, cat, w_o, g, b, tm):
    t = x.shape[0]
    tm = min(tm, t)

    def body(x_ref, cat_ref, w_ref, g_ref, b_ref, r_ref, h_ref, hb_ref, xb_ref):
        r = ALPHA * x_ref[...] + _dot(cat_ref[...], w_ref[...], 1, 0)
        r_ref[...] = r
        h = _layernorm(r, g_ref[...], b_ref[...])
        h_ref[...] = h
        hb_ref[...] = h.astype(BF16)
        xb_ref[...] = x_ref[...].astype(BF16)

    rows = pl.BlockSpec((tm, D_MODEL), lambda i: (i, 0))
    vec = pl.BlockSpec((1, D_MODEL), lambda i: (0, 0))
    return pl.pallas_call(
        body, name="mix_ln1_fwd", grid=(t // tm,),
        in_specs=[rows, pl.BlockSpec((tm, cat.shape[1]), lambda i: (i, 0)), pl.BlockSpec(w_o.shape, lambda i: (0, 0)), vec, vec],
        out_specs=[rows] * 4,
        out_shape=[jax.ShapeDtypeStruct(x.shape, F32)] * 2 + [jax.ShapeDtypeStruct(x.shape, BF16)] * 2,
        compiler_params=_params(("parallel",)),
    )(x, cat, w_o, g, b)


def _ln1_bwd(r1, dr2, da, dgpre, w_gate, w_o, g, b, tm):
    t = r1.shape[0]
    tm = min(tm, t)

    def body(r_ref, d2_ref, da_ref, dgp_ref, wg_ref, wo_ref, g_ref, b_ref, dr_ref, drb_ref, dcat_ref, dg_ref, dbias_ref):
        dh = ALPHA * d2_ref[...] + da_ref[...] + _dot(dgp_ref[...], wg_ref[...], 1, 1)
        _, vjp = jax.vjp(_layernorm, r_ref[...], g_ref[...], b_ref[...])
        dr, dg, dbias = vjp(dh)
        dr_ref[...] = dr
        drb_ref[...] = dr.astype(BF16)
        dcat_ref[...] = _dot(dr, wo_ref[...], 1, 1)

        @pl.when(pl.program_id(0) == 0)
        def _():
            dg_ref[...] = jnp.zeros_like(dg_ref)
            dbias_ref[...] = jnp.zeros_like(dbias_ref)

        dg_ref[...] += dg
        dbias_ref[...] += dbias

    rows = pl.BlockSpec((tm, D_MODEL), lambda i: (i, 0))
    vec = pl.BlockSpec((1, D_MODEL), lambda i: (0, 0))
    return pl.pallas_call(
        body, name="ln1_bwd", grid=(t // tm,),
        in_specs=[rows] * 4 + [pl.BlockSpec(w_gate.shape, lambda i: (0, 0)), pl.BlockSpec(w_o.shape, lambda i: (0, 0)), vec, vec],
        out_specs=[rows, rows, pl.BlockSpec((tm, w_o.shape[0]), lambda i: (i, 0)), vec, vec],
        out_shape=[jax.ShapeDtypeStruct(r1.shape, F32), jax.ShapeDtypeStruct(r1.shape, BF16),
                   jax.ShapeDtypeStruct((t, w_o.shape[0]), F32)] + [jax.ShapeDtypeStruct((1, D_MODEL), F32)] * 2,
        compiler_params=_params(("arbitrary",)),
    )(r1, dr2, da, dgpre, w_gate, w_o, g, b)


def _ffn_act_fwd(u, conv_w, conv_b, bl, s, cb):
    nj = D_FF // cb

    def body(ug_ref, uu_ref, wg_ref, wu_ref, bg_ref, bu_ref, act_ref):
        def store(rows, act):
            act_ref[rows, :] = act.astype(BF16)

        _rows_apply(_ffn_act, [lambda r: ug_ref[r, :], lambda r: uu_ref[r, :]],
                    [wg_ref[...], wu_ref[...], bg_ref[...], bu_ref[...]], store, s, cb, CONV_HALO)

    return pl.pallas_call(
        body, name="ffn_act_fwd", grid=(bl, nj),
        in_specs=[
            pl.BlockSpec((s, cb), lambda b, j: (b, j)), pl.BlockSpec((s, cb), lambda b, j: (b, nj + j)),
            pl.BlockSpec((FFN_CONV, cb), lambda b, j: (0, j)), pl.BlockSpec((FFN_CONV, cb), lambda b, j: (0, nj + j)),
            pl.BlockSpec((1, cb), lambda b, j: (0, j)), pl.BlockSpec((1, cb), lambda b, j: (0, nj + j)),
        ],
        out_specs=pl.BlockSpec((s, cb), lambda b, j: (b, j)),
        out_shape=jax.ShapeDtypeStruct((bl * s, D_FF), BF16),
        compiler_params=_params(("parallel", "parallel")),
    )(u, u, conv_w, conv_w, conv_b, conv_b)


def _ffn_act_bwd(u, conv_w, conv_b, dact, bl, s, cb):
    nj = D_FF // cb

    def body(ug_ref, uu_ref, wg_ref, wu_ref, bg_ref, bu_ref, da_ref, du_ref, dwg_ref, dwu_ref, dbg_ref, dbu_ref, acc):
        def store_into(half):
            def store(rows, value, add):
                if add:
                    acc[half, rows, :] += value
                else:
                    acc[half, rows, :] = value
            return store

        dwg_ref[...], dwu_ref[...], dbg_ref[...], dbu_ref[...] = _rows_vjp(
            _ffn_act, [lambda r: ug_ref[r, :], lambda r: uu_ref[r, :]], [wg_ref[...], wu_ref[...], bg_ref[...], bu_ref[...]],
            lambda r: da_ref[r, :], [store_into(0), store_into(1)], s, cb, CONV_HALO)
        du_ref[...] = acc[...].astype(BF16)

    t = bl * s
    blk = pl.BlockSpec((s, cb), lambda b, j: (b, j))
    wpart = pl.BlockSpec((None, FFN_CONV, cb), lambda b, j: (b, 0, j))
    bpart = pl.BlockSpec((None, 1, cb), lambda b, j: (b, 0, j))
    return pl.pallas_call(
        body, name="ffn_act_bwd", grid=(bl, nj),
        in_specs=[
            blk, pl.BlockSpec((s, cb), lambda b, j: (b, nj + j)),
            pl.BlockSpec((FFN_CONV, cb), lambda b, j: (0, j)), pl.BlockSpec((FFN_CONV, cb), lambda b, j: (0, nj + j)),
            pl.BlockSpec((1, cb), lambda b, j: (0, j)), pl.BlockSpec((1, cb), lambda b, j: (0, nj + j)),
            blk,
        ],
        out_specs=[pl.BlockSpec((2, s, cb), lambda b, j: (0, b, j)), wpart, wpart, bpart, bpart],
        out_shape=[jax.ShapeDtypeStruct((2, t, D_FF), BF16)] + [jax.ShapeDtypeStruct((bl, FFN_CONV, D_FF), F32)] * 2
        + [jax.ShapeDtypeStruct((bl, 1, D_FF), F32)] * 2,
        scratch_shapes=[pltpu.VMEM((2, s, cb), F32)],
        compiler_params=_params(("parallel", "parallel")),
    )(u, u, conv_w, conv_w, conv_b, conv_b, dact)


def _head(h1, h1b, act, pt, w_down, w_gate, w_proj, bgate, g2, b2, target, tm):
    t = h1.shape[0]
    tm = min(tm, t)

    def body(h1_ref, h1b_ref, act_ref, pt_ref, wd_ref, wg_ref, wp_ref, bg_ref, g2_ref, b2_ref, tg_ref,
             dr_ref, drb_ref, dgp_ref, dpp_ref, loss_ref, dbg_ref, dg2_ref, db2_ref):
        fn = functools.partial(_head_loss, target=tg_ref[...])
        ffn = _dot(act_ref[...], wd_ref[...], 1, 0)
        gpre, pp = _dot(h1b_ref[...], wg_ref[...], 1, 0), _dot(pt_ref[...], wp_ref[...], 1, 0)
        loss, vjp = jax.vjp(fn, h1_ref[...], ffn, gpre, pp, bg_ref[...], g2_ref[...], b2_ref[...])
        _, dffn, dgp, dpp, dbg, dg2, db2 = vjp(jnp.ones((1, 1), F32))
        dr_ref[...] = dffn
        drb_ref[...], dgp_ref[...], dpp_ref[...] = dffn.astype(BF16), dgp.astype(BF16), dpp.astype(BF16)

        @pl.when(pl.program_id(0) == 0)
        def _():
            loss_ref[...] = jnp.zeros_like(loss_ref)
            dbg_ref[...] = jnp.zeros_like(dbg_ref)
            dg2_ref[...] = jnp.zeros_like(dg2_ref)
            db2_ref[...] = jnp.zeros_like(db2_ref)

        loss_ref[...] += jnp.broadcast_to(loss, loss_ref.shape)
        dbg_ref[...] += dbg
        dg2_ref[...] += dg2
        db2_ref[...] += db2

    rows = pl.BlockSpec((tm, D_MODEL), lambda i: (i, 0))
    vec = pl.BlockSpec((1, D_MODEL), lambda i: (0, 0))
    return pl.pallas_call(
        body, name="head", grid=(t // tm,),
        in_specs=[rows, rows, pl.BlockSpec((tm, act.shape[1]), lambda i: (i, 0)), pl.BlockSpec((tm, pt.shape[1]), lambda i: (i, 0))]
        + [pl.BlockSpec(w.shape, lambda i: (0, 0)) for w in (w_down, w_gate, w_proj)] + [vec] * 3 + [rows],
        out_specs=[rows] * 4 + [pl.BlockSpec((8, 128), lambda i: (0, 0))] + [vec] * 3,
        out_shape=[jax.ShapeDtypeStruct(h1.shape, F32)] + [jax.ShapeDtypeStruct(h1.shape, BF16)] * 3
        + [jax.ShapeDtypeStruct((8, 128), F32)]
        + [jax.ShapeDtypeStruct((1, D_MODEL), F32)] * 3,
        compiler_params=_params(("arbitrary",)),
    )(h1, h1b, act, pt, w_down, w_gate, w_proj, bgate, g2, b2, target)


def _adam_update(g, w_ref, m_ref, v_ref, g_ref, d_ref, nm_ref, nv_ref):
    m2 = ADAM_B1 * m_ref[...] + (1.0 - ADAM_B1) * g
    v2 = ADAM_B2 * v_ref[...] + (1.0 - ADAM_B2) * jnp.square(g)
    m_hat = m2 / (1.0 - ADAM_B1 ** ADAM_STEP)
    v_hat = v2 / (1.0 - ADAM_B2 ** ADAM_STEP)
    g_ref[...] = g
    d_ref[...] = -ADAM_LR * (m_hat / (jnp.sqrt(v_hat) + ADAM_EPS) + ADAM_WD * w_ref[...])
    nm_ref[...] = m2
    nv_ref[...] = v2


def _row_tile(rows, cols, limit_bytes=1024 * 1024):
    best = None
    for t in range(HALF_ROWS_QUANTUM, rows + 1, HALF_ROWS_QUANTUM):
        if rows % t == 0 and t * cols * 4 <= limit_bytes:
            best = t
    return best or rows


def _adamw_reduced(recv, w, m, v, name):
    a, b = w.shape
    ta = _row_tile(a, b)

    def body(recv_ref, w_ref, m_ref, v_ref, g_ref, d_ref, nm_ref, nv_ref):
        c = lax.axis_index("c")
        for core in range(2):
            @pl.when(c == core)
            def _():
                got = [recv_ref[k].astype(F32) for k in range(N_DEV)]
                same = [got[7], got[0], got[1], got[2]]
                other = got[3:7]
                core0, core1 = (same, other) if core == 0 else (other, same)
                g = core0[0] + core1[0]
                for r in range(1, N_CHIPS):
                    g = (g + core0[r]) + core1[r]
                _adam_update(g, w_ref, m_ref, v_ref, g_ref, d_ref, nm_ref, nv_ref)

    blk = pl.BlockSpec((ta, b), lambda i: (i, 0))
    return pl.pallas_call(
        body, name=name, grid=(a // ta,),
        in_specs=[pl.BlockSpec((N_DEV, ta, b), lambda i: (0, i, 0)), blk, blk, blk], out_specs=[blk] * 4,
        out_shape=[jax.ShapeDtypeStruct(w.shape, F32)] * 4, compiler_params=_params(("parallel",)),
    )(recv, w, m, v)


def _adamw_small(g, w, m, v):
    def body(g_in, w_ref, m_ref, v_ref, g_ref, d_ref, nm_ref, nv_ref):
        _adam_update(g_in[...], w_ref, m_ref, v_ref, g_ref, d_ref, nm_ref, nv_ref)

    blk = pl.BlockSpec(w.shape, lambda i: (0, 0))
    return pl.pallas_call(
        body, name="adamw_small", grid=(1,), in_specs=[blk] * 4, out_specs=[blk] * 4,
        out_shape=[jax.ShapeDtypeStruct(w.shape, F32)] * 4, compiler_params=_params(("arbitrary",)),
    )(g, w, m, v)


def _remote(src, dst, send_sem, recv_sem, device):
    return pltpu.make_async_remote_copy(src_ref=src, dst_ref=dst, send_sem=send_sem, recv_sem=recv_sem,
                                        device_id=device, device_id_type=MESH)


def _place():
    x, y, c = lax.axis_index("x"), lax.axis_index("y"), lax.axis_index("c")
    return x, y, c, 2 * x + y, [(1 - x, y), (x, 1 - y), (1 - x, 1 - y)]


HBM_REF = pl.BlockSpec(memory_space=pl.ANY)
HALF_ROWS_QUANTUM = 16


def _gather_sems(n):
    return [pltpu.SemaphoreType.DMA((3 * n,))] * 4 + [pltpu.SemaphoreType.DMA((n,))]


def _gather_copies(ins, outs, sems):
    send_s, recv_s, fsend_s, frecv_s, local_s = sems
    x, y, c, me, chips = _place()
    local, sends, steps = [], [], []
    for i, (src, dst) in enumerate(zip(ins, outs)):
        local.append(pltpu.make_async_copy(src, dst.at[me], local_s.at[i]))
        half = src.shape[0] // 2
        split = src.shape[0] % (2 * HALF_ROWS_QUANTUM) == 0
        if split:
            mine = pl.ds(pl.multiple_of(c * half, HALF_ROWS_QUANTUM), half)
            theirs = pl.ds(pl.multiple_of((1 - c) * half, HALF_ROWS_QUANTUM), half)
        for r, (px, py) in enumerate(chips):
            k, peer = 3 * i + r, 2 * px + py
            if split:
                sends.append(_remote(src.at[mine], dst.at[me, mine], send_s.at[k], recv_s.at[k], (px, py, c)))
                landed = dst.at[peer, mine]
                steps.append((_remote(src.at[mine], landed, send_s.at[k], recv_s.at[k], (px, py, c)),
                              _remote(landed, landed, fsend_s.at[k], frecv_s.at[k], (x, y, 1 - c)),
                              _remote(dst.at[peer, theirs], dst.at[peer, theirs], fsend_s.at[k], frecv_s.at[k], (x, y, 1 - c))))
            else:
                sends.append(_remote(src, dst.at[me], send_s.at[k], recv_s.at[k], (px, py, c)))
                steps.append((_remote(src, dst.at[peer], send_s.at[k], recv_s.at[k], (px, py, c)), None, None))
    return local, sends, steps


def _scatter_sems(n):
    return [pltpu.SemaphoreType.DMA((4 * n,))] * 2 + [pltpu.SemaphoreType.DMA((3 * n,))] * 2 + [pltpu.SemaphoreType.DMA((n,))]


def _scatter_copies(ins, outs, sems):
    send_s, recv_s, fsend_s, frecv_s, local_s = sems
    x, y, c, me, chips = _place()
    local, sends, steps = [], [], []
    for i, (src, dst) in enumerate(zip(ins, outs)):
        local.append(pltpu.make_async_copy(src.at[me], dst.at[N_DEV - 1], local_s.at[i]))
        for r, (px, py) in enumerate(chips):
            k = 4 * i + r
            cp = _remote(src.at[2 * px + py], dst.at[r], send_s.at[k], recv_s.at[k], (px, py, c))
            fwd = _remote(dst.at[r], dst.at[4 + r], fsend_s.at[3 * i + r], frecv_s.at[3 * i + r], (x, y, 1 - c))
            sends.append(cp)
            steps.append((cp, fwd, fwd))
        k = 4 * i + 3
        cp = _remote(src.at[me], dst.at[3], send_s.at[k], recv_s.at[k], (x, y, 1 - c))
        sends.append(cp)
        steps.append((cp, None, None))
    return local, sends, steps


def _exchange_start(plan):
    local, sends, _ = plan
    for cp in local + sends:
        cp.start()


def _exchange_pass_on(plan):
    for arrival, pass_on, _ in plan[2]:
        arrival.wait_recv()
        if pass_on is not None:
            pass_on.start()


def _exchange_finish(plan):
    local, sends, steps = plan
    for _, pass_on, passed in steps:
        if pass_on is not None:
            passed.wait_recv()
    for cp in sends:
        cp.wait_send()
    for _, pass_on, _ in steps:
        if pass_on is not None:
            pass_on.wait_send()
    for cp in local:
        cp.wait()


def _exchange_call(arrays, copies, sems, out_shapes, name):
    n = len(arrays)

    def body(*refs):
        plan = copies(refs[:n], refs[n:2 * n], refs[2 * n:])
        _exchange_start(plan)
        _exchange_pass_on(plan)
        _exchange_finish(plan)

    return pl.pallas_call(
        body, name=name, in_specs=[HBM_REF] * n, out_specs=[HBM_REF] * n, out_shape=out_shapes,
        scratch_shapes=sems(n), compiler_params=pltpu.CompilerParams(has_side_effects=True),
    )(*arrays)


def _gather_call(shards, name):
    shapes = [jax.ShapeDtypeStruct((N_CHIPS,) + a.shape, a.dtype) for a in shards]
    return _exchange_call(shards, _gather_copies, _gather_sems, shapes, name)


def _all_reduce_small(a):
    def body(in_ref, out_ref, slots, send_sems, recv_sems):
        x, y, c = lax.axis_index("x"), lax.axis_index("y"), lax.axis_index("c")
        me = 4 * x + 2 * y + c
        slots[0] = in_ref[...]
        sends = []
        for r in range(1, N_DEV):
            peer = (x ^ (r >> 2), y ^ ((r >> 1) & 1), c ^ (r & 1))
            sends.append(pltpu.make_async_remote_copy(src_ref=in_ref, dst_ref=slots.at[r], send_sem=send_sems.at[r],
                                                      recv_sem=recv_sems.at[r], device_id=peer, device_id_type=MESH))
        for cp in sends:
            cp.start()
        for cp in sends:
            cp.wait_recv()
        acc = slots[me]
        for dev in range(1, N_DEV):
            acc = acc + slots[dev ^ me]
        out_ref[...] = acc
        for cp in sends:
            cp.wait_send()

    return pl.pallas_call(
        body, name="small_all_reduce",
        in_specs=[pl.BlockSpec(memory_space=pltpu.VMEM)], out_specs=pl.BlockSpec(memory_space=pltpu.VMEM),
        out_shape=jax.ShapeDtypeStruct(a.shape, a.dtype),
        scratch_shapes=[pltpu.VMEM((N_DEV,) + a.shape, a.dtype), pltpu.SemaphoreType.DMA((N_DEV,)),
                        pltpu.SemaphoreType.DMA((N_DEV,))],
        compiler_params=pltpu.CompilerParams(has_side_effects=True),
    )(a)


SHARDED = ["w_in", "mla_w_q_up", "mla_w_kv_up", "w_out", "ffn_w_up", "ffn_w_down", "ple_w_gate", "ple_w_proj",
           "gdn_conv_w", "ffn_conv_w"]
SHARD_AXIS = {"w_in": 1, "mla_w_q_up": 1, "mla_w_kv_up": 1, "w_out": 0, "ffn_w_up": 1, "ffn_w_down": 0,
              "ple_w_gate": 0, "ple_w_proj": 1, "gdn_conv_w": 1, "ffn_conv_w": 1}
SMALL = ["gdn_a_log", "gdn_dt_bias", "gdn_norm_g", "mla_q_norm_g", "mla_kv_norm_g", "ln1_g", "ln1_b", "ffn_conv_b",
         "ple_b_gate", "ln2_g", "ln2_b"]
WEIGHTS = ["w_in", "gdn_conv_w", "gdn_a_log", "gdn_dt_bias", "gdn_norm_g", "mla_q_norm_g", "mla_w_q_up", "mla_kv_norm_g",
           "mla_w_kv_up", "w_out", "ln1_g", "ln1_b", "ffn_w_up", "ffn_conv_w", "ffn_conv_b", "ffn_w_down", "ple_w_gate",
           "ple_b_gate", "ple_w_proj", "ln2_g", "ln2_b"]
F32_ON_WIRE = ("gdn_conv_w", "ffn_conv_w")
GATHER_EARLY = ["w_in", "gdn_conv_w"]
GATHER_LATE = ["mla_w_q_up", "mla_w_kv_up", "w_out", "ffn_w_up", "ffn_conv_w", "ffn_w_down", "ple_w_gate", "ple_w_proj"]
SCATTER_EARLY = ["ffn_w_up", "ffn_conv_w", "ffn_w_down", "ple_w_gate", "ple_w_proj", "w_out"]
SCATTER_LATE = ["w_in", "gdn_conv_w", "mla_w_q_up", "mla_w_kv_up"]
PACK_COLS = 1024
PACK_ROW_TILE = 8


def _join_blocks(blocks, axis):
    n, a, b = blocks.shape
    if axis == 0:
        return blocks.reshape(n * a, b)
    return jnp.transpose(blocks, (1, 0, 2)).reshape(a, n * b)


def _split_blocks(full, axis):
    if axis == 0:
        return full.reshape(N_CHIPS, full.shape[0] // N_CHIPS, full.shape[1])
    a, nb = full.shape
    return jnp.transpose(full.reshape(a, N_CHIPS, nb // N_CHIPS), (1, 0, 2))


def _pack(arrays):
    flat = jnp.concatenate([a.reshape(-1) for a in arrays])
    quantum = PACK_COLS * PACK_ROW_TILE
    padded = -(-flat.shape[0] // quantum) * quantum
    return jnp.pad(flat, (0, padded - flat.shape[0])).reshape(-1, PACK_COLS)


def _unpack(packed, shapes):
    flat = packed.reshape(-1)
    out, off = [], 0
    for shp in shapes:
        n = int(np.prod(shp))
        out.append(flat[off:off + n].reshape(shp))
        off += n
    return out


def kernel(x, p, w_in, gdn_conv_w, gdn_a_log, gdn_dt_bias, gdn_norm_g, mla_q_norm_g, mla_w_q_up, mla_kv_norm_g, mla_w_kv_up, w_out, ln1_g, ln1_b, ffn_w_up, ffn_conv_w, ffn_conv_b, ffn_w_down, ple_w_gate, ple_b_gate, ple_w_proj, ln2_g, ln2_b, loss_target, m_w_in, m_gdn_conv_w, m_gdn_a_log, m_gdn_dt_bias, m_gdn_norm_g, m_mla_q_norm_g, m_mla_w_q_up, m_mla_kv_norm_g, m_mla_w_kv_up, m_w_out, m_ln1_g, m_ln1_b, m_ffn_w_up, m_ffn_conv_w, m_ffn_conv_b, m_ffn_w_down, m_ple_w_gate, m_ple_b_gate, m_ple_w_proj, m_ln2_g, m_ln2_b, v_w_in, v_gdn_conv_w, v_gdn_a_log, v_gdn_dt_bias, v_gdn_norm_g, v_mla_q_norm_g, v_mla_w_q_up, v_mla_kv_norm_g, v_mla_w_kv_up, v_w_out, v_ln1_g, v_ln1_b, v_ffn_w_up, v_ffn_conv_w, v_ffn_conv_b, v_ffn_w_down, v_ple_w_gate, v_ple_b_gate, v_ple_w_proj, v_ln2_g, v_ln2_b):
    given = dict(locals())
    wsh = {n: given[n][0] for n in WEIGHTS}
    msh = {n: given["m_" + n][0] for n in WEIGHTS}
    vsh = {n: given["v_" + n][0] for n in WEIGHTS}
    bl, s, _ = x.shape
    t = bl * s
    xt = x.reshape(t, D_MODEL)
    pt = p.reshape(t, PLE_DIM)
    target = loss_target.reshape(t, D_MODEL)

    wire = lambda n: wsh[n] if n in F32_ON_WIRE else wsh[n].astype(BF16)
    early = _gather_call([wire(n) for n in GATHER_EARLY], "weights_gather_early")
    full = {n: _join_blocks(g, SHARD_AXIS[n]) for n, g in zip(GATHER_EARLY, early)}
    late_shards = [wire(n) for n in GATHER_LATE]
    late_ride = (late_shards, _gather_copies, _gather_sems,
                 [jax.ShapeDtypeStruct((N_CHIPS,) + a.shape, a.dtype) for a in late_shards], 0.75)

    in_cols, q_cols = _w_in_cols(), _w_q_cols()
    w_in_p = _pad_cols(full["w_in"], in_cols)
    gconv = full["gdn_conv_w"]
    row = lambda a: a.reshape(1, -1)
    sc = jnp.zeros((8, 128), F32).at[0, :GDN_HEADS].set(wsh["gdn_a_log"]).at[1, :GDN_HEADS].set(wsh["gdn_dt_bias"])
    norm_g, qg, kvg = row(wsh["gdn_norm_g"]), row(wsh["mla_q_norm_g"]), row(wsh["mla_kv_norm_g"])
    g1, b1, g2, b2 = row(wsh["ln1_g"]), row(wsh["ln1_b"]), row(wsh["ln2_g"]), row(wsh["ln2_b"])
    fbias, bgate = row(wsh["ffn_conv_b"]), row(wsh["ple_b_gate"])

    inv = ROPE_THETA ** (-jnp.arange(0, MLA_ROPE, 2, dtype=F32) / MLA_ROPE)
    ang = jnp.arange(s, dtype=F32)[:, None] * inv[None, :]
    zero = jnp.zeros_like(ang)
    cos_t = jnp.concatenate([jnp.cos(ang), zero, jnp.cos(ang), zero], axis=1)
    sin_t = jnp.concatenate([-jnp.sin(ang), zero, jnp.sin(ang), zero], axis=1)

    proj = _matmul(xt, w_in_p, name="proj", tm=1024)
    cat, o_raw, states, qkvg, late = _gdn_fwd(proj, gconv, sc, norm_g, bl, s, late_ride)
    w_up = late[GATHER_LATE.index("ffn_w_up")]
    full.update({n: _join_blocks(g, SHARD_AXIS[n]) for n, g in zip(GATHER_LATE, late) if n != "ffn_w_up"})
    w_o, w_down = full["w_out"], full["ffn_w_down"]
    w_gate, w_proj, fconv = full["ple_w_gate"], full["ple_w_proj"], full["ffn_conv_w"]
    w_q_p, w_kv = _pad_cols(full["mla_w_q_up"], q_cols), full["mla_w_kv_up"]
    qf, kvf, kr = _mla_prep_fwd(proj, qg, kvg, w_q_p, w_kv, cos_t, sin_t, s, 512)
    cat, attn_o32, attn_lse = _attn_fwd(qf, kvf, kr, cat, bl, s, 512)
    wide = dict(tm=1024, tn=1024)
    r1, h1, h1b, xb = _mix_ln1_fwd(xt, cat, w_o, g1, b1, 512)
    u = _matmul(h1b, w_up, name="ffn_up", tm=1024, tn=1408, b_parts=N_CHIPS)
    act = _ffn_act_fwd(u, fconv, fbias, bl, s, 256)
    dr2, dr2b, dgpre, dpp, loss_acc, dbgate, dg2, db2 = _head(h1, h1b, act, pt, w_down, w_gate, w_proj, bgate, g2, b2, target, 256)

    dact = _matmul(dr2b, w_down, name="d_act", tb=True, tm=1024, tn=1408)
    long_k = dict(ta=True, tk=2048)
    d_w_down = _matmul(act, dr2b, name="dw_down", tm=1408, tn=1024, **long_k)
    du, dfcw_g, dfcw_u, dfcb_g, dfcb_u = _ffn_act_bwd(u, fconv, fbias, dact, bl, s, 256)
    dh1_a = _matmul(du, w_up, name="dh1_ffn", tb=True, tk=1408, a_halves=True, b_parts=N_CHIPS, **wide)
    d_w_up = _matmul(h1b, du, name="dw_up", tn=1408, b_parts=2, out_parts=N_CHIPS, out_dtype=BF16, **long_k)
    d_w_gate = _matmul(h1b, dgpre, name="dw_gate", **long_k, **wide)
    d_w_proj = _matmul(pt, dpp, name="dw_proj", ta=True, tn=1024)
    dr1, dr1b, dcat, dg1, db1 = _ln1_bwd(r1, dr2, dh1_a, dgpre, w_gate, w_o, g1, b1, 256)
    d_w_o = _matmul(cat, dr1b, name="dw_out", **long_k, **wide)

    gfull = {
        "ffn_w_down": d_w_down, "ple_w_gate": d_w_gate, "ple_w_proj": d_w_proj, "w_out": d_w_o,
        "ffn_conv_w": jnp.concatenate([jnp.sum(dfcw_g, 0), jnp.sum(dfcw_u, 0)], axis=1),
    }
    slabs = {n: _split_blocks(g, SHARD_AXIS[n]).astype(BF16) for n, g in gfull.items()}
    slabs["ffn_w_up"] = d_w_up
    early_slabs = [slabs[n] for n in SCATTER_EARLY]
    early_ride = (early_slabs, _scatter_copies, _scatter_sems,
                  [jax.ShapeDtypeStruct((N_DEV,) + a.shape[1:], a.dtype) for a in early_slabs], 0.7)
    dproj, dab, dcwq, dcwk, dcwv, dsc, dng, early_recv = _gdn_bwd(proj, gconv, sc, norm_g, o_raw, states, qkvg, dcat, bl, s,
                                                                  early_ride)
    received = dict(zip(SCATTER_EARLY, early_recv))
    dqf, dkvf, dkr = _attn_bwd(qf, kvf, kr, dcat, attn_o32, attn_lse, bl, s, 256)
    dproj, dqg, dkvg, d_w_q_p, d_w_kv = _mla_prep_bwd(proj, qg, kvg, w_q_p, w_kv, cos_t, sin_t, dqf, dkvf, dkr, dab, dproj, s, 512)
    d_w_in_p = _matmul(xb, dproj, name="dw_in", **long_k, **wide)

    gfull.update({
        "w_in": _unpad_cols(d_w_in_p, in_cols, D_IN),
        "mla_w_q_up": _unpad_cols(d_w_q_p, q_cols, MLA_HEADS * (MLA_NOPE + MLA_ROPE)),
        "mla_w_kv_up": d_w_kv,
        "gdn_conv_w": jnp.concatenate([jnp.sum(dcwq, 0), jnp.sum(dcwk, 0), jnp.sum(dcwv, 0)], axis=1),
    })
    slabs.update({n: _split_blocks(gfull[n], SHARD_AXIS[n]).astype(BF16) for n in SCATTER_LATE})
    late_slabs = [slabs[n] for n in SCATTER_LATE]
    late_scatter = (late_slabs, _scatter_copies, _scatter_sems,
                    [jax.ShapeDtypeStruct((N_DEV,) + a.shape[1:], a.dtype) for a in late_slabs], 0.85)
    grad_x, late_recv = _matmul(dproj, w_in_p, name="d_x", tb=True, add=dr1, add_scale=ALPHA, ride=late_scatter, **wide)
    received.update(zip(SCATTER_LATE, late_recv))
    dsc_sum = jnp.sum(dsc, axis=(0, 1))
    gsmall = {
        "gdn_a_log": dsc_sum[0, :GDN_HEADS], "gdn_dt_bias": dsc_sum[1, :GDN_HEADS],
        "gdn_norm_g": jnp.sum(dng[:, :, 0, :], axis=(0, 1)),
        "mla_q_norm_g": dqg[0], "mla_kv_norm_g": dkvg[0], "ln1_g": dg1[0], "ln1_b": db1[0],
        "ffn_conv_b": jnp.concatenate([jnp.sum(dfcb_g, 0), jnp.sum(dfcb_u, 0)], axis=1)[0],
        "ple_b_gate": dbgate[0], "ln2_g": dg2[0], "ln2_b": db2[0],
    }

    big = [{}, {}, {}, {}]
    for n in SHARDED:
        for kind, val in enumerate(_adamw_reduced(received[n], wsh[n], msh[n], vsh[n], "adamw_" + n)):
            big[kind][n] = val

    small_shapes = [wsh[n].shape for n in SMALL]
    gsum = _all_reduce_small(_pack([gsmall[n] for n in SMALL]))
    spacks = _adamw_small(gsum, _pack([wsh[n] for n in SMALL]), _pack([msh[n] for n in SMALL]), _pack([vsh[n] for n in SMALL]))
    small = [dict(zip(SMALL, _unpack(pk, small_shapes))) for pk in spacks]

    loss = lax.psum(loss_acc[0, 0], ("x", "y", "c"))
    outs = [loss, grad_x.reshape(x.shape)]
    for kind in range(4):
        for n in WEIGHTS:
            val = big[kind][n] if n in big[kind] else small[kind][n]
            outs.append(val[None])
    return tuple(outs)
```

```python
import functools
import math

import numpy as np
import jax
import jax.numpy as jnp
from jax import lax
from jax.experimental import pallas as pl
from jax.experimental.pallas import tpu as pltpu

F32 = jnp.float32
BF16 = jnp.bfloat16

D_MODEL = 1024
CHUNK = 64
PLE_DIM = 256
GDN_HEADS = 4
GDN_DK = 128
GDN_DV = 128
GDN_CONV = 4
MLA_HEADS = 4
MLA_NOPE = 128
MLA_ROPE = 64
MLA_V = 128
MLA_Q_LORA = 384
MLA_KV_LORA = 256
ROPE_THETA = 10000.0
D_FF = 2816
FFN_CONV = 3
DEPTH = 1
ALPHA = (2.0 * DEPTH) ** 0.25
NORM_EPS = 1e-6
GDN_QK = GDN_HEADS * GDN_DK
GDN_VW = GDN_HEADS * GDN_DV
D_IN = 2 * GDN_QK + 2 * GDN_VW + 2 * GDN_HEADS + MLA_Q_LORA + MLA_KV_LORA + MLA_ROPE
ATT_SCALE = (MLA_NOPE + MLA_ROPE) ** -0.5

ADAM_LR = 0.001
ADAM_B1 = 0.9
ADAM_B2 = 0.999
ADAM_EPS = 1e-08
ADAM_WD = 0.01
ADAM_STEP = 10

LANES = 128
VMEM_LIMIT = 60 * 1024 * 1024
GDN_FWD_GROUP = 16
GDN_BWD_GROUP = 16
N_CHIPS = 4
N_DEV = 8

P_WIDTH = 3072
P_MLA = 2048
MESH = pl.DeviceIdType.MESH


def _rope_slot(j):
    return j if j < MLA_ROPE // 2 else 64 + (j - MLA_ROPE // 2)


def _w_in_cols():
    idx = -np.ones((P_WIDTH,), np.int64)
    for h in range(GDN_HEADS):
        base = h * 512
        idx[base:base + 128] = np.arange(128) + h * GDN_DK
        idx[base + 128:base + 256] = np.arange(128) + GDN_QK + h * GDN_DK
        idx[base + 256:base + 384] = np.arange(128) + 2 * GDN_QK + h * GDN_DV
        idx[base + 384:base + 512] = np.arange(128) + 2 * GDN_QK + GDN_VW + h * GDN_DV
    o_a = 2 * GDN_QK + 2 * GDN_VW
    idx[P_MLA:P_MLA + 2 * GDN_HEADS] = np.arange(2 * GDN_HEADS) + o_a
    o_cq = o_a + 2 * GDN_HEADS
    idx[P_MLA + 128:P_MLA + 512] = np.arange(MLA_Q_LORA) + o_cq
    o_ckv = o_cq + MLA_Q_LORA
    idx[P_MLA + 512:P_MLA + 768] = np.arange(MLA_KV_LORA) + o_ckv
    o_kr = o_ckv + MLA_KV_LORA
    for j in range(MLA_ROPE):
        idx[P_MLA + 768 + _rope_slot(j)] = o_kr + j
    return idx


def _w_q_cols():
    idx = -np.ones((MLA_HEADS * 256,), np.int64)
    for h in range(MLA_HEADS):
        o = h * (MLA_NOPE + MLA_ROPE)
        idx[h * 256:h * 256 + 128] = np.arange(128) + o
        for j in range(MLA_ROPE):
            idx[h * 256 + 128 + _rope_slot(j)] = o + MLA_NOPE + j
    return idx


def _pad_cols(w, idx):
    safe = np.where(idx >= 0, idx, 0)
    return jnp.where(jnp.asarray(idx >= 0)[None, :], w[:, safe], 0.0)


def _unpad_cols(wp, idx, n):
    inv = np.zeros((n,), np.int64)
    inv[idx[idx >= 0]] = np.nonzero(idx >= 0)[0]
    return wp[:, inv]


def _dot(a, b, ca, cb, precision=None):
    if precision is None:
        a = a.astype(BF16)
        b = b.astype(BF16)
    return lax.dot_general(a, b, (((ca,), (cb,)), ((), ())), preferred_element_type=F32, precision=precision)


@jax.custom_vjp
def mm(a, b):
    return _dot(a, b, 1, 0)


@jax.custom_vjp
def mm_nt(a, b):
    return _dot(a, b, 1, 1)


@jax.custom_vjp
def mm_tn(a, b):
    return _dot(a, b, 0, 0)


mm.defvjp(lambda a, b: (mm(a, b), (a, b)), lambda r, g: (mm_nt(g, r[1]), mm_tn(r[0], g)))
mm_nt.defvjp(lambda a, b: (mm_nt(a, b), (a, b)), lambda r, g: (mm(g, r[1]), mm_tn(g, r[0])))
mm_tn.defvjp(lambda a, b: (mm_tn(a, b), (a, b)), lambda r, g: (mm_nt(r[1], g), mm(r[0], g)))

def _split(a):
    hi = a.astype(BF16)
    return hi, (a - hi.astype(F32)).astype(BF16)


def _dot3(a, b, ca, cb):
    a_hi, a_lo = _split(a)
    b_hi, b_lo = _split(b)
    return (_dot(a_hi, b_hi, ca, cb) + _dot(a_hi, b_lo, ca, cb)) + _dot(a_lo, b_hi, ca, cb)


def _shift_rows(x, s):
    return x if s == 0 else pltpu.roll(x, s % x.shape[0], 0)


def _row(w, j):
    tap = lax.broadcasted_iota(jnp.int32, w.shape, 0)
    return jnp.sum(jnp.where(tap == j, w, 0.0), axis=0, keepdims=True)


@jax.custom_vjp
def dwconv(x, w):
    k = w.shape[0]
    y = _row(w, k - 1) * x
    for j in range(k - 1):
        y = y + _row(w, j) * _shift_rows(x, k - 1 - j)
    return y


def _dwconv_fwd(x, w):
    return dwconv(x, w), (x, w)


def _dwconv_bwd(res, dy):
    x, w = res
    k = w.shape[0]
    dx = _row(w, k - 1) * dy
    tap = lax.broadcasted_iota(jnp.int32, w.shape, 0)
    dw = jnp.where(tap == k - 1, jnp.sum(dy * x, axis=0, keepdims=True), 0.0)
    for j in range(k - 1):
        dx = dx + _row(w, j) * _shift_rows(dy, -(k - 1 - j))
        dw = dw + jnp.where(tap == j, jnp.sum(dy * _shift_rows(x, k - 1 - j), axis=0, keepdims=True), 0.0)
    return dx, dw


dwconv.defvjp(_dwconv_fwd, _dwconv_bwd)


@jax.custom_vjp
def rope128(x, cos, sin):
    return x * cos + pltpu.roll(x, 64, 1) * sin


rope128.defvjp(lambda x, c, s: (rope128(x, c, s), (c, s)),
               lambda r, g: (g * r[0] + pltpu.roll(g * r[1], 64, 1), jnp.zeros_like(r[0]), jnp.zeros_like(r[1])))


def _silu(x):
    return x * jax.nn.sigmoid(x)


def _softplus(x):
    return jnp.maximum(x, 0.0) + jnp.log(1.0 + jnp.exp(-jnp.abs(x)))


def _rmsnorm(x, g):
    return x * lax.rsqrt(jnp.mean(x * x, axis=-1, keepdims=True) + NORM_EPS) * g


def _layernorm(x, g, b):
    mu = jnp.mean(x, axis=-1, keepdims=True)
    xc = x - mu
    var = jnp.mean(xc * xc, axis=-1, keepdims=True)
    return xc * lax.rsqrt(var + NORM_EPS) * g + b


def _pick_lane(row, lane):
    idx = lax.broadcasted_iota(jnp.int32, row.shape, 1)
    return jnp.sum(jnp.where(idx == lane, row, 0.0), axis=1, keepdims=True)


def _gdn_q(pq, cw):
    h = _silu(dwconv(pq, cw))
    return h * lax.rsqrt(jnp.sum(h * h, axis=-1, keepdims=True) + NORM_EPS) * (GDN_DK ** -0.5)


def _gdn_k(pk, cw):
    h = _silu(dwconv(pk, cw))
    return h * lax.rsqrt(jnp.sum(h * h, axis=-1, keepdims=True) + NORM_EPS)


def _gdn_v(pv, cw):
    return _silu(dwconv(pv, cw))


def _gdn_gate(ab, sc, head):
    a = _pick_lane(ab, head)
    b = _pick_lane(ab, GDN_HEADS + head)
    a_log = _pick_lane(_row(sc, 0), head)
    dt_bias = _pick_lane(_row(sc, 1), head)
    beta = jax.nn.sigmoid(b)
    g = -jnp.exp(a_log) * _softplus(a + dt_bias)
    return _two_lanes(g, beta)


def _two_lanes(c0, c1):
    lane = lax.broadcasted_iota(jnp.int32, (c0.shape[0], LANES), 1)
    return jnp.where(lane == 0, c0, jnp.where(lane == 1, c1, 0.0))


def _run(levels):
    try:
        while True:
            next(levels)
    except StopIteration as stop:
        return stop.value


def _inverse_levels(lows):
    n = lows[0].shape[0]
    ii = lax.broadcasted_iota(jnp.int32, (n, n), 0)
    jj = lax.broadcasted_iota(jnp.int32, (n, n), 1)
    eye = jnp.where(ii == jj, 1.0, 0.0)
    invs = [eye - low for low in lows]
    powers = [_dot3(low, low, 1, 0) for low in lows]
    yield
    k = 2
    while k < n:
        invs = [inv + _dot3(inv, p, 1, 0) for inv, p in zip(invs, powers)]
        yield
        k *= 2
        if k < n:
            powers = [_dot3(p, p, 1, 0) for p in powers]
            yield
    return invs


def _inverse_group(lows):
    return _run(_inverse_levels(lows))


@jax.custom_vjp
def solve_group(lows, rhss):
    return [_dot3(inv, rhs, 1, 0) for inv, rhs in zip(_inverse_group(lows), rhss)]


def _solve_whole(lows, rhss):
    return solve_group(lows, rhss)
    yield


def _solve_levels(lows, rhss):
    invs = yield from _inverse_levels(lows)
    return [_dot3(inv, rhs, 1, 0) for inv, rhs in zip(invs, rhss)]


def _solve_group_fwd(lows, rhss):
    invs = _inverse_group(lows)
    xs = [_dot3(inv, rhs, 1, 0) for inv, rhs in zip(invs, rhss)]
    return xs, (invs, xs)


def _solve_group_bwd(res, dxs):
    invs, xs = res
    n = invs[0].shape[0]
    strict = lax.broadcasted_iota(jnp.int32, (n, n), 0) > lax.broadcasted_iota(jnp.int32, (n, n), 1)
    drhss = [_dot3(inv, dx, 0, 0) for inv, dx in zip(invs, dxs)]
    dlows = [jnp.where(strict, -_dot3(drhs, x, 1, 1), 0.0) for drhs, x in zip(drhss, xs)]
    return dlows, drhss


solve_group.defvjp(_solve_group_fwd, _solve_group_bwd)


def _gdn_local_group(qs, ks, vs, gbs):
    return _run(_gdn_local_levels(qs, ks, vs, gbs, _solve_whole))


def _gdn_local_levels(qs, ks, vs, gbs, solve):
    c = qs[0].shape[0]
    ii = lax.broadcasted_iota(jnp.int32, (c, c), 0)
    jj = lax.broadcasted_iota(jnp.int32, (c, c), 1)
    incl = ii >= jj
    gs = [_pick_lane(gb, 0) for gb in gbs]
    betas = [_pick_lane(gb, 1) for gb in gbs]
    g_rows = [jnp.sum(jnp.where(ii == jj, g, 0.0), axis=0, keepdims=True) for g in gs]
    gc_cols = [jnp.sum(jnp.where(incl, g_row, 0.0), axis=1, keepdims=True) for g_row in g_rows]
    gc_rows = [jnp.sum(jnp.where(jj >= ii, g, 0.0), axis=0, keepdims=True) for g in gs]
    decays = [jnp.where(incl, jnp.exp(jnp.where(incl, gc - gr, 0.0)), 0.0) for gc, gr in zip(gc_cols, gc_rows)]
    yield
    kbs = [k * beta for k, beta in zip(ks, betas)]
    lows = [jnp.where(ii > jj, mm_nt(kb, k) * decay, 0.0) for kb, k, decay in zip(kbs, ks, decays)]
    egs = [jnp.exp(gc) for gc in gc_cols]
    yield
    wus = yield from solve(lows, [jnp.concatenate([kb * eg, v * beta], axis=1) for kb, eg, v, beta in zip(kbs, egs, vs, betas)])
    yield
    qks = [mm_nt(q, k) * decay for q, k, decay in zip(qs, ks, decays)]
    g_lasts = [jnp.sum(g_row, axis=1, keepdims=True) for g_row in g_rows]
    kds = [k * jnp.exp(gl - gc) for k, gl, gc in zip(ks, g_lasts, gc_cols)]
    yield
    ws, us = [wu[:, :GDN_DK] for wu in wus], [wu[:, GDN_DK:] for wu in wus]
    q_effs = [q * eg - mm(qk, w) for q, eg, qk, w in zip(qs, egs, qks, ws)]
    yield
    o_locals = [mm(qk, u) for qk, u in zip(qks, us)]
    yield
    mixes = [mm_tn(kd, w) for kd, w in zip(kds, ws)]
    yield
    adds = [mm_tn(kd, u) for kd, u in zip(kds, us)]
    return [(q_eff, o_loc, mix, add, jnp.exp(gl))
            for q_eff, o_loc, mix, add, gl in zip(q_effs, o_locals, mixes, adds, g_lasts)]


def _gdn_state_step(q_eff, o_local, mix, add, eg_last, state):
    return mm(q_eff, state) + o_local, state * eg_last - mm(mix, state) + add


def _gdn_post(o, z, norm_g):
    return _rmsnorm(o, norm_g) * _silu(z)


MASKED = -1e30


def _scores(q, kn, kr, q0, k0):
    s = (mm_nt(q[:, :128], kn) + mm_nt(q[:, 128:], kr)) * ATT_SCALE
    if k0 + kn.shape[0] <= q0:
        return s
    qpos = q0 + lax.broadcasted_iota(jnp.int32, s.shape, 0)
    kpos = k0 + lax.broadcasted_iota(jnp.int32, s.shape, 1)
    shift = int(math.log2(CHUNK))
    return jnp.where((kpos >> shift) <= (qpos >> shift), s, MASKED)


def _softmax_times(s, v):
    top = jnp.max(s, axis=-1, keepdims=True)
    p = jnp.exp(s - top)
    norm = jnp.sum(p, axis=-1, keepdims=True)
    return mm(p / norm, v), top + jnp.log(norm)


def _mla_prep(pm, qg, kvg, wq, wkv, cos, sin):
    cq = pm[:, 128:512]
    ckv = pm[:, 512:768]
    qf = mm(_rmsnorm(cq, qg), wq)
    parts = []
    for h in range(MLA_HEADS):
        parts.append(qf[:, h * 256:h * 256 + 128])
        parts.append(rope128(qf[:, h * 256 + 128:h * 256 + 256], cos, sin))
    kvf = mm(_rmsnorm(ckv, kvg), wkv)
    return jnp.concatenate(parts, axis=1), kvf, rope128(pm[:, 768:896], cos, sin)


def _ffn_act(ug, uu, wg, wu, bg, bu):
    return _silu(dwconv(ug, wg) + bg) * (dwconv(uu, wu) + bu)


def _head_loss(h1, ffn, gpre, pp, bgate, g2, b2, target):
    gate = jax.nn.sigmoid(gpre + bgate)
    h2 = _layernorm(ALPHA * h1 + ffn + gate * pp, g2, b2)
    err = h2 - target
    return 0.5 * jnp.sum(jnp.sum(err * err, axis=1, keepdims=True), axis=0, keepdims=True) / D_MODEL


ROW_TILE_VREGS = 32
CONV_HALO = 8


def _rows_per_tile(n_rows, cols):
    tile = min(n_rows, ROW_TILE_VREGS * 8 * LANES // cols)
    assert n_rows % tile == 0 and tile % CONV_HALO == 0, (n_rows, cols)
    return tile


def _tile_inputs(loads, t0, first, halo, tile):
    if halo == 0:
        return [ld(pl.ds(t0, tile)) for ld in loads]
    if first:
        xs = [ld(pl.ds(0, tile)) for ld in loads]
        return [jnp.concatenate([jnp.zeros((halo, x.shape[1]), x.dtype), x], axis=0) for x in xs]
    return [ld(pl.ds(pl.multiple_of(t0 - halo, CONV_HALO), tile + halo)) for ld in loads]


def _rows_apply(fn, loads, consts, store, n_rows, cols, halo):
    tile = _rows_per_tile(n_rows, cols)

    def one(t0, first):
        y = fn(*_tile_inputs(loads, t0, first, halo, tile), *consts)
        store(pl.ds(t0, tile), y[halo:] if halo else y)

    one(0, True)

    def step(i, carry):
        one(pl.multiple_of(i * tile, tile), False)
        return carry

    lax.fori_loop(1, n_rows // tile, step, 0)


def _rows_vjp(fn, loads, consts, load_dy, stores, n_rows, cols, halo):
    tile = _rows_per_tile(n_rows, cols)

    def one(t0, first, dconsts):
        xs = _tile_inputs(loads, t0, first, halo, tile)
        _, vjp = jax.vjp(lambda *a: fn(*a)[halo:] if halo else fn(*a), *xs, *consts)
        grads = vjp(load_dy(pl.ds(t0, tile)))
        for st, dx in zip(stores, grads[:len(xs)]):
            st(pl.ds(t0, tile), dx[halo:] if halo else dx, False)
            if halo and not first:
                st(pl.ds(pl.multiple_of(t0 - halo, CONV_HALO), halo), dx[:halo], True)
        return tuple(a + b for a, b in zip(dconsts, grads[len(xs):]))

    dconsts = one(0, True, tuple(jnp.zeros_like(c) for c in consts))
    return lax.fori_loop(1, n_rows // tile, lambda i, dc: one(pl.multiple_of(i * tile, tile), False, dc), dconsts)


def _params(sem):
    return pltpu.CompilerParams(dimension_semantics=sem, vmem_limit_bytes=VMEM_LIMIT)


def _matmul(a, b, *, name, ta=False, tb=False, tm=512, tn=512, tk=1024, add=None, add_scale=1.0,
            a_halves=False, b_parts=0, out_parts=0, ride=None, out_dtype=F32):
    assert not (a_halves and ta)
    a_shape = (a.shape[1], 2 * a.shape[2]) if a_halves else a.shape
    b_shape = (b.shape[1], b_parts * b.shape[2]) if b_parts else b.shape
    (k_dim, m) = a_shape if ta else a_shape[::-1]
    (n, k2) = b_shape if tb else b_shape[::-1]
    assert k_dim == k2, (a.shape, b.shape)
    tm, tn, tk = min(tm, m), min(tn, n), min(tk, k_dim)
    assert m % tm == 0 and n % tn == 0 and k_dim % tk == 0, (name, m, n, k_dim, tm, tn, tk)
    nk = k_dim // tk
    ca, cb = (0 if ta else 1), (1 if tb else 0)

    def body(*refs):
        if add is None:
            a_ref, b_ref, o_ref, acc = refs
        else:
            a_ref, b_ref, c_ref, o_ref, acc = refs
        kk = pl.program_id(2)

        @pl.when(kk == 0)
        def _():
            acc[...] = jnp.zeros_like(acc)

        acc[...] += _dot(a_ref[...], b_ref[...], ca, cb)

        @pl.when(kk == nk - 1)
        def _():
            r = acc[...]
            if add is not None:
                r = r + add_scale * c_ref[...]
            o_ref[...] = r.astype(out_dtype)

    def per_part(total, parts, tile):
        per = total // parts // tile
        assert per * tile * parts == total, (name, total, parts, tile)
        return per

    spec = pl.BlockSpec
    a_spec = spec((tk, tm), lambda i, j, k: (k, i)) if ta else spec((tm, tk), lambda i, j, k: (i, k))
    b_spec = spec((tn, tk), lambda i, j, k: (j, k)) if tb else spec((tk, tn), lambda i, j, k: (k, j))
    if a_halves:
        kh = per_part(k_dim, 2, tk)
        a_spec = spec((None, tm, tk), lambda i, j, k: (k // kh, i, k % kh))
    if b_parts and tb:
        kp = per_part(k_dim, b_parts, tk)
        b_spec = spec((None, tn, tk), lambda i, j, k: (k // kp, j, k % kp))
    elif b_parts:
        np_ = per_part(n, b_parts, tn)
        b_spec = spec((None, tk, tn), lambda i, j, k: (j // np_, k, j % np_))
    out_spec, out_shape = spec((tm, tn), lambda i, j, k: (i, j)), (m, n)
    if out_parts:
        op = per_part(n, out_parts, tn)
        out_spec, out_shape = spec((None, tm, tn), lambda i, j, k: (j // op, i, j % op)), (out_parts, m, n // out_parts)
    in_specs = [a_spec, b_spec]
    args = [a, b]
    if add is not None:
        in_specs.append(pl.BlockSpec((tm, tn), lambda i, j, k: (i, j)))
        args.append(add)
    grid = (m // tm, n // tn, nk)
    r_in, r_out, r_shapes, r_sems, r_args = _ride_specs(ride)
    outs = pl.pallas_call(
        _riding(body, len(args), 1, 1, ride, grid), name=name, grid=grid,
        in_specs=in_specs + r_in, out_specs=[out_spec] + r_out,
        out_shape=[jax.ShapeDtypeStruct(out_shape, out_dtype)] + r_shapes,
        scratch_shapes=[pltpu.VMEM((tm, tn), F32)] + r_sems,
        compiler_params=_params(("parallel", "parallel", "arbitrary") if ride is None else ("arbitrary",) * 3),
    )(*args, *r_args)
    return outs[0] if ride is None else (outs[0], list(outs[1:]))


def _riding(core, n_in, n_out, n_scratch, ride, grid):
    if ride is None:
        return core
    copies, nr = ride[1], len(ride[0])
    steps = int(np.prod(grid))
    pass_step = min(max(int(steps * ride[4]), 1), steps - 2)
    assert steps >= 3, grid

    def body(*refs):
        cuts = np.cumsum([0, n_in, nr, n_out, nr, n_scratch])
        ins, rin, outs, rout, scratch = (refs[a:b] for a, b in zip(cuts[:-1], cuts[1:]))
        sems = refs[cuts[-1]:]
        step = 0
        for axis, size in enumerate(grid):
            step = step * size + pl.program_id(axis)

        @pl.when(step == 0)
        def _():
            _exchange_start(copies(rin, rout, sems))

        @pl.when(step == pass_step)
        def _():
            _exchange_pass_on(copies(rin, rout, sems))

        core(*ins, *outs, *scratch)

        @pl.when(step == steps - 1)
        def _():
            _exchange_finish(copies(rin, rout, sems))

    return body


def _ride_specs(ride):
    if ride is None:
        return [], [], [], [], []
    arrays, _, sems, shapes, _ = ride
    return [HBM_REF] * len(arrays), [HBM_REF] * len(arrays), list(shapes), sems(len(arrays)), list(arrays)


def _gdn_fwd(proj, conv_w, sc, norm_g, bl, s, ride=None):
    nc = s // CHUNK

    def core(ph_ref, ab_ref, cwq_ref, cwk_ref, cwv_ref, sc_ref, ng_ref, cat_ref, o_ref, st_ref, q_s, k_s, v_s, gb_s):
        def into(ref):
            def store(rows, value):
                ref[rows, :] = value.astype(ref.dtype)
            return store

        for fn, col, cw_ref, val_s in [(_gdn_q, 0, cwq_ref, q_s), (_gdn_k, 128, cwk_ref, k_s), (_gdn_v, 256, cwv_ref, v_s)]:
            _rows_apply(fn, [lambda r, col=col: ph_ref[r, col:col + 128]], [cw_ref[...]], into(val_s), s, LANES, CONV_HALO)
        _rows_apply(functools.partial(_gdn_gate, head=pl.program_id(1)), [lambda r: ab_ref[r, :]], [sc_ref[...]], into(gb_s), s, LANES, 0)

        group = math.gcd(nc, GDN_FWD_GROUP)

        def rows_of(n):
            return slice(n * CHUNK, (n + 1) * CHUNK)

        def levels_of(g):
            rows = [rows_of(g * group + j) for j in range(group)]
            return _gdn_local_levels([q_s[r, :] for r in rows], [k_s[r, :] for r in rows], [v_s[r, :] for r in rows],
                                     [gb_s[r, :] for r in rows], _solve_levels)

        local, state = _run(levels_of(0)), jnp.zeros((GDN_DK, GDN_DV), F32)
        for g in range(nc // group):
            ahead = levels_of(g + 1) if g + 1 < nc // group else None
            following = None
            for j, loc in enumerate(local):
                n = g * group + j
                st_ref[n] = state
                o_ref[rows_of(n), :], state = _gdn_state_step(*loc, state)
                if ahead is not None and following is None:
                    try:
                        next(ahead)
                    except StopIteration as stop:
                        following = stop.value
            if ahead is not None and following is None:
                following = _run(ahead)
            local = following
        _rows_apply(_gdn_post, [lambda r: o_ref[r, :], lambda r: ph_ref[r, 384:512]], [ng_ref[...]], into(cat_ref), s, LANES, 0)

    t = bl * s
    r_in, r_out, r_shapes, r_sems, r_args = _ride_specs(ride)
    outs = pl.pallas_call(
        _riding(core, 7, 7, 0, ride, (bl, GDN_HEADS)), name="gdn_fwd", grid=(bl, GDN_HEADS),
        in_specs=[
            pl.BlockSpec((s, 512), lambda b, h: (b, h)),
            pl.BlockSpec((s, 128), lambda b, h: (b, P_MLA // 128)),
            pl.BlockSpec((GDN_CONV, 128), lambda b, h: (0, h)),
            pl.BlockSpec((GDN_CONV, 128), lambda b, h: (0, GDN_HEADS + h)),
            pl.BlockSpec((GDN_CONV, 128), lambda b, h: (0, 2 * GDN_HEADS + h)),
            pl.BlockSpec((8, 128), lambda b, h: (0, 0)),
            pl.BlockSpec((1, 128), lambda b, h: (0, 0)),
        ] + r_in,
        out_specs=[
            pl.BlockSpec((s, 128), lambda b, h: (b, h)),
            pl.BlockSpec((s, 128), lambda b, h: (b, h)),
            pl.BlockSpec((None, None, nc, GDN_DK, GDN_DV), lambda b, h: (b, h, 0, 0, 0)),
        ] + [pl.BlockSpec((s, 128), lambda b, h: (b, h))] * 4 + r_out,
        out_shape=[
            jax.ShapeDtypeStruct((t, 2 * GDN_VW), BF16),
            jax.ShapeDtypeStruct((t, GDN_VW), F32),
            jax.ShapeDtypeStruct((bl, GDN_HEADS, nc, GDN_DK, GDN_DV), F32),
        ] + [jax.ShapeDtypeStruct((t, GDN_VW), F32)] * 4 + r_shapes,
        scratch_shapes=r_sems,
        compiler_params=_params(("arbitrary", "arbitrary")),
    )(proj, proj, conv_w, conv_w, conv_w, sc, norm_g, *r_args)
    return outs[0], outs[1], outs[2], tuple(outs[3:7]), list(outs[7:])


def _gdn_bwd(proj, conv_w, sc, norm_g, o_raw, states, qkvg, dcat, bl, s, ride=None):
    nc = s // CHUNK

    def core(ph_ref, ab_ref, cwq_ref, cwk_ref, cwv_ref, sc_ref, ng_ref, o_ref, st_ref, dc_ref, q_in, k_in, v_in, gb_in,
             dph_ref, dab_ref, dcwq_ref, dcwk_ref, dcwv_ref, dsc_ref, dng_ref, q_s, k_s, v_s, gb_s, do_s):
        head = pl.program_id(1)
        gate = functools.partial(_gdn_gate, head=head)
        paths = [(_gdn_q, 0, cwq_ref, q_s, dcwq_ref), (_gdn_k, 128, cwk_ref, k_s, dcwk_ref), (_gdn_v, 256, cwv_ref, v_s, dcwv_ref)]
        def into(ref, cols=slice(None)):
            def store(rows, value, add=False):
                if add:
                    ref[rows, cols] += value.astype(ref.dtype)
                else:
                    ref[rows, cols] = value.astype(ref.dtype)
            return store

        (dng,) = _rows_vjp(_gdn_post, [lambda r: o_ref[r, :], lambda r: ph_ref[r, 384:512]], [ng_ref[...]],
                           lambda r: dc_ref[r, :], [into(do_s), into(dph_ref, slice(384, 512))], s, LANES, 0)
        dng_ref[...] = jnp.broadcast_to(dng, dng_ref.shape)

        group = math.gcd(nc, GDN_BWD_GROUP)

        def chunks(i, dstate):
            ns = [nc - 1 - (i * group + j) for j in range(group)]
            rows = [pl.ds(pl.multiple_of(n * CHUNK, CHUNK), CHUNK) for n in ns]
            local, local_vjp = jax.vjp(_gdn_local_group, [q_in[r, :] for r in rows], [k_in[r, :] for r in rows],
                                       [v_in[r, :] for r in rows], [gb_in[r, :] for r in rows])
            d_os = [do_s[r, :] for r in rows]
            dlocal = []
            for n, loc, d_o in zip(ns, local, d_os):
                _, step_vjp = jax.vjp(_gdn_state_step, *loc, st_ref[n])
                *dloc, dstate = step_vjp((d_o, dstate))
                dlocal.append(tuple(dloc))
            dqs, dks, dvs, dgbs = local_vjp(dlocal)
            for r, dq, dk, dv, dgb in zip(rows, dqs, dks, dvs, dgbs):
                q_s[r, :], k_s[r, :], v_s[r, :], gb_s[r, :] = dq, dk, dv, dgb
            return dstate

        lax.fori_loop(0, nc // group, chunks, jnp.zeros((GDN_DK, GDN_DV), F32))
        for fn, col, cw_ref, val_s, dcw_ref in paths:
            (dcw_ref[...],) = _rows_vjp(fn, [lambda r, col=col: ph_ref[r, col:col + 128]], [cw_ref[...]],
                                        lambda r, val_s=val_s: val_s[r, :], [into(val_s)], s, LANES, CONV_HALO)
            dph_ref[:, col:col + 128] = val_s[...].astype(BF16)

        @pl.when(head == 0)
        def _():
            dab_ref[...] = jnp.zeros_like(dab_ref)

        def add_dab(rows, value, add=False):
            dab_ref[rows, :] += value

        (dsc_ref[...],) = _rows_vjp(gate, [lambda r: ab_ref[r, :]], [sc_ref[...]], lambda r: gb_s[r, :], [add_dab], s, LANES, 0)

    t = bl * s
    cw_out = pl.BlockSpec((None, GDN_CONV, 128), lambda b, h: (b, 0, h))
    part = pl.BlockSpec((None, None, 8, 128), lambda b, h: (b, h, 0, 0))
    r_in, r_out, r_shapes, r_sems, r_args = _ride_specs(ride)
    outs = pl.pallas_call(
        _riding(core, 14, 7, 5, ride, (bl, GDN_HEADS)), name="gdn_bwd", grid=(bl, GDN_HEADS),
        in_specs=[
            pl.BlockSpec((s, 512), lambda b, h: (b, h)),
            pl.BlockSpec((s, 128), lambda b, h: (b, P_MLA // 128)),
            pl.BlockSpec((GDN_CONV, 128), lambda b, h: (0, h)),
            pl.BlockSpec((GDN_CONV, 128), lambda b, h: (0, GDN_HEADS + h)),
            pl.BlockSpec((GDN_CONV, 128), lambda b, h: (0, 2 * GDN_HEADS + h)),
            pl.BlockSpec((8, 128), lambda b, h: (0, 0)),
            pl.BlockSpec((1, 128), lambda b, h: (0, 0)),
            pl.BlockSpec((s, 128), lambda b, h: (b, h)),
            pl.BlockSpec((None, None, nc, GDN_DK, GDN_DV), lambda b, h: (b, h, 0, 0, 0)),
        ] + [pl.BlockSpec((s, 128), lambda b, h: (b, h))] * 5 + r_in,
        out_specs=[
            pl.BlockSpec((s, 512), lambda b, h: (b, h)),
            pl.BlockSpec((s, 128), lambda b, h: (b, 0)),
            cw_out, cw_out, cw_out, part, part,
        ] + r_out,
        out_shape=[
            jax.ShapeDtypeStruct((t, P_WIDTH), BF16),
            jax.ShapeDtypeStruct((t, 128), F32),
            jax.ShapeDtypeStruct((bl, GDN_CONV, 512), F32),
            jax.ShapeDtypeStruct((bl, GDN_CONV, 512), F32),
            jax.ShapeDtypeStruct((bl, GDN_CONV, 512), F32),
            jax.ShapeDtypeStruct((bl, GDN_HEADS, 8, 128), F32),
            jax.ShapeDtypeStruct((bl, GDN_HEADS, 8, 128), F32),
        ] + r_shapes,
        scratch_shapes=[pltpu.VMEM((s, 128), F32)] * 5 + r_sems,
        compiler_params=_params(("arbitrary", "arbitrary")),
    )(proj, proj, conv_w, conv_w, conv_w, sc, norm_g, o_raw, states, dcat, *qkvg, *r_args)
    return tuple(outs[:7]) + (list(outs[7:]),)


def _mla_prep_fwd(proj, qg, kvg, wq, wkv, cos, sin, s, tm):
    t = proj.shape[0]
    tm = min(tm, s)
    nps = s // tm
    const = lambda shape: pl.BlockSpec(shape, lambda i: (0, 0))

    def body(pm_ref, qg_ref, kvg_ref, wq_ref, wkv_ref, cos_ref, sin_ref, qf_ref, kvf_ref, kr_ref):
        qf, kvf, kr = _mla_prep(pm_ref[...], qg_ref[...], kvg_ref[...], wq_ref[...], wkv_ref[...], cos_ref[...], sin_ref[...])
        qf_ref[...], kvf_ref[...], kr_ref[...] = qf.astype(BF16), kvf.astype(BF16), kr.astype(BF16)

    return pl.pallas_call(
        body, name="mla_prep_fwd", grid=(t // tm,),
        in_specs=[
            pl.BlockSpec((tm, 1024), lambda i: (i, P_MLA // 1024)),
            const((1, MLA_Q_LORA)), const((1, MLA_KV_LORA)), const(wq.shape), const(wkv.shape),
            pl.BlockSpec((tm, 128), lambda i: (i % nps, 0)), pl.BlockSpec((tm, 128), lambda i: (i % nps, 0)),
        ],
        out_specs=[pl.BlockSpec((tm, 1024), lambda i: (i, 0)), pl.BlockSpec((tm, 1024), lambda i: (i, 0)),
                   pl.BlockSpec((tm, 128), lambda i: (i, 0))],
        out_shape=[jax.ShapeDtypeStruct((t, 1024), BF16), jax.ShapeDtypeStruct((t, 1024), BF16),
                   jax.ShapeDtypeStruct((t, 128), BF16)],
        compiler_params=_params(("parallel",)),
    )(proj, qg, kvg, wq, wkv, cos, sin)


def _mla_prep_bwd(proj, qg, kvg, wq, wkv, cos, sin, dqf, dkvf, dkr, dab, dproj, s, tm):
    t = proj.shape[0]
    tm = min(tm, s)
    nps = s // tm
    const = lambda shape: pl.BlockSpec(shape, lambda i: (0, 0))

    def body(pm_ref, qg_ref, kvg_ref, wq_ref, wkv_ref, cos_ref, sin_ref, dqf_ref, dkvf_ref, dkr_ref, dab_ref, dp_in,
             dp_ref, dqg_ref, dkvg_ref, dwq_ref, dwkv_ref):
        del dp_in
        fn = lambda pm, qg_, kvg_, wq_, wkv_: _mla_prep(pm, qg_, kvg_, wq_, wkv_, cos_ref[...], sin_ref[...])
        _, vjp = jax.vjp(fn, pm_ref[...], qg_ref[...], kvg_ref[...], wq_ref[...].astype(F32), wkv_ref[...].astype(F32))
        dpm, dqg, dkvg, dwq, dwkv = vjp((dqf_ref[...], dkvf_ref[...], dkr_ref[...]))
        dp_ref[...] = jnp.concatenate([dab_ref[...], dpm[:, 128:]], axis=1).astype(BF16)

        @pl.when(pl.program_id(0) == 0)
        def _():
            dqg_ref[...] = jnp.zeros_like(dqg_ref)
            dkvg_ref[...] = jnp.zeros_like(dkvg_ref)
            dwq_ref[...] = jnp.zeros_like(dwq_ref)
            dwkv_ref[...] = jnp.zeros_like(dwkv_ref)

        dqg_ref[...] += dqg
        dkvg_ref[...] += dkvg
        dwq_ref[...] += dwq
        dwkv_ref[...] += dwkv

    rows = lambda w: pl.BlockSpec((tm, w), lambda i: (i, 0))
    return pl.pallas_call(
        body, name="mla_prep_bwd", grid=(t // tm,),
        in_specs=[
            pl.BlockSpec((tm, 1024), lambda i: (i, P_MLA // 1024)),
            const((1, MLA_Q_LORA)), const((1, MLA_KV_LORA)), const(wq.shape), const(wkv.shape),
            pl.BlockSpec((tm, 128), lambda i: (i % nps, 0)), pl.BlockSpec((tm, 128), lambda i: (i % nps, 0)),
            rows(1024), rows(1024), rows(128), rows(128),
            pl.BlockSpec(memory_space=pl.ANY),
        ],
        out_specs=[pl.BlockSpec((tm, 1024), lambda i: (i, P_MLA // 1024)),
                   const((1, MLA_Q_LORA)), const((1, MLA_KV_LORA)), const(wq.shape), const(wkv.shape)],
        out_shape=[jax.ShapeDtypeStruct(dproj.shape, dproj.dtype),
                   jax.ShapeDtypeStruct((1, MLA_Q_LORA), F32), jax.ShapeDtypeStruct((1, MLA_KV_LORA), F32),
                   jax.ShapeDtypeStruct(wq.shape, F32), jax.ShapeDtypeStruct(wkv.shape, F32)],
        input_output_aliases={11: 0},
        compiler_params=_params(("arbitrary",)),
    )(proj, qg, kvg, wq, wkv, cos, sin, dqf, dkvf, dkr, dab, dproj)


def _attn_fwd(qf, kvf, kr, cat, bl, s, tq):
    tq = min(tq, s)
    nq = s // tq

    def body(q_ref, kv_ref, kr_ref, cat_in, o_ref, o32_ref, lse_ref):
        del cat_in

        def scores_of(i):
            keys = slice(0, (i + 1) * tq)
            return _scores(q_ref[i * tq:(i + 1) * tq, :], kv_ref[keys, 0:128], kr_ref[keys, :], i * tq, 0)

        ready = scores_of(0)
        for i in range(nq):
            scores = ready
            if i + 1 < nq:
                ready = scores_of(i + 1)
            rows = slice(i * tq, (i + 1) * tq)
            o, lse = _softmax_times(scores, kv_ref[0:(i + 1) * tq, 128:256])
            o_ref[rows, :] = o.astype(o_ref.dtype)
            o32_ref[rows, :] = o
            lse_ref[rows, :] = jnp.broadcast_to(lse, o.shape)

    t = bl * s
    head_cols = pl.BlockSpec((s, 128), lambda b, h: (b, h))
    return pl.pallas_call(
        body, name="attn_fwd", grid=(bl, MLA_HEADS),
        in_specs=[
            pl.BlockSpec((s, 256), lambda b, h: (b, h)),
            pl.BlockSpec((s, 256), lambda b, h: (b, h)),
            pl.BlockSpec((s, 128), lambda b, h: (b, 0)),
            pl.BlockSpec(memory_space=pl.ANY),
        ],
        out_specs=[pl.BlockSpec((s, 128), lambda b, h: (b, GDN_HEADS + h)), head_cols, head_cols],
        out_shape=[jax.ShapeDtypeStruct(cat.shape, cat.dtype)] + [jax.ShapeDtypeStruct((t, MLA_HEADS * MLA_V), F32)] * 2,
        input_output_aliases={3: 0},
        compiler_params=_params(("parallel", "parallel")),
    )(qf, kvf, kr, cat)


def _attn_bwd(qf, kvf, kr, dcat, o32, lse, bl, s, tq):
    tq = min(tq, s)
    nq = s // tq

    def body(q_ref, kv_ref, kr_ref, do_ref, o_ref, lse_ref, dq_ref, dkv_ref, dkr_ref):
        dkv_ref[...] = jnp.zeros_like(dkv_ref)

        @pl.when(pl.program_id(1) == 0)
        def _():
            dkr_ref[...] = jnp.zeros_like(dkr_ref)

        def block(i):
            rows = slice(i * tq, (i + 1) * tq)
            return q_ref[rows, :], do_ref[rows, :]

        def first_products(i, j):
            (q, d_o), keys = block(i), slice(j * tq, (j + 1) * tq)
            return _scores(q, kv_ref[keys, 0:128], kr_ref[keys, :], i * tq, j * tq), mm_nt(d_o, kv_ref[keys, 128:256])

        tiles = [(i, j) for i in range(nq) for j in range(i + 1)]
        ready = first_products(*tiles[0])
        for t, (i, j) in enumerate(tiles):
            scores, dp = ready
            if t + 1 < len(tiles):
                ready = first_products(*tiles[t + 1])
            rows, keys = slice(i * tq, (i + 1) * tq), slice(j * tq, (j + 1) * tq)
            q, d_o = block(i)
            if j == 0:
                delta = jnp.sum(d_o * o_ref[rows, :], axis=-1, keepdims=True)
                lse_i = jnp.max(lse_ref[rows, :], axis=-1, keepdims=True)
                dqn, dqr = jnp.zeros((tq, MLA_NOPE), F32), jnp.zeros((tq, 128), F32)
            p = jnp.exp(scores - lse_i)
            ds = p * (dp - delta) * ATT_SCALE
            dkv_ref[keys, 128:256] += mm_tn(p, d_o)
            dkv_ref[keys, 0:128] += mm_tn(ds, q[:, :128])
            dkr_ref[keys, :] += mm_tn(ds, q[:, 128:])
            dqn = dqn + mm(ds, kv_ref[keys, 0:128])
            dqr = dqr + mm(ds, kr_ref[keys, :])
            if j == i:
                dq_ref[rows, 0:128], dq_ref[rows, 128:256] = dqn, dqr

    t = bl * s
    head_cols = pl.BlockSpec((s, 128), lambda b, h: (b, h))
    return pl.pallas_call(
        body, name="attn_bwd", grid=(bl, MLA_HEADS),
        in_specs=[
            pl.BlockSpec((s, 256), lambda b, h: (b, h)),
            pl.BlockSpec((s, 256), lambda b, h: (b, h)),
            pl.BlockSpec((s, 128), lambda b, h: (b, 0)),
            pl.BlockSpec((s, 128), lambda b, h: (b, GDN_HEADS + h)),
            head_cols, head_cols,
        ],
        out_specs=[
            pl.BlockSpec((s, 256), lambda b, h: (b, h)),
            pl.BlockSpec((s, 256), lambda b, h: (b, h)),
            pl.BlockSpec((s, 128), lambda b, h: (b, 0)),
        ],
        out_shape=[jax.ShapeDtypeStruct((t, 1024), F32), jax.ShapeDtypeStruct((t, 1024), F32),
                   jax.ShapeDtypeStruct((t, 128), F32)],
        compiler_params=_params(("parallel", "arbitrary")),
    )(qf, kvf, kr, dcat, o32, lse)


def _mix_ln1_fwd(x, cat, w_o, g, b, tm):
    t = x.shape[0]
    tm = min(tm, t)

    def body(x_ref, cat_ref, w_ref, g_ref, b_ref, r_ref, h_ref, hb_ref, xb_ref):
        r = ALPHA * x_ref[...] + _dot(cat_ref[...], w_ref[...], 1, 0)
        r_ref[...] = r
        h = _layernorm(r, g_ref[...], b_ref[...])
        h_ref[...] = h
        hb_ref[...] = h.astype(BF16)
        xb_ref[...] = x_ref[...].astype(BF16)

    rows = pl.BlockSpec((tm, D_MODEL), lambda i: (i, 0))
    vec = pl.BlockSpec((1, D_MODEL), lambda i: (0, 0))
    return pl.pallas_call(
        body, name="mix_ln1_fwd", grid=(t // tm,),
        in_specs=[rows, pl.BlockSpec((tm, cat.shape[1]), lambda i: (i, 0)), pl.BlockSpec(w_o.shape, lambda i: (0, 0)), vec, vec],
        out_specs=[rows] * 4,
        out_shape=[jax.ShapeDtypeStruct(x.shape, F32)] * 2 + [jax.ShapeDtypeStruct(x.shape, BF16)] * 2,
        compiler_params=_params(("parallel",)),
    )(x, cat, w_o, g, b)


def _ln1_bwd(r1, dr2, da, dgpre, w_gate, w_o, g, b, tm):
    t = r1.shape[0]
    tm = min(tm, t)

    def body(r_ref, d2_ref, da_ref, dgp_ref, wg_ref, wo_ref, g_ref, b_ref, dr_ref, drb_ref, dcat_ref, dg_ref, dbias_ref):
        dh = ALPHA * d2_ref[...] + da_ref[...] + _dot(dgp_ref[...], wg_ref[...], 1, 1)
        _, vjp = jax.vjp(_layernorm, r_ref[...], g_ref[...], b_ref[...])
        dr, dg, dbias = vjp(dh)
        dr_ref[...] = dr
        drb_ref[...] = dr.astype(BF16)
        dcat_ref[...] = _dot(dr, wo_ref[...], 1, 1)

        @pl.when(pl.program_id(0) == 0)
        def _():
            dg_ref[...] = jnp.zeros_like(dg_ref)
            dbias_ref[...] = jnp.zeros_like(dbias_ref)

        dg_ref[...] += dg
        dbias_ref[...] += dbias

    rows = pl.BlockSpec((tm, D_MODEL), lambda i: (i, 0))
    vec = pl.BlockSpec((1, D_MODEL), lambda i: (0, 0))
    return pl.pallas_call(
        body, name="ln1_bwd", grid=(t // tm,),
        in_specs=[rows] * 4 + [pl.BlockSpec(w_gate.shape, lambda i: (0, 0)), pl.BlockSpec(w_o.shape, lambda i: (0, 0)), vec, vec],
        out_specs=[rows, rows, pl.BlockSpec((tm, w_o.shape[0]), lambda i: (i, 0)), vec, vec],
        out_shape=[jax.ShapeDtypeStruct(r1.shape, F32), jax.ShapeDtypeStruct(r1.shape, BF16),
                   jax.ShapeDtypeStruct((t, w_o.shape[0]), F32)] + [jax.ShapeDtypeStruct((1, D_MODEL), F32)] * 2,
        compiler_params=_params(("arbitrary",)),
    )(r1, dr2, da, dgpre, w_gate, w_o, g, b)


def _ffn_act_fwd(u, conv_w, conv_b, bl, s, cb):
    nj = D_FF // cb

    def body(ug_ref, uu_ref, wg_ref, wu_ref, bg_ref, bu_ref, act_ref):
        def store(rows, act):
            act_ref[rows, :] = act.astype(BF16)

        _rows_apply(_ffn_act, [lambda r: ug_ref[r, :], lambda r: uu_ref[r, :]],
                    [wg_ref[...], wu_ref[...], bg_ref[...], bu_ref[...]], store, s, cb, CONV_HALO)

    return pl.pallas_call(
        body, name="ffn_act_fwd", grid=(bl, nj),
        in_specs=[
            pl.BlockSpec((s, cb), lambda b, j: (b, j)), pl.BlockSpec((s, cb), lambda b, j: (b, nj + j)),
            pl.BlockSpec((FFN_CONV, cb), lambda b, j: (0, j)), pl.BlockSpec((FFN_CONV, cb), lambda b, j: (0, nj + j)),
            pl.BlockSpec((1, cb), lambda b, j: (0, j)), pl.BlockSpec((1, cb), lambda b, j: (0, nj + j)),
        ],
        out_specs=pl.BlockSpec((s, cb), lambda b, j: (b, j)),
        out_shape=jax.ShapeDtypeStruct((bl * s, D_FF), BF16),
        compiler_params=_params(("parallel", "parallel")),
    )(u, u, conv_w, conv_w, conv_b, conv_b)


def _ffn_act_bwd(u, conv_w, conv_b, dact, bl, s, cb):
    nj = D_FF // cb

    def body(ug_ref, uu_ref, wg_ref, wu_ref, bg_ref, bu_ref, da_ref, du_ref, dwg_ref, dwu_ref, dbg_ref, dbu_ref, acc):
        def store_into(half):
            def store(rows, value, add):
                if add:
                    acc[half, rows, :] += value
                else:
                    acc[half, rows, :] = value
            return store

        dwg_ref[...], dwu_ref[...], dbg_ref[...], dbu_ref[...] = _rows_vjp(
            _ffn_act, [lambda r: ug_ref[r, :], lambda r: uu_ref[r, :]], [wg_ref[...], wu_ref[...], bg_ref[...], bu_ref[...]],
            lambda r: da_ref[r, :], [store_into(0), store_into(1)], s, cb, CONV_HALO)
        du_ref[...] = acc[...].astype(BF16)

    t = bl * s
    blk = pl.BlockSpec((s, cb), lambda b, j: (b, j))
    wpart = pl.BlockSpec((None, FFN_CONV, cb), lambda b, j: (b, 0, j))
    bpart = pl.BlockSpec((None, 1, cb), lambda b, j: (b, 0, j))
    return pl.pallas_call(
        body, name="ffn_act_bwd", grid=(bl, nj),
        in_specs=[
            blk, pl.BlockSpec((s, cb), lambda b, j: (b, nj + j)),
            pl.BlockSpec((FFN_CONV, cb), lambda b, j: (0, j)), pl.BlockSpec((FFN_CONV, cb), lambda b, j: (0, nj + j)),
            pl.BlockSpec((1, cb), lambda b, j: (0, j)), pl.BlockSpec((1, cb), lambda b, j: (0, nj + j)),
            blk,
        ],
        out_specs=[pl.BlockSpec((2, s, cb), lambda b, j: (0, b, j)), wpart, wpart, bpart, bpart],
        out_shape=[jax.ShapeDtypeStruct((2, t, D_FF), BF16)] + [jax.ShapeDtypeStruct((bl, FFN_CONV, D_FF), F32)] * 2
        + [jax.ShapeDtypeStruct((bl, 1, D_FF), F32)] * 2,
        scratch_shapes=[pltpu.VMEM((2, s, cb), F32)],
        compiler_params=_params(("parallel", "parallel")),
    )(u, u, conv_w, conv_w, conv_b, conv_b, dact)


def _head(h1, h1b, act, pt, w_down, w_gate, w_proj, bgate, g2, b2, target, tm):
    t = h1.shape[0]
    tm = min(tm, t)

    def body(h1_ref, h1b_ref, act_ref, pt_ref, wd_ref, wg_ref, wp_ref, bg_ref, g2_ref, b2_ref, tg_ref,
             dr_ref, drb_ref, dgp_ref, dpp_ref, loss_ref, dbg_ref, dg2_ref, db2_ref):
        fn = functools.partial(_head_loss, target=tg_ref[...])
        ffn = _dot(act_ref[...], wd_ref[...], 1, 0)
        gpre, pp = _dot(h1b_ref[...], wg_ref[...], 1, 0), _dot(pt_ref[...], wp_ref[...], 1, 0)
        loss, vjp = jax.vjp(fn, h1_ref[...], ffn, gpre, pp, bg_ref[...], g2_ref[...], b2_ref[...])
        _, dffn, dgp, dpp, dbg, dg2, db2 = vjp(jnp.ones((1, 1), F32))
        dr_ref[...] = dffn
        drb_ref[...], dgp_ref[...], dpp_ref[...] = dffn.astype(BF16), dgp.astype(BF16), dpp.astype(BF16)

        @pl.when(pl.program_id(0) == 0)
        def _():
            loss_ref[...] = jnp.zeros_like(loss_ref)
            dbg_ref[...] = jnp.zeros_like(dbg_ref)
            dg2_ref[...] = jnp.zeros_like(dg2_ref)
            db2_ref[...] = jnp.zeros_like(db2_ref)

        loss_ref[...] += jnp.broadcast_to(loss, loss_ref.shape)
        dbg_ref[...] += dbg
        dg2_ref[...] += dg2
        db2_ref[...] += db2

    rows = pl.BlockSpec((tm, D_MODEL), lambda i: (i, 0))
    vec = pl.BlockSpec((1, D_MODEL), lambda i: (0, 0))
    return pl.pallas_call(
        body, name="head", grid=(t // tm,),
        in_specs=[rows, rows, pl.BlockSpec((tm, act.shape[1]), lambda i: (i, 0)), pl.BlockSpec((tm, pt.shape[1]), lambda i: (i, 0))]
        + [pl.BlockSpec(w.shape, lambda i: (0, 0)) for w in (w_down, w_gate, w_proj)] + [vec] * 3 + [rows],
        out_specs=[rows] * 4 + [pl.BlockSpec((8, 128), lambda i: (0, 0))] + [vec] * 3,
        out_shape=[jax.ShapeDtypeStruct(h1.shape, F32)] + [jax.ShapeDtypeStruct(h1.shape, BF16)] * 3
        + [jax.ShapeDtypeStruct((8, 128), F32)]
        + [jax.ShapeDtypeStruct((1, D_MODEL), F32)] * 3,
        compiler_params=_params(("arbitrary",)),
    )(h1, h1b, act, pt, w_down, w_gate, w_proj, bgate, g2, b2, target)


def _adam_update(g, w_ref, m_ref, v_ref, g_ref, d_ref, nm_ref, nv_ref):
    m2 = ADAM_B1 * m_ref[...] + (1.0 - ADAM_B1) * g
    v2 = ADAM_B2 * v_ref[...] + (1.0 - ADAM_B2) * jnp.square(g)
    m_hat = m2 / (1.0 - ADAM_B1 ** ADAM_STEP)
    v_hat = v2 / (1.0 - ADAM_B2 ** ADAM_STEP)
    g_ref[...] = g
    d_ref[...] = -ADAM_LR * (m_hat / (jnp.sqrt(v_hat) + ADAM_EPS) + ADAM_WD * w_ref[...])
    nm_ref[...] = m2
    nv_ref[...] = v2


def _row_tile(rows, cols, limit_bytes=1024 * 1024):
    best = None
    for t in range(HALF_ROWS_QUANTUM, rows + 1, HALF_ROWS_QUANTUM):
        if rows % t == 0 and t * cols * 4 <= limit_bytes:
            best = t
    return best or rows


def _adamw_reduced(recv, w, m, v, name):
    a, b = w.shape
    ta = _row_tile(a, b)

    def body(recv_ref, w_ref, m_ref, v_ref, g_ref, d_ref, nm_ref, nv_ref):
        c = lax.axis_index("c")
        for core in range(2):
            @pl.when(c == core)
            def _():
                got = [recv_ref[k].astype(F32) for k in range(N_DEV)]
                same = [got[7], got[0], got[1], got[2]]
                other = got[3:7]
                core0, core1 = (same, other) if core == 0 else (other, same)
                g = core0[0] + core1[0]
                for r in range(1, N_CHIPS):
                    g = (g + core0[r]) + core1[r]
                _adam_update(g, w_ref, m_ref, v_ref, g_ref, d_ref, nm_ref, nv_ref)

    blk = pl.BlockSpec((ta, b), lambda i: (i, 0))
    return pl.pallas_call(
        body, name=name, grid=(a // ta,),
        in_specs=[pl.BlockSpec((N_DEV, ta, b), lambda i: (0, i, 0)), blk, blk, blk], out_specs=[blk] * 4,
        out_shape=[jax.ShapeDtypeStruct(w.shape, F32)] * 4, compiler_params=_params(("parallel",)),
    )(recv, w, m, v)


def _adamw_small(g, w, m, v):
    def body(g_in, w_ref, m_ref, v_ref, g_ref, d_ref, nm_ref, nv_ref):
        _adam_update(g_in[...], w_ref, m_ref, v_ref, g_ref, d_ref, nm_ref, nv_ref)

    blk = pl.BlockSpec(w.shape, lambda i: (0, 0))
    return pl.pallas_call(
        body, name="adamw_small", grid=(1,), in_specs=[blk] * 4, out_specs=[blk] * 4,
        out_shape=[jax.ShapeDtypeStruct(w.shape, F32)] * 4, compiler_params=_params(("arbitrary",)),
    )(g, w, m, v)


def _remote(src, dst, send_sem, recv_sem, device):
    return pltpu.make_async_remote_copy(src_ref=src, dst_ref=dst, send_sem=send_sem, recv_sem=recv_sem,
                                        device_id=device, device_id_type=MESH)


def _place():
    x, y, c = lax.axis_index("x"), lax.axis_index("y"), lax.axis_index("c")
    return x, y, c, 2 * x + y, [(1 - x, y), (x, 1 - y), (1 - x, 1 - y)]


HBM_REF = pl.BlockSpec(memory_space=pl.ANY)
HALF_ROWS_QUANTUM = 16


def _gather_sems(n):
    return [pltpu.SemaphoreType.DMA((3 * n,))] * 4 + [pltpu.SemaphoreType.DMA((n,))]


def _gather_copies(ins, outs, sems):
    send_s, recv_s, fsend_s, frecv_s, local_s = sems
    x, y, c, me, chips = _place()
    local, sends, steps = [], [], []
    for i, (src, dst) in enumerate(zip(ins, outs)):
        local.append(pltpu.make_async_copy(src, dst.at[me], local_s.at[i]))
        half = src.shape[0] // 2
        split = src.shape[0] % (2 * HALF_ROWS_QUANTUM) == 0
        if split:
            mine = pl.ds(pl.multiple_of(c * half, HALF_ROWS_QUANTUM), half)
            theirs = pl.ds(pl.multiple_of((1 - c) * half, HALF_ROWS_QUANTUM), half)
        for r, (px, py) in enumerate(chips):
            k, peer = 3 * i + r, 2 * px + py
            if split:
                sends.append(_remote(src.at[mine], dst.at[me, mine], send_s.at[k], recv_s.at[k], (px, py, c)))
                landed = dst.at[peer, mine]
                steps.append((_remote(src.at[mine], landed, send_s.at[k], recv_s.at[k], (px, py, c)),
                              _remote(landed, landed, fsend_s.at[k], frecv_s.at[k], (x, y, 1 - c)),
                              _remote(dst.at[peer, theirs], dst.at[peer, theirs], fsend_s.at[k], frecv_s.at[k], (x, y, 1 - c))))
            else:
                sends.append(_remote(src, dst.at[me], send_s.at[k], recv_s.at[k], (px, py, c)))
                steps.append((_remote(src, dst.at[peer], send_s.at[k], recv_s.at[k], (px, py, c)), None, None))
    return local, sends, steps


def _scatter_sems(n):
    return [pltpu.SemaphoreType.DMA((4 * n,))] * 2 + [pltpu.SemaphoreType.DMA((3 * n,))] * 2 + [pltpu.SemaphoreType.DMA((n,))]


def _scatter_copies(ins, outs, sems):
    send_s, recv_s, fsend_s, frecv_s, local_s = sems
    x, y, c, me, chips = _place()
    local, sends, steps = [], [], []
    for i, (src, dst) in enumerate(zip(ins, outs)):
        local.append(pltpu.make_async_copy(src.at[me], dst.at[N_DEV - 1], local_s.at[i]))
        for r, (px, py) in enumerate(chips):
            k = 4 * i + r
            cp = _remote(src.at[2 * px + py], dst.at[r], send_s.at[k], recv_s.at[k], (px, py, c))
            fwd = _remote(dst.at[r], dst.at[4 + r], fsend_s.at[3 * i + r], frecv_s.at[3 * i + r], (x, y, 1 - c))
            sends.append(cp)
            steps.append((cp, fwd, fwd))
        k = 4 * i + 3
        cp = _remote(src.at[me], dst.at[3], send_s.at[k], recv_s.at[k], (x, y, 1 - c))
        sends.append(cp)
        steps.append((cp, None, None))
    return local, sends, steps


def _exchange_start(plan):
    local, sends, _ = plan
    for cp in local + sends:
        cp.start()


def _exchange_pass_on(plan):
    for arrival, pass_on, _ in plan[2]:
        arrival.wait_recv()
        if pass_on is not None:
            pass_on.start()


def _exchange_finish(plan):
    local, sends, steps = plan
    for _, pass_on, passed in steps:
        if pass_on is not None:
            passed.wait_recv()
    for cp in sends:
        cp.wait_send()
    for _, pass_on, _ in steps:
        if pass_on is not None:
            pass_on.wait_send()
    for cp in local:
        cp.wait()


def _exchange_call(arrays, copies, sems, out_shapes, name):
    n = len(arrays)

    def body(*refs):
        plan = copies(refs[:n], refs[n:2 * n], refs[2 * n:])
        _exchange_start(plan)
        _exchange_pass_on(plan)
        _exchange_finish(plan)

    return pl.pallas_call(
        body, name=name, in_specs=[HBM_REF] * n, out_specs=[HBM_REF] * n, out_shape=out_shapes,
        scratch_shapes=sems(n), compiler_params=pltpu.CompilerParams(has_side_effects=True),
    )(*arrays)


def _gather_call(shards, name):
    shapes = [jax.ShapeDtypeStruct((N_CHIPS,) + a.shape, a.dtype) for a in shards]
    return _exchange_call(shards, _gather_copies, _gather_sems, shapes, name)


def _all_reduce_small(a):
    def body(in_ref, out_ref, slots, send_sems, recv_sems):
        x, y, c = lax.axis_index("x"), lax.axis_index("y"), lax.axis_index("c")
        me = 4 * x + 2 * y + c
        slots[0] = in_ref[...]
        sends = []
        for r in range(1, N_DEV):
            peer = (x ^ (r >> 2), y ^ ((r >> 1) & 1), c ^ (r & 1))
            sends.append(pltpu.make_async_remote_copy(src_ref=in_ref, dst_ref=slots.at[r], send_sem=send_sems.at[r],
                                                      recv_sem=recv_sems.at[r], device_id=peer, device_id_type=MESH))
        for cp in sends:
            cp.start()
        for cp in sends:
            cp.wait_recv()
        acc = slots[me]
        for dev in range(1, N_DEV):
            acc = acc + slots[dev ^ me]
        out_ref[...] = acc
        for cp in sends:
            cp.wait_send()

    return pl.pallas_call(
        body, name="small_all_reduce",
        in_specs=[pl.BlockSpec(memory_space=pltpu.VMEM)], out_specs=pl.BlockSpec(memory_space=pltpu.VMEM),
        out_shape=jax.ShapeDtypeStruct(a.shape, a.dtype),
        scratch_shapes=[pltpu.VMEM((N_DEV,) + a.shape, a.dtype), pltpu.SemaphoreType.DMA((N_DEV,)),
                        pltpu.SemaphoreType.DMA((N_DEV,))],
        compiler_params=pltpu.CompilerParams(has_side_effects=True),
    )(a)


SHARDED = ["w_in", "mla_w_q_up", "mla_w_kv_up", "w_out", "ffn_w_up", "ffn_w_down", "ple_w_gate", "ple_w_proj",
           "gdn_conv_w", "ffn_conv_w"]
SHARD_AXIS = {"w_in": 1, "mla_w_q_up": 1, "mla_w_kv_up": 1, "w_out": 0, "ffn_w_up": 1, "ffn_w_down": 0,
              "ple_w_gate": 0, "ple_w_proj": 1, "gdn_conv_w": 1, "ffn_conv_w": 1}
SMALL = ["gdn_a_log", "gdn_dt_bias", "gdn_norm_g", "mla_q_norm_g", "mla_kv_norm_g", "ln1_g", "ln1_b", "ffn_conv_b",
         "ple_b_gate", "ln2_g", "ln2_b"]
WEIGHTS = ["w_in", "gdn_conv_w", "gdn_a_log", "gdn_dt_bias", "gdn_norm_g", "mla_q_norm_g", "mla_w_q_up", "mla_kv_norm_g",
           "mla_w_kv_up", "w_out", "ln1_g", "ln1_b", "ffn_w_up", "ffn_conv_w", "ffn_conv_b", "ffn_w_down", "ple_w_gate",
           "ple_b_gate", "ple_w_proj", "ln2_g", "ln2_b"]
F32_ON_WIRE = ("gdn_conv_w", "ffn_conv_w")
GATHER_EARLY = ["w_in", "gdn_conv_w"]
GATHER_LATE = ["mla_w_q_up", "mla_w_kv_up", "w_out", "ffn_w_up", "ffn_conv_w", "ffn_w_down", "ple_w_gate", "ple_w_proj"]
SCATTER_EARLY = ["ffn_w_up", "ffn_conv_w", "ffn_w_down", "ple_w_gate", "ple_w_proj", "w_out"]
SCATTER_LATE = ["w_in", "gdn_conv_w", "mla_w_q_up", "mla_w_kv_up"]
PACK_COLS = 1024
PACK_ROW_TILE = 8


def _join_blocks(blocks, axis):
    n, a, b = blocks.shape
    if axis == 0:
        return blocks.reshape(n * a, b)
    return jnp.transpose(blocks, (1, 0, 2)).reshape(a, n * b)


def _split_blocks(full, axis):
    if axis == 0:
        return full.reshape(N_CHIPS, full.shape[0] // N_CHIPS, full.shape[1])
    a, nb = full.shape
    return jnp.transpose(full.reshape(a, N_CHIPS, nb // N_CHIPS), (1, 0, 2))


def _pack(arrays):
    flat = jnp.concatenate([a.reshape(-1) for a in arrays])
    quantum = PACK_COLS * PACK_ROW_TILE
    padded = -(-flat.shape[0] // quantum) * quantum
    return jnp.pad(flat, (0, padded - flat.shape[0])).reshape(-1, PACK_COLS)


def _unpack(packed, shapes):
    flat = packed.reshape(-1)
    out, off = [], 0
    for shp in shapes:
        n = int(np.prod(shp))
        out.append(flat[off:off + n].reshape(shp))
        off += n
    return out


def kernel(x, p, w_in, gdn_conv_w, gdn_a_log, gdn_dt_bias, gdn_norm_g, mla_q_norm_g, mla_w_q_up, mla_kv_norm_g, mla_w_kv_up, w_out, ln1_g, ln1_b, ffn_w_up, ffn_conv_w, ffn_conv_b, ffn_w_down, ple_w_gate, ple_b_gate, ple_w_proj, ln2_g, ln2_b, loss_target, m_w_in, m_gdn_conv_w, m_gdn_a_log, m_gdn_dt_bias, m_gdn_norm_g, m_mla_q_norm_g, m_mla_w_q_up, m_mla_kv_norm_g, m_mla_w_kv_up, m_w_out, m_ln1_g, m_ln1_b, m_ffn_w_up, m_ffn_conv_w, m_ffn_conv_b, m_ffn_w_down, m_ple_w_gate, m_ple_b_gate, m_ple_w_proj, m_ln2_g, m_ln2_b, v_w_in, v_gdn_conv_w, v_gdn_a_log, v_gdn_dt_bias, v_gdn_norm_g, v_mla_q_norm_g, v_mla_w_q_up, v_mla_kv_norm_g, v_mla_w_kv_up, v_w_out, v_ln1_g, v_ln1_b, v_ffn_w_up, v_ffn_conv_w, v_ffn_conv_b, v_ffn_w_down, v_ple_w_gate, v_ple_b_gate, v_ple_w_proj, v_ln2_g, v_ln2_b):
    given = dict(locals())
    wsh = {n: given[n][0] for n in WEIGHTS}
    msh = {n: given["m_" + n][0] for n in WEIGHTS}
    vsh = {n: given["v_" + n][0] for n in WEIGHTS}
    bl, s, _ = x.shape
    t = bl * s
    xt = x.reshape(t, D_MODEL)
    pt = p.reshape(t, PLE_DIM)
    target = loss_target.reshape(t, D_MODEL)

    wire = lambda n: wsh[n] if n in F32_ON_WIRE else wsh[n].astype(BF16)
    early = _gather_call([wire(n) for n in GATHER_EARLY], "weights_gather_early")
    full = {n: _join_blocks(g, SHARD_AXIS[n]) for n, g in zip(GATHER_EARLY, early)}
    late_shards = [wire(n) for n in GATHER_LATE]
    late_ride = (late_shards, _gather_copies, _gather_sems,
                 [jax.ShapeDtypeStruct((N_CHIPS,) + a.shape, a.dtype) for a in late_shards], 0.75)

    in_cols, q_cols = _w_in_cols(), _w_q_cols()
    w_in_p = _pad_cols(full["w_in"], in_cols)
    gconv = full["gdn_conv_w"]
    row = lambda a: a.reshape(1, -1)
    sc = jnp.zeros((8, 128), F32).at[0, :GDN_HEADS].set(wsh["gdn_a_log"]).at[1, :GDN_HEADS].set(wsh["gdn_dt_bias"])
    norm_g, qg, kvg = row(wsh["gdn_norm_g"]), row(wsh["mla_q_norm_g"]), row(wsh["mla_kv_norm_g"])
    g1, b1, g2, b2 = row(wsh["ln1_g"]), row(wsh["ln1_b"]), row(wsh["ln2_g"]), row(wsh["ln2_b"])
    fbias, bgate = row(wsh["ffn_conv_b"]), row(wsh["ple_b_gate"])

    inv = ROPE_THETA ** (-jnp.arange(0, MLA_ROPE, 2, dtype=F32) / MLA_ROPE)
    ang = jnp.arange(s, dtype=F32)[:, None] * inv[None, :]
    zero = jnp.zeros_like(ang)
    cos_t = jnp.concatenate([jnp.cos(ang), zero, jnp.cos(ang), zero], axis=1)
    sin_t = jnp.concatenate([-jnp.sin(ang), zero, jnp.sin(ang), zero], axis=1)

    proj = _matmul(xt, w_in_p, name="proj", tm=1024)
    cat, o_raw, states, qkvg, late = _gdn_fwd(proj, gconv, sc, norm_g, bl, s, late_ride)
    w_up = late[GATHER_LATE.index("ffn_w_up")]
    full.update({n: _join_blocks(g, SHARD_AXIS[n]) for n, g in zip(GATHER_LATE, late) if n != "ffn_w_up"})
    w_o, w_down = full["w_out"], full["ffn_w_down"]
    w_gate, w_proj, fconv = full["ple_w_gate"], full["ple_w_proj"], full["ffn_conv_w"]
    w_q_p, w_kv = _pad_cols(full["mla_w_q_up"], q_cols), full["mla_w_kv_up"]
    qf, kvf, kr = _mla_prep_fwd(proj, qg, kvg, w_q_p, w_kv, cos_t, sin_t, s, 512)
    cat, attn_o32, attn_lse = _attn_fwd(qf, kvf, kr, cat, bl, s, 512)
    wide = dict(tm=1024, tn=1024)
    r1, h1, h1b, xb = _mix_ln1_fwd(xt, cat, w_o, g1, b1, 512)
    u = _matmul(h1b, w_up, name="ffn_up", tm=1024, tn=1408, b_parts=N_CHIPS)
    act = _ffn_act_fwd(u, fconv, fbias, bl, s, 256)
    dr2, dr2b, dgpre, dpp, loss_acc, dbgate, dg2, db2 = _head(h1, h1b, act, pt, w_down, w_gate, w_proj, bgate, g2, b2, target, 256)

    dact = _matmul(dr2b, w_down, name="d_act", tb=True, tm=1024, tn=1408)
    long_k = dict(ta=True, tk=2048)
    d_w_down = _matmul(act, dr2b, name="dw_down", tm=1408, tn=1024, **long_k)
    du, dfcw_g, dfcw_u, dfcb_g, dfcb_u = _ffn_act_bwd(u, fconv, fbias, dact, bl, s, 256)
    dh1_a = _matmul(du, w_up, name="dh1_ffn", tb=True, tk=1408, a_halves=True, b_parts=N_CHIPS, **wide)
    d_w_up = _matmul(h1b, du, name="dw_up", tn=1408, b_parts=2, out_parts=N_CHIPS, out_dtype=BF16, **long_k)
    d_w_gate = _matmul(h1b, dgpre, name="dw_gate", **long_k, **wide)
    d_w_proj = _matmul(pt, dpp, name="dw_proj", ta=True, tn=1024)
    dr1, dr1b, dcat, dg1, db1 = _ln1_bwd(r1, dr2, dh1_a, dgpre, w_gate, w_o, g1, b1, 512)
    d_w_o = _matmul(cat, dr1b, name="dw_out", **long_k, **wide)

    gfull = {
        "ffn_w_down": d_w_down, "ple_w_gate": d_w_gate, "ple_w_proj": d_w_proj, "w_out": d_w_o,
        "ffn_conv_w": jnp.concatenate([jnp.sum(dfcw_g, 0), jnp.sum(dfcw_u, 0)], axis=1),
    }
    slabs = {n: _split_blocks(g, SHARD_AXIS[n]).astype(BF16) for n, g in gfull.items()}
    slabs["ffn_w_up"] = d_w_up
    early_slabs = [slabs[n] for n in SCATTER_EARLY]
    early_ride = (early_slabs, _scatter_copies, _scatter_sems,
                  [jax.ShapeDtypeStruct((N_DEV,) + a.shape[1:], a.dtype) for a in early_slabs], 0.7)
    dproj, dab, dcwq, dcwk, dcwv, dsc, dng, early_recv = _gdn_bwd(proj, gconv, sc, norm_g, o_raw, states, qkvg, dcat, bl, s,
                                                                  early_ride)
    received = dict(zip(SCATTER_EARLY, early_recv))
    dqf, dkvf, dkr = _attn_bwd(qf, kvf, kr, dcat, attn_o32, attn_lse, bl, s, 256)
    dproj, dqg, dkvg, d_w_q_p, d_w_kv = _mla_prep_bwd(proj, qg, kvg, w_q_p, w_kv, cos_t, sin_t, dqf, dkvf, dkr, dab, dproj, s, 512)
    d_w_in_p = _matmul(xb, dproj, name="dw_in", **long_k, **wide)

    gfull.update({
        "w_in": _unpad_cols(d_w_in_p, in_cols, D_IN),
        "mla_w_q_up": _unpad_cols(d_w_q_p, q_cols, MLA_HEADS * (MLA_NOPE + MLA_ROPE)),
        "mla_w_kv_up": d_w_kv,
        "gdn_conv_w": jnp.concatenate([jnp.sum(dcwq, 0), jnp.sum(dcwk, 0), jnp.sum(dcwv, 0)], axis=1),
    })
    slabs.update({n: _split_blocks(gfull[n], SHARD_AXIS[n]).astype(BF16) for n in SCATTER_LATE})
    late_slabs = [slabs[n] for n in SCATTER_LATE]
    late_scatter = (late_slabs, _scatter_copies, _scatter_sems,
                    [jax.ShapeDtypeStruct((N_DEV,) + a.shape[1:], a.dtype) for a in late_slabs], 0.85)
    grad_x, late_recv = _matmul(dproj, w_in_p, name="d_x", tb=True, add=dr1, add_scale=ALPHA, ride=late_scatter, **wide)
    received.update(zip(SCATTER_LATE, late_recv))
    dsc_sum = jnp.sum(dsc, axis=(0, 1))
    gsmall = {
        "gdn_a_log": dsc_sum[0, :GDN_HEADS], "gdn_dt_bias": dsc_sum[1, :GDN_HEADS],
        "gdn_norm_g": jnp.sum(dng[:, :, 0, :], axis=(0, 1)),
        "mla_q_norm_g": dqg[0], "mla_kv_norm_g": dkvg[0], "ln1_g": dg1[0], "ln1_b": db1[0],
        "ffn_conv_b": jnp.concatenate([jnp.sum(dfcb_g, 0), jnp.sum(dfcb_u, 0)], axis=1)[0],
        "ple_b_gate": dbgate[0], "ln2_g": dg2[0], "ln2_b": db2[0],
    }

    big = [{}, {}, {}, {}]
    for n in SHARDED:
        for kind, val in enumerate(_adamw_reduced(received[n], wsh[n], msh[n], vsh[n], "adamw_" + n)):
            big[kind][n] = val

    small_shapes = [wsh[n].shape for n in SMALL]
    gsum = _all_reduce_small(_pack([gsmall[n] for n in SMALL]))
    spacks = _adamw_small(gsum, _pack([wsh[n] for n in SMALL]), _pack([msh[n] for n in SMALL]), _pack([vsh[n] for n in SMALL]))
    small = [dict(zip(SMALL, _unpack(pk, small_shapes))) for pk in spacks]

    loss = lax.psum(loss_acc[0, 0], ("x", "y", "c"))
    outs = [loss, grad_x.reshape(x.shape)]
    for kind in range(4):
        for n in WEIGHTS:
            val = big[kind][n] if n in big[kind] else small[kind][n]
            outs.append(val[None])
    return tuple(outs)
```

```python
import functools
import math

import numpy as np
import jax
import jax.numpy as jnp
from jax import lax
from jax.experimental import pallas as pl
from jax.experimental.pallas import tpu as pltpu

F32 = jnp.float32
BF16 = jnp.bfloat16

D_MODEL = 1024
CHUNK = 64
PLE_DIM = 256
GDN_HEADS = 4
GDN_DK = 128
GDN_DV = 128
GDN_CONV = 4
MLA_HEADS = 4
MLA_NOPE = 128
MLA_ROPE = 64
MLA_V = 128
MLA_Q_LORA = 384
MLA_KV_LORA = 256
ROPE_THETA = 10000.0
D_FF = 2816
FFN_CONV = 3
DEPTH = 1
ALPHA = (2.0 * DEPTH) ** 0.25
NORM_EPS = 1e-6
GDN_QK = GDN_HEADS * GDN_DK
GDN_VW = GDN_HEADS * GDN_DV
D_IN = 2 * GDN_QK + 2 * GDN_VW + 2 * GDN_HEADS + MLA_Q_LORA + MLA_KV_LORA + MLA_ROPE
ATT_SCALE = (MLA_NOPE + MLA_ROPE) ** -0.5

ADAM_LR = 0.001
ADAM_B1 = 0.9
ADAM_B2 = 0.999
ADAM_EPS = 1e-08
ADAM_WD = 0.01
ADAM_STEP = 10

LANES = 128
VMEM_LIMIT = 60 * 1024 * 1024
GDN_FWD_GROUP = 16
GDN_BWD_GROUP = 16
N_CHIPS = 4
N_DEV = 8

P_WIDTH = 3072
P_MLA = 2048
MESH = pl.DeviceIdType.MESH


def _rope_slot(j):
    return j if j < MLA_ROPE // 2 else 64 + (j - MLA_ROPE // 2)


def _w_in_cols():
    idx = -np.ones((P_WIDTH,), np.int64)
    for h in range(GDN_HEADS):
        base = h * 512
        idx[base:base + 128] = np.arange(128) + h * GDN_DK
        idx[base + 128:base + 256] = np.arange(128) + GDN_QK + h * GDN_DK
        idx[base + 256:base + 384] = np.arange(128) + 2 * GDN_QK + h * GDN_DV
        idx[base + 384:base + 512] = np.arange(128) + 2 * GDN_QK + GDN_VW + h * GDN_DV
    o_a = 2 * GDN_QK + 2 * GDN_VW
    idx[P_MLA:P_MLA + 2 * GDN_HEADS] = np.arange(2 * GDN_HEADS) + o_a
    o_cq = o_a + 2 * GDN_HEADS
    idx[P_MLA + 128:P_MLA + 512] = np.arange(MLA_Q_LORA) + o_cq
    o_ckv = o_cq + MLA_Q_LORA
    idx[P_MLA + 512:P_MLA + 768] = np.arange(MLA_KV_LORA) + o_ckv
    o_kr = o_ckv + MLA_KV_LORA
    for j in range(MLA_ROPE):
        idx[P_MLA + 768 + _rope_slot(j)] = o_kr + j
    return idx


def _w_q_cols():
    idx = -np.ones((MLA_HEADS * 256,), np.int64)
    for h in range(MLA_HEADS):
        o = h * (MLA_NOPE + MLA_ROPE)
        idx[h * 256:h * 256 + 128] = np.arange(128) + o
        for j in range(MLA_ROPE):
            idx[h * 256 + 128 + _rope_slot(j)] = o + MLA_NOPE + j
    return idx


def _pad_cols(w, idx):
    safe = np.where(idx >= 0, idx, 0)
    return jnp.where(jnp.asarray(idx >= 0)[None, :], w[:, safe], 0.0)


def _unpad_cols(wp, idx, n):
    inv = np.zeros((n,), np.int64)
    inv[idx[idx >= 0]] = np.nonzero(idx >= 0)[0]
    return wp[:, inv]


def _dot(a, b, ca, cb, precision=None):
    if precision is None:
        a = a.astype(BF16)
        b = b.astype(BF16)
    return lax.dot_general(a, b, (((ca,), (cb,)), ((), ())), preferred_element_type=F32, precision=precision)


@jax.custom_vjp
def mm(a, b):
    return _dot(a, b, 1, 0)


@jax.custom_vjp
def mm_nt(a, b):
    return _dot(a, b, 1, 1)


@jax.custom_vjp
def mm_tn(a, b):
    return _dot(a, b, 0, 0)


mm.defvjp(lambda a, b: (mm(a, b), (a, b)), lambda r, g: (mm_nt(g, r[1]), mm_tn(r[0], g)))
mm_nt.defvjp(lambda a, b: (mm_nt(a, b), (a, b)), lambda r, g: (mm(g, r[1]), mm_tn(g, r[0])))
mm_tn.defvjp(lambda a, b: (mm_tn(a, b), (a, b)), lambda r, g: (mm_nt(r[1], g), mm(r[0], g)))

def _split(a):
    hi = a.astype(BF16)
    return hi, (a - hi.astype(F32)).astype(BF16)


def _dot3(a, b, ca, cb):
    a_hi, a_lo = _split(a)
    b_hi, b_lo = _split(b)
    return (_dot(a_hi, b_hi, ca, cb) + _dot(a_hi, b_lo, ca, cb)) + _dot(a_lo, b_hi, ca, cb)


def _shift_rows(x, s):
    return x if s == 0 else pltpu.roll(x, s % x.shape[0], 0)


def _row(w, j):
    tap = lax.broadcasted_iota(jnp.int32, w.shape, 0)
    return jnp.sum(jnp.where(tap == j, w, 0.0), axis=0, keepdims=True)


@jax.custom_vjp
def dwconv(x, w):
    k = w.shape[0]
    y = _row(w, k - 1) * x
    for j in range(k - 1):
        y = y + _row(w, j) * _shift_rows(x, k - 1 - j)
    return y


def _dwconv_fwd(x, w):
    return dwconv(x, w), (x, w)


def _dwconv_bwd(res, dy):
    x, w = res
    k = w.shape[0]
    dx = _row(w, k - 1) * dy
    tap = lax.broadcasted_iota(jnp.int32, w.shape, 0)
    dw = jnp.where(tap == k - 1, jnp.sum(dy * x, axis=0, keepdims=True), 0.0)
    for j in range(k - 1):
        dx = dx + _row(w, j) * _shift_rows(dy, -(k - 1 - j))
        dw = dw + jnp.where(tap == j, jnp.sum(dy * _shift_rows(x, k - 1 - j), axis=0, keepdims=True), 0.0)
    return dx, dw


dwconv.defvjp(_dwconv_fwd, _dwconv_bwd)


@jax.custom_vjp
def rope128(x, cos, sin):
    return x * cos + pltpu.roll(x, 64, 1) * sin


rope128.defvjp(lambda x, c, s: (rope128(x, c, s), (c, s)),
               lambda r, g: (g * r[0] + pltpu.roll(g * r[1], 64, 1), jnp.zeros_like(r[0]), jnp.zeros_like(r[1])))


def _silu(x):
    return x * jax.nn.sigmoid(x)


def _softplus(x):
    return jnp.maximum(x, 0.0) + jnp.log(1.0 + jnp.exp(-jnp.abs(x)))


def _rmsnorm(x, g):
    return x * lax.rsqrt(jnp.mean(x * x, axis=-1, keepdims=True) + NORM_EPS) * g


def _layernorm(x, g, b):
    mu = jnp.mean(x, axis=-1, keepdims=True)
    xc = x - mu
    var = jnp.mean(xc * xc, axis=-1, keepdims=True)
    return xc * lax.rsqrt(var + NORM_EPS) * g + b


def _pick_lane(row, lane):
    idx = lax.broadcasted_iota(jnp.int32, row.shape, 1)
    return jnp.sum(jnp.where(idx == lane, row, 0.0), axis=1, keepdims=True)


def _gdn_q(pq, cw):
    h = _silu(dwconv(pq, cw))
    return h * lax.rsqrt(jnp.sum(h * h, axis=-1, keepdims=True) + NORM_EPS) * (GDN_DK ** -0.5)


def _gdn_k(pk, cw):
    h = _silu(dwconv(pk, cw))
    return h * lax.rsqrt(jnp.sum(h * h, axis=-1, keepdims=True) + NORM_EPS)


def _gdn_v(pv, cw):
    return _silu(dwconv(pv, cw))


def _gdn_gate(ab, sc, head):
    a = _pick_lane(ab, head)
    b = _pick_lane(ab, GDN_HEADS + head)
    a_log = _pick_lane(_row(sc, 0), head)
    dt_bias = _pick_lane(_row(sc, 1), head)
    beta = jax.nn.sigmoid(b)
    g = -jnp.exp(a_log) * _softplus(a + dt_bias)
    return _two_lanes(g, beta)


def _two_lanes(c0, c1):
    lane = lax.broadcasted_iota(jnp.int32, (c0.shape[0], LANES), 1)
    return jnp.where(lane == 0, c0, jnp.where(lane == 1, c1, 0.0))


def _run(levels):
    try:
        while True:
            next(levels)
    except StopIteration as stop:
        return stop.value


def _inverse_levels(lows):
    n = lows[0].shape[0]
    ii = lax.broadcasted_iota(jnp.int32, (n, n), 0)
    jj = lax.broadcasted_iota(jnp.int32, (n, n), 1)
    eye = jnp.where(ii == jj, 1.0, 0.0)
    invs = [eye - low for low in lows]
    powers = [_dot3(low, low, 1, 0) for low in lows]
    yield
    k = 2
    while k < n:
        invs = [inv + _dot3(inv, p, 1, 0) for inv, p in zip(invs, powers)]
        yield
        k *= 2
        if k < n:
            powers = [_dot3(p, p, 1, 0) for p in powers]
            yield
    return invs


def _inverse_group(lows):
    return _run(_inverse_levels(lows))


@jax.custom_vjp
def solve_group(lows, rhss):
    return [_dot3(inv, rhs, 1, 0) for inv, rhs in zip(_inverse_group(lows), rhss)]


def _solve_whole(lows, rhss):
    return solve_group(lows, rhss)
    yield


def _solve_levels(lows, rhss):
    invs = yield from _inverse_levels(lows)
    return [_dot3(inv, rhs, 1, 0) for inv, rhs in zip(invs, rhss)]


def _solve_group_fwd(lows, rhss):
    invs = _inverse_group(lows)
    xs = [_dot3(inv, rhs, 1, 0) for inv, rhs in zip(invs, rhss)]
    return xs, (invs, xs)


def _solve_group_bwd(res, dxs):
    invs, xs = res
    n = invs[0].shape[0]
    strict = lax.broadcasted_iota(jnp.int32, (n, n), 0) > lax.broadcasted_iota(jnp.int32, (n, n), 1)
    drhss = [_dot3(inv, dx, 0, 0) for inv, dx in zip(invs, dxs)]
    dlows = [jnp.where(strict, -_dot3(drhs, x, 1, 1), 0.0) for drhs, x in zip(drhss, xs)]
    return dlows, drhss


solve_group.defvjp(_solve_group_fwd, _solve_group_bwd)


def _gdn_local_group(qs, ks, vs, gbs):
    return _run(_gdn_local_levels(qs, ks, vs, gbs, _solve_whole))


def _gdn_local_levels(qs, ks, vs, gbs, solve):
    c = qs[0].shape[0]
    ii = lax.broadcasted_iota(jnp.int32, (c, c), 0)
    jj = lax.broadcasted_iota(jnp.int32, (c, c), 1)
    incl = ii >= jj
    gs = [_pick_lane(gb, 0) for gb in gbs]
    betas = [_pick_lane(gb, 1) for gb in gbs]
    g_rows = [jnp.sum(jnp.where(ii == jj, g, 0.0), axis=0, keepdims=True) for g in gs]
    gc_cols = [jnp.sum(jnp.where(incl, g_row, 0.0), axis=1, keepdims=True) for g_row in g_rows]
    gc_rows = [jnp.sum(jnp.where(jj >= ii, g, 0.0), axis=0, keepdims=True) for g in gs]
    decays = [jnp.where(incl, jnp.exp(jnp.where(incl, gc - gr, 0.0)), 0.0) for gc, gr in zip(gc_cols, gc_rows)]
    yield
    kbs = [k * beta for k, beta in zip(ks, betas)]
    lows = [jnp.where(ii > jj, mm_nt(kb, k) * decay, 0.0) for kb, k, decay in zip(kbs, ks, decays)]
    egs = [jnp.exp(gc) for gc in gc_cols]
    yield
    wus = yield from solve(lows, [jnp.concatenate([kb * eg, v * beta], axis=1) for kb, eg, v, beta in zip(kbs, egs, vs, betas)])
    yield
    qks = [mm_nt(q, k) * decay for q, k, decay in zip(qs, ks, decays)]
    g_lasts = [jnp.sum(g_row, axis=1, keepdims=True) for g_row in g_rows]
    kds = [k * jnp.exp(gl - gc) for k, gl, gc in zip(ks, g_lasts, gc_cols)]
    yield
    ws, us = [wu[:, :GDN_DK] for wu in wus], [wu[:, GDN_DK:] for wu in wus]
    q_effs = [q * eg - mm(qk, w) for q, eg, qk, w in zip(qs, egs, qks, ws)]
    yield
    o_locals = [mm(qk, u) for qk, u in zip(qks, us)]
    yield
    mixes = [mm_tn(kd, w) for kd, w in zip(kds, ws)]
    yield
    adds = [mm_tn(kd, u) for kd, u in zip(kds, us)]
    return [(q_eff, o_loc, mix, add, jnp.exp(gl))
            for q_eff, o_loc, mix, add, gl in zip(q_effs, o_locals, mixes, adds, g_lasts)]


def _gdn_state_step(q_eff, o_local, mix, add, eg_last, state):
    return mm(q_eff, state) + o_local, state * eg_last - mm(mix, state) + add


def _gdn_post(o, z, norm_g):
    return _rmsnorm(o, norm_g) * _silu(z)


MASKED = -1e30


def _scores(q, kn, kr, q0, k0):
    s = (mm_nt(q[:, :128], kn) + mm_nt(q[:, 128:], kr)) * ATT_SCALE
    if k0 + kn.shape[0] <= q0:
        return s
    qpos = q0 + lax.broadcasted_iota(jnp.int32, s.shape, 0)
    kpos = k0 + lax.broadcasted_iota(jnp.int32, s.shape, 1)
    shift = int(math.log2(CHUNK))
    return jnp.where((kpos >> shift) <= (qpos >> shift), s, MASKED)


def _softmax_times(s, v):
    top = jnp.max(s, axis=-1, keepdims=True)
    p = jnp.exp(s - top)
    norm = jnp.sum(p, axis=-1, keepdims=True)
    return mm(p / norm, v), top + jnp.log(norm)


def _mla_prep(pm, qg, kvg, wq, wkv, cos, sin):
    cq = pm[:, 128:512]
    ckv = pm[:, 512:768]
    qf = mm(_rmsnorm(cq, qg), wq)
    parts = []
    for h in range(MLA_HEADS):
        parts.append(qf[:, h * 256:h * 256 + 128])
        parts.append(rope128(qf[:, h * 256 + 128:h * 256 + 256], cos, sin))
    kvf = mm(_rmsnorm(ckv, kvg), wkv)
    return jnp.concatenate(parts, axis=1), kvf, rope128(pm[:, 768:896], cos, sin)


def _ffn_act(ug, uu, wg, wu, bg, bu):
    return _silu(dwconv(ug, wg) + bg) * (dwconv(uu, wu) + bu)


def _head_loss(h1, ffn, gpre, pp, bgate, g2, b2, target):
    gate = jax.nn.sigmoid(gpre + bgate)
    h2 = _layernorm(ALPHA * h1 + ffn + gate * pp, g2, b2)
    err = h2 - target
    return 0.5 * jnp.sum(jnp.sum(err * err, axis=1, keepdims=True), axis=0, keepdims=True) / D_MODEL


ROW_TILE_VREGS = 32
CONV_HALO = 8


def _rows_per_tile(n_rows, cols):
    tile = min(n_rows, ROW_TILE_VREGS * 8 * LANES // cols)
    assert n_rows % tile == 0 and tile % CONV_HALO == 0, (n_rows, cols)
    return tile


def _tile_inputs(loads, t0, first, halo, tile):
    if halo == 0:
        return [ld(pl.ds(t0, tile)) for ld in loads]
    if first:
        xs = [ld(pl.ds(0, tile)) for ld in loads]
        return [jnp.concatenate([jnp.zeros((halo, x.shape[1]), x.dtype), x], axis=0) for x in xs]
    return [ld(pl.ds(pl.multiple_of(t0 - halo, CONV_HALO), tile + halo)) for ld in loads]


def _rows_apply(fn, loads, consts, store, n_rows, cols, halo):
    tile = _rows_per_tile(n_rows, cols)

    def one(t0, first):
        y = fn(*_tile_inputs(loads, t0, first, halo, tile), *consts)
        store(pl.ds(t0, tile), y[halo:] if halo else y)

    one(0, True)

    def step(i, carry):
        one(pl.multiple_of(i * tile, tile), False)
        return carry

    lax.fori_loop(1, n_rows // tile, step, 0)


def _rows_vjp(fn, loads, consts, load_dy, stores, n_rows, cols, halo):
    tile = _rows_per_tile(n_rows, cols)

    def one(t0, first, dconsts):
        xs = _tile_inputs(loads, t0, first, halo, tile)
        _, vjp = jax.vjp(lambda *a: fn(*a)[halo:] if halo else fn(*a), *xs, *consts)
        grads = vjp(load_dy(pl.ds(t0, tile)))
        for st, dx in zip(stores, grads[:len(xs)]):
            st(pl.ds(t0, tile), dx[halo:] if halo else dx, False)
            if halo and not first:
                st(pl.ds(pl.multiple_of(t0 - halo, CONV_HALO), halo), dx[:halo], True)
        return tuple(a + b for a, b in zip(dconsts, grads[len(xs):]))

    dconsts = one(0, True, tuple(jnp.zeros_like(c) for c in consts))
    return lax.fori_loop(1, n_rows // tile, lambda i, dc: one(pl.multiple_of(i * tile, tile), False, dc), dconsts)


def _params(sem):
    return pltpu.CompilerParams(dimension_semantics=sem, vmem_limit_bytes=VMEM_LIMIT)


def _matmul(a, b, *, name, ta=False, tb=False, tm=512, tn=512, tk=1024, add=None, add_scale=1.0,
            a_halves=False, b_parts=0, out_parts=0, ride=None, out_dtype=F32):
    assert not (a_halves and ta)
    a_shape = (a.shape[1], 2 * a.shape[2]) if a_halves else a.shape
    b_shape = (b.shape[1], b_parts * b.shape[2]) if b_parts else b.shape
    (k_dim, m) = a_shape if ta else a_shape[::-1]
    (n, k2) = b_shape if tb else b_shape[::-1]
    assert k_dim == k2, (a.shape, b.shape)
    tm, tn, tk = min(tm, m), min(tn, n), min(tk, k_dim)
    assert m % tm == 0 and n % tn == 0 and k_dim % tk == 0, (name, m, n, k_dim, tm, tn, tk)
    nk = k_dim // tk
    ca, cb = (0 if ta else 1), (1 if tb else 0)

    def body(*refs):
        if add is None:
            a_ref, b_ref, o_ref, acc = refs
        else:
            a_ref, b_ref, c_ref, o_ref, acc = refs
        kk = pl.program_id(2)

        @pl.when(kk == 0)
        def _():
            acc[...] = jnp.zeros_like(acc)

        acc[...] += _dot(a_ref[...], b_ref[...], ca, cb)

        @pl.when(kk == nk - 1)
        def _():
            r = acc[...]
            if add is not None:
                r = r + add_scale * c_ref[...]
            o_ref[...] = r.astype(out_dtype)

    def per_part(total, parts, tile):
        per = total // parts // tile
        assert per * tile * parts == total, (name, total, parts, tile)
        return per

    spec = pl.BlockSpec
    a_spec = spec((tk, tm), lambda i, j, k: (k, i)) if ta else spec((tm, tk), lambda i, j, k: (i, k))
    b_spec = spec((tn, tk), lambda i, j, k: (j, k)) if tb else spec((tk, tn), lambda i, j, k: (k, j))
    if a_halves:
        kh = per_part(k_dim, 2, tk)
        a_spec = spec((None, tm, tk), lambda i, j, k: (k // kh, i, k % kh))
    if b_parts and tb:
        kp = per_part(k_dim, b_parts, tk)
        b_spec = spec((None, tn, tk), lambda i, j, k: (k // kp, j, k % kp))
    elif b_parts:
        np_ = per_part(n, b_parts, tn)
        b_spec = spec((None, tk, tn), lambda i, j, k: (j // np_, k, j % np_))
    out_spec, out_shape = spec((tm, tn), lambda i, j, k: (i, j)), (m, n)
    if out_parts:
        op = per_part(n, out_parts, tn)
        out_spec, out_shape = spec((None, tm, tn), lambda i, j, k: (j // op, i, j % op)), (out_parts, m, n // out_parts)
    in_specs = [a_spec, b_spec]
    args = [a, b]
    if add is not None:
        in_specs.append(pl.BlockSpec((tm, tn), lambda i, j, k: (i, j)))
        args.append(add)
    grid = (m // tm, n // tn, nk)
    r_in, r_out, r_shapes, r_sems, r_args = _ride_specs(ride)
    outs = pl.pallas_call(
        _riding(body, len(args), 1, 1, ride, grid), name=name, grid=grid,
        in_specs=in_specs + r_in, out_specs=[out_spec] + r_out,
        out_shape=[jax.ShapeDtypeStruct(out_shape, out_dtype)] + r_shapes,
        scratch_shapes=[pltpu.VMEM((tm, tn), F32)] + r_sems,
        compiler_params=_params(("parallel", "parallel", "arbitrary") if ride is None else ("arbitrary",) * 3),
    )(*args, *r_args)
    return outs[0] if ride is None else (outs[0], list(outs[1:]))


def _riding(core, n_in, n_out, n_scratch, ride, grid):
    if ride is None:
        return core
    copies, nr = ride[1], len(ride[0])
    steps = int(np.prod(grid))
    pass_step = min(max(int(steps * ride[4]), 1), steps - 2)
    assert steps >= 3, grid

    def body(*refs):
        cuts = np.cumsum([0, n_in, nr, n_out, nr, n_scratch])
        ins, rin, outs, rout, scratch = (refs[a:b] for a, b in zip(cuts[:-1], cuts[1:]))
        sems = refs[cuts[-1]:]
        step = 0
        for axis, size in enumerate(grid):
            step = step * size + pl.program_id(axis)

        @pl.when(step == 0)
        def _():
            _exchange_start(copies(rin, rout, sems))

        @pl.when(step == pass_step)
        def _():
            _exchange_pass_on(copies(rin, rout, sems))

        core(*ins, *outs, *scratch)

        @pl.when(step == steps - 1)
        def _():
            _exchange_finish(copies(rin, rout, sems))

    return body


def _ride_specs(ride):
    if ride is None:
        return [], [], [], [], []
    arrays, _, sems, shapes, _ = ride
    return [HBM_REF] * len(arrays), [HBM_REF] * len(arrays), list(shapes), sems(len(arrays)), list(arrays)


def _gdn_fwd(proj, conv_w, sc, norm_g, bl, s, ride=None):
    nc = s // CHUNK

    def core(ph_ref, ab_ref, cwq_ref, cwk_ref, cwv_ref, sc_ref, ng_ref, cat_ref, o_ref, st_ref, q_s, k_s, v_s, gb_s):
        def into(ref):
            def store(rows, value):
                ref[rows, :] = value.astype(ref.dtype)
            return store

        for fn, col, cw_ref, val_s in [(_gdn_q, 0, cwq_ref, q_s), (_gdn_k, 128, cwk_ref, k_s), (_gdn_v, 256, cwv_ref, v_s)]:
            _rows_apply(fn, [lambda r, col=col: ph_ref[r, col:col + 128]], [cw_ref[...]], into(val_s), s, LANES, CONV_HALO)
        _rows_apply(functools.partial(_gdn_gate, head=pl.program_id(1)), [lambda r: ab_ref[r, :]], [sc_ref[...]], into(gb_s), s, LANES, 0)

        group = math.gcd(nc, GDN_FWD_GROUP)

        def rows_of(n):
            return slice(n * CHUNK, (n + 1) * CHUNK)

        def levels_of(g):
            rows = [rows_of(g * group + j) for j in range(group)]
            return _gdn_local_levels([q_s[r, :] for r in rows], [k_s[r, :] for r in rows], [v_s[r, :] for r in rows],
                                     [gb_s[r, :] for r in rows], _solve_levels)

        local, state = _run(levels_of(0)), jnp.zeros((GDN_DK, GDN_DV), F32)
        for g in range(nc // group):
            ahead = levels_of(g + 1) if g + 1 < nc // group else None
            following = None
            for j, loc in enumerate(local):
                n = g * group + j
                st_ref[n] = state
                o_ref[rows_of(n), :], state = _gdn_state_step(*loc, state)
                if ahead is not None and following is None:
                    try:
                        next(ahead)
                    except StopIteration as stop:
                        following = stop.value
            if ahead is not None and following is None:
                following = _run(ahead)
            local = following
        _rows_apply(_gdn_post, [lambda r: o_ref[r, :], lambda r: ph_ref[r, 384:512]], [ng_ref[...]], into(cat_ref), s, LANES, 0)

    t = bl * s
    r_in, r_out, r_shapes, r_sems, r_args = _ride_specs(ride)
    outs = pl.pallas_call(
        _riding(core, 7, 7, 0, ride, (bl, GDN_HEADS)), name="gdn_fwd", grid=(bl, GDN_HEADS),
        in_specs=[
            pl.BlockSpec((s, 512), lambda b, h: (b, h)),
            pl.BlockSpec((s, 128), lambda b, h: (b, P_MLA // 128)),
            pl.BlockSpec((GDN_CONV, 128), lambda b, h: (0, h)),
            pl.BlockSpec((GDN_CONV, 128), lambda b, h: (0, GDN_HEADS + h)),
            pl.BlockSpec((GDN_CONV, 128), lambda b, h: (0, 2 * GDN_HEADS + h)),
            pl.BlockSpec((8, 128), lambda b, h: (0, 0)),
            pl.BlockSpec((1, 128), lambda b, h: (0, 0)),
        ] + r_in,
        out_specs=[
            pl.BlockSpec((s, 128), lambda b, h: (b, h)),
            pl.BlockSpec((s, 128), lambda b, h: (b, h)),
            pl.BlockSpec((None, None, nc, GDN_DK, GDN_DV), lambda b, h: (b, h, 0, 0, 0)),
        ] + [pl.BlockSpec((s, 128), lambda b, h: (b, h))] * 4 + r_out,
        out_shape=[
            jax.ShapeDtypeStruct((t, 2 * GDN_VW), BF16),
            jax.ShapeDtypeStruct((t, GDN_VW), F32),
            jax.ShapeDtypeStruct((bl, GDN_HEADS, nc, GDN_DK, GDN_DV), F32),
        ] + [jax.ShapeDtypeStruct((t, GDN_VW), F32)] * 4 + r_shapes,
        scratch_shapes=r_sems,
        compiler_params=_params(("arbitrary", "arbitrary")),
    )(proj, proj, conv_w, conv_w, conv_w, sc, norm_g, *r_args)
    return outs[0], outs[1], outs[2], tuple(outs[3:7]), list(outs[7:])


def _gdn_bwd(proj, conv_w, sc, norm_g, o_raw, states, qkvg, dcat, bl, s, ride=None):
    nc = s // CHUNK

    def core(ph_ref, ab_ref, cwq_ref, cwk_ref, cwv_ref, sc_ref, ng_ref, o_ref, st_ref, dc_ref, q_in, k_in, v_in, gb_in,
             dph_ref, dab_ref, dcwq_ref, dcwk_ref, dcwv_ref, dsc_ref, dng_ref, q_s, k_s, v_s, gb_s, do_s):
        head = pl.program_id(1)
        gate = functools.partial(_gdn_gate, head=head)
        paths = [(_gdn_q, 0, cwq_ref, q_s, dcwq_ref), (_gdn_k, 128, cwk_ref, k_s, dcwk_ref), (_gdn_v, 256, cwv_ref, v_s, dcwv_ref)]
        def into(ref, cols=slice(None)):
            def store(rows, value, add=False):
                if add:
                    ref[rows, cols] += value.astype(ref.dtype)
                else:
                    ref[rows, cols] = value.astype(ref.dtype)
            return store

        (dng,) = _rows_vjp(_gdn_post, [lambda r: o_ref[r, :], lambda r: ph_ref[r, 384:512]], [ng_ref[...]],
                           lambda r: dc_ref[r, :], [into(do_s), into(dph_ref, slice(384, 512))], s, LANES, 0)
        dng_ref[...] = jnp.broadcast_to(dng, dng_ref.shape)

        group = math.gcd(nc, GDN_BWD_GROUP)

        def chunks(i, dstate):
            ns = [nc - 1 - (i * group + j) for j in range(group)]
            rows = [pl.ds(pl.multiple_of(n * CHUNK, CHUNK), CHUNK) for n in ns]
            local, local_vjp = jax.vjp(_gdn_local_group, [q_in[r, :] for r in rows], [k_in[r, :] for r in rows],
                                       [v_in[r, :] for r in rows], [gb_in[r, :] for r in rows])
            d_os = [do_s[r, :] for r in rows]
            dlocal = []
            for n, loc, d_o in zip(ns, local, d_os):
                _, step_vjp = jax.vjp(_gdn_state_step, *loc, st_ref[n])
                *dloc, dstate = step_vjp((d_o, dstate))
                dlocal.append(tuple(dloc))
            dqs, dks, dvs, dgbs = local_vjp(dlocal)
            for r, dq, dk, dv, dgb in zip(rows, dqs, dks, dvs, dgbs):
                q_s[r, :], k_s[r, :], v_s[r, :], gb_s[r, :] = dq, dk, dv, dgb
            return dstate

        lax.fori_loop(0, nc // group, chunks, jnp.zeros((GDN_DK, GDN_DV), F32))
        for fn, col, cw_ref, val_s, dcw_ref in paths:
            (dcw_ref[...],) = _rows_vjp(fn, [lambda r, col=col: ph_ref[r, col:col + 128]], [cw_ref[...]],
                                        lambda r, val_s=val_s: val_s[r, :], [into(val_s)], s, LANES, CONV_HALO)
            dph_ref[:, col:col + 128] = val_s[...].astype(BF16)

        @pl.when(head == 0)
        def _():
            dab_ref[...] = jnp.zeros_like(dab_ref)

        def add_dab(rows, value, add=False):
            dab_ref[rows, :] += value

        (dsc_ref[...],) = _rows_vjp(gate, [lambda r: ab_ref[r, :]], [sc_ref[...]], lambda r: gb_s[r, :], [add_dab], s, LANES, 0)

    t = bl * s
    cw_out = pl.BlockSpec((None, GDN_CONV, 128), lambda b, h: (b, 0, h))
    part = pl.BlockSpec((None, None, 8, 128), lambda b, h: (b, h, 0, 0))
    r_in, r_out, r_shapes, r_sems, r_args = _ride_specs(ride)
    outs = pl.pallas_call(
        _riding(core, 14, 7, 5, ride, (bl, GDN_HEADS)), name="gdn_bwd", grid=(bl, GDN_HEADS),
        in_specs=[
            pl.BlockSpec((s, 512), lambda b, h: (b, h)),
            pl.BlockSpec((s, 128), lambda b, h: (b, P_MLA // 128)),
            pl.BlockSpec((GDN_CONV, 128), lambda b, h: (0, h)),
            pl.BlockSpec((GDN_CONV, 128), lambda b, h: (0, GDN_HEADS + h)),
            pl.BlockSpec((GDN_CONV, 128), lambda b, h: (0, 2 * GDN_HEADS + h)),
            pl.BlockSpec((8, 128), lambda b, h: (0, 0)),
            pl.BlockSpec((1, 128), lambda b, h: (0, 0)),
            pl.BlockSpec((s, 128), lambda b, h: (b, h)),
            pl.BlockSpec((None, None, nc, GDN_DK, GDN_DV), lambda b, h: (b, h, 0, 0, 0)),
        ] + [pl.BlockSpec((s, 128), lambda b, h: (b, h))] * 5 + r_in,
        out_specs=[
            pl.BlockSpec((s, 512), lambda b, h: (b, h)),
            pl.BlockSpec((s, 128), lambda b, h: (b, 0)),
            cw_out, cw_out, cw_out, part, part,
        ] + r_out,
        out_shape=[
            jax.ShapeDtypeStruct((t, P_WIDTH), BF16),
            jax.ShapeDtypeStruct((t, 128), F32),
            jax.ShapeDtypeStruct((bl, GDN_CONV, 512), F32),
            jax.ShapeDtypeStruct((bl, GDN_CONV, 512), F32),
            jax.ShapeDtypeStruct((bl, GDN_CONV, 512), F32),
            jax.ShapeDtypeStruct((bl, GDN_HEADS, 8, 128), F32),
            jax.ShapeDtypeStruct((bl, GDN_HEADS, 8, 128), F32),
        ] + r_shapes,
        scratch_shapes=[pltpu.VMEM((s, 128), F32)] * 5 + r_sems,
        compiler_params=_params(("arbitrary", "arbitrary")),
    )(proj, proj, conv_w, conv_w, conv_w, sc, norm_g, o_raw, states, dcat, *qkvg, *r_args)
    return tuple(outs[:7]) + (list(outs[7:]),)


def _mla_prep_fwd(proj, qg, kvg, wq, wkv, cos, sin, s, tm):
    t = proj.shape[0]
    tm = min(tm, s)
    nps = s // tm
    const = lambda shape: pl.BlockSpec(shape, lambda i: (0, 0))

    def body(pm_ref, qg_ref, kvg_ref, wq_ref, wkv_ref, cos_ref, sin_ref, qf_ref, kvf_ref, kr_ref):
        qf, kvf, kr = _mla_prep(pm_ref[...], qg_ref[...], kvg_ref[...], wq_ref[...], wkv_ref[...], cos_ref[...], sin_ref[...])
        qf_ref[...], kvf_ref[...], kr_ref[...] = qf.astype(BF16), kvf.astype(BF16), kr.astype(BF16)

    return pl.pallas_call(
        body, name="mla_prep_fwd", grid=(t // tm,),
        in_specs=[
            pl.BlockSpec((tm, 1024), lambda i: (i, P_MLA // 1024)),
            const((1, MLA_Q_LORA)), const((1, MLA_KV_LORA)), const(wq.shape), const(wkv.shape),
            pl.BlockSpec((tm, 128), lambda i: (i % nps, 0)), pl.BlockSpec((tm, 128), lambda i: (i % nps, 0)),
        ],
        out_specs=[pl.BlockSpec((tm, 1024), lambda i: (i, 0)), pl.BlockSpec((tm, 1024), lambda i: (i, 0)),
                   pl.BlockSpec((tm, 128), lambda i: (i, 0))],
        out_shape=[jax.ShapeDtypeStruct((t, 1024), BF16), jax.ShapeDtypeStruct((t, 1024), BF16),
                   jax.ShapeDtypeStruct((t, 128), BF16)],
        compiler_params=_params(("parallel",)),
    )(proj, qg, kvg, wq, wkv, cos, sin)


def _mla_prep_bwd(proj, qg, kvg, wq, wkv, cos, sin, dqf, dkvf, dkr, dab, dproj, s, tm):
    t = proj.shape[0]
    tm = min(tm, s)
    nps = s // tm
    const = lambda shape: pl.BlockSpec(shape, lambda i: (0, 0))

    def body(pm_ref, qg_ref, kvg_ref, wq_ref, wkv_ref, cos_ref, sin_ref, dqf_ref, dkvf_ref, dkr_ref, dab_ref, dp_in,
             dp_ref, dqg_ref, dkvg_ref, dwq_ref, dwkv_ref):
        del dp_in
        fn = lambda pm, qg_, kvg_, wq_, wkv_: _mla_prep(pm, qg_, kvg_, wq_, wkv_, cos_ref[...], sin_ref[...])
        _, vjp = jax.vjp(fn, pm_ref[...], qg_ref[...], kvg_ref[...], wq_ref[...].astype(F32), wkv_ref[...].astype(F32))
        dpm, dqg, dkvg, dwq, dwkv = vjp((dqf_ref[...], dkvf_ref[...], dkr_ref[...]))
        dp_ref[...] = jnp.concatenate([dab_ref[...], dpm[:, 128:]], axis=1).astype(BF16)

        @pl.when(pl.program_id(0) == 0)
        def _():
            dqg_ref[...] = jnp.zeros_like(dqg_ref)
            dkvg_ref[...] = jnp.zeros_like(dkvg_ref)
            dwq_ref[...] = jnp.zeros_like(dwq_ref)
            dwkv_ref[...] = jnp.zeros_like(dwkv_ref)

        dqg_ref[...] += dqg
        dkvg_ref[...] += dkvg
        dwq_ref[...] += dwq
        dwkv_ref[...] += dwkv

    rows = lambda w: pl.BlockSpec((tm, w), lambda i: (i, 0))
    return pl.pallas_call(
        body, name="mla_prep_bwd", grid=(t // tm,),
        in_specs=[
            pl.BlockSpec((tm, 1024), lambda i: (i, P_MLA // 1024)),
            const((1, MLA_Q_LORA)), const((1, MLA_KV_LORA)), const(wq.shape), const(wkv.shape),
            pl.BlockSpec((tm, 128), lambda i: (i % nps, 0)), pl.BlockSpec((tm, 128), lambda i: (i % nps, 0)),
            rows(1024), rows(1024), rows(128), rows(128),
            pl.BlockSpec(memory_space=pl.ANY),
        ],
        out_specs=[pl.BlockSpec((tm, 1024), lambda i: (i, P_MLA // 1024)),
                   const((1, MLA_Q_LORA)), const((1, MLA_KV_LORA)), const(wq.shape), const(wkv.shape)],
        out_shape=[jax.ShapeDtypeStruct(dproj.shape, dproj.dtype),
                   jax.ShapeDtypeStruct((1, MLA_Q_LORA), F32), jax.ShapeDtypeStruct((1, MLA_KV_LORA), F32),
                   jax.ShapeDtypeStruct(wq.shape, F32), jax.ShapeDtypeStruct(wkv.shape, F32)],
        input_output_aliases={11: 0},
        compiler_params=_params(("arbitrary",)),
    )(proj, qg, kvg, wq, wkv, cos, sin, dqf, dkvf, dkr, dab, dproj)


def _attn_fwd(qf, kvf, kr, cat, bl, s, tq):
    tq = min(tq, s)
    nq = s // tq

    def body(q_ref, kv_ref, kr_ref, cat_in, o_ref, o32_ref, lse_ref):
        del cat_in

        def scores_of(i):
            keys = slice(0, (i + 1) * tq)
            return _scores(q_ref[i * tq:(i + 1) * tq, :], kv_ref[keys, 0:128], kr_ref[keys, :], i * tq, 0)

        ready = scores_of(0)
        for i in range(nq):
            scores = ready
            if i + 1 < nq:
                ready = scores_of(i + 1)
            rows = slice(i * tq, (i + 1) * tq)
            o, lse = _softmax_times(scores, kv_ref[0:(i + 1) * tq, 128:256])
            o_ref[rows, :] = o.astype(o_ref.dtype)
            o32_ref[rows, :] = o
            lse_ref[rows, :] = jnp.broadcast_to(lse, o.shape)

    t = bl * s
    head_cols = pl.BlockSpec((s, 128), lambda b, h: (b, h))
    return pl.pallas_call(
        body, name="attn_fwd", grid=(bl, MLA_HEADS),
        in_specs=[
            pl.BlockSpec((s, 256), lambda b, h: (b, h)),
            pl.BlockSpec((s, 256), lambda b, h: (b, h)),
            pl.BlockSpec((s, 128), lambda b, h: (b, 0)),
            pl.BlockSpec(memory_space=pl.ANY),
        ],
        out_specs=[pl.BlockSpec((s, 128), lambda b, h: (b, GDN_HEADS + h)), head_cols, head_cols],
        out_shape=[jax.ShapeDtypeStruct(cat.shape, cat.dtype)] + [jax.ShapeDtypeStruct((t, MLA_HEADS * MLA_V), F32)] * 2,
        input_output_aliases={3: 0},
        compiler_params=_params(("parallel", "parallel")),
    )(qf, kvf, kr, cat)


def _attn_bwd(qf, kvf, kr, dcat, o32, lse, bl, s, tq):
    tq = min(tq, s)
    nq = s // tq

    def body(q_ref, kv_ref, kr_ref, do_ref, o_ref, lse_ref, dq_ref, dkv_ref, dkr_ref):
        dkv_ref[...] = jnp.zeros_like(dkv_ref)

        @pl.when(pl.program_id(1) == 0)
        def _():
            dkr_ref[...] = jnp.zeros_like(dkr_ref)

        def block(i):
            rows = slice(i * tq, (i + 1) * tq)
            return q_ref[rows, :], do_ref[rows, :]

        def first_products(i, j):
            (q, d_o), keys = block(i), slice(j * tq, (j + 1) * tq)
            return _scores(q, kv_ref[keys, 0:128], kr_ref[keys, :], i * tq, j * tq), mm_nt(d_o, kv_ref[keys, 128:256])

        tiles = [(i, j) for i in range(nq) for j in range(i + 1)]
        ready = first_products(*tiles[0])
        for t, (i, j) in enumerate(tiles):
            scores, dp = ready
            if t + 1 < len(tiles):
                ready = first_products(*tiles[t + 1])
            rows, keys = slice(i * tq, (i + 1) * tq), slice(j * tq, (j + 1) * tq)
            q, d_o = block(i)
            if j == 0:
                delta = jnp.sum(d_o * o_ref[rows, :], axis=-1, keepdims=True)
                lse_i = jnp.max(lse_ref[rows, :], axis=-1, keepdims=True)
                dqn, dqr = jnp.zeros((tq, MLA_NOPE), F32), jnp.zeros((tq, 128), F32)
            p = jnp.exp(scores - lse_i)
            ds = p * (dp - delta) * ATT_SCALE
            dkv_ref[keys, 128:256] += mm_tn(p, d_o)
            dkv_ref[keys, 0:128] += mm_tn(ds, q[:, :128])
            dkr_ref[keys, :] += mm_tn(ds, q[:, 128:])
            dqn = dqn + mm(ds, kv_ref[keys, 0:128])
            dqr = dqr + mm(ds, kr_ref[keys, :])
            if j == i:
                dq_ref[rows, 0:128], dq_ref[rows, 128:256] = dqn, dqr

    t = bl * s
    head_cols = pl.BlockSpec((s, 128), lambda b, h: (b, h))
    return pl.pallas_call(
        body, name="attn_bwd", grid=(bl, MLA_HEADS),
        in_specs=[
            pl.BlockSpec((s, 256), lambda b, h: (b, h)),
            pl.BlockSpec((s, 256), lambda b, h: (b, h)),
            pl.BlockSpec((s, 128), lambda b, h: (b, 0)),
            pl.BlockSpec((s, 128), lambda b, h: (b, GDN_HEADS + h)),
            head_cols, head_cols,
        ],
        out_specs=[
            pl.BlockSpec((s, 256), lambda b, h: (b, h)),
            pl.BlockSpec((s, 256), lambda b, h: (b, h)),
            pl.BlockSpec((s, 128), lambda b, h: (b, 0)),
        ],
        out_shape=[jax.ShapeDtypeStruct((t, 1024), F32), jax.ShapeDtypeStruct((t, 1024), F32),
                   jax.ShapeDtypeStruct((t, 128), F32)],
        compiler_params=_params(("parallel", "arbitrary")),
    )(qf, kvf, kr, dcat, o32, lse)


def _to_bf16(x, tm):
    t, d = x.shape
    tm = min(tm, t)

    def body(x_ref, o_ref):
        o_ref[...] = x_ref[...].astype(BF16)

    rows = pl.BlockSpec((tm, d), lambda i: (i, 0))
    return pl.pallas_call(
        body, name="x_to_bf16", grid=(t // tm,), in_specs=[rows], out_specs=rows,
        out_shape=jax.ShapeDtypeStruct(x.shape, BF16), compiler_params=_params(("parallel",)),
    )(x)


def _mix_ln1_fwd(x, cat, w_o, g, b, tm):
    t = x.shape[0]
    tm = min(tm, t)

    def body(x_ref, cat_ref, w_ref, g_ref, b_ref, r_ref, h_ref, hb_ref):
        r = ALPHA * x_ref[...] + _dot(cat_ref[...], w_ref[...], 1, 0)
        r_ref[...] = r
        h = _layernorm(r, g_ref[...], b_ref[...])
        h_ref[...] = h
        hb_ref[...] = h.astype(BF16)

    rows = pl.BlockSpec((tm, D_MODEL), lambda i: (i, 0))
    vec = pl.BlockSpec((1, D_MODEL), lambda i: (0, 0))
    return pl.pallas_call(
        body, name="mix_ln1_fwd", grid=(t // tm,),
        in_specs=[rows, pl.BlockSpec((tm, cat.shape[1]), lambda i: (i, 0)), pl.BlockSpec(w_o.shape, lambda i: (0, 0)), vec, vec],
        out_specs=[rows] * 3,
        out_shape=[jax.ShapeDtypeStruct(x.shape, F32)] * 2 + [jax.ShapeDtypeStruct(x.shape, BF16)],
        compiler_params=_params(("parallel",)),
    )(x, cat, w_o, g, b)


def _ln1_bwd(r1, dr2, da, dgpre, w_gate, w_o, g, b, tm):
    t = r1.shape[0]
    tm = min(tm, t)

    def body(r_ref, d2_ref, da_ref, dgp_ref, wg_ref, wo_ref, g_ref, b_ref, dr_ref, drb_ref, dcat_ref, dg_ref, dbias_ref):
        dh = ALPHA * d2_ref[...] + da_ref[...] + _dot(dgp_ref[...], wg_ref[...], 1, 1)
        _, vjp = jax.vjp(_layernorm, r_ref[...], g_ref[...], b_ref[...])
        dr, dg, dbias = vjp(dh)
        dr_ref[...] = dr
        drb_ref[...] = dr.astype(BF16)
        dcat_ref[...] = _dot(dr, wo_ref[...], 1, 1)

        @pl.when(pl.program_id(0) == 0)
        def _():
            dg_ref[...] = jnp.zeros_like(dg_ref)
            dbias_ref[...] = jnp.zeros_like(dbias_ref)

        dg_ref[...] += dg
        dbias_ref[...] += dbias

    rows = pl.BlockSpec((tm, D_MODEL), lambda i: (i, 0))
    vec = pl.BlockSpec((1, D_MODEL), lambda i: (0, 0))
    return pl.pallas_call(
        body, name="ln1_bwd", grid=(t // tm,),
        in_specs=[rows] * 4 + [pl.BlockSpec(w_gate.shape, lambda i: (0, 0)), pl.BlockSpec(w_o.shape, lambda i: (0, 0)), vec, vec],
        out_specs=[rows, rows, pl.BlockSpec((tm, w_o.shape[0]), lambda i: (i, 0)), vec, vec],
        out_shape=[jax.ShapeDtypeStruct(r1.shape, F32), jax.ShapeDtypeStruct(r1.shape, BF16),
                   jax.ShapeDtypeStruct((t, w_o.shape[0]), F32)] + [jax.ShapeDtypeStruct((1, D_MODEL), F32)] * 2,
        compiler_params=_params(("arbitrary",)),
    )(r1, dr2, da, dgpre, w_gate, w_o, g, b)


def _ffn_act_fwd(u, conv_w, conv_b, bl, s, cb):
    nj = D_FF // cb

    def body(ug_ref, uu_ref, wg_ref, wu_ref, bg_ref, bu_ref, act_ref):
        def store(rows, act):
            act_ref[rows, :] = act.astype(BF16)

        _rows_apply(_ffn_act, [lambda r: ug_ref[r, :], lambda r: uu_ref[r, :]],
                    [wg_ref[...], wu_ref[...], bg_ref[...], bu_ref[...]], store, s, cb, CONV_HALO)

    return pl.pallas_call(
        body, name="ffn_act_fwd", grid=(bl, nj),
        in_specs=[
            pl.BlockSpec((s, cb), lambda b, j: (b, j)), pl.BlockSpec((s, cb), lambda b, j: (b, nj + j)),
            pl.BlockSpec((FFN_CONV, cb), lambda b, j: (0, j)), pl.BlockSpec((FFN_CONV, cb), lambda b, j: (0, nj + j)),
            pl.BlockSpec((1, cb), lambda b, j: (0, j)), pl.BlockSpec((1, cb), lambda b, j: (0, nj + j)),
        ],
        out_specs=pl.BlockSpec((s, cb), lambda b, j: (b, j)),
        out_shape=jax.ShapeDtypeStruct((bl * s, D_FF), BF16),
        compiler_params=_params(("parallel", "parallel")),
    )(u, u, conv_w, conv_w, conv_b, conv_b)


def _ffn_act_bwd(u, conv_w, conv_b, dact, bl, s, cb):
    nj = D_FF // cb

    def body(ug_ref, uu_ref, wg_ref, wu_ref, bg_ref, bu_ref, da_ref, du_ref, dwg_ref, dwu_ref, dbg_ref, dbu_ref, acc):
        def store_into(half):
            def store(rows, value, add):
                if add:
                    acc[half, rows, :] += value
                else:
                    acc[half, rows, :] = value
            return store

        dwg_ref[...], dwu_ref[...], dbg_ref[...], dbu_ref[...] = _rows_vjp(
            _ffn_act, [lambda r: ug_ref[r, :], lambda r: uu_ref[r, :]], [wg_ref[...], wu_ref[...], bg_ref[...], bu_ref[...]],
            lambda r: da_ref[r, :], [store_into(0), store_into(1)], s, cb, CONV_HALO)
        du_ref[...] = acc[...].astype(BF16)

    t = bl * s
    blk = pl.BlockSpec((s, cb), lambda b, j: (b, j))
    wpart = pl.BlockSpec((None, FFN_CONV, cb), lambda b, j: (b, 0, j))
    bpart = pl.BlockSpec((None, 1, cb), lambda b, j: (b, 0, j))
    return pl.pallas_call(
        body, name="ffn_act_bwd", grid=(bl, nj),
        in_specs=[
            blk, pl.BlockSpec((s, cb), lambda b, j: (b, nj + j)),
            pl.BlockSpec((FFN_CONV, cb), lambda b, j: (0, j)), pl.BlockSpec((FFN_CONV, cb), lambda b, j: (0, nj + j)),
            pl.BlockSpec((1, cb), lambda b, j: (0, j)), pl.BlockSpec((1, cb), lambda b, j: (0, nj + j)),
            blk,
        ],
        out_specs=[pl.BlockSpec((2, s, cb), lambda b, j: (0, b, j)), wpart, wpart, bpart, bpart],
        out_shape=[jax.ShapeDtypeStruct((2, t, D_FF), BF16)] + [jax.ShapeDtypeStruct((bl, FFN_CONV, D_FF), F32)] * 2
        + [jax.ShapeDtypeStruct((bl, 1, D_FF), F32)] * 2,
        scratch_shapes=[pltpu.VMEM((2, s, cb), F32)],
        compiler_params=_params(("parallel", "parallel")),
    )(u, u, conv_w, conv_w, conv_b, conv_b, dact)


def _head(h1, h1b, act, pt, w_down, w_gate, w_proj, bgate, g2, b2, target, tm):
    t = h1.shape[0]
    tm = min(tm, t)

    def body(h1_ref, h1b_ref, act_ref, pt_ref, wd_ref, wg_ref, wp_ref, bg_ref, g2_ref, b2_ref, tg_ref,
             dr_ref, drb_ref, dgp_ref, dpp_ref, loss_ref, dbg_ref, dg2_ref, db2_ref):
        fn = functools.partial(_head_loss, target=tg_ref[...])
        ffn = _dot(act_ref[...], wd_ref[...], 1, 0)
        gpre, pp = _dot(h1b_ref[...], wg_ref[...], 1, 0), _dot(pt_ref[...], wp_ref[...], 1, 0)
        loss, vjp = jax.vjp(fn, h1_ref[...], ffn, gpre, pp, bg_ref[...], g2_ref[...], b2_ref[...])
        _, dffn, dgp, dpp, dbg, dg2, db2 = vjp(jnp.ones((1, 1), F32))
        dr_ref[...] = dffn
        drb_ref[...], dgp_ref[...], dpp_ref[...] = dffn.astype(BF16), dgp.astype(BF16), dpp.astype(BF16)

        @pl.when(pl.program_id(0) == 0)
        def _():
            loss_ref[...] = jnp.zeros_like(loss_ref)
            dbg_ref[...] = jnp.zeros_like(dbg_ref)
            dg2_ref[...] = jnp.zeros_like(dg2_ref)
            db2_ref[...] = jnp.zeros_like(db2_ref)

        loss_ref[...] += jnp.broadcast_to(loss, loss_ref.shape)
        dbg_ref[...] += dbg
        dg2_ref[...] += dg2
        db2_ref[...] += db2

    rows = pl.BlockSpec((tm, D_MODEL), lambda i: (i, 0))
    vec = pl.BlockSpec((1, D_MODEL), lambda i: (0, 0))
    return pl.pallas_call(
        body, name="head", grid=(t // tm,),
        in_specs=[rows, rows, pl.BlockSpec((tm, act.shape[1]), lambda i: (i, 0)), pl.BlockSpec((tm, pt.shape[1]), lambda i: (i, 0))]
        + [pl.BlockSpec(w.shape, lambda i: (0, 0)) for w in (w_down, w_gate, w_proj)] + [vec] * 3 + [rows],
        out_specs=[rows] * 4 + [pl.BlockSpec((8, 128), lambda i: (0, 0))] + [vec] * 3,
        out_shape=[jax.ShapeDtypeStruct(h1.shape, F32)] + [jax.ShapeDtypeStruct(h1.shape, BF16)] * 3
        + [jax.ShapeDtypeStruct((8, 128), F32)]
        + [jax.ShapeDtypeStruct((1, D_MODEL), F32)] * 3,
        compiler_params=_params(("arbitrary",)),
    )(h1, h1b, act, pt, w_down, w_gate, w_proj, bgate, g2, b2, target)


def _adam_update(g, w_ref, m_ref, v_ref, g_ref, d_ref, nm_ref, nv_ref):
    m2 = ADAM_B1 * m_ref[...] + (1.0 - ADAM_B1) * g
    v2 = ADAM_B2 * v_ref[...] + (1.0 - ADAM_B2) * jnp.square(g)
    m_hat = m2 / (1.0 - ADAM_B1 ** ADAM_STEP)
    v_hat = v2 / (1.0 - ADAM_B2 ** ADAM_STEP)
    g_ref[...] = g
    d_ref[...] = -ADAM_LR * (m_hat / (jnp.sqrt(v_hat) + ADAM_EPS) + ADAM_WD * w_ref[...])
    nm_ref[...] = m2
    nv_ref[...] = v2


def _row_tile(rows, cols, limit_bytes=1024 * 1024):
    best = None
    for t in range(HALF_ROWS_QUANTUM, rows + 1, HALF_ROWS_QUANTUM):
        if rows % t == 0 and t * cols * 4 <= limit_bytes:
            best = t
    return best or rows


def _adamw_reduced(recv, w, m, v, name):
    a, b = w.shape
    ta = _row_tile(a, b)

    def body(recv_ref, w_ref, m_ref, v_ref, g_ref, d_ref, nm_ref, nv_ref):
        c = lax.axis_index("c")
        for core in range(2):
            @pl.when(c == core)
            def _():
                got = [recv_ref[k].astype(F32) for k in range(N_DEV)]
                same = [got[7], got[0], got[1], got[2]]
                other = got[3:7]
                core0, core1 = (same, other) if core == 0 else (other, same)
                g = core0[0] + core1[0]
                for r in range(1, N_CHIPS):
                    g = (g + core0[r]) + core1[r]
                _adam_update(g, w_ref, m_ref, v_ref, g_ref, d_ref, nm_ref, nv_ref)

    blk = pl.BlockSpec((ta, b), lambda i: (i, 0))
    return pl.pallas_call(
        body, name=name, grid=(a // ta,),
        in_specs=[pl.BlockSpec((N_DEV, ta, b), lambda i: (0, i, 0)), blk, blk, blk], out_specs=[blk] * 4,
        out_shape=[jax.ShapeDtypeStruct(w.shape, F32)] * 4, compiler_params=_params(("parallel",)),
    )(recv, w, m, v)


def _adamw_small(g, w, m, v):
    def body(g_in, w_ref, m_ref, v_ref, g_ref, d_ref, nm_ref, nv_ref):
        _adam_update(g_in[...], w_ref, m_ref, v_ref, g_ref, d_ref, nm_ref, nv_ref)

    blk = pl.BlockSpec(w.shape, lambda i: (0, 0))
    return pl.pallas_call(
        body, name="adamw_small", grid=(1,), in_specs=[blk] * 4, out_specs=[blk] * 4,
        out_shape=[jax.ShapeDtypeStruct(w.shape, F32)] * 4, compiler_params=_params(("arbitrary",)),
    )(g, w, m, v)


def _remote(src, dst, send_sem, recv_sem, device):
    return pltpu.make_async_remote_copy(src_ref=src, dst_ref=dst, send_sem=send_sem, recv_sem=recv_sem,
                                        device_id=device, device_id_type=MESH)


def _place():
    x, y, c = lax.axis_index("x"), lax.axis_index("y"), lax.axis_index("c")
    return x, y, c, 2 * x + y, [(1 - x, y), (x, 1 - y), (1 - x, 1 - y)]


HBM_REF = pl.BlockSpec(memory_space=pl.ANY)
HALF_ROWS_QUANTUM = 16


def _gather_sems(n):
    return [pltpu.SemaphoreType.DMA((3 * n,))] * 4 + [pltpu.SemaphoreType.DMA((n,))]


def _gather_copies(ins, outs, sems):
    send_s, recv_s, fsend_s, frecv_s, local_s = sems
    x, y, c, me, chips = _place()
    local, sends, steps = [], [], []
    for i, (src, dst) in enumerate(zip(ins, outs)):
        local.append(pltpu.make_async_copy(src, dst.at[me], local_s.at[i]))
        half = src.shape[0] // 2
        split = src.shape[0] % (2 * HALF_ROWS_QUANTUM) == 0
        if split:
            mine = pl.ds(pl.multiple_of(c * half, HALF_ROWS_QUANTUM), half)
            theirs = pl.ds(pl.multiple_of((1 - c) * half, HALF_ROWS_QUANTUM), half)
        for r, (px, py) in enumerate(chips):
            k, peer = 3 * i + r, 2 * px + py
            if split:
                sends.append(_remote(src.at[mine], dst.at[me, mine], send_s.at[k], recv_s.at[k], (px, py, c)))
                landed = dst.at[peer, mine]
                steps.append((_remote(src.at[mine], landed, send_s.at[k], recv_s.at[k], (px, py, c)),
                              _remote(landed, landed, fsend_s.at[k], frecv_s.at[k], (x, y, 1 - c)),
                              _remote(dst.at[peer, theirs], dst.at[peer, theirs], fsend_s.at[k], frecv_s.at[k], (x, y, 1 - c))))
            else:
                sends.append(_remote(src, dst.at[me], send_s.at[k], recv_s.at[k], (px, py, c)))
                steps.append((_remote(src, dst.at[peer], send_s.at[k], recv_s.at[k], (px, py, c)), None, None))
    return local, sends, steps


def _scatter_sems(n):
    return [pltpu.SemaphoreType.DMA((4 * n,))] * 2 + [pltpu.SemaphoreType.DMA((3 * n,))] * 2 + [pltpu.SemaphoreType.DMA((n,))]


def _scatter_copies(ins, outs, sems):
    send_s, recv_s, fsend_s, frecv_s, local_s = sems
    x, y, c, me, chips = _place()
    local, sends, steps = [], [], []
    for i, (src, dst) in enumerate(zip(ins, outs)):
        local.append(pltpu.make_async_copy(src.at[me], dst.at[N_DEV - 1], local_s.at[i]))
        for r, (px, py) in enumerate(chips):
            k = 4 * i + r
            cp = _remote(src.at[2 * px + py], dst.at[r], send_s.at[k], recv_s.at[k], (px, py, c))
            fwd = _remote(dst.at[r], dst.at[4 + r], fsend_s.at[3 * i + r], frecv_s.at[3 * i + r], (x, y, 1 - c))
            sends.append(cp)
            steps.append((cp, fwd, fwd))
        k = 4 * i + 3
        cp = _remote(src.at[me], dst.at[3], send_s.at[k], recv_s.at[k], (x, y, 1 - c))
        sends.append(cp)
        steps.append((cp, None, None))
    return local, sends, steps


def _exchange_start(plan):
    local, sends, _ = plan
    for cp in local + sends:
        cp.start()


def _exchange_pass_on(plan):
    for arrival, pass_on, _ in plan[2]:
        arrival.wait_recv()
        if pass_on is not None:
            pass_on.start()


def _exchange_finish(plan):
    local, sends, steps = plan
    for _, pass_on, passed in steps:
        if pass_on is not None:
            passed.wait_recv()
    for cp in sends:
        cp.wait_send()
    for _, pass_on, _ in steps:
        if pass_on is not None:
            pass_on.wait_send()
    for cp in local:
        cp.wait()


def _exchange_call(arrays, copies, sems, out_shapes, name):
    n = len(arrays)

    def body(*refs):
        plan = copies(refs[:n], refs[n:2 * n], refs[2 * n:])
        _exchange_start(plan)
        _exchange_pass_on(plan)
        _exchange_finish(plan)

    return pl.pallas_call(
        body, name=name, in_specs=[HBM_REF] * n, out_specs=[HBM_REF] * n, out_shape=out_shapes,
        scratch_shapes=sems(n), compiler_params=pltpu.CompilerParams(has_side_effects=True),
    )(*arrays)


def _gather_call(shards, name):
    shapes = [jax.ShapeDtypeStruct((N_CHIPS,) + a.shape, a.dtype) for a in shards]
    return _exchange_call(shards, _gather_copies, _gather_sems, shapes, name)


def _all_reduce_small(a):
    def body(in_ref, out_ref, slots, send_sems, recv_sems):
        x, y, c = lax.axis_index("x"), lax.axis_index("y"), lax.axis_index("c")
        me = 4 * x + 2 * y + c
        slots[0] = in_ref[...]
        sends = []
        for r in range(1, N_DEV):
            peer = (x ^ (r >> 2), y ^ ((r >> 1) & 1), c ^ (r & 1))
            sends.append(pltpu.make_async_remote_copy(src_ref=in_ref, dst_ref=slots.at[r], send_sem=send_sems.at[r],
                                                      recv_sem=recv_sems.at[r], device_id=peer, device_id_type=MESH))
        for cp in sends:
            cp.start()
        for cp in sends:
            cp.wait_recv()
        acc = slots[me]
        for dev in range(1, N_DEV):
            acc = acc + slots[dev ^ me]
        out_ref[...] = acc
        for cp in sends:
            cp.wait_send()

    return pl.pallas_call(
        body, name="small_all_reduce",
        in_specs=[pl.BlockSpec(memory_space=pltpu.VMEM)], out_specs=pl.BlockSpec(memory_space=pltpu.VMEM),
        out_shape=jax.ShapeDtypeStruct(a.shape, a.dtype),
        scratch_shapes=[pltpu.VMEM((N_DEV,) + a.shape, a.dtype), pltpu.SemaphoreType.DMA((N_DEV,)),
                        pltpu.SemaphoreType.DMA((N_DEV,))],
        compiler_params=pltpu.CompilerParams(has_side_effects=True),
    )(a)


SHARDED = ["w_in", "mla_w_q_up", "mla_w_kv_up", "w_out", "ffn_w_up", "ffn_w_down", "ple_w_gate", "ple_w_proj",
           "gdn_conv_w", "ffn_conv_w"]
SHARD_AXIS = {"w_in": 1, "mla_w_q_up": 1, "mla_w_kv_up": 1, "w_out": 0, "ffn_w_up": 1, "ffn_w_down": 0,
              "ple_w_gate": 0, "ple_w_proj": 1, "gdn_conv_w": 1, "ffn_conv_w": 1}
SMALL = ["gdn_a_log", "gdn_dt_bias", "gdn_norm_g", "mla_q_norm_g", "mla_kv_norm_g", "ln1_g", "ln1_b", "ffn_conv_b",
         "ple_b_gate", "ln2_g", "ln2_b"]
WEIGHTS = ["w_in", "gdn_conv_w", "gdn_a_log", "gdn_dt_bias", "gdn_norm_g", "mla_q_norm_g", "mla_w_q_up", "mla_kv_norm_g",
           "mla_w_kv_up", "w_out", "ln1_g", "ln1_b", "ffn_w_up", "ffn_conv_w", "ffn_conv_b", "ffn_w_down", "ple_w_gate",
           "ple_b_gate", "ple_w_proj", "ln2_g", "ln2_b"]
F32_ON_WIRE = ("gdn_conv_w", "ffn_conv_w")
GATHER_EARLY = ["w_in", "gdn_conv_w"]
GATHER_LATE = ["mla_w_q_up", "mla_w_kv_up", "w_out", "ffn_w_up", "ffn_conv_w", "ffn_w_down", "ple_w_gate", "ple_w_proj"]
SCATTER_EARLY = ["ffn_w_up", "ffn_conv_w", "ffn_w_down", "ple_w_gate", "ple_w_proj", "w_out"]
SCATTER_LATE = ["w_in", "gdn_conv_w", "mla_w_q_up", "mla_w_kv_up"]
PACK_COLS = 1024
PACK_ROW_TILE = 8


def _join_blocks(blocks, axis):
    n, a, b = blocks.shape
    if axis == 0:
        return blocks.reshape(n * a, b)
    return jnp.transpose(blocks, (1, 0, 2)).reshape(a, n * b)


def _split_blocks(full, axis):
    if axis == 0:
        return full.reshape(N_CHIPS, full.shape[0] // N_CHIPS, full.shape[1])
    a, nb = full.shape
    return jnp.transpose(full.reshape(a, N_CHIPS, nb // N_CHIPS), (1, 0, 2))


def _pack(arrays):
    flat = jnp.concatenate([a.reshape(-1) for a in arrays])
    quantum = PACK_COLS * PACK_ROW_TILE
    padded = -(-flat.shape[0] // quantum) * quantum
    return jnp.pad(flat, (0, padded - flat.shape[0])).reshape(-1, PACK_COLS)


def _unpack(packed, shapes):
    flat = packed.reshape(-1)
    out, off = [], 0
    for shp in shapes:
        n = int(np.prod(shp))
        out.append(flat[off:off + n].reshape(shp))
        off += n
    return out


def kernel(x, p, w_in, gdn_conv_w, gdn_a_log, gdn_dt_bias, gdn_norm_g, mla_q_norm_g, mla_w_q_up, mla_kv_norm_g, mla_w_kv_up, w_out, ln1_g, ln1_b, ffn_w_up, ffn_conv_w, ffn_conv_b, ffn_w_down, ple_w_gate, ple_b_gate, ple_w_proj, ln2_g, ln2_b, loss_target, m_w_in, m_gdn_conv_w, m_gdn_a_log, m_gdn_dt_bias, m_gdn_norm_g, m_mla_q_norm_g, m_mla_w_q_up, m_mla_kv_norm_g, m_mla_w_kv_up, m_w_out, m_ln1_g, m_ln1_b, m_ffn_w_up, m_ffn_conv_w, m_ffn_conv_b, m_ffn_w_down, m_ple_w_gate, m_ple_b_gate, m_ple_w_proj, m_ln2_g, m_ln2_b, v_w_in, v_gdn_conv_w, v_gdn_a_log, v_gdn_dt_bias, v_gdn_norm_g, v_mla_q_norm_g, v_mla_w_q_up, v_mla_kv_norm_g, v_mla_w_kv_up, v_w_out, v_ln1_g, v_ln1_b, v_ffn_w_up, v_ffn_conv_w, v_ffn_conv_b, v_ffn_w_down, v_ple_w_gate, v_ple_b_gate, v_ple_w_proj, v_ln2_g, v_ln2_b):
    given = dict(locals())
    wsh = {n: given[n][0] for n in WEIGHTS}
    msh = {n: given["m_" + n][0] for n in WEIGHTS}
    vsh = {n: given["v_" + n][0] for n in WEIGHTS}
    bl, s, _ = x.shape
    t = bl * s
    xt = x.reshape(t, D_MODEL)
    pt = p.reshape(t, PLE_DIM)
    target = loss_target.reshape(t, D_MODEL)

    wire = lambda n: wsh[n] if n in F32_ON_WIRE else wsh[n].astype(BF16)
    early = _gather_call([wire(n) for n in GATHER_EARLY], "weights_gather_early")
    full = {n: _join_blocks(g, SHARD_AXIS[n]) for n, g in zip(GATHER_EARLY, early)}
    late_shards = [wire(n) for n in GATHER_LATE]
    late_ride = (late_shards, _gather_copies, _gather_sems,
                 [jax.ShapeDtypeStruct((N_CHIPS,) + a.shape, a.dtype) for a in late_shards], 0.75)

    in_cols, q_cols = _w_in_cols(), _w_q_cols()
    w_in_p = _pad_cols(full["w_in"], in_cols)
    gconv = full["gdn_conv_w"]
    row = lambda a: a.reshape(1, -1)
    sc = jnp.zeros((8, 128), F32).at[0, :GDN_HEADS].set(wsh["gdn_a_log"]).at[1, :GDN_HEADS].set(wsh["gdn_dt_bias"])
    norm_g, qg, kvg = row(wsh["gdn_norm_g"]), row(wsh["mla_q_norm_g"]), row(wsh["mla_kv_norm_g"])
    g1, b1, g2, b2 = row(wsh["ln1_g"]), row(wsh["ln1_b"]), row(wsh["ln2_g"]), row(wsh["ln2_b"])
    fbias, bgate = row(wsh["ffn_conv_b"]), row(wsh["ple_b_gate"])

    inv = ROPE_THETA ** (-jnp.arange(0, MLA_ROPE, 2, dtype=F32) / MLA_ROPE)
    ang = jnp.arange(s, dtype=F32)[:, None] * inv[None, :]
    zero = jnp.zeros_like(ang)
    cos_t = jnp.concatenate([jnp.cos(ang), zero, jnp.cos(ang), zero], axis=1)
    sin_t = jnp.concatenate([-jnp.sin(ang), zero, jnp.sin(ang), zero], axis=1)

    xb = _to_bf16(xt, 1024)
    proj = _matmul(xb, w_in_p, name="proj", tm=1024, tn=1024)
    cat, o_raw, states, qkvg, late = _gdn_fwd(proj, gconv, sc, norm_g, bl, s, late_ride)
    w_up = late[GATHER_LATE.index("ffn_w_up")]
    full.update({n: _join_blocks(g, SHARD_AXIS[n]) for n, g in zip(GATHER_LATE, late) if n != "ffn_w_up"})
    w_o, w_down = full["w_out"], full["ffn_w_down"]
    w_gate, w_proj, fconv = full["ple_w_gate"], full["ple_w_proj"], full["ffn_conv_w"]
    w_q_p, w_kv = _pad_cols(full["mla_w_q_up"], q_cols), full["mla_w_kv_up"]
    qf, kvf, kr = _mla_prep_fwd(proj, qg, kvg, w_q_p, w_kv, cos_t, sin_t, s, 512)
    cat, attn_o32, attn_lse = _attn_fwd(qf, kvf, kr, cat, bl, s, 512)
    wide = dict(tm=1024, tn=1024)
    r1, h1, h1b = _mix_ln1_fwd(xt, cat, w_o, g1, b1, 512)
    u = _matmul(h1b, w_up, name="ffn_up", tm=1024, tn=1408, b_parts=N_CHIPS)
    act = _ffn_act_fwd(u, fconv, fbias, bl, s, 256)
    dr2, dr2b, dgpre, dpp, loss_acc, dbgate, dg2, db2 = _head(h1, h1b, act, pt, w_down, w_gate, w_proj, bgate, g2, b2, target, 256)

    dact = _matmul(dr2b, w_down, name="d_act", tb=True, tm=1024, tn=1408)
    long_k = dict(ta=True, tk=2048)
    d_w_down = _matmul(act, dr2b, name="dw_down", tm=1408, tn=1024, **long_k)
    du, dfcw_g, dfcw_u, dfcb_g, dfcb_u = _ffn_act_bwd(u, fconv, fbias, dact, bl, s, 256)
    dh1_a = _matmul(du, w_up, name="dh1_ffn", tb=True, tk=1408, a_halves=True, b_parts=N_CHIPS, **wide)
    d_w_up = _matmul(h1b, du, name="dw_up", tn=1408, b_parts=2, out_parts=N_CHIPS, out_dtype=BF16, **long_k)
    d_w_gate = _matmul(h1b, dgpre, name="dw_gate", **long_k, **wide)
    d_w_proj = _matmul(pt, dpp, name="dw_proj", ta=True, tn=1024)
    dr1, dr1b, dcat, dg1, db1 = _ln1_bwd(r1, dr2, dh1_a, dgpre, w_gate, w_o, g1, b1, 512)
    d_w_o = _matmul(cat, dr1b, name="dw_out", **long_k, **wide)

    gfull = {
        "ffn_w_down": d_w_down, "ple_w_gate": d_w_gate, "ple_w_proj": d_w_proj, "w_out": d_w_o,
        "ffn_conv_w": jnp.concatenate([jnp.sum(dfcw_g, 0), jnp.sum(dfcw_u, 0)], axis=1),
    }
    slabs = {n: _split_blocks(g, SHARD_AXIS[n]).astype(BF16) for n, g in gfull.items()}
    slabs["ffn_w_up"] = d_w_up
    early_slabs = [slabs[n] for n in SCATTER_EARLY]
    early_ride = (early_slabs, _scatter_copies, _scatter_sems,
                  [jax.ShapeDtypeStruct((N_DEV,) + a.shape[1:], a.dtype) for a in early_slabs], 0.7)
    dproj, dab, dcwq, dcwk, dcwv, dsc, dng, early_recv = _gdn_bwd(proj, gconv, sc, norm_g, o_raw, states, qkvg, dcat, bl, s,
                                                                  early_ride)
    received = dict(zip(SCATTER_EARLY, early_recv))
    dqf, dkvf, dkr = _attn_bwd(qf, kvf, kr, dcat, attn_o32, attn_lse, bl, s, 256)
    dproj, dqg, dkvg, d_w_q_p, d_w_kv = _mla_prep_bwd(proj, qg, kvg, w_q_p, w_kv, cos_t, sin_t, dqf, dkvf, dkr, dab, dproj, s, 512)
    d_w_in_p = _matmul(xb, dproj, name="dw_in", **long_k, **wide)

    gfull.update({
        "w_in": _unpad_cols(d_w_in_p, in_cols, D_IN),
        "mla_w_q_up": _unpad_cols(d_w_q_p, q_cols, MLA_HEADS * (MLA_NOPE + MLA_ROPE)),
        "mla_w_kv_up": d_w_kv,
        "gdn_conv_w": jnp.concatenate([jnp.sum(dcwq, 0), jnp.sum(dcwk, 0), jnp.sum(dcwv, 0)], axis=1),
    })
    slabs.update({n: _split_blocks(gfull[n], SHARD_AXIS[n]).astype(BF16) for n in SCATTER_LATE})
    late_slabs = [slabs[n] for n in SCATTER_LATE]
    late_scatter = (late_slabs, _scatter_copies, _scatter_sems,
                    [jax.ShapeDtypeStruct((N_DEV,) + a.shape[1:], a.dtype) for a in late_slabs], 0.85)
    grad_x, late_recv = _matmul(dproj, w_in_p, name="d_x", tb=True, add=dr1, add_scale=ALPHA, ride=late_scatter, **wide)
    received.update(zip(SCATTER_LATE, late_recv))
    dsc_sum = jnp.sum(dsc, axis=(0, 1))
    gsmall = {
        "gdn_a_log": dsc_sum[0, :GDN_HEADS], "gdn_dt_bias": dsc_sum[1, :GDN_HEADS],
        "gdn_norm_g": jnp.sum(dng[:, :, 0, :], axis=(0, 1)),
        "mla_q_norm_g": dqg[0], "mla_kv_norm_g": dkvg[0], "ln1_g": dg1[0], "ln1_b": db1[0],
        "ffn_conv_b": jnp.concatenate([jnp.sum(dfcb_g, 0), jnp.sum(dfcb_u, 0)], axis=1)[0],
        "ple_b_gate": dbgate[0], "ln2_g": dg2[0], "ln2_b": db2[0],
    }

    big = [{}, {}, {}, {}]
    for n in SHARDED:
        for kind, val in enumerate(_adamw_reduced(received[n], wsh[n], msh[n], vsh[n], "adamw_" + n)):
            big[kind][n] = val

    small_shapes = [wsh[n].shape for n in SMALL]
    gsum = _all_reduce_small(_pack([gsmall[n] for n in SMALL]))
    spacks = _adamw_small(gsum, _pack([wsh[n] for n in SMALL]), _pack([msh[n] for n in SMALL]), _pack([vsh[n] for n in SMALL]))
    small = [dict(zip(SMALL, _unpack(pk, small_shapes))) for pk in spacks]

    loss = lax.psum(loss_acc[0, 0], ("x", "y", "c"))
    outs = [loss, grad_x.reshape(x.shape)]
    for kind in range(4):
        for n in WEIGHTS:
            val = big[kind][n] if n in big[kind] else small[kind][n]
            outs.append(val[None])
    return tuple(outs)
```
